```python
import math
import jax, jax.numpy as jnp
from jax import lax
import numpy as np

D_MODEL = 1024
BATCH = 8
SEQ = 2048
DEPTH = 4

CHUNK = 64
MEM_LEN = 256
Q_BLOCK = 128

FOX_HEADS = 8
FOX_HEAD_DIM = 64
FOX_WIDTH = FOX_HEADS * FOX_HEAD_DIM
FORGET_BIAS = 2.0
SSM_GROUP = 16
SSM_WIDTH = 512
SSM_GROUPS = SSM_WIDTH // SSM_GROUP
SSM_STATE = 64
DT_MIN = 1e-3
DT_MAX = 1e-1
CA_HEADS = 8
CA_HEAD_DIM = 64
CA_WIDTH = CA_HEADS * CA_HEAD_DIM
CA_LEFT_CHUNKS = 8
CA_BAND = (CA_LEFT_CHUNKS + 1) * CHUNK
REL_MIN = -(CHUNK - 1)
REL_MAX = 4 * CHUNK
N_REL = REL_MAX - REL_MIN + 1
N_BRANCH = 3
XA_HEADS = 4
XA_HEAD_DIM = D_MODEL // XA_HEADS
D_FF = 2816
CONV_WIDTH = 3
DN_ALPHA = (2 * DEPTH) ** 0.25
DN_BETA = (8 * DEPTH) ** -0.25
LN_EPS = 1e-5
NEG_INF = -1e30

_SPLITS = (FOX_WIDTH, FOX_WIDTH, FOX_WIDTH, FOX_HEADS, SSM_WIDTH,
           CA_WIDTH, CA_WIDTH, CA_WIDTH, D_MODEL, D_MODEL, D_MODEL)
N_IN = sum(_SPLITS)

kernel_name = "hybrid_fox_s5_chunkattn_deepnorm"


def _layer_norm(x, g, b):
    xf = x.astype(jnp.float32)
    mu = jnp.mean(xf, axis=-1, keepdims=True)
    var = jnp.mean(jnp.square(xf - mu), axis=-1, keepdims=True)
    y = (xf - mu) * lax.rsqrt(var + LN_EPS) * g.astype(jnp.float32) + b.astype(jnp.float32)
    return y.astype(x.dtype)


def _forgetting_attention(q, k, v, f_logit):
    S = q.shape[1]
    cum = jnp.cumsum(jax.nn.log_sigmoid(f_logit.astype(jnp.float32)), axis=1)
    cum = jnp.transpose(cum, (0, 2, 1))
    scale = FOX_HEAD_DIM ** -0.5
    outs = []
    for blk in range(S // Q_BLOCK):
        q0, q1 = blk * Q_BLOCK, (blk + 1) * Q_BLOCK
        s = jnp.einsum('bqhd,bkhd->bhqk', q[:, q0:q1], k[:, :q1]).astype(jnp.float32) * scale
        s = s + cum[:, :, q0:q1, None] - cum[:, :, None, :q1]
        causal = jnp.arange(q0, q1)[:, None] >= jnp.arange(q1)[None, :]
        p = jax.nn.softmax(jnp.where(causal, s, NEG_INF), axis=-1)
        outs.append(jnp.einsum('bhqk,bkhd->bqhd', p.astype(v.dtype), v[:, :q1]))
    return jnp.concatenate(outs, axis=1)


def _complex_linear_combine(e1, e2):
    a1r, a1i, b1r, b1i = e1
    a2r, a2i, b2r, b2i = e2
    ar = a2r * a1r - a2i * a1i
    ai = a2r * a1i + a2i * a1r
    br = a2r * b1r - a2i * b1i + b2r
    bi = a2r * b1i + a2i * b1r + b2i
    return ar, ai, br, bi


def _s5(u, lam_re, lam_im, log_dt, b_re, b_im, c_re, c_im, d):
    f32 = jnp.float32
    Bsz, S, _ = u.shape
    uf = u.astype(f32)
    ug = uf.reshape(Bsz, S, SSM_GROUPS, SSM_GROUP)
    lr = jnp.minimum(lam_re.astype(f32), -1e-4)
    li = lam_im.astype(f32)
    dt = jnp.exp(log_dt.astype(f32))[:, None]
    mag = jnp.exp(lr * dt)
    ar = mag * jnp.cos(li * dt)
    ai = mag * jnp.sin(li * dt)
    den = lr * lr + li * li
    gr = ((ar - 1.0) * lr + ai * li) / den
    gi = (ai * lr - (ar - 1.0) * li) / den
    br, bi = b_re.astype(f32), b_im.astype(f32)
    bbr = gr[..., None] * br - gi[..., None] * bi
    bbi = gr[..., None] * bi + gi[..., None] * br
    xr = jnp.einsum('bsgc,gpc->bsgp', ug, bbr)
    xi = jnp.einsum('bsgc,gpc->bsgp', ug, bbi)
    a_r = jnp.broadcast_to(ar, xr.shape)
    a_i = jnp.broadcast_to(ai, xr.shape)
    _, _, hr, hi = lax.associative_scan(_complex_linear_combine, (a_r, a_i, xr, xi), axis=1)
    y = (jnp.einsum('bsgp,gcp->bsgc', hr, c_re.astype(f32))
         - jnp.einsum('bsgp,gcp->bsgc', hi, c_im.astype(f32)))
    y = y.reshape(Bsz, S, SSM_WIDTH) + d.astype(f32) * uf
    return y.astype(u.dtype)


def _chunk_attention(q, k, v, rel_bias):
    Bsz, S, H, Dh = q.shape
    NC = S // CHUNK
    qc = q.reshape(Bsz, NC, CHUNK, H, Dh)

    def band(t):
        tc = t.reshape(Bsz, NC, CHUNK, H, Dh)
        tp = jnp.pad(tc, ((0, 0), (CA_LEFT_CHUNKS, 0), (0, 0), (0, 0), (0, 0)))
        return jnp.concatenate([tp[:, j:j + NC] for j in range(CA_LEFT_CHUNKS + 1)], axis=2)

    kb, vb = band(k), band(v)
    s = jnp.einsum('bnqhd,bnkhd->bnhqk', qc, kb).astype(jnp.float32) * (CA_HEAD_DIM ** -0.5)
    qi = jnp.arange(CHUNK)
    kk = jnp.arange(CA_BAND)
    rel = CA_LEFT_CHUNKS * CHUNK + qi[:, None] - kk[None, :]
    idx = jnp.clip(rel, REL_MIN, REL_MAX) - REL_MIN
    bias = rel_bias.astype(jnp.float32)[:, idx]
    valid = (jnp.arange(NC)[:, None] + (kk // CHUNK)[None, :] - CA_LEFT_CHUNKS) >= 0
    s = jnp.where(valid[None, :, None, None, :], s + bias[None, None], NEG_INF)
    p = jax.nn.softmax(s, axis=-1)
    o = jnp.einsum('bnhqk,bnkhd->bnqhd', p.astype(v.dtype), vb)
    return o.reshape(Bsz, S, H * Dh)


def _hybrid_mixer(x, w_in, b_in, lam_re, lam_im, log_dt, b_re, b_im, c_re, c_im, ssm_d,
                  rel_bias, w_fox_o, w_ssm_glu, w_ca_o, w_o):
    Bsz, S, _ = x.shape
    z = x @ w_in + b_in
    fq, fk, fv, ff, su, cq, ck, cv, g_fox, g_ssm, g_ca = jnp.split(z, np.cumsum(_SPLITS)[:-1], axis=-1)
    hs = lambda t, h, dh: t.reshape(Bsz, S, h, dh)
    ya = _forgetting_attention(hs(fq, FOX_HEADS, FOX_HEAD_DIM), hs(fk, FOX_HEADS, FOX_HEAD_DIM),
                               hs(fv, FOX_HEADS, FOX_HEAD_DIM), ff)
    ya = ya.reshape(Bsz, S, FOX_WIDTH) @ w_fox_o
    yb = jax.nn.gelu(_s5(su, lam_re, lam_im, log_dt, b_re, b_im, c_re, c_im, ssm_d)) @ w_ssm_glu
    yb_val, yb_gate = jnp.split(yb, 2, axis=-1)
    yb = yb_val * jax.nn.sigmoid(yb_gate)
    yc = _chunk_attention(hs(cq, CA_HEADS, CA_HEAD_DIM), hs(ck, CA_HEADS, CA_HEAD_DIM),
                          hs(cv, CA_HEADS, CA_HEAD_DIM), rel_bias) @ w_ca_o
    merged = (jax.nn.sigmoid(g_fox) * ya + jax.nn.sigmoid(g_ssm) * yb + jax.nn.sigmoid(g_ca) * yc)
    return merged @ w_o


def _memory_cross_attention(x, mem, wq, wkv, wo):
    Bsz, S, _ = x.shape
    M = mem.shape[1]
    q = (x @ wq).reshape(Bsz, S, XA_HEADS, XA_HEAD_DIM)
    k, v = jnp.split(mem @ wkv, 2, axis=-1)
    k = k.reshape(Bsz, M, XA_HEADS, XA_HEAD_DIM)
    v = v.reshape(Bsz, M, XA_HEADS, XA_HEAD_DIM)
    s = jnp.einsum('bqhd,bkhd->bhqk', q, k).astype(jnp.float32) * (XA_HEAD_DIM ** -0.5)
    p = jax.nn.softmax(s, axis=-1)
    o = jnp.einsum('bhqk,bkhd->bqhd', p.astype(v.dtype), v)
    return o.reshape(Bsz, S, D_MODEL) @ wo


def _conv_ffn(x, w_up, conv_w, conv_b, w_down):
    a, g = jnp.split(x @ w_up, 2, axis=-1)
    a = lax.conv_general_dilated(a, conv_w[:, None, :], window_strides=(1,),
                                 padding=[(CONV_WIDTH - 1, 0)],
                                 dimension_numbers=('NWC', 'WIO', 'NWC'),
                                 feature_group_count=D_FF) + conv_b
    return (jax.nn.gelu(a) * g) @ w_down


def _fwd_setup_inputs(seed: int = 0) -> dict:
    key = jax.random.key(seed)
    ks = jax.random.split(key, 26)
    f32 = jnp.float32
    L, D = DEPTH, D_MODEL
    nrm = lambda k, shape, s: jax.random.normal(k, shape, f32) * s
    sd = D ** -0.5
    col_scale = jnp.concatenate([
        jnp.full((FOX_WIDTH,), sd, f32), jnp.full((FOX_WIDTH,), sd, f32),
        jnp.full((FOX_WIDTH,), sd * DN_BETA, f32), jnp.full((FOX_HEADS,), sd, f32),
        jnp.full((SSM_WIDTH,), sd, f32),
        jnp.full((CA_WIDTH,), sd, f32), jnp.full((CA_WIDTH,), sd, f32),
        jnp.full((CA_WIDTH,), sd * DN_BETA, f32),
        jnp.full((N_BRANCH * D,), sd, f32)])
    col_offset = jnp.concatenate([
        jnp.zeros((3 * FOX_WIDTH,), f32), jnp.full((FOX_HEADS,), FORGET_BIAS, f32),
        jnp.zeros((N_IN - 3 * FOX_WIDTH - FOX_HEADS,), f32)])
    xa_kv_scale = jnp.concatenate([jnp.full((D,), sd, f32), jnp.full((D,), sd * DN_BETA, f32)])
    lam_im0 = math.pi * jnp.arange(SSM_STATE, dtype=f32)
    return {
        "x": nrm(ks[0], (BATCH, SEQ, D), 1.0),
        "mem": nrm(ks[1], (BATCH, MEM_LEN, D), 1.0),
        "w_in": jax.random.normal(ks[2], (L, D, N_IN), f32) * col_scale,
        "b_in": nrm(ks[3], (L, N_IN), 0.02) + col_offset,
        "ssm_lambda_re": -0.5 + nrm(ks[4], (L, SSM_GROUPS, SSM_STATE), 1e-3),
        "ssm_lambda_im": lam_im0 + nrm(ks[5], (L, SSM_GROUPS, SSM_STATE), 1e-3),
        "ssm_log_dt": jax.random.uniform(ks[6], (L, SSM_GROUPS), f32, math.log(DT_MIN), math.log(DT_MAX)),
        "ssm_b_re": nrm(ks[7], (L, SSM_GROUPS, SSM_STATE, SSM_GROUP), (2 * SSM_GROUP) ** -0.5),
        "ssm_b_im": nrm(ks[8], (L, SSM_GROUPS, SSM_STATE, SSM_GROUP), (2 * SSM_GROUP) ** -0.5),
        "ssm_c_re": nrm(ks[9], (L, SSM_GROUPS, SSM_GROUP, SSM_STATE), SSM_STATE ** -0.5),
        "ssm_c_im": nrm(ks[10], (L, SSM_GROUPS, SSM_GROUP, SSM_STATE), SSM_STATE ** -0.5),
        "ssm_d": nrm(ks[11], (L, SSM_WIDTH), 1.0),
        "ca_rel_bias": nrm(ks[12], (L, CA_HEADS, N_REL), 0.1),
        "w_fox_o": nrm(ks[13], (L, FOX_WIDTH, D), DN_BETA * FOX_WIDTH ** -0.5),
        "w_ssm_glu": nrm(ks[14], (L, SSM_WIDTH, 2 * D), DN_BETA * SSM_WIDTH ** -0.5),
        "w_ca_o": nrm(ks[15], (L, CA_WIDTH, D), DN_BETA * CA_WIDTH ** -0.5),
        "w_o": nrm(ks[16], (L, D, D), DN_BETA * sd),
        "xa_wq": nrm(ks[17], (L, D, D), sd),
        "xa_wkv": jax.random.normal(ks[18], (L, D, 2 * D), f32) * xa_kv_scale,
        "xa_wo": nrm(ks[19], (L, D, D), DN_BETA * sd),
        "ffn_w_up": nrm(ks[20], (L, D, 2 * D_FF), DN_BETA * sd),
        "ffn_conv_w": nrm(ks[21], (L, CONV_WIDTH, D_FF), CONV_WIDTH ** -0.5),
        "ffn_conv_b": nrm(ks[22], (L, D_FF), 0.02),
        "ffn_w_down": nrm(ks[23], (L, D_FF, D), DN_BETA * D_FF ** -0.5),
        "ln_g": 1.0 + nrm(ks[24], (L, 3, D), 0.02),
        "ln_b": nrm(ks[25], (L, 3, D), 0.02),
    }


def _fwd_reference(x, mem, w_in, b_in, ssm_lambda_re, ssm_lambda_im, ssm_log_dt, ssm_b_re, ssm_b_im,
              ssm_c_re, ssm_c_im, ssm_d, ca_rel_bias, w_fox_o, w_ssm_glu, w_ca_o, w_o,
              xa_wq, xa_wkv, xa_wo, ffn_w_up, ffn_conv_w, ffn_conv_b, ffn_w_down, ln_g, ln_b):
    for l in range(DEPTH):
        h = _hybrid_mixer(x, w_in[l], b_in[l], ssm_lambda_re[l], ssm_lambda_im[l], ssm_log_dt[l],
                          ssm_b_re[l], ssm_b_im[l], ssm_c_re[l], ssm_c_im[l], ssm_d[l],
                          ca_rel_bias[l], w_fox_o[l], w_ssm_glu[l], w_ca_o[l], w_o[l])
        x = _layer_norm(DN_ALPHA * x + h, ln_g[l, 0], ln_b[l, 0])
        h = _memory_cross_attention(x, mem, xa_wq[l], xa_wkv[l], xa_wo[l])
        x = _layer_norm(DN_ALPHA * x + h, ln_g[l, 1], ln_b[l, 1])
        h = _conv_ffn(x, ffn_w_up[l], ffn_conv_w[l], ffn_conv_b[l], ffn_w_down[l])
        x = _layer_norm(DN_ALPHA * x + h, ln_g[l, 2], ln_b[l, 2])
    return x


import jax as _jax
import jax.numpy as _jnp

TWIN_FORMAT = 'train_step'
FWD_PARAMS = ['x', 'mem', 'w_in', 'b_in', 'ssm_lambda_re', 'ssm_lambda_im', 'ssm_log_dt', 'ssm_b_re', 'ssm_b_im', 'ssm_c_re', 'ssm_c_im', 'ssm_d', 'ca_rel_bias', 'w_fox_o', 'w_ssm_glu', 'w_ca_o', 'w_o', 'xa_wq', 'xa_wkv', 'xa_wo', 'ffn_w_up', 'ffn_conv_w', 'ffn_conv_b', 'ffn_w_down', 'ln_g', 'ln_b']
TWIN_WEIGHTS = ['w_in', 'b_in', 'ssm_lambda_re', 'ssm_lambda_im', 'ssm_log_dt', 'ssm_b_re', 'ssm_b_im', 'ssm_c_re', 'ssm_c_im', 'ssm_d', 'ca_rel_bias', 'w_fox_o', 'w_ssm_glu', 'w_ca_o', 'w_o', 'xa_wq', 'xa_wkv', 'xa_wo', 'ffn_w_up', 'ffn_conv_w', 'ffn_conv_b', 'ffn_w_down', 'ln_g', 'ln_b']
TWIN_DIFF_INPUT = 'x'
TWIN_INPUTS = ['x', 'mem', 'w_in', 'b_in', 'ssm_lambda_re', 'ssm_lambda_im', 'ssm_log_dt', 'ssm_b_re', 'ssm_b_im', 'ssm_c_re', 'ssm_c_im', 'ssm_d', 'ca_rel_bias', 'w_fox_o', 'w_ssm_glu', 'w_ca_o', 'w_o', 'xa_wq', 'xa_wkv', 'xa_wo', 'ffn_w_up', 'ffn_conv_w', 'ffn_conv_b', 'ffn_w_down', 'ln_g', 'ln_b', 'loss_target', 'm_w_in', 'm_b_in', 'm_ssm_lambda_re', 'm_ssm_lambda_im', 'm_ssm_log_dt', 'm_ssm_b_re', 'm_ssm_b_im', 'm_ssm_c_re', 'm_ssm_c_im', 'm_ssm_d', 'm_ca_rel_bias', 'm_w_fox_o', 'm_w_ssm_glu', 'm_w_ca_o', 'm_w_o', 'm_xa_wq', 'm_xa_wkv', 'm_xa_wo', 'm_ffn_w_up', 'm_ffn_conv_w', 'm_ffn_conv_b', 'm_ffn_w_down', 'm_ln_g', 'm_ln_b', 'v_w_in', 'v_b_in', 'v_ssm_lambda_re', 'v_ssm_lambda_im', 'v_ssm_log_dt', 'v_ssm_b_re', 'v_ssm_b_im', 'v_ssm_c_re', 'v_ssm_c_im', 'v_ssm_d', 'v_ca_rel_bias', 'v_w_fox_o', 'v_w_ssm_glu', 'v_w_ca_o', 'v_w_o', 'v_xa_wq', 'v_xa_wkv', 'v_xa_wo', 'v_ffn_w_up', 'v_ffn_conv_w', 'v_ffn_conv_b', 'v_ffn_w_down', 'v_ln_g', 'v_ln_b']
TWIN_OUTPUTS = ['loss', 'grad_x', 'grad_w_in', 'grad_b_in', 'grad_ssm_lambda_re', 'grad_ssm_lambda_im', 'grad_ssm_log_dt', 'grad_ssm_b_re', 'grad_ssm_b_im', 'grad_ssm_c_re', 'grad_ssm_c_im', 'grad_ssm_d', 'grad_ca_rel_bias', 'grad_w_fox_o', 'grad_w_ssm_glu', 'grad_w_ca_o', 'grad_w_o', 'grad_xa_wq', 'grad_xa_wkv', 'grad_xa_wo', 'grad_ffn_w_up', 'grad_ffn_conv_w', 'grad_ffn_conv_b', 'grad_ffn_w_down', 'grad_ln_g', 'grad_ln_b', 'delta_w_in', 'delta_b_in', 'delta_ssm_lambda_re', 'delta_ssm_lambda_im', 'delta_ssm_log_dt', 'delta_ssm_b_re', 'delta_ssm_b_im', 'delta_ssm_c_re', 'delta_ssm_c_im', 'delta_ssm_d', 'delta_ca_rel_bias', 'delta_w_fox_o', 'delta_w_ssm_glu', 'delta_w_ca_o', 'delta_w_o', 'delta_xa_wq', 'delta_xa_wkv', 'delta_xa_wo', 'delta_ffn_w_up', 'delta_ffn_conv_w', 'delta_ffn_conv_b', 'delta_ffn_w_down', 'delta_ln_g', 'delta_ln_b', 'new_m_w_in', 'new_m_b_in', 'new_m_ssm_lambda_re', 'new_m_ssm_lambda_im', 'new_m_ssm_log_dt', 'new_m_ssm_b_re', 'new_m_ssm_b_im', 'new_m_ssm_c_re', 'new_m_ssm_c_im', 'new_m_ssm_d', 'new_m_ca_rel_bias', 'new_m_w_fox_o', 'new_m_w_ssm_glu', 'new_m_w_ca_o', 'new_m_w_o', 'new_m_xa_wq', 'new_m_xa_wkv', 'new_m_xa_wo', 'new_m_ffn_w_up', 'new_m_ffn_conv_w', 'new_m_ffn_conv_b', 'new_m_ffn_w_down', 'new_m_ln_g', 'new_m_ln_b', 'new_v_w_in', 'new_v_b_in', 'new_v_ssm_lambda_re', 'new_v_ssm_lambda_im', 'new_v_ssm_log_dt', 'new_v_ssm_b_re', 'new_v_ssm_b_im', 'new_v_ssm_c_re', 'new_v_ssm_c_im', 'new_v_ssm_d', 'new_v_ca_rel_bias', 'new_v_w_fox_o', 'new_v_w_ssm_glu', 'new_v_w_ca_o', 'new_v_w_o', 'new_v_xa_wq', 'new_v_xa_wkv', 'new_v_xa_wo', 'new_v_ffn_w_up', 'new_v_ffn_conv_w', 'new_v_ffn_conv_b', 'new_v_ffn_w_down', 'new_v_ln_g', 'new_v_ln_b']
TWIN_LEAF_KINDS = {'loss': 'loss', 'grad_x': 'grad_x', 'grad_w_in': 'grad_w', 'grad_b_in': 'grad_w', 'grad_ssm_lambda_re': 'grad_w', 'grad_ssm_lambda_im': 'grad_w', 'grad_ssm_log_dt': 'grad_w', 'grad_ssm_b_re': 'grad_w', 'grad_ssm_b_im': 'grad_w', 'grad_ssm_c_re': 'grad_w', 'grad_ssm_c_im': 'grad_w', 'grad_ssm_d': 'grad_w', 'grad_ca_rel_bias': 'grad_w', 'grad_w_fox_o': 'grad_w', 'grad_w_ssm_glu': 'grad_w', 'grad_w_ca_o': 'grad_w', 'grad_w_o': 'grad_w', 'grad_xa_wq': 'grad_w', 'grad_xa_wkv': 'grad_w', 'grad_xa_wo': 'grad_w', 'grad_ffn_w_up': 'grad_w', 'grad_ffn_conv_w': 'grad_w', 'grad_ffn_conv_b': 'grad_w', 'grad_ffn_w_down': 'grad_w', 'grad_ln_g': 'grad_w', 'grad_ln_b': 'grad_w', 'delta_w_in': 'delta_w', 'delta_b_in': 'delta_w', 'delta_ssm_lambda_re': 'delta_w', 'delta_ssm_lambda_im': 'delta_w', 'delta_ssm_log_dt': 'delta_w', 'delta_ssm_b_re': 'delta_w', 'delta_ssm_b_im': 'delta_w', 'delta_ssm_c_re': 'delta_w', 'delta_ssm_c_im': 'delta_w', 'delta_ssm_d': 'delta_w', 'delta_ca_rel_bias': 'delta_w', 'delta_w_fox_o': 'delta_w', 'delta_w_ssm_glu': 'delta_w', 'delta_w_ca_o': 'delta_w', 'delta_w_o': 'delta_w', 'delta_xa_wq': 'delta_w', 'delta_xa_wkv': 'delta_w', 'delta_xa_wo': 'delta_w', 'delta_ffn_w_up': 'delta_w', 'delta_ffn_conv_w': 'delta_w', 'delta_ffn_conv_b': 'delta_w', 'delta_ffn_w_down': 'delta_w', 'delta_ln_g': 'delta_w', 'delta_ln_b': 'delta_w', 'new_m_w_in': 'new_m', 'new_m_b_in': 'new_m', 'new_m_ssm_lambda_re': 'new_m', 'new_m_ssm_lambda_im': 'new_m', 'new_m_ssm_log_dt': 'new_m', 'new_m_ssm_b_re': 'new_m', 'new_m_ssm_b_im': 'new_m', 'new_m_ssm_c_re': 'new_m', 'new_m_ssm_c_im': 'new_m', 'new_m_ssm_d': 'new_m', 'new_m_ca_rel_bias': 'new_m', 'new_m_w_fox_o': 'new_m', 'new_m_w_ssm_glu': 'new_m', 'new_m_w_ca_o': 'new_m', 'new_m_w_o': 'new_m', 'new_m_xa_wq': 'new_m', 'new_m_xa_wkv': 'new_m', 'new_m_xa_wo': 'new_m', 'new_m_ffn_w_up': 'new_m', 'new_m_ffn_conv_w': 'new_m', 'new_m_ffn_conv_b': 'new_m', 'new_m_ffn_w_down': 'new_m', 'new_m_ln_g': 'new_m', 'new_m_ln_b': 'new_m', 'new_v_w_in': 'new_v', 'new_v_b_in': 'new_v', 'new_v_ssm_lambda_re': 'new_v', 'new_v_ssm_lambda_im': 'new_v', 'new_v_ssm_log_dt': 'new_v', 'new_v_ssm_b_re': 'new_v', 'new_v_ssm_b_im': 'new_v', 'new_v_ssm_c_re': 'new_v', 'new_v_ssm_c_im': 'new_v', 'new_v_ssm_d': 'new_v', 'new_v_ca_rel_bias': 'new_v', 'new_v_w_fox_o': 'new_v', 'new_v_w_ssm_glu': 'new_v', 'new_v_w_ca_o': 'new_v', 'new_v_w_o': 'new_v', 'new_v_xa_wq': 'new_v', 'new_v_xa_wkv': 'new_v', 'new_v_xa_wo': 'new_v', 'new_v_ffn_w_up': 'new_v', 'new_v_ffn_conv_w': 'new_v', 'new_v_ffn_conv_b': 'new_v', 'new_v_ffn_w_down': 'new_v', 'new_v_ln_g': 'new_v', 'new_v_ln_b': 'new_v'}


def _forward(args):
    return _fwd_reference(*[args[k] for k in FWD_PARAMS])


def _output_shape():
    out = _jax.eval_shape(lambda: _forward(_fwd_setup_inputs(0)))
    return out.shape, out.dtype

N_MICROBATCH = 1
ADAM_LR = 0.001
ADAM_B1 = 0.9
ADAM_B2 = 0.999
ADAM_EPS = 1e-08
ADAM_WD = 0.01
ADAM_STEP = 10
PER_EXAMPLE_BATCH_AXIS = {'x': 0, 'mem': 0, 'loss_target': 0}
SHARED_INPUTS = []
_WEIGHT_DTYPES = {'w_in': _jnp.float32, 'b_in': _jnp.float32, 'ssm_lambda_re': _jnp.float32, 'ssm_lambda_im': _jnp.float32, 'ssm_log_dt': _jnp.float32, 'ssm_b_re': _jnp.float32, 'ssm_b_im': _jnp.float32, 'ssm_c_re': _jnp.float32, 'ssm_c_im': _jnp.float32, 'ssm_d': _jnp.float32, 'ca_rel_bias': _jnp.float32, 'w_fox_o': _jnp.float32, 'w_ssm_glu': _jnp.float32, 'w_ca_o': _jnp.float32, 'w_o': _jnp.float32, 'xa_wq': _jnp.float32, 'xa_wkv': _jnp.float32, 'xa_wo': _jnp.float32, 'ffn_w_up': _jnp.float32, 'ffn_conv_w': _jnp.float32, 'ffn_conv_b': _jnp.float32, 'ffn_w_down': _jnp.float32, 'ln_g': _jnp.float32, 'ln_b': _jnp.float32}
MOMENT_SCALE = {'w_in': 1.733471e-03, 'b_in': 8.073450e-03, 'ssm_lambda_re': 2.692924e-04, 'ssm_lambda_im': 2.549939e-04, 'ssm_log_dt': 1.501013e-01, 'ssm_b_re': 1.665945e-04, 'ssm_b_im': 1.702884e-04, 'ssm_c_re': 2.341344e-04, 'ssm_c_im': 2.347814e-04, 'ssm_d': 4.537856e-03, 'ca_rel_bias': 2.601856e-04, 'w_fox_o': 3.022960e-03, 'w_ssm_glu': 5.013583e-03, 'w_ca_o': 1.202194e-03, 'w_o': 7.514632e-03, 'xa_wq': 1.380290e-03, 'xa_wkv': 2.755171e-03, 'xa_wo': 3.648770e-03, 'ffn_w_up': 4.681825e-03, 'ffn_conv_w': 2.003321e-03, 'ffn_conv_b': 4.576042e-03, 'ffn_w_down': 7.659605e-03, 'ln_g': 4.693675e+00, 'ln_b': 3.856798e-01}


def _to_microbatches(a, axis):
    t = _jnp.moveaxis(a, axis, 0)
    t = t.reshape((N_MICROBATCH, t.shape[0] // N_MICROBATCH) + t.shape[1:])
    return _jnp.moveaxis(t, 1, axis + 1)


def setup_inputs(seed: int = 0) -> dict:
    inp = _fwd_setup_inputs(seed)
    key = _jax.random.fold_in(_jax.random.key(seed), 7919)
    shape, _ = _output_shape()
    out = dict(inp)
    out["loss_target"] = _jax.random.normal(_jax.random.fold_in(key, 0), shape, _jnp.float32)
    for i, name in enumerate(TWIN_WEIGHTS):
        w = inp[name].astype(_jnp.float32)
        if MOMENT_SCALE is None:
            s = _jnp.sqrt(_jnp.mean(_jnp.square(w)) + 1e-30)
        else:
            s = MOMENT_SCALE[name]
        km, kv = _jax.random.split(_jax.random.fold_in(key, i + 1))
        out[name] = w
        out["m_" + name] = s * _jax.random.normal(km, w.shape, _jnp.float32)
        out["v_" + name] = (s * s) * _jax.random.uniform(kv, w.shape, _jnp.float32, 0.5, 1.5)
    if N_MICROBATCH > 1:
        for name, axis in PER_EXAMPLE_BATCH_AXIS.items():
            out[name] = _to_microbatches(out[name], axis)
    return {'x': out['x'], 'mem': out['mem'], 'w_in': out['w_in'], 'b_in': out['b_in'], 'ssm_lambda_re': out['ssm_lambda_re'], 'ssm_lambda_im': out['ssm_lambda_im'], 'ssm_log_dt': out['ssm_log_dt'], 'ssm_b_re': out['ssm_b_re'], 'ssm_b_im': out['ssm_b_im'], 'ssm_c_re': out['ssm_c_re'], 'ssm_c_im': out['ssm_c_im'], 'ssm_d': out['ssm_d'], 'ca_rel_bias': out['ca_rel_bias'], 'w_fox_o': out['w_fox_o'], 'w_ssm_glu': out['w_ssm_glu'], 'w_ca_o': out['w_ca_o'], 'w_o': out['w_o'], 'xa_wq': out['xa_wq'], 'xa_wkv': out['xa_wkv'], 'xa_wo': out['xa_wo'], 'ffn_w_up': out['ffn_w_up'], 'ffn_conv_w': out['ffn_conv_w'], 'ffn_conv_b': out['ffn_conv_b'], 'ffn_w_down': out['ffn_w_down'], 'ln_g': out['ln_g'], 'ln_b': out['ln_b'], 'loss_target': out['loss_target'], 'm_w_in': out['m_w_in'], 'm_b_in': out['m_b_in'], 'm_ssm_lambda_re': out['m_ssm_lambda_re'], 'm_ssm_lambda_im': out['m_ssm_lambda_im'], 'm_ssm_log_dt': out['m_ssm_log_dt'], 'm_ssm_b_re': out['m_ssm_b_re'], 'm_ssm_b_im': out['m_ssm_b_im'], 'm_ssm_c_re': out['m_ssm_c_re'], 'm_ssm_c_im': out['m_ssm_c_im'], 'm_ssm_d': out['m_ssm_d'], 'm_ca_rel_bias': out['m_ca_rel_bias'], 'm_w_fox_o': out['m_w_fox_o'], 'm_w_ssm_glu': out['m_w_ssm_glu'], 'm_w_ca_o': out['m_w_ca_o'], 'm_w_o': out['m_w_o'], 'm_xa_wq': out['m_xa_wq'], 'm_xa_wkv': out['m_xa_wkv'], 'm_xa_wo': out['m_xa_wo'], 'm_ffn_w_up': out['m_ffn_w_up'], 'm_ffn_conv_w': out['m_ffn_conv_w'], 'm_ffn_conv_b': out['m_ffn_conv_b'], 'm_ffn_w_down': out['m_ffn_w_down'], 'm_ln_g': out['m_ln_g'], 'm_ln_b': out['m_ln_b'], 'v_w_in': out['v_w_in'], 'v_b_in': out['v_b_in'], 'v_ssm_lambda_re': out['v_ssm_lambda_re'], 'v_ssm_lambda_im': out['v_ssm_lambda_im'], 'v_ssm_log_dt': out['v_ssm_log_dt'], 'v_ssm_b_re': out['v_ssm_b_re'], 'v_ssm_b_im': out['v_ssm_b_im'], 'v_ssm_c_re': out['v_ssm_c_re'], 'v_ssm_c_im': out['v_ssm_c_im'], 'v_ssm_d': out['v_ssm_d'], 'v_ca_rel_bias': out['v_ca_rel_bias'], 'v_w_fox_o': out['v_w_fox_o'], 'v_w_ssm_glu': out['v_w_ssm_glu'], 'v_w_ca_o': out['v_w_ca_o'], 'v_w_o': out['v_w_o'], 'v_xa_wq': out['v_xa_wq'], 'v_xa_wkv': out['v_xa_wkv'], 'v_xa_wo': out['v_xa_wo'], 'v_ffn_w_up': out['v_ffn_w_up'], 'v_ffn_conv_w': out['v_ffn_conv_w'], 'v_ffn_conv_b': out['v_ffn_conv_b'], 'v_ffn_w_down': out['v_ffn_w_down'], 'v_ln_g': out['v_ln_g'], 'v_ln_b': out['v_ln_b']}


def _loss(weights, diff, rest, loss_target):
    with _jax.named_scope("forward"):
        args = {**rest, TWIN_DIFF_INPUT: diff, **{k: w.astype(_WEIGHT_DTYPES[k]) for k, w in weights.items()}}
        y = _forward(args)
    with _jax.named_scope("loss_head"):
        err = _jnp.square(y.astype(_jnp.float32) - loss_target)
        return 0.5 * _jnp.sum(_jnp.mean(err, axis=-1)) if err.ndim else 0.5 * err


def _adamw(w, g, m, v):
    m = ADAM_B1 * m + (1.0 - ADAM_B1) * g
    v = ADAM_B2 * v + (1.0 - ADAM_B2) * _jnp.square(g)
    m_hat = m / (1.0 - ADAM_B1 ** ADAM_STEP)
    v_hat = v / (1.0 - ADAM_B2 ** ADAM_STEP)
    delta = -ADAM_LR * (m_hat / (_jnp.sqrt(v_hat) + ADAM_EPS) + ADAM_WD * w)
    return delta, m, v


def reference(x, mem, w_in, b_in, ssm_lambda_re, ssm_lambda_im, ssm_log_dt, ssm_b_re, ssm_b_im, ssm_c_re, ssm_c_im, ssm_d, ca_rel_bias, w_fox_o, w_ssm_glu, w_ca_o, w_o, xa_wq, xa_wkv, xa_wo, ffn_w_up, ffn_conv_w, ffn_conv_b, ffn_w_down, ln_g, ln_b, loss_target, m_w_in, m_b_in, m_ssm_lambda_re, m_ssm_lambda_im, m_ssm_log_dt, m_ssm_b_re, m_ssm_b_im, m_ssm_c_re, m_ssm_c_im, m_ssm_d, m_ca_rel_bias, m_w_fox_o, m_w_ssm_glu, m_w_ca_o, m_w_o, m_xa_wq, m_xa_wkv, m_xa_wo, m_ffn_w_up, m_ffn_conv_w, m_ffn_conv_b, m_ffn_w_down, m_ln_g, m_ln_b, v_w_in, v_b_in, v_ssm_lambda_re, v_ssm_lambda_im, v_ssm_log_dt, v_ssm_b_re, v_ssm_b_im, v_ssm_c_re, v_ssm_c_im, v_ssm_d, v_ca_rel_bias, v_w_fox_o, v_w_ssm_glu, v_w_ca_o, v_w_o, v_xa_wq, v_xa_wkv, v_xa_wo, v_ffn_w_up, v_ffn_conv_w, v_ffn_conv_b, v_ffn_w_down, v_ln_g, v_ln_b):
    given = dict(x=x, mem=mem, w_in=w_in, b_in=b_in, ssm_lambda_re=ssm_lambda_re, ssm_lambda_im=ssm_lambda_im, ssm_log_dt=ssm_log_dt, ssm_b_re=ssm_b_re, ssm_b_im=ssm_b_im, ssm_c_re=ssm_c_re, ssm_c_im=ssm_c_im, ssm_d=ssm_d, ca_rel_bias=ca_rel_bias, w_fox_o=w_fox_o, w_ssm_glu=w_ssm_glu, w_ca_o=w_ca_o, w_o=w_o, xa_wq=xa_wq, xa_wkv=xa_wkv, xa_wo=xa_wo, ffn_w_up=ffn_w_up, ffn_conv_w=ffn_conv_w, ffn_conv_b=ffn_conv_b, ffn_w_down=ffn_w_down, ln_g=ln_g, ln_b=ln_b, loss_target=loss_target, m_w_in=m_w_in, m_b_in=m_b_in, m_ssm_lambda_re=m_ssm_lambda_re, m_ssm_lambda_im=m_ssm_lambda_im, m_ssm_log_dt=m_ssm_log_dt, m_ssm_b_re=m_ssm_b_re, m_ssm_b_im=m_ssm_b_im, m_ssm_c_re=m_ssm_c_re, m_ssm_c_im=m_ssm_c_im, m_ssm_d=m_ssm_d, m_ca_rel_bias=m_ca_rel_bias, m_w_fox_o=m_w_fox_o, m_w_ssm_glu=m_w_ssm_glu, m_w_ca_o=m_w_ca_o, m_w_o=m_w_o, m_xa_wq=m_xa_wq, m_xa_wkv=m_xa_wkv, m_xa_wo=m_xa_wo, m_ffn_w_up=m_ffn_w_up, m_ffn_conv_w=m_ffn_conv_w, m_ffn_conv_b=m_ffn_conv_b, m_ffn_w_down=m_ffn_w_down, m_ln_g=m_ln_g, m_ln_b=m_ln_b, v_w_in=v_w_in, v_b_in=v_b_in, v_ssm_lambda_re=v_ssm_lambda_re, v_ssm_lambda_im=v_ssm_lambda_im, v_ssm_log_dt=v_ssm_log_dt, v_ssm_b_re=v_ssm_b_re, v_ssm_b_im=v_ssm_b_im, v_ssm_c_re=v_ssm_c_re, v_ssm_c_im=v_ssm_c_im, v_ssm_d=v_ssm_d, v_ca_rel_bias=v_ca_rel_bias, v_w_fox_o=v_w_fox_o, v_w_ssm_glu=v_w_ssm_glu, v_w_ca_o=v_w_ca_o, v_w_o=v_w_o, v_xa_wq=v_xa_wq, v_xa_wkv=v_xa_wkv, v_xa_wo=v_xa_wo, v_ffn_w_up=v_ffn_w_up, v_ffn_conv_w=v_ffn_conv_w, v_ffn_conv_b=v_ffn_conv_b, v_ffn_w_down=v_ffn_w_down, v_ln_g=v_ln_g, v_ln_b=v_ln_b)
    weights = {n: given[n] for n in TWIN_WEIGHTS}
    shared = {n: given[n] for n in SHARED_INPUTS}
    per_example = {n: given[n] for n in ['x', 'mem']}
    grad_fn = _jax.value_and_grad(_loss, argnums=(0, 1))

    def one_microbatch(ex, loss_target):
        ex = dict(ex)
        diff = ex.pop(TWIN_DIFF_INPUT)
        return grad_fn(weights, diff, {**shared, **ex}, loss_target)

    if N_MICROBATCH == 1:
        loss, (grad_w, grad_x) = one_microbatch(per_example, given["loss_target"])
    else:
        def body(carry, xs):
            loss_sum, grad_sum = carry
            l_k, (gw_k, gx_k) = one_microbatch(xs[0], xs[1])
            with _jax.named_scope("update"):
                return (loss_sum + l_k, _jax.tree.map(_jnp.add, grad_sum, gw_k)), gx_k

        init = (_jnp.zeros((), _jnp.float32), _jax.tree.map(_jnp.zeros_like, weights))
        (loss, grad_w), grad_x = _jax.lax.scan(body, init, (per_example, given["loss_target"]))
    with _jax.named_scope("update"):
        delta_w, new_m, new_v = {}, {}, {}
        for n in TWIN_WEIGHTS:
            delta_w[n], new_m[n], new_v[n] = _adamw(weights[n], grad_w[n], given["m_" + n], given["v_" + n])
    return (loss, grad_x, *[grad_w[n] for n in TWIN_WEIGHTS], *[delta_w[n] for n in TWIN_WEIGHTS],
            *[new_m[n] for n in TWIN_WEIGHTS], *[new_v[n] for n in TWIN_WEIGHTS])
```

```python
import functools
import math

import jax
import jax.numpy as jnp
from jax import lax
from jax.experimental import pallas as pl
from jax.experimental.pallas import tpu as pltpu

F32, BF16 = jnp.float32, jnp.bfloat16

D_MODEL = 1024
DEPTH = 4
CHUNK = 64
FOX_HEADS, FOX_HEAD_DIM, FOX_WIDTH = 8, 64, 512
SSM_GROUP, SSM_WIDTH, SSM_GROUPS, SSM_STATE = 16, 512, 32, 64
CA_HEADS, CA_HEAD_DIM, CA_WIDTH, CA_LEFT_CHUNKS = 8, 64, 512, 8
CA_BAND = (CA_LEFT_CHUNKS + 1) * CHUNK
CA_PAD = CA_LEFT_CHUNKS * CHUNK
REL_MIN, REL_MAX = -(CHUNK - 1), 4 * CHUNK
N_REL = REL_MAX - REL_MIN + 1
XA_HEADS, XA_HEAD_DIM = 4, 256
D_FF = 2816
DN_ALPHA = (2 * DEPTH) ** 0.25
LN_EPS = 1e-5
NEG_INF = -1e30
ADAM_LR, ADAM_B1, ADAM_B2, ADAM_EPS, ADAM_WD, ADAM_STEP = 0.001, 0.9, 0.999, 1e-08, 0.01, 10

N_DEV = 8
LANE = 128
N_SSM_CH = SSM_GROUPS * SSM_STATE
SCAN_CB = 256
N_IN = 6664
W_IN_SHARD, W_IN_SLOT = N_IN // N_DEV, 896
Z_W = 7168
Z_FQ, Z_FK, Z_FV, Z_FF, Z_SU, Z_CQ, Z_CK, Z_CV, Z_GF, Z_GS, Z_GC = (
    0, 512, 1024, 1536, 2048, 2560, 3072, 3584, 4096, 5120, 6144)
Z_PIECES = ((0, 512, Z_FQ), (512, 512, Z_FK), (1024, 512, Z_FV), (1536, 8, Z_FF), (1544, 512, Z_SU),
            (2056, 512, Z_CQ), (2568, 512, Z_CK), (3080, 512, Z_CV), (3592, 1024, Z_GF), (4616, 1024, Z_GS),
            (5640, 1024, Z_GC))
FF_HALF, FF_HALF_P = D_FF // N_DEV, 384
D_FF_P = N_DEV * FF_HALF_P

VMEM_LIMIT_BYTES = 56 * 1024 * 1024

BIG = {"w_in": ((1024, W_IN_SHARD), 1), "w_fox_o": ((512, 128), 1), "w_ssm_glu": ((512, 256), 1),
       "w_ca_o": ((512, 128), 1), "w_o": ((128, 1024), 0), "xa_wq": ((128, 1024), 0), "xa_wkv": ((1024, 256), 1),
       "xa_wo": ((128, 1024), 0), "ffn_w_up": ((1024, 2 * FF_HALF), 1), "ffn_w_down": ((FF_HALF, 1024), 0)}
SMALL = {"ca_rel_bias": (8, 320), "ffn_conv_w": (3, D_FF), "ln_g": (3, 1024), "ln_b": (3, 1024)}
REPL = {"b_in": (N_IN,), "ssm_lambda_re": (32, 64), "ssm_lambda_im": (32, 64), "ssm_log_dt": (32,),
        "ssm_b_re": (32, 64, 16), "ssm_b_im": (32, 64, 16), "ssm_c_re": (32, 16, 64), "ssm_c_im": (32, 16, 64),
        "ssm_d": (512,), "ffn_conv_b": (D_FF,)}
WEIGHTS = ("w_in", "b_in", "ssm_lambda_re", "ssm_lambda_im", "ssm_log_dt", "ssm_b_re", "ssm_b_im", "ssm_c_re",
           "ssm_c_im", "ssm_d", "ca_rel_bias", "w_fox_o", "w_ssm_glu", "w_ca_o", "w_o", "xa_wq", "xa_wkv", "xa_wo",
           "ffn_w_up", "ffn_conv_w", "ffn_conv_b", "ffn_w_down", "ln_g", "ln_b")
LANES = 1024
PACK_ROWS = 256
SMALL_ROWS = 8


def _w_in_segments():
    segs = []
    for src, width, dst in Z_PIECES:
        n = src
        while n < src + width:
            d = n // W_IN_SHARD
            end = min(src + width, (d + 1) * W_IN_SHARD)
            segs.append((W_IN_SLOT * d + n - W_IN_SHARD * d, dst + n - src, end - n))
            n = end
    return tuple(segs)


W_IN_SEGS = _w_in_segments()


def _pcall(body, **kw):
    return pl.pallas_call(body, **kw)


def _params(sem):
    return pltpu.CompilerParams(dimension_semantics=sem, vmem_limit_bytes=VMEM_LIMIT_BYTES)


def _pick(dim, prefs):
    for p in prefs:
        if dim % p == 0:
            return p
    return dim


def _mm(a, b, *, ta=False, tb=False, bias=None, add=None, a_off=0, a_cols=None, b_slots=False, out_slots=None,
        name, out_dtype=F32):
    a_cols = a_cols if a_cols is not None else a.shape[1]
    M, K = (a_cols, a.shape[0]) if ta else (a.shape[0], a_cols)
    tm = _pick(M, (1024, 512, 256, 128))
    if ta:
        tk = _pick(K, (1024, 512, 256))
    elif b_slots and tb:
        tk = _pick(b.shape[2], (1024, 768, 512, 256, 128))
    else:
        tk = K if K <= 3072 else _pick(K, (1024, 512, 256, 128))
    nk = K // tk
    if b_slots:
        ns = b.shape[2]
        if tb:
            N = b.shape[1]
            tn = _pick(N, (512, 256, 128))
            per = ns // tk
            b_spec = pl.BlockSpec((None, tn, tk), lambda i, j, k: (k // per, j, k % per))
            b_dim = 1
            assert N_DEV * ns == K
        else:
            N = N_DEV * ns
            tn = _pick(ns, (512, 256, 128))
            per = ns // tn
            b_spec = pl.BlockSpec((None, tk, tn), lambda i, j, k: (j // per, k, j % per))
            b_dim = 0
            assert b.shape[1] == K
    else:
        N = b.shape[0] if tb else b.shape[1]
        assert (b.shape[1] if tb else b.shape[0]) == K, (a.shape, b.shape, ta, tb)
        tn = _pick(N if out_slots is None else out_slots, (512, 256, 128))
        if tb:
            b_spec = pl.BlockSpec((tn, tk), lambda i, j, k: (j, k))
            b_dim = 1
        else:
            b_spec = pl.BlockSpec((tk, tn), lambda i, j, k: (k, j))
            b_dim = 0
    if ta:
        assert a_off % tm == 0
        a_spec = pl.BlockSpec((tk, tm), lambda i, j, k: (k, i + a_off // tm))
        a_dim = 0
    else:
        assert a_off % tk == 0
        a_spec = pl.BlockSpec((tm, tk), lambda i, j, k: (i, k + a_off // tk))
        a_dim = 1
    dims = (((a_dim,), (b_dim,)), ((), ()))
    ins, specs = [a, b], [a_spec, b_spec]
    if bias is not None:
        ins.append(bias)
        specs.append(pl.BlockSpec((1, tn), lambda i, j, k: (0, j)))
    add_scale = None
    if add is not None:
        ins.append(add[0])
        add_scale = add[1]
        specs.append(pl.BlockSpec((tm, tn), lambda i, j, k: (i, j)))
    if out_slots is None:
        out_spec = pl.BlockSpec((tm, tn), lambda i, j, k: (i, j))
        out_shape = jax.ShapeDtypeStruct((M, N), out_dtype)
    else:
        assert N == N_DEV * out_slots
        per_o = out_slots // tn
        out_spec = pl.BlockSpec((None, tm, tn), lambda i, j, k: (j // per_o, i, j % per_o))
        out_shape = jax.ShapeDtypeStruct((N_DEV, M, out_slots), out_dtype)

    def body(*refs):
        a_ref, b_ref = refs[0], refs[1]
        pos = 2
        bias_ref = add_ref = None
        if bias is not None:
            bias_ref = refs[pos]
            pos += 1
        if add is not None:
            add_ref = refs[pos]
            pos += 1
        o_ref = refs[pos]
        acc_ref = refs[pos + 1] if nk > 1 else None
        part = lax.dot_general(a_ref[...].astype(BF16), b_ref[...].astype(BF16), dims, preferred_element_type=F32)

        def finish(acc):
            if bias_ref is not None:
                acc = acc + bias_ref[...]
            if add_ref is not None:
                acc = acc + add_scale * add_ref[...]
            o_ref[...] = acc.astype(out_dtype)

        if nk == 1:
            finish(part)
        else:
            k = pl.program_id(2)

            @pl.when(k == 0)
            def _():
                acc_ref[...] = part

            @pl.when(k > 0)
            def _():
                acc_ref[...] += part

            @pl.when(k == nk - 1)
            def _():
                finish(acc_ref[...])

    return _pcall(
        body, name=name, grid=(M // tm, N // tn, nk), in_specs=specs, out_specs=out_spec, out_shape=out_shape,
        scratch_shapes=[pltpu.VMEM((tm, tn), F32)] if nk > 1 else [],
        compiler_params=_params(("parallel", "parallel", "arbitrary")),
    )(*ins)


class Win:
    def __init__(self, arr, width, blk):
        self.arr, self.width, self.blk = arr, width, blk


def _rowwise(fn, rows, vecs=(), *, n_red=0, tr=256, name):
    wins = [r if isinstance(r, Win) else Win(r, r.shape[1], 0) for r in rows]
    S = wins[0].arr.shape[0]
    tr = min(tr, S)
    tile_args = [jax.ShapeDtypeStruct((tr, w.width), w.arr.dtype) for w in wins]
    tile_args += [jax.ShapeDtypeStruct(v.shape, v.dtype) for v in vecs]
    outs = jax.eval_shape(fn, *tile_args)
    n_row = len(outs) - n_red
    specs = [pl.BlockSpec((tr, w.width), functools.partial(lambda i, b: (i, b), b=w.blk)) for w in wins]
    specs += [pl.BlockSpec(v.shape, functools.partial(lambda i, nd: (0,) * nd, nd=v.ndim)) for v in vecs]
    out_specs = [pl.BlockSpec((tr, o.shape[1]), lambda i: (i, 0)) for o in outs[:n_row]]
    out_specs += [pl.BlockSpec(o.shape, functools.partial(lambda i, nd: (0,) * nd, nd=len(o.shape))) for o in outs[n_row:]]
    out_shape = [jax.ShapeDtypeStruct((S, o.shape[1]), o.dtype) for o in outs[:n_row]]
    out_shape += [jax.ShapeDtypeStruct(o.shape, o.dtype) for o in outs[n_row:]]
    n_in = len(wins) + len(vecs)

    def body(*refs):
        res = fn(*[r[...] for r in refs[:n_in]])
        o_refs = refs[n_in:]
        for o_ref, r in zip(o_refs[:n_row], res[:n_row]):
            o_ref[...] = r.astype(o_ref.dtype)
        i = pl.program_id(0)
        for o_ref, r in zip(o_refs[n_row:], res[n_row:]):
            @pl.when(i == 0)
            def _(o_ref=o_ref, r=r):
                o_ref[...] = r

            @pl.when(i > 0)
            def _(o_ref=o_ref, r=r):
                o_ref[...] += r

    return _pcall(
        body, name=name, grid=(S // tr,), in_specs=specs, out_specs=out_specs, out_shape=out_shape,
        compiler_params=_params(("arbitrary",)),
    )(*[w.arr for w in wins], *vecs)


def _whole(fn, *arrays, name):
    outs = jax.eval_shape(fn, *arrays)
    n_in = len(arrays)

    def body(*refs):
        res = fn(*[r[...] for r in refs[:n_in]])
        for o_ref, r in zip(refs[n_in:], res):
            o_ref[...] = r

    vm = pl.BlockSpec(memory_space=pltpu.VMEM)
    return _pcall(body, name=name, in_specs=[vm] * n_in, out_specs=[vm] * len(outs),
                  out_shape=[jax.ShapeDtypeStruct(o.shape, o.dtype) for o in outs])(*arrays)


def _split3(x):
    hi = x.astype(BF16)
    r = x - hi.astype(F32)
    mid = r.astype(BF16)
    lo = (r - mid.astype(F32)).astype(BF16)
    return hi, mid, lo


def _dot3(x, onehot, dims):
    return sum(lax.dot_general(t, onehot, dims, preferred_element_type=F32) for t in _split3(x))


_NT = (((1,), (1,)), ((), ()))
_NN = (((1,), (0,)), ((), ()))
_TN = (((0,), (0,)), ((), ()))


def _colmap(x, *, inverse, name):
    R = x.shape[0] if inverse else x.shape[1]
    tr = 256
    per = W_IN_SLOT // LANE
    n_out = N_DEV * per if inverse else Z_W // LANE
    segs = [(p, q, n) for q, p, n in W_IN_SEGS] if inverse else list(W_IN_SEGS)

    def body(x_ref, o_ref):
        ia = lax.broadcasted_iota(jnp.int32, (LANE, LANE), 0)
        ib = lax.broadcasted_iota(jnp.int32, (LANE, LANE), 1)

        def src_block(i):
            if inverse:
                return x_ref[:, i * LANE:(i + 1) * LANE]
            return x_ref[i // per, :, (i % per) * LANE:(i % per + 1) * LANE]

        for jb in range(n_out):
            acc = None
            for s0, d0, n in segs:
                lo, hi = max(d0, jb * LANE), min(d0 + n, (jb + 1) * LANE)
                if lo >= hi:
                    continue
                delta = d0 - s0
                for i in range((lo - delta) // LANE, (hi - delta - 1) // LANE + 1):
                    shift = jb * LANE - i * LANE - delta
                    sel = ((ia - ib == shift) & (ib >= lo - jb * LANE) & (ib < hi - jb * LANE)).astype(BF16)
                    blk = src_block(i)
                    part = _dot3(blk, sel, _NN) if inverse else lax.dot_general(blk, sel, _NN, preferred_element_type=F32)
                    acc = part if acc is None else acc + part
            if acc is None:
                acc = jnp.zeros((tr, LANE), F32)
            if inverse:
                o_ref[jb // per, :, (jb % per) * LANE:(jb % per + 1) * LANE] = acc
            else:
                o_ref[:, jb * LANE:(jb + 1) * LANE] = acc.astype(BF16)

    slot_spec = pl.BlockSpec((N_DEV, tr, W_IN_SLOT), lambda i: (0, i, 0))
    flat_spec = pl.BlockSpec((tr, Z_W), lambda i: (i, 0))
    if inverse:
        return _pcall(body, name=name, grid=(R // tr,), in_specs=[flat_spec], out_specs=slot_spec,
                      out_shape=jax.ShapeDtypeStruct((N_DEV, R, W_IN_SLOT), F32), compiler_params=_params(("parallel",)))(x)
    return _pcall(body, name=name, grid=(R // tr,), in_specs=[slot_spec], out_specs=flat_spec,
                  out_shape=jax.ShapeDtypeStruct((R, Z_W), BF16), compiler_params=_params(("parallel",)))(x)


def _log_sigmoid(x):
    return jnp.minimum(x, 0.0) - jnp.log(1.0 + jnp.exp(-jnp.abs(x)))


def _cum_heads(f, dcum=None, *, name):
    H, S = f.shape
    tn = min(512, S)
    rev = dcum is not None

    def body(*refs):
        j = pl.program_id(0)
        s_idx = lax.broadcasted_iota(jnp.int32, (S, tn), 0)
        t_idx = lax.broadcasted_iota(jnp.int32, (S, tn), 1) + j * tn
        if not rev:
            f_ref, o_ref = refs
            tri = (s_idx <= t_idx).astype(BF16)
            o_ref[...] = _dot3(_log_sigmoid(f_ref[...]), tri, _NN)
        else:
            fj_ref, d_ref, o_ref = refs
            tri = (s_idx >= t_idx).astype(BF16)
            o_ref[...] = _dot3(d_ref[...], tri, _NN) * jax.nn.sigmoid(-fj_ref[...])

    full = pl.BlockSpec((H, S), lambda j: (0, 0))
    blk = pl.BlockSpec((H, tn), lambda j: (0, j))
    ins, specs = ([f], [full]) if not rev else ([f, dcum], [blk, full])
    return _pcall(body, name=name, grid=(S // tn,), in_specs=specs, out_specs=blk,
                  out_shape=jax.ShapeDtypeStruct((H, S), F32), compiler_params=_params(("arbitrary",)))(*ins)


def _rel_onehot(qi, transposed):
    shape = (N_REL, CA_BAND) if transposed else (CA_BAND, N_REL)
    kk = lax.broadcasted_iota(jnp.int32, shape, 1 if transposed else 0)
    rr = lax.broadcasted_iota(jnp.int32, shape, 0 if transposed else 1)
    idx = jnp.clip(CA_PAD + qi - kk, REL_MIN, REL_MAX) - REL_MIN
    return (idx == rr).astype(BF16)


def _relbias_expand(rb):
    def body(rb_ref, o_ref):
        o_ref[0] = _dot3(rb_ref[...], _rel_onehot(pl.program_id(0), True), _NN)

    return _pcall(body, name="relbias_expand", grid=(CHUNK,),
                  in_specs=[pl.BlockSpec((CA_HEADS, N_REL), lambda q: (0, 0))],
                  out_specs=pl.BlockSpec((1, CA_HEADS, CA_BAND), lambda q: (q, 0, 0)),
                  out_shape=jax.ShapeDtypeStruct((CHUNK, CA_HEADS, CA_BAND), F32),
                  compiler_params=_params(("arbitrary",)))(rb)


def _relbias_reduce(db):
    def body(db_ref, o_ref):
        q = pl.program_id(0)
        part = _dot3(db_ref[0], _rel_onehot(q, False), _NN)

        @pl.when(q == 0)
        def _():
            o_ref[...] = part

        @pl.when(q > 0)
        def _():
            o_ref[...] += part

    return _pcall(body, name="relbias_reduce", grid=(CHUNK,),
                  in_specs=[pl.BlockSpec((1, CA_HEADS, CA_BAND), lambda q: (q, 0, 0))],
                  out_specs=pl.BlockSpec((CA_HEADS, N_REL), lambda q: (0, 0)),
                  out_shape=jax.ShapeDtypeStruct((CA_HEADS, N_REL), F32),
                  compiler_params=_params(("arbitrary",)))(db)


def _attn_cfg(mode, S):
    if mode == "fox":
        return min(256, S), FOX_HEAD_DIM ** -0.5
    if mode == "chunk":
        return CHUNK, CA_HEAD_DIM ** -0.5
    return min(512, S), XA_HEAD_DIM ** -0.5


def _scores(mode, i, tq, scale, qb, kb, extra):
    s = lax.dot_general(qb, kb, _NT, preferred_element_type=F32) * scale
    nk = kb.shape[0]
    if mode == "fox":
        cq, ck = extra
        s = s + cq - ck
        row = lax.broadcasted_iota(jnp.int32, (tq, nk), 0) + i * tq
        col = lax.broadcasted_iota(jnp.int32, (tq, nk), 1)
        s = jnp.where(row >= col, s, NEG_INF)
    elif mode == "chunk":
        (bias,) = extra
        col = lax.broadcasted_iota(jnp.int32, (tq, nk), 1) + i * CHUNK
        s = jnp.where(col >= CA_PAD, s + bias, NEG_INF)
    return s


def _attn_specs(mode, q, k, tq):
    H, S, D = q.shape
    Sk = k.shape[1]
    q_spec = pl.BlockSpec((1, tq, D), lambda h, i: (h, i, 0))
    kv_spec = pl.BlockSpec((1, Sk, D), lambda h, i: (h, 0, 0))
    if mode == "fox":
        ex = [pl.BlockSpec((1, tq, 1), lambda h, i: (h, i, 0)), pl.BlockSpec((1, 1, Sk), lambda h, i: (h, 0, 0))]
    elif mode == "chunk":
        ex = [pl.BlockSpec((1, CHUNK, CA_BAND), lambda h, i: (h, 0, 0))]
    else:
        ex = []
    col_spec = pl.BlockSpec((1, tq, 1), lambda h, i: (h, i, 0))
    return q_spec, kv_spec, ex, col_spec


def _kv_window(mode, i, ref):
    if mode == "chunk":
        return ref[0, pl.ds(pl.multiple_of(i * CHUNK, CHUNK), CA_BAND), :]
    return ref[0]


def _attn_fwd(mode, q, k, v, extra=(), *, name):
    H, S, D = q.shape
    tq, scale = _attn_cfg(mode, S)
    q_spec, kv_spec, ex_specs, col_spec = _attn_specs(mode, q, k, tq)
    n_ex = len(extra)

    def body(*refs):
        q_ref, k_ref, v_ref = refs[:3]
        ex = [r[0] for r in refs[3:3 + n_ex]]
        o_ref, lse_ref = refs[3 + n_ex:]
        i = pl.program_id(1)
        kb = _kv_window(mode, i, k_ref).astype(BF16)
        vb = _kv_window(mode, i, v_ref).astype(BF16)
        s = _scores(mode, i, tq, scale, q_ref[0].astype(BF16), kb, ex)
        m = jnp.max(s, axis=1, keepdims=True)
        e = jnp.exp(s - m)
        l = jnp.sum(e, axis=1, keepdims=True)
        p = e / l
        o_ref[0] = lax.dot_general(p.astype(BF16), vb, _NN, preferred_element_type=F32)
        lse_ref[0] = m + jnp.log(l)

    return _pcall(
        body, name=name, grid=(H, S // tq), in_specs=[q_spec, kv_spec, kv_spec] + ex_specs,
        out_specs=[q_spec, col_spec],
        out_shape=[jax.ShapeDtypeStruct((H, S, D), F32), jax.ShapeDtypeStruct((H, S, 1), F32)],
        compiler_params=_params(("parallel", "arbitrary")),
    )(q, k, v, *extra)


def _attn_bwd(mode, q, k, v, lse, do, extra=(), *, name):
    H, S, D = q.shape
    Sk = k.shape[1]
    tq, scale = _attn_cfg(mode, S)
    q_spec, kv_spec, ex_specs, col_spec = _attn_specs(mode, q, k, tq)
    n_ex = len(extra)
    out_specs = [q_spec, kv_spec, kv_spec]
    out_shape = [jax.ShapeDtypeStruct((H, S, D), F32), jax.ShapeDtypeStruct((H, Sk, D), F32),
                 jax.ShapeDtypeStruct((H, Sk, D), F32)]
    if mode == "fox":
        out_specs += [col_spec, ex_specs[1]]
        out_shape += [jax.ShapeDtypeStruct((H, S, 1), F32), jax.ShapeDtypeStruct((H, 1, Sk), F32)]
    elif mode == "chunk":
        out_specs += [ex_specs[0]]
        out_shape += [jax.ShapeDtypeStruct((H, CHUNK, CA_BAND), F32)]

    def body(*refs):
        q_ref, k_ref, v_ref, lse_ref, do_ref = refs[:5]
        ex = [r[0] for r in refs[5:5 + n_ex]]
        dq_ref, dk_ref, dv_ref = refs[5 + n_ex:8 + n_ex]
        rest = refs[8 + n_ex:]
        i = pl.program_id(1)
        qb = q_ref[0].astype(BF16)
        kb = _kv_window(mode, i, k_ref).astype(BF16)
        vb = _kv_window(mode, i, v_ref).astype(BF16)
        dob = do_ref[0].astype(BF16)
        s = _scores(mode, i, tq, scale, qb, kb, ex)
        p = jnp.exp(s - lse_ref[0])
        dp = lax.dot_general(dob, vb, _NT, preferred_element_type=F32)
        ds = p * (dp - jnp.sum(dp * p, axis=1, keepdims=True))
        dsb = (ds * scale).astype(BF16)
        dq_ref[0] = lax.dot_general(dsb, kb, _NN, preferred_element_type=F32)
        dk_part = lax.dot_general(dsb, qb, _TN, preferred_element_type=F32)
        dv_part = lax.dot_general(p.astype(BF16), dob, _TN, preferred_element_type=F32)

        @pl.when(i == 0)
        def _():
            dk_ref[...] = jnp.zeros_like(dk_ref)
            dv_ref[...] = jnp.zeros_like(dv_ref)
            if mode == "fox":
                rest[1][...] = jnp.zeros_like(rest[1])
            elif mode == "chunk":
                rest[0][...] = jnp.zeros_like(rest[0])

        if mode == "chunk":
            win = pl.ds(pl.multiple_of(i * CHUNK, CHUNK), CA_BAND)
            dk_ref[0, win, :] += dk_part
            dv_ref[0, win, :] += dv_part
            rest[0][0] += ds
        else:
            dk_ref[0] += dk_part
            dv_ref[0] += dv_part
        if mode == "fox":
            rest[0][0] = jnp.sum(ds, axis=1, keepdims=True)
            rest[1][0] += -jnp.sum(ds, axis=0, keepdims=True)

    return _pcall(
        body, name=name, grid=(H, S // tq),
        in_specs=[q_spec, kv_spec, kv_spec, col_spec, q_spec] + ex_specs,
        out_specs=out_specs, out_shape=out_shape,
        compiler_params=_params(("parallel", "arbitrary")),
    )(q, k, v, lse, do, *extra)


def _scan(x, a, h=None, *, name):
    S = x.shape[0]
    CB = SCAN_CB
    rev = h is not None
    n_grp = S // 8

    def body(*refs):
        if rev:
            x_ref, a_ref, h_ref, o_ref, da_ref = refs
        else:
            x_ref, a_ref, o_ref = refs
        ar = a_ref[:, :CB]
        ai = -a_ref[:, CB:] if rev else a_ref[:, CB:]
        zero = jnp.zeros((1, CB), F32)

        def group(g, carry):
            base = pl.multiple_of((n_grp - 1 - g) * 8 if rev else g * 8, 8)
            for j in (range(7, -1, -1) if rev else range(8)):
                t = base + j
                if rev:
                    hr, hi, dar, dai = carry
                else:
                    hr, hi = carry
                xr = x_ref[pl.ds(t, 1), :CB]
                xi = x_ref[pl.ds(t, 1), CB:]
                hr, hi = ar * hr - ai * hi + xr, ar * hi + ai * hr + xi
                o_ref[pl.ds(t, 1), :CB] = hr
                o_ref[pl.ds(t, 1), CB:] = hi
                if rev:
                    tp = jnp.maximum(t - 1, 0)
                    live = (t > 0).astype(F32)
                    pr = h_ref[pl.ds(tp, 1), :CB] * live
                    pi = h_ref[pl.ds(tp, 1), CB:] * live
                    carry = (hr, hi, dar + hr * pr + hi * pi, dai + hi * pr - hr * pi)
                else:
                    carry = (hr, hi)
            return carry

        if rev:
            _, _, dar, dai = lax.fori_loop(0, n_grp, group, (zero, zero, zero, zero))
            da_ref[:, :CB] = dar
            da_ref[:, CB:] = dai
        else:
            lax.fori_loop(0, n_grp, group, (zero, zero))

    big = pl.BlockSpec((S, 2 * CB), lambda c: (0, c))
    vec = pl.BlockSpec((1, 2 * CB), lambda c: (0, c))
    n_blk = x.shape[1] // (2 * CB)
    if rev:
        return _pcall(body, name=name, grid=(n_blk,), in_specs=[big, vec, big], out_specs=[big, vec],
                      out_shape=[jax.ShapeDtypeStruct(x.shape, F32), jax.ShapeDtypeStruct(a.shape, F32)],
                      compiler_params=_params(("parallel",)))(x, a, h)
    return _pcall(body, name=name, grid=(n_blk,), in_specs=[big, vec], out_specs=big,
                  out_shape=jax.ShapeDtypeStruct(x.shape, F32), compiler_params=_params(("parallel",)))(x, a)


def _ssm_prep1(lr_, li, ldt):
    lr = jnp.minimum(lr_, -1e-4)
    dt = jnp.exp(ldt)
    mag = jnp.exp(lr * dt)
    ar = mag * jnp.cos(li * dt)
    ai = mag * jnp.sin(li * dt)
    den = lr * lr + li * li
    gr = ((ar - 1.0) * lr + ai * li) / den
    gi = (ai * lr - (ar - 1.0) * li) / den
    return ar, ai, gr, gi


def _ssm_prep2(gr, gi, br, bi):
    return gr * br - gi * bi, gr * bi + gi * br


def _vjp_of(fn, n_in):
    def bwd(*args):
        cts = args[n_in:]
        return jax.vjp(fn, *args[:n_in])[1](cts[0] if len(cts) == 1 else tuple(cts))
    return bwd


def _to_blocked(r, i):
    lead = r.shape[:-1]
    t = jnp.stack([r.reshape(lead + (N_SSM_CH // SCAN_CB, SCAN_CB)), i.reshape(lead + (N_SSM_CH // SCAN_CB, SCAN_CB))],
                  axis=-2)
    return t.reshape(lead + (2 * N_SSM_CH,))


def _from_blocked(m):
    lead = m.shape[:-1]
    t = m.reshape(lead + (N_SSM_CH // SCAN_CB, 2, SCAN_CB))
    return t[..., 0, :].reshape(lead + (N_SSM_CH,)), t[..., 1, :].reshape(lead + (N_SSM_CH,))


def _blockdiag(r, i):
    eye = jnp.eye(SSM_GROUPS, dtype=F32)
    bd = lambda t: jnp.einsum("gcp,gh->gchp", t, eye).reshape(SSM_WIDTH, N_SSM_CH)
    return _to_blocked(bd(r), bd(i))


def _blockdiag_inv(m):
    eye = jnp.eye(SSM_GROUPS, dtype=F32)
    r, i = _from_blocked(m)
    diag = lambda t: jnp.einsum("gchp,gh->gcp", t.reshape(SSM_GROUPS, SSM_GROUP, SSM_GROUPS, SSM_STATE), eye)
    return diag(r), diag(i)


def _shift_rows(x, n):
    S = x.shape[0]
    row = lax.broadcasted_iota(jnp.int32, x.shape, 0)
    if n > 0:
        return jnp.where(row >= n, pltpu.roll(x, n, 0), 0.0)
    return jnp.where(row < S + n, pltpu.roll(x, S + n, 0), 0.0)


def _bf(x):
    return x.astype(BF16).astype(F32)


def _conv_pre(a, w, b):
    ab, wb = _bf(a), _bf(w)
    return wb[2:3] * ab + wb[1:2] * _shift_rows(ab, 1) + wb[0:1] * _shift_rows(ab, 2) + b


def _ffn_mid(up, conv_w, conv_b, dh=None, *, name):
    S = up.shape[0]
    tn = LANE
    nb = D_FF_P // tn
    rev = dh is not None

    def body(*refs):
        if not rev:
            a_ref, g_ref, w_ref, b_ref, o_ref = refs
            o_ref[...] = jax.nn.gelu(_conv_pre(a_ref[...], w_ref[...], b_ref[...])) * g_ref[...]
            return
        a_ref, g_ref, w_ref, b_ref, dh_ref, dup_a_ref, dup_g_ref, dw_ref, db_ref = refs
        a, w, dh_ = a_ref[...], w_ref[...], dh_ref[...]
        pre = _conv_pre(a, w, b_ref[...])
        gl, gelu_vjp = jax.vjp(jax.nn.gelu, pre)
        dup_g_ref[...] = dh_ * gl
        (dpre,) = gelu_vjp(dh_ * g_ref[...])
        db_ref[...] = jnp.sum(dpre, axis=0, keepdims=True)
        dpb, ab, wb = _bf(dpre), _bf(a), _bf(w)
        dup_a_ref[...] = wb[2:3] * dpb + wb[1:2] * _shift_rows(dpb, -1) + wb[0:1] * _shift_rows(dpb, -2)
        dw_ref[2:3, :] = jnp.sum(dpb * ab, axis=0, keepdims=True)
        dw_ref[1:2, :] = jnp.sum(dpb * _shift_rows(ab, 1), axis=0, keepdims=True)
        dw_ref[0:1, :] = jnp.sum(dpb * _shift_rows(ab, 2), axis=0, keepdims=True)

    a_spec = pl.BlockSpec((S, tn), lambda j: (0, j))
    g_spec = pl.BlockSpec((S, tn), lambda j: (0, j + nb))
    w_spec = pl.BlockSpec((3, tn), lambda j: (0, j))
    b_spec = pl.BlockSpec((1, tn), lambda j: (0, j))
    if not rev:
        return _pcall(body, name=name, grid=(nb,), in_specs=[a_spec, g_spec, w_spec, b_spec], out_specs=a_spec,
                      out_shape=jax.ShapeDtypeStruct((S, D_FF_P), F32), compiler_params=_params(("parallel",)))(
                          up, up, conv_w, conv_b)
    return _pcall(body, name=name, grid=(nb,), in_specs=[a_spec, g_spec, w_spec, b_spec, a_spec],
                  out_specs=[a_spec, a_spec, w_spec, b_spec],
                  out_shape=[jax.ShapeDtypeStruct((S, D_FF_P), F32), jax.ShapeDtypeStruct((S, D_FF_P), F32),
                             jax.ShapeDtypeStruct((3, D_FF_P), F32), jax.ShapeDtypeStruct((1, D_FF_P), F32)],
                  compiler_params=_params(("parallel",)))(up, up, conv_w, conv_b, dh)


def _ff_pad(t):
    lead = t.shape[:-1]
    t = t.reshape(lead + (N_DEV, FF_HALF))
    return jnp.pad(t, [(0, 0)] * len(lead) + [(0, 0), (0, FF_HALF_P - FF_HALF)]).reshape(lead + (D_FF_P,))


def _ff_unpad(t):
    lead = t.shape[:-1]
    return t.reshape(lead + (N_DEV, FF_HALF_P))[..., :FF_HALF].reshape(lead + (D_FF,))


def _ln_fwd(x, h, g, b):
    r = DN_ALPHA * x + h
    mu = jnp.mean(r, axis=-1, keepdims=True)
    var = jnp.mean(jnp.square(r - mu), axis=-1, keepdims=True)
    return r, (r - mu) * lax.rsqrt(var + LN_EPS) * g + b


def _ln_bwd(r, dy, g):
    mu = jnp.mean(r, axis=-1, keepdims=True)
    var = jnp.mean(jnp.square(r - mu), axis=-1, keepdims=True)
    xhat = (r - mu) * lax.rsqrt(var + LN_EPS)
    dxh = dy * g
    dr = lax.rsqrt(var + LN_EPS) * (dxh - jnp.mean(dxh, axis=-1, keepdims=True)
                                    - xhat * jnp.mean(dxh * xhat, axis=-1, keepdims=True))
    return dr, jnp.sum(dy * xhat, axis=0, keepdims=True), jnp.sum(dy, axis=0, keepdims=True)


def _merge(gf, gs, gc, ya, yb2, yc):
    yb = yb2[:, :D_MODEL] * jax.nn.sigmoid(yb2[:, D_MODEL:])
    return jax.nn.sigmoid(gf) * ya + jax.nn.sigmoid(gs) * yb + jax.nn.sigmoid(gc) * yc


def _s5_tail(hc, su, d):
    return jax.nn.gelu(hc + d * su)


def _s5_tail_bwd(hc, su, dgel, d):
    _, vjp = jax.vjp(jax.nn.gelu, hc + d * su)
    (dy,) = vjp(dgel)
    return dy, d * dy, jnp.sum(dy * su, axis=0, keepdims=True)


def _loss_rows(y, tgt):
    err = y - tgt
    return err * (1.0 / D_MODEL), jnp.sum(0.5 * jnp.square(err), axis=0, keepdims=True) * (1.0 / D_MODEL)


def _peer(k):
    x, y, c = lax.axis_index("x"), lax.axis_index("y"), lax.axis_index("c")
    return (x ^ ((k >> 2) & 1), y ^ ((k >> 1) & 1), c ^ (k & 1))


def _my_slot():
    return 4 * lax.axis_index("x") + 2 * lax.axis_index("y") + lax.axis_index("c")


def _peer_slot(k):
    px, py, pc = _peer(k)
    return 4 * px + 2 * py + pc


def _all_gather(shards, *, name):
    n = len(shards)

    def body(*refs):
        ins, outs = refs[:n], refs[n:2 * n]
        send, recv, loc = refs[2 * n:]
        me = _my_slot()
        copies = []
        for t in range(n):
            lc = pltpu.make_async_copy(ins[t], outs[t].at[me], loc.at[t])
            lc.start()
            copies.append(lc)
        for k in range(1, N_DEV):
            for t in range(n):
                pltpu.make_async_remote_copy(
                    src_ref=ins[t], dst_ref=outs[t].at[me], send_sem=send.at[t, k - 1], recv_sem=recv.at[t, k - 1],
                    device_id=_peer(k), device_id_type=pl.DeviceIdType.MESH).start()
        for k in range(1, N_DEV):
            for t in range(n):
                pltpu.make_async_remote_copy(
                    src_ref=ins[t], dst_ref=outs[t].at[_peer_slot(k)], send_sem=send.at[t, k - 1],
                    recv_sem=recv.at[t, k - 1], device_id=_peer(k), device_id_type=pl.DeviceIdType.MESH).wait()
        for lc in copies:
            lc.wait()

    hbm = pl.BlockSpec(memory_space=pl.ANY)
    return _pcall(
        body, name=name, in_specs=[hbm] * n, out_specs=[hbm] * n,
        out_shape=[jax.ShapeDtypeStruct((N_DEV,) + s.shape, s.dtype) for s in shards],
        scratch_shapes=[pltpu.SemaphoreType.DMA((n, N_DEV - 1)), pltpu.SemaphoreType.DMA((n, N_DEV - 1)),
                        pltpu.SemaphoreType.DMA((n,))],
    )(*shards)


def _exchange_slots(grads, *, name):
    n = len(grads)

    def body(*refs):
        ins, outs = refs[:n], refs[n:2 * n]
        send, recv, loc = refs[2 * n:]
        me = _my_slot()
        copies = []
        for t in range(n):
            lc = pltpu.make_async_copy(ins[t].at[me], outs[t].at[me], loc.at[t])
            lc.start()
            copies.append(lc)
        for k in range(1, N_DEV):
            for t in range(n):
                pltpu.make_async_remote_copy(
                    src_ref=ins[t].at[_peer_slot(k)], dst_ref=outs[t].at[me], send_sem=send.at[t, k - 1],
                    recv_sem=recv.at[t, k - 1], device_id=_peer(k), device_id_type=pl.DeviceIdType.MESH).start()
        for k in range(1, N_DEV):
            for t in range(n):
                pltpu.make_async_remote_copy(
                    src_ref=ins[t].at[_peer_slot(k)], dst_ref=outs[t].at[_peer_slot(k)], send_sem=send.at[t, k - 1],
                    recv_sem=recv.at[t, k - 1], device_id=_peer(k), device_id_type=pl.DeviceIdType.MESH).wait()
        for lc in copies:
            lc.wait()

    hbm = pl.BlockSpec(memory_space=pl.ANY)
    return _pcall(
        body, name=name, in_specs=[hbm] * n, out_specs=[hbm] * n,
        out_shape=[jax.ShapeDtypeStruct(g.shape, g.dtype) for g in grads],
        scratch_shapes=[pltpu.SemaphoreType.DMA((n, N_DEV - 1)), pltpu.SemaphoreType.DMA((n, N_DEV - 1)),
                        pltpu.SemaphoreType.DMA((n,))],
    )(*grads)


def _adamw(recv, w, m, v, layer=None, *, name):
    _, R, C = recv.shape
    tr = _pick(R, (128, 64, 32, 16, 8))

    def body(r_ref, w_ref, m_ref, v_ref, g_ref, d_ref, nm_ref, nv_ref):
        g = r_ref[0]
        for s in range(1, N_DEV):
            g = g + r_ref[s]
        m_new = ADAM_B1 * m_ref[...] + (1.0 - ADAM_B1) * g
        v_new = ADAM_B2 * v_ref[...] + (1.0 - ADAM_B2) * jnp.square(g)
        m_hat = m_new / (1.0 - ADAM_B1 ** ADAM_STEP)
        v_hat = v_new / (1.0 - ADAM_B2 ** ADAM_STEP)
        g_ref[...] = g
        d_ref[...] = -ADAM_LR * (m_hat / (jnp.sqrt(v_hat) + ADAM_EPS) + ADAM_WD * w_ref[...])
        nm_ref[...] = m_new
        nv_ref[...] = v_new

    row = pl.BlockSpec((tr, C), lambda i: (i, 0))
    state = row if layer is None else pl.BlockSpec((None, tr, C), lambda i: (layer, i, 0))
    return _pcall(
        body, name=name, grid=(R // tr,),
        in_specs=[pl.BlockSpec((N_DEV, tr, C), lambda i: (0, i, 0)), state, state, state],
        out_specs=[row] * 4, out_shape=[jax.ShapeDtypeStruct((R, C), F32)] * 4,
        compiler_params=_params(("parallel",)),
    )(recv, w, m, v)


def _flat_pad(parts, rows):
    flat = jnp.concatenate([p.reshape(-1) for p in parts])
    return jnp.pad(flat, (0, rows * LANES - flat.shape[0])).reshape(rows, LANES)


def _small_shard_shape(n):
    return SMALL[n][:-1] + (SMALL[n][-1] // N_DEV,)


def _unpack_small(gathered):
    out, off = {}, 0
    flat = gathered.reshape(N_DEV, -1)
    for n in SMALL:
        r, c = _small_shard_shape(n)
        out[n] = flat[:, off:off + r * c].reshape(N_DEV, r, c).transpose(1, 0, 2).reshape(r, N_DEV * c)
        off += r * c
    return out


def _pack_state(state, prefix, layer):
    return _flat_pad([state[prefix + n][layer] for n in (*SMALL, *REPL)], PACK_ROWS)


def _pack_small_grads(grads):
    cols = []
    for n in SMALL:
        r, c = _small_shard_shape(n)
        cols.append(grads[n].reshape(r, N_DEV, c).transpose(1, 0, 2).reshape(N_DEV, r * c))
    cols += [jnp.broadcast_to(grads[n].reshape(1, -1), (N_DEV, grads[n].size)) for n in REPL]
    flat = jnp.concatenate(cols, axis=1)
    return jnp.pad(flat, ((0, 0), (0, PACK_ROWS * LANES - flat.shape[1]))).reshape(N_DEV, PACK_ROWS, LANES)


def _unpack_state(flat):
    out, off = {}, 0
    flat = flat.reshape(-1)
    for n, shape in [(n, _small_shard_shape(n)) for n in SMALL] + list(REPL.items()):
        sz = math.prod(shape)
        out[n] = flat[off:off + sz].reshape(shape)
        off += sz
    return out


def _pad_b_in(b):
    parts, pos = [], 0
    for src, width, dst in Z_PIECES:
        parts += [jnp.zeros((1, dst - pos), b.dtype), b[:, src:src + width]]
        pos = dst + width
    return jnp.concatenate(parts + [jnp.zeros((1, Z_W - pos), b.dtype)], axis=1)


def _unpad_b_in(bp):
    return jnp.concatenate([bp[:, dst:dst + width] for _, width, dst in Z_PIECES], axis=1)


def _heads(t, n_heads):
    S = t.shape[0]
    return t.reshape(S, n_heads, -1).transpose(1, 0, 2)


def _unheads(t):
    H, S, D = t.shape
    return t.transpose(1, 0, 2).reshape(S, H * D)


def _up_shard_pad(t):
    lead = t.shape[:-1]
    t = jnp.pad(t.reshape(lead + (2, FF_HALF)), [(0, 0)] * len(lead) + [(0, 0), (0, FF_HALF_P - FF_HALF)])
    return t.reshape(lead + (2 * FF_HALF_P,))


def _up_shard_unpad(t):
    lead = t.shape[:-1]
    return t.reshape(lead + (2, FF_HALF_P))[..., :FF_HALF].reshape(lead + (2 * FF_HALF,))


def _pad_shard(n, t):
    if n == "w_in":
        return jnp.pad(t, ((0, 0), (0, W_IN_SLOT - W_IN_SHARD)))
    if n == "ffn_w_up":
        return _up_shard_pad(t)
    if n == "ffn_w_down":
        return jnp.pad(t, ((0, FF_HALF_P - FF_HALF), (0, 0)))
    return t


def _unpad_shard(n, t):
    if n == "w_in":
        return t[:, :W_IN_SHARD]
    if n == "ffn_w_up":
        return _up_shard_unpad(t)
    if n == "ffn_w_down":
        return t[:FF_HALF]
    return t


def _gather_layer(state, layer):
    shards = [_pad_shard(n, state[n][layer].astype(BF16)) for n in BIG]
    shards.append(_flat_pad([state[n][layer] for n in SMALL], SMALL_ROWS))
    *big, small = _all_gather(shards, name="all_gather_weights")
    W = dict(zip(BIG, big))
    for n in ("w_o", "xa_wq", "xa_wo", "ffn_w_down"):
        W[n] = W[n].reshape(-1, D_MODEL)
    W["w_in_p"] = _colmap(W.pop("w_in"), inverse=False, name="w_in_colmap")
    W.update(_unpack_small(small))
    return W


def _ssm_params(p):
    prep1_in = (p["ssm_lambda_re"], p["ssm_lambda_im"], p["ssm_log_dt"][:, None])
    ar, ai, gr, gi = _whole(_ssm_prep1, *prep1_in, name="ssm_prep1")
    to_cn = lambda b: b.transpose(2, 0, 1).reshape(SSM_GROUP, N_SSM_CH)
    prep2_in = (gr.reshape(1, N_SSM_CH), gi.reshape(1, N_SSM_CH), to_cn(p["ssm_b_re"]), to_cn(p["ssm_b_im"]))
    bbr, bbi = _whole(_ssm_prep2, *prep2_in, name="ssm_prep2")
    to_gcp = lambda t: t.reshape(SSM_GROUP, SSM_GROUPS, SSM_STATE).transpose(1, 0, 2)
    bb = _blockdiag(to_gcp(bbr), to_gcp(bbi))
    cct = _blockdiag(p["ssm_c_re"], -p["ssm_c_im"])
    a_vec = _to_blocked(ar.reshape(1, N_SSM_CH), ai.reshape(1, N_SSM_CH))
    return dict(bb=bb, cct=cct, a_vec=a_vec, prep1_in=prep1_in, prep2_in=prep2_in)


def _layer_fwd(x, mem, p, W):
    sp = _ssm_params(p)
    z = _mm(x, W["w_in_p"], bias=_pad_b_in(p["b_in"][None, :]), name="mm_in")
    zh = lambda off: _heads(z[:, off:off + 512], 8)
    f_t = z[:, Z_FF:Z_FF + FOX_HEADS].T
    cum = _cum_heads(f_t, name="fox_cum")
    ya_h, lse_a = _attn_fwd("fox", zh(Z_FQ), zh(Z_FK), zh(Z_FV), (cum[:, :, None], cum[:, None, :]), name="fox_fwd")
    ya_pre = _unheads(ya_h)
    ya = _mm(ya_pre, W["w_fox_o"], b_slots=True, name="mm_fox_o")
    x_ri = _mm(z, sp["bb"], a_off=Z_SU, a_cols=SSM_WIDTH, name="mm_s5_in")
    h_ri = _scan(x_ri, sp["a_vec"], name="s5_scan")
    hc = _mm(h_ri, sp["cct"], tb=True, name="mm_s5_out")
    d_row = p["ssm_d"][None, :]
    (gel,) = _rowwise(lambda a, b, c: (_s5_tail(a, b, c),), [hc, Win(z, 512, Z_SU // 512)], [d_row], name="s5_tail")
    yb2 = _mm(gel, W["w_ssm_glu"], b_slots=True, name="mm_glu")
    bias = _relbias_expand(W["ca_rel_bias"]).transpose(1, 0, 2)
    padk = lambda t: jnp.pad(t, ((0, 0), (CA_PAD, 0), (0, 0)))
    yc_h, lse_c = _attn_fwd("chunk", zh(Z_CQ), padk(zh(Z_CK)), padk(zh(Z_CV)), (bias,), name="chunk_fwd")
    yc_pre = _unheads(yc_h)
    yc = _mm(yc_pre, W["w_ca_o"], b_slots=True, name="mm_ca_o")
    gates = [Win(z, 1024, Z_GF // 1024), Win(z, 1024, Z_GS // 1024), Win(z, 1024, Z_GC // 1024)]
    (merged,) = _rowwise(lambda *a: (_merge(*a),), gates + [ya, yb2, yc], name="merge")
    h1 = _mm(merged, W["w_o"], name="mm_o")
    ln_g, ln_b = W["ln_g"], W["ln_b"]
    r1, x1 = _rowwise(_ln_fwd, [x, h1], [ln_g[0:1], ln_b[0:1]], name="ln_fwd")
    q = _mm(x1, W["xa_wq"], name="mm_xq")
    kv = _mm(mem, W["xa_wkv"], b_slots=True, name="mm_xkv")
    qh, kh, vh = _heads(q, XA_HEADS), _heads(kv[:, :D_MODEL], XA_HEADS), _heads(kv[:, D_MODEL:], XA_HEADS)
    o_h, lse_x = _attn_fwd("xa", qh, kh, vh, name="xa_fwd")
    o = _unheads(o_h)
    h2 = _mm(o, W["xa_wo"], name="mm_xo")
    r2, x2 = _rowwise(_ln_fwd, [x1, h2], [ln_g[1:2], ln_b[1:2]], name="ln_fwd")
    up = _mm(x2, W["ffn_w_up"], b_slots=True, name="mm_up")
    hmid = _ffn_mid(up, _ff_pad(W["ffn_conv_w"]), _ff_pad(p["ffn_conv_b"][None, :]), name="ffn_mid")
    h3 = _mm(hmid, W["ffn_w_down"], name="mm_down")
    r3, x3 = _rowwise(_ln_fwd, [x2, h3], [ln_g[2:3], ln_b[2:3]], name="ln_fwd")
    res = dict(x=x, z=z, cum=cum, lse_a=lse_a, ya_pre=ya_pre, ya=ya, h_ri=h_ri, hc=hc, gel=gel, yb2=yb2, lse_c=lse_c,
               yc_pre=yc_pre, yc=yc, merged=merged, r1=r1, x1=x1, q=q, kv=kv, o=o, lse_x=lse_x, r2=r2, x2=x2, up=up,
               hmid=hmid, r3=r3, W=W)
    return x3, res


def _layer_bwd(dx3, mem, p, res):
    W = res["W"]
    x, z = res["x"], res["z"]
    sp = _ssm_params(p)
    ln_g = W["ln_g"]
    big, small = {}, {}
    slots = lambda t: t.reshape(N_DEV, -1, D_MODEL)
    dr3, dg2, db2 = _rowwise(_ln_bwd, [res["r3"], dx3], [ln_g[2:3]], n_red=2, name="ln_bwd")
    dhmid = _mm(dr3, W["ffn_w_down"], tb=True, name="mm_down_dx")
    big["ffn_w_down"] = slots(_mm(res["hmid"], dr3, ta=True, name="mm_down_dw"))
    conv_w_p, conv_b_p = _ff_pad(W["ffn_conv_w"]), _ff_pad(p["ffn_conv_b"][None, :])
    dup_a, dup_g, dcw, dcb = _ffn_mid(res["up"], conv_w_p, conv_b_p, dhmid, name="ffn_mid_bwd")
    dup = jnp.concatenate([dup_a, dup_g], axis=1)
    small["ffn_conv_w"], small["ffn_conv_b"] = _ff_unpad(dcw), _ff_unpad(dcb)[0]
    dx2 = _mm(dup, W["ffn_w_up"], tb=True, b_slots=True, add=(dr3, DN_ALPHA), name="mm_up_dx")
    big["ffn_w_up"] = _mm(res["x2"], dup, ta=True, out_slots=2 * FF_HALF_P, name="mm_up_dw")
    dr2, dg1, db1 = _rowwise(_ln_bwd, [res["r2"], dx2], [ln_g[1:2]], n_red=2, name="ln_bwd")
    do = _mm(dr2, W["xa_wo"], tb=True, name="mm_xo_dx")
    big["xa_wo"] = slots(_mm(res["o"], dr2, ta=True, name="mm_xo_dw"))
    kv = res["kv"]
    qh, kh, vh = _heads(res["q"], XA_HEADS), _heads(kv[:, :D_MODEL], XA_HEADS), _heads(kv[:, D_MODEL:], XA_HEADS)
    dqh, dkh, dvh = _attn_bwd("xa", qh, kh, vh, res["lse_x"], _heads(do, XA_HEADS), name="xa_bwd")
    dq = _unheads(dqh)
    dkv = jnp.concatenate([_unheads(dkh), _unheads(dvh)], axis=1)
    dx1 = _mm(dq, W["xa_wq"], tb=True, add=(dr2, DN_ALPHA), name="mm_xq_dx")
    big["xa_wq"] = slots(_mm(res["x1"], dq, ta=True, name="mm_xq_dw"))
    big["xa_wkv"] = _mm(mem, dkv, ta=True, out_slots=256, name="mm_xkv_dw")
    dr1, dg0, db0 = _rowwise(_ln_bwd, [res["r1"], dx1], [ln_g[0:1]], n_red=2, name="ln_bwd")
    small["ln_g"] = jnp.concatenate([dg0, dg1, dg2], axis=0)
    small["ln_b"] = jnp.concatenate([db0, db1, db2], axis=0)
    dmerged = _mm(dr1, W["w_o"], tb=True, name="mm_o_dx")
    big["w_o"] = slots(_mm(res["merged"], dr1, ta=True, name="mm_o_dw"))
    gates = [Win(z, 1024, Z_GF // 1024), Win(z, 1024, Z_GS // 1024), Win(z, 1024, Z_GC // 1024)]
    dgf, dgs, dgc, dya, dyb2, dyc = _rowwise(_vjp_of(_merge, 6), gates + [res["ya"], res["yb2"], res["yc"], dmerged],
                                             name="merge_bwd")
    zh = lambda off: _heads(z[:, off:off + 512], 8)
    dya_pre = _mm(dya, W["w_fox_o"], tb=True, b_slots=True, name="mm_fox_o_dx")
    big["w_fox_o"] = _mm(res["ya_pre"], dya, ta=True, out_slots=128, name="mm_fox_o_dw")
    cum = res["cum"]
    dfq, dfk, dfv, dcq, dck = _attn_bwd("fox", zh(Z_FQ), zh(Z_FK), zh(Z_FV), res["lse_a"], _heads(dya_pre, 8),
                                        (cum[:, :, None], cum[:, None, :]), name="fox_bwd")
    f_t = z[:, Z_FF:Z_FF + FOX_HEADS].T
    dff = _cum_heads(f_t, dcq[:, :, 0] + dck[:, 0, :], name="fox_cum_bwd")
    dgel = _mm(dyb2, W["w_ssm_glu"], tb=True, b_slots=True, name="mm_glu_dx")
    big["w_ssm_glu"] = _mm(res["gel"], dyb2, ta=True, out_slots=256, name="mm_glu_dw")
    d_row = p["ssm_d"][None, :]
    su_win = Win(z, 512, Z_SU // 512)
    dy, dsu1, dd = _rowwise(_s5_tail_bwd, [res["hc"], su_win, dgel], [d_row], n_red=1, name="s5_tail_bwd")
    small["ssm_d"] = dd[0]
    dh_ri = _mm(dy, sp["cct"], name="mm_s5_out_dx")
    dcct = _mm(dy, res["h_ri"], ta=True, name="mm_s5_out_dw")
    dx_ri, da_vec = _scan(dh_ri, sp["a_vec"], res["h_ri"], name="s5_scan_bwd")
    dsu = _mm(dx_ri, sp["bb"], tb=True, add=(dsu1, 1.0), name="mm_s5_in_dx")
    dbb = _mm(z, dx_ri, ta=True, a_off=Z_SU, a_cols=SSM_WIDTH, name="mm_s5_in_dw")
    dcr, dci = _blockdiag_inv(dcct)
    small["ssm_c_re"], small["ssm_c_im"] = dcr, -dci
    dbbr, dbbi = _blockdiag_inv(dbb)
    to_cn = lambda t: t.transpose(1, 0, 2).reshape(SSM_GROUP, N_SSM_CH)
    dgr, dgi, dbr, dbi = _whole(_vjp_of(_ssm_prep2, 4), *sp["prep2_in"], to_cn(dbbr), to_cn(dbbi), name="ssm_prep2_bwd")
    from_cn = lambda t: t.reshape(SSM_GROUP, SSM_GROUPS, SSM_STATE).transpose(1, 2, 0)
    small["ssm_b_re"], small["ssm_b_im"] = from_cn(dbr), from_cn(dbi)
    dar, dai = _from_blocked(da_vec)
    sq = lambda t: t.reshape(SSM_GROUPS, SSM_STATE)
    dlr, dli, dldt = _whole(_vjp_of(_ssm_prep1, 3), *sp["prep1_in"], sq(dar), sq(dai), sq(dgr), sq(dgi),
                            name="ssm_prep1_bwd")
    small["ssm_lambda_re"], small["ssm_lambda_im"], small["ssm_log_dt"] = dlr, dli, dldt[:, 0]
    dyc_pre = _mm(dyc, W["w_ca_o"], tb=True, b_slots=True, name="mm_ca_o_dx")
    big["w_ca_o"] = _mm(res["yc_pre"], dyc, ta=True, out_slots=128, name="mm_ca_o_dw")
    bias = _relbias_expand(W["ca_rel_bias"]).transpose(1, 0, 2)
    padk = lambda t: jnp.pad(t, ((0, 0), (CA_PAD, 0), (0, 0)))
    dcqh, dckh, dcvh, dbias = _attn_bwd("chunk", zh(Z_CQ), padk(zh(Z_CK)), padk(zh(Z_CV)), res["lse_c"],
                                        _heads(dyc_pre, 8), (bias,), name="chunk_bwd")
    small["ca_rel_bias"] = _relbias_reduce(dbias.transpose(1, 0, 2))
    dff_p = jnp.pad(dff.T, ((0, 0), (0, 512 - FOX_HEADS)))
    dz = jnp.concatenate([_unheads(dfq), _unheads(dfk), _unheads(dfv), dff_p, dsu, _unheads(dcqh),
                          _unheads(dckh[:, CA_PAD:]), _unheads(dcvh[:, CA_PAD:]), dgf, dgs, dgc], axis=1)
    dx = _mm(dz, W["w_in_p"], tb=True, add=(dr1, DN_ALPHA), name="mm_in_dx")
    big["w_in"] = _colmap(_mm(x, dz, ta=True, name="mm_in_dw"), inverse=True, name="w_in_colmap_inv")
    (db_in_p,) = _rowwise(lambda t: (jnp.sum(t, axis=0, keepdims=True),), [dz], n_red=1, name="colsum")
    small["b_in"] = _unpad_b_in(db_in_p)[0]
    return dx, big, small


def kernel(x, mem, w_in, b_in, ssm_lambda_re, ssm_lambda_im, ssm_log_dt, ssm_b_re, ssm_b_im, ssm_c_re, ssm_c_im, ssm_d, ca_rel_bias, w_fox_o, w_ssm_glu, w_ca_o, w_o, xa_wq, xa_wkv, xa_wo, ffn_w_up, ffn_conv_w, ffn_conv_b, ffn_w_down, ln_g, ln_b, loss_target, m_w_in, m_b_in, m_ssm_lambda_re, m_ssm_lambda_im, m_ssm_log_dt, m_ssm_b_re, m_ssm_b_im, m_ssm_c_re, m_ssm_c_im, m_ssm_d, m_ca_rel_bias, m_w_fox_o, m_w_ssm_glu, m_w_ca_o, m_w_o, m_xa_wq, m_xa_wkv, m_xa_wo, m_ffn_w_up, m_ffn_conv_w, m_ffn_conv_b, m_ffn_w_down, m_ln_g, m_ln_b, v_w_in, v_b_in, v_ssm_lambda_re, v_ssm_lambda_im, v_ssm_log_dt, v_ssm_b_re, v_ssm_b_im, v_ssm_c_re, v_ssm_c_im, v_ssm_d, v_ca_rel_bias, v_w_fox_o, v_w_ssm_glu, v_w_ca_o, v_w_o, v_xa_wq, v_xa_wkv, v_xa_wo, v_ffn_w_up, v_ffn_conv_w, v_ffn_conv_b, v_ffn_w_down, v_ln_g, v_ln_b):
    given = dict(locals())
    state = {pre + n: given[pre + n] for n in WEIGHTS for pre in ("", "m_", "v_")}
    mem0 = mem[0]
    layer_params = [{n: state[n][l] for n in REPL} for l in range(DEPTH)]

    h, residuals = x[0], []
    for l in range(DEPTH):
        h, res = _layer_fwd(h, mem0, layer_params[l], _gather_layer(state, l))
        residuals.append(res)
    dh, loss_cols = _rowwise(_loss_rows, [h, loss_target[0]], n_red=1, name="loss")
    loss = lax.psum(jnp.sum(loss_cols), ("x", "y", "c"))

    outs = [None] * DEPTH
    for l in reversed(range(DEPTH)):
        dh, big, small = _layer_bwd(dh, mem0, layer_params[l], residuals[l])
        *recv_big, recv_small = _exchange_slots([big[n] for n in BIG] + [_pack_small_grads(small)],
                                                name="exchange_grads")
        layer_out = {}
        for n, recv in zip(BIG, recv_big):
            if recv.shape[1:] == BIG[n][0]:
                res4 = _adamw(recv, state[n], state["m_" + n], state["v_" + n], l, name="adamw_" + n)
            else:
                padded = [_pad_shard(n, state[pre + n][l]) for pre in ("", "m_", "v_")]
                res4 = [_unpad_shard(n, t) for t in _adamw(recv, *padded, name="adamw_" + n)]
            layer_out[n] = res4
        packed = _adamw(recv_small, *[_pack_state(state, pre, l) for pre in ("", "m_", "v_")], name="adamw_small")
        for n, t4 in zip((*SMALL, *REPL), zip(*[_unpack_state(t).values() for t in packed])):
            layer_out[n] = t4
        outs[l] = layer_out

    stacked = lambda n, j: jnp.stack([outs[l][n][j] for l in range(DEPTH)])
    return (loss, dh[None], *[stacked(n, j) for j in range(4) for n in WEIGHTS])
```

```python
import functools
import math

import jax
import jax.numpy as jnp
from jax import lax
from jax.experimental import pallas as pl
from jax.experimental.pallas import tpu as pltpu

F32, BF16 = jnp.float32, jnp.bfloat16

D_MODEL = 1024
DEPTH = 4
CHUNK = 64
FOX_HEADS, FOX_HEAD_DIM, FOX_WIDTH = 8, 64, 512
SSM_GROUP, SSM_WIDTH, SSM_GROUPS, SSM_STATE = 16, 512, 32, 64
CA_HEADS, CA_HEAD_DIM, CA_WIDTH, CA_LEFT_CHUNKS = 8, 64, 512, 8
CA_BAND = (CA_LEFT_CHUNKS + 1) * CHUNK
CA_PAD = CA_LEFT_CHUNKS * CHUNK
REL_MIN, REL_MAX = -(CHUNK - 1), 4 * CHUNK
N_REL = REL_MAX - REL_MIN + 1
XA_HEADS, XA_HEAD_DIM = 4, 256
D_FF = 2816
DN_ALPHA = (2 * DEPTH) ** 0.25
LN_EPS = 1e-5
NEG_INF = -1e30
ADAM_LR, ADAM_B1, ADAM_B2, ADAM_EPS, ADAM_WD, ADAM_STEP = 0.001, 0.9, 0.999, 1e-08, 0.01, 10

N_DEV = 8
LANE = 128
N_SSM_CH = SSM_GROUPS * SSM_STATE
SCAN_CB = 256
N_IN = 6664
W_IN_SHARD, W_IN_SLOT = N_IN // N_DEV, 896
Z_W = 7168
Z_FQ, Z_FK, Z_FV, Z_FF, Z_SU, Z_CQ, Z_CK, Z_CV, Z_GF, Z_GS, Z_GC = (
    0, 512, 1024, 1536, 2048, 2560, 3072, 3584, 4096, 5120, 6144)
Z_PIECES = ((0, 512, Z_FQ), (512, 512, Z_FK), (1024, 512, Z_FV), (1536, 8, Z_FF), (1544, 512, Z_SU),
            (2056, 512, Z_CQ), (2568, 512, Z_CK), (3080, 512, Z_CV), (3592, 1024, Z_GF), (4616, 1024, Z_GS),
            (5640, 1024, Z_GC))
FF_HALF, FF_HALF_P = D_FF // N_DEV, 384
D_FF_P = N_DEV * FF_HALF_P

VMEM_LIMIT_BYTES = 56 * 1024 * 1024

BIG = {"w_in": ((1024, W_IN_SHARD), 1), "w_fox_o": ((512, 128), 1), "w_ssm_glu": ((512, 256), 1),
       "w_ca_o": ((512, 128), 1), "w_o": ((128, 1024), 0), "xa_wq": ((128, 1024), 0), "xa_wkv": ((1024, 256), 1),
       "xa_wo": ((128, 1024), 0), "ffn_w_up": ((1024, 2 * FF_HALF), 1), "ffn_w_down": ((FF_HALF, 1024), 0)}
SMALL = {"ca_rel_bias": (8, 320), "ffn_conv_w": (3, D_FF), "ln_g": (3, 1024), "ln_b": (3, 1024)}
REPL = {"b_in": (N_IN,), "ssm_lambda_re": (32, 64), "ssm_lambda_im": (32, 64), "ssm_log_dt": (32,),
        "ssm_b_re": (32, 64, 16), "ssm_b_im": (32, 64, 16), "ssm_c_re": (32, 16, 64), "ssm_c_im": (32, 16, 64),
        "ssm_d": (512,), "ffn_conv_b": (D_FF,)}
WEIGHTS = ("w_in", "b_in", "ssm_lambda_re", "ssm_lambda_im", "ssm_log_dt", "ssm_b_re", "ssm_b_im", "ssm_c_re",
           "ssm_c_im", "ssm_d", "ca_rel_bias", "w_fox_o", "w_ssm_glu", "w_ca_o", "w_o", "xa_wq", "xa_wkv", "xa_wo",
           "ffn_w_up", "ffn_conv_w", "ffn_conv_b", "ffn_w_down", "ln_g", "ln_b")
LANES = 1024
PACK_ROWS = 256
SMALL_ROWS = 8


def _w_in_segments():
    segs = []
    for src, width, dst in Z_PIECES:
        n = src
        while n < src + width:
            d = n // W_IN_SHARD
            end = min(src + width, (d + 1) * W_IN_SHARD)
            segs.append((W_IN_SLOT * d + n - W_IN_SHARD * d, dst + n - src, end - n))
            n = end
    return tuple(segs)


W_IN_SEGS = _w_in_segments()


def _pcall(body, **kw):
    return pl.pallas_call(body, **kw)


def _params(sem):
    return pltpu.CompilerParams(dimension_semantics=sem, vmem_limit_bytes=VMEM_LIMIT_BYTES)


def _pick(dim, prefs):
    for p in prefs:
        if dim % p == 0:
            return p
    return dim


def _mm(a, b, *, ta=False, tb=False, bias=None, add=None, a_off=0, a_cols=None, b_slots=False, out_slots=None,
        name, out_dtype=F32):
    a_cols = a_cols if a_cols is not None else a.shape[1]
    M, K = (a_cols, a.shape[0]) if ta else (a.shape[0], a_cols)
    tm = _pick(M, (1024, 512, 256, 128))
    if ta:
        tk = _pick(K, (1024, 512, 256))
    elif b_slots and tb:
        tk = _pick(b.shape[2], (1024, 768, 512, 256, 128))
    else:
        tk = K if K <= 3072 else _pick(K, (1024, 512, 256, 128))
    nk = K // tk
    if b_slots:
        ns = b.shape[2]
        if tb:
            N = b.shape[1]
            tn = _pick(N, (512, 256, 128))
            per = ns // tk
            b_spec = pl.BlockSpec((None, tn, tk), lambda i, j, k: (k // per, j, k % per))
            b_dim = 1
            assert N_DEV * ns == K
        else:
            N = N_DEV * ns
            tn = _pick(ns, (512, 256, 128))
            per = ns // tn
            b_spec = pl.BlockSpec((None, tk, tn), lambda i, j, k: (j // per, k, j % per))
            b_dim = 0
            assert b.shape[1] == K
    else:
        N = b.shape[0] if tb else b.shape[1]
        assert (b.shape[1] if tb else b.shape[0]) == K, (a.shape, b.shape, ta, tb)
        tn = _pick(N if out_slots is None else out_slots, (512, 256, 128))
        if tb:
            b_spec = pl.BlockSpec((tn, tk), lambda i, j, k: (j, k))
            b_dim = 1
        else:
            b_spec = pl.BlockSpec((tk, tn), lambda i, j, k: (k, j))
            b_dim = 0
    if ta:
        assert a_off % tm == 0
        a_spec = pl.BlockSpec((tk, tm), lambda i, j, k: (k, i + a_off // tm))
        a_dim = 0
    else:
        assert a_off % tk == 0
        a_spec = pl.BlockSpec((tm, tk), lambda i, j, k: (i, k + a_off // tk))
        a_dim = 1
    dims = (((a_dim,), (b_dim,)), ((), ()))
    ins, specs = [a, b], [a_spec, b_spec]
    if bias is not None:
        ins.append(bias)
        specs.append(pl.BlockSpec((1, tn), lambda i, j, k: (0, j)))
    add_scale = None
    if add is not None:
        ins.append(add[0])
        add_scale = add[1]
        specs.append(pl.BlockSpec((tm, tn), lambda i, j, k: (i, j)))
    if out_slots is None:
        out_spec = pl.BlockSpec((tm, tn), lambda i, j, k: (i, j))
        out_shape = jax.ShapeDtypeStruct((M, N), out_dtype)
    else:
        assert N == N_DEV * out_slots
        per_o = out_slots // tn
        out_spec = pl.BlockSpec((None, tm, tn), lambda i, j, k: (j // per_o, i, j % per_o))
        out_shape = jax.ShapeDtypeStruct((N_DEV, M, out_slots), out_dtype)

    def body(*refs):
        a_ref, b_ref = refs[0], refs[1]
        pos = 2
        bias_ref = add_ref = None
        if bias is not None:
            bias_ref = refs[pos]
            pos += 1
        if add is not None:
            add_ref = refs[pos]
            pos += 1
        o_ref = refs[pos]
        acc_ref = refs[pos + 1] if nk > 1 else None
        part = lax.dot_general(a_ref[...].astype(BF16), b_ref[...].astype(BF16), dims, preferred_element_type=F32)

        def finish(acc):
            if bias_ref is not None:
                acc = acc + bias_ref[...]
            if add_ref is not None:
                acc = acc + add_scale * add_ref[...]
            o_ref[...] = acc.astype(out_dtype)

        if nk == 1:
            finish(part)
        else:
            k = pl.program_id(2)

            @pl.when(k == 0)
            def _():
                acc_ref[...] = part

            @pl.when(k > 0)
            def _():
                acc_ref[...] += part

            @pl.when(k == nk - 1)
            def _():
                finish(acc_ref[...])

    return _pcall(
        body, name=name, grid=(M // tm, N // tn, nk), in_specs=specs, out_specs=out_spec, out_shape=out_shape,
        scratch_shapes=[pltpu.VMEM((tm, tn), F32)] if nk > 1 else [],
        compiler_params=_params(("parallel", "parallel", "arbitrary")),
    )(*ins)


class Win:
    def __init__(self, arr, width, blk):
        self.arr, self.width, self.blk = arr, width, blk


def _rowwise(fn, rows, vecs=(), *, n_red=0, tr=256, name):
    wins = [r if isinstance(r, Win) else Win(r, r.shape[1], 0) for r in rows]
    S = wins[0].arr.shape[0]
    tr = min(tr, S)
    tile_args = [jax.ShapeDtypeStruct((tr, w.width), w.arr.dtype) for w in wins]
    tile_args += [jax.ShapeDtypeStruct(v.shape, v.dtype) for v in vecs]
    outs = jax.eval_shape(fn, *tile_args)
    n_row = len(outs) - n_red
    specs = [pl.BlockSpec((tr, w.width), functools.partial(lambda i, b: (i, b), b=w.blk)) for w in wins]
    specs += [pl.BlockSpec(v.shape, functools.partial(lambda i, nd: (0,) * nd, nd=v.ndim)) for v in vecs]
    out_specs = [pl.BlockSpec((tr, o.shape[1]), lambda i: (i, 0)) for o in outs[:n_row]]
    out_specs += [pl.BlockSpec(o.shape, functools.partial(lambda i, nd: (0,) * nd, nd=len(o.shape))) for o in outs[n_row:]]
    out_shape = [jax.ShapeDtypeStruct((S, o.shape[1]), o.dtype) for o in outs[:n_row]]
    out_shape += [jax.ShapeDtypeStruct(o.shape, o.dtype) for o in outs[n_row:]]
    n_in = len(wins) + len(vecs)

    def body(*refs):
        res = fn(*[r[...] for r in refs[:n_in]])
        o_refs = refs[n_in:]
        for o_ref, r in zip(o_refs[:n_row], res[:n_row]):
            o_ref[...] = r.astype(o_ref.dtype)
        i = pl.program_id(0)
        for o_ref, r in zip(o_refs[n_row:], res[n_row:]):
            @pl.when(i == 0)
            def _(o_ref=o_ref, r=r):
                o_ref[...] = r

            @pl.when(i > 0)
            def _(o_ref=o_ref, r=r):
                o_ref[...] += r

    return _pcall(
        body, name=name, grid=(S // tr,), in_specs=specs, out_specs=out_specs, out_shape=out_shape,
        compiler_params=_params(("arbitrary",)),
    )(*[w.arr for w in wins], *vecs)


def _whole(fn, *arrays, name):
    outs = jax.eval_shape(fn, *arrays)
    n_in = len(arrays)

    def body(*refs):
        res = fn(*[r[...] for r in refs[:n_in]])
        for o_ref, r in zip(refs[n_in:], res):
            o_ref[...] = r

    vm = pl.BlockSpec(memory_space=pltpu.VMEM)
    return _pcall(body, name=name, in_specs=[vm] * n_in, out_specs=[vm] * len(outs),
                  out_shape=[jax.ShapeDtypeStruct(o.shape, o.dtype) for o in outs])(*arrays)


def _split3(x):
    hi = x.astype(BF16)
    r = x - hi.astype(F32)
    mid = r.astype(BF16)
    lo = (r - mid.astype(F32)).astype(BF16)
    return hi, mid, lo


def _dot3(x, onehot, dims):
    return sum(lax.dot_general(t, onehot, dims, preferred_element_type=F32) for t in _split3(x))


_NT = (((1,), (1,)), ((), ()))
_NN = (((1,), (0,)), ((), ()))
_TN = (((0,), (0,)), ((), ()))


def _colmap(x, *, inverse, name):
    R = x.shape[0] if inverse else x.shape[1]
    tr = 256
    per = W_IN_SLOT // LANE
    n_out = N_DEV * per if inverse else Z_W // LANE
    segs = [(p, q, n) for q, p, n in W_IN_SEGS] if inverse else list(W_IN_SEGS)

    def body(x_ref, o_ref):
        ia = lax.broadcasted_iota(jnp.int32, (LANE, LANE), 0)
        ib = lax.broadcasted_iota(jnp.int32, (LANE, LANE), 1)

        def src_block(i):
            if inverse:
                return x_ref[:, i * LANE:(i + 1) * LANE]
            return x_ref[i // per, :, (i % per) * LANE:(i % per + 1) * LANE]

        for jb in range(n_out):
            acc = None
            for s0, d0, n in segs:
                lo, hi = max(d0, jb * LANE), min(d0 + n, (jb + 1) * LANE)
                if lo >= hi:
                    continue
                delta = d0 - s0
                for i in range((lo - delta) // LANE, (hi - delta - 1) // LANE + 1):
                    shift = jb * LANE - i * LANE - delta
                    sel = ((ia - ib == shift) & (ib >= lo - jb * LANE) & (ib < hi - jb * LANE)).astype(BF16)
                    blk = src_block(i)
                    part = _dot3(blk, sel, _NN) if inverse else lax.dot_general(blk, sel, _NN, preferred_element_type=F32)
                    acc = part if acc is None else acc + part
            if acc is None:
                acc = jnp.zeros((tr, LANE), F32)
            if inverse:
                o_ref[jb // per, :, (jb % per) * LANE:(jb % per + 1) * LANE] = acc
            else:
                o_ref[:, jb * LANE:(jb + 1) * LANE] = acc.astype(BF16)

    slot_spec = pl.BlockSpec((N_DEV, tr, W_IN_SLOT), lambda i: (0, i, 0))
    flat_spec = pl.BlockSpec((tr, Z_W), lambda i: (i, 0))
    if inverse:
        return _pcall(body, name=name, grid=(R // tr,), in_specs=[flat_spec], out_specs=slot_spec,
                      out_shape=jax.ShapeDtypeStruct((N_DEV, R, W_IN_SLOT), F32), compiler_params=_params(("parallel",)))(x)
    return _pcall(body, name=name, grid=(R // tr,), in_specs=[slot_spec], out_specs=flat_spec,
                  out_shape=jax.ShapeDtypeStruct((R, Z_W), BF16), compiler_params=_params(("parallel",)))(x)


def _log_sigmoid(x):
    return jnp.minimum(x, 0.0) - jnp.log(1.0 + jnp.exp(-jnp.abs(x)))


def _cum_heads(f, dcum=None, *, name):
    H, S = f.shape
    tn = min(512, S)
    rev = dcum is not None

    def body(*refs):
        j = pl.program_id(0)
        s_idx = lax.broadcasted_iota(jnp.int32, (S, tn), 0)
        t_idx = lax.broadcasted_iota(jnp.int32, (S, tn), 1) + j * tn
        if not rev:
            f_ref, o_ref = refs
            tri = (s_idx <= t_idx).astype(BF16)
            o_ref[...] = _dot3(_log_sigmoid(f_ref[...]), tri, _NN)
        else:
            fj_ref, d_ref, o_ref = refs
            tri = (s_idx >= t_idx).astype(BF16)
            o_ref[...] = _dot3(d_ref[...], tri, _NN) * jax.nn.sigmoid(-fj_ref[...])

    full = pl.BlockSpec((H, S), lambda j: (0, 0))
    blk = pl.BlockSpec((H, tn), lambda j: (0, j))
    ins, specs = ([f], [full]) if not rev else ([f, dcum], [blk, full])
    return _pcall(body, name=name, grid=(S // tn,), in_specs=specs, out_specs=blk,
                  out_shape=jax.ShapeDtypeStruct((H, S), F32), compiler_params=_params(("arbitrary",)))(*ins)


def _rel_onehot(qi, transposed):
    shape = (N_REL, CA_BAND) if transposed else (CA_BAND, N_REL)
    kk = lax.broadcasted_iota(jnp.int32, shape, 1 if transposed else 0)
    rr = lax.broadcasted_iota(jnp.int32, shape, 0 if transposed else 1)
    idx = jnp.clip(CA_PAD + qi - kk, REL_MIN, REL_MAX) - REL_MIN
    return (idx == rr).astype(BF16)


def _relbias_expand(rb):
    def body(rb_ref, o_ref):
        o_ref[0] = _dot3(rb_ref[...], _rel_onehot(pl.program_id(0), True), _NN)

    return _pcall(body, name="relbias_expand", grid=(CHUNK,),
                  in_specs=[pl.BlockSpec((CA_HEADS, N_REL), lambda q: (0, 0))],
                  out_specs=pl.BlockSpec((1, CA_HEADS, CA_BAND), lambda q: (q, 0, 0)),
                  out_shape=jax.ShapeDtypeStruct((CHUNK, CA_HEADS, CA_BAND), F32),
                  compiler_params=_params(("arbitrary",)))(rb)


def _relbias_reduce(db):
    def body(db_ref, o_ref):
        q = pl.program_id(0)
        part = _dot3(db_ref[0], _rel_onehot(q, False), _NN)

        @pl.when(q == 0)
        def _():
            o_ref[...] = part

        @pl.when(q > 0)
        def _():
            o_ref[...] += part

    return _pcall(body, name="relbias_reduce", grid=(CHUNK,),
                  in_specs=[pl.BlockSpec((1, CA_HEADS, CA_BAND), lambda q: (q, 0, 0))],
                  out_specs=pl.BlockSpec((CA_HEADS, N_REL), lambda q: (0, 0)),
                  out_shape=jax.ShapeDtypeStruct((CA_HEADS, N_REL), F32),
                  compiler_params=_params(("arbitrary",)))(db)


def _attn_cfg(mode, S):
    if mode == "fox":
        return min(256, S), FOX_HEAD_DIM ** -0.5
    if mode == "chunk":
        return CHUNK, CA_HEAD_DIM ** -0.5
    return min(512, S), XA_HEAD_DIM ** -0.5


def _scores(mode, i, tq, scale, qb, kb, extra):
    s = lax.dot_general(qb, kb, _NT, preferred_element_type=F32) * scale
    nk = kb.shape[0]
    if mode == "fox":
        cq, ck = extra
        s = s + cq - ck
        row = lax.broadcasted_iota(jnp.int32, (tq, nk), 0) + i * tq
        col = lax.broadcasted_iota(jnp.int32, (tq, nk), 1)
        s = jnp.where(row >= col, s, NEG_INF)
    elif mode == "chunk":
        (bias,) = extra
        col = lax.broadcasted_iota(jnp.int32, (tq, nk), 1) + i * CHUNK
        s = jnp.where(col >= CA_PAD, s + bias, NEG_INF)
    return s


def _attn_specs(mode, q, k, tq):
    H, S, D = q.shape
    Sk = k.shape[1]
    q_spec = pl.BlockSpec((1, tq, D), lambda h, i: (h, i, 0))
    kv_spec = pl.BlockSpec((1, Sk, D), lambda h, i: (h, 0, 0))
    if mode == "fox":
        ex = [pl.BlockSpec((1, tq, 1), lambda h, i: (h, i, 0)), pl.BlockSpec((1, 1, Sk), lambda h, i: (h, 0, 0))]
    elif mode == "chunk":
        ex = [pl.BlockSpec((1, CHUNK, CA_BAND), lambda h, i: (h, 0, 0))]
    else:
        ex = []
    col_spec = pl.BlockSpec((1, tq, 1), lambda h, i: (h, i, 0))
    return q_spec, kv_spec, ex, col_spec


def _kv_window(mode, i, ref):
    if mode == "chunk":
        return ref[0, pl.ds(pl.multiple_of(i * CHUNK, CHUNK), CA_BAND), :]
    return ref[0]


def _attn_fwd(mode, q, k, v, extra=(), *, name):
    H, S, D = q.shape
    tq, scale = _attn_cfg(mode, S)
    q_spec, kv_spec, ex_specs, col_spec = _attn_specs(mode, q, k, tq)
    n_ex = len(extra)

    def body(*refs):
        q_ref, k_ref, v_ref = refs[:3]
        ex = [r[0] for r in refs[3:3 + n_ex]]
        o_ref, lse_ref = refs[3 + n_ex:]
        i = pl.program_id(1)
        kb = _kv_window(mode, i, k_ref).astype(BF16)
        vb = _kv_window(mode, i, v_ref).astype(BF16)
        s = _scores(mode, i, tq, scale, q_ref[0].astype(BF16), kb, ex)
        m = jnp.max(s, axis=1, keepdims=True)
        e = jnp.exp(s - m)
        l = jnp.sum(e, axis=1, keepdims=True)
        p = e / l
        o_ref[0] = lax.dot_general(p.astype(BF16), vb, _NN, preferred_element_type=F32)
        lse_ref[0] = m + jnp.log(l)

    return _pcall(
        body, name=name, grid=(H, S // tq), in_specs=[q_spec, kv_spec, kv_spec] + ex_specs,
        out_specs=[q_spec, col_spec],
        out_shape=[jax.ShapeDtypeStruct((H, S, D), F32), jax.ShapeDtypeStruct((H, S, 1), F32)],
        compiler_params=_params(("parallel", "arbitrary")),
    )(q, k, v, *extra)


def _attn_bwd(mode, q, k, v, lse, do, extra=(), *, name):
    H, S, D = q.shape
    Sk = k.shape[1]
    tq, scale = _attn_cfg(mode, S)
    q_spec, kv_spec, ex_specs, col_spec = _attn_specs(mode, q, k, tq)
    n_ex = len(extra)
    out_specs = [q_spec, kv_spec, kv_spec]
    out_shape = [jax.ShapeDtypeStruct((H, S, D), F32), jax.ShapeDtypeStruct((H, Sk, D), F32),
                 jax.ShapeDtypeStruct((H, Sk, D), F32)]
    if mode == "fox":
        out_specs += [col_spec, ex_specs[1]]
        out_shape += [jax.ShapeDtypeStruct((H, S, 1), F32), jax.ShapeDtypeStruct((H, 1, Sk), F32)]
    elif mode == "chunk":
        out_specs += [ex_specs[0]]
        out_shape += [jax.ShapeDtypeStruct((H, CHUNK, CA_BAND), F32)]

    def body(*refs):
        q_ref, k_ref, v_ref, lse_ref, do_ref = refs[:5]
        ex = [r[0] for r in refs[5:5 + n_ex]]
        dq_ref, dk_ref, dv_ref = refs[5 + n_ex:8 + n_ex]
        rest = refs[8 + n_ex:]
        i = pl.program_id(1)
        qb = q_ref[0].astype(BF16)
        kb = _kv_window(mode, i, k_ref).astype(BF16)
        vb = _kv_window(mode, i, v_ref).astype(BF16)
        dob = do_ref[0].astype(BF16)
        s = _scores(mode, i, tq, scale, qb, kb, ex)
        p = jnp.exp(s - lse_ref[0])
        dp = lax.dot_general(dob, vb, _NT, preferred_element_type=F32)
        ds = p * (dp - jnp.sum(dp * p, axis=1, keepdims=True))
        dsb = (ds * scale).astype(BF16)
        dq_ref[0] = lax.dot_general(dsb, kb, _NN, preferred_element_type=F32)
        dk_part = lax.dot_general(dsb, qb, _TN, preferred_element_type=F32)
        dv_part = lax.dot_general(p.astype(BF16), dob, _TN, preferred_element_type=F32)

        @pl.when(i == 0)
        def _():
            dk_ref[...] = jnp.zeros_like(dk_ref)
            dv_ref[...] = jnp.zeros_like(dv_ref)
            if mode == "fox":
                rest[1][...] = jnp.zeros_like(rest[1])
            elif mode == "chunk":
                rest[0][...] = jnp.zeros_like(rest[0])

        if mode == "chunk":
            win = pl.ds(pl.multiple_of(i * CHUNK, CHUNK), CA_BAND)
            dk_ref[0, win, :] += dk_part
            dv_ref[0, win, :] += dv_part
            rest[0][0] += ds
        else:
            dk_ref[0] += dk_part
            dv_ref[0] += dv_part
        if mode == "fox":
            rest[0][0] = jnp.sum(ds, axis=1, keepdims=True)
            rest[1][0] += -jnp.sum(ds, axis=0, keepdims=True)

    return _pcall(
        body, name=name, grid=(H, S // tq),
        in_specs=[q_spec, kv_spec, kv_spec, col_spec, q_spec] + ex_specs,
        out_specs=out_specs, out_shape=out_shape,
        compiler_params=_params(("parallel", "arbitrary")),
    )(q, k, v, lse, do, *extra)


def _scan(x, a, h=None, *, name):
    S = x.shape[0]
    CB = SCAN_CB
    rev = h is not None
    n_grp = S // 8

    def body(*refs):
        if rev:
            x_ref, a_ref, h_ref, o_ref, da_ref = refs
        else:
            x_ref, a_ref, o_ref = refs
        ar = a_ref[:, :CB]
        ai = -a_ref[:, CB:] if rev else a_ref[:, CB:]
        zero = jnp.zeros((1, CB), F32)

        def group(g, carry):
            base = pl.multiple_of((n_grp - 1 - g) * 8 if rev else g * 8, 8)
            for j in (range(7, -1, -1) if rev else range(8)):
                t = base + j
                if rev:
                    hr, hi, dar, dai = carry
                else:
                    hr, hi = carry
                xr = x_ref[pl.ds(t, 1), :CB]
                xi = x_ref[pl.ds(t, 1), CB:]
                hr, hi = ar * hr - ai * hi + xr, ar * hi + ai * hr + xi
                o_ref[pl.ds(t, 1), :CB] = hr
                o_ref[pl.ds(t, 1), CB:] = hi
                if rev:
                    tp = jnp.maximum(t - 1, 0)
                    live = (t > 0).astype(F32)
                    pr = h_ref[pl.ds(tp, 1), :CB] * live
                    pi = h_ref[pl.ds(tp, 1), CB:] * live
                    carry = (hr, hi, dar + hr * pr + hi * pi, dai + hi * pr - hr * pi)
                else:
                    carry = (hr, hi)
            return carry

        if rev:
            _, _, dar, dai = lax.fori_loop(0, n_grp, group, (zero, zero, zero, zero))
            da_ref[:, :CB] = dar
            da_ref[:, CB:] = dai
        else:
            lax.fori_loop(0, n_grp, group, (zero, zero))

    big = pl.BlockSpec((S, 2 * CB), lambda c: (0, c))
    vec = pl.BlockSpec((1, 2 * CB), lambda c: (0, c))
    n_blk = x.shape[1] // (2 * CB)
    if rev:
        return _pcall(body, name=name, grid=(n_blk,), in_specs=[big, vec, big], out_specs=[big, vec],
                      out_shape=[jax.ShapeDtypeStruct(x.shape, F32), jax.ShapeDtypeStruct(a.shape, F32)],
                      compiler_params=_params(("parallel",)))(x, a, h)
    return _pcall(body, name=name, grid=(n_blk,), in_specs=[big, vec], out_specs=big,
                  out_shape=jax.ShapeDtypeStruct(x.shape, F32), compiler_params=_params(("parallel",)))(x, a)


def _ssm_prep1(lr_, li, ldt):
    lr = jnp.minimum(lr_, -1e-4)
    dt = jnp.exp(ldt)
    mag = jnp.exp(lr * dt)
    ar = mag * jnp.cos(li * dt)
    ai = mag * jnp.sin(li * dt)
    den = lr * lr + li * li
    gr = ((ar - 1.0) * lr + ai * li) / den
    gi = (ai * lr - (ar - 1.0) * li) / den
    return ar, ai, gr, gi


def _ssm_prep2(gr, gi, br, bi):
    return gr * br - gi * bi, gr * bi + gi * br


def _vjp_of(fn, n_in):
    def bwd(*args):
        cts = args[n_in:]
        return jax.vjp(fn, *args[:n_in])[1](cts[0] if len(cts) == 1 else tuple(cts))
    return bwd


def _to_blocked(r, i):
    lead = r.shape[:-1]
    t = jnp.stack([r.reshape(lead + (N_SSM_CH // SCAN_CB, SCAN_CB)), i.reshape(lead + (N_SSM_CH // SCAN_CB, SCAN_CB))],
                  axis=-2)
    return t.reshape(lead + (2 * N_SSM_CH,))


def _from_blocked(m):
    lead = m.shape[:-1]
    t = m.reshape(lead + (N_SSM_CH // SCAN_CB, 2, SCAN_CB))
    return t[..., 0, :].reshape(lead + (N_SSM_CH,)), t[..., 1, :].reshape(lead + (N_SSM_CH,))


def _blockdiag(r, i):
    eye = jnp.eye(SSM_GROUPS, dtype=F32)
    bd = lambda t: jnp.einsum("gcp,gh->gchp", t, eye).reshape(SSM_WIDTH, N_SSM_CH)
    return _to_blocked(bd(r), bd(i))


def _blockdiag_inv(m):
    eye = jnp.eye(SSM_GROUPS, dtype=F32)
    r, i = _from_blocked(m)
    diag = lambda t: jnp.einsum("gchp,gh->gcp", t.reshape(SSM_GROUPS, SSM_GROUP, SSM_GROUPS, SSM_STATE), eye)
    return diag(r), diag(i)


def _shift_rows(x, n):
    S = x.shape[0]
    row = lax.broadcasted_iota(jnp.int32, x.shape, 0)
    if n > 0:
        return jnp.where(row >= n, pltpu.roll(x, n, 0), 0.0)
    return jnp.where(row < S + n, pltpu.roll(x, S + n, 0), 0.0)


def _bf(x):
    return x.astype(BF16).astype(F32)


def _conv_pre(a, w, b):
    ab, wb = _bf(a), _bf(w)
    return wb[2:3] * ab + wb[1:2] * _shift_rows(ab, 1) + wb[0:1] * _shift_rows(ab, 2) + b


def _ffn_mid(up, conv_w, conv_b, dh=None, *, name):
    S = up.shape[0]
    tn = LANE
    nb = D_FF_P // tn
    rev = dh is not None

    def body(*refs):
        if not rev:
            a_ref, g_ref, w_ref, b_ref, o_ref = refs
            o_ref[...] = jax.nn.gelu(_conv_pre(a_ref[...], w_ref[...], b_ref[...])) * g_ref[...]
            return
        a_ref, g_ref, w_ref, b_ref, dh_ref, dup_a_ref, dup_g_ref, dw_ref, db_ref = refs
        a, w, dh_ = a_ref[...], w_ref[...], dh_ref[...]
        pre = _conv_pre(a, w, b_ref[...])
        gl, gelu_vjp = jax.vjp(jax.nn.gelu, pre)
        dup_g_ref[...] = dh_ * gl
        (dpre,) = gelu_vjp(dh_ * g_ref[...])
        db_ref[...] = jnp.sum(dpre, axis=0, keepdims=True)
        dpb, ab, wb = _bf(dpre), _bf(a), _bf(w)
        dup_a_ref[...] = wb[2:3] * dpb + wb[1:2] * _shift_rows(dpb, -1) + wb[0:1] * _shift_rows(dpb, -2)
        dw_ref[2:3, :] = jnp.sum(dpb * ab, axis=0, keepdims=True)
        dw_ref[1:2, :] = jnp.sum(dpb * _shift_rows(ab, 1), axis=0, keepdims=True)
        dw_ref[0:1, :] = jnp.sum(dpb * _shift_rows(ab, 2), axis=0, keepdims=True)

    a_spec = pl.BlockSpec((S, tn), lambda j: (0, j))
    g_spec = pl.BlockSpec((S, tn), lambda j: (0, j + nb))
    w_spec = pl.BlockSpec((3, tn), lambda j: (0, j))
    b_spec = pl.BlockSpec((1, tn), lambda j: (0, j))
    if not rev:
        return _pcall(body, name=name, grid=(nb,), in_specs=[a_spec, g_spec, w_spec, b_spec], out_specs=a_spec,
                      out_shape=jax.ShapeDtypeStruct((S, D_FF_P), F32), compiler_params=_params(("parallel",)))(
                          up, up, conv_w, conv_b)
    return _pcall(body, name=name, grid=(nb,), in_specs=[a_spec, g_spec, w_spec, b_spec, a_spec],
                  out_specs=[a_spec, a_spec, w_spec, b_spec],
                  out_shape=[jax.ShapeDtypeStruct((S, D_FF_P), F32), jax.ShapeDtypeStruct((S, D_FF_P), F32),
                             jax.ShapeDtypeStruct((3, D_FF_P), F32), jax.ShapeDtypeStruct((1, D_FF_P), F32)],
                  compiler_params=_params(("parallel",)))(up, up, conv_w, conv_b, dh)


def _ff_pad(t):
    lead = t.shape[:-1]
    t = t.reshape(lead + (N_DEV, FF_HALF))
    return jnp.pad(t, [(0, 0)] * len(lead) + [(0, 0), (0, FF_HALF_P - FF_HALF)]).reshape(lead + (D_FF_P,))


def _ff_unpad(t):
    lead = t.shape[:-1]
    return t.reshape(lead + (N_DEV, FF_HALF_P))[..., :FF_HALF].reshape(lead + (D_FF,))


def _ln_fwd(x, h, g, b):
    r = DN_ALPHA * x + h
    mu = jnp.mean(r, axis=-1, keepdims=True)
    var = jnp.mean(jnp.square(r - mu), axis=-1, keepdims=True)
    return r, (r - mu) * lax.rsqrt(var + LN_EPS) * g + b


def _ln_bwd(r, dy, g):
    mu = jnp.mean(r, axis=-1, keepdims=True)
    var = jnp.mean(jnp.square(r - mu), axis=-1, keepdims=True)
    xhat = (r - mu) * lax.rsqrt(var + LN_EPS)
    dxh = dy * g
    dr = lax.rsqrt(var + LN_EPS) * (dxh - jnp.mean(dxh, axis=-1, keepdims=True)
                                    - xhat * jnp.mean(dxh * xhat, axis=-1, keepdims=True))
    return dr, jnp.sum(dy * xhat, axis=0, keepdims=True), jnp.sum(dy, axis=0, keepdims=True)


def _merge(gf, gs, gc, ya, yb2, yc):
    yb = yb2[:, :D_MODEL] * jax.nn.sigmoid(yb2[:, D_MODEL:])
    return jax.nn.sigmoid(gf) * ya + jax.nn.sigmoid(gs) * yb + jax.nn.sigmoid(gc) * yc


def _s5_tail(hc, su, d):
    return jax.nn.gelu(hc + d * su)


def _s5_tail_bwd(hc, su, dgel, d):
    _, vjp = jax.vjp(jax.nn.gelu, hc + d * su)
    (dy,) = vjp(dgel)
    return dy, d * dy, jnp.sum(dy * su, axis=0, keepdims=True)


def _loss_rows(y, tgt):
    err = y - tgt
    return err * (1.0 / D_MODEL), jnp.sum(0.5 * jnp.square(err), axis=0, keepdims=True) * (1.0 / D_MODEL)


def _peer(k):
    x, y, c = lax.axis_index("x"), lax.axis_index("y"), lax.axis_index("c")
    return (x ^ ((k >> 2) & 1), y ^ ((k >> 1) & 1), c ^ (k & 1))


def _my_slot():
    return 4 * lax.axis_index("x") + 2 * lax.axis_index("y") + lax.axis_index("c")


def _peer_slot(k):
    px, py, pc = _peer(k)
    return 4 * px + 2 * py + pc


N_CHIP = N_DEV // 2
OTHER_CHIPS = (2, 4, 6)


def _chip_of(dev):
    return 2 * dev[0] + dev[1]


def _remote(src, dst, send, recv, dev):
    return pltpu.make_async_remote_copy(src_ref=src, dst_ref=dst, send_sem=send, recv_sem=recv, device_id=dev,
                                        device_id_type=pl.DeviceIdType.MESH)


def _all_gather(shards, *, name):
    n = len(shards)

    def body(*refs):
        ins, outs = refs[:n], refs[n:2 * n]
        send, recv, loc = refs[2 * n:]
        me = _my_slot()
        sibling = _peer(1)
        local = [pltpu.make_async_copy(ins[t], outs[t].at[me], loc.at[t]) for t in range(n)]
        for lc in local:
            lc.start()
        first = [_remote(ins[t], outs[t].at[me], send.at[t, k - 1], recv.at[t, k - 1], _peer(k))
                 for k in (1,) + OTHER_CHIPS for t in range(n)]
        for cp in first:
            cp.start()
        passed = []
        for k in OTHER_CHIPS:
            for t in range(n):
                slot = outs[t].at[_peer_slot(k)]
                _remote(ins[t], slot, send.at[t, k - 1], recv.at[t, k - 1], _peer(k)).wait_recv()
                cp = _remote(slot, slot, send.at[t, k], recv.at[t, k], sibling)
                cp.start()
                passed.append(cp)
        for t in range(n):
            _remote(ins[t], outs[t].at[_peer_slot(1)], send.at[t, 0], recv.at[t, 0], sibling).wait_recv()
            for k in OTHER_CHIPS:
                _remote(ins[t], outs[t].at[_peer_slot(k + 1)], send.at[t, k], recv.at[t, k], sibling).wait_recv()
        for cp in first + passed:
            cp.wait_send()
        for lc in local:
            lc.wait()

    hbm = pl.BlockSpec(memory_space=pl.ANY)
    return _pcall(
        body, name=name, in_specs=[hbm] * n, out_specs=[hbm] * n,
        out_shape=[jax.ShapeDtypeStruct((N_DEV,) + s.shape, s.dtype) for s in shards],
        scratch_shapes=[pltpu.SemaphoreType.DMA((n, N_DEV - 1)), pltpu.SemaphoreType.DMA((n, N_DEV - 1)),
                        pltpu.SemaphoreType.DMA((n,))],
    )(*shards)


def _sibling_swap(grads, *, name):
    n = len(grads)

    def body(*refs):
        ins, outs = refs[:n], refs[n:2 * n]
        send, recv = refs[2 * n:]
        c = lax.axis_index("c")
        copies = [_remote(ins[t].at[:, 1 - c], outs[t], send.at[t], recv.at[t], _peer(1)) for t in range(n)]
        for cp in copies:
            cp.start()
        for cp in copies:
            cp.wait()

    hbm = pl.BlockSpec(memory_space=pl.ANY)
    return _pcall(
        body, name=name, in_specs=[hbm] * n, out_specs=[hbm] * n,
        out_shape=[jax.ShapeDtypeStruct((N_CHIP,) + g.shape[2:], g.dtype) for g in grads],
        scratch_shapes=[pltpu.SemaphoreType.DMA((n,)), pltpu.SemaphoreType.DMA((n,))],
    )(*grads)


def _pair_add(g, p, *, name):
    _, _, R, C = g.shape
    tr = _pick(R, (128, 64, 32, 16, 8))

    def body(c_ref, g_ref, p_ref, o_ref):
        o_ref[...] = g_ref[...] + p_ref[...]

    grid_spec = pltpu.PrefetchScalarGridSpec(
        num_scalar_prefetch=1, grid=(N_CHIP, R // tr),
        in_specs=[pl.BlockSpec((None, None, tr, C), lambda j, i, c_ref: (j, c_ref[0], i, 0)),
                  pl.BlockSpec((None, tr, C), lambda j, i, c_ref: (j, i, 0))],
        out_specs=pl.BlockSpec((None, tr, C), lambda j, i, c_ref: (j, i, 0)))
    core = lax.axis_index("c").astype(jnp.int32).reshape(1)
    return _pcall(body, name=name, grid_spec=grid_spec, out_shape=jax.ShapeDtypeStruct(p.shape, F32),
                  compiler_params=_params(("parallel", "parallel")))(core, g, p)


def _chip_exchange(sums, *, name):
    n = len(sums)

    def body(*refs):
        ins, outs = refs[:n], refs[n:2 * n]
        send, recv, loc = refs[2 * n:]
        mine = 2 * lax.axis_index("x") + lax.axis_index("y")
        local = [pltpu.make_async_copy(ins[t].at[mine], outs[t].at[mine], loc.at[t]) for t in range(n)]
        for lc in local:
            lc.start()
        copies = []
        for k in OTHER_CHIPS:
            for t in range(n):
                theirs = _chip_of(_peer(k))
                cp = _remote(ins[t].at[theirs], outs[t].at[mine], send.at[t, k // 2 - 1], recv.at[t, k // 2 - 1], _peer(k))
                cp.start()
                copies.append(_remote(ins[t].at[theirs], outs[t].at[theirs], send.at[t, k // 2 - 1],
                                      recv.at[t, k // 2 - 1], _peer(k)))
        for cp in copies:
            cp.wait()
        for lc in local:
            lc.wait()

    hbm = pl.BlockSpec(memory_space=pl.ANY)
    return _pcall(
        body, name=name, in_specs=[hbm] * n, out_specs=[hbm] * n,
        out_shape=[jax.ShapeDtypeStruct(s.shape, s.dtype) for s in sums],
        scratch_shapes=[pltpu.SemaphoreType.DMA((n, N_CHIP - 1)), pltpu.SemaphoreType.DMA((n, N_CHIP - 1)),
                        pltpu.SemaphoreType.DMA((n,))],
    )(*sums)


def _reduce_scatter(grads):
    pairs = [g.reshape((N_CHIP, 2) + g.shape[1:]) for g in grads]
    partner = _sibling_swap(pairs, name="grad_sibling_swap")
    sums = [_pair_add(g, p, name="grad_pair_add") for g, p in zip(pairs, partner)]
    return _chip_exchange(sums, name="grad_chip_exchange")


def _adamw(recv, w, m, v, layer=None, *, name):
    n_slots, R, C = recv.shape
    tr = _pick(R, (128, 64, 32, 16, 8))

    def body(r_ref, w_ref, m_ref, v_ref, g_ref, d_ref, nm_ref, nv_ref):
        g = r_ref[0]
        for s in range(1, n_slots):
            g = g + r_ref[s]
        m_new = ADAM_B1 * m_ref[...] + (1.0 - ADAM_B1) * g
        v_new = ADAM_B2 * v_ref[...] + (1.0 - ADAM_B2) * jnp.square(g)
        m_hat = m_new / (1.0 - ADAM_B1 ** ADAM_STEP)
        v_hat = v_new / (1.0 - ADAM_B2 ** ADAM_STEP)
        g_ref[...] = g
        d_ref[...] = -ADAM_LR * (m_hat / (jnp.sqrt(v_hat) + ADAM_EPS) + ADAM_WD * w_ref[...])
        nm_ref[...] = m_new
        nv_ref[...] = v_new

    row = pl.BlockSpec((tr, C), lambda i: (i, 0))
    state = row if layer is None else pl.BlockSpec((None, tr, C), lambda i: (layer, i, 0))
    return _pcall(
        body, name=name, grid=(R // tr,),
        in_specs=[pl.BlockSpec((n_slots, tr, C), lambda i: (0, i, 0)), state, state, state],
        out_specs=[row] * 4, out_shape=[jax.ShapeDtypeStruct((R, C), F32)] * 4,
        compiler_params=_params(("parallel",)),
    )(recv, w, m, v)


def _flat_pad(parts, rows):
    flat = jnp.concatenate([p.reshape(-1) for p in parts])
    return jnp.pad(flat, (0, rows * LANES - flat.shape[0])).reshape(rows, LANES)


def _small_shard_shape(n):
    return SMALL[n][:-1] + (SMALL[n][-1] // N_DEV,)


def _unpack_small(gathered):
    out, off = {}, 0
    flat = gathered.reshape(N_DEV, -1)
    for n in SMALL:
        r, c = _small_shard_shape(n)
        out[n] = flat[:, off:off + r * c].reshape(N_DEV, r, c).transpose(1, 0, 2).reshape(r, N_DEV * c)
        off += r * c
    return out


def _pack_state(state, prefix, layer):
    return _flat_pad([state[prefix + n][layer] for n in (*SMALL, *REPL)], PACK_ROWS)


def _pack_small_grads(grads):
    cols = []
    for n in SMALL:
        r, c = _small_shard_shape(n)
        cols.append(grads[n].reshape(r, N_DEV, c).transpose(1, 0, 2).reshape(N_DEV, r * c))
    cols += [jnp.broadcast_to(grads[n].reshape(1, -1), (N_DEV, grads[n].size)) for n in REPL]
    flat = jnp.concatenate(cols, axis=1)
    return jnp.pad(flat, ((0, 0), (0, PACK_ROWS * LANES - flat.shape[1]))).reshape(N_DEV, PACK_ROWS, LANES)


def _unpack_state(flat):
    out, off = {}, 0
    flat = flat.reshape(-1)
    for n, shape in [(n, _small_shard_shape(n)) for n in SMALL] + list(REPL.items()):
        sz = math.prod(shape)
        out[n] = flat[off:off + sz].reshape(shape)
        off += sz
    return out


def _pad_b_in(b):
    parts, pos = [], 0
    for src, width, dst in Z_PIECES:
        parts += [jnp.zeros((1, dst - pos), b.dtype), b[:, src:src + width]]
        pos = dst + width
    return jnp.concatenate(parts + [jnp.zeros((1, Z_W - pos), b.dtype)], axis=1)


def _unpad_b_in(bp):
    return jnp.concatenate([bp[:, dst:dst + width] for _, width, dst in Z_PIECES], axis=1)


def _heads(t, n_heads):
    S = t.shape[0]
    return t.reshape(S, n_heads, -1).transpose(1, 0, 2)


def _unheads(t):
    H, S, D = t.shape
    return t.transpose(1, 0, 2).reshape(S, H * D)


def _up_shard_pad(t):
    lead = t.shape[:-1]
    t = jnp.pad(t.reshape(lead + (2, FF_HALF)), [(0, 0)] * len(lead) + [(0, 0), (0, FF_HALF_P - FF_HALF)])
    return t.reshape(lead + (2 * FF_HALF_P,))


def _up_shard_unpad(t):
    lead = t.shape[:-1]
    return t.reshape(lead + (2, FF_HALF_P))[..., :FF_HALF].reshape(lead + (2 * FF_HALF,))


def _pad_shard(n, t):
    if n == "w_in":
        return jnp.pad(t, ((0, 0), (0, W_IN_SLOT - W_IN_SHARD)))
    if n == "ffn_w_up":
        return _up_shard_pad(t)
    if n == "ffn_w_down":
        return jnp.pad(t, ((0, FF_HALF_P - FF_HALF), (0, 0)))
    return t


def _unpad_shard(n, t):
    if n == "w_in":
        return t[:, :W_IN_SHARD]
    if n == "ffn_w_up":
        return _up_shard_unpad(t)
    if n == "ffn_w_down":
        return t[:FF_HALF]
    return t


def _gather_layer(state, layer):
    shards = [_pad_shard(n, state[n][layer].astype(BF16)) for n in BIG]
    shards.append(_flat_pad([state[n][layer] for n in SMALL], SMALL_ROWS))
    *big, small = _all_gather(shards, name="all_gather_weights")
    W = dict(zip(BIG, big))
    for n in ("w_o", "xa_wq", "xa_wo", "ffn_w_down"):
        W[n] = W[n].reshape(-1, D_MODEL)
    W["w_in_p"] = _colmap(W.pop("w_in"), inverse=False, name="w_in_colmap")
    W.update(_unpack_small(small))
    return W


def _ssm_params(p):
    prep1_in = (p["ssm_lambda_re"], p["ssm_lambda_im"], p["ssm_log_dt"][:, None])
    ar, ai, gr, gi = _whole(_ssm_prep1, *prep1_in, name="ssm_prep1")
    to_cn = lambda b: b.transpose(2, 0, 1).reshape(SSM_GROUP, N_SSM_CH)
    prep2_in = (gr.reshape(1, N_SSM_CH), gi.reshape(1, N_SSM_CH), to_cn(p["ssm_b_re"]), to_cn(p["ssm_b_im"]))
    bbr, bbi = _whole(_ssm_prep2, *prep2_in, name="ssm_prep2")
    to_gcp = lambda t: t.reshape(SSM_GROUP, SSM_GROUPS, SSM_STATE).transpose(1, 0, 2)
    bb = _blockdiag(to_gcp(bbr), to_gcp(bbi))
    cct = _blockdiag(p["ssm_c_re"], -p["ssm_c_im"])
    a_vec = _to_blocked(ar.reshape(1, N_SSM_CH), ai.reshape(1, N_SSM_CH))
    return dict(bb=bb, cct=cct, a_vec=a_vec, prep1_in=prep1_in, prep2_in=prep2_in)


def _layer_fwd(x, mem, p, W):
    sp = _ssm_params(p)
    z = _mm(x, W["w_in_p"], bias=_pad_b_in(p["b_in"][None, :]), name="mm_in")
    zh = lambda off: _heads(z[:, off:off + 512], 8)
    f_t = z[:, Z_FF:Z_FF + FOX_HEADS].T
    cum = _cum_heads(f_t, name="fox_cum")
    ya_h, lse_a = _attn_fwd("fox", zh(Z_FQ), zh(Z_FK), zh(Z_FV), (cum[:, :, None], cum[:, None, :]), name="fox_fwd")
    ya_pre = _unheads(ya_h)
    ya = _mm(ya_pre, W["w_fox_o"], b_slots=True, name="mm_fox_o")
    x_ri = _mm(z, sp["bb"], a_off=Z_SU, a_cols=SSM_WIDTH, name="mm_s5_in")
    h_ri = _scan(x_ri, sp["a_vec"], name="s5_scan")
    hc = _mm(h_ri, sp["cct"], tb=True, name="mm_s5_out")
    d_row = p["ssm_d"][None, :]
    (gel,) = _rowwise(lambda a, b, c: (_s5_tail(a, b, c),), [hc, Win(z, 512, Z_SU // 512)], [d_row], name="s5_tail")
    yb2 = _mm(gel, W["w_ssm_glu"], b_slots=True, name="mm_glu")
    bias = _relbias_expand(W["ca_rel_bias"]).transpose(1, 0, 2)
    padk = lambda t: jnp.pad(t, ((0, 0), (CA_PAD, 0), (0, 0)))
    yc_h, lse_c = _attn_fwd("chunk", zh(Z_CQ), padk(zh(Z_CK)), padk(zh(Z_CV)), (bias,), name="chunk_fwd")
    yc_pre = _unheads(yc_h)
    yc = _mm(yc_pre, W["w_ca_o"], b_slots=True, name="mm_ca_o")
    gates = [Win(z, 1024, Z_GF // 1024), Win(z, 1024, Z_GS // 1024), Win(z, 1024, Z_GC // 1024)]
    (merged,) = _rowwise(lambda *a: (_merge(*a),), gates + [ya, yb2, yc], name="merge")
    h1 = _mm(merged, W["w_o"], name="mm_o")
    ln_g, ln_b = W["ln_g"], W["ln_b"]
    r1, x1 = _rowwise(_ln_fwd, [x, h1], [ln_g[0:1], ln_b[0:1]], name="ln_fwd")
    q = _mm(x1, W["xa_wq"], name="mm_xq")
    kv = _mm(mem, W["xa_wkv"], b_slots=True, name="mm_xkv")
    qh, kh, vh = _heads(q, XA_HEADS), _heads(kv[:, :D_MODEL], XA_HEADS), _heads(kv[:, D_MODEL:], XA_HEADS)
    o_h, lse_x = _attn_fwd("xa", qh, kh, vh, name="xa_fwd")
    o = _unheads(o_h)
    h2 = _mm(o, W["xa_wo"], name="mm_xo")
    r2, x2 = _rowwise(_ln_fwd, [x1, h2], [ln_g[1:2], ln_b[1:2]], name="ln_fwd")
    up = _mm(x2, W["ffn_w_up"], b_slots=True, name="mm_up")
    hmid = _ffn_mid(up, _ff_pad(W["ffn_conv_w"]), _ff_pad(p["ffn_conv_b"][None, :]), name="ffn_mid")
    h3 = _mm(hmid, W["ffn_w_down"], name="mm_down")
    r3, x3 = _rowwise(_ln_fwd, [x2, h3], [ln_g[2:3], ln_b[2:3]], name="ln_fwd")
    res = dict(x=x, z=z, cum=cum, lse_a=lse_a, ya_pre=ya_pre, ya=ya, h_ri=h_ri, hc=hc, gel=gel, yb2=yb2, lse_c=lse_c,
               yc_pre=yc_pre, yc=yc, merged=merged, r1=r1, x1=x1, q=q, kv=kv, o=o, lse_x=lse_x, r2=r2, x2=x2, up=up,
               hmid=hmid, r3=r3, W=W)
    return x3, res


def _layer_bwd(dx3, mem, p, res):
    W = res["W"]
    x, z = res["x"], res["z"]
    sp = _ssm_params(p)
    ln_g = W["ln_g"]
    big, small = {}, {}
    slots = lambda t: t.reshape(N_DEV, -1, D_MODEL)
    dr3, dg2, db2 = _rowwise(_ln_bwd, [res["r3"], dx3], [ln_g[2:3]], n_red=2, name="ln_bwd")
    dhmid = _mm(dr3, W["ffn_w_down"], tb=True, name="mm_down_dx")
    big["ffn_w_down"] = slots(_mm(res["hmid"], dr3, ta=True, name="mm_down_dw"))
    conv_w_p, conv_b_p = _ff_pad(W["ffn_conv_w"]), _ff_pad(p["ffn_conv_b"][None, :])
    dup_a, dup_g, dcw, dcb = _ffn_mid(res["up"], conv_w_p, conv_b_p, dhmid, name="ffn_mid_bwd")
    dup = jnp.concatenate([dup_a, dup_g], axis=1)
    small["ffn_conv_w"], small["ffn_conv_b"] = _ff_unpad(dcw), _ff_unpad(dcb)[0]
    dx2 = _mm(dup, W["ffn_w_up"], tb=True, b_slots=True, add=(dr3, DN_ALPHA), name="mm_up_dx")
    big["ffn_w_up"] = _mm(res["x2"], dup, ta=True, out_slots=2 * FF_HALF_P, name="mm_up_dw")
    dr2, dg1, db1 = _rowwise(_ln_bwd, [res["r2"], dx2], [ln_g[1:2]], n_red=2, name="ln_bwd")
    do = _mm(dr2, W["xa_wo"], tb=True, name="mm_xo_dx")
    big["xa_wo"] = slots(_mm(res["o"], dr2, ta=True, name="mm_xo_dw"))
    kv = res["kv"]
    qh, kh, vh = _heads(res["q"], XA_HEADS), _heads(kv[:, :D_MODEL], XA_HEADS), _heads(kv[:, D_MODEL:], XA_HEADS)
    dqh, dkh, dvh = _attn_bwd("xa", qh, kh, vh, res["lse_x"], _heads(do, XA_HEADS), name="xa_bwd")
    dq = _unheads(dqh)
    dkv = jnp.concatenate([_unheads(dkh), _unheads(dvh)], axis=1)
    dx1 = _mm(dq, W["xa_wq"], tb=True, add=(dr2, DN_ALPHA), name="mm_xq_dx")
    big["xa_wq"] = slots(_mm(res["x1"], dq, ta=True, name="mm_xq_dw"))
    big["xa_wkv"] = _mm(mem, dkv, ta=True, out_slots=256, name="mm_xkv_dw")
    dr1, dg0, db0 = _rowwise(_ln_bwd, [res["r1"], dx1], [ln_g[0:1]], n_red=2, name="ln_bwd")
    small["ln_g"] = jnp.concatenate([dg0, dg1, dg2], axis=0)
    small["ln_b"] = jnp.concatenate([db0, db1, db2], axis=0)
    dmerged = _mm(dr1, W["w_o"], tb=True, name="mm_o_dx")
    big["w_o"] = slots(_mm(res["merged"], dr1, ta=True, name="mm_o_dw"))
    gates = [Win(z, 1024, Z_GF // 1024), Win(z, 1024, Z_GS // 1024), Win(z, 1024, Z_GC // 1024)]
    dgf, dgs, dgc, dya, dyb2, dyc = _rowwise(_vjp_of(_merge, 6), gates + [res["ya"], res["yb2"], res["yc"], dmerged],
                                             name="merge_bwd")
    zh = lambda off: _heads(z[:, off:off + 512], 8)
    dya_pre = _mm(dya, W["w_fox_o"], tb=True, b_slots=True, name="mm_fox_o_dx")
    big["w_fox_o"] = _mm(res["ya_pre"], dya, ta=True, out_slots=128, name="mm_fox_o_dw")
    cum = res["cum"]
    dfq, dfk, dfv, dcq, dck = _attn_bwd("fox", zh(Z_FQ), zh(Z_FK), zh(Z_FV), res["lse_a"], _heads(dya_pre, 8),
                                        (cum[:, :, None], cum[:, None, :]), name="fox_bwd")
    f_t = z[:, Z_FF:Z_FF + FOX_HEADS].T
    dff = _cum_heads(f_t, dcq[:, :, 0] + dck[:, 0, :], name="fox_cum_bwd")
    dgel = _mm(dyb2, W["w_ssm_glu"], tb=True, b_slots=True, name="mm_glu_dx")
    big["w_ssm_glu"] = _mm(res["gel"], dyb2, ta=True, out_slots=256, name="mm_glu_dw")
    d_row = p["ssm_d"][None, :]
    su_win = Win(z, 512, Z_SU // 512)
    dy, dsu1, dd = _rowwise(_s5_tail_bwd, [res["hc"], su_win, dgel], [d_row], n_red=1, name="s5_tail_bwd")
    small["ssm_d"] = dd[0]
    dh_ri = _mm(dy, sp["cct"], name="mm_s5_out_dx")
    dcct = _mm(dy, res["h_ri"], ta=True, name="mm_s5_out_dw")
    dx_ri, da_vec = _scan(dh_ri, sp["a_vec"], res["h_ri"], name="s5_scan_bwd")
    dsu = _mm(dx_ri, sp["bb"], tb=True, add=(dsu1, 1.0), name="mm_s5_in_dx")
    dbb = _mm(z, dx_ri, ta=True, a_off=Z_SU, a_cols=SSM_WIDTH, name="mm_s5_in_dw")
    dcr, dci = _blockdiag_inv(dcct)
    small["ssm_c_re"], small["ssm_c_im"] = dcr, -dci
    dbbr, dbbi = _blockdiag_inv(dbb)
    to_cn = lambda t: t.transpose(1, 0, 2).reshape(SSM_GROUP, N_SSM_CH)
    dgr, dgi, dbr, dbi = _whole(_vjp_of(_ssm_prep2, 4), *sp["prep2_in"], to_cn(dbbr), to_cn(dbbi), name="ssm_prep2_bwd")
    from_cn = lambda t: t.reshape(SSM_GROUP, SSM_GROUPS, SSM_STATE).transpose(1, 2, 0)
    small["ssm_b_re"], small["ssm_b_im"] = from_cn(dbr), from_cn(dbi)
    dar, dai = _from_blocked(da_vec)
    sq = lambda t: t.reshape(SSM_GROUPS, SSM_STATE)
    dlr, dli, dldt = _whole(_vjp_of(_ssm_prep1, 3), *sp["prep1_in"], sq(dar), sq(dai), sq(dgr), sq(dgi),
                            name="ssm_prep1_bwd")
    small["ssm_lambda_re"], small["ssm_lambda_im"], small["ssm_log_dt"] = dlr, dli, dldt[:, 0]
    dyc_pre = _mm(dyc, W["w_ca_o"], tb=True, b_slots=True, name="mm_ca_o_dx")
    big["w_ca_o"] = _mm(res["yc_pre"], dyc, ta=True, out_slots=128, name="mm_ca_o_dw")
    bias = _relbias_expand(W["ca_rel_bias"]).transpose(1, 0, 2)
    padk = lambda t: jnp.pad(t, ((0, 0), (CA_PAD, 0), (0, 0)))
    dcqh, dckh, dcvh, dbias = _attn_bwd("chunk", zh(Z_CQ), padk(zh(Z_CK)), padk(zh(Z_CV)), res["lse_c"],
                                        _heads(dyc_pre, 8), (bias,), name="chunk_bwd")
    small["ca_rel_bias"] = _relbias_reduce(dbias.transpose(1, 0, 2))
    dff_p = jnp.pad(dff.T, ((0, 0), (0, 512 - FOX_HEADS)))
    dz = jnp.concatenate([_unheads(dfq), _unheads(dfk), _unheads(dfv), dff_p, dsu, _unheads(dcqh),
                          _unheads(dckh[:, CA_PAD:]), _unheads(dcvh[:, CA_PAD:]), dgf, dgs, dgc], axis=1)
    dx = _mm(dz, W["w_in_p"], tb=True, add=(dr1, DN_ALPHA), name="mm_in_dx")
    big["w_in"] = _colmap(_mm(x, dz, ta=True, name="mm_in_dw"), inverse=True, name="w_in_colmap_inv")
    (db_in_p,) = _rowwise(lambda t: (jnp.sum(t, axis=0, keepdims=True),), [dz], n_red=1, name="colsum")
    small["b_in"] = _unpad_b_in(db_in_p)[0]
    return dx, big, small


def kernel(x, mem, w_in, b_in, ssm_lambda_re, ssm_lambda_im, ssm_log_dt, ssm_b_re, ssm_b_im, ssm_c_re, ssm_c_im, ssm_d, ca_rel_bias, w_fox_o, w_ssm_glu, w_ca_o, w_o, xa_wq, xa_wkv, xa_wo, ffn_w_up, ffn_conv_w, ffn_conv_b, ffn_w_down, ln_g, ln_b, loss_target, m_w_in, m_b_in, m_ssm_lambda_re, m_ssm_lambda_im, m_ssm_log_dt, m_ssm_b_re, m_ssm_b_im, m_ssm_c_re, m_ssm_c_im, m_ssm_d, m_ca_rel_bias, m_w_fox_o, m_w_ssm_glu, m_w_ca_o, m_w_o, m_xa_wq, m_xa_wkv, m_xa_wo, m_ffn_w_up, m_ffn_conv_w, m_ffn_conv_b, m_ffn_w_down, m_ln_g, m_ln_b, v_w_in, v_b_in, v_ssm_lambda_re, v_ssm_lambda_im, v_ssm_log_dt, v_ssm_b_re, v_ssm_b_im, v_ssm_c_re, v_ssm_c_im, v_ssm_d, v_ca_rel_bias, v_w_fox_o, v_w_ssm_glu, v_w_ca_o, v_w_o, v_xa_wq, v_xa_wkv, v_xa_wo, v_ffn_w_up, v_ffn_conv_w, v_ffn_conv_b, v_ffn_w_down, v_ln_g, v_ln_b):
    given = dict(locals())
    state = {pre + n: given[pre + n] for n in WEIGHTS for pre in ("", "m_", "v_")}
    mem0 = mem[0]
    layer_params = [{n: state[n][l] for n in REPL} for l in range(DEPTH)]

    h, residuals = x[0], []
    for l in range(DEPTH):
        h, res = _layer_fwd(h, mem0, layer_params[l], _gather_layer(state, l))
        residuals.append(res)
    dh, loss_cols = _rowwise(_loss_rows, [h, loss_target[0]], n_red=1, name="loss")
    loss = lax.psum(jnp.sum(loss_cols), ("x", "y", "c"))

    outs = [None] * DEPTH
    for l in reversed(range(DEPTH)):
        dh, big, small = _layer_bwd(dh, mem0, layer_params[l], residuals[l])
        *recv_big, recv_small = _reduce_scatter([big[n] for n in BIG] + [_pack_small_grads(small)])
        layer_out = {}
        for n, recv in zip(BIG, recv_big):
            if recv.shape[1:] == BIG[n][0]:
                res4 = _adamw(recv, state[n], state["m_" + n], state["v_" + n], l, name="adamw_" + n)
            else:
                padded = [_pad_shard(n, state[pre + n][l]) for pre in ("", "m_", "v_")]
                res4 = [_unpad_shard(n, t) for t in _adamw(recv, *padded, name="adamw_" + n)]
            layer_out[n] = res4
        packed = _adamw(recv_small, *[_pack_state(state, pre, l) for pre in ("", "m_", "v_")], name="adamw_small")
        for n, t4 in zip((*SMALL, *REPL), zip(*[_unpack_state(t).values() for t in packed])):
            layer_out[n] = t4
        outs[l] = layer_out

    stacked = lambda n, j: jnp.stack([outs[l][n][j] for l in range(DEPTH)])
    return (loss, dh[None], *[stacked(n, j) for j in range(4) for n in WEIGHTS])
```

```python
import functools
import math

import jax
import jax.numpy as jnp
from jax import lax
from jax.experimental import pallas as pl
from jax.experimental.pallas import tpu as pltpu

F32, BF16 = jnp.float32, jnp.bfloat16

D_MODEL = 1024
DEPTH = 4
CHUNK = 64
FOX_HEADS, FOX_HEAD_DIM, FOX_WIDTH = 8, 64, 512
SSM_GROUP, SSM_WIDTH, SSM_GROUPS, SSM_STATE = 16, 512, 32, 64
CA_HEADS, CA_HEAD_DIM, CA_WIDTH, CA_LEFT_CHUNKS = 8, 64, 512, 8
CA_BAND = (CA_LEFT_CHUNKS + 1) * CHUNK
CA_PAD = CA_LEFT_CHUNKS * CHUNK
REL_MIN, REL_MAX = -(CHUNK - 1), 4 * CHUNK
N_REL = REL_MAX - REL_MIN + 1
XA_HEADS, XA_HEAD_DIM = 4, 256
D_FF = 2816
DN_ALPHA = (2 * DEPTH) ** 0.25
LN_EPS = 1e-5
NEG_INF = -1e30
ADAM_LR, ADAM_B1, ADAM_B2, ADAM_EPS, ADAM_WD, ADAM_STEP = 0.001, 0.9, 0.999, 1e-08, 0.01, 10

N_DEV = 8
LANE = 128
N_SSM_CH = SSM_GROUPS * SSM_STATE
SCAN_CB = 256
N_IN = 6664
W_IN_SHARD, W_IN_SLOT = N_IN // N_DEV, 896
Z_W = 7168
Z_FQ, Z_FK, Z_FV, Z_FF, Z_SU, Z_CQ, Z_CK, Z_CV, Z_GF, Z_GS, Z_GC = (
    0, 512, 1024, 1536, 2048, 2560, 3072, 3584, 4096, 5120, 6144)
Z_PIECES = ((0, 512, Z_FQ), (512, 512, Z_FK), (1024, 512, Z_FV), (1536, 8, Z_FF), (1544, 512, Z_SU),
            (2056, 512, Z_CQ), (2568, 512, Z_CK), (3080, 512, Z_CV), (3592, 1024, Z_GF), (4616, 1024, Z_GS),
            (5640, 1024, Z_GC))
FF_HALF, FF_HALF_P = D_FF // N_DEV, 384
D_FF_P = N_DEV * FF_HALF_P

VMEM_LIMIT_BYTES = 56 * 1024 * 1024

BIG = {"w_in": ((1024, W_IN_SHARD), 1), "w_fox_o": ((512, 128), 1), "w_ssm_glu": ((512, 256), 1),
       "w_ca_o": ((512, 128), 1), "w_o": ((128, 1024), 0), "xa_wq": ((128, 1024), 0), "xa_wkv": ((1024, 256), 1),
       "xa_wo": ((128, 1024), 0), "ffn_w_up": ((1024, 2 * FF_HALF), 1), "ffn_w_down": ((FF_HALF, 1024), 0)}
SMALL = {"ca_rel_bias": (8, 320), "ffn_conv_w": (3, D_FF), "ln_g": (3, 1024), "ln_b": (3, 1024)}
REPL = {"b_in": (N_IN,), "ssm_lambda_re": (32, 64), "ssm_lambda_im": (32, 64), "ssm_log_dt": (32,),
        "ssm_b_re": (32, 64, 16), "ssm_b_im": (32, 64, 16), "ssm_c_re": (32, 16, 64), "ssm_c_im": (32, 16, 64),
        "ssm_d": (512,), "ffn_conv_b": (D_FF,)}
WEIGHTS = ("w_in", "b_in", "ssm_lambda_re", "ssm_lambda_im", "ssm_log_dt", "ssm_b_re", "ssm_b_im", "ssm_c_re",
           "ssm_c_im", "ssm_d", "ca_rel_bias", "w_fox_o", "w_ssm_glu", "w_ca_o", "w_o", "xa_wq", "xa_wkv", "xa_wo",
           "ffn_w_up", "ffn_conv_w", "ffn_conv_b", "ffn_w_down", "ln_g", "ln_b")
LANES = 1024
PACK_ROWS = 256
SMALL_ROWS = 8


def _w_in_segments():
    segs = []
    for src, width, dst in Z_PIECES:
        n = src
        while n < src + width:
            d = n // W_IN_SHARD
            end = min(src + width, (d + 1) * W_IN_SHARD)
            segs.append((W_IN_SLOT * d + n - W_IN_SHARD * d, dst + n - src, end - n))
            n = end
    return tuple(segs)


W_IN_SEGS = _w_in_segments()


def _pcall(body, **kw):
    return pl.pallas_call(body, **kw)


def _params(sem):
    return pltpu.CompilerParams(dimension_semantics=sem, vmem_limit_bytes=VMEM_LIMIT_BYTES)


def _pick(dim, prefs):
    for p in prefs:
        if dim % p == 0:
            return p
    return dim


def _mm(a, b, *, ta=False, tb=False, bias=None, add=None, a_off=0, a_cols=None, b_slots=False, out_slots=None,
        name, out_dtype=F32):
    a_cols = a_cols if a_cols is not None else a.shape[1]
    M, K = (a_cols, a.shape[0]) if ta else (a.shape[0], a_cols)
    tm = _pick(M, (1024, 512, 256, 128))
    if ta:
        tk = _pick(K, (1024, 512, 256))
    elif b_slots and tb:
        tk = _pick(b.shape[2], (1024, 768, 512, 256, 128))
    else:
        tk = K if K <= 3072 else _pick(K, (1024, 512, 256, 128))
    nk = K // tk
    if b_slots:
        ns = b.shape[2]
        if tb:
            N = b.shape[1]
            tn = _pick(N, (512, 256, 128))
            per = ns // tk
            b_spec = pl.BlockSpec((None, tn, tk), lambda i, j, k: (k // per, j, k % per))
            b_dim = 1
            assert N_DEV * ns == K
        else:
            N = N_DEV * ns
            tn = _pick(ns, (512, 256, 128))
            per = ns // tn
            b_spec = pl.BlockSpec((None, tk, tn), lambda i, j, k: (j // per, k, j % per))
            b_dim = 0
            assert b.shape[1] == K
    else:
        N = b.shape[0] if tb else b.shape[1]
        assert (b.shape[1] if tb else b.shape[0]) == K, (a.shape, b.shape, ta, tb)
        tn = _pick(N if out_slots is None else out_slots, (512, 256, 128))
        if tb:
            b_spec = pl.BlockSpec((tn, tk), lambda i, j, k: (j, k))
            b_dim = 1
        else:
            b_spec = pl.BlockSpec((tk, tn), lambda i, j, k: (k, j))
            b_dim = 0
    if ta:
        assert a_off % tm == 0
        a_spec = pl.BlockSpec((tk, tm), lambda i, j, k: (k, i + a_off // tm))
        a_dim = 0
    else:
        assert a_off % tk == 0
        a_spec = pl.BlockSpec((tm, tk), lambda i, j, k: (i, k + a_off // tk))
        a_dim = 1
    dims = (((a_dim,), (b_dim,)), ((), ()))
    ins, specs = [a, b], [a_spec, b_spec]
    if bias is not None:
        ins.append(bias)
        specs.append(pl.BlockSpec((1, tn), lambda i, j, k: (0, j)))
    add_scale = None
    if add is not None:
        ins.append(add[0])
        add_scale = add[1]
        specs.append(pl.BlockSpec((tm, tn), lambda i, j, k: (i, j)))
    if out_slots is None:
        out_spec = pl.BlockSpec((tm, tn), lambda i, j, k: (i, j))
        out_shape = jax.ShapeDtypeStruct((M, N), out_dtype)
    else:
        assert N == N_DEV * out_slots
        per_o = out_slots // tn
        out_spec = pl.BlockSpec((None, tm, tn), lambda i, j, k: (j // per_o, i, j % per_o))
        out_shape = jax.ShapeDtypeStruct((N_DEV, M, out_slots), out_dtype)

    def body(*refs):
        a_ref, b_ref = refs[0], refs[1]
        pos = 2
        bias_ref = add_ref = None
        if bias is not None:
            bias_ref = refs[pos]
            pos += 1
        if add is not None:
            add_ref = refs[pos]
            pos += 1
        o_ref = refs[pos]
        acc_ref = refs[pos + 1] if nk > 1 else None
        part = lax.dot_general(a_ref[...].astype(BF16), b_ref[...].astype(BF16), dims, preferred_element_type=F32)

        def finish(acc):
            if bias_ref is not None:
                acc = acc + bias_ref[...]
            if add_ref is not None:
                acc = acc + add_scale * add_ref[...]
            o_ref[...] = acc.astype(out_dtype)

        if nk == 1:
            finish(part)
        else:
            k = pl.program_id(2)

            @pl.when(k == 0)
            def _():
                acc_ref[...] = part

            @pl.when(k > 0)
            def _():
                acc_ref[...] += part

            @pl.when(k == nk - 1)
            def _():
                finish(acc_ref[...])

    return _pcall(
        body, name=name, grid=(M // tm, N // tn, nk), in_specs=specs, out_specs=out_spec, out_shape=out_shape,
        scratch_shapes=[pltpu.VMEM((tm, tn), F32)] if nk > 1 else [],
        compiler_params=_params(("parallel", "parallel", "arbitrary")),
    )(*ins)


class Win:
    def __init__(self, arr, width, blk):
        self.arr, self.width, self.blk = arr, width, blk


def _rowwise(fn, rows, vecs=(), *, n_red=0, tr=256, name):
    wins = [r if isinstance(r, Win) else Win(r, r.shape[1], 0) for r in rows]
    S = wins[0].arr.shape[0]
    tr = min(tr, S)
    tile_args = [jax.ShapeDtypeStruct((tr, w.width), w.arr.dtype) for w in wins]
    tile_args += [jax.ShapeDtypeStruct(v.shape, v.dtype) for v in vecs]
    outs = jax.eval_shape(fn, *tile_args)
    n_row = len(outs) - n_red
    specs = [pl.BlockSpec((tr, w.width), functools.partial(lambda i, b: (i, b), b=w.blk)) for w in wins]
    specs += [pl.BlockSpec(v.shape, functools.partial(lambda i, nd: (0,) * nd, nd=v.ndim)) for v in vecs]
    out_specs = [pl.BlockSpec((tr, o.shape[1]), lambda i: (i, 0)) for o in outs[:n_row]]
    out_specs += [pl.BlockSpec(o.shape, functools.partial(lambda i, nd: (0,) * nd, nd=len(o.shape))) for o in outs[n_row:]]
    out_shape = [jax.ShapeDtypeStruct((S, o.shape[1]), o.dtype) for o in outs[:n_row]]
    out_shape += [jax.ShapeDtypeStruct(o.shape, o.dtype) for o in outs[n_row:]]
    n_in = len(wins) + len(vecs)

    def body(*refs):
        res = fn(*[r[...] for r in refs[:n_in]])
        o_refs = refs[n_in:]
        for o_ref, r in zip(o_refs[:n_row], res[:n_row]):
            o_ref[...] = r.astype(o_ref.dtype)
        i = pl.program_id(0)
        for o_ref, r in zip(o_refs[n_row:], res[n_row:]):
            @pl.when(i == 0)
            def _(o_ref=o_ref, r=r):
                o_ref[...] = r

            @pl.when(i > 0)
            def _(o_ref=o_ref, r=r):
                o_ref[...] += r

    return _pcall(
        body, name=name, grid=(S // tr,), in_specs=specs, out_specs=out_specs, out_shape=out_shape,
        compiler_params=_params(("arbitrary",)),
    )(*[w.arr for w in wins], *vecs)


def _whole(fn, *arrays, name):
    outs = jax.eval_shape(fn, *arrays)
    n_in = len(arrays)

    def body(*refs):
        res = fn(*[r[...] for r in refs[:n_in]])
        for o_ref, r in zip(refs[n_in:], res):
            o_ref[...] = r

    vm = pl.BlockSpec(memory_space=pltpu.VMEM)
    return _pcall(body, name=name, in_specs=[vm] * n_in, out_specs=[vm] * len(outs),
                  out_shape=[jax.ShapeDtypeStruct(o.shape, o.dtype) for o in outs])(*arrays)


def _split3(x):
    hi = x.astype(BF16)
    r = x - hi.astype(F32)
    mid = r.astype(BF16)
    lo = (r - mid.astype(F32)).astype(BF16)
    return hi, mid, lo


def _dot3(x, onehot, dims):
    return sum(lax.dot_general(t, onehot, dims, preferred_element_type=F32) for t in _split3(x))


_NT = (((1,), (1,)), ((), ()))
_NN = (((1,), (0,)), ((), ()))
_TN = (((0,), (0,)), ((), ()))


def _colmap(x, *, inverse, name):
    R = x.shape[0] if inverse else x.shape[1]
    tr = 256
    per = W_IN_SLOT // LANE
    n_out = N_DEV * per if inverse else Z_W // LANE
    segs = [(p, q, n) for q, p, n in W_IN_SEGS] if inverse else list(W_IN_SEGS)

    def body(x_ref, o_ref):
        ia = lax.broadcasted_iota(jnp.int32, (LANE, LANE), 0)
        ib = lax.broadcasted_iota(jnp.int32, (LANE, LANE), 1)

        def src_block(i):
            if inverse:
                return x_ref[:, i * LANE:(i + 1) * LANE]
            return x_ref[i // per, :, (i % per) * LANE:(i % per + 1) * LANE]

        for jb in range(n_out):
            acc = None
            for s0, d0, n in segs:
                lo, hi = max(d0, jb * LANE), min(d0 + n, (jb + 1) * LANE)
                if lo >= hi:
                    continue
                delta = d0 - s0
                for i in range((lo - delta) // LANE, (hi - delta - 1) // LANE + 1):
                    shift = jb * LANE - i * LANE - delta
                    sel = ((ia - ib == shift) & (ib >= lo - jb * LANE) & (ib < hi - jb * LANE)).astype(BF16)
                    blk = src_block(i)
                    part = _dot3(blk, sel, _NN) if inverse else lax.dot_general(blk, sel, _NN, preferred_element_type=F32)
                    acc = part if acc is None else acc + part
            if acc is None:
                acc = jnp.zeros((tr, LANE), F32)
            if inverse:
                o_ref[jb // per, :, (jb % per) * LANE:(jb % per + 1) * LANE] = acc
            else:
                o_ref[:, jb * LANE:(jb + 1) * LANE] = acc.astype(BF16)

    slot_spec = pl.BlockSpec((N_DEV, tr, W_IN_SLOT), lambda i: (0, i, 0))
    flat_spec = pl.BlockSpec((tr, Z_W), lambda i: (i, 0))
    if inverse:
        return _pcall(body, name=name, grid=(R // tr,), in_specs=[flat_spec], out_specs=slot_spec,
                      out_shape=jax.ShapeDtypeStruct((N_DEV, R, W_IN_SLOT), F32), compiler_params=_params(("parallel",)))(x)
    return _pcall(body, name=name, grid=(R // tr,), in_specs=[slot_spec], out_specs=flat_spec,
                  out_shape=jax.ShapeDtypeStruct((R, Z_W), BF16), compiler_params=_params(("parallel",)))(x)


def _log_sigmoid(x):
    return jnp.minimum(x, 0.0) - jnp.log(1.0 + jnp.exp(-jnp.abs(x)))


def _cum_heads(f, dcum=None, *, name):
    H, S = f.shape
    tn = min(512, S)
    rev = dcum is not None

    def body(*refs):
        j = pl.program_id(0)
        s_idx = lax.broadcasted_iota(jnp.int32, (S, tn), 0)
        t_idx = lax.broadcasted_iota(jnp.int32, (S, tn), 1) + j * tn
        if not rev:
            f_ref, o_ref = refs
            tri = (s_idx <= t_idx).astype(BF16)
            o_ref[...] = _dot3(_log_sigmoid(f_ref[...]), tri, _NN)
        else:
            fj_ref, d_ref, o_ref = refs
            tri = (s_idx >= t_idx).astype(BF16)
            o_ref[...] = _dot3(d_ref[...], tri, _NN) * jax.nn.sigmoid(-fj_ref[...])

    full = pl.BlockSpec((H, S), lambda j: (0, 0))
    blk = pl.BlockSpec((H, tn), lambda j: (0, j))
    ins, specs = ([f], [full]) if not rev else ([f, dcum], [blk, full])
    return _pcall(body, name=name, grid=(S // tn,), in_specs=specs, out_specs=blk,
                  out_shape=jax.ShapeDtypeStruct((H, S), F32), compiler_params=_params(("arbitrary",)))(*ins)


def _rel_onehot(qi, transposed):
    shape = (N_REL, CA_BAND) if transposed else (CA_BAND, N_REL)
    kk = lax.broadcasted_iota(jnp.int32, shape, 1 if transposed else 0)
    rr = lax.broadcasted_iota(jnp.int32, shape, 0 if transposed else 1)
    idx = jnp.clip(CA_PAD + qi - kk, REL_MIN, REL_MAX) - REL_MIN
    return (idx == rr).astype(BF16)


def _relbias_expand(rb):
    def body(rb_ref, o_ref):
        o_ref[0] = _dot3(rb_ref[...], _rel_onehot(pl.program_id(0), True), _NN)

    return _pcall(body, name="relbias_expand", grid=(CHUNK,),
                  in_specs=[pl.BlockSpec((CA_HEADS, N_REL), lambda q: (0, 0))],
                  out_specs=pl.BlockSpec((1, CA_HEADS, CA_BAND), lambda q: (q, 0, 0)),
                  out_shape=jax.ShapeDtypeStruct((CHUNK, CA_HEADS, CA_BAND), F32),
                  compiler_params=_params(("arbitrary",)))(rb)


def _relbias_reduce(db):
    def body(db_ref, o_ref):
        q = pl.program_id(0)
        part = _dot3(db_ref[0], _rel_onehot(q, False), _NN)

        @pl.when(q == 0)
        def _():
            o_ref[...] = part

        @pl.when(q > 0)
        def _():
            o_ref[...] += part

    return _pcall(body, name="relbias_reduce", grid=(CHUNK,),
                  in_specs=[pl.BlockSpec((1, CA_HEADS, CA_BAND), lambda q: (q, 0, 0))],
                  out_specs=pl.BlockSpec((CA_HEADS, N_REL), lambda q: (0, 0)),
                  out_shape=jax.ShapeDtypeStruct((CA_HEADS, N_REL), F32),
                  compiler_params=_params(("arbitrary",)))(db)


def _attn_cfg(mode, S):
    if mode == "fox":
        return min(256, S), FOX_HEAD_DIM ** -0.5
    if mode == "chunk":
        return CHUNK, CA_HEAD_DIM ** -0.5
    return min(512, S), XA_HEAD_DIM ** -0.5


def _scores(mode, i, tq, scale, qb, kb, extra):
    s = lax.dot_general(qb, kb, _NT, preferred_element_type=F32) * scale
    nk = kb.shape[0]
    if mode == "fox":
        cq, ck = extra
        s = s + cq - ck
        row = lax.broadcasted_iota(jnp.int32, (tq, nk), 0) + i * tq
        col = lax.broadcasted_iota(jnp.int32, (tq, nk), 1)
        s = jnp.where(row >= col, s, NEG_INF)
    elif mode == "chunk":
        (bias,) = extra
        col = lax.broadcasted_iota(jnp.int32, (tq, nk), 1) + i * CHUNK
        s = jnp.where(col >= CA_PAD, s + bias, NEG_INF)
    return s


def _attn_specs(mode, q, k, tq, hg):
    H, S, D = q.shape
    Sk = k.shape[1]
    q_spec = pl.BlockSpec((hg, tq, D), lambda g, i: (g, i, 0))
    kv_spec = pl.BlockSpec((hg, Sk, D), lambda g, i: (g, 0, 0))
    if mode == "fox":
        ex = [pl.BlockSpec((hg, tq, 1), lambda g, i: (g, i, 0)), pl.BlockSpec((hg, 1, Sk), lambda g, i: (g, 0, 0))]
    elif mode == "chunk":
        ex = [pl.BlockSpec((hg, CHUNK, CA_BAND), lambda g, i: (g, 0, 0))]
    else:
        ex = []
    col_spec = pl.BlockSpec((hg, tq, 1), lambda g, i: (g, i, 0))
    return q_spec, kv_spec, ex, col_spec


def _kv_window(mode, i, ref, h):
    if mode == "chunk":
        return ref[h, pl.ds(pl.multiple_of(i * CHUNK, CHUNK), CA_BAND), :]
    return ref[h]


def _attn_fwd(mode, q, k, v, extra=(), *, name):
    H, S, D = q.shape
    tq, scale = _attn_cfg(mode, S)
    hg = H
    q_spec, kv_spec, ex_specs, col_spec = _attn_specs(mode, q, k, tq, hg)
    n_ex = len(extra)

    def body(*refs):
        q_ref, k_ref, v_ref = refs[:3]
        o_ref, lse_ref = refs[3 + n_ex:]
        i = pl.program_id(1)
        for h in range(hg):
            ex = [r[h] for r in refs[3:3 + n_ex]]
            kb = _kv_window(mode, i, k_ref, h).astype(BF16)
            vb = _kv_window(mode, i, v_ref, h).astype(BF16)
            s = _scores(mode, i, tq, scale, q_ref[h].astype(BF16), kb, ex)
            m = jnp.max(s, axis=1, keepdims=True)
            e = jnp.exp(s - m)
            l = jnp.sum(e, axis=1, keepdims=True)
            p = e / l
            o_ref[h] = lax.dot_general(p.astype(BF16), vb, _NN, preferred_element_type=F32)
            lse_ref[h] = m + jnp.log(l)

    return _pcall(
        body, name=name, grid=(H // hg, S // tq), in_specs=[q_spec, kv_spec, kv_spec] + ex_specs,
        out_specs=[q_spec, col_spec],
        out_shape=[jax.ShapeDtypeStruct((H, S, D), F32), jax.ShapeDtypeStruct((H, S, 1), F32)],
        compiler_params=_params(("parallel", "arbitrary")),
    )(q, k, v, *extra)


def _attn_bwd(mode, q, k, v, lse, do, extra=(), *, name):
    H, S, D = q.shape
    Sk = k.shape[1]
    tq, scale = _attn_cfg(mode, S)
    hg = min(H, 4)
    q_spec, kv_spec, ex_specs, col_spec = _attn_specs(mode, q, k, tq, hg)
    n_ex = len(extra)
    out_specs = [q_spec, kv_spec, kv_spec]
    out_shape = [jax.ShapeDtypeStruct((H, S, D), F32), jax.ShapeDtypeStruct((H, Sk, D), F32),
                 jax.ShapeDtypeStruct((H, Sk, D), F32)]
    if mode == "fox":
        out_specs += [col_spec, ex_specs[1]]
        out_shape += [jax.ShapeDtypeStruct((H, S, 1), F32), jax.ShapeDtypeStruct((H, 1, Sk), F32)]
    elif mode == "chunk":
        out_specs += [ex_specs[0]]
        out_shape += [jax.ShapeDtypeStruct((H, CHUNK, CA_BAND), F32)]

    def body(*refs):
        q_ref, k_ref, v_ref, lse_ref, do_ref = refs[:5]
        dq_ref, dk_ref, dv_ref = refs[5 + n_ex:8 + n_ex]
        rest = refs[8 + n_ex:]
        i = pl.program_id(1)

        @pl.when(i == 0)
        def _():
            dk_ref[...] = jnp.zeros_like(dk_ref)
            dv_ref[...] = jnp.zeros_like(dv_ref)
            if mode == "fox":
                rest[1][...] = jnp.zeros_like(rest[1])
            elif mode == "chunk":
                rest[0][...] = jnp.zeros_like(rest[0])

        for h in range(hg):
            ex = [r[h] for r in refs[5:5 + n_ex]]
            qb = q_ref[h].astype(BF16)
            kb = _kv_window(mode, i, k_ref, h).astype(BF16)
            vb = _kv_window(mode, i, v_ref, h).astype(BF16)
            dob = do_ref[h].astype(BF16)
            s = _scores(mode, i, tq, scale, qb, kb, ex)
            p = jnp.exp(s - lse_ref[h])
            dp = lax.dot_general(dob, vb, _NT, preferred_element_type=F32)
            ds = p * (dp - jnp.sum(dp * p, axis=1, keepdims=True))
            dsb = (ds * scale).astype(BF16)
            dq_ref[h] = lax.dot_general(dsb, kb, _NN, preferred_element_type=F32)
            dk_part = lax.dot_general(dsb, qb, _TN, preferred_element_type=F32)
            dv_part = lax.dot_general(p.astype(BF16), dob, _TN, preferred_element_type=F32)
            if mode == "chunk":
                win = pl.ds(pl.multiple_of(i * CHUNK, CHUNK), CA_BAND)
                dk_ref[h, win, :] += dk_part
                dv_ref[h, win, :] += dv_part
                rest[0][h] += ds
            else:
                dk_ref[h] += dk_part
                dv_ref[h] += dv_part
            if mode == "fox":
                rest[0][h] = jnp.sum(ds, axis=1, keepdims=True)
                rest[1][h] += -jnp.sum(ds, axis=0, keepdims=True)

    return _pcall(
        body, name=name, grid=(H // hg, S // tq),
        in_specs=[q_spec, kv_spec, kv_spec, col_spec, q_spec] + ex_specs,
        out_specs=out_specs, out_shape=out_shape,
        compiler_params=_params(("parallel", "arbitrary")),
    )(q, k, v, lse, do, *extra)


def _scan(x, a, h=None, *, name):
    S = x.shape[0]
    CB = SCAN_CB
    rev = h is not None
    n_grp = S // 8

    def body(*refs):
        if rev:
            x_ref, a_ref, h_ref, o_ref, da_ref = refs
        else:
            x_ref, a_ref, o_ref = refs
        ar = a_ref[:, :CB]
        ai = -a_ref[:, CB:] if rev else a_ref[:, CB:]
        zero = jnp.zeros((1, CB), F32)

        def group(g, carry):
            base = pl.multiple_of((n_grp - 1 - g) * 8 if rev else g * 8, 8)
            for j in (range(7, -1, -1) if rev else range(8)):
                t = base + j
                if rev:
                    hr, hi, dar, dai = carry
                else:
                    hr, hi = carry
                xr = x_ref[pl.ds(t, 1), :CB]
                xi = x_ref[pl.ds(t, 1), CB:]
                hr, hi = ar * hr - ai * hi + xr, ar * hi + ai * hr + xi
                o_ref[pl.ds(t, 1), :CB] = hr
                o_ref[pl.ds(t, 1), CB:] = hi
                if rev:
                    tp = jnp.maximum(t - 1, 0)
                    live = (t > 0).astype(F32)
                    pr = h_ref[pl.ds(tp, 1), :CB] * live
                    pi = h_ref[pl.ds(tp, 1), CB:] * live
                    carry = (hr, hi, dar + hr * pr + hi * pi, dai + hi * pr - hr * pi)
                else:
                    carry = (hr, hi)
            return carry

        if rev:
            _, _, dar, dai = lax.fori_loop(0, n_grp, group, (zero, zero, zero, zero))
            da_ref[:, :CB] = dar
            da_ref[:, CB:] = dai
        else:
            lax.fori_loop(0, n_grp, group, (zero, zero))

    big = pl.BlockSpec((S, 2 * CB), lambda c: (0, c))
    vec = pl.BlockSpec((1, 2 * CB), lambda c: (0, c))
    n_blk = x.shape[1] // (2 * CB)
    if rev:
        return _pcall(body, name=name, grid=(n_blk,), in_specs=[big, vec, big], out_specs=[big, vec],
                      out_shape=[jax.ShapeDtypeStruct(x.shape, F32), jax.ShapeDtypeStruct(a.shape, F32)],
                      compiler_params=_params(("parallel",)))(x, a, h)
    return _pcall(body, name=name, grid=(n_blk,), in_specs=[big, vec], out_specs=big,
                  out_shape=jax.ShapeDtypeStruct(x.shape, F32), compiler_params=_params(("parallel",)))(x, a)


def _ssm_prep1(lr_, li, ldt):
    lr = jnp.minimum(lr_, -1e-4)
    dt = jnp.exp(ldt)
    mag = jnp.exp(lr * dt)
    ar = mag * jnp.cos(li * dt)
    ai = mag * jnp.sin(li * dt)
    den = lr * lr + li * li
    gr = ((ar - 1.0) * lr + ai * li) / den
    gi = (ai * lr - (ar - 1.0) * li) / den
    return ar, ai, gr, gi


def _ssm_prep2(gr, gi, br, bi):
    return gr * br - gi * bi, gr * bi + gi * br


def _vjp_of(fn, n_in):
    def bwd(*args):
        cts = args[n_in:]
        return jax.vjp(fn, *args[:n_in])[1](cts[0] if len(cts) == 1 else tuple(cts))
    return bwd


def _to_blocked(r, i):
    lead = r.shape[:-1]
    t = jnp.stack([r.reshape(lead + (N_SSM_CH // SCAN_CB, SCAN_CB)), i.reshape(lead + (N_SSM_CH // SCAN_CB, SCAN_CB))],
                  axis=-2)
    return t.reshape(lead + (2 * N_SSM_CH,))


def _from_blocked(m):
    lead = m.shape[:-1]
    t = m.reshape(lead + (N_SSM_CH // SCAN_CB, 2, SCAN_CB))
    return t[..., 0, :].reshape(lead + (N_SSM_CH,)), t[..., 1, :].reshape(lead + (N_SSM_CH,))


def _blockdiag(r, i):
    eye = jnp.eye(SSM_GROUPS, dtype=F32)
    bd = lambda t: jnp.einsum("gcp,gh->gchp", t, eye).reshape(SSM_WIDTH, N_SSM_CH)
    return _to_blocked(bd(r), bd(i))


def _blockdiag_inv(m):
    eye = jnp.eye(SSM_GROUPS, dtype=F32)
    r, i = _from_blocked(m)
    diag = lambda t: jnp.einsum("gchp,gh->gcp", t.reshape(SSM_GROUPS, SSM_GROUP, SSM_GROUPS, SSM_STATE), eye)
    return diag(r), diag(i)


def _shift_rows(x, n):
    S = x.shape[0]
    row = lax.broadcasted_iota(jnp.int32, x.shape, 0)
    if n > 0:
        return jnp.where(row >= n, pltpu.roll(x, n, 0), 0.0)
    return jnp.where(row < S + n, pltpu.roll(x, S + n, 0), 0.0)


def _bf(x):
    return x.astype(BF16).astype(F32)


def _conv_pre(a, w, b):
    ab, wb = _bf(a), _bf(w)
    return wb[2:3] * ab + wb[1:2] * _shift_rows(ab, 1) + wb[0:1] * _shift_rows(ab, 2) + b


def _ffn_mid(up, conv_w, conv_b, dh=None, *, name):
    S = up.shape[0]
    tn = LANE
    nb = D_FF_P // tn
    rev = dh is not None

    def body(*refs):
        if not rev:
            a_ref, g_ref, w_ref, b_ref, o_ref = refs
            o_ref[...] = jax.nn.gelu(_conv_pre(a_ref[...], w_ref[...], b_ref[...])) * g_ref[...]
            return
        a_ref, g_ref, w_ref, b_ref, dh_ref, dup_a_ref, dup_g_ref, dw_ref, db_ref = refs
        a, w, dh_ = a_ref[...], w_ref[...], dh_ref[...]
        pre = _conv_pre(a, w, b_ref[...])
        gl, gelu_vjp = jax.vjp(jax.nn.gelu, pre)
        dup_g_ref[...] = dh_ * gl
        (dpre,) = gelu_vjp(dh_ * g_ref[...])
        db_ref[...] = jnp.sum(dpre, axis=0, keepdims=True)
        dpb, ab, wb = _bf(dpre), _bf(a), _bf(w)
        dup_a_ref[...] = wb[2:3] * dpb + wb[1:2] * _shift_rows(dpb, -1) + wb[0:1] * _shift_rows(dpb, -2)
        dw_ref[2:3, :] = jnp.sum(dpb * ab, axis=0, keepdims=True)
        dw_ref[1:2, :] = jnp.sum(dpb * _shift_rows(ab, 1), axis=0, keepdims=True)
        dw_ref[0:1, :] = jnp.sum(dpb * _shift_rows(ab, 2), axis=0, keepdims=True)

    a_spec = pl.BlockSpec((S, tn), lambda j: (0, j))
    g_spec = pl.BlockSpec((S, tn), lambda j: (0, j + nb))
    w_spec = pl.BlockSpec((3, tn), lambda j: (0, j))
    b_spec = pl.BlockSpec((1, tn), lambda j: (0, j))
    if not rev:
        return _pcall(body, name=name, grid=(nb,), in_specs=[a_spec, g_spec, w_spec, b_spec], out_specs=a_spec,
                      out_shape=jax.ShapeDtypeStruct((S, D_FF_P), F32), compiler_params=_params(("parallel",)))(
                          up, up, conv_w, conv_b)
    return _pcall(body, name=name, grid=(nb,), in_specs=[a_spec, g_spec, w_spec, b_spec, a_spec],
                  out_specs=[a_spec, a_spec, w_spec, b_spec],
                  out_shape=[jax.ShapeDtypeStruct((S, D_FF_P), F32), jax.ShapeDtypeStruct((S, D_FF_P), F32),
                             jax.ShapeDtypeStruct((3, D_FF_P), F32), jax.ShapeDtypeStruct((1, D_FF_P), F32)],
                  compiler_params=_params(("parallel",)))(up, up, conv_w, conv_b, dh)


def _ff_pad(t):
    lead = t.shape[:-1]
    t = t.reshape(lead + (N_DEV, FF_HALF))
    return jnp.pad(t, [(0, 0)] * len(lead) + [(0, 0), (0, FF_HALF_P - FF_HALF)]).reshape(lead + (D_FF_P,))


def _ff_unpad(t):
    lead = t.shape[:-1]
    return t.reshape(lead + (N_DEV, FF_HALF_P))[..., :FF_HALF].reshape(lead + (D_FF,))


def _ln_fwd(x, h, g, b):
    r = DN_ALPHA * x + h
    mu = jnp.mean(r, axis=-1, keepdims=True)
    var = jnp.mean(jnp.square(r - mu), axis=-1, keepdims=True)
    return r, (r - mu) * lax.rsqrt(var + LN_EPS) * g + b


def _ln_bwd(r, dy, g):
    mu = jnp.mean(r, axis=-1, keepdims=True)
    var = jnp.mean(jnp.square(r - mu), axis=-1, keepdims=True)
    xhat = (r - mu) * lax.rsqrt(var + LN_EPS)
    dxh = dy * g
    dr = lax.rsqrt(var + LN_EPS) * (dxh - jnp.mean(dxh, axis=-1, keepdims=True)
                                    - xhat * jnp.mean(dxh * xhat, axis=-1, keepdims=True))
    return dr, jnp.sum(dy * xhat, axis=0, keepdims=True), jnp.sum(dy, axis=0, keepdims=True)


def _merge(gf, gs, gc, ya, yb2, yc):
    yb = yb2[:, :D_MODEL] * jax.nn.sigmoid(yb2[:, D_MODEL:])
    return jax.nn.sigmoid(gf) * ya + jax.nn.sigmoid(gs) * yb + jax.nn.sigmoid(gc) * yc


def _s5_tail(hc, su, d):
    return jax.nn.gelu(hc + d * su)


def _s5_tail_bwd(hc, su, dgel, d):
    _, vjp = jax.vjp(jax.nn.gelu, hc + d * su)
    (dy,) = vjp(dgel)
    return dy, d * dy, jnp.sum(dy * su, axis=0, keepdims=True)


def _loss_rows(y, tgt):
    err = y - tgt
    return err * (1.0 / D_MODEL), jnp.sum(0.5 * jnp.square(err), axis=0, keepdims=True) * (1.0 / D_MODEL)


def _peer(k):
    x, y, c = lax.axis_index("x"), lax.axis_index("y"), lax.axis_index("c")
    return (x ^ ((k >> 2) & 1), y ^ ((k >> 1) & 1), c ^ (k & 1))


def _my_slot():
    return 4 * lax.axis_index("x") + 2 * lax.axis_index("y") + lax.axis_index("c")


def _peer_slot(k):
    px, py, pc = _peer(k)
    return 4 * px + 2 * py + pc


N_CHIP = N_DEV // 2
OTHER_CHIPS = (2, 4, 6)


def _chip_of(dev):
    return 2 * dev[0] + dev[1]


def _remote(src, dst, send, recv, dev):
    return pltpu.make_async_remote_copy(src_ref=src, dst_ref=dst, send_sem=send, recv_sem=recv, device_id=dev,
                                        device_id_type=pl.DeviceIdType.MESH)


def _all_gather(shards, *, name):
    n = len(shards)

    def body(*refs):
        ins, outs = refs[:n], refs[n:2 * n]
        send, recv, loc = refs[2 * n:]
        me = _my_slot()
        sibling = _peer(1)
        local = [pltpu.make_async_copy(ins[t], outs[t].at[me], loc.at[t]) for t in range(n)]
        for lc in local:
            lc.start()
        first = [_remote(ins[t], outs[t].at[me], send.at[t, k - 1], recv.at[t, k - 1], _peer(k))
                 for k in (1,) + OTHER_CHIPS for t in range(n)]
        for cp in first:
            cp.start()
        passed = []
        for k in OTHER_CHIPS:
            for t in range(n):
                slot = outs[t].at[_peer_slot(k)]
                _remote(ins[t], slot, send.at[t, k - 1], recv.at[t, k - 1], _peer(k)).wait_recv()
                cp = _remote(slot, slot, send.at[t, k], recv.at[t, k], sibling)
                cp.start()
                passed.append(cp)
        for t in range(n):
            _remote(ins[t], outs[t].at[_peer_slot(1)], send.at[t, 0], recv.at[t, 0], sibling).wait_recv()
            for k in OTHER_CHIPS:
                _remote(ins[t], outs[t].at[_peer_slot(k + 1)], send.at[t, k], recv.at[t, k], sibling).wait_recv()
        for cp in first + passed:
            cp.wait_send()
        for lc in local:
            lc.wait()

    hbm = pl.BlockSpec(memory_space=pl.ANY)
    return _pcall(
        body, name=name, in_specs=[hbm] * n, out_specs=[hbm] * n,
        out_shape=[jax.ShapeDtypeStruct((N_DEV,) + s.shape, s.dtype) for s in shards],
        scratch_shapes=[pltpu.SemaphoreType.DMA((n, N_DEV - 1)), pltpu.SemaphoreType.DMA((n, N_DEV - 1)),
                        pltpu.SemaphoreType.DMA((n,))],
    )(*shards)


def _sibling_swap(grads, *, name):
    n = len(grads)

    def body(*refs):
        ins, outs = refs[:n], refs[n:2 * n]
        send, recv = refs[2 * n:]
        c = lax.axis_index("c")
        copies = [_remote(ins[t].at[:, 1 - c], outs[t], send.at[t], recv.at[t], _peer(1)) for t in range(n)]
        for cp in copies:
            cp.start()
        for cp in copies:
            cp.wait()

    hbm = pl.BlockSpec(memory_space=pl.ANY)
    return _pcall(
        body, name=name, in_specs=[hbm] * n, out_specs=[hbm] * n,
        out_shape=[jax.ShapeDtypeStruct((N_CHIP,) + g.shape[2:], g.dtype) for g in grads],
        scratch_shapes=[pltpu.SemaphoreType.DMA((n,)), pltpu.SemaphoreType.DMA((n,))],
    )(*grads)


def _pair_add(g, p, out_dtype, *, name):
    _, _, R, C = g.shape
    tr = _pick(R, (128, 64, 32, 16, 8))

    def body(c_ref, g_ref, p_ref, o_ref):
        o_ref[...] = (g_ref[...] + p_ref[...]).astype(out_dtype)

    grid_spec = pltpu.PrefetchScalarGridSpec(
        num_scalar_prefetch=1, grid=(N_CHIP, R // tr),
        in_specs=[pl.BlockSpec((None, None, tr, C), lambda j, i, c_ref: (j, c_ref[0], i, 0)),
                  pl.BlockSpec((None, tr, C), lambda j, i, c_ref: (j, i, 0))],
        out_specs=pl.BlockSpec((None, tr, C), lambda j, i, c_ref: (j, i, 0)))
    core = lax.axis_index("c").astype(jnp.int32).reshape(1)
    return _pcall(body, name=name, grid_spec=grid_spec, out_shape=jax.ShapeDtypeStruct(p.shape, out_dtype),
                  compiler_params=_params(("parallel", "parallel")))(core, g, p)


def _chip_exchange(sums, *, name):
    n = len(sums)

    def body(*refs):
        ins, outs = refs[:n], refs[n:2 * n]
        send, recv, loc = refs[2 * n:]
        mine = 2 * lax.axis_index("x") + lax.axis_index("y")
        local = [pltpu.make_async_copy(ins[t].at[mine], outs[t].at[mine], loc.at[t]) for t in range(n)]
        for lc in local:
            lc.start()
        copies = []
        for k in OTHER_CHIPS:
            for t in range(n):
                theirs = _chip_of(_peer(k))
                cp = _remote(ins[t].at[theirs], outs[t].at[mine], send.at[t, k // 2 - 1], recv.at[t, k // 2 - 1], _peer(k))
                cp.start()
                copies.append(_remote(ins[t].at[theirs], outs[t].at[theirs], send.at[t, k // 2 - 1],
                                      recv.at[t, k // 2 - 1], _peer(k)))
        for cp in copies:
            cp.wait()
        for lc in local:
            lc.wait()

    hbm = pl.BlockSpec(memory_space=pl.ANY)
    return _pcall(
        body, name=name, in_specs=[hbm] * n, out_specs=[hbm] * n,
        out_shape=[jax.ShapeDtypeStruct(s.shape, s.dtype) for s in sums],
        scratch_shapes=[pltpu.SemaphoreType.DMA((n, N_CHIP - 1)), pltpu.SemaphoreType.DMA((n, N_CHIP - 1)),
                        pltpu.SemaphoreType.DMA((n,))],
    )(*sums)


def _reduce_scatter(grads, wire_dtypes):
    pairs = [g.reshape((N_CHIP, 2) + g.shape[1:]) for g in grads]
    partner = _sibling_swap(pairs, name="grad_sibling_swap")
    sums = [_pair_add(g, p, dt, name="grad_pair_add") for g, p, dt in zip(pairs, partner, wire_dtypes)]
    return _chip_exchange(sums, name="grad_chip_exchange")


def _adamw(recv, w, m, v, layer=None, *, name):
    n_slots, R, C = recv.shape
    tr = _pick(R, (128, 64, 32, 16, 8))

    def body(r_ref, w_ref, m_ref, v_ref, g_ref, d_ref, nm_ref, nv_ref):
        g = r_ref[0].astype(F32)
        for s in range(1, n_slots):
            g = g + r_ref[s].astype(F32)
        m_new = ADAM_B1 * m_ref[...] + (1.0 - ADAM_B1) * g
        v_new = ADAM_B2 * v_ref[...] + (1.0 - ADAM_B2) * jnp.square(g)
        m_hat = m_new / (1.0 - ADAM_B1 ** ADAM_STEP)
        v_hat = v_new / (1.0 - ADAM_B2 ** ADAM_STEP)
        g_ref[...] = g
        d_ref[...] = -ADAM_LR * (m_hat / (jnp.sqrt(v_hat) + ADAM_EPS) + ADAM_WD * w_ref[...])
        nm_ref[...] = m_new
        nv_ref[...] = v_new

    row = pl.BlockSpec((tr, C), lambda i: (i, 0))
    state = row if layer is None else pl.BlockSpec((None, tr, C), lambda i: (layer, i, 0))
    return _pcall(
        body, name=name, grid=(R // tr,),
        in_specs=[pl.BlockSpec((n_slots, tr, C), lambda i: (0, i, 0)), state, state, state],
        out_specs=[row] * 4, out_shape=[jax.ShapeDtypeStruct((R, C), F32)] * 4,
        compiler_params=_params(("parallel",)),
    )(recv, w, m, v)


def _flat_pad(parts, rows):
    flat = jnp.concatenate([p.reshape(-1) for p in parts])
    return jnp.pad(flat, (0, rows * LANES - flat.shape[0])).reshape(rows, LANES)


def _small_shard_shape(n):
    return SMALL[n][:-1] + (SMALL[n][-1] // N_DEV,)


def _unpack_small(gathered):
    out, off = {}, 0
    flat = gathered.reshape(N_DEV, -1)
    for n in SMALL:
        r, c = _small_shard_shape(n)
        out[n] = flat[:, off:off + r * c].reshape(N_DEV, r, c).transpose(1, 0, 2).reshape(r, N_DEV * c)
        off += r * c
    return out


def _pack_state(state, prefix, layer):
    return _flat_pad([state[prefix + n][layer] for n in (*SMALL, *REPL)], PACK_ROWS)


def _pack_small_grads(grads):
    cols = []
    for n in SMALL:
        r, c = _small_shard_shape(n)
        cols.append(grads[n].reshape(r, N_DEV, c).transpose(1, 0, 2).reshape(N_DEV, r * c))
    cols += [jnp.broadcast_to(grads[n].reshape(1, -1), (N_DEV, grads[n].size)) for n in REPL]
    flat = jnp.concatenate(cols, axis=1)
    return jnp.pad(flat, ((0, 0), (0, PACK_ROWS * LANES - flat.shape[1]))).reshape(N_DEV, PACK_ROWS, LANES)


def _unpack_state(flat):
    out, off = {}, 0
    flat = flat.reshape(-1)
    for n, shape in [(n, _small_shard_shape(n)) for n in SMALL] + list(REPL.items()):
        sz = math.prod(shape)
        out[n] = flat[off:off + sz].reshape(shape)
        off += sz
    return out


def _pad_b_in(b):
    parts, pos = [], 0
    for src, width, dst in Z_PIECES:
        parts += [jnp.zeros((1, dst - pos), b.dtype), b[:, src:src + width]]
        pos = dst + width
    return jnp.concatenate(parts + [jnp.zeros((1, Z_W - pos), b.dtype)], axis=1)


def _unpad_b_in(bp):
    return jnp.concatenate([bp[:, dst:dst + width] for _, width, dst in Z_PIECES], axis=1)


def _heads(t, n_heads):
    S = t.shape[0]
    return t.astype(BF16).reshape(S, n_heads, -1).transpose(1, 0, 2)


def _unheads(t):
    H, S, D = t.shape
    return t.transpose(1, 0, 2).reshape(S, H * D)


def _up_shard_pad(t):
    lead = t.shape[:-1]
    t = jnp.pad(t.reshape(lead + (2, FF_HALF)), [(0, 0)] * len(lead) + [(0, 0), (0, FF_HALF_P - FF_HALF)])
    return t.reshape(lead + (2 * FF_HALF_P,))


def _up_shard_unpad(t):
    lead = t.shape[:-1]
    return t.reshape(lead + (2, FF_HALF_P))[..., :FF_HALF].reshape(lead + (2 * FF_HALF,))


def _pad_shard(n, t):
    if n == "w_in":
        return jnp.pad(t, ((0, 0), (0, W_IN_SLOT - W_IN_SHARD)))
    if n == "ffn_w_up":
        return _up_shard_pad(t)
    if n == "ffn_w_down":
        return jnp.pad(t, ((0, FF_HALF_P - FF_HALF), (0, 0)))
    return t


def _unpad_shard(n, t):
    if n == "w_in":
        return t[:, :W_IN_SHARD]
    if n == "ffn_w_up":
        return _up_shard_unpad(t)
    if n == "ffn_w_down":
        return t[:FF_HALF]
    return t


def _gather_layer(state, layer):
    shards = [_pad_shard(n, state[n][layer].astype(BF16)) for n in BIG]
    shards.append(_flat_pad([state[n][layer] for n in SMALL], SMALL_ROWS))
    *big, small = _all_gather(shards, name="all_gather_weights")
    W = dict(zip(BIG, big))
    for n in ("w_o", "xa_wq", "xa_wo", "ffn_w_down"):
        W[n] = W[n].reshape(-1, D_MODEL)
    W["w_in_p"] = _colmap(W.pop("w_in"), inverse=False, name="w_in_colmap")
    W.update(_unpack_small(small))
    return W


def _ssm_params(p):
    prep1_in = (p["ssm_lambda_re"], p["ssm_lambda_im"], p["ssm_log_dt"][:, None])
    ar, ai, gr, gi = _whole(_ssm_prep1, *prep1_in, name="ssm_prep1")
    to_cn = lambda b: b.transpose(2, 0, 1).reshape(SSM_GROUP, N_SSM_CH)
    prep2_in = (gr.reshape(1, N_SSM_CH), gi.reshape(1, N_SSM_CH), to_cn(p["ssm_b_re"]), to_cn(p["ssm_b_im"]))
    bbr, bbi = _whole(_ssm_prep2, *prep2_in, name="ssm_prep2")
    to_gcp = lambda t: t.reshape(SSM_GROUP, SSM_GROUPS, SSM_STATE).transpose(1, 0, 2)
    bb = _blockdiag(to_gcp(bbr), to_gcp(bbi))
    cct = _blockdiag(p["ssm_c_re"], -p["ssm_c_im"])
    a_vec = _to_blocked(ar.reshape(1, N_SSM_CH), ai.reshape(1, N_SSM_CH))
    return dict(bb=bb, cct=cct, a_vec=a_vec, prep1_in=prep1_in, prep2_in=prep2_in)


def _layer_fwd(x, mem, p, W):
    sp = _ssm_params(p)
    z = _mm(x, W["w_in_p"], bias=_pad_b_in(p["b_in"][None, :]), name="mm_in")
    zh = lambda off: _heads(z[:, off:off + 512], 8)
    f_t = z[:, Z_FF:Z_FF + FOX_HEADS].T
    cum = _cum_heads(f_t, name="fox_cum")
    ya_h, lse_a = _attn_fwd("fox", zh(Z_FQ), zh(Z_FK), zh(Z_FV), (cum[:, :, None], cum[:, None, :]), name="fox_fwd")
    ya_pre = _unheads(ya_h)
    ya = _mm(ya_pre, W["w_fox_o"], b_slots=True, name="mm_fox_o")
    x_ri = _mm(z, sp["bb"], a_off=Z_SU, a_cols=SSM_WIDTH, name="mm_s5_in")
    h_ri = _scan(x_ri, sp["a_vec"], name="s5_scan")
    hc = _mm(h_ri, sp["cct"], tb=True, name="mm_s5_out")
    d_row = p["ssm_d"][None, :]
    (gel,) = _rowwise(lambda a, b, c: (_s5_tail(a, b, c),), [hc, Win(z, 512, Z_SU // 512)], [d_row], name="s5_tail")
    yb2 = _mm(gel, W["w_ssm_glu"], b_slots=True, name="mm_glu")
    bias = _relbias_expand(W["ca_rel_bias"]).transpose(1, 0, 2)
    padk = lambda t: jnp.pad(t, ((0, 0), (CA_PAD, 0), (0, 0)))
    yc_h, lse_c = _attn_fwd("chunk", zh(Z_CQ), padk(zh(Z_CK)), padk(zh(Z_CV)), (bias,), name="chunk_fwd")
    yc_pre = _unheads(yc_h)
    yc = _mm(yc_pre, W["w_ca_o"], b_slots=True, name="mm_ca_o")
    gates = [Win(z, 1024, Z_GF // 1024), Win(z, 1024, Z_GS // 1024), Win(z, 1024, Z_GC // 1024)]
    (merged,) = _rowwise(lambda *a: (_merge(*a),), gates + [ya, yb2, yc], name="merge")
    h1 = _mm(merged, W["w_o"], name="mm_o")
    ln_g, ln_b = W["ln_g"], W["ln_b"]
    r1, x1 = _rowwise(_ln_fwd, [x, h1], [ln_g[0:1], ln_b[0:1]], name="ln_fwd")
    q = _mm(x1, W["xa_wq"], name="mm_xq")
    kv = _mm(mem, W["xa_wkv"], b_slots=True, name="mm_xkv")
    qh, kh, vh = _heads(q, XA_HEADS), _heads(kv[:, :D_MODEL], XA_HEADS), _heads(kv[:, D_MODEL:], XA_HEADS)
    o_h, lse_x = _attn_fwd("xa", qh, kh, vh, name="xa_fwd")
    o = _unheads(o_h)
    h2 = _mm(o, W["xa_wo"], name="mm_xo")
    r2, x2 = _rowwise(_ln_fwd, [x1, h2], [ln_g[1:2], ln_b[1:2]], name="ln_fwd")
    up = _mm(x2, W["ffn_w_up"], b_slots=True, name="mm_up")
    hmid = _ffn_mid(up, _ff_pad(W["ffn_conv_w"]), _ff_pad(p["ffn_conv_b"][None, :]), name="ffn_mid")
    h3 = _mm(hmid, W["ffn_w_down"], name="mm_down")
    r3, x3 = _rowwise(_ln_fwd, [x2, h3], [ln_g[2:3], ln_b[2:3]], name="ln_fwd")
    res = dict(x=x, z=z, cum=cum, lse_a=lse_a, ya_pre=ya_pre, ya=ya, h_ri=h_ri, hc=hc, gel=gel, yb2=yb2, lse_c=lse_c,
               yc_pre=yc_pre, yc=yc, merged=merged, r1=r1, x1=x1, q=q, kv=kv, o=o, lse_x=lse_x, r2=r2, x2=x2, up=up,
               hmid=hmid, r3=r3, bias=bias, W=W)
    return x3, res


def _layer_bwd(dx3, mem, p, res):
    W = res["W"]
    x, z = res["x"], res["z"]
    sp = _ssm_params(p)
    ln_g = W["ln_g"]
    big, small = {}, {}
    slots = lambda t: t.reshape(N_DEV, -1, D_MODEL)
    dr3, dg2, db2 = _rowwise(_ln_bwd, [res["r3"], dx3], [ln_g[2:3]], n_red=2, name="ln_bwd")
    dhmid = _mm(dr3, W["ffn_w_down"], tb=True, name="mm_down_dx")
    big["ffn_w_down"] = slots(_mm(res["hmid"], dr3, ta=True, name="mm_down_dw"))
    conv_w_p, conv_b_p = _ff_pad(W["ffn_conv_w"]), _ff_pad(p["ffn_conv_b"][None, :])
    dup_a, dup_g, dcw, dcb = _ffn_mid(res["up"], conv_w_p, conv_b_p, dhmid, name="ffn_mid_bwd")
    dup = jnp.concatenate([dup_a, dup_g], axis=1)
    small["ffn_conv_w"], small["ffn_conv_b"] = _ff_unpad(dcw), _ff_unpad(dcb)[0]
    dx2 = _mm(dup, W["ffn_w_up"], tb=True, b_slots=True, add=(dr3, DN_ALPHA), name="mm_up_dx")
    big["ffn_w_up"] = _mm(res["x2"], dup, ta=True, out_slots=2 * FF_HALF_P, name="mm_up_dw")
    dr2, dg1, db1 = _rowwise(_ln_bwd, [res["r2"], dx2], [ln_g[1:2]], n_red=2, name="ln_bwd")
    do = _mm(dr2, W["xa_wo"], tb=True, name="mm_xo_dx")
    big["xa_wo"] = slots(_mm(res["o"], dr2, ta=True, name="mm_xo_dw"))
    kv = res["kv"]
    qh, kh, vh = _heads(res["q"], XA_HEADS), _heads(kv[:, :D_MODEL], XA_HEADS), _heads(kv[:, D_MODEL:], XA_HEADS)
    dqh, dkh, dvh = _attn_bwd("xa", qh, kh, vh, res["lse_x"], _heads(do, XA_HEADS), name="xa_bwd")
    dq = _unheads(dqh)
    dkv = jnp.concatenate([_unheads(dkh), _unheads(dvh)], axis=1)
    dx1 = _mm(dq, W["xa_wq"], tb=True, add=(dr2, DN_ALPHA), name="mm_xq_dx")
    big["xa_wq"] = slots(_mm(res["x1"], dq, ta=True, name="mm_xq_dw"))
    big["xa_wkv"] = _mm(mem, dkv, ta=True, out_slots=256, name="mm_xkv_dw")
    dr1, dg0, db0 = _rowwise(_ln_bwd, [res["r1"], dx1], [ln_g[0:1]], n_red=2, name="ln_bwd")
    small["ln_g"] = jnp.concatenate([dg0, dg1, dg2], axis=0)
    small["ln_b"] = jnp.concatenate([db0, db1, db2], axis=0)
    dmerged = _mm(dr1, W["w_o"], tb=True, name="mm_o_dx")
    big["w_o"] = slots(_mm(res["merged"], dr1, ta=True, name="mm_o_dw"))
    gates = [Win(z, 1024, Z_GF // 1024), Win(z, 1024, Z_GS // 1024), Win(z, 1024, Z_GC // 1024)]
    dgf, dgs, dgc, dya, dyb2, dyc = _rowwise(_vjp_of(_merge, 6), gates + [res["ya"], res["yb2"], res["yc"], dmerged],
                                             name="merge_bwd")
    zh = lambda off: _heads(z[:, off:off + 512], 8)
    dya_pre = _mm(dya, W["w_fox_o"], tb=True, b_slots=True, name="mm_fox_o_dx")
    big["w_fox_o"] = _mm(res["ya_pre"], dya, ta=True, out_slots=128, name="mm_fox_o_dw")
    cum = res["cum"]
    dfq, dfk, dfv, dcq, dck = _attn_bwd("fox", zh(Z_FQ), zh(Z_FK), zh(Z_FV), res["lse_a"], _heads(dya_pre, 8),
                                        (cum[:, :, None], cum[:, None, :]), name="fox_bwd")
    f_t = z[:, Z_FF:Z_FF + FOX_HEADS].T
    dff = _cum_heads(f_t, dcq[:, :, 0] + dck[:, 0, :], name="fox_cum_bwd")
    dgel = _mm(dyb2, W["w_ssm_glu"], tb=True, b_slots=True, name="mm_glu_dx")
    big["w_ssm_glu"] = _mm(res["gel"], dyb2, ta=True, out_slots=256, name="mm_glu_dw")
    d_row = p["ssm_d"][None, :]
    su_win = Win(z, 512, Z_SU // 512)
    dy, dsu1, dd = _rowwise(_s5_tail_bwd, [res["hc"], su_win, dgel], [d_row], n_red=1, name="s5_tail_bwd")
    small["ssm_d"] = dd[0]
    dh_ri = _mm(dy, sp["cct"], name="mm_s5_out_dx")
    dcct = _mm(dy, res["h_ri"], ta=True, name="mm_s5_out_dw")
    dx_ri, da_vec = _scan(dh_ri, sp["a_vec"], res["h_ri"], name="s5_scan_bwd")
    dsu = _mm(dx_ri, sp["bb"], tb=True, add=(dsu1, 1.0), name="mm_s5_in_dx")
    dbb = _mm(z, dx_ri, ta=True, a_off=Z_SU, a_cols=SSM_WIDTH, name="mm_s5_in_dw")
    dcr, dci = _blockdiag_inv(dcct)
    small["ssm_c_re"], small["ssm_c_im"] = dcr, -dci
    dbbr, dbbi = _blockdiag_inv(dbb)
    to_cn = lambda t: t.transpose(1, 0, 2).reshape(SSM_GROUP, N_SSM_CH)
    dgr, dgi, dbr, dbi = _whole(_vjp_of(_ssm_prep2, 4), *sp["prep2_in"], to_cn(dbbr), to_cn(dbbi), name="ssm_prep2_bwd")
    from_cn = lambda t: t.reshape(SSM_GROUP, SSM_GROUPS, SSM_STATE).transpose(1, 2, 0)
    small["ssm_b_re"], small["ssm_b_im"] = from_cn(dbr), from_cn(dbi)
    dar, dai = _from_blocked(da_vec)
    sq = lambda t: t.reshape(SSM_GROUPS, SSM_STATE)
    dlr, dli, dldt = _whole(_vjp_of(_ssm_prep1, 3), *sp["prep1_in"], sq(dar), sq(dai), sq(dgr), sq(dgi),
                            name="ssm_prep1_bwd")
    small["ssm_lambda_re"], small["ssm_lambda_im"], small["ssm_log_dt"] = dlr, dli, dldt[:, 0]
    dyc_pre = _mm(dyc, W["w_ca_o"], tb=True, b_slots=True, name="mm_ca_o_dx")
    big["w_ca_o"] = _mm(res["yc_pre"], dyc, ta=True, out_slots=128, name="mm_ca_o_dw")
    bias = res["bias"]
    padk = lambda t: jnp.pad(t, ((0, 0), (CA_PAD, 0), (0, 0)))
    dcqh, dckh, dcvh, dbias = _attn_bwd("chunk", zh(Z_CQ), padk(zh(Z_CK)), padk(zh(Z_CV)), res["lse_c"],
                                        _heads(dyc_pre, 8), (bias,), name="chunk_bwd")
    small["ca_rel_bias"] = _relbias_reduce(dbias.transpose(1, 0, 2))
    dff_p = jnp.pad(dff.T, ((0, 0), (0, 512 - FOX_HEADS)))
    dz = jnp.concatenate([_unheads(dfq), _unheads(dfk), _unheads(dfv), dff_p, dsu, _unheads(dcqh),
                          _unheads(dckh[:, CA_PAD:]), _unheads(dcvh[:, CA_PAD:]), dgf, dgs, dgc], axis=1)
    dx = _mm(dz, W["w_in_p"], tb=True, add=(dr1, DN_ALPHA), name="mm_in_dx")
    big["w_in"] = _colmap(_mm(x, dz, ta=True, name="mm_in_dw"), inverse=True, name="w_in_colmap_inv")
    (db_in_p,) = _rowwise(lambda t: (jnp.sum(t, axis=0, keepdims=True),), [dz], n_red=1, name="colsum")
    small["b_in"] = _unpad_b_in(db_in_p)[0]
    return dx, big, small


def kernel(x, mem, w_in, b_in, ssm_lambda_re, ssm_lambda_im, ssm_log_dt, ssm_b_re, ssm_b_im, ssm_c_re, ssm_c_im, ssm_d, ca_rel_bias, w_fox_o, w_ssm_glu, w_ca_o, w_o, xa_wq, xa_wkv, xa_wo, ffn_w_up, ffn_conv_w, ffn_conv_b, ffn_w_down, ln_g, ln_b, loss_target, m_w_in, m_b_in, m_ssm_lambda_re, m_ssm_lambda_im, m_ssm_log_dt, m_ssm_b_re, m_ssm_b_im, m_ssm_c_re, m_ssm_c_im, m_ssm_d, m_ca_rel_bias, m_w_fox_o, m_w_ssm_glu, m_w_ca_o, m_w_o, m_xa_wq, m_xa_wkv, m_xa_wo, m_ffn_w_up, m_ffn_conv_w, m_ffn_conv_b, m_ffn_w_down, m_ln_g, m_ln_b, v_w_in, v_b_in, v_ssm_lambda_re, v_ssm_lambda_im, v_ssm_log_dt, v_ssm_b_re, v_ssm_b_im, v_ssm_c_re, v_ssm_c_im, v_ssm_d, v_ca_rel_bias, v_w_fox_o, v_w_ssm_glu, v_w_ca_o, v_w_o, v_xa_wq, v_xa_wkv, v_xa_wo, v_ffn_w_up, v_ffn_conv_w, v_ffn_conv_b, v_ffn_w_down, v_ln_g, v_ln_b):
    given = dict(locals())
    state = {pre + n: given[pre + n] for n in WEIGHTS for pre in ("", "m_", "v_")}
    mem0 = mem[0]
    layer_params = [{n: state[n][l] for n in REPL} for l in range(DEPTH)]

    h, residuals = x[0], []
    for l in range(DEPTH):
        h, res = _layer_fwd(h, mem0, layer_params[l], _gather_layer(state, l))
        residuals.append(res)
    dh, loss_cols = _rowwise(_loss_rows, [h, loss_target[0]], n_red=1, name="loss")
    loss = lax.psum(jnp.sum(loss_cols), ("x", "y", "c"))

    outs = [None] * DEPTH
    for l in reversed(range(DEPTH)):
        dh, big, small = _layer_bwd(dh, mem0, layer_params[l], residuals[l])
        *recv_big, recv_small = _reduce_scatter([big[n] for n in BIG] + [_pack_small_grads(small)],
                                                [BF16] * len(BIG) + [F32])
        layer_out = {}
        for n, recv in zip(BIG, recv_big):
            if recv.shape[1:] == BIG[n][0]:
                res4 = _adamw(recv, state[n], state["m_" + n], state["v_" + n], l, name="adamw_" + n)
            else:
                padded = [_pad_shard(n, state[pre + n][l]) for pre in ("", "m_", "v_")]
                res4 = [_unpad_shard(n, t) for t in _adamw(recv, *padded, name="adamw_" + n)]
            layer_out[n] = res4
        packed = _adamw(recv_small, *[_pack_state(state, pre, l) for pre in ("", "m_", "v_")], name="adamw_small")
        for n, t4 in zip((*SMALL, *REPL), zip(*[_unpack_state(t).values() for t in packed])):
            layer_out[n] = t4
        outs[l] = layer_out

    stacked = lambda n, j: jnp.stack([outs[l][n][j] for l in range(DEPTH)])
    return (loss, dh[None], *[stacked(n, j) for j in range(4) for n in WEIGHTS])
```

```python
import functools
import math

import jax
import jax.numpy as jnp
from jax import lax
from jax.experimental import pallas as pl
from jax.experimental.pallas import tpu as pltpu

F32, BF16 = jnp.float32, jnp.bfloat16

D_MODEL = 1024
DEPTH = 4
CHUNK = 64
FOX_HEADS, FOX_HEAD_DIM, FOX_WIDTH = 8, 64, 512
SSM_GROUP, SSM_WIDTH, SSM_GROUPS, SSM_STATE = 16, 512, 32, 64
CA_HEADS, CA_HEAD_DIM, CA_WIDTH, CA_LEFT_CHUNKS = 8, 64, 512, 8
CA_BAND = (CA_LEFT_CHUNKS + 1) * CHUNK
CA_PAD = CA_LEFT_CHUNKS * CHUNK
REL_MIN, REL_MAX = -(CHUNK - 1), 4 * CHUNK
N_REL = REL_MAX - REL_MIN + 1
XA_HEADS, XA_HEAD_DIM = 4, 256
D_FF = 2816
DN_ALPHA = (2 * DEPTH) ** 0.25
LN_EPS = 1e-5
NEG_INF = -1e30
ADAM_LR, ADAM_B1, ADAM_B2, ADAM_EPS, ADAM_WD, ADAM_STEP = 0.001, 0.9, 0.999, 1e-08, 0.01, 10

N_DEV = 8
LANE = 128
N_SSM_CH = SSM_GROUPS * SSM_STATE
SCAN_CB = 256
N_IN = 6664
W_IN_SHARD, W_IN_SLOT = N_IN // N_DEV, 896
Z_W = 7168
Z_FQ, Z_FK, Z_FV, Z_FF, Z_SU, Z_CQ, Z_CK, Z_CV, Z_GF, Z_GS, Z_GC = (
    0, 512, 1024, 1536, 2048, 2560, 3072, 3584, 4096, 5120, 6144)
Z_PIECES = ((0, 512, Z_FQ), (512, 512, Z_FK), (1024, 512, Z_FV), (1536, 8, Z_FF), (1544, 512, Z_SU),
            (2056, 512, Z_CQ), (2568, 512, Z_CK), (3080, 512, Z_CV), (3592, 1024, Z_GF), (4616, 1024, Z_GS),
            (5640, 1024, Z_GC))
FF_HALF, FF_HALF_P = D_FF // N_DEV, 384
D_FF_P = N_DEV * FF_HALF_P

VMEM_LIMIT_BYTES = 56 * 1024 * 1024

BIG = {"w_in": ((1024, W_IN_SHARD), 1), "w_fox_o": ((512, 128), 1), "w_ssm_glu": ((512, 256), 1),
       "w_ca_o": ((512, 128), 1), "w_o": ((128, 1024), 0), "xa_wq": ((128, 1024), 0), "xa_wkv": ((1024, 256), 1),
       "xa_wo": ((128, 1024), 0), "ffn_w_up": ((1024, 2 * FF_HALF), 1), "ffn_w_down": ((FF_HALF, 1024), 0)}
SMALL = {"ca_rel_bias": (8, 320), "ffn_conv_w": (3, D_FF), "ln_g": (3, 1024), "ln_b": (3, 1024)}
REPL = {"b_in": (N_IN,), "ssm_lambda_re": (32, 64), "ssm_lambda_im": (32, 64), "ssm_log_dt": (32,),
        "ssm_b_re": (32, 64, 16), "ssm_b_im": (32, 64, 16), "ssm_c_re": (32, 16, 64), "ssm_c_im": (32, 16, 64),
        "ssm_d": (512,), "ffn_conv_b": (D_FF,)}
WEIGHTS = ("w_in", "b_in", "ssm_lambda_re", "ssm_lambda_im", "ssm_log_dt", "ssm_b_re", "ssm_b_im", "ssm_c_re",
           "ssm_c_im", "ssm_d", "ca_rel_bias", "w_fox_o", "w_ssm_glu", "w_ca_o", "w_o", "xa_wq", "xa_wkv", "xa_wo",
           "ffn_w_up", "ffn_conv_w", "ffn_conv_b", "ffn_w_down", "ln_g", "ln_b")
LANES = 1024
PACK_ROWS = 256
SMALL_ROWS = 8


def _w_in_segments():
    segs = []
    for src, width, dst in Z_PIECES:
        n = src
        while n < src + width:
            d = n // W_IN_SHARD
            end = min(src + width, (d + 1) * W_IN_SHARD)
            segs.append((W_IN_SLOT * d + n - W_IN_SHARD * d, dst + n - src, end - n))
            n = end
    return tuple(segs)


W_IN_SEGS = _w_in_segments()


def _pallas(body, **kw):
    return pl.pallas_call(body, **kw)


def _params(sem):
    return pltpu.CompilerParams(dimension_semantics=sem, vmem_limit_bytes=VMEM_LIMIT_BYTES)


class _Rider:
    def __init__(self, host, ins, out_shapes, sem_shapes, start, finish, then=None):
        self.host, self.ins, self.out_shapes, self.sem_shapes = host, list(ins), list(out_shapes), list(sem_shapes)
        self.start, self.finish, self.then, self.results = start, finish, then, None


_RIDERS = []


def _pcall(body, *, name, **kw):
    rider = next((r for r in _RIDERS if r.host == name), None)
    if rider is None:
        return _pallas(body, name=name, **kw)
    _RIDERS.remove(rider)
    grid, in_specs, scratch = kw["grid"], list(kw["in_specs"]), list(kw.get("scratch_shapes", ()))
    single = not isinstance(kw["out_shape"], (list, tuple))
    out_specs = [kw["out_specs"]] if single else list(kw["out_specs"])
    out_shape = [kw["out_shape"]] if single else list(kw["out_shape"])
    n_in, n_out, n_scr = len(in_specs), len(out_specs), len(scratch)
    r_in, r_out = len(rider.ins), len(rider.out_shapes)

    def fused(*refs):
        a, ra = refs[:n_in], refs[n_in:n_in + r_in]
        o, ro = refs[n_in + r_in:n_in + r_in + n_out], refs[n_in + r_in + n_out:n_in + r_in + n_out + r_out]
        scr, sems = refs[n_in + r_in + n_out + r_out:][:n_scr], refs[n_in + r_in + n_out + r_out + n_scr:]
        ids = [pl.program_id(d) for d in range(len(grid))]
        first = functools.reduce(jnp.logical_and, [i == 0 for i in ids])
        last = functools.reduce(jnp.logical_and, [i == g - 1 for i, g in zip(ids, grid)])

        @pl.when(first)
        def _():
            rider.start(ra, ro, sems)

        body(*a, *o, *scr)

        @pl.when(last)
        def _():
            rider.finish(ra, ro, sems)

    hbm = pl.BlockSpec(memory_space=pl.ANY)
    call = _pallas(fused, name=name, grid=grid, in_specs=in_specs + [hbm] * r_in, out_specs=out_specs + [hbm] * r_out,
                   out_shape=out_shape + rider.out_shapes, scratch_shapes=scratch + rider.sem_shapes,
                   compiler_params=_params(("arbitrary",) * len(grid)))

    def run(*operands):
        outs = call(*operands, *rider.ins)
        rider.results = list(outs[n_out:])
        if rider.then is not None:
            rider.then(rider.results)
        return outs[0] if single else list(outs[:n_out])

    return run


def _pick(dim, prefs):
    for p in prefs:
        if dim % p == 0:
            return p
    return dim


def _mm(a, b, *, ta=False, tb=False, bias=None, add=None, a_off=0, a_cols=None, b_slots=False, out_slots=None,
        name, out_dtype=F32):
    a_cols = a_cols if a_cols is not None else a.shape[1]
    M, K = (a_cols, a.shape[0]) if ta else (a.shape[0], a_cols)
    tm = _pick(M, (1024, 512, 256, 128))
    if ta:
        tk = _pick(K, (1024, 512, 256))
    elif b_slots and tb:
        tk = _pick(b.shape[2], (1024, 768, 512, 256, 128))
    else:
        tk = K if K <= 3072 else _pick(K, (1024, 512, 256, 128))
    nk = K // tk
    if b_slots:
        ns = b.shape[2]
        if tb:
            N = b.shape[1]
            tn = _pick(N, (512, 256, 128))
            per = ns // tk
            b_spec = pl.BlockSpec((None, tn, tk), lambda i, j, k: (k // per, j, k % per))
            b_dim = 1
            assert N_DEV * ns == K
        else:
            N = N_DEV * ns
            tn = _pick(ns, (512, 256, 128))
            per = ns // tn
            b_spec = pl.BlockSpec((None, tk, tn), lambda i, j, k: (j // per, k, j % per))
            b_dim = 0
            assert b.shape[1] == K
    else:
        N = b.shape[0] if tb else b.shape[1]
        assert (b.shape[1] if tb else b.shape[0]) == K, (a.shape, b.shape, ta, tb)
        tn = _pick(N if out_slots is None else out_slots, (512, 256, 128))
        if tb:
            b_spec = pl.BlockSpec((tn, tk), lambda i, j, k: (j, k))
            b_dim = 1
        else:
            b_spec = pl.BlockSpec((tk, tn), lambda i, j, k: (k, j))
            b_dim = 0
    if ta:
        assert a_off % tm == 0
        a_spec = pl.BlockSpec((tk, tm), lambda i, j, k: (k, i + a_off // tm))
        a_dim = 0
    else:
        assert a_off % tk == 0
        a_spec = pl.BlockSpec((tm, tk), lambda i, j, k: (i, k + a_off // tk))
        a_dim = 1
    dims = (((a_dim,), (b_dim,)), ((), ()))
    ins, specs = [a, b], [a_spec, b_spec]
    if bias is not None:
        ins.append(bias)
        specs.append(pl.BlockSpec((1, tn), lambda i, j, k: (0, j)))
    add_scale = None
    if add is not None:
        ins.append(add[0])
        add_scale = add[1]
        specs.append(pl.BlockSpec((tm, tn), lambda i, j, k: (i, j)))
    if out_slots is None:
        out_spec = pl.BlockSpec((tm, tn), lambda i, j, k: (i, j))
        out_shape = jax.ShapeDtypeStruct((M, N), out_dtype)
    else:
        assert N == N_DEV * out_slots
        per_o = out_slots // tn
        out_spec = pl.BlockSpec((None, tm, tn), lambda i, j, k: (j // per_o, i, j % per_o))
        out_shape = jax.ShapeDtypeStruct((N_DEV, M, out_slots), out_dtype)

    def body(*refs):
        a_ref, b_ref = refs[0], refs[1]
        pos = 2
        bias_ref = add_ref = None
        if bias is not None:
            bias_ref = refs[pos]
            pos += 1
        if add is not None:
            add_ref = refs[pos]
            pos += 1
        o_ref = refs[pos]
        acc_ref = refs[pos + 1] if nk > 1 else None
        part = lax.dot_general(a_ref[...].astype(BF16), b_ref[...].astype(BF16), dims, preferred_element_type=F32)

        def finish(acc):
            if bias_ref is not None:
                acc = acc + bias_ref[...]
            if add_ref is not None:
                acc = acc + add_scale * add_ref[...]
            o_ref[...] = acc.astype(out_dtype)

        if nk == 1:
            finish(part)
        else:
            k = pl.program_id(2)

            @pl.when(k == 0)
            def _():
                acc_ref[...] = part

            @pl.when(k > 0)
            def _():
                acc_ref[...] += part

            @pl.when(k == nk - 1)
            def _():
                finish(acc_ref[...])

    return _pcall(
        body, name=name, grid=(M // tm, N // tn, nk), in_specs=specs, out_specs=out_spec, out_shape=out_shape,
        scratch_shapes=[pltpu.VMEM((tm, tn), F32)] if nk > 1 else [],
        compiler_params=_params(("parallel", "parallel", "arbitrary")),
    )(*ins)


class Win:
    def __init__(self, arr, width, blk):
        self.arr, self.width, self.blk = arr, width, blk


def _rowwise(fn, rows, vecs=(), *, n_red=0, tr=256, name):
    wins = [r if isinstance(r, Win) else Win(r, r.shape[1], 0) for r in rows]
    S = wins[0].arr.shape[0]
    tr = min(tr, S)
    tile_args = [jax.ShapeDtypeStruct((tr, w.width), w.arr.dtype) for w in wins]
    tile_args += [jax.ShapeDtypeStruct(v.shape, v.dtype) for v in vecs]
    outs = jax.eval_shape(fn, *tile_args)
    n_row = len(outs) - n_red
    specs = [pl.BlockSpec((tr, w.width), functools.partial(lambda i, b: (i, b), b=w.blk)) for w in wins]
    specs += [pl.BlockSpec(v.shape, functools.partial(lambda i, nd: (0,) * nd, nd=v.ndim)) for v in vecs]
    out_specs = [pl.BlockSpec((tr, o.shape[1]), lambda i: (i, 0)) for o in outs[:n_row]]
    out_specs += [pl.BlockSpec(o.shape, functools.partial(lambda i, nd: (0,) * nd, nd=len(o.shape))) for o in outs[n_row:]]
    out_shape = [jax.ShapeDtypeStruct((S, o.shape[1]), o.dtype) for o in outs[:n_row]]
    out_shape += [jax.ShapeDtypeStruct(o.shape, o.dtype) for o in outs[n_row:]]
    n_in = len(wins) + len(vecs)

    def body(*refs):
        res = fn(*[r[...] for r in refs[:n_in]])
        o_refs = refs[n_in:]
        for o_ref, r in zip(o_refs[:n_row], res[:n_row]):
            o_ref[...] = r.astype(o_ref.dtype)
        i = pl.program_id(0)
        for o_ref, r in zip(o_refs[n_row:], res[n_row:]):
            @pl.when(i == 0)
            def _(o_ref=o_ref, r=r):
                o_ref[...] = r

            @pl.when(i > 0)
            def _(o_ref=o_ref, r=r):
                o_ref[...] += r

    return _pcall(
        body, name=name, grid=(S // tr,), in_specs=specs, out_specs=out_specs, out_shape=out_shape,
        compiler_params=_params(("arbitrary",)),
    )(*[w.arr for w in wins], *vecs)


def _whole(fn, *arrays, name):
    outs = jax.eval_shape(fn, *arrays)
    n_in = len(arrays)

    def body(*refs):
        res = fn(*[r[...] for r in refs[:n_in]])
        for o_ref, r in zip(refs[n_in:], res):
            o_ref[...] = r

    vm = pl.BlockSpec(memory_space=pltpu.VMEM)
    return _pcall(body, name=name, in_specs=[vm] * n_in, out_specs=[vm] * len(outs),
                  out_shape=[jax.ShapeDtypeStruct(o.shape, o.dtype) for o in outs])(*arrays)


def _split3(x):
    hi = x.astype(BF16)
    r = x - hi.astype(F32)
    mid = r.astype(BF16)
    lo = (r - mid.astype(F32)).astype(BF16)
    return hi, mid, lo


def _dot3(x, onehot, dims):
    return sum(lax.dot_general(t, onehot, dims, preferred_element_type=F32) for t in _split3(x))


_NT = (((1,), (1,)), ((), ()))
_NN = (((1,), (0,)), ((), ()))
_TN = (((0,), (0,)), ((), ()))


def _colmap(x, *, inverse, name):
    R = x.shape[0] if inverse else x.shape[1]
    tr = 256
    per = W_IN_SLOT // LANE
    n_out = N_DEV * per if inverse else Z_W // LANE
    segs = [(p, q, n) for q, p, n in W_IN_SEGS] if inverse else list(W_IN_SEGS)

    def body(x_ref, o_ref):
        ia = lax.broadcasted_iota(jnp.int32, (LANE, LANE), 0)
        ib = lax.broadcasted_iota(jnp.int32, (LANE, LANE), 1)

        def src_block(i):
            if inverse:
                return x_ref[:, i * LANE:(i + 1) * LANE]
            return x_ref[i // per, :, (i % per) * LANE:(i % per + 1) * LANE]

        for jb in range(n_out):
            acc = None
            for s0, d0, n in segs:
                lo, hi = max(d0, jb * LANE), min(d0 + n, (jb + 1) * LANE)
                if lo >= hi:
                    continue
                delta = d0 - s0
                for i in range((lo - delta) // LANE, (hi - delta - 1) // LANE + 1):
                    shift = jb * LANE - i * LANE - delta
                    sel = ((ia - ib == shift) & (ib >= lo - jb * LANE) & (ib < hi - jb * LANE)).astype(BF16)
                    blk = src_block(i)
                    part = _dot3(blk, sel, _NN) if inverse else lax.dot_general(blk, sel, _NN, preferred_element_type=F32)
                    acc = part if acc is None else acc + part
            if acc is None:
                acc = jnp.zeros((tr, LANE), F32)
            if inverse:
                o_ref[jb // per, :, (jb % per) * LANE:(jb % per + 1) * LANE] = acc
            else:
                o_ref[:, jb * LANE:(jb + 1) * LANE] = acc.astype(BF16)

    slot_spec = pl.BlockSpec((N_DEV, tr, W_IN_SLOT), lambda i: (0, i, 0))
    flat_spec = pl.BlockSpec((tr, Z_W), lambda i: (i, 0))
    if inverse:
        return _pcall(body, name=name, grid=(R // tr,), in_specs=[flat_spec], out_specs=slot_spec,
                      out_shape=jax.ShapeDtypeStruct((N_DEV, R, W_IN_SLOT), F32), compiler_params=_params(("parallel",)))(x)
    return _pcall(body, name=name, grid=(R // tr,), in_specs=[slot_spec], out_specs=flat_spec,
                  out_shape=jax.ShapeDtypeStruct((R, Z_W), BF16), compiler_params=_params(("parallel",)))(x)


def _log_sigmoid(x):
    return jnp.minimum(x, 0.0) - jnp.log(1.0 + jnp.exp(-jnp.abs(x)))


def _cum_heads(f, dcum=None, *, name):
    H, S = f.shape
    tn = min(512, S)
    rev = dcum is not None

    def body(*refs):
        j = pl.program_id(0)
        s_idx = lax.broadcasted_iota(jnp.int32, (S, tn), 0)
        t_idx = lax.broadcasted_iota(jnp.int32, (S, tn), 1) + j * tn
        if not rev:
            f_ref, o_ref = refs
            tri = (s_idx <= t_idx).astype(BF16)
            o_ref[...] = _dot3(_log_sigmoid(f_ref[...]), tri, _NN)
        else:
            fj_ref, d_ref, o_ref = refs
            tri = (s_idx >= t_idx).astype(BF16)
            o_ref[...] = _dot3(d_ref[...], tri, _NN) * jax.nn.sigmoid(-fj_ref[...])

    full = pl.BlockSpec((H, S), lambda j: (0, 0))
    blk = pl.BlockSpec((H, tn), lambda j: (0, j))
    ins, specs = ([f], [full]) if not rev else ([f, dcum], [blk, full])
    return _pcall(body, name=name, grid=(S // tn,), in_specs=specs, out_specs=blk,
                  out_shape=jax.ShapeDtypeStruct((H, S), F32), compiler_params=_params(("arbitrary",)))(*ins)


def _rel_onehot(qi, transposed):
    shape = (N_REL, CA_BAND) if transposed else (CA_BAND, N_REL)
    kk = lax.broadcasted_iota(jnp.int32, shape, 1 if transposed else 0)
    rr = lax.broadcasted_iota(jnp.int32, shape, 0 if transposed else 1)
    idx = jnp.clip(CA_PAD + qi - kk, REL_MIN, REL_MAX) - REL_MIN
    return (idx == rr).astype(BF16)


def _relbias_expand(rb):
    def body(rb_ref, o_ref):
        o_ref[0] = _dot3(rb_ref[...], _rel_onehot(pl.program_id(0), True), _NN)

    return _pcall(body, name="relbias_expand", grid=(CHUNK,),
                  in_specs=[pl.BlockSpec((CA_HEADS, N_REL), lambda q: (0, 0))],
                  out_specs=pl.BlockSpec((1, CA_HEADS, CA_BAND), lambda q: (q, 0, 0)),
                  out_shape=jax.ShapeDtypeStruct((CHUNK, CA_HEADS, CA_BAND), F32),
                  compiler_params=_params(("arbitrary",)))(rb)


def _relbias_reduce(db):
    def body(db_ref, o_ref):
        q = pl.program_id(0)
        part = _dot3(db_ref[0], _rel_onehot(q, False), _NN)

        @pl.when(q == 0)
        def _():
            o_ref[...] = part

        @pl.when(q > 0)
        def _():
            o_ref[...] += part

    return _pcall(body, name="relbias_reduce", grid=(CHUNK,),
                  in_specs=[pl.BlockSpec((1, CA_HEADS, CA_BAND), lambda q: (q, 0, 0))],
                  out_specs=pl.BlockSpec((CA_HEADS, N_REL), lambda q: (0, 0)),
                  out_shape=jax.ShapeDtypeStruct((CA_HEADS, N_REL), F32),
                  compiler_params=_params(("arbitrary",)))(db)


def _attn_cfg(mode, S):
    if mode == "fox":
        return min(256, S), FOX_HEAD_DIM ** -0.5
    if mode == "chunk":
        return CHUNK, CA_HEAD_DIM ** -0.5
    return min(512, S), XA_HEAD_DIM ** -0.5


def _scores(mode, i, tq, scale, qb, kb, extra):
    s = lax.dot_general(qb, kb, _NT, preferred_element_type=F32) * scale
    nk = kb.shape[0]
    if mode == "fox":
        cq, ck = extra
        s = s + cq - ck
        row = lax.broadcasted_iota(jnp.int32, (tq, nk), 0) + i * tq
        col = lax.broadcasted_iota(jnp.int32, (tq, nk), 1)
        s = jnp.where(row >= col, s, NEG_INF)
    elif mode == "chunk":
        (bias,) = extra
        col = lax.broadcasted_iota(jnp.int32, (tq, nk), 1) + i * CHUNK
        s = jnp.where(col >= CA_PAD, s + bias, NEG_INF)
    return s


def _attn_specs(mode, q, k, tq, hg):
    H, S, D = q.shape
    Sk = k.shape[1]
    q_spec = pl.BlockSpec((hg, tq, D), lambda g, i: (g, i, 0))
    kv_spec = pl.BlockSpec((hg, Sk, D), lambda g, i: (g, 0, 0))
    if mode == "fox":
        ex = [pl.BlockSpec((hg, tq, 1), lambda g, i: (g, i, 0)), pl.BlockSpec((hg, 1, Sk), lambda g, i: (g, 0, 0))]
    elif mode == "chunk":
        ex = [pl.BlockSpec((hg, CHUNK, CA_BAND), lambda g, i: (g, 0, 0))]
    else:
        ex = []
    col_spec = pl.BlockSpec((hg, tq, 1), lambda g, i: (g, i, 0))
    return q_spec, kv_spec, ex, col_spec


def _kv_window(mode, i, ref, h):
    if mode == "chunk":
        return ref[h, pl.ds(pl.multiple_of(i * CHUNK, CHUNK), CA_BAND), :]
    return ref[h]


def _attn_fwd(mode, q, k, v, extra=(), *, name):
    H, S, D = q.shape
    tq, scale = _attn_cfg(mode, S)
    hg = H
    q_spec, kv_spec, ex_specs, col_spec = _attn_specs(mode, q, k, tq, hg)
    n_ex = len(extra)

    def body(*refs):
        q_ref, k_ref, v_ref = refs[:3]
        o_ref, lse_ref = refs[3 + n_ex:]
        i = pl.program_id(1)
        for h in range(hg):
            ex = [r[h] for r in refs[3:3 + n_ex]]
            kb = _kv_window(mode, i, k_ref, h).astype(BF16)
            vb = _kv_window(mode, i, v_ref, h).astype(BF16)
            s = _scores(mode, i, tq, scale, q_ref[h].astype(BF16), kb, ex)
            m = jnp.max(s, axis=1, keepdims=True)
            e = jnp.exp(s - m)
            l = jnp.sum(e, axis=1, keepdims=True)
            p = e / l
            o_ref[h] = lax.dot_general(p.astype(BF16), vb, _NN, preferred_element_type=F32)
            lse_ref[h] = m + jnp.log(l)

    return _pcall(
        body, name=name, grid=(H // hg, S // tq), in_specs=[q_spec, kv_spec, kv_spec] + ex_specs,
        out_specs=[q_spec, col_spec],
        out_shape=[jax.ShapeDtypeStruct((H, S, D), F32), jax.ShapeDtypeStruct((H, S, 1), F32)],
        compiler_params=_params(("parallel", "arbitrary")),
    )(q, k, v, *extra)


def _attn_bwd(mode, q, k, v, lse, do, extra=(), *, name):
    H, S, D = q.shape
    Sk = k.shape[1]
    tq, scale = _attn_cfg(mode, S)
    hg = min(H, 4)
    q_spec, kv_spec, ex_specs, col_spec = _attn_specs(mode, q, k, tq, hg)
    n_ex = len(extra)
    out_specs = [q_spec, kv_spec, kv_spec]
    out_shape = [jax.ShapeDtypeStruct((H, S, D), F32), jax.ShapeDtypeStruct((H, Sk, D), F32),
                 jax.ShapeDtypeStruct((H, Sk, D), F32)]
    if mode == "fox":
        out_specs += [col_spec, ex_specs[1]]
        out_shape += [jax.ShapeDtypeStruct((H, S, 1), F32), jax.ShapeDtypeStruct((H, 1, Sk), F32)]
    elif mode == "chunk":
        out_specs += [ex_specs[0]]
        out_shape += [jax.ShapeDtypeStruct((H, CHUNK, CA_BAND), F32)]

    def body(*refs):
        q_ref, k_ref, v_ref, lse_ref, do_ref = refs[:5]
        dq_ref, dk_ref, dv_ref = refs[5 + n_ex:8 + n_ex]
        rest = refs[8 + n_ex:]
        i = pl.program_id(1)

        @pl.when(i == 0)
        def _():
            dk_ref[...] = jnp.zeros_like(dk_ref)
            dv_ref[...] = jnp.zeros_like(dv_ref)
            if mode == "fox":
                rest[1][...] = jnp.zeros_like(rest[1])
            elif mode == "chunk":
                rest[0][...] = jnp.zeros_like(rest[0])

        for h in range(hg):
            ex = [r[h] for r in refs[5:5 + n_ex]]
            qb = q_ref[h].astype(BF16)
            kb = _kv_window(mode, i, k_ref, h).astype(BF16)
            vb = _kv_window(mode, i, v_ref, h).astype(BF16)
            dob = do_ref[h].astype(BF16)
            s = _scores(mode, i, tq, scale, qb, kb, ex)
            p = jnp.exp(s - lse_ref[h])
            dp = lax.dot_general(dob, vb, _NT, preferred_element_type=F32)
            ds = p * (dp - jnp.sum(dp * p, axis=1, keepdims=True))
            dsb = (ds * scale).astype(BF16)
            dq_ref[h] = lax.dot_general(dsb, kb, _NN, preferred_element_type=F32)
            dk_part = lax.dot_general(dsb, qb, _TN, preferred_element_type=F32)
            dv_part = lax.dot_general(p.astype(BF16), dob, _TN, preferred_element_type=F32)
            if mode == "chunk":
                win = pl.ds(pl.multiple_of(i * CHUNK, CHUNK), CA_BAND)
                dk_ref[h, win, :] += dk_part
                dv_ref[h, win, :] += dv_part
                rest[0][h] += ds
            else:
                dk_ref[h] += dk_part
                dv_ref[h] += dv_part
            if mode == "fox":
                rest[0][h] = jnp.sum(ds, axis=1, keepdims=True)
                rest[1][h] += -jnp.sum(ds, axis=0, keepdims=True)

    return _pcall(
        body, name=name, grid=(H // hg, S // tq),
        in_specs=[q_spec, kv_spec, kv_spec, col_spec, q_spec] + ex_specs,
        out_specs=out_specs, out_shape=out_shape,
        compiler_params=_params(("parallel", "arbitrary")),
    )(q, k, v, lse, do, *extra)


def _scan(x, a, h=None, *, name):
    S = x.shape[0]
    CB = SCAN_CB
    rev = h is not None
    n_grp = S // 8

    def body(*refs):
        if rev:
            x_ref, a_ref, h_ref, o_ref, da_ref = refs
        else:
            x_ref, a_ref, o_ref = refs
        ar = a_ref[:, :CB]
        ai = -a_ref[:, CB:] if rev else a_ref[:, CB:]
        zero = jnp.zeros((1, CB), F32)

        def group(g, carry):
            base = pl.multiple_of((n_grp - 1 - g) * 8 if rev else g * 8, 8)
            for j in (range(7, -1, -1) if rev else range(8)):
                t = base + j
                if rev:
                    hr, hi, dar, dai = carry
                else:
                    hr, hi = carry
                xr = x_ref[pl.ds(t, 1), :CB]
                xi = x_ref[pl.ds(t, 1), CB:]
                hr, hi = ar * hr - ai * hi + xr, ar * hi + ai * hr + xi
                o_ref[pl.ds(t, 1), :CB] = hr
                o_ref[pl.ds(t, 1), CB:] = hi
                if rev:
                    tp = jnp.maximum(t - 1, 0)
                    live = (t > 0).astype(F32)
                    pr = h_ref[pl.ds(tp, 1), :CB] * live
                    pi = h_ref[pl.ds(tp, 1), CB:] * live
                    carry = (hr, hi, dar + hr * pr + hi * pi, dai + hi * pr - hr * pi)
                else:
                    carry = (hr, hi)
            return carry

        if rev:
            _, _, dar, dai = lax.fori_loop(0, n_grp, group, (zero, zero, zero, zero))
            da_ref[:, :CB] = dar
            da_ref[:, CB:] = dai
        else:
            lax.fori_loop(0, n_grp, group, (zero, zero))

    big = pl.BlockSpec((S, 2 * CB), lambda c: (0, c))
    vec = pl.BlockSpec((1, 2 * CB), lambda c: (0, c))
    n_blk = x.shape[1] // (2 * CB)
    if rev:
        return _pcall(body, name=name, grid=(n_blk,), in_specs=[big, vec, big], out_specs=[big, vec],
                      out_shape=[jax.ShapeDtypeStruct(x.shape, F32), jax.ShapeDtypeStruct(a.shape, F32)],
                      compiler_params=_params(("parallel",)))(x, a, h)
    return _pcall(body, name=name, grid=(n_blk,), in_specs=[big, vec], out_specs=big,
                  out_shape=jax.ShapeDtypeStruct(x.shape, F32), compiler_params=_params(("parallel",)))(x, a)


def _ssm_prep1(lr_, li, ldt):
    lr = jnp.minimum(lr_, -1e-4)
    dt = jnp.exp(ldt)
    mag = jnp.exp(lr * dt)
    ar = mag * jnp.cos(li * dt)
    ai = mag * jnp.sin(li * dt)
    den = lr * lr + li * li
    gr = ((ar - 1.0) * lr + ai * li) / den
    gi = (ai * lr - (ar - 1.0) * li) / den
    return ar, ai, gr, gi


def _ssm_prep2(gr, gi, br, bi):
    return gr * br - gi * bi, gr * bi + gi * br


def _vjp_of(fn, n_in):
    def bwd(*args):
        cts = args[n_in:]
        return jax.vjp(fn, *args[:n_in])[1](cts[0] if len(cts) == 1 else tuple(cts))
    return bwd


def _to_blocked(r, i):
    lead = r.shape[:-1]
    t = jnp.stack([r.reshape(lead + (N_SSM_CH // SCAN_CB, SCAN_CB)), i.reshape(lead + (N_SSM_CH // SCAN_CB, SCAN_CB))],
                  axis=-2)
    return t.reshape(lead + (2 * N_SSM_CH,))


def _from_blocked(m):
    lead = m.shape[:-1]
    t = m.reshape(lead + (N_SSM_CH // SCAN_CB, 2, SCAN_CB))
    return t[..., 0, :].reshape(lead + (N_SSM_CH,)), t[..., 1, :].reshape(lead + (N_SSM_CH,))


def _blockdiag(r, i):
    eye = jnp.eye(SSM_GROUPS, dtype=F32)
    bd = lambda t: jnp.einsum("gcp,gh->gchp", t, eye).reshape(SSM_WIDTH, N_SSM_CH)
    return _to_blocked(bd(r), bd(i))


def _blockdiag_inv(m):
    eye = jnp.eye(SSM_GROUPS, dtype=F32)
    r, i = _from_blocked(m)
    diag = lambda t: jnp.einsum("gchp,gh->gcp", t.reshape(SSM_GROUPS, SSM_GROUP, SSM_GROUPS, SSM_STATE), eye)
    return diag(r), diag(i)


def _shift_rows(x, n):
    S = x.shape[0]
    row = lax.broadcasted_iota(jnp.int32, x.shape, 0)
    if n > 0:
        return jnp.where(row >= n, pltpu.roll(x, n, 0), 0.0)
    return jnp.where(row < S + n, pltpu.roll(x, S + n, 0), 0.0)


def _bf(x):
    return x.astype(BF16).astype(F32)


def _conv_pre(a, w, b):
    ab, wb = _bf(a), _bf(w)
    return wb[2:3] * ab + wb[1:2] * _shift_rows(ab, 1) + wb[0:1] * _shift_rows(ab, 2) + b


def _ffn_mid(up, conv_w, conv_b, dh=None, *, name):
    S = up.shape[0]
    tn = LANE
    nb = D_FF_P // tn
    rev = dh is not None

    def body(*refs):
        if not rev:
            a_ref, g_ref, w_ref, b_ref, o_ref = refs
            o_ref[...] = jax.nn.gelu(_conv_pre(a_ref[...], w_ref[...], b_ref[...])) * g_ref[...]
            return
        a_ref, g_ref, w_ref, b_ref, dh_ref, dup_a_ref, dup_g_ref, dw_ref, db_ref = refs
        a, w, dh_ = a_ref[...], w_ref[...], dh_ref[...]
        pre = _conv_pre(a, w, b_ref[...])
        gl, gelu_vjp = jax.vjp(jax.nn.gelu, pre)
        dup_g_ref[...] = dh_ * gl
        (dpre,) = gelu_vjp(dh_ * g_ref[...])
        db_ref[...] = jnp.sum(dpre, axis=0, keepdims=True)
        dpb, ab, wb = _bf(dpre), _bf(a), _bf(w)
        dup_a_ref[...] = wb[2:3] * dpb + wb[1:2] * _shift_rows(dpb, -1) + wb[0:1] * _shift_rows(dpb, -2)
        dw_ref[2:3, :] = jnp.sum(dpb * ab, axis=0, keepdims=True)
        dw_ref[1:2, :] = jnp.sum(dpb * _shift_rows(ab, 1), axis=0, keepdims=True)
        dw_ref[0:1, :] = jnp.sum(dpb * _shift_rows(ab, 2), axis=0, keepdims=True)

    a_spec = pl.BlockSpec((S, tn), lambda j: (0, j))
    g_spec = pl.BlockSpec((S, tn), lambda j: (0, j + nb))
    w_spec = pl.BlockSpec((3, tn), lambda j: (0, j))
    b_spec = pl.BlockSpec((1, tn), lambda j: (0, j))
    if not rev:
        return _pcall(body, name=name, grid=(nb,), in_specs=[a_spec, g_spec, w_spec, b_spec], out_specs=a_spec,
                      out_shape=jax.ShapeDtypeStruct((S, D_FF_P), F32), compiler_params=_params(("parallel",)))(
                          up, up, conv_w, conv_b)
    return _pcall(body, name=name, grid=(nb,), in_specs=[a_spec, g_spec, w_spec, b_spec, a_spec],
                  out_specs=[a_spec, a_spec, w_spec, b_spec],
                  out_shape=[jax.ShapeDtypeStruct((S, D_FF_P), F32), jax.ShapeDtypeStruct((S, D_FF_P), F32),
                             jax.ShapeDtypeStruct((3, D_FF_P), F32), jax.ShapeDtypeStruct((1, D_FF_P), F32)],
                  compiler_params=_params(("parallel",)))(up, up, conv_w, conv_b, dh)


def _ff_pad(t):
    lead = t.shape[:-1]
    t = t.reshape(lead + (N_DEV, FF_HALF))
    return jnp.pad(t, [(0, 0)] * len(lead) + [(0, 0), (0, FF_HALF_P - FF_HALF)]).reshape(lead + (D_FF_P,))


def _ff_unpad(t):
    lead = t.shape[:-1]
    return t.reshape(lead + (N_DEV, FF_HALF_P))[..., :FF_HALF].reshape(lead + (D_FF,))


def _ln_fwd(x, h, g, b):
    r = DN_ALPHA * x + h
    mu = jnp.mean(r, axis=-1, keepdims=True)
    var = jnp.mean(jnp.square(r - mu), axis=-1, keepdims=True)
    return r, (r - mu) * lax.rsqrt(var + LN_EPS) * g + b


def _ln_bwd(r, dy, g):
    mu = jnp.mean(r, axis=-1, keepdims=True)
    var = jnp.mean(jnp.square(r - mu), axis=-1, keepdims=True)
    xhat = (r - mu) * lax.rsqrt(var + LN_EPS)
    dxh = dy * g
    dr = lax.rsqrt(var + LN_EPS) * (dxh - jnp.mean(dxh, axis=-1, keepdims=True)
                                    - xhat * jnp.mean(dxh * xhat, axis=-1, keepdims=True))
    return dr, jnp.sum(dy * xhat, axis=0, keepdims=True), jnp.sum(dy, axis=0, keepdims=True)


def _merge(gf, gs, gc, ya, yb2, yc):
    yb = yb2[:, :D_MODEL] * jax.nn.sigmoid(yb2[:, D_MODEL:])
    return jax.nn.sigmoid(gf) * ya + jax.nn.sigmoid(gs) * yb + jax.nn.sigmoid(gc) * yc


def _s5_tail(hc, su, d):
    return jax.nn.gelu(hc + d * su)


def _s5_tail_bwd(hc, su, dgel, d):
    _, vjp = jax.vjp(jax.nn.gelu, hc + d * su)
    (dy,) = vjp(dgel)
    return dy, d * dy, jnp.sum(dy * su, axis=0, keepdims=True)


def _loss_rows(y, tgt):
    err = y - tgt
    return err * (1.0 / D_MODEL), jnp.sum(0.5 * jnp.square(err), axis=0, keepdims=True) * (1.0 / D_MODEL)


def _peer(k):
    x, y, c = lax.axis_index("x"), lax.axis_index("y"), lax.axis_index("c")
    return (x ^ ((k >> 2) & 1), y ^ ((k >> 1) & 1), c ^ (k & 1))


def _my_slot():
    return 4 * lax.axis_index("x") + 2 * lax.axis_index("y") + lax.axis_index("c")


def _peer_slot(k):
    px, py, pc = _peer(k)
    return 4 * px + 2 * py + pc


N_CHIP = N_DEV // 2
OTHER_CHIPS = (2, 4, 6)


def _chip_of(dev):
    return 2 * dev[0] + dev[1]


def _remote(src, dst, send, recv, dev):
    return pltpu.make_async_remote_copy(src_ref=src, dst_ref=dst, send_sem=send, recv_sem=recv, device_id=dev,
                                        device_id_type=pl.DeviceIdType.MESH)


def _all_gather(shards, *, name, host=None, then=None):
    return _exchange(shards, *_all_gather_parts(shards), name=name, host=host, then=then)


def _exchange(ins, out_shapes, sem_shapes, start, finish, *, name, host=None, then=None):
    if host is not None:
        rider = _Rider(host, ins, out_shapes, sem_shapes, start, finish, then)
        _RIDERS.append(rider)
        return rider
    n = len(ins)

    def body(*refs):
        start(refs[:n], refs[n:2 * n], refs[2 * n:])
        finish(refs[:n], refs[n:2 * n], refs[2 * n:])

    hbm = pl.BlockSpec(memory_space=pl.ANY)
    return _pcall(body, name=name, in_specs=[hbm] * n, out_specs=[hbm] * n, out_shape=list(out_shapes),
                  scratch_shapes=list(sem_shapes))(*ins)


def _all_gather_parts(shards):
    n = len(shards)

    def first_copies(ins, outs, sems):
        send, recv, _ = sems
        return [_remote(ins[t], outs[t].at[_my_slot()], send.at[t, k - 1], recv.at[t, k - 1], _peer(k))
                for k in (1,) + OTHER_CHIPS for t in range(n)]

    def local_copies(ins, outs, sems):
        return [pltpu.make_async_copy(ins[t], outs[t].at[_my_slot()], sems[2].at[t]) for t in range(n)]

    def start(ins, outs, sems):
        for cp in local_copies(ins, outs, sems) + first_copies(ins, outs, sems):
            cp.start()

    def finish(ins, outs, sems):
        send, recv, _ = sems
        sibling = _peer(1)
        passed = []
        for k in OTHER_CHIPS:
            for t in range(n):
                slot = outs[t].at[_peer_slot(k)]
                _remote(ins[t], slot, send.at[t, k - 1], recv.at[t, k - 1], _peer(k)).wait_recv()
                cp = _remote(slot, slot, send.at[t, k], recv.at[t, k], sibling)
                cp.start()
                passed.append(cp)
        for t in range(n):
            _remote(ins[t], outs[t].at[_peer_slot(1)], send.at[t, 0], recv.at[t, 0], sibling).wait_recv()
            for k in OTHER_CHIPS:
                _remote(ins[t], outs[t].at[_peer_slot(k + 1)], send.at[t, k], recv.at[t, k], sibling).wait_recv()
        for cp in first_copies(ins, outs, sems) + passed:
            cp.wait_send()
        for lc in local_copies(ins, outs, sems):
            lc.wait()

    out_shapes = [jax.ShapeDtypeStruct((N_DEV,) + s.shape, s.dtype) for s in shards]
    sem_shapes = [pltpu.SemaphoreType.DMA((n, N_DEV - 1)), pltpu.SemaphoreType.DMA((n, N_DEV - 1)),
                  pltpu.SemaphoreType.DMA((n,))]
    return out_shapes, sem_shapes, start, finish


def _sibling_swap(grads, *, name, host=None, then=None):
    n = len(grads)

    def copies(ins, outs, sems):
        c = lax.axis_index("c")
        return [_remote(ins[t].at[:, 1 - c], outs[t], sems[0].at[t], sems[1].at[t], _peer(1)) for t in range(n)]

    def start(ins, outs, sems):
        for cp in copies(ins, outs, sems):
            cp.start()

    def finish(ins, outs, sems):
        for cp in copies(ins, outs, sems):
            cp.wait()

    out_shapes = [jax.ShapeDtypeStruct((N_CHIP,) + g.shape[2:], g.dtype) for g in grads]
    sem_shapes = [pltpu.SemaphoreType.DMA((n,)), pltpu.SemaphoreType.DMA((n,))]
    return _exchange(grads, out_shapes, sem_shapes, start, finish, name=name, host=host, then=then)


def _pair_add(g, p, out_dtype, *, name):
    _, _, R, C = g.shape
    tr = _pick(R, (128, 64, 32, 16, 8))

    def body(c_ref, g_ref, p_ref, o_ref):
        o_ref[...] = (g_ref[...] + p_ref[...]).astype(out_dtype)

    grid_spec = pltpu.PrefetchScalarGridSpec(
        num_scalar_prefetch=1, grid=(N_CHIP, R // tr),
        in_specs=[pl.BlockSpec((None, None, tr, C), lambda j, i, c_ref: (j, c_ref[0], i, 0)),
                  pl.BlockSpec((None, tr, C), lambda j, i, c_ref: (j, i, 0))],
        out_specs=pl.BlockSpec((None, tr, C), lambda j, i, c_ref: (j, i, 0)))
    core = lax.axis_index("c").astype(jnp.int32).reshape(1)
    return _pcall(body, name=name, grid_spec=grid_spec, out_shape=jax.ShapeDtypeStruct(p.shape, out_dtype),
                  compiler_params=_params(("parallel", "parallel")))(core, g, p)


def _chip_exchange(sums, *, name, host=None):
    n = len(sums)

    def copies(ins, outs, sems, dst_is_mine):
        send, recv, _ = sems
        mine = 2 * lax.axis_index("x") + lax.axis_index("y")
        out = []
        for k in OTHER_CHIPS:
            theirs = _chip_of(_peer(k))
            for t in range(n):
                out.append(_remote(ins[t].at[theirs], outs[t].at[mine if dst_is_mine else theirs],
                                   send.at[t, k // 2 - 1], recv.at[t, k // 2 - 1], _peer(k)))
        return out

    def local_copies(ins, outs, sems):
        mine = 2 * lax.axis_index("x") + lax.axis_index("y")
        return [pltpu.make_async_copy(ins[t].at[mine], outs[t].at[mine], sems[2].at[t]) for t in range(n)]

    def start(ins, outs, sems):
        for cp in local_copies(ins, outs, sems) + copies(ins, outs, sems, True):
            cp.start()

    def finish(ins, outs, sems):
        for cp in copies(ins, outs, sems, False) + local_copies(ins, outs, sems):
            cp.wait()

    out_shapes = [jax.ShapeDtypeStruct(s.shape, s.dtype) for s in sums]
    sem_shapes = [pltpu.SemaphoreType.DMA((n, N_CHIP - 1)), pltpu.SemaphoreType.DMA((n, N_CHIP - 1)),
                  pltpu.SemaphoreType.DMA((n,))]
    return _exchange(sums, out_shapes, sem_shapes, start, finish, name=name, host=host)


class _GradReduce:
    def __init__(self, grads, wire_dtypes, hosts=None):
        pairs = [g.reshape((N_CHIP, 2) + g.shape[1:]) for g in grads]
        self.n, self.riders, self.recv = len(grads), [], None

        def after_swap(partner):
            sums = [_pair_add(g, p, dt, name="grad_pair_add") for g, p, dt in zip(pairs, partner, wire_dtypes)]
            if hosts is None:
                self.recv = _chip_exchange(sums, name="grad_chip_exchange")
            else:
                self.riders = [(idx, _chip_exchange([sums[i] for i in idx], name="grad_chip_exchange", host=h))
                               for h, idx in hosts[1]]

        if hosts is None:
            after_swap(_sibling_swap(pairs, name="grad_sibling_swap"))
        else:
            _sibling_swap(pairs, name="grad_sibling_swap", host=hosts[0], then=after_swap)

    def result(self):
        if self.recv is None:
            self.recv = [None] * self.n
            for idx, rider in self.riders:
                assert rider.results is not None, rider.host
                for i, r in zip(idx, rider.results):
                    self.recv[i] = r
        return self.recv


def _adamw(recv, w, m, v, layer=None, *, name):
    n_slots, R, C = recv.shape
    tr = _pick(R, (128, 64, 32, 16, 8))

    def body(r_ref, w_ref, m_ref, v_ref, g_ref, d_ref, nm_ref, nv_ref):
        g = r_ref[0].astype(F32)
        for s in range(1, n_slots):
            g = g + r_ref[s].astype(F32)
        m_new = ADAM_B1 * m_ref[...] + (1.0 - ADAM_B1) * g
        v_new = ADAM_B2 * v_ref[...] + (1.0 - ADAM_B2) * jnp.square(g)
        m_hat = m_new / (1.0 - ADAM_B1 ** ADAM_STEP)
        v_hat = v_new / (1.0 - ADAM_B2 ** ADAM_STEP)
        g_ref[...] = g
        d_ref[...] = -ADAM_LR * (m_hat / (jnp.sqrt(v_hat) + ADAM_EPS) + ADAM_WD * w_ref[...])
        nm_ref[...] = m_new
        nv_ref[...] = v_new

    row = pl.BlockSpec((tr, C), lambda i: (i, 0))
    state = row if layer is None else pl.BlockSpec((None, tr, C), lambda i: (layer, i, 0))
    return _pcall(
        body, name=name, grid=(R // tr,),
        in_specs=[pl.BlockSpec((n_slots, tr, C), lambda i: (0, i, 0)), state, state, state],
        out_specs=[row] * 4, out_shape=[jax.ShapeDtypeStruct((R, C), F32)] * 4,
        compiler_params=_params(("parallel",)),
    )(recv, w, m, v)


def _flat_pad(parts, rows):
    flat = jnp.concatenate([p.reshape(-1) for p in parts])
    return jnp.pad(flat, (0, rows * LANES - flat.shape[0])).reshape(rows, LANES)


def _small_shard_shape(n):
    return SMALL[n][:-1] + (SMALL[n][-1] // N_DEV,)


def _unpack_small(gathered):
    out, off = {}, 0
    flat = gathered.reshape(N_DEV, -1)
    for n in SMALL:
        r, c = _small_shard_shape(n)
        out[n] = flat[:, off:off + r * c].reshape(N_DEV, r, c).transpose(1, 0, 2).reshape(r, N_DEV * c)
        off += r * c
    return out


def _pack_state(state, prefix, layer):
    return _flat_pad([state[prefix + n][layer] for n in (*SMALL, *REPL)], PACK_ROWS)


def _pack_small_grads(grads):
    cols = []
    for n in SMALL:
        r, c = _small_shard_shape(n)
        cols.append(grads[n].reshape(r, N_DEV, c).transpose(1, 0, 2).reshape(N_DEV, r * c))
    cols += [jnp.broadcast_to(grads[n].reshape(1, -1), (N_DEV, grads[n].size)) for n in REPL]
    flat = jnp.concatenate(cols, axis=1)
    return jnp.pad(flat, ((0, 0), (0, PACK_ROWS * LANES - flat.shape[1]))).reshape(N_DEV, PACK_ROWS, LANES)


def _unpack_state(flat):
    out, off = {}, 0
    flat = flat.reshape(-1)
    for n, shape in [(n, _small_shard_shape(n)) for n in SMALL] + list(REPL.items()):
        sz = math.prod(shape)
        out[n] = flat[off:off + sz].reshape(shape)
        off += sz
    return out


def _pad_b_in(b):
    parts, pos = [], 0
    for src, width, dst in Z_PIECES:
        parts += [jnp.zeros((1, dst - pos), b.dtype), b[:, src:src + width]]
        pos = dst + width
    return jnp.concatenate(parts + [jnp.zeros((1, Z_W - pos), b.dtype)], axis=1)


def _unpad_b_in(bp):
    return jnp.concatenate([bp[:, dst:dst + width] for _, width, dst in Z_PIECES], axis=1)


def _heads(t, n_heads):
    S = t.shape[0]
    return t.astype(BF16).reshape(S, n_heads, -1).transpose(1, 0, 2)


def _unheads(t):
    H, S, D = t.shape
    return t.transpose(1, 0, 2).reshape(S, H * D)


def _up_shard_pad(t):
    lead = t.shape[:-1]
    t = jnp.pad(t.reshape(lead + (2, FF_HALF)), [(0, 0)] * len(lead) + [(0, 0), (0, FF_HALF_P - FF_HALF)])
    return t.reshape(lead + (2 * FF_HALF_P,))


def _up_shard_unpad(t):
    lead = t.shape[:-1]
    return t.reshape(lead + (2, FF_HALF_P))[..., :FF_HALF].reshape(lead + (2 * FF_HALF,))


def _pad_shard(n, t):
    if n == "w_in":
        return jnp.pad(t, ((0, 0), (0, W_IN_SLOT - W_IN_SHARD)))
    if n == "ffn_w_up":
        return _up_shard_pad(t)
    if n == "ffn_w_down":
        return jnp.pad(t, ((0, FF_HALF_P - FF_HALF), (0, 0)))
    return t


def _unpad_shard(n, t):
    if n == "w_in":
        return t[:, :W_IN_SHARD]
    if n == "ffn_w_up":
        return _up_shard_unpad(t)
    if n == "ffn_w_down":
        return t[:FF_HALF]
    return t


GATHER_HOSTS = (("fox_fwd", (0,)), ("chunk_fwd", (8, 9)), ("mm_up", (1, 2, 3, 4, 5, 6, 7, 10)))
REDUCE_HOSTS = ("mm_up_dw", (("fox_bwd", (0,)), ("chunk_bwd", (8, 9)), ("s5_scan_bwd", (1, 2, 3, 4, 5, 6, 7, 10))))


def _layer_shards(state, layer):
    shards = [_pad_shard(n, state[n][layer].astype(BF16)) for n in BIG]
    return shards + [_flat_pad([state[n][layer] for n in SMALL], SMALL_ROWS)]


def _gathered_weights(gathered):
    *big, small = gathered
    W = dict(zip(BIG, big))
    for n in ("w_o", "xa_wq", "xa_wo", "ffn_w_down"):
        W[n] = W[n].reshape(-1, D_MODEL)
    W["w_in_p"] = _colmap(W.pop("w_in"), inverse=False, name="w_in_colmap")
    W.update(_unpack_small(small))
    return W


def _ssm_params(p):
    prep1_in = (p["ssm_lambda_re"], p["ssm_lambda_im"], p["ssm_log_dt"][:, None])
    ar, ai, gr, gi = _whole(_ssm_prep1, *prep1_in, name="ssm_prep1")
    to_cn = lambda b: b.transpose(2, 0, 1).reshape(SSM_GROUP, N_SSM_CH)
    prep2_in = (gr.reshape(1, N_SSM_CH), gi.reshape(1, N_SSM_CH), to_cn(p["ssm_b_re"]), to_cn(p["ssm_b_im"]))
    bbr, bbi = _whole(_ssm_prep2, *prep2_in, name="ssm_prep2")
    to_gcp = lambda t: t.reshape(SSM_GROUP, SSM_GROUPS, SSM_STATE).transpose(1, 0, 2)
    bb = _blockdiag(to_gcp(bbr), to_gcp(bbi))
    cct = _blockdiag(p["ssm_c_re"], -p["ssm_c_im"])
    a_vec = _to_blocked(ar.reshape(1, N_SSM_CH), ai.reshape(1, N_SSM_CH))
    return dict(bb=bb, cct=cct, a_vec=a_vec, prep1_in=prep1_in, prep2_in=prep2_in)


def _layer_fwd(x, mem, p, W):
    sp = _ssm_params(p)
    z = _mm(x, W["w_in_p"], bias=_pad_b_in(p["b_in"][None, :]), name="mm_in")
    zh = lambda off: _heads(z[:, off:off + 512], 8)
    f_t = z[:, Z_FF:Z_FF + FOX_HEADS].T
    cum = _cum_heads(f_t, name="fox_cum")
    ya_h, lse_a = _attn_fwd("fox", zh(Z_FQ), zh(Z_FK), zh(Z_FV), (cum[:, :, None], cum[:, None, :]), name="fox_fwd")
    ya_pre = _unheads(ya_h)
    ya = _mm(ya_pre, W["w_fox_o"], b_slots=True, name="mm_fox_o")
    x_ri = _mm(z, sp["bb"], a_off=Z_SU, a_cols=SSM_WIDTH, name="mm_s5_in")
    h_ri = _scan(x_ri, sp["a_vec"], name="s5_scan")
    hc = _mm(h_ri, sp["cct"], tb=True, name="mm_s5_out")
    d_row = p["ssm_d"][None, :]
    (gel,) = _rowwise(lambda a, b, c: (_s5_tail(a, b, c),), [hc, Win(z, 512, Z_SU // 512)], [d_row], name="s5_tail")
    yb2 = _mm(gel, W["w_ssm_glu"], b_slots=True, name="mm_glu")
    bias = _relbias_expand(W["ca_rel_bias"]).transpose(1, 0, 2)
    padk = lambda t: jnp.pad(t, ((0, 0), (CA_PAD, 0), (0, 0)))
    yc_h, lse_c = _attn_fwd("chunk", zh(Z_CQ), padk(zh(Z_CK)), padk(zh(Z_CV)), (bias,), name="chunk_fwd")
    yc_pre = _unheads(yc_h)
    yc = _mm(yc_pre, W["w_ca_o"], b_slots=True, name="mm_ca_o")
    gates = [Win(z, 1024, Z_GF // 1024), Win(z, 1024, Z_GS // 1024), Win(z, 1024, Z_GC // 1024)]
    (merged,) = _rowwise(lambda *a: (_merge(*a),), gates + [ya, yb2, yc], name="merge")
    h1 = _mm(merged, W["w_o"], name="mm_o")
    ln_g, ln_b = W["ln_g"], W["ln_b"]
    r1, x1 = _rowwise(_ln_fwd, [x, h1], [ln_g[0:1], ln_b[0:1]], name="ln_fwd")
    q = _mm(x1, W["xa_wq"], name="mm_xq")
    kv = _mm(mem, W["xa_wkv"], b_slots=True, name="mm_xkv")
    qh, kh, vh = _heads(q, XA_HEADS), _heads(kv[:, :D_MODEL], XA_HEADS), _heads(kv[:, D_MODEL:], XA_HEADS)
    o_h, lse_x = _attn_fwd("xa", qh, kh, vh, name="xa_fwd")
    o = _unheads(o_h)
    h2 = _mm(o, W["xa_wo"], name="mm_xo")
    r2, x2 = _rowwise(_ln_fwd, [x1, h2], [ln_g[1:2], ln_b[1:2]], name="ln_fwd")
    up = _mm(x2, W["ffn_w_up"], b_slots=True, name="mm_up")
    hmid = _ffn_mid(up, _ff_pad(W["ffn_conv_w"]), _ff_pad(p["ffn_conv_b"][None, :]), name="ffn_mid")
    h3 = _mm(hmid, W["ffn_w_down"], name="mm_down")
    r3, x3 = _rowwise(_ln_fwd, [x2, h3], [ln_g[2:3], ln_b[2:3]], name="ln_fwd")
    res = dict(x=x, z=z, cum=cum, lse_a=lse_a, ya_pre=ya_pre, ya=ya, h_ri=h_ri, hc=hc, gel=gel, yb2=yb2, lse_c=lse_c,
               yc_pre=yc_pre, yc=yc, merged=merged, r1=r1, x1=x1, q=q, kv=kv, o=o, lse_x=lse_x, r2=r2, x2=x2, up=up,
               hmid=hmid, r3=r3, bias=bias, W=W)
    return x3, res


def _layer_bwd(dx3, mem, p, res):
    W = res["W"]
    x, z = res["x"], res["z"]
    sp = _ssm_params(p)
    ln_g = W["ln_g"]
    big, small = {}, {}
    slots = lambda t: t.reshape(N_DEV, -1, D_MODEL)
    dr3, dg2, db2 = _rowwise(_ln_bwd, [res["r3"], dx3], [ln_g[2:3]], n_red=2, name="ln_bwd")
    dhmid = _mm(dr3, W["ffn_w_down"], tb=True, name="mm_down_dx")
    big["ffn_w_down"] = slots(_mm(res["hmid"], dr3, ta=True, name="mm_down_dw"))
    conv_w_p, conv_b_p = _ff_pad(W["ffn_conv_w"]), _ff_pad(p["ffn_conv_b"][None, :])
    dup_a, dup_g, dcw, dcb = _ffn_mid(res["up"], conv_w_p, conv_b_p, dhmid, name="ffn_mid_bwd")
    dup = jnp.concatenate([dup_a, dup_g], axis=1)
    small["ffn_conv_w"], small["ffn_conv_b"] = _ff_unpad(dcw), _ff_unpad(dcb)[0]
    dx2 = _mm(dup, W["ffn_w_up"], tb=True, b_slots=True, add=(dr3, DN_ALPHA), name="mm_up_dx")
    big["ffn_w_up"] = _mm(res["x2"], dup, ta=True, out_slots=2 * FF_HALF_P, name="mm_up_dw")
    dr2, dg1, db1 = _rowwise(_ln_bwd, [res["r2"], dx2], [ln_g[1:2]], n_red=2, name="ln_bwd")
    do = _mm(dr2, W["xa_wo"], tb=True, name="mm_xo_dx")
    big["xa_wo"] = slots(_mm(res["o"], dr2, ta=True, name="mm_xo_dw"))
    kv = res["kv"]
    qh, kh, vh = _heads(res["q"], XA_HEADS), _heads(kv[:, :D_MODEL], XA_HEADS), _heads(kv[:, D_MODEL:], XA_HEADS)
    dqh, dkh, dvh = _attn_bwd("xa", qh, kh, vh, res["lse_x"], _heads(do, XA_HEADS), name="xa_bwd")
    dq = _unheads(dqh)
    dkv = jnp.concatenate([_unheads(dkh), _unheads(dvh)], axis=1)
    dx1 = _mm(dq, W["xa_wq"], tb=True, add=(dr2, DN_ALPHA), name="mm_xq_dx")
    big["xa_wq"] = slots(_mm(res["x1"], dq, ta=True, name="mm_xq_dw"))
    big["xa_wkv"] = _mm(mem, dkv, ta=True, out_slots=256, name="mm_xkv_dw")
    dr1, dg0, db0 = _rowwise(_ln_bwd, [res["r1"], dx1], [ln_g[0:1]], n_red=2, name="ln_bwd")
    small["ln_g"] = jnp.concatenate([dg0, dg1, dg2], axis=0)
    small["ln_b"] = jnp.concatenate([db0, db1, db2], axis=0)
    dmerged = _mm(dr1, W["w_o"], tb=True, name="mm_o_dx")
    big["w_o"] = slots(_mm(res["merged"], dr1, ta=True, name="mm_o_dw"))
    gates = [Win(z, 1024, Z_GF // 1024), Win(z, 1024, Z_GS // 1024), Win(z, 1024, Z_GC // 1024)]
    dgf, dgs, dgc, dya, dyb2, dyc = _rowwise(_vjp_of(_merge, 6), gates + [res["ya"], res["yb2"], res["yc"], dmerged],
                                             name="merge_bwd")
    zh = lambda off: _heads(z[:, off:off + 512], 8)
    dya_pre = _mm(dya, W["w_fox_o"], tb=True, b_slots=True, name="mm_fox_o_dx")
    big["w_fox_o"] = _mm(res["ya_pre"], dya, ta=True, out_slots=128, name="mm_fox_o_dw")
    cum = res["cum"]
    dfq, dfk, dfv, dcq, dck = _attn_bwd("fox", zh(Z_FQ), zh(Z_FK), zh(Z_FV), res["lse_a"], _heads(dya_pre, 8),
                                        (cum[:, :, None], cum[:, None, :]), name="fox_bwd")
    f_t = z[:, Z_FF:Z_FF + FOX_HEADS].T
    dff = _cum_heads(f_t, dcq[:, :, 0] + dck[:, 0, :], name="fox_cum_bwd")
    dgel = _mm(dyb2, W["w_ssm_glu"], tb=True, b_slots=True, name="mm_glu_dx")
    big["w_ssm_glu"] = _mm(res["gel"], dyb2, ta=True, out_slots=256, name="mm_glu_dw")
    d_row = p["ssm_d"][None, :]
    su_win = Win(z, 512, Z_SU // 512)
    dy, dsu1, dd = _rowwise(_s5_tail_bwd, [res["hc"], su_win, dgel], [d_row], n_red=1, name="s5_tail_bwd")
    small["ssm_d"] = dd[0]
    dh_ri = _mm(dy, sp["cct"], name="mm_s5_out_dx")
    dcct = _mm(dy, res["h_ri"], ta=True, name="mm_s5_out_dw")
    dx_ri, da_vec = _scan(dh_ri, sp["a_vec"], res["h_ri"], name="s5_scan_bwd")
    dsu = _mm(dx_ri, sp["bb"], tb=True, add=(dsu1, 1.0), name="mm_s5_in_dx")
    dbb = _mm(z, dx_ri, ta=True, a_off=Z_SU, a_cols=SSM_WIDTH, name="mm_s5_in_dw")
    dcr, dci = _blockdiag_inv(dcct)
    small["ssm_c_re"], small["ssm_c_im"] = dcr, -dci
    dbbr, dbbi = _blockdiag_inv(dbb)
    to_cn = lambda t: t.transpose(1, 0, 2).reshape(SSM_GROUP, N_SSM_CH)
    dgr, dgi, dbr, dbi = _whole(_vjp_of(_ssm_prep2, 4), *sp["prep2_in"], to_cn(dbbr), to_cn(dbbi), name="ssm_prep2_bwd")
    from_cn = lambda t: t.reshape(SSM_GROUP, SSM_GROUPS, SSM_STATE).transpose(1, 2, 0)
    small["ssm_b_re"], small["ssm_b_im"] = from_cn(dbr), from_cn(dbi)
    dar, dai = _from_blocked(da_vec)
    sq = lambda t: t.reshape(SSM_GROUPS, SSM_STATE)
    dlr, dli, dldt = _whole(_vjp_of(_ssm_prep1, 3), *sp["prep1_in"], sq(dar), sq(dai), sq(dgr), sq(dgi),
                            name="ssm_prep1_bwd")
    small["ssm_lambda_re"], small["ssm_lambda_im"], small["ssm_log_dt"] = dlr, dli, dldt[:, 0]
    dyc_pre = _mm(dyc, W["w_ca_o"], tb=True, b_slots=True, name="mm_ca_o_dx")
    big["w_ca_o"] = _mm(res["yc_pre"], dyc, ta=True, out_slots=128, name="mm_ca_o_dw")
    bias = res["bias"]
    padk = lambda t: jnp.pad(t, ((0, 0), (CA_PAD, 0), (0, 0)))
    dcqh, dckh, dcvh, dbias = _attn_bwd("chunk", zh(Z_CQ), padk(zh(Z_CK)), padk(zh(Z_CV)), res["lse_c"],
                                        _heads(dyc_pre, 8), (bias,), name="chunk_bwd")
    small["ca_rel_bias"] = _relbias_reduce(dbias.transpose(1, 0, 2))
    dff_p = jnp.pad(dff.T, ((0, 0), (0, 512 - FOX_HEADS)))
    dz = jnp.concatenate([_unheads(dfq), _unheads(dfk), _unheads(dfv), dff_p, dsu, _unheads(dcqh),
                          _unheads(dckh[:, CA_PAD:]), _unheads(dcvh[:, CA_PAD:]), dgf, dgs, dgc], axis=1)
    dx = _mm(dz, W["w_in_p"], tb=True, add=(dr1, DN_ALPHA), name="mm_in_dx")
    big["w_in"] = _colmap(_mm(x, dz, ta=True, name="mm_in_dw"), inverse=True, name="w_in_colmap_inv")
    (db_in_p,) = _rowwise(lambda t: (jnp.sum(t, axis=0, keepdims=True),), [dz], n_red=1, name="colsum")
    small["b_in"] = _unpad_b_in(db_in_p)[0]
    return dx, big, small


def kernel(x, mem, w_in, b_in, ssm_lambda_re, ssm_lambda_im, ssm_log_dt, ssm_b_re, ssm_b_im, ssm_c_re, ssm_c_im, ssm_d, ca_rel_bias, w_fox_o, w_ssm_glu, w_ca_o, w_o, xa_wq, xa_wkv, xa_wo, ffn_w_up, ffn_conv_w, ffn_conv_b, ffn_w_down, ln_g, ln_b, loss_target, m_w_in, m_b_in, m_ssm_lambda_re, m_ssm_lambda_im, m_ssm_log_dt, m_ssm_b_re, m_ssm_b_im, m_ssm_c_re, m_ssm_c_im, m_ssm_d, m_ca_rel_bias, m_w_fox_o, m_w_ssm_glu, m_w_ca_o, m_w_o, m_xa_wq, m_xa_wkv, m_xa_wo, m_ffn_w_up, m_ffn_conv_w, m_ffn_conv_b, m_ffn_w_down, m_ln_g, m_ln_b, v_w_in, v_b_in, v_ssm_lambda_re, v_ssm_lambda_im, v_ssm_log_dt, v_ssm_b_re, v_ssm_b_im, v_ssm_c_re, v_ssm_c_im, v_ssm_d, v_ca_rel_bias, v_w_fox_o, v_w_ssm_glu, v_w_ca_o, v_w_o, v_xa_wq, v_xa_wkv, v_xa_wo, v_ffn_w_up, v_ffn_conv_w, v_ffn_conv_b, v_ffn_w_down, v_ln_g, v_ln_b):
    given = dict(locals())
    state = {pre + n: given[pre + n] for n in WEIGHTS for pre in ("", "m_", "v_")}
    mem0 = mem[0]
    layer_params = [{n: state[n][l] for n in REPL} for l in range(DEPTH)]

    _RIDERS.clear()
    h, residuals = x[0], []
    gathered = _all_gather(_layer_shards(state, 0), name="all_gather_weights")
    for l in range(DEPTH):
        riders = []
        if l + 1 < DEPTH:
            shards = _layer_shards(state, l + 1)
            riders = [(idx, _all_gather([shards[i] for i in idx], name="all_gather_weights", host=host))
                      for host, idx in GATHER_HOSTS]
        h, res = _layer_fwd(h, mem0, layer_params[l], _gathered_weights(gathered))
        residuals.append(res)
        gathered = [None] * (len(BIG) + 1)
        for idx, rider in riders:
            for i, r in zip(idx, rider.results):
                gathered[i] = r
    dh, loss_cols = _rowwise(_loss_rows, [h, loss_target[0]], n_red=1, name="loss")
    loss = lax.psum(jnp.sum(loss_cols), ("x", "y", "c"))

    outs = [None] * DEPTH
    wire = [BF16] * len(BIG) + [F32]
    pending = None
    for l in reversed(range(-1, DEPTH)):
        if l >= 0:
            dh, big, small = _layer_bwd(dh, mem0, layer_params[l], residuals[l])
            reduce = _GradReduce([big[n] for n in BIG] + [_pack_small_grads(small)], wire,
                                 hosts=REDUCE_HOSTS if l > 0 else None)
        done, pending = pending, (l, reduce) if l >= 0 else None
        if done is None:
            continue
        l_done, reduce_done = done
        *recv_big, recv_small = reduce_done.result()
        layer_out = {}
        for n, recv in zip(BIG, recv_big):
            if recv.shape[1:] == BIG[n][0]:
                res4 = _adamw(recv, state[n], state["m_" + n], state["v_" + n], l_done, name="adamw_" + n)
            else:
                padded = [_pad_shard(n, state[pre + n][l_done]) for pre in ("", "m_", "v_")]
                res4 = [_unpad_shard(n, t) for t in _adamw(recv, *padded, name="adamw_" + n)]
            layer_out[n] = res4
        packed = _adamw(recv_small, *[_pack_state(state, pre, l_done) for pre in ("", "m_", "v_")], name="adamw_small")
        for n, t4 in zip((*SMALL, *REPL), zip(*[_unpack_state(t).values() for t in packed])):
            layer_out[n] = t4
        outs[l_done] = layer_out

    assert not _RIDERS, [r.host for r in _RIDERS]
    stacked = lambda n, j: jnp.stack([outs[l][n][j] for l in range(DEPTH)])
    return (loss, dh[None], *[stacked(n, j) for j in range(4) for n in WEIGHTS])
```

```python
import functools
import math

import jax
import jax.numpy as jnp
from jax import lax
from jax.experimental import pallas as pl
from jax.experimental.pallas import tpu as pltpu

F32, BF16 = jnp.float32, jnp.bfloat16

D_MODEL = 1024
DEPTH = 4
CHUNK = 64
FOX_HEADS, FOX_HEAD_DIM, FOX_WIDTH = 8, 64, 512
SSM_GROUP, SSM_WIDTH, SSM_GROUPS, SSM_STATE = 16, 512, 32, 64
CA_HEADS, CA_HEAD_DIM, CA_WIDTH, CA_LEFT_CHUNKS = 8, 64, 512, 8
CA_BAND = (CA_LEFT_CHUNKS + 1) * CHUNK
CA_PAD = CA_LEFT_CHUNKS * CHUNK
REL_MIN, REL_MAX = -(CHUNK - 1), 4 * CHUNK
N_REL = REL_MAX - REL_MIN + 1
XA_HEADS, XA_HEAD_DIM = 4, 256
D_FF = 2816
DN_ALPHA = (2 * DEPTH) ** 0.25
LN_EPS = 1e-5
NEG_INF = -1e30
ADAM_LR, ADAM_B1, ADAM_B2, ADAM_EPS, ADAM_WD, ADAM_STEP = 0.001, 0.9, 0.999, 1e-08, 0.01, 10

N_DEV = 8
LANE = 128
N_SSM_CH = SSM_GROUPS * SSM_STATE
SCAN_CB = 256
N_IN = 6664
W_IN_SHARD, W_IN_SLOT = N_IN // N_DEV, 896
Z_W = 7168
Z_FQ, Z_FK, Z_FV, Z_FF, Z_SU, Z_CQ, Z_CK, Z_CV, Z_GF, Z_GS, Z_GC = (
    0, 512, 1024, 1536, 2048, 2560, 3072, 3584, 4096, 5120, 6144)
Z_PIECES = ((0, 512, Z_FQ), (512, 512, Z_FK), (1024, 512, Z_FV), (1536, 8, Z_FF), (1544, 512, Z_SU),
            (2056, 512, Z_CQ), (2568, 512, Z_CK), (3080, 512, Z_CV), (3592, 1024, Z_GF), (4616, 1024, Z_GS),
            (5640, 1024, Z_GC))
FF_HALF, FF_HALF_P = D_FF // N_DEV, 384
D_FF_P = N_DEV * FF_HALF_P

VMEM_LIMIT_BYTES = 56 * 1024 * 1024

BIG = {"w_in": ((1024, W_IN_SHARD), 1), "w_fox_o": ((512, 128), 1), "w_ssm_glu": ((512, 256), 1),
       "w_ca_o": ((512, 128), 1), "w_o": ((128, 1024), 0), "xa_wq": ((128, 1024), 0), "xa_wkv": ((1024, 256), 1),
       "xa_wo": ((128, 1024), 0), "ffn_w_up": ((1024, 2 * FF_HALF), 1), "ffn_w_down": ((FF_HALF, 1024), 0)}
SMALL = {"ca_rel_bias": (8, 320), "ffn_conv_w": (3, D_FF), "ln_g": (3, 1024), "ln_b": (3, 1024)}
REPL = {"b_in": (N_IN,), "ssm_lambda_re": (32, 64), "ssm_lambda_im": (32, 64), "ssm_log_dt": (32,),
        "ssm_b_re": (32, 64, 16), "ssm_b_im": (32, 64, 16), "ssm_c_re": (32, 16, 64), "ssm_c_im": (32, 16, 64),
        "ssm_d": (512,), "ffn_conv_b": (D_FF,)}
WEIGHTS = ("w_in", "b_in", "ssm_lambda_re", "ssm_lambda_im", "ssm_log_dt", "ssm_b_re", "ssm_b_im", "ssm_c_re",
           "ssm_c_im", "ssm_d", "ca_rel_bias", "w_fox_o", "w_ssm_glu", "w_ca_o", "w_o", "xa_wq", "xa_wkv", "xa_wo",
           "ffn_w_up", "ffn_conv_w", "ffn_conv_b", "ffn_w_down", "ln_g", "ln_b")
LANES = 1024
PACK_ROWS = 256
SMALL_ROWS = 8


def _w_in_segments():
    segs = []
    for src, width, dst in Z_PIECES:
        n = src
        while n < src + width:
            d = n // W_IN_SHARD
            end = min(src + width, (d + 1) * W_IN_SHARD)
            segs.append((W_IN_SLOT * d + n - W_IN_SHARD * d, dst + n - src, end - n))
            n = end
    return tuple(segs)


W_IN_SEGS = _w_in_segments()


def _pallas(body, **kw):
    return pl.pallas_call(body, **kw)


def _params(sem):
    return pltpu.CompilerParams(dimension_semantics=sem, vmem_limit_bytes=VMEM_LIMIT_BYTES)


class _Rider:
    def __init__(self, host, ins, out_shapes, sem_shapes, start, finish, then=None):
        self.host, self.ins, self.out_shapes, self.sem_shapes = host, list(ins), list(out_shapes), list(sem_shapes)
        self.start, self.finish, self.then, self.results = start, finish, then, None


_RIDERS = []


def _pcall(body, *, name, **kw):
    rider = next((r for r in _RIDERS if r.host == name), None)
    if rider is None:
        return _pallas(body, name=name, **kw)
    _RIDERS.remove(rider)
    grid, in_specs, scratch = kw["grid"], list(kw["in_specs"]), list(kw.get("scratch_shapes", ()))
    single = not isinstance(kw["out_shape"], (list, tuple))
    out_specs = [kw["out_specs"]] if single else list(kw["out_specs"])
    out_shape = [kw["out_shape"]] if single else list(kw["out_shape"])
    n_in, n_out, n_scr = len(in_specs), len(out_specs), len(scratch)
    r_in, r_out = len(rider.ins), len(rider.out_shapes)

    def fused(*refs):
        a, ra = refs[:n_in], refs[n_in:n_in + r_in]
        o, ro = refs[n_in + r_in:n_in + r_in + n_out], refs[n_in + r_in + n_out:n_in + r_in + n_out + r_out]
        scr, sems = refs[n_in + r_in + n_out + r_out:][:n_scr], refs[n_in + r_in + n_out + r_out + n_scr:]
        ids = [pl.program_id(d) for d in range(len(grid))]
        first = functools.reduce(jnp.logical_and, [i == 0 for i in ids])
        last = functools.reduce(jnp.logical_and, [i == g - 1 for i, g in zip(ids, grid)])

        @pl.when(first)
        def _():
            rider.start(ra, ro, sems)

        body(*a, *o, *scr)

        @pl.when(last)
        def _():
            rider.finish(ra, ro, sems)

    hbm = pl.BlockSpec(memory_space=pl.ANY)
    call = _pallas(fused, name=name, grid=grid, in_specs=in_specs + [hbm] * r_in, out_specs=out_specs + [hbm] * r_out,
                   out_shape=out_shape + rider.out_shapes, scratch_shapes=scratch + rider.sem_shapes,
                   compiler_params=_params(("arbitrary",) * len(grid)))

    def run(*operands):
        outs = call(*operands, *rider.ins)
        rider.results = list(outs[n_out:])
        if rider.then is not None:
            rider.then(rider.results)
        return outs[0] if single else list(outs[:n_out])

    return run


def _pick(dim, prefs):
    for p in prefs:
        if dim % p == 0:
            return p
    return dim


def _mm(a, b, *, ta=False, tb=False, bias=None, add=None, a_off=0, a_cols=None, b_slots=False, out_slots=None,
        name, out_dtype=F32):
    a_cols = a_cols if a_cols is not None else a.shape[1]
    M, K = (a_cols, a.shape[0]) if ta else (a.shape[0], a_cols)
    tm = _pick(M, (1024, 512, 256, 128))
    if ta:
        tk = _pick(K, (2048, 1024, 512, 256))
    elif b_slots and tb:
        tk = _pick(b.shape[2], (1024, 768, 512, 256, 128))
    else:
        tk = K if K <= 3072 else _pick(K, (1024, 512, 256, 128))
    nk = K // tk
    if b_slots:
        ns = b.shape[2]
        if tb:
            N = b.shape[1]
            tn = _pick(N, (512, 256, 128))
            per = ns // tk
            b_spec = pl.BlockSpec((None, tn, tk), lambda i, j, k: (k // per, j, k % per))
            b_dim = 1
            assert N_DEV * ns == K
        else:
            N = N_DEV * ns
            tn = _pick(ns, (512, 256, 128))
            per = ns // tn
            b_spec = pl.BlockSpec((None, tk, tn), lambda i, j, k: (j // per, k, j % per))
            b_dim = 0
            assert b.shape[1] == K
    else:
        N = b.shape[0] if tb else b.shape[1]
        assert (b.shape[1] if tb else b.shape[0]) == K, (a.shape, b.shape, ta, tb)
        tn = _pick(N if out_slots is None else out_slots, (512, 256, 128))
        if tb:
            b_spec = pl.BlockSpec((tn, tk), lambda i, j, k: (j, k))
            b_dim = 1
        else:
            b_spec = pl.BlockSpec((tk, tn), lambda i, j, k: (k, j))
            b_dim = 0
    if ta:
        assert a_off % tm == 0
        a_spec = pl.BlockSpec((tk, tm), lambda i, j, k: (k, i + a_off // tm))
        a_dim = 0
    else:
        assert a_off % tk == 0
        a_spec = pl.BlockSpec((tm, tk), lambda i, j, k: (i, k + a_off // tk))
        a_dim = 1
    cache_at = ta and nk == 1
    dims = (((1 if cache_at else a_dim,), (b_dim,)), ((), ()))
    ins, specs = [a, b], [a_spec, b_spec]
    if bias is not None:
        ins.append(bias)
        specs.append(pl.BlockSpec((1, tn), lambda i, j, k: (0, j)))
    add_scale = None
    if add is not None:
        ins.append(add[0])
        add_scale = add[1]
        specs.append(pl.BlockSpec((tm, tn), lambda i, j, k: (i, j)))
    if out_slots is None:
        out_spec = pl.BlockSpec((tm, tn), lambda i, j, k: (i, j))
        out_shape = jax.ShapeDtypeStruct((M, N), out_dtype)
    else:
        assert N == N_DEV * out_slots
        per_o = out_slots // tn
        out_spec = pl.BlockSpec((None, tm, tn), lambda i, j, k: (j // per_o, i, j % per_o))
        out_shape = jax.ShapeDtypeStruct((N_DEV, M, out_slots), out_dtype)

    def body(*refs):
        a_ref, b_ref = refs[0], refs[1]
        pos = 2
        bias_ref = add_ref = None
        if bias is not None:
            bias_ref = refs[pos]
            pos += 1
        if add is not None:
            add_ref = refs[pos]
            pos += 1
        o_ref = refs[pos]
        acc_ref = refs[pos + 1] if nk > 1 else None
        if cache_at:
            at_ref = refs[pos + 1]

            @pl.when(pl.program_id(1) == 0)
            def _():
                step = min(tk, 256)
                for c in range(0, tk, step):
                    at_ref[:, c:c + step] = a_ref[c:c + step, :].T.astype(BF16)

            lhs = at_ref[...]
        else:
            lhs = a_ref[...].astype(BF16)
        part = lax.dot_general(lhs, b_ref[...].astype(BF16), dims, preferred_element_type=F32)

        def finish(acc):
            if bias_ref is not None:
                acc = acc + bias_ref[...]
            if add_ref is not None:
                acc = acc + add_scale * add_ref[...]
            o_ref[...] = acc.astype(out_dtype)

        if nk == 1:
            finish(part)
        else:
            k = pl.program_id(2)

            @pl.when(k == 0)
            def _():
                acc_ref[...] = part

            @pl.when(k > 0)
            def _():
                acc_ref[...] += part

            @pl.when(k == nk - 1)
            def _():
                finish(acc_ref[...])

    return _pcall(
        body, name=name, grid=(M // tm, N // tn, nk), in_specs=specs, out_specs=out_spec, out_shape=out_shape,
        scratch_shapes=[pltpu.VMEM((tm, tn), F32)] if nk > 1 else [pltpu.VMEM((tm, tk), BF16)] if cache_at else [],
        compiler_params=_params(("parallel", "arbitrary", "arbitrary")),
    )(*ins)


class Win:
    def __init__(self, arr, width, blk):
        self.arr, self.width, self.blk = arr, width, blk


def _rowwise(fn, rows, vecs=(), *, n_red=0, tr=256, name):
    wins = [r if isinstance(r, Win) else Win(r, r.shape[1], 0) for r in rows]
    S = wins[0].arr.shape[0]
    tr = min(tr, S)
    tile_args = [jax.ShapeDtypeStruct((tr, w.width), w.arr.dtype) for w in wins]
    tile_args += [jax.ShapeDtypeStruct(v.shape, v.dtype) for v in vecs]
    outs = jax.eval_shape(fn, *tile_args)
    n_row = len(outs) - n_red
    specs = [pl.BlockSpec((tr, w.width), functools.partial(lambda i, b: (i, b), b=w.blk)) for w in wins]
    specs += [pl.BlockSpec(v.shape, functools.partial(lambda i, nd: (0,) * nd, nd=v.ndim)) for v in vecs]
    out_specs = [pl.BlockSpec((tr, o.shape[1]), lambda i: (i, 0)) for o in outs[:n_row]]
    out_specs += [pl.BlockSpec(o.shape, functools.partial(lambda i, nd: (0,) * nd, nd=len(o.shape))) for o in outs[n_row:]]
    out_shape = [jax.ShapeDtypeStruct((S, o.shape[1]), o.dtype) for o in outs[:n_row]]
    out_shape += [jax.ShapeDtypeStruct(o.shape, o.dtype) for o in outs[n_row:]]
    n_in = len(wins) + len(vecs)

    def body(*refs):
        res = fn(*[r[...] for r in refs[:n_in]])
        o_refs = refs[n_in:]
        for o_ref, r in zip(o_refs[:n_row], res[:n_row]):
            o_ref[...] = r.astype(o_ref.dtype)
        i = pl.program_id(0)
        for o_ref, r in zip(o_refs[n_row:], res[n_row:]):
            @pl.when(i == 0)
            def _(o_ref=o_ref, r=r):
                o_ref[...] = r

            @pl.when(i > 0)
            def _(o_ref=o_ref, r=r):
                o_ref[...] += r

    return _pcall(
        body, name=name, grid=(S // tr,), in_specs=specs, out_specs=out_specs, out_shape=out_shape,
        compiler_params=_params(("arbitrary",)),
    )(*[w.arr for w in wins], *vecs)


def _whole(fn, *arrays, name):
    outs = jax.eval_shape(fn, *arrays)
    n_in = len(arrays)

    def body(*refs):
        res = fn(*[r[...] for r in refs[:n_in]])
        for o_ref, r in zip(refs[n_in:], res):
            o_ref[...] = r

    vm = pl.BlockSpec(memory_space=pltpu.VMEM)
    return _pcall(body, name=name, in_specs=[vm] * n_in, out_specs=[vm] * len(outs),
                  out_shape=[jax.ShapeDtypeStruct(o.shape, o.dtype) for o in outs])(*arrays)


def _split3(x):
    hi = x.astype(BF16)
    r = x - hi.astype(F32)
    mid = r.astype(BF16)
    lo = (r - mid.astype(F32)).astype(BF16)
    return hi, mid, lo


def _dot3(x, onehot, dims):
    return sum(lax.dot_general(t, onehot, dims, preferred_element_type=F32) for t in _split3(x))


_NT = (((1,), (1,)), ((), ()))
_NN = (((1,), (0,)), ((), ()))
_TN = (((0,), (0,)), ((), ()))


def _colmap(x, *, inverse, name):
    R = x.shape[0] if inverse else x.shape[1]
    tr = 256
    per = W_IN_SLOT // LANE
    n_out = N_DEV * per if inverse else Z_W // LANE
    segs = [(p, q, n) for q, p, n in W_IN_SEGS] if inverse else list(W_IN_SEGS)

    def body(x_ref, o_ref):
        ia = lax.broadcasted_iota(jnp.int32, (LANE, LANE), 0)
        ib = lax.broadcasted_iota(jnp.int32, (LANE, LANE), 1)

        def src_block(i):
            if inverse:
                return x_ref[:, i * LANE:(i + 1) * LANE]
            return x_ref[i // per, :, (i % per) * LANE:(i % per + 1) * LANE]

        for jb in range(n_out):
            acc = None
            for s0, d0, n in segs:
                lo, hi = max(d0, jb * LANE), min(d0 + n, (jb + 1) * LANE)
                if lo >= hi:
                    continue
                delta = d0 - s0
                for i in range((lo - delta) // LANE, (hi - delta - 1) // LANE + 1):
                    shift = jb * LANE - i * LANE - delta
                    sel = ((ia - ib == shift) & (ib >= lo - jb * LANE) & (ib < hi - jb * LANE)).astype(BF16)
                    blk = src_block(i)
                    part = _dot3(blk, sel, _NN) if inverse else lax.dot_general(blk, sel, _NN, preferred_element_type=F32)
                    acc = part if acc is None else acc + part
            if acc is None:
                acc = jnp.zeros((tr, LANE), F32)
            if inverse:
                o_ref[jb // per, :, (jb % per) * LANE:(jb % per + 1) * LANE] = acc
            else:
                o_ref[:, jb * LANE:(jb + 1) * LANE] = acc.astype(BF16)

    slot_spec = pl.BlockSpec((N_DEV, tr, W_IN_SLOT), lambda i: (0, i, 0))
    flat_spec = pl.BlockSpec((tr, Z_W), lambda i: (i, 0))
    if inverse:
        return _pcall(body, name=name, grid=(R // tr,), in_specs=[flat_spec], out_specs=slot_spec,
                      out_shape=jax.ShapeDtypeStruct((N_DEV, R, W_IN_SLOT), F32), compiler_params=_params(("parallel",)))(x)
    return _pcall(body, name=name, grid=(R // tr,), in_specs=[slot_spec], out_specs=flat_spec,
                  out_shape=jax.ShapeDtypeStruct((R, Z_W), BF16), compiler_params=_params(("parallel",)))(x)


def _log_sigmoid(x):
    return jnp.minimum(x, 0.0) - jnp.log(1.0 + jnp.exp(-jnp.abs(x)))


def _cum_heads(f, dcum=None, *, name):
    H, S = f.shape
    tn = min(512, S)
    rev = dcum is not None

    def body(*refs):
        j = pl.program_id(0)
        s_idx = lax.broadcasted_iota(jnp.int32, (S, tn), 0)
        t_idx = lax.broadcasted_iota(jnp.int32, (S, tn), 1) + j * tn
        if not rev:
            f_ref, o_ref = refs
            tri = (s_idx <= t_idx).astype(BF16)
            o_ref[...] = _dot3(_log_sigmoid(f_ref[...]), tri, _NN)
        else:
            fj_ref, d_ref, o_ref = refs
            tri = (s_idx >= t_idx).astype(BF16)
            o_ref[...] = _dot3(d_ref[...], tri, _NN) * jax.nn.sigmoid(-fj_ref[...])

    full = pl.BlockSpec((H, S), lambda j: (0, 0))
    blk = pl.BlockSpec((H, tn), lambda j: (0, j))
    ins, specs = ([f], [full]) if not rev else ([f, dcum], [blk, full])
    return _pcall(body, name=name, grid=(S // tn,), in_specs=specs, out_specs=blk,
                  out_shape=jax.ShapeDtypeStruct((H, S), F32), compiler_params=_params(("arbitrary",)))(*ins)


def _rel_onehot(qi, transposed):
    shape = (N_REL, CA_BAND) if transposed else (CA_BAND, N_REL)
    kk = lax.broadcasted_iota(jnp.int32, shape, 1 if transposed else 0)
    rr = lax.broadcasted_iota(jnp.int32, shape, 0 if transposed else 1)
    idx = jnp.clip(CA_PAD + qi - kk, REL_MIN, REL_MAX) - REL_MIN
    return (idx == rr).astype(BF16)


def _relbias_expand(rb):
    def body(rb_ref, o_ref):
        o_ref[0] = _dot3(rb_ref[...], _rel_onehot(pl.program_id(0), True), _NN)

    return _pcall(body, name="relbias_expand", grid=(CHUNK,),
                  in_specs=[pl.BlockSpec((CA_HEADS, N_REL), lambda q: (0, 0))],
                  out_specs=pl.BlockSpec((1, CA_HEADS, CA_BAND), lambda q: (q, 0, 0)),
                  out_shape=jax.ShapeDtypeStruct((CHUNK, CA_HEADS, CA_BAND), F32),
                  compiler_params=_params(("arbitrary",)))(rb)


def _relbias_reduce(db):
    def body(db_ref, o_ref):
        q = pl.program_id(0)
        part = _dot3(db_ref[0], _rel_onehot(q, False), _NN)

        @pl.when(q == 0)
        def _():
            o_ref[...] = part

        @pl.when(q > 0)
        def _():
            o_ref[...] += part

    return _pcall(body, name="relbias_reduce", grid=(CHUNK,),
                  in_specs=[pl.BlockSpec((1, CA_HEADS, CA_BAND), lambda q: (q, 0, 0))],
                  out_specs=pl.BlockSpec((CA_HEADS, N_REL), lambda q: (0, 0)),
                  out_shape=jax.ShapeDtypeStruct((CA_HEADS, N_REL), F32),
                  compiler_params=_params(("arbitrary",)))(db)


def _attn_cfg(mode, S):
    if mode == "fox":
        return min(256, S), FOX_HEAD_DIM ** -0.5
    if mode == "chunk":
        return CHUNK, CA_HEAD_DIM ** -0.5
    return min(512, S), XA_HEAD_DIM ** -0.5


def _scores(mode, i, tq, scale, qb, kb, extra):
    s = lax.dot_general(qb, kb, _NT, preferred_element_type=F32) * scale
    nk = kb.shape[0]
    if mode == "fox":
        cq, ck = extra
        s = s + cq - ck
        row = lax.broadcasted_iota(jnp.int32, (tq, nk), 0) + i * tq
        col = lax.broadcasted_iota(jnp.int32, (tq, nk), 1)
        s = jnp.where(row >= col, s, NEG_INF)
    elif mode == "chunk":
        (bias,) = extra
        col = lax.broadcasted_iota(jnp.int32, (tq, nk), 1) + i * CHUNK
        s = jnp.where(col >= CA_PAD, s + bias, NEG_INF)
    return s


def _attn_specs(mode, q, k, tq, hg):
    H, S, D = q.shape
    Sk = k.shape[1]
    q_spec = pl.BlockSpec((hg, tq, D), lambda g, i: (g, i, 0))
    kv_spec = pl.BlockSpec((hg, Sk, D), lambda g, i: (g, 0, 0))
    if mode == "fox":
        ex = [pl.BlockSpec((hg, tq, 1), lambda g, i: (g, i, 0)), pl.BlockSpec((hg, 1, Sk), lambda g, i: (g, 0, 0))]
    elif mode == "chunk":
        ex = [pl.BlockSpec((hg, CHUNK, CA_BAND), lambda g, i: (g, 0, 0))]
    else:
        ex = []
    col_spec = pl.BlockSpec((hg, tq, 1), lambda g, i: (g, i, 0))
    return q_spec, kv_spec, ex, col_spec


def _kv_window(mode, i, ref, h):
    if mode == "chunk":
        return ref[h, pl.ds(pl.multiple_of(i * CHUNK, CHUNK), CA_BAND), :]
    return ref[h]


def _attn_fwd(mode, q, k, v, extra=(), *, name):
    H, S, D = q.shape
    tq, scale = _attn_cfg(mode, S)
    hg = H
    q_spec, kv_spec, ex_specs, col_spec = _attn_specs(mode, q, k, tq, hg)
    n_ex = len(extra)

    def body(*refs):
        q_ref, k_ref, v_ref = refs[:3]
        o_ref, lse_ref = refs[3 + n_ex:]
        i = pl.program_id(1)
        for h in range(hg):
            ex = [r[h] for r in refs[3:3 + n_ex]]
            kb = _kv_window(mode, i, k_ref, h).astype(BF16)
            vb = _kv_window(mode, i, v_ref, h).astype(BF16)
            s = _scores(mode, i, tq, scale, q_ref[h].astype(BF16), kb, ex)
            m = jnp.max(s, axis=1, keepdims=True)
            e = jnp.exp(s - m)
            l = jnp.sum(e, axis=1, keepdims=True)
            p = e / l
            o_ref[h] = lax.dot_general(p.astype(BF16), vb, _NN, preferred_element_type=F32)
            lse_ref[h] = m + jnp.log(l)

    return _pcall(
        body, name=name, grid=(H // hg, S // tq), in_specs=[q_spec, kv_spec, kv_spec] + ex_specs,
        out_specs=[q_spec, col_spec],
        out_shape=[jax.ShapeDtypeStruct((H, S, D), F32), jax.ShapeDtypeStruct((H, S, 1), F32)],
        compiler_params=_params(("parallel", "arbitrary")),
    )(q, k, v, *extra)


def _attn_bwd(mode, q, k, v, lse, do, extra=(), *, name):
    H, S, D = q.shape
    Sk = k.shape[1]
    tq, scale = _attn_cfg(mode, S)
    hg = min(H, 4)
    q_spec, kv_spec, ex_specs, col_spec = _attn_specs(mode, q, k, tq, hg)
    n_ex = len(extra)
    out_specs = [q_spec, kv_spec, kv_spec]
    out_shape = [jax.ShapeDtypeStruct((H, S, D), F32), jax.ShapeDtypeStruct((H, Sk, D), F32),
                 jax.ShapeDtypeStruct((H, Sk, D), F32)]
    if mode == "fox":
        out_specs += [col_spec, ex_specs[1]]
        out_shape += [jax.ShapeDtypeStruct((H, S, 1), F32), jax.ShapeDtypeStruct((H, 1, Sk), F32)]
    elif mode == "chunk":
        out_specs += [ex_specs[0]]
        out_shape += [jax.ShapeDtypeStruct((H, CHUNK, CA_BAND), F32)]

    def body(*refs):
        q_ref, k_ref, v_ref, lse_ref, do_ref = refs[:5]
        dq_ref, dk_ref, dv_ref = refs[5 + n_ex:8 + n_ex]
        rest = refs[8 + n_ex:]
        i = pl.program_id(1)

        @pl.when(i == 0)
        def _():
            dk_ref[...] = jnp.zeros_like(dk_ref)
            dv_ref[...] = jnp.zeros_like(dv_ref)
            if mode == "fox":
                rest[1][...] = jnp.zeros_like(rest[1])
            elif mode == "chunk":
                rest[0][...] = jnp.zeros_like(rest[0])

        for h in range(hg):
            ex = [r[h] for r in refs[5:5 + n_ex]]
            qb = q_ref[h].astype(BF16)
            kb = _kv_window(mode, i, k_ref, h).astype(BF16)
            vb = _kv_window(mode, i, v_ref, h).astype(BF16)
            dob = do_ref[h].astype(BF16)
            s = _scores(mode, i, tq, scale, qb, kb, ex)
            p = jnp.exp(s - lse_ref[h])
            dp = lax.dot_general(dob, vb, _NT, preferred_element_type=F32)
            ds = p * (dp - jnp.sum(dp * p, axis=1, keepdims=True))
            dsb = (ds * scale).astype(BF16)
            dq_ref[h] = lax.dot_general(dsb, kb, _NN, preferred_element_type=F32)
            dk_part = lax.dot_general(dsb, qb, _TN, preferred_element_type=F32)
            dv_part = lax.dot_general(p.astype(BF16), dob, _TN, preferred_element_type=F32)
            if mode == "chunk":
                win = pl.ds(pl.multiple_of(i * CHUNK, CHUNK), CA_BAND)
                dk_ref[h, win, :] += dk_part
                dv_ref[h, win, :] += dv_part
                rest[0][h] += ds
            else:
                dk_ref[h] += dk_part
                dv_ref[h] += dv_part
            if mode == "fox":
                rest[0][h] = jnp.sum(ds, axis=1, keepdims=True)
                rest[1][h] += -jnp.sum(ds, axis=0, keepdims=True)

    return _pcall(
        body, name=name, grid=(H // hg, S // tq),
        in_specs=[q_spec, kv_spec, kv_spec, col_spec, q_spec] + ex_specs,
        out_specs=out_specs, out_shape=out_shape,
        compiler_params=_params(("parallel", "arbitrary")),
    )(q, k, v, lse, do, *extra)


def _scan(x, a, h=None, *, name):
    S = x.shape[0]
    CB = SCAN_CB
    rev = h is not None
    n_grp = S // 8

    def body(*refs):
        if rev:
            x_ref, a_ref, h_ref, o_ref, da_ref = refs
        else:
            x_ref, a_ref, o_ref = refs
        ar = a_ref[:, :CB]
        ai = -a_ref[:, CB:] if rev else a_ref[:, CB:]
        zero = jnp.zeros((1, CB), F32)

        def group(g, carry):
            base = pl.multiple_of((n_grp - 1 - g) * 8 if rev else g * 8, 8)
            for j in (range(7, -1, -1) if rev else range(8)):
                t = base + j
                if rev:
                    hr, hi, dar, dai = carry
                else:
                    hr, hi = carry
                xr = x_ref[pl.ds(t, 1), :CB]
                xi = x_ref[pl.ds(t, 1), CB:]
                hr, hi = ar * hr - ai * hi + xr, ar * hi + ai * hr + xi
                o_ref[pl.ds(t, 1), :CB] = hr
                o_ref[pl.ds(t, 1), CB:] = hi
                if rev:
                    tp = jnp.maximum(t - 1, 0)
                    live = (t > 0).astype(F32)
                    pr = h_ref[pl.ds(tp, 1), :CB] * live
                    pi = h_ref[pl.ds(tp, 1), CB:] * live
                    carry = (hr, hi, dar + hr * pr + hi * pi, dai + hi * pr - hr * pi)
                else:
                    carry = (hr, hi)
            return carry

        if rev:
            _, _, dar, dai = lax.fori_loop(0, n_grp, group, (zero, zero, zero, zero))
            da_ref[:, :CB] = dar
            da_ref[:, CB:] = dai
        else:
            lax.fori_loop(0, n_grp, group, (zero, zero))

    big = pl.BlockSpec((S, 2 * CB), lambda c: (0, c))
    vec = pl.BlockSpec((1, 2 * CB), lambda c: (0, c))
    n_blk = x.shape[1] // (2 * CB)
    if rev:
        return _pcall(body, name=name, grid=(n_blk,), in_specs=[big, vec, big], out_specs=[big, vec],
                      out_shape=[jax.ShapeDtypeStruct(x.shape, F32), jax.ShapeDtypeStruct(a.shape, F32)],
                      compiler_params=_params(("parallel",)))(x, a, h)
    return _pcall(body, name=name, grid=(n_blk,), in_specs=[big, vec], out_specs=big,
                  out_shape=jax.ShapeDtypeStruct(x.shape, F32), compiler_params=_params(("parallel",)))(x, a)


def _ssm_prep1(lr_, li, ldt):
    lr = jnp.minimum(lr_, -1e-4)
    dt = jnp.exp(ldt)
    mag = jnp.exp(lr * dt)
    ar = mag * jnp.cos(li * dt)
    ai = mag * jnp.sin(li * dt)
    den = lr * lr + li * li
    gr = ((ar - 1.0) * lr + ai * li) / den
    gi = (ai * lr - (ar - 1.0) * li) / den
    return ar, ai, gr, gi


def _ssm_prep2(gr, gi, br, bi):
    return gr * br - gi * bi, gr * bi + gi * br


def _vjp_of(fn, n_in):
    def bwd(*args):
        cts = args[n_in:]
        return jax.vjp(fn, *args[:n_in])[1](cts[0] if len(cts) == 1 else tuple(cts))
    return bwd


def _to_blocked(r, i):
    lead = r.shape[:-1]
    t = jnp.stack([r.reshape(lead + (N_SSM_CH // SCAN_CB, SCAN_CB)), i.reshape(lead + (N_SSM_CH // SCAN_CB, SCAN_CB))],
                  axis=-2)
    return t.reshape(lead + (2 * N_SSM_CH,))


def _from_blocked(m):
    lead = m.shape[:-1]
    t = m.reshape(lead + (N_SSM_CH // SCAN_CB, 2, SCAN_CB))
    return t[..., 0, :].reshape(lead + (N_SSM_CH,)), t[..., 1, :].reshape(lead + (N_SSM_CH,))


def _blockdiag(r, i):
    eye = jnp.eye(SSM_GROUPS, dtype=F32)
    bd = lambda t: jnp.einsum("gcp,gh->gchp", t, eye).reshape(SSM_WIDTH, N_SSM_CH)
    return _to_blocked(bd(r), bd(i))


def _blockdiag_inv(m):
    eye = jnp.eye(SSM_GROUPS, dtype=F32)
    r, i = _from_blocked(m)
    diag = lambda t: jnp.einsum("gchp,gh->gcp", t.reshape(SSM_GROUPS, SSM_GROUP, SSM_GROUPS, SSM_STATE), eye)
    return diag(r), diag(i)


def _shift_rows(x, n):
    S = x.shape[0]
    row = lax.broadcasted_iota(jnp.int32, x.shape, 0)
    if n > 0:
        return jnp.where(row >= n, pltpu.roll(x, n, 0), 0.0)
    return jnp.where(row < S + n, pltpu.roll(x, S + n, 0), 0.0)


def _bf(x):
    return x.astype(BF16).astype(F32)


def _conv_pre(a, w, b):
    ab, wb = _bf(a), _bf(w)
    return wb[2:3] * ab + wb[1:2] * _shift_rows(ab, 1) + wb[0:1] * _shift_rows(ab, 2) + b


def _ffn_mid(up, conv_w, conv_b, dh=None, *, name):
    S = up.shape[0]
    tn = LANE
    nb = D_FF_P // tn
    rev = dh is not None

    def body(*refs):
        if not rev:
            a_ref, g_ref, w_ref, b_ref, o_ref = refs
            o_ref[...] = jax.nn.gelu(_conv_pre(a_ref[...], w_ref[...], b_ref[...])) * g_ref[...]
            return
        a_ref, g_ref, w_ref, b_ref, dh_ref, dup_a_ref, dup_g_ref, dw_ref, db_ref = refs
        a, w, dh_ = a_ref[...], w_ref[...], dh_ref[...]
        pre = _conv_pre(a, w, b_ref[...])
        gl, gelu_vjp = jax.vjp(jax.nn.gelu, pre)
        dup_g_ref[...] = dh_ * gl
        (dpre,) = gelu_vjp(dh_ * g_ref[...])
        db_ref[...] = jnp.sum(dpre, axis=0, keepdims=True)
        dpb, ab, wb = _bf(dpre), _bf(a), _bf(w)
        dup_a_ref[...] = wb[2:3] * dpb + wb[1:2] * _shift_rows(dpb, -1) + wb[0:1] * _shift_rows(dpb, -2)
        dw_ref[2:3, :] = jnp.sum(dpb * ab, axis=0, keepdims=True)
        dw_ref[1:2, :] = jnp.sum(dpb * _shift_rows(ab, 1), axis=0, keepdims=True)
        dw_ref[0:1, :] = jnp.sum(dpb * _shift_rows(ab, 2), axis=0, keepdims=True)

    a_spec = pl.BlockSpec((S, tn), lambda j: (0, j))
    g_spec = pl.BlockSpec((S, tn), lambda j: (0, j + nb))
    w_spec = pl.BlockSpec((3, tn), lambda j: (0, j))
    b_spec = pl.BlockSpec((1, tn), lambda j: (0, j))
    if not rev:
        return _pcall(body, name=name, grid=(nb,), in_specs=[a_spec, g_spec, w_spec, b_spec], out_specs=a_spec,
                      out_shape=jax.ShapeDtypeStruct((S, D_FF_P), F32), compiler_params=_params(("parallel",)))(
                          up, up, conv_w, conv_b)
    return _pcall(body, name=name, grid=(nb,), in_specs=[a_spec, g_spec, w_spec, b_spec, a_spec],
                  out_specs=[a_spec, a_spec, w_spec, b_spec],
                  out_shape=[jax.ShapeDtypeStruct((S, D_FF_P), F32), jax.ShapeDtypeStruct((S, D_FF_P), F32),
                             jax.ShapeDtypeStruct((3, D_FF_P), F32), jax.ShapeDtypeStruct((1, D_FF_P), F32)],
                  compiler_params=_params(("parallel",)))(up, up, conv_w, conv_b, dh)


def _ff_pad(t):
    lead = t.shape[:-1]
    t = t.reshape(lead + (N_DEV, FF_HALF))
    return jnp.pad(t, [(0, 0)] * len(lead) + [(0, 0), (0, FF_HALF_P - FF_HALF)]).reshape(lead + (D_FF_P,))


def _ff_unpad(t):
    lead = t.shape[:-1]
    return t.reshape(lead + (N_DEV, FF_HALF_P))[..., :FF_HALF].reshape(lead + (D_FF,))


def _ln_fwd(x, h, g, b):
    r = DN_ALPHA * x + h
    mu = jnp.mean(r, axis=-1, keepdims=True)
    var = jnp.mean(jnp.square(r - mu), axis=-1, keepdims=True)
    return r, (r - mu) * lax.rsqrt(var + LN_EPS) * g + b


def _ln_bwd(r, dy, g):
    mu = jnp.mean(r, axis=-1, keepdims=True)
    var = jnp.mean(jnp.square(r - mu), axis=-1, keepdims=True)
    xhat = (r - mu) * lax.rsqrt(var + LN_EPS)
    dxh = dy * g
    dr = lax.rsqrt(var + LN_EPS) * (dxh - jnp.mean(dxh, axis=-1, keepdims=True)
                                    - xhat * jnp.mean(dxh * xhat, axis=-1, keepdims=True))
    return dr, jnp.sum(dy * xhat, axis=0, keepdims=True), jnp.sum(dy, axis=0, keepdims=True)


def _merge(gf, gs, gc, ya, yb2, yc):
    yb = yb2[:, :D_MODEL] * jax.nn.sigmoid(yb2[:, D_MODEL:])
    return jax.nn.sigmoid(gf) * ya + jax.nn.sigmoid(gs) * yb + jax.nn.sigmoid(gc) * yc


def _s5_tail(hc, su, d):
    return jax.nn.gelu(hc + d * su)


def _s5_tail_bwd(hc, su, dgel, d):
    _, vjp = jax.vjp(jax.nn.gelu, hc + d * su)
    (dy,) = vjp(dgel)
    return dy, d * dy, jnp.sum(dy * su, axis=0, keepdims=True)


def _loss_rows(y, tgt):
    err = y - tgt
    return err * (1.0 / D_MODEL), jnp.sum(0.5 * jnp.square(err), axis=0, keepdims=True) * (1.0 / D_MODEL)


def _peer(k):
    x, y, c = lax.axis_index("x"), lax.axis_index("y"), lax.axis_index("c")
    return (x ^ ((k >> 2) & 1), y ^ ((k >> 1) & 1), c ^ (k & 1))


def _my_slot():
    return 4 * lax.axis_index("x") + 2 * lax.axis_index("y") + lax.axis_index("c")


def _peer_slot(k):
    px, py, pc = _peer(k)
    return 4 * px + 2 * py + pc


N_CHIP = N_DEV // 2
OTHER_CHIPS = (2, 4, 6)


def _chip_of(dev):
    return 2 * dev[0] + dev[1]


def _remote(src, dst, send, recv, dev):
    return pltpu.make_async_remote_copy(src_ref=src, dst_ref=dst, send_sem=send, recv_sem=recv, device_id=dev,
                                        device_id_type=pl.DeviceIdType.MESH)


def _all_gather(shards, *, name, host=None, then=None):
    return _exchange(shards, *_all_gather_parts(shards), name=name, host=host, then=then)


def _exchange(ins, out_shapes, sem_shapes, start, finish, *, name, host=None, then=None):
    if host is not None:
        rider = _Rider(host, ins, out_shapes, sem_shapes, start, finish, then)
        _RIDERS.append(rider)
        return rider
    n = len(ins)

    def body(*refs):
        start(refs[:n], refs[n:2 * n], refs[2 * n:])
        finish(refs[:n], refs[n:2 * n], refs[2 * n:])

    hbm = pl.BlockSpec(memory_space=pl.ANY)
    return _pcall(body, name=name, in_specs=[hbm] * n, out_specs=[hbm] * n, out_shape=list(out_shapes),
                  scratch_shapes=list(sem_shapes))(*ins)


def _all_gather_parts(shards):
    n = len(shards)

    def first_copies(ins, outs, sems):
        send, recv, _ = sems
        return [_remote(ins[t], outs[t].at[_my_slot()], send.at[t, k - 1], recv.at[t, k - 1], _peer(k))
                for k in (1,) + OTHER_CHIPS for t in range(n)]

    def local_copies(ins, outs, sems):
        return [pltpu.make_async_copy(ins[t], outs[t].at[_my_slot()], sems[2].at[t]) for t in range(n)]

    def start(ins, outs, sems):
        for cp in local_copies(ins, outs, sems) + first_copies(ins, outs, sems):
            cp.start()

    def finish(ins, outs, sems):
        send, recv, _ = sems
        sibling = _peer(1)
        passed = []
        for k in OTHER_CHIPS:
            for t in range(n):
                slot = outs[t].at[_peer_slot(k)]
                _remote(ins[t], slot, send.at[t, k - 1], recv.at[t, k - 1], _peer(k)).wait_recv()
                cp = _remote(slot, slot, send.at[t, k], recv.at[t, k], sibling)
                cp.start()
                passed.append(cp)
        for t in range(n):
            _remote(ins[t], outs[t].at[_peer_slot(1)], send.at[t, 0], recv.at[t, 0], sibling).wait_recv()
            for k in OTHER_CHIPS:
                _remote(ins[t], outs[t].at[_peer_slot(k + 1)], send.at[t, k], recv.at[t, k], sibling).wait_recv()
        for cp in first_copies(ins, outs, sems) + passed:
            cp.wait_send()
        for lc in local_copies(ins, outs, sems):
            lc.wait()

    out_shapes = [jax.ShapeDtypeStruct((N_DEV,) + s.shape, s.dtype) for s in shards]
    sem_shapes = [pltpu.SemaphoreType.DMA((n, N_DEV - 1)), pltpu.SemaphoreType.DMA((n, N_DEV - 1)),
                  pltpu.SemaphoreType.DMA((n,))]
    return out_shapes, sem_shapes, start, finish


def _sibling_swap(grads, *, name, host=None, then=None):
    n = len(grads)

    def copies(ins, outs, sems):
        c = lax.axis_index("c")
        return [_remote(ins[t].at[:, 1 - c], outs[t], sems[0].at[t], sems[1].at[t], _peer(1)) for t in range(n)]

    def start(ins, outs, sems):
        for cp in copies(ins, outs, sems):
            cp.start()

    def finish(ins, outs, sems):
        for cp in copies(ins, outs, sems):
            cp.wait()

    out_shapes = [jax.ShapeDtypeStruct((N_CHIP,) + g.shape[2:], g.dtype) for g in grads]
    sem_shapes = [pltpu.SemaphoreType.DMA((n,)), pltpu.SemaphoreType.DMA((n,))]
    return _exchange(grads, out_shapes, sem_shapes, start, finish, name=name, host=host, then=then)


def _pair_add(g, p, out_dtype, *, name):
    _, _, R, C = g.shape
    tr = _pick(R, (512, 256, 128, 64, 32, 16, 8))

    def body(c_ref, g_ref, p_ref, o_ref):
        o_ref[...] = (g_ref[...] + p_ref[...]).astype(out_dtype)

    grid_spec = pltpu.PrefetchScalarGridSpec(
        num_scalar_prefetch=1, grid=(N_CHIP, R // tr),
        in_specs=[pl.BlockSpec((None, None, tr, C), lambda j, i, c_ref: (j, c_ref[0], i, 0)),
                  pl.BlockSpec((None, tr, C), lambda j, i, c_ref: (j, i, 0))],
        out_specs=pl.BlockSpec((None, tr, C), lambda j, i, c_ref: (j, i, 0)))
    core = lax.axis_index("c").astype(jnp.int32).reshape(1)
    return _pcall(body, name=name, grid_spec=grid_spec, out_shape=jax.ShapeDtypeStruct(p.shape, out_dtype),
                  compiler_params=_params(("parallel", "parallel")))(core, g, p)


def _chip_exchange(sums, *, name, host=None):
    n = len(sums)

    def copies(ins, outs, sems, dst_is_mine):
        send, recv, _ = sems
        mine = 2 * lax.axis_index("x") + lax.axis_index("y")
        out = []
        for k in OTHER_CHIPS:
            theirs = _chip_of(_peer(k))
            for t in range(n):
                out.append(_remote(ins[t].at[theirs], outs[t].at[mine if dst_is_mine else theirs],
                                   send.at[t, k // 2 - 1], recv.at[t, k // 2 - 1], _peer(k)))
        return out

    def local_copies(ins, outs, sems):
        mine = 2 * lax.axis_index("x") + lax.axis_index("y")
        return [pltpu.make_async_copy(ins[t].at[mine], outs[t].at[mine], sems[2].at[t]) for t in range(n)]

    def start(ins, outs, sems):
        for cp in local_copies(ins, outs, sems) + copies(ins, outs, sems, True):
            cp.start()

    def finish(ins, outs, sems):
        for cp in copies(ins, outs, sems, False) + local_copies(ins, outs, sems):
            cp.wait()

    out_shapes = [jax.ShapeDtypeStruct(s.shape, s.dtype) for s in sums]
    sem_shapes = [pltpu.SemaphoreType.DMA((n, N_CHIP - 1)), pltpu.SemaphoreType.DMA((n, N_CHIP - 1)),
                  pltpu.SemaphoreType.DMA((n,))]
    return _exchange(sums, out_shapes, sem_shapes, start, finish, name=name, host=host)


class _GradReduce:
    def __init__(self, grads, wire_dtypes, hosts=None):
        pairs = [g.reshape((N_CHIP, 2) + g.shape[1:]) for g in grads]
        self.n, self.riders, self.recv = len(grads), [], None

        def after_swap(partner):
            sums = [_pair_add(g, p, dt, name="grad_pair_add") for g, p, dt in zip(pairs, partner, wire_dtypes)]
            if hosts is None:
                self.recv = _chip_exchange(sums, name="grad_chip_exchange")
            else:
                self.riders = [(idx, _chip_exchange([sums[i] for i in idx], name="grad_chip_exchange", host=h))
                               for h, idx in hosts[1]]

        if hosts is None:
            after_swap(_sibling_swap(pairs, name="grad_sibling_swap"))
        else:
            _sibling_swap(pairs, name="grad_sibling_swap", host=hosts[0], then=after_swap)

    def result(self):
        if self.recv is None:
            self.recv = [None] * self.n
            for idx, rider in self.riders:
                assert rider.results is not None, rider.host
                for i, r in zip(idx, rider.results):
                    self.recv[i] = r
        return self.recv


def _adamw(recv, w, m, v, layer=None, *, name):
    n_slots, R, C = recv.shape
    tr = _pick(R, (256, 128, 64, 32, 16, 8))

    def body(r_ref, w_ref, m_ref, v_ref, g_ref, d_ref, nm_ref, nv_ref):
        g = r_ref[0].astype(F32)
        for s in range(1, n_slots):
            g = g + r_ref[s].astype(F32)
        m_new = ADAM_B1 * m_ref[...] + (1.0 - ADAM_B1) * g
        v_new = ADAM_B2 * v_ref[...] + (1.0 - ADAM_B2) * jnp.square(g)
        m_hat = m_new / (1.0 - ADAM_B1 ** ADAM_STEP)
        v_hat = v_new / (1.0 - ADAM_B2 ** ADAM_STEP)
        g_ref[...] = g
        d_ref[...] = -ADAM_LR * (m_hat / (jnp.sqrt(v_hat) + ADAM_EPS) + ADAM_WD * w_ref[...])
        nm_ref[...] = m_new
        nv_ref[...] = v_new

    row = pl.BlockSpec((tr, C), lambda i: (i, 0))
    state = row if layer is None else pl.BlockSpec((None, tr, C), lambda i: (layer, i, 0))
    return _pcall(
        body, name=name, grid=(R // tr,),
        in_specs=[pl.BlockSpec((n_slots, tr, C), lambda i: (0, i, 0)), state, state, state],
        out_specs=[row] * 4, out_shape=[jax.ShapeDtypeStruct((R, C), F32)] * 4,
        compiler_params=_params(("parallel",)),
    )(recv, w, m, v)


def _flat_pad(parts, rows):
    flat = jnp.concatenate([p.reshape(-1) for p in parts])
    return jnp.pad(flat, (0, rows * LANES - flat.shape[0])).reshape(rows, LANES)


def _small_shard_shape(n):
    return SMALL[n][:-1] + (SMALL[n][-1] // N_DEV,)


def _unpack_small(gathered):
    out, off = {}, 0
    flat = gathered.reshape(N_DEV, -1)
    for n in SMALL:
        r, c = _small_shard_shape(n)
        out[n] = flat[:, off:off + r * c].reshape(N_DEV, r, c).transpose(1, 0, 2).reshape(r, N_DEV * c)
        off += r * c
    return out


def _pack_state(state, prefix):
    flat = jnp.concatenate([state[prefix + n].reshape(DEPTH, -1) for n in (*SMALL, *REPL)], axis=1)
    return jnp.pad(flat, ((0, 0), (0, PACK_ROWS * LANES - flat.shape[1]))).reshape(DEPTH * PACK_ROWS, LANES)


def _pack_small_grads(grads):
    cols = []
    for n in SMALL:
        r, c = _small_shard_shape(n)
        cols.append(grads[n].reshape(r, N_DEV, c).transpose(1, 0, 2).reshape(N_DEV, r * c))
    cols += [jnp.broadcast_to(grads[n].reshape(1, -1), (N_DEV, grads[n].size)) for n in REPL]
    flat = jnp.concatenate(cols, axis=1)
    return jnp.pad(flat, ((0, 0), (0, PACK_ROWS * LANES - flat.shape[1]))).reshape(N_DEV, PACK_ROWS, LANES)


def _unpack_state(packed):
    out, off = {}, 0
    flat = packed.reshape(DEPTH, -1)
    for n, shape in [(n, _small_shard_shape(n)) for n in SMALL] + list(REPL.items()):
        sz = math.prod(shape)
        out[n] = flat[:, off:off + sz].reshape((DEPTH,) + shape)
        off += sz
    return out


def _pad_b_in(b):
    parts, pos = [], 0
    for src, width, dst in Z_PIECES:
        parts += [jnp.zeros((b.shape[0], dst - pos), b.dtype), b[:, src:src + width]]
        pos = dst + width
    return jnp.concatenate(parts + [jnp.zeros((b.shape[0], Z_W - pos), b.dtype)], axis=1)


def _unpad_b_in(bp):
    return jnp.concatenate([bp[:, dst:dst + width] for _, width, dst in Z_PIECES], axis=1)


def _heads(t, n_heads):
    S = t.shape[0]
    return t.astype(BF16).reshape(S, n_heads, -1).transpose(1, 0, 2)


def _unheads(t):
    H, S, D = t.shape
    return t.transpose(1, 0, 2).reshape(S, H * D)


def _up_shard_pad(t):
    lead = t.shape[:-1]
    t = jnp.pad(t.reshape(lead + (2, FF_HALF)), [(0, 0)] * len(lead) + [(0, 0), (0, FF_HALF_P - FF_HALF)])
    return t.reshape(lead + (2 * FF_HALF_P,))


def _up_shard_unpad(t):
    lead = t.shape[:-1]
    return t.reshape(lead + (2, FF_HALF_P))[..., :FF_HALF].reshape(lead + (2 * FF_HALF,))


def _pad_shard(n, t):
    if n == "w_in":
        return jnp.pad(t, ((0, 0), (0, W_IN_SLOT - W_IN_SHARD)))
    if n == "ffn_w_up":
        return _up_shard_pad(t)
    if n == "ffn_w_down":
        return jnp.pad(t, ((0, FF_HALF_P - FF_HALF), (0, 0)))
    return t


def _unpad_shard(n, t):
    if n == "w_in":
        return t[:, :W_IN_SHARD]
    if n == "ffn_w_up":
        return _up_shard_unpad(t)
    if n == "ffn_w_down":
        return t[:FF_HALF]
    return t


GATHER_HOSTS = (("fox_fwd", (0,)), ("chunk_fwd", (8, 9)), ("mm_up", (1, 2, 3, 4, 5, 6, 7, 10)))
REDUCE_HOSTS = ("mm_up_dw", (("fox_bwd", (0,)), ("chunk_bwd", (8, 9)), ("s5_scan_bwd", (1, 2, 3, 4, 5, 6, 7, 10))))


def _layer_shards(state, layer):
    shards = [_pad_shard(n, state[n][layer].astype(BF16)) for n in BIG]
    return shards + [_flat_pad([state[n][layer] for n in SMALL], SMALL_ROWS)]


def _gathered_weights(gathered):
    *big, small = gathered
    W = dict(zip(BIG, big))
    for n in ("w_o", "xa_wq", "xa_wo", "ffn_w_down"):
        W[n] = W[n].reshape(-1, D_MODEL)
    W["w_in_p"] = _colmap(W.pop("w_in"), inverse=False, name="w_in_colmap")
    W.update(_unpack_small(small))
    return W


def _ssm_params(p):
    prep1_in = (p["ssm_lambda_re"], p["ssm_lambda_im"], p["ssm_log_dt"][:, None])
    ar, ai, gr, gi = _whole(_ssm_prep1, *prep1_in, name="ssm_prep1")
    to_cn = lambda b: b.transpose(2, 0, 1).reshape(SSM_GROUP, N_SSM_CH)
    prep2_in = (gr.reshape(1, N_SSM_CH), gi.reshape(1, N_SSM_CH), to_cn(p["ssm_b_re"]), to_cn(p["ssm_b_im"]))
    bbr, bbi = _whole(_ssm_prep2, *prep2_in, name="ssm_prep2")
    to_gcp = lambda t: t.reshape(SSM_GROUP, SSM_GROUPS, SSM_STATE).transpose(1, 0, 2)
    bb = _blockdiag(to_gcp(bbr), to_gcp(bbi))
    cct = _blockdiag(p["ssm_c_re"], -p["ssm_c_im"])
    a_vec = _to_blocked(ar.reshape(1, N_SSM_CH), ai.reshape(1, N_SSM_CH))
    return dict(bb=bb, cct=cct, a_vec=a_vec, prep1_in=prep1_in, prep2_in=prep2_in)


def _layer_fwd(x, mem, p, W):
    sp = _ssm_params(p)
    z = _mm(x, W["w_in_p"], bias=p["b_in_p"], name="mm_in")
    zh = lambda off: _heads(z[:, off:off + 512], 8)
    f_t = z[:, Z_FF:Z_FF + FOX_HEADS].T
    cum = _cum_heads(f_t, name="fox_cum")
    ya_h, lse_a = _attn_fwd("fox", zh(Z_FQ), zh(Z_FK), zh(Z_FV), (cum[:, :, None], cum[:, None, :]), name="fox_fwd")
    ya_pre = _unheads(ya_h)
    ya = _mm(ya_pre, W["w_fox_o"], b_slots=True, name="mm_fox_o")
    x_ri = _mm(z, sp["bb"], a_off=Z_SU, a_cols=SSM_WIDTH, name="mm_s5_in")
    h_ri = _scan(x_ri, sp["a_vec"], name="s5_scan")
    hc = _mm(h_ri, sp["cct"], tb=True, name="mm_s5_out")
    d_row = p["ssm_d"][None, :]
    (gel,) = _rowwise(lambda a, b, c: (_s5_tail(a, b, c),), [hc, Win(z, 512, Z_SU // 512)], [d_row], name="s5_tail")
    yb2 = _mm(gel, W["w_ssm_glu"], b_slots=True, name="mm_glu")
    bias = _relbias_expand(W["ca_rel_bias"]).transpose(1, 0, 2)
    padk = lambda t: jnp.pad(t, ((0, 0), (CA_PAD, 0), (0, 0)))
    yc_h, lse_c = _attn_fwd("chunk", zh(Z_CQ), padk(zh(Z_CK)), padk(zh(Z_CV)), (bias,), name="chunk_fwd")
    yc_pre = _unheads(yc_h)
    yc = _mm(yc_pre, W["w_ca_o"], b_slots=True, name="mm_ca_o")
    gates = [Win(z, 1024, Z_GF // 1024), Win(z, 1024, Z_GS // 1024), Win(z, 1024, Z_GC // 1024)]
    (merged,) = _rowwise(lambda *a: (_merge(*a),), gates + [ya, yb2, yc], name="merge")
    h1 = _mm(merged, W["w_o"], name="mm_o")
    ln_g, ln_b = W["ln_g"], W["ln_b"]
    r1, x1 = _rowwise(_ln_fwd, [x, h1], [ln_g[0:1], ln_b[0:1]], name="ln_fwd")
    q = _mm(x1, W["xa_wq"], name="mm_xq")
    kv = _mm(mem, W["xa_wkv"], b_slots=True, name="mm_xkv")
    qh, kh, vh = _heads(q, XA_HEADS), _heads(kv[:, :D_MODEL], XA_HEADS), _heads(kv[:, D_MODEL:], XA_HEADS)
    o_h, lse_x = _attn_fwd("xa", qh, kh, vh, name="xa_fwd")
    o = _unheads(o_h)
    h2 = _mm(o, W["xa_wo"], name="mm_xo")
    r2, x2 = _rowwise(_ln_fwd, [x1, h2], [ln_g[1:2], ln_b[1:2]], name="ln_fwd")
    up = _mm(x2, W["ffn_w_up"], b_slots=True, name="mm_up")
    hmid = _ffn_mid(up, _ff_pad(W["ffn_conv_w"]), _ff_pad(p["ffn_conv_b"][None, :]), name="ffn_mid")
    h3 = _mm(hmid, W["ffn_w_down"], name="mm_down")
    r3, x3 = _rowwise(_ln_fwd, [x2, h3], [ln_g[2:3], ln_b[2:3]], name="ln_fwd")
    res = dict(x=x, z=z, cum=cum, lse_a=lse_a, ya_pre=ya_pre, ya=ya, h_ri=h_ri, hc=hc, gel=gel, yb2=yb2, lse_c=lse_c,
               yc_pre=yc_pre, yc=yc, merged=merged, r1=r1, x1=x1, q=q, kv=kv, o=o, lse_x=lse_x, r2=r2, x2=x2, up=up,
               hmid=hmid, r3=r3, bias=bias, sp=sp, W=W)
    return x3, res


def _layer_bwd(dx3, mem, p, res):
    W = res["W"]
    x, z = res["x"], res["z"]
    sp = res["sp"]
    ln_g = W["ln_g"]
    big, small = {}, {}
    slots = lambda t: t.reshape(N_DEV, -1, D_MODEL)
    dr3, dg2, db2 = _rowwise(_ln_bwd, [res["r3"], dx3], [ln_g[2:3]], n_red=2, name="ln_bwd")
    dhmid = _mm(dr3, W["ffn_w_down"], tb=True, name="mm_down_dx")
    big["ffn_w_down"] = slots(_mm(res["hmid"], dr3, ta=True, name="mm_down_dw"))
    conv_w_p, conv_b_p = _ff_pad(W["ffn_conv_w"]), _ff_pad(p["ffn_conv_b"][None, :])
    dup_a, dup_g, dcw, dcb = _ffn_mid(res["up"], conv_w_p, conv_b_p, dhmid, name="ffn_mid_bwd")
    dup = jnp.concatenate([dup_a, dup_g], axis=1)
    small["ffn_conv_w"], small["ffn_conv_b"] = _ff_unpad(dcw), _ff_unpad(dcb)[0]
    dx2 = _mm(dup, W["ffn_w_up"], tb=True, b_slots=True, add=(dr3, DN_ALPHA), name="mm_up_dx")
    big["ffn_w_up"] = _mm(res["x2"], dup, ta=True, out_slots=2 * FF_HALF_P, name="mm_up_dw")
    dr2, dg1, db1 = _rowwise(_ln_bwd, [res["r2"], dx2], [ln_g[1:2]], n_red=2, name="ln_bwd")
    do = _mm(dr2, W["xa_wo"], tb=True, name="mm_xo_dx")
    big["xa_wo"] = slots(_mm(res["o"], dr2, ta=True, name="mm_xo_dw"))
    kv = res["kv"]
    qh, kh, vh = _heads(res["q"], XA_HEADS), _heads(kv[:, :D_MODEL], XA_HEADS), _heads(kv[:, D_MODEL:], XA_HEADS)
    dqh, dkh, dvh = _attn_bwd("xa", qh, kh, vh, res["lse_x"], _heads(do, XA_HEADS), name="xa_bwd")
    dq = _unheads(dqh)
    dkv = jnp.concatenate([_unheads(dkh), _unheads(dvh)], axis=1)
    dx1 = _mm(dq, W["xa_wq"], tb=True, add=(dr2, DN_ALPHA), name="mm_xq_dx")
    big["xa_wq"] = slots(_mm(res["x1"], dq, ta=True, name="mm_xq_dw"))
    big["xa_wkv"] = _mm(mem, dkv, ta=True, out_slots=256, name="mm_xkv_dw")
    dr1, dg0, db0 = _rowwise(_ln_bwd, [res["r1"], dx1], [ln_g[0:1]], n_red=2, name="ln_bwd")
    small["ln_g"] = jnp.concatenate([dg0, dg1, dg2], axis=0)
    small["ln_b"] = jnp.concatenate([db0, db1, db2], axis=0)
    dmerged = _mm(dr1, W["w_o"], tb=True, name="mm_o_dx")
    big["w_o"] = slots(_mm(res["merged"], dr1, ta=True, name="mm_o_dw"))
    gates = [Win(z, 1024, Z_GF // 1024), Win(z, 1024, Z_GS // 1024), Win(z, 1024, Z_GC // 1024)]
    dgf, dgs, dgc, dya, dyb2, dyc = _rowwise(_vjp_of(_merge, 6), gates + [res["ya"], res["yb2"], res["yc"], dmerged],
                                             name="merge_bwd")
    zh = lambda off: _heads(z[:, off:off + 512], 8)
    dya_pre = _mm(dya, W["w_fox_o"], tb=True, b_slots=True, name="mm_fox_o_dx")
    big["w_fox_o"] = _mm(res["ya_pre"], dya, ta=True, out_slots=128, name="mm_fox_o_dw")
    cum = res["cum"]
    dfq, dfk, dfv, dcq, dck = _attn_bwd("fox", zh(Z_FQ), zh(Z_FK), zh(Z_FV), res["lse_a"], _heads(dya_pre, 8),
                                        (cum[:, :, None], cum[:, None, :]), name="fox_bwd")
    f_t = z[:, Z_FF:Z_FF + FOX_HEADS].T
    dff = _cum_heads(f_t, dcq[:, :, 0] + dck[:, 0, :], name="fox_cum_bwd")
    dgel = _mm(dyb2, W["w_ssm_glu"], tb=True, b_slots=True, name="mm_glu_dx")
    big["w_ssm_glu"] = _mm(res["gel"], dyb2, ta=True, out_slots=256, name="mm_glu_dw")
    d_row = p["ssm_d"][None, :]
    su_win = Win(z, 512, Z_SU // 512)
    dy, dsu1, dd = _rowwise(_s5_tail_bwd, [res["hc"], su_win, dgel], [d_row], n_red=1, name="s5_tail_bwd")
    small["ssm_d"] = dd[0]
    dh_ri = _mm(dy, sp["cct"], name="mm_s5_out_dx")
    dcct = _mm(dy, res["h_ri"], ta=True, name="mm_s5_out_dw")
    dx_ri, da_vec = _scan(dh_ri, sp["a_vec"], res["h_ri"], name="s5_scan_bwd")
    dsu = _mm(dx_ri, sp["bb"], tb=True, add=(dsu1, 1.0), name="mm_s5_in_dx")
    dbb = _mm(z, dx_ri, ta=True, a_off=Z_SU, a_cols=SSM_WIDTH, name="mm_s5_in_dw")
    dcr, dci = _blockdiag_inv(dcct)
    small["ssm_c_re"], small["ssm_c_im"] = dcr, -dci
    dbbr, dbbi = _blockdiag_inv(dbb)
    to_cn = lambda t: t.transpose(1, 0, 2).reshape(SSM_GROUP, N_SSM_CH)
    dgr, dgi, dbr, dbi = _whole(_vjp_of(_ssm_prep2, 4), *sp["prep2_in"], to_cn(dbbr), to_cn(dbbi), name="ssm_prep2_bwd")
    from_cn = lambda t: t.reshape(SSM_GROUP, SSM_GROUPS, SSM_STATE).transpose(1, 2, 0)
    small["ssm_b_re"], small["ssm_b_im"] = from_cn(dbr), from_cn(dbi)
    dar, dai = _from_blocked(da_vec)
    sq = lambda t: t.reshape(SSM_GROUPS, SSM_STATE)
    dlr, dli, dldt = _whole(_vjp_of(_ssm_prep1, 3), *sp["prep1_in"], sq(dar), sq(dai), sq(dgr), sq(dgi),
                            name="ssm_prep1_bwd")
    small["ssm_lambda_re"], small["ssm_lambda_im"], small["ssm_log_dt"] = dlr, dli, dldt[:, 0]
    dyc_pre = _mm(dyc, W["w_ca_o"], tb=True, b_slots=True, name="mm_ca_o_dx")
    big["w_ca_o"] = _mm(res["yc_pre"], dyc, ta=True, out_slots=128, name="mm_ca_o_dw")
    bias = res["bias"]
    padk = lambda t: jnp.pad(t, ((0, 0), (CA_PAD, 0), (0, 0)))
    dcqh, dckh, dcvh, dbias = _attn_bwd("chunk", zh(Z_CQ), padk(zh(Z_CK)), padk(zh(Z_CV)), res["lse_c"],
                                        _heads(dyc_pre, 8), (bias,), name="chunk_bwd")
    small["ca_rel_bias"] = _relbias_reduce(dbias.transpose(1, 0, 2))
    dff_p = jnp.pad(dff.T, ((0, 0), (0, 512 - FOX_HEADS)))
    dz = jnp.concatenate([_unheads(dfq), _unheads(dfk), _unheads(dfv), dff_p, dsu, _unheads(dcqh),
                          _unheads(dckh[:, CA_PAD:]), _unheads(dcvh[:, CA_PAD:]), dgf, dgs, dgc], axis=1)
    dx = _mm(dz, W["w_in_p"], tb=True, add=(dr1, DN_ALPHA), name="mm_in_dx")
    big["w_in"] = _colmap(_mm(x, dz, ta=True, name="mm_in_dw"), inverse=True, name="w_in_colmap_inv")
    (db_in_p,) = _rowwise(lambda t: (jnp.sum(t, axis=0, keepdims=True),), [dz], n_red=1, name="colsum")
    small["b_in"] = _unpad_b_in(db_in_p)[0]
    return dx, big, small


def kernel(x, mem, w_in, b_in, ssm_lambda_re, ssm_lambda_im, ssm_log_dt, ssm_b_re, ssm_b_im, ssm_c_re, ssm_c_im, ssm_d, ca_rel_bias, w_fox_o, w_ssm_glu, w_ca_o, w_o, xa_wq, xa_wkv, xa_wo, ffn_w_up, ffn_conv_w, ffn_conv_b, ffn_w_down, ln_g, ln_b, loss_target, m_w_in, m_b_in, m_ssm_lambda_re, m_ssm_lambda_im, m_ssm_log_dt, m_ssm_b_re, m_ssm_b_im, m_ssm_c_re, m_ssm_c_im, m_ssm_d, m_ca_rel_bias, m_w_fox_o, m_w_ssm_glu, m_w_ca_o, m_w_o, m_xa_wq, m_xa_wkv, m_xa_wo, m_ffn_w_up, m_ffn_conv_w, m_ffn_conv_b, m_ffn_w_down, m_ln_g, m_ln_b, v_w_in, v_b_in, v_ssm_lambda_re, v_ssm_lambda_im, v_ssm_log_dt, v_ssm_b_re, v_ssm_b_im, v_ssm_c_re, v_ssm_c_im, v_ssm_d, v_ca_rel_bias, v_w_fox_o, v_w_ssm_glu, v_w_ca_o, v_w_o, v_xa_wq, v_xa_wkv, v_xa_wo, v_ffn_w_up, v_ffn_conv_w, v_ffn_conv_b, v_ffn_w_down, v_ln_g, v_ln_b):
    given = dict(locals())
    state = {pre + n: given[pre + n] for n in WEIGHTS for pre in ("", "m_", "v_")}
    mem0 = mem[0]
    b_in_p = _pad_b_in(b_in)
    layer_params = [{**{n: state[n][l] for n in REPL}, "b_in_p": b_in_p[l:l + 1]} for l in range(DEPTH)]

    _RIDERS.clear()
    h, residuals = x[0], []
    gathered = _all_gather(_layer_shards(state, 0), name="all_gather_weights")
    for l in range(DEPTH):
        riders = []
        if l + 1 < DEPTH:
            shards = _layer_shards(state, l + 1)
            riders = [(idx, _all_gather([shards[i] for i in idx], name="all_gather_weights", host=host))
                      for host, idx in GATHER_HOSTS]
        h, res = _layer_fwd(h, mem0, layer_params[l], _gathered_weights(gathered))
        residuals.append(res)
        gathered = [None] * (len(BIG) + 1)
        for idx, rider in riders:
            for i, r in zip(idx, rider.results):
                gathered[i] = r
    dh, loss_cols = _rowwise(_loss_rows, [h, loss_target[0]], n_red=1, name="loss")
    loss = lax.psum(jnp.sum(loss_cols), ("x", "y", "c"))

    outs = [None] * DEPTH
    wire = [BF16] * len(BIG) + [F32]
    pending = None
    for l in reversed(range(-1, DEPTH)):
        if l >= 0:
            dh, big, small = _layer_bwd(dh, mem0, layer_params[l], residuals[l])
            reduce = _GradReduce([big[n] for n in BIG] + [_pack_small_grads(small)], wire,
                                 hosts=REDUCE_HOSTS if l > 0 else None)
        done, pending = pending, (l, reduce) if l >= 0 else None
        if done is None:
            continue
        l_done, reduce_done = done
        *recv_big, recv_small = reduce_done.result()
        layer_out = {}
        for n, recv in zip(BIG, recv_big):
            if recv.shape[1:] == BIG[n][0]:
                res4 = _adamw(recv, state[n], state["m_" + n], state["v_" + n], l_done, name="adamw_" + n)
            else:
                padded = [_pad_shard(n, state[pre + n][l_done]) for pre in ("", "m_", "v_")]
                res4 = [_unpad_shard(n, t) for t in _adamw(recv, *padded, name="adamw_" + n)]
            layer_out[n] = res4
        layer_out["recv_small"] = recv_small
        outs[l_done] = layer_out

    assert not _RIDERS, [r.host for r in _RIDERS]
    recv_small = jnp.concatenate([outs[l]["recv_small"] for l in range(DEPTH)], axis=1)
    packed = _adamw(recv_small, *[_pack_state(state, pre) for pre in ("", "m_", "v_")], name="adamw_small")
    small_out = [_unpack_state(t) for t in packed]
    stacked = lambda n, j: small_out[j][n] if n not in BIG else jnp.stack([outs[l][n][j] for l in range(DEPTH)])
    return (loss, dh[None], *[stacked(n, j) for j in range(4) for n in WEIGHTS])
```

```python
import functools
import math

import jax
import jax.numpy as jnp
from jax import lax
from jax.experimental import pallas as pl
from jax.experimental.pallas import tpu as pltpu

F32, BF16 = jnp.float32, jnp.bfloat16

D_MODEL = 1024
DEPTH = 4
CHUNK = 64
FOX_HEADS, FOX_HEAD_DIM, FOX_WIDTH = 8, 64, 512
SSM_GROUP, SSM_WIDTH, SSM_GROUPS, SSM_STATE = 16, 512, 32, 64
CA_HEADS, CA_HEAD_DIM, CA_WIDTH, CA_LEFT_CHUNKS = 8, 64, 512, 8
CA_BAND = (CA_LEFT_CHUNKS + 1) * CHUNK
CA_PAD = CA_LEFT_CHUNKS * CHUNK
REL_MIN, REL_MAX = -(CHUNK - 1), 4 * CHUNK
N_REL = REL_MAX - REL_MIN + 1
XA_HEADS, XA_HEAD_DIM = 4, 256
D_FF = 2816
DN_ALPHA = (2 * DEPTH) ** 0.25
LN_EPS = 1e-5
NEG_INF = -1e30
ADAM_LR, ADAM_B1, ADAM_B2, ADAM_EPS, ADAM_WD, ADAM_STEP = 0.001, 0.9, 0.999, 1e-08, 0.01, 10

N_DEV = 8
LANE = 128
N_SSM_CH = SSM_GROUPS * SSM_STATE
SCAN_CB = 256
N_IN = 6664
W_IN_SHARD, W_IN_SLOT = N_IN // N_DEV, 896
Z_W = 7168
Z_FQ, Z_FK, Z_FV, Z_FF, Z_SU, Z_CQ, Z_CK, Z_CV, Z_GF, Z_GS, Z_GC = (
    0, 512, 1024, 1536, 2048, 2560, 3072, 3584, 4096, 5120, 6144)
Z_PIECES = ((0, 512, Z_FQ), (512, 512, Z_FK), (1024, 512, Z_FV), (1536, 8, Z_FF), (1544, 512, Z_SU),
            (2056, 512, Z_CQ), (2568, 512, Z_CK), (3080, 512, Z_CV), (3592, 1024, Z_GF), (4616, 1024, Z_GS),
            (5640, 1024, Z_GC))
FF_HALF, FF_HALF_P = D_FF // N_DEV, 384
D_FF_P = N_DEV * FF_HALF_P

VMEM_LIMIT_BYTES = 56 * 1024 * 1024

BIG = {"w_in": ((1024, W_IN_SHARD), 1), "w_fox_o": ((512, 128), 1), "w_ssm_glu": ((512, 256), 1),
       "w_ca_o": ((512, 128), 1), "w_o": ((128, 1024), 0), "xa_wq": ((128, 1024), 0), "xa_wkv": ((1024, 256), 1),
       "xa_wo": ((128, 1024), 0), "ffn_w_up": ((1024, 2 * FF_HALF), 1), "ffn_w_down": ((FF_HALF, 1024), 0)}
SMALL = {"ca_rel_bias": (8, 320), "ffn_conv_w": (3, D_FF), "ln_g": (3, 1024), "ln_b": (3, 1024)}
REPL = {"b_in": (N_IN,), "ssm_lambda_re": (32, 64), "ssm_lambda_im": (32, 64), "ssm_log_dt": (32,),
        "ssm_b_re": (32, 64, 16), "ssm_b_im": (32, 64, 16), "ssm_c_re": (32, 16, 64), "ssm_c_im": (32, 16, 64),
        "ssm_d": (512,), "ffn_conv_b": (D_FF,)}
WEIGHTS = ("w_in", "b_in", "ssm_lambda_re", "ssm_lambda_im", "ssm_log_dt", "ssm_b_re", "ssm_b_im", "ssm_c_re",
           "ssm_c_im", "ssm_d", "ca_rel_bias", "w_fox_o", "w_ssm_glu", "w_ca_o", "w_o", "xa_wq", "xa_wkv", "xa_wo",
           "ffn_w_up", "ffn_conv_w", "ffn_conv_b", "ffn_w_down", "ln_g", "ln_b")
LANES = 1024
PACK_ROWS = 256
SMALL_ROWS = 8


def _w_in_segments():
    segs = []
    for src, width, dst in Z_PIECES:
        n = src
        while n < src + width:
            d = n // W_IN_SHARD
            end = min(src + width, (d + 1) * W_IN_SHARD)
            segs.append((W_IN_SLOT * d + n - W_IN_SHARD * d, dst + n - src, end - n))
            n = end
    return tuple(segs)


W_IN_SEGS = _w_in_segments()


def _pallas(body, **kw):
    return pl.pallas_call(body, **kw)


def _params(sem):
    return pltpu.CompilerParams(dimension_semantics=sem, vmem_limit_bytes=VMEM_LIMIT_BYTES)


class _Rider:
    def __init__(self, host, ins, out_shapes, sem_shapes, start, finish, then=None):
        self.host, self.ins, self.out_shapes, self.sem_shapes = host, list(ins), list(out_shapes), list(sem_shapes)
        self.start, self.finish, self.then, self.results = start, finish, then, None


_RIDERS = []


def _pcall(body, *, name, **kw):
    rider = next((r for r in _RIDERS if r.host == name), None)
    if rider is None:
        return _pallas(body, name=name, **kw)
    _RIDERS.remove(rider)
    grid, in_specs, scratch = kw["grid"], list(kw["in_specs"]), list(kw.get("scratch_shapes", ()))
    single = not isinstance(kw["out_shape"], (list, tuple))
    out_specs = [kw["out_specs"]] if single else list(kw["out_specs"])
    out_shape = [kw["out_shape"]] if single else list(kw["out_shape"])
    n_in, n_out, n_scr = len(in_specs), len(out_specs), len(scratch)
    r_in, r_out = len(rider.ins), len(rider.out_shapes)

    def fused(*refs):
        a, ra = refs[:n_in], refs[n_in:n_in + r_in]
        o, ro = refs[n_in + r_in:n_in + r_in + n_out], refs[n_in + r_in + n_out:n_in + r_in + n_out + r_out]
        scr, sems = refs[n_in + r_in + n_out + r_out:][:n_scr], refs[n_in + r_in + n_out + r_out + n_scr:]
        ids = [pl.program_id(d) for d in range(len(grid))]
        first = functools.reduce(jnp.logical_and, [i == 0 for i in ids])
        last = functools.reduce(jnp.logical_and, [i == g - 1 for i, g in zip(ids, grid)])

        @pl.when(first)
        def _():
            rider.start(ra, ro, sems)

        body(*a, *o, *scr)

        @pl.when(last)
        def _():
            rider.finish(ra, ro, sems)

    hbm = pl.BlockSpec(memory_space=pl.ANY)
    call = _pallas(fused, name=name, grid=grid, in_specs=in_specs + [hbm] * r_in, out_specs=out_specs + [hbm] * r_out,
                   out_shape=out_shape + rider.out_shapes, scratch_shapes=scratch + rider.sem_shapes,
                   compiler_params=_params(("arbitrary",) * len(grid)))

    def run(*operands):
        outs = call(*operands, *rider.ins)
        rider.results = list(outs[n_out:])
        if rider.then is not None:
            rider.then(rider.results)
        return outs[0] if single else list(outs[:n_out])

    return run


def _pick(dim, prefs):
    for p in prefs:
        if dim % p == 0:
            return p
    return dim


def _mm(a, b, *, ta=False, tb=False, bias=None, add=None, a_off=0, a_cols=None, b_slots=False, out_slots=None,
        name, out_dtype=F32):
    a_cols = a_cols if a_cols is not None else a.shape[1]
    M, K = (a_cols, a.shape[0]) if ta else (a.shape[0], a_cols)
    tm = _pick(M, (1024, 512, 256, 128))
    if ta:
        tk = _pick(K, (2048, 1024, 512, 256))
    elif b_slots and tb:
        tk = _pick(b.shape[2], (1024, 768, 512, 256, 128))
    else:
        tk = K if K <= 3072 else _pick(K, (1024, 512, 256, 128))
    nk = K // tk
    if b_slots:
        ns = b.shape[2]
        if tb:
            N = b.shape[1]
            tn = _pick(N, (512, 256, 128))
            per = ns // tk
            b_spec = pl.BlockSpec((None, tn, tk), lambda i, j, k: (k // per, j, k % per))
            b_dim = 1
            assert N_DEV * ns == K
        else:
            N = N_DEV * ns
            tn = _pick(ns, (512, 256, 128))
            per = ns // tn
            b_spec = pl.BlockSpec((None, tk, tn), lambda i, j, k: (j // per, k, j % per))
            b_dim = 0
            assert b.shape[1] == K
    else:
        N = b.shape[0] if tb else b.shape[1]
        assert (b.shape[1] if tb else b.shape[0]) == K, (a.shape, b.shape, ta, tb)
        tn = _pick(N if out_slots is None else out_slots, (512, 256, 128))
        if tb:
            b_spec = pl.BlockSpec((tn, tk), lambda i, j, k: (j, k))
            b_dim = 1
        else:
            b_spec = pl.BlockSpec((tk, tn), lambda i, j, k: (k, j))
            b_dim = 0
    if ta:
        assert a_off % tm == 0
        a_spec = pl.BlockSpec((tk, tm), lambda i, j, k: (k, i + a_off // tm))
        a_dim = 0
    else:
        assert a_off % tk == 0
        a_spec = pl.BlockSpec((tm, tk), lambda i, j, k: (i, k + a_off // tk))
        a_dim = 1
    cache_at = ta and nk == 1
    dims = (((1 if cache_at else a_dim,), (b_dim,)), ((), ()))
    ins, specs = [a, b], [a_spec, b_spec]
    if bias is not None:
        ins.append(bias)
        specs.append(pl.BlockSpec((1, tn), lambda i, j, k: (0, j)))
    add_scale = None
    if add is not None:
        ins.append(add[0])
        add_scale = add[1]
        specs.append(pl.BlockSpec((tm, tn), lambda i, j, k: (i, j)))
    if out_slots is None:
        out_spec = pl.BlockSpec((tm, tn), lambda i, j, k: (i, j))
        out_shape = jax.ShapeDtypeStruct((M, N), out_dtype)
    else:
        assert N == N_DEV * out_slots
        per_o = out_slots // tn
        out_spec = pl.BlockSpec((None, tm, tn), lambda i, j, k: (j // per_o, i, j % per_o))
        out_shape = jax.ShapeDtypeStruct((N_DEV, M, out_slots), out_dtype)

    def body(*refs):
        a_ref, b_ref = refs[0], refs[1]
        pos = 2
        bias_ref = add_ref = None
        if bias is not None:
            bias_ref = refs[pos]
            pos += 1
        if add is not None:
            add_ref = refs[pos]
            pos += 1
        o_ref = refs[pos]
        acc_ref = refs[pos + 1] if nk > 1 else None
        if cache_at:
            at_ref = refs[pos + 1]

            @pl.when(pl.program_id(1) == 0)
            def _():
                step = min(tk, 256)
                for c in range(0, tk, step):
                    at_ref[:, c:c + step] = a_ref[c:c + step, :].T.astype(BF16)

            lhs = at_ref[...]
        else:
            lhs = a_ref[...].astype(BF16)
        part = lax.dot_general(lhs, b_ref[...].astype(BF16), dims, preferred_element_type=F32)

        def finish(acc):
            if bias_ref is not None:
                acc = acc + bias_ref[...]
            if add_ref is not None:
                acc = acc + add_scale * add_ref[...]
            o_ref[...] = acc.astype(out_dtype)

        if nk == 1:
            finish(part)
        else:
            k = pl.program_id(2)

            @pl.when(k == 0)
            def _():
                acc_ref[...] = part

            @pl.when(k > 0)
            def _():
                acc_ref[...] += part

            @pl.when(k == nk - 1)
            def _():
                finish(acc_ref[...])

    return _pcall(
        body, name=name, grid=(M // tm, N // tn, nk), in_specs=specs, out_specs=out_spec, out_shape=out_shape,
        scratch_shapes=[pltpu.VMEM((tm, tn), F32)] if nk > 1 else [pltpu.VMEM((tm, tk), BF16)] if cache_at else [],
        compiler_params=_params(("parallel", "arbitrary", "arbitrary")),
    )(*ins)


class Win:
    def __init__(self, arr, width, blk):
        self.arr, self.width, self.blk = arr, width, blk


def _rowwise(fn, rows, vecs=(), *, n_red=0, tr=256, name):
    wins = [r if isinstance(r, Win) else Win(r, r.shape[1], 0) for r in rows]
    S = wins[0].arr.shape[0]
    tr = min(tr, S)
    tile_args = [jax.ShapeDtypeStruct((tr, w.width), w.arr.dtype) for w in wins]
    tile_args += [jax.ShapeDtypeStruct(v.shape, v.dtype) for v in vecs]
    outs = jax.eval_shape(fn, *tile_args)
    n_row = len(outs) - n_red
    specs = [pl.BlockSpec((tr, w.width), functools.partial(lambda i, b: (i, b), b=w.blk)) for w in wins]
    specs += [pl.BlockSpec(v.shape, functools.partial(lambda i, nd: (0,) * nd, nd=v.ndim)) for v in vecs]
    out_specs = [pl.BlockSpec((tr, o.shape[1]), lambda i: (i, 0)) for o in outs[:n_row]]
    out_specs += [pl.BlockSpec(o.shape, functools.partial(lambda i, nd: (0,) * nd, nd=len(o.shape))) for o in outs[n_row:]]
    out_shape = [jax.ShapeDtypeStruct((S, o.shape[1]), o.dtype) for o in outs[:n_row]]
    out_shape += [jax.ShapeDtypeStruct(o.shape, o.dtype) for o in outs[n_row:]]
    n_in = len(wins) + len(vecs)

    def body(*refs):
        res = fn(*[r[...] for r in refs[:n_in]])
        o_refs = refs[n_in:]
        for o_ref, r in zip(o_refs[:n_row], res[:n_row]):
            o_ref[...] = r.astype(o_ref.dtype)
        i = pl.program_id(0)
        for o_ref, r in zip(o_refs[n_row:], res[n_row:]):
            @pl.when(i == 0)
            def _(o_ref=o_ref, r=r):
                o_ref[...] = r

            @pl.when(i > 0)
            def _(o_ref=o_ref, r=r):
                o_ref[...] += r

    return _pcall(
        body, name=name, grid=(S // tr,), in_specs=specs, out_specs=out_specs, out_shape=out_shape,
        compiler_params=_params(("arbitrary",)),
    )(*[w.arr for w in wins], *vecs)


def _whole(fn, *arrays, name):
    outs = jax.eval_shape(fn, *arrays)
    n_in = len(arrays)

    def body(*refs):
        res = fn(*[r[...] for r in refs[:n_in]])
        for o_ref, r in zip(refs[n_in:], res):
            o_ref[...] = r

    vm = pl.BlockSpec(memory_space=pltpu.VMEM)
    return _pcall(body, name=name, in_specs=[vm] * n_in, out_specs=[vm] * len(outs),
                  out_shape=[jax.ShapeDtypeStruct(o.shape, o.dtype) for o in outs])(*arrays)


def _split3(x):
    hi = x.astype(BF16)
    r = x - hi.astype(F32)
    mid = r.astype(BF16)
    lo = (r - mid.astype(F32)).astype(BF16)
    return hi, mid, lo


def _dot3(x, onehot, dims):
    return sum(lax.dot_general(t, onehot, dims, preferred_element_type=F32) for t in _split3(x))


_NT = (((1,), (1,)), ((), ()))
_NN = (((1,), (0,)), ((), ()))
_TN = (((0,), (0,)), ((), ()))


def _colmap(x, *, inverse, name):
    R = x.shape[0] if inverse else x.shape[1]
    tr = 256
    per = W_IN_SLOT // LANE
    n_out = N_DEV * per if inverse else Z_W // LANE
    segs = [(p, q, n) for q, p, n in W_IN_SEGS] if inverse else list(W_IN_SEGS)

    def body(x_ref, o_ref):
        ia = lax.broadcasted_iota(jnp.int32, (LANE, LANE), 0)
        ib = lax.broadcasted_iota(jnp.int32, (LANE, LANE), 1)

        def src_block(i):
            if inverse:
                return x_ref[:, i * LANE:(i + 1) * LANE]
            return x_ref[i // per, :, (i % per) * LANE:(i % per + 1) * LANE]

        for jb in range(n_out):
            acc = None
            for s0, d0, n in segs:
                lo, hi = max(d0, jb * LANE), min(d0 + n, (jb + 1) * LANE)
                if lo >= hi:
                    continue
                delta = d0 - s0
                for i in range((lo - delta) // LANE, (hi - delta - 1) // LANE + 1):
                    shift = jb * LANE - i * LANE - delta
                    sel = ((ia - ib == shift) & (ib >= lo - jb * LANE) & (ib < hi - jb * LANE)).astype(BF16)
                    blk = src_block(i)
                    part = _dot3(blk, sel, _NN) if inverse else lax.dot_general(blk, sel, _NN, preferred_element_type=F32)
                    acc = part if acc is None else acc + part
            if acc is None:
                acc = jnp.zeros((tr, LANE), F32)
            if inverse:
                o_ref[jb // per, :, (jb % per) * LANE:(jb % per + 1) * LANE] = acc
            else:
                o_ref[:, jb * LANE:(jb + 1) * LANE] = acc.astype(BF16)

    slot_spec = pl.BlockSpec((N_DEV, tr, W_IN_SLOT), lambda i: (0, i, 0))
    flat_spec = pl.BlockSpec((tr, Z_W), lambda i: (i, 0))
    if inverse:
        return _pcall(body, name=name, grid=(R // tr,), in_specs=[flat_spec], out_specs=slot_spec,
                      out_shape=jax.ShapeDtypeStruct((N_DEV, R, W_IN_SLOT), F32), compiler_params=_params(("parallel",)))(x)
    return _pcall(body, name=name, grid=(R // tr,), in_specs=[slot_spec], out_specs=flat_spec,
                  out_shape=jax.ShapeDtypeStruct((R, Z_W), BF16), compiler_params=_params(("parallel",)))(x)


def _log_sigmoid(x):
    return jnp.minimum(x, 0.0) - jnp.log(1.0 + jnp.exp(-jnp.abs(x)))


def _cum_heads(f, dcum=None, *, name):
    H, S = f.shape
    tn = min(512, S)
    rev = dcum is not None

    def body(*refs):
        j = pl.program_id(0)
        s_idx = lax.broadcasted_iota(jnp.int32, (S, tn), 0)
        t_idx = lax.broadcasted_iota(jnp.int32, (S, tn), 1) + j * tn
        if not rev:
            f_ref, o_ref = refs
            tri = (s_idx <= t_idx).astype(BF16)
            o_ref[...] = _dot3(_log_sigmoid(f_ref[...]), tri, _NN)
        else:
            fj_ref, d_ref, o_ref = refs
            tri = (s_idx >= t_idx).astype(BF16)
            o_ref[...] = _dot3(d_ref[...], tri, _NN) * jax.nn.sigmoid(-fj_ref[...])

    full = pl.BlockSpec((H, S), lambda j: (0, 0))
    blk = pl.BlockSpec((H, tn), lambda j: (0, j))
    ins, specs = ([f], [full]) if not rev else ([f, dcum], [blk, full])
    return _pcall(body, name=name, grid=(S // tn,), in_specs=specs, out_specs=blk,
                  out_shape=jax.ShapeDtypeStruct((H, S), F32), compiler_params=_params(("arbitrary",)))(*ins)


def _rel_onehot(qi, transposed):
    shape = (N_REL, CA_BAND) if transposed else (CA_BAND, N_REL)
    kk = lax.broadcasted_iota(jnp.int32, shape, 1 if transposed else 0)
    rr = lax.broadcasted_iota(jnp.int32, shape, 0 if transposed else 1)
    idx = jnp.clip(CA_PAD + qi - kk, REL_MIN, REL_MAX) - REL_MIN
    return (idx == rr).astype(BF16)


def _relbias_expand(rb):
    def body(rb_ref, o_ref):
        o_ref[0] = _dot3(rb_ref[...], _rel_onehot(pl.program_id(0), True), _NN)

    return _pcall(body, name="relbias_expand", grid=(CHUNK,),
                  in_specs=[pl.BlockSpec((CA_HEADS, N_REL), lambda q: (0, 0))],
                  out_specs=pl.BlockSpec((1, CA_HEADS, CA_BAND), lambda q: (q, 0, 0)),
                  out_shape=jax.ShapeDtypeStruct((CHUNK, CA_HEADS, CA_BAND), F32),
                  compiler_params=_params(("arbitrary",)))(rb)


def _relbias_reduce(db):
    def body(db_ref, o_ref):
        q = pl.program_id(0)
        part = _dot3(db_ref[0], _rel_onehot(q, False), _NN)

        @pl.when(q == 0)
        def _():
            o_ref[...] = part

        @pl.when(q > 0)
        def _():
            o_ref[...] += part

    return _pcall(body, name="relbias_reduce", grid=(CHUNK,),
                  in_specs=[pl.BlockSpec((1, CA_HEADS, CA_BAND), lambda q: (q, 0, 0))],
                  out_specs=pl.BlockSpec((CA_HEADS, N_REL), lambda q: (0, 0)),
                  out_shape=jax.ShapeDtypeStruct((CA_HEADS, N_REL), F32),
                  compiler_params=_params(("arbitrary",)))(db)


def _attn_cfg(mode, S):
    if mode == "fox":
        return min(256, S), FOX_HEAD_DIM ** -0.5
    if mode == "chunk":
        return CHUNK, CA_HEAD_DIM ** -0.5
    return min(512, S), XA_HEAD_DIM ** -0.5


def _scores(mode, i, tq, scale, qb, kb, extra):
    s = lax.dot_general(qb, kb, _NT, preferred_element_type=F32) * scale
    nk = kb.shape[0]
    if mode == "fox":
        cq, ck = extra
        s = s + cq - ck
        row = lax.broadcasted_iota(jnp.int32, (tq, nk), 0) + i * tq
        col = lax.broadcasted_iota(jnp.int32, (tq, nk), 1)
        s = jnp.where(row >= col, s, NEG_INF)
    elif mode == "chunk":
        (bias,) = extra
        col = lax.broadcasted_iota(jnp.int32, (tq, nk), 1) + i * CHUNK
        s = jnp.where(col >= CA_PAD, s + bias, NEG_INF)
    return s


class _AttnPlan:
    def __init__(self, mode, q, k, pp):
        self.mode, self.pp = mode, pp
        self.D, self.hpb, self.bw = (XA_HEAD_DIM, 1, XA_HEAD_DIM) if mode == "xa" else (64, 2, LANE)
        self.S, self.Sk = q[0].shape[0], k[0].shape[0]
        self.H = (XA_HEADS if mode == "xa" else FOX_HEADS)
        self.W = self.H * self.D
        self.gw = self.bw * pp
        self.hpg = self.hpb * pp
        self.tq, self.scale = _attn_cfg(mode, self.S)
        self.grid = (self.W // self.gw, self.S // self.tq)

    def rows(self, win):
        off = win[1] // self.gw
        return pl.BlockSpec((self.tq, self.gw), lambda g, i: (i, off + g))

    def cols(self, win):
        off = win[1] // self.gw
        return pl.BlockSpec((win[0].shape[0], self.gw), lambda g, i: (0, off + g))

    def extras(self):
        if self.mode == "fox":
            return [pl.BlockSpec((self.hpg, self.tq, 1), lambda g, i: (g, i, 0)),
                    pl.BlockSpec((self.hpg, 1, self.Sk), lambda g, i: (g, 0, 0))]
        if self.mode == "chunk":
            return [pl.BlockSpec((self.hpg, CHUNK, CA_BAND), lambda g, i: (g, 0, 0))]
        return []

    def per_row(self):
        return pl.BlockSpec((self.hpg, self.tq, 1), lambda g, i: (g, i, 0))

    def lanes(self, p):
        return slice(p * self.bw, (p + 1) * self.bw)

    def keys(self, i, ref, p):
        if self.mode == "chunk":
            return ref[pl.ds(pl.multiple_of(i * CHUNK, CHUNK), CA_BAND), self.lanes(p)]
        return ref[:, self.lanes(p)]

    def head(self, x, hh):
        if self.hpb == 1:
            return x
        lane = lax.broadcasted_iota(jnp.int32, x.shape, 1)
        return jnp.where(lane // self.D == hh, x, jnp.zeros_like(x))


def _attn_fwd(mode, q, k, v, extra=(), *, name):
    pl_ = _AttnPlan(mode, q, k, pp=4)
    n_ex = len(extra)
    out = (None, 0)

    def body(*refs):
        q_ref, k_ref, v_ref = refs[:3]
        o_ref, lse_ref = refs[3 + n_ex:]
        i = pl.program_id(1)
        for p in range(pl_.pp):
            qp = q_ref[:, pl_.lanes(p)].astype(BF16)
            kp = pl_.keys(i, k_ref, p).astype(BF16)
            vp = pl_.keys(i, v_ref, p).astype(BF16)
            acc = None
            for hh in range(pl_.hpb):
                h = p * pl_.hpb + hh
                ex = [r[h] for r in refs[3:3 + n_ex]]
                s = _scores(mode, i, pl_.tq, pl_.scale, pl_.head(qp, hh), kp, ex)
                m = jnp.max(s, axis=1, keepdims=True)
                e = jnp.exp(s - m)
                l = jnp.sum(e, axis=1, keepdims=True)
                part = lax.dot_general((e / l).astype(BF16), pl_.head(vp, hh), _NN, preferred_element_type=F32)
                acc = part if acc is None else acc + part
                lse_ref[h] = m + jnp.log(l)
            o_ref[:, pl_.lanes(p)] = acc

    return _pcall(
        body, name=name, grid=pl_.grid, in_specs=[pl_.rows(q), pl_.cols(k), pl_.cols(v)] + pl_.extras(),
        out_specs=[pl_.rows(out), pl_.per_row()],
        out_shape=[jax.ShapeDtypeStruct((pl_.S, pl_.W), F32), jax.ShapeDtypeStruct((pl_.H, pl_.S, 1), F32)],
        compiler_params=_params(("parallel", "arbitrary")),
    )(q[0], k[0], v[0], *extra)


def _attn_bwd(mode, q, k, v, lse, do, extra=(), *, name):
    pl_ = _AttnPlan(mode, q, k, pp=2 if mode == "fox" else 4)
    H, S, Sk, W = pl_.H, pl_.S, pl_.Sk, pl_.W
    n_ex = len(extra)
    out = (None, 0)
    kv_out = pl.BlockSpec((Sk, pl_.gw), lambda g, i: (0, g))
    ex_specs = pl_.extras()
    out_specs = [pl_.rows(out), kv_out, kv_out]
    out_shape = [jax.ShapeDtypeStruct((S, W), F32), jax.ShapeDtypeStruct((Sk, W), F32),
                 jax.ShapeDtypeStruct((Sk, W), F32)]
    if mode == "fox":
        out_specs += [pl_.per_row(), ex_specs[1]]
        out_shape += [jax.ShapeDtypeStruct((H, S, 1), F32), jax.ShapeDtypeStruct((H, 1, Sk), F32)]
    elif mode == "chunk":
        out_specs += [ex_specs[0]]
        out_shape += [jax.ShapeDtypeStruct((H, CHUNK, CA_BAND), F32)]

    def body(*refs):
        q_ref, k_ref, v_ref, lse_ref, do_ref = refs[:5]
        dq_ref, dk_ref, dv_ref = refs[5 + n_ex:8 + n_ex]
        rest = refs[8 + n_ex:]
        i = pl.program_id(1)

        @pl.when(i == 0)
        def _():
            dk_ref[...] = jnp.zeros_like(dk_ref)
            dv_ref[...] = jnp.zeros_like(dv_ref)
            if mode == "fox":
                rest[1][...] = jnp.zeros_like(rest[1])
            elif mode == "chunk":
                rest[0][...] = jnp.zeros_like(rest[0])

        for p in range(pl_.pp):
            lanes = pl_.lanes(p)
            qp = q_ref[:, lanes].astype(BF16)
            kp = pl_.keys(i, k_ref, p).astype(BF16)
            vp = pl_.keys(i, v_ref, p).astype(BF16)
            dop = do_ref[:, lanes].astype(BF16)
            dq = dk_part = dv_part = None
            for hh in range(pl_.hpb):
                h = p * pl_.hpb + hh
                ex = [r[h] for r in refs[5:5 + n_ex]]
                qh, doh = pl_.head(qp, hh), pl_.head(dop, hh)
                s = _scores(mode, i, pl_.tq, pl_.scale, qh, kp, ex)
                pr = jnp.exp(s - lse_ref[h])
                dp = lax.dot_general(doh, vp, _NT, preferred_element_type=F32)
                ds = pr * (dp - jnp.sum(dp * pr, axis=1, keepdims=True))
                dsb = (ds * pl_.scale).astype(BF16)
                parts = (lax.dot_general(dsb, pl_.head(kp, hh), _NN, preferred_element_type=F32),
                         lax.dot_general(dsb, qh, _TN, preferred_element_type=F32),
                         lax.dot_general(pr.astype(BF16), doh, _TN, preferred_element_type=F32))
                dq, dk_part, dv_part = parts if dq is None else (dq + parts[0], dk_part + parts[1], dv_part + parts[2])
                if mode == "chunk":
                    rest[0][h] += ds
                if mode == "fox":
                    rest[0][h] = jnp.sum(ds, axis=1, keepdims=True)
                    rest[1][h] += -jnp.sum(ds, axis=0, keepdims=True)
            dq_ref[:, lanes] = dq
            if mode == "chunk":
                win = pl.ds(pl.multiple_of(i * CHUNK, CHUNK), CA_BAND)
                dk_ref[win, lanes] += dk_part
                dv_ref[win, lanes] += dv_part
            else:
                dk_ref[:, lanes] += dk_part
                dv_ref[:, lanes] += dv_part

    return _pcall(
        body, name=name, grid=pl_.grid,
        in_specs=[pl_.rows(q), pl_.cols(k), pl_.cols(v), pl_.per_row(), pl_.rows(do)] + ex_specs,
        out_specs=out_specs, out_shape=out_shape,
        compiler_params=_params(("parallel", "arbitrary")),
    )(q[0], k[0], v[0], lse, do[0], *extra)


def _scan(x, a, h=None, *, name):
    S = x.shape[0]
    CB = SCAN_CB
    rev = h is not None
    n_grp = S // 8

    def body(*refs):
        if rev:
            x_ref, a_ref, h_ref, o_ref, da_ref = refs
        else:
            x_ref, a_ref, o_ref = refs
        ar = a_ref[:, :CB]
        ai = -a_ref[:, CB:] if rev else a_ref[:, CB:]
        zero = jnp.zeros((1, CB), F32)

        def group(g, carry):
            base = pl.multiple_of((n_grp - 1 - g) * 8 if rev else g * 8, 8)
            for j in (range(7, -1, -1) if rev else range(8)):
                t = base + j
                if rev:
                    hr, hi, dar, dai = carry
                else:
                    hr, hi = carry
                xr = x_ref[pl.ds(t, 1), :CB]
                xi = x_ref[pl.ds(t, 1), CB:]
                hr, hi = ar * hr - ai * hi + xr, ar * hi + ai * hr + xi
                o_ref[pl.ds(t, 1), :CB] = hr
                o_ref[pl.ds(t, 1), CB:] = hi
                if rev:
                    tp = jnp.maximum(t - 1, 0)
                    live = (t > 0).astype(F32)
                    pr = h_ref[pl.ds(tp, 1), :CB] * live
                    pi = h_ref[pl.ds(tp, 1), CB:] * live
                    carry = (hr, hi, dar + hr * pr + hi * pi, dai + hi * pr - hr * pi)
                else:
                    carry = (hr, hi)
            return carry

        if rev:
            _, _, dar, dai = lax.fori_loop(0, n_grp, group, (zero, zero, zero, zero))
            da_ref[:, :CB] = dar
            da_ref[:, CB:] = dai
        else:
            lax.fori_loop(0, n_grp, group, (zero, zero))

    big = pl.BlockSpec((S, 2 * CB), lambda c: (0, c))
    vec = pl.BlockSpec((1, 2 * CB), lambda c: (0, c))
    n_blk = x.shape[1] // (2 * CB)
    if rev:
        return _pcall(body, name=name, grid=(n_blk,), in_specs=[big, vec, big], out_specs=[big, vec],
                      out_shape=[jax.ShapeDtypeStruct(x.shape, F32), jax.ShapeDtypeStruct(a.shape, F32)],
                      compiler_params=_params(("parallel",)))(x, a, h)
    return _pcall(body, name=name, grid=(n_blk,), in_specs=[big, vec], out_specs=big,
                  out_shape=jax.ShapeDtypeStruct(x.shape, F32), compiler_params=_params(("parallel",)))(x, a)


def _ssm_prep1(lr_, li, ldt):
    lr = jnp.minimum(lr_, -1e-4)
    dt = jnp.exp(ldt)
    mag = jnp.exp(lr * dt)
    ar = mag * jnp.cos(li * dt)
    ai = mag * jnp.sin(li * dt)
    den = lr * lr + li * li
    gr = ((ar - 1.0) * lr + ai * li) / den
    gi = (ai * lr - (ar - 1.0) * li) / den
    return ar, ai, gr, gi


def _ssm_prep2(gr, gi, br, bi):
    return gr * br - gi * bi, gr * bi + gi * br


def _vjp_of(fn, n_in):
    def bwd(*args):
        cts = args[n_in:]
        return jax.vjp(fn, *args[:n_in])[1](cts[0] if len(cts) == 1 else tuple(cts))
    return bwd


def _to_blocked(r, i):
    lead = r.shape[:-1]
    t = jnp.stack([r.reshape(lead + (N_SSM_CH // SCAN_CB, SCAN_CB)), i.reshape(lead + (N_SSM_CH // SCAN_CB, SCAN_CB))],
                  axis=-2)
    return t.reshape(lead + (2 * N_SSM_CH,))


def _from_blocked(m):
    lead = m.shape[:-1]
    t = m.reshape(lead + (N_SSM_CH // SCAN_CB, 2, SCAN_CB))
    return t[..., 0, :].reshape(lead + (N_SSM_CH,)), t[..., 1, :].reshape(lead + (N_SSM_CH,))


def _blockdiag(r, i):
    eye = jnp.eye(SSM_GROUPS, dtype=F32)
    bd = lambda t: jnp.einsum("gcp,gh->gchp", t, eye).reshape(SSM_WIDTH, N_SSM_CH)
    return _to_blocked(bd(r), bd(i))


def _blockdiag_inv(m):
    eye = jnp.eye(SSM_GROUPS, dtype=F32)
    r, i = _from_blocked(m)
    diag = lambda t: jnp.einsum("gchp,gh->gcp", t.reshape(SSM_GROUPS, SSM_GROUP, SSM_GROUPS, SSM_STATE), eye)
    return diag(r), diag(i)


def _shift_rows(x, n):
    S = x.shape[0]
    row = lax.broadcasted_iota(jnp.int32, x.shape, 0)
    if n > 0:
        return jnp.where(row >= n, pltpu.roll(x, n, 0), 0.0)
    return jnp.where(row < S + n, pltpu.roll(x, S + n, 0), 0.0)


def _bf(x):
    return x.astype(BF16).astype(F32)


def _conv_pre(a, w, b):
    ab, wb = _bf(a), _bf(w)
    return wb[2:3] * ab + wb[1:2] * _shift_rows(ab, 1) + wb[0:1] * _shift_rows(ab, 2) + b


def _ffn_mid(up, conv_w, conv_b, dh=None, *, name):
    S = up.shape[0]
    tn = LANE
    nb = D_FF_P // tn
    rev = dh is not None

    def body(*refs):
        if not rev:
            a_ref, g_ref, w_ref, b_ref, o_ref = refs
            o_ref[...] = jax.nn.gelu(_conv_pre(a_ref[...], w_ref[...], b_ref[...])) * g_ref[...]
            return
        a_ref, g_ref, w_ref, b_ref, dh_ref, dup_a_ref, dup_g_ref, dw_ref, db_ref = refs
        a, w, dh_ = a_ref[...], w_ref[...], dh_ref[...]
        pre = _conv_pre(a, w, b_ref[...])
        gl, gelu_vjp = jax.vjp(jax.nn.gelu, pre)
        dup_g_ref[...] = dh_ * gl
        (dpre,) = gelu_vjp(dh_ * g_ref[...])
        db_ref[...] = jnp.sum(dpre, axis=0, keepdims=True)
        dpb, ab, wb = _bf(dpre), _bf(a), _bf(w)
        dup_a_ref[...] = wb[2:3] * dpb + wb[1:2] * _shift_rows(dpb, -1) + wb[0:1] * _shift_rows(dpb, -2)
        dw_ref[2:3, :] = jnp.sum(dpb * ab, axis=0, keepdims=True)
        dw_ref[1:2, :] = jnp.sum(dpb * _shift_rows(ab, 1), axis=0, keepdims=True)
        dw_ref[0:1, :] = jnp.sum(dpb * _shift_rows(ab, 2), axis=0, keepdims=True)

    a_spec = pl.BlockSpec((S, tn), lambda j: (0, j))
    g_spec = pl.BlockSpec((S, tn), lambda j: (0, j + nb))
    w_spec = pl.BlockSpec((3, tn), lambda j: (0, j))
    b_spec = pl.BlockSpec((1, tn), lambda j: (0, j))
    if not rev:
        return _pcall(body, name=name, grid=(nb,), in_specs=[a_spec, g_spec, w_spec, b_spec], out_specs=a_spec,
                      out_shape=jax.ShapeDtypeStruct((S, D_FF_P), F32), compiler_params=_params(("parallel",)))(
                          up, up, conv_w, conv_b)
    return _pcall(body, name=name, grid=(nb,), in_specs=[a_spec, g_spec, w_spec, b_spec, a_spec],
                  out_specs=[a_spec, a_spec, w_spec, b_spec],
                  out_shape=[jax.ShapeDtypeStruct((S, D_FF_P), F32), jax.ShapeDtypeStruct((S, D_FF_P), F32),
                             jax.ShapeDtypeStruct((3, D_FF_P), F32), jax.ShapeDtypeStruct((1, D_FF_P), F32)],
                  compiler_params=_params(("parallel",)))(up, up, conv_w, conv_b, dh)


def _ff_pad(t):
    lead = t.shape[:-1]
    t = t.reshape(lead + (N_DEV, FF_HALF))
    return jnp.pad(t, [(0, 0)] * len(lead) + [(0, 0), (0, FF_HALF_P - FF_HALF)]).reshape(lead + (D_FF_P,))


def _ff_unpad(t):
    lead = t.shape[:-1]
    return t.reshape(lead + (N_DEV, FF_HALF_P))[..., :FF_HALF].reshape(lead + (D_FF,))


def _ln_fwd(x, h, g, b):
    r = DN_ALPHA * x + h
    mu = jnp.mean(r, axis=-1, keepdims=True)
    var = jnp.mean(jnp.square(r - mu), axis=-1, keepdims=True)
    return r, (r - mu) * lax.rsqrt(var + LN_EPS) * g + b


def _ln_bwd(r, dy, g):
    mu = jnp.mean(r, axis=-1, keepdims=True)
    var = jnp.mean(jnp.square(r - mu), axis=-1, keepdims=True)
    xhat = (r - mu) * lax.rsqrt(var + LN_EPS)
    dxh = dy * g
    dr = lax.rsqrt(var + LN_EPS) * (dxh - jnp.mean(dxh, axis=-1, keepdims=True)
                                    - xhat * jnp.mean(dxh * xhat, axis=-1, keepdims=True))
    return dr, jnp.sum(dy * xhat, axis=0, keepdims=True), jnp.sum(dy, axis=0, keepdims=True)


def _merge(gf, gs, gc, ya, yb2, yc):
    yb = yb2[:, :D_MODEL] * jax.nn.sigmoid(yb2[:, D_MODEL:])
    return jax.nn.sigmoid(gf) * ya + jax.nn.sigmoid(gs) * yb + jax.nn.sigmoid(gc) * yc


def _s5_tail(hc, su, d):
    return jax.nn.gelu(hc + d * su)


def _s5_tail_bwd(hc, su, dgel, d):
    _, vjp = jax.vjp(jax.nn.gelu, hc + d * su)
    (dy,) = vjp(dgel)
    return dy, d * dy, jnp.sum(dy * su, axis=0, keepdims=True)


def _loss_rows(y, tgt):
    err = y - tgt
    return err * (1.0 / D_MODEL), jnp.sum(0.5 * jnp.square(err), axis=0, keepdims=True) * (1.0 / D_MODEL)


def _peer(k):
    x, y, c = lax.axis_index("x"), lax.axis_index("y"), lax.axis_index("c")
    return (x ^ ((k >> 2) & 1), y ^ ((k >> 1) & 1), c ^ (k & 1))


def _my_slot():
    return 4 * lax.axis_index("x") + 2 * lax.axis_index("y") + lax.axis_index("c")


def _peer_slot(k):
    px, py, pc = _peer(k)
    return 4 * px + 2 * py + pc


N_CHIP = N_DEV // 2
OTHER_CHIPS = (2, 4, 6)


def _chip_of(dev):
    return 2 * dev[0] + dev[1]


def _remote(src, dst, send, recv, dev):
    return pltpu.make_async_remote_copy(src_ref=src, dst_ref=dst, send_sem=send, recv_sem=recv, device_id=dev,
                                        device_id_type=pl.DeviceIdType.MESH)


def _all_gather(shards, *, name, host=None, then=None):
    return _exchange(shards, *_all_gather_parts(shards), name=name, host=host, then=then)


def _exchange(ins, out_shapes, sem_shapes, start, finish, *, name, host=None, then=None):
    if host is not None:
        rider = _Rider(host, ins, out_shapes, sem_shapes, start, finish, then)
        _RIDERS.append(rider)
        return rider
    n = len(ins)

    def body(*refs):
        start(refs[:n], refs[n:2 * n], refs[2 * n:])
        finish(refs[:n], refs[n:2 * n], refs[2 * n:])

    hbm = pl.BlockSpec(memory_space=pl.ANY)
    return _pcall(body, name=name, in_specs=[hbm] * n, out_specs=[hbm] * n, out_shape=list(out_shapes),
                  scratch_shapes=list(sem_shapes))(*ins)


def _all_gather_parts(shards):
    n = len(shards)

    def first_copies(ins, outs, sems):
        send, recv, _ = sems
        return [_remote(ins[t], outs[t].at[_my_slot()], send.at[t, k - 1], recv.at[t, k - 1], _peer(k))
                for k in (1,) + OTHER_CHIPS for t in range(n)]

    def local_copies(ins, outs, sems):
        return [pltpu.make_async_copy(ins[t], outs[t].at[_my_slot()], sems[2].at[t]) for t in range(n)]

    def start(ins, outs, sems):
        for cp in local_copies(ins, outs, sems) + first_copies(ins, outs, sems):
            cp.start()

    def finish(ins, outs, sems):
        send, recv, _ = sems
        sibling = _peer(1)
        passed = []
        for k in OTHER_CHIPS:
            for t in range(n):
                slot = outs[t].at[_peer_slot(k)]
                _remote(ins[t], slot, send.at[t, k - 1], recv.at[t, k - 1], _peer(k)).wait_recv()
                cp = _remote(slot, slot, send.at[t, k], recv.at[t, k], sibling)
                cp.start()
                passed.append(cp)
        for t in range(n):
            _remote(ins[t], outs[t].at[_peer_slot(1)], send.at[t, 0], recv.at[t, 0], sibling).wait_recv()
            for k in OTHER_CHIPS:
                _remote(ins[t], outs[t].at[_peer_slot(k + 1)], send.at[t, k], recv.at[t, k], sibling).wait_recv()
        for cp in first_copies(ins, outs, sems) + passed:
            cp.wait_send()
        for lc in local_copies(ins, outs, sems):
            lc.wait()

    out_shapes = [jax.ShapeDtypeStruct((N_DEV,) + s.shape, s.dtype) for s in shards]
    sem_shapes = [pltpu.SemaphoreType.DMA((n, N_DEV - 1)), pltpu.SemaphoreType.DMA((n, N_DEV - 1)),
                  pltpu.SemaphoreType.DMA((n,))]
    return out_shapes, sem_shapes, start, finish


def _sibling_swap(grads, *, name, host=None, then=None):
    n = len(grads)

    def copies(ins, outs, sems):
        c = lax.axis_index("c")
        return [_remote(ins[t].at[:, 1 - c], outs[t], sems[0].at[t], sems[1].at[t], _peer(1)) for t in range(n)]

    def start(ins, outs, sems):
        for cp in copies(ins, outs, sems):
            cp.start()

    def finish(ins, outs, sems):
        for cp in copies(ins, outs, sems):
            cp.wait()

    out_shapes = [jax.ShapeDtypeStruct((N_CHIP,) + g.shape[2:], g.dtype) for g in grads]
    sem_shapes = [pltpu.SemaphoreType.DMA((n,)), pltpu.SemaphoreType.DMA((n,))]
    return _exchange(grads, out_shapes, sem_shapes, start, finish, name=name, host=host, then=then)


def _pair_add(g, p, out_dtype, *, name):
    _, _, R, C = g.shape
    tr = _pick(R, (512, 256, 128, 64, 32, 16, 8))

    def body(c_ref, g_ref, p_ref, o_ref):
        o_ref[...] = (g_ref[...] + p_ref[...]).astype(out_dtype)

    grid_spec = pltpu.PrefetchScalarGridSpec(
        num_scalar_prefetch=1, grid=(N_CHIP, R // tr),
        in_specs=[pl.BlockSpec((None, None, tr, C), lambda j, i, c_ref: (j, c_ref[0], i, 0)),
                  pl.BlockSpec((None, tr, C), lambda j, i, c_ref: (j, i, 0))],
        out_specs=pl.BlockSpec((None, tr, C), lambda j, i, c_ref: (j, i, 0)))
    core = lax.axis_index("c").astype(jnp.int32).reshape(1)
    return _pcall(body, name=name, grid_spec=grid_spec, out_shape=jax.ShapeDtypeStruct(p.shape, out_dtype),
                  compiler_params=_params(("parallel", "parallel")))(core, g, p)


def _chip_exchange(sums, *, name, host=None):
    n = len(sums)

    def copies(ins, outs, sems, dst_is_mine):
        send, recv, _ = sems
        mine = 2 * lax.axis_index("x") + lax.axis_index("y")
        out = []
        for k in OTHER_CHIPS:
            theirs = _chip_of(_peer(k))
            for t in range(n):
                out.append(_remote(ins[t].at[theirs], outs[t].at[mine if dst_is_mine else theirs],
                                   send.at[t, k // 2 - 1], recv.at[t, k // 2 - 1], _peer(k)))
        return out

    def local_copies(ins, outs, sems):
        mine = 2 * lax.axis_index("x") + lax.axis_index("y")
        return [pltpu.make_async_copy(ins[t].at[mine], outs[t].at[mine], sems[2].at[t]) for t in range(n)]

    def start(ins, outs, sems):
        for cp in local_copies(ins, outs, sems) + copies(ins, outs, sems, True):
            cp.start()

    def finish(ins, outs, sems):
        for cp in copies(ins, outs, sems, False) + local_copies(ins, outs, sems):
            cp.wait()

    out_shapes = [jax.ShapeDtypeStruct(s.shape, s.dtype) for s in sums]
    sem_shapes = [pltpu.SemaphoreType.DMA((n, N_CHIP - 1)), pltpu.SemaphoreType.DMA((n, N_CHIP - 1)),
                  pltpu.SemaphoreType.DMA((n,))]
    return _exchange(sums, out_shapes, sem_shapes, start, finish, name=name, host=host)


class _GradReduce:
    def __init__(self, grads, wire_dtypes, hosts=None):
        pairs = [g.reshape((N_CHIP, 2) + g.shape[1:]) for g in grads]
        self.n, self.riders, self.recv = len(grads), [], None

        def after_swap(partner):
            sums = [_pair_add(g, p, dt, name="grad_pair_add") for g, p, dt in zip(pairs, partner, wire_dtypes)]
            if hosts is None:
                self.recv = _chip_exchange(sums, name="grad_chip_exchange")
            else:
                self.riders = [(idx, _chip_exchange([sums[i] for i in idx], name="grad_chip_exchange", host=h))
                               for h, idx in hosts[1]]

        if hosts is None:
            after_swap(_sibling_swap(pairs, name="grad_sibling_swap"))
        else:
            _sibling_swap(pairs, name="grad_sibling_swap", host=hosts[0], then=after_swap)

    def result(self):
        if self.recv is None:
            self.recv = [None] * self.n
            for idx, rider in self.riders:
                assert rider.results is not None, rider.host
                for i, r in zip(idx, rider.results):
                    self.recv[i] = r
        return self.recv


def _adamw(recv, w, m, v, layer=None, *, name):
    n_slots, R, C = recv.shape
    tr = _pick(R, (256, 128, 64, 32, 16, 8))

    def body(r_ref, w_ref, m_ref, v_ref, g_ref, d_ref, nm_ref, nv_ref):
        g = r_ref[0].astype(F32)
        for s in range(1, n_slots):
            g = g + r_ref[s].astype(F32)
        m_new = ADAM_B1 * m_ref[...] + (1.0 - ADAM_B1) * g
        v_new = ADAM_B2 * v_ref[...] + (1.0 - ADAM_B2) * jnp.square(g)
        m_hat = m_new / (1.0 - ADAM_B1 ** ADAM_STEP)
        v_hat = v_new / (1.0 - ADAM_B2 ** ADAM_STEP)
        g_ref[...] = g
        d_ref[...] = -ADAM_LR * (m_hat / (jnp.sqrt(v_hat) + ADAM_EPS) + ADAM_WD * w_ref[...])
        nm_ref[...] = m_new
        nv_ref[...] = v_new

    row = pl.BlockSpec((tr, C), lambda i: (i, 0))
    state = row if layer is None else pl.BlockSpec((None, tr, C), lambda i: (layer, i, 0))
    return _pcall(
        body, name=name, grid=(R // tr,),
        in_specs=[pl.BlockSpec((n_slots, tr, C), lambda i: (0, i, 0)), state, state, state],
        out_specs=[row] * 4, out_shape=[jax.ShapeDtypeStruct((R, C), F32)] * 4,
        compiler_params=_params(("parallel",)),
    )(recv, w, m, v)


def _flat_pad(parts, rows):
    flat = jnp.concatenate([p.reshape(-1) for p in parts])
    return jnp.pad(flat, (0, rows * LANES - flat.shape[0])).reshape(rows, LANES)


def _small_shard_shape(n):
    return SMALL[n][:-1] + (SMALL[n][-1] // N_DEV,)


def _unpack_small(gathered):
    out, off = {}, 0
    flat = gathered.reshape(N_DEV, -1)
    for n in SMALL:
        r, c = _small_shard_shape(n)
        out[n] = flat[:, off:off + r * c].reshape(N_DEV, r, c).transpose(1, 0, 2).reshape(r, N_DEV * c)
        off += r * c
    return out


def _pack_state(state, prefix):
    flat = jnp.concatenate([state[prefix + n].reshape(DEPTH, -1) for n in (*SMALL, *REPL)], axis=1)
    return jnp.pad(flat, ((0, 0), (0, PACK_ROWS * LANES - flat.shape[1]))).reshape(DEPTH * PACK_ROWS, LANES)


def _pack_small_grads(grads):
    cols = []
    for n in SMALL:
        r, c = _small_shard_shape(n)
        cols.append(grads[n].reshape(r, N_DEV, c).transpose(1, 0, 2).reshape(N_DEV, r * c))
    cols += [jnp.broadcast_to(grads[n].reshape(1, -1), (N_DEV, grads[n].size)) for n in REPL]
    flat = jnp.concatenate(cols, axis=1)
    return jnp.pad(flat, ((0, 0), (0, PACK_ROWS * LANES - flat.shape[1]))).reshape(N_DEV, PACK_ROWS, LANES)


def _unpack_state(packed):
    out, off = {}, 0
    flat = packed.reshape(DEPTH, -1)
    for n, shape in [(n, _small_shard_shape(n)) for n in SMALL] + list(REPL.items()):
        sz = math.prod(shape)
        out[n] = flat[:, off:off + sz].reshape((DEPTH,) + shape)
        off += sz
    return out


def _pad_b_in(b):
    parts, pos = [], 0
    for src, width, dst in Z_PIECES:
        parts += [jnp.zeros((b.shape[0], dst - pos), b.dtype), b[:, src:src + width]]
        pos = dst + width
    return jnp.concatenate(parts + [jnp.zeros((b.shape[0], Z_W - pos), b.dtype)], axis=1)


def _unpad_b_in(bp):
    return jnp.concatenate([bp[:, dst:dst + width] for _, width, dst in Z_PIECES], axis=1)


def _up_shard_pad(t):
    lead = t.shape[:-1]
    t = jnp.pad(t.reshape(lead + (2, FF_HALF)), [(0, 0)] * len(lead) + [(0, 0), (0, FF_HALF_P - FF_HALF)])
    return t.reshape(lead + (2 * FF_HALF_P,))


def _up_shard_unpad(t):
    lead = t.shape[:-1]
    return t.reshape(lead + (2, FF_HALF_P))[..., :FF_HALF].reshape(lead + (2 * FF_HALF,))


def _pad_shard(n, t):
    if n == "w_in":
        return jnp.pad(t, ((0, 0), (0, W_IN_SLOT - W_IN_SHARD)))
    if n == "ffn_w_up":
        return _up_shard_pad(t)
    if n == "ffn_w_down":
        return jnp.pad(t, ((0, FF_HALF_P - FF_HALF), (0, 0)))
    return t


def _unpad_shard(n, t):
    if n == "w_in":
        return t[:, :W_IN_SHARD]
    if n == "ffn_w_up":
        return _up_shard_unpad(t)
    if n == "ffn_w_down":
        return t[:FF_HALF]
    return t


GATHER_HOSTS = (("fox_fwd", (0,)), ("chunk_fwd", (8, 9)), ("mm_up", (1, 2, 3, 4, 5, 6, 7, 10)))
REDUCE_HOSTS = ("mm_up_dw", (("fox_bwd", (0,)), ("chunk_bwd", (8, 9)), ("s5_scan_bwd", (1, 2, 3, 4, 5, 6, 7, 10))))


def _layer_shards(state, layer):
    shards = [_pad_shard(n, state[n][layer].astype(BF16)) for n in BIG]
    return shards + [_flat_pad([state[n][layer] for n in SMALL], SMALL_ROWS)]


def _gathered_weights(gathered):
    *big, small = gathered
    W = dict(zip(BIG, big))
    for n in ("w_o", "xa_wq", "xa_wo", "ffn_w_down"):
        W[n] = W[n].reshape(-1, D_MODEL)
    W["w_in_p"] = _colmap(W.pop("w_in"), inverse=False, name="w_in_colmap")
    W.update(_unpack_small(small))
    return W


def _ssm_params(p):
    prep1_in = (p["ssm_lambda_re"], p["ssm_lambda_im"], p["ssm_log_dt"][:, None])
    ar, ai, gr, gi = _whole(_ssm_prep1, *prep1_in, name="ssm_prep1")
    to_cn = lambda b: b.transpose(2, 0, 1).reshape(SSM_GROUP, N_SSM_CH)
    prep2_in = (gr.reshape(1, N_SSM_CH), gi.reshape(1, N_SSM_CH), to_cn(p["ssm_b_re"]), to_cn(p["ssm_b_im"]))
    bbr, bbi = _whole(_ssm_prep2, *prep2_in, name="ssm_prep2")
    to_gcp = lambda t: t.reshape(SSM_GROUP, SSM_GROUPS, SSM_STATE).transpose(1, 0, 2)
    bb = _blockdiag(to_gcp(bbr), to_gcp(bbi))
    cct = _blockdiag(p["ssm_c_re"], -p["ssm_c_im"])
    a_vec = _to_blocked(ar.reshape(1, N_SSM_CH), ai.reshape(1, N_SSM_CH))
    return dict(bb=bb, cct=cct, a_vec=a_vec, prep1_in=prep1_in, prep2_in=prep2_in)


def _layer_fwd(x, mem, p, W):
    sp = _ssm_params(p)
    z = _mm(x, W["w_in_p"], bias=p["b_in_p"], name="mm_in")
    f_t = z[:, Z_FF:Z_FF + FOX_HEADS].T
    cum = _cum_heads(f_t, name="fox_cum")
    ya_pre, lse_a = _attn_fwd("fox", (z, Z_FQ), (z, Z_FK), (z, Z_FV), (cum[:, :, None], cum[:, None, :]),
                              name="fox_fwd")
    ya = _mm(ya_pre, W["w_fox_o"], b_slots=True, name="mm_fox_o")
    x_ri = _mm(z, sp["bb"], a_off=Z_SU, a_cols=SSM_WIDTH, name="mm_s5_in")
    h_ri = _scan(x_ri, sp["a_vec"], name="s5_scan")
    hc = _mm(h_ri, sp["cct"], tb=True, name="mm_s5_out")
    d_row = p["ssm_d"][None, :]
    (gel,) = _rowwise(lambda a, b, c: (_s5_tail(a, b, c),), [hc, Win(z, 512, Z_SU // 512)], [d_row], name="s5_tail")
    yb2 = _mm(gel, W["w_ssm_glu"], b_slots=True, name="mm_glu")
    bias = _relbias_expand(W["ca_rel_bias"]).transpose(1, 0, 2)
    kv_band = jnp.pad(z[:, Z_CK:Z_CK + 2 * CA_WIDTH].astype(BF16), ((CA_PAD, 0), (0, 0)))
    yc_pre, lse_c = _attn_fwd("chunk", (z, Z_CQ), (kv_band, 0), (kv_band, CA_WIDTH), (bias,), name="chunk_fwd")
    yc = _mm(yc_pre, W["w_ca_o"], b_slots=True, name="mm_ca_o")
    gates = [Win(z, 1024, Z_GF // 1024), Win(z, 1024, Z_GS // 1024), Win(z, 1024, Z_GC // 1024)]
    (merged,) = _rowwise(lambda *a: (_merge(*a),), gates + [ya, yb2, yc], name="merge")
    h1 = _mm(merged, W["w_o"], name="mm_o")
    ln_g, ln_b = W["ln_g"], W["ln_b"]
    r1, x1 = _rowwise(_ln_fwd, [x, h1], [ln_g[0:1], ln_b[0:1]], name="ln_fwd")
    q = _mm(x1, W["xa_wq"], name="mm_xq")
    kv = _mm(mem, W["xa_wkv"], b_slots=True, name="mm_xkv")
    o, lse_x = _attn_fwd("xa", (q, 0), (kv, 0), (kv, D_MODEL), name="xa_fwd")
    h2 = _mm(o, W["xa_wo"], name="mm_xo")
    r2, x2 = _rowwise(_ln_fwd, [x1, h2], [ln_g[1:2], ln_b[1:2]], name="ln_fwd")
    up = _mm(x2, W["ffn_w_up"], b_slots=True, name="mm_up")
    hmid = _ffn_mid(up, _ff_pad(W["ffn_conv_w"]), _ff_pad(p["ffn_conv_b"][None, :]), name="ffn_mid")
    h3 = _mm(hmid, W["ffn_w_down"], name="mm_down")
    r3, x3 = _rowwise(_ln_fwd, [x2, h3], [ln_g[2:3], ln_b[2:3]], name="ln_fwd")
    res = dict(x=x, z=z, cum=cum, lse_a=lse_a, ya_pre=ya_pre, ya=ya, h_ri=h_ri, hc=hc, gel=gel, yb2=yb2, lse_c=lse_c,
               yc_pre=yc_pre, yc=yc, merged=merged, r1=r1, x1=x1, q=q, kv=kv, o=o, lse_x=lse_x, r2=r2, x2=x2, up=up,
               hmid=hmid, r3=r3, bias=bias, sp=sp, kv_band=kv_band, W=W)
    return x3, res


def _layer_bwd(dx3, mem, p, res):
    W = res["W"]
    x, z = res["x"], res["z"]
    sp = res["sp"]
    ln_g = W["ln_g"]
    big, small = {}, {}
    slots = lambda t: t.reshape(N_DEV, -1, D_MODEL)
    dr3, dg2, db2 = _rowwise(_ln_bwd, [res["r3"], dx3], [ln_g[2:3]], n_red=2, name="ln_bwd")
    dhmid = _mm(dr3, W["ffn_w_down"], tb=True, name="mm_down_dx")
    big["ffn_w_down"] = slots(_mm(res["hmid"], dr3, ta=True, name="mm_down_dw"))
    conv_w_p, conv_b_p = _ff_pad(W["ffn_conv_w"]), _ff_pad(p["ffn_conv_b"][None, :])
    dup_a, dup_g, dcw, dcb = _ffn_mid(res["up"], conv_w_p, conv_b_p, dhmid, name="ffn_mid_bwd")
    dup = jnp.concatenate([dup_a, dup_g], axis=1)
    small["ffn_conv_w"], small["ffn_conv_b"] = _ff_unpad(dcw), _ff_unpad(dcb)[0]
    dx2 = _mm(dup, W["ffn_w_up"], tb=True, b_slots=True, add=(dr3, DN_ALPHA), name="mm_up_dx")
    big["ffn_w_up"] = _mm(res["x2"], dup, ta=True, out_slots=2 * FF_HALF_P, name="mm_up_dw")
    dr2, dg1, db1 = _rowwise(_ln_bwd, [res["r2"], dx2], [ln_g[1:2]], n_red=2, name="ln_bwd")
    do = _mm(dr2, W["xa_wo"], tb=True, name="mm_xo_dx")
    big["xa_wo"] = slots(_mm(res["o"], dr2, ta=True, name="mm_xo_dw"))
    kv = res["kv"]
    dq, dk, dv = _attn_bwd("xa", (res["q"], 0), (kv, 0), (kv, D_MODEL), res["lse_x"], (do, 0), name="xa_bwd")
    dkv = jnp.concatenate([dk, dv], axis=1)
    dx1 = _mm(dq, W["xa_wq"], tb=True, add=(dr2, DN_ALPHA), name="mm_xq_dx")
    big["xa_wq"] = slots(_mm(res["x1"], dq, ta=True, name="mm_xq_dw"))
    big["xa_wkv"] = _mm(mem, dkv, ta=True, out_slots=256, name="mm_xkv_dw")
    dr1, dg0, db0 = _rowwise(_ln_bwd, [res["r1"], dx1], [ln_g[0:1]], n_red=2, name="ln_bwd")
    small["ln_g"] = jnp.concatenate([dg0, dg1, dg2], axis=0)
    small["ln_b"] = jnp.concatenate([db0, db1, db2], axis=0)
    dmerged = _mm(dr1, W["w_o"], tb=True, name="mm_o_dx")
    big["w_o"] = slots(_mm(res["merged"], dr1, ta=True, name="mm_o_dw"))
    gates = [Win(z, 1024, Z_GF // 1024), Win(z, 1024, Z_GS // 1024), Win(z, 1024, Z_GC // 1024)]
    dgf, dgs, dgc, dya, dyb2, dyc = _rowwise(_vjp_of(_merge, 6), gates + [res["ya"], res["yb2"], res["yc"], dmerged],
                                             name="merge_bwd")
    dya_pre = _mm(dya, W["w_fox_o"], tb=True, b_slots=True, name="mm_fox_o_dx")
    big["w_fox_o"] = _mm(res["ya_pre"], dya, ta=True, out_slots=128, name="mm_fox_o_dw")
    cum = res["cum"]
    dfq, dfk, dfv, dcq, dck = _attn_bwd("fox", (z, Z_FQ), (z, Z_FK), (z, Z_FV), res["lse_a"], (dya_pre, 0),
                                        (cum[:, :, None], cum[:, None, :]), name="fox_bwd")
    f_t = z[:, Z_FF:Z_FF + FOX_HEADS].T
    dff = _cum_heads(f_t, dcq[:, :, 0] + dck[:, 0, :], name="fox_cum_bwd")
    dgel = _mm(dyb2, W["w_ssm_glu"], tb=True, b_slots=True, name="mm_glu_dx")
    big["w_ssm_glu"] = _mm(res["gel"], dyb2, ta=True, out_slots=256, name="mm_glu_dw")
    d_row = p["ssm_d"][None, :]
    su_win = Win(z, 512, Z_SU // 512)
    dy, dsu1, dd = _rowwise(_s5_tail_bwd, [res["hc"], su_win, dgel], [d_row], n_red=1, name="s5_tail_bwd")
    small["ssm_d"] = dd[0]
    dh_ri = _mm(dy, sp["cct"], name="mm_s5_out_dx")
    dcct = _mm(dy, res["h_ri"], ta=True, name="mm_s5_out_dw")
    dx_ri, da_vec = _scan(dh_ri, sp["a_vec"], res["h_ri"], name="s5_scan_bwd")
    dsu = _mm(dx_ri, sp["bb"], tb=True, add=(dsu1, 1.0), name="mm_s5_in_dx")
    dbb = _mm(z, dx_ri, ta=True, a_off=Z_SU, a_cols=SSM_WIDTH, name="mm_s5_in_dw")
    dcr, dci = _blockdiag_inv(dcct)
    small["ssm_c_re"], small["ssm_c_im"] = dcr, -dci
    dbbr, dbbi = _blockdiag_inv(dbb)
    to_cn = lambda t: t.transpose(1, 0, 2).reshape(SSM_GROUP, N_SSM_CH)
    dgr, dgi, dbr, dbi = _whole(_vjp_of(_ssm_prep2, 4), *sp["prep2_in"], to_cn(dbbr), to_cn(dbbi), name="ssm_prep2_bwd")
    from_cn = lambda t: t.reshape(SSM_GROUP, SSM_GROUPS, SSM_STATE).transpose(1, 2, 0)
    small["ssm_b_re"], small["ssm_b_im"] = from_cn(dbr), from_cn(dbi)
    dar, dai = _from_blocked(da_vec)
    sq = lambda t: t.reshape(SSM_GROUPS, SSM_STATE)
    dlr, dli, dldt = _whole(_vjp_of(_ssm_prep1, 3), *sp["prep1_in"], sq(dar), sq(dai), sq(dgr), sq(dgi),
                            name="ssm_prep1_bwd")
    small["ssm_lambda_re"], small["ssm_lambda_im"], small["ssm_log_dt"] = dlr, dli, dldt[:, 0]
    dyc_pre = _mm(dyc, W["w_ca_o"], tb=True, b_slots=True, name="mm_ca_o_dx")
    big["w_ca_o"] = _mm(res["yc_pre"], dyc, ta=True, out_slots=128, name="mm_ca_o_dw")
    bias = res["bias"]
    kv_band = res["kv_band"]
    dcq_, dck_band, dcv_band, dbias = _attn_bwd("chunk", (z, Z_CQ), (kv_band, 0), (kv_band, CA_WIDTH), res["lse_c"],
                                                (dyc_pre, 0), (bias,), name="chunk_bwd")
    small["ca_rel_bias"] = _relbias_reduce(dbias.transpose(1, 0, 2))
    dff_p = jnp.pad(dff.T, ((0, 0), (0, 512 - FOX_HEADS)))
    dz = jnp.concatenate([dfq, dfk, dfv, dff_p, dsu, dcq_, dck_band[CA_PAD:], dcv_band[CA_PAD:], dgf, dgs, dgc],
                         axis=1)
    dx = _mm(dz, W["w_in_p"], tb=True, add=(dr1, DN_ALPHA), name="mm_in_dx")
    big["w_in"] = _colmap(_mm(x, dz, ta=True, name="mm_in_dw"), inverse=True, name="w_in_colmap_inv")
    (db_in_p,) = _rowwise(lambda t: (jnp.sum(t, axis=0, keepdims=True),), [dz], n_red=1, name="colsum")
    small["b_in"] = _unpad_b_in(db_in_p)[0]
    return dx, big, small


def kernel(x, mem, w_in, b_in, ssm_lambda_re, ssm_lambda_im, ssm_log_dt, ssm_b_re, ssm_b_im, ssm_c_re, ssm_c_im, ssm_d, ca_rel_bias, w_fox_o, w_ssm_glu, w_ca_o, w_o, xa_wq, xa_wkv, xa_wo, ffn_w_up, ffn_conv_w, ffn_conv_b, ffn_w_down, ln_g, ln_b, loss_target, m_w_in, m_b_in, m_ssm_lambda_re, m_ssm_lambda_im, m_ssm_log_dt, m_ssm_b_re, m_ssm_b_im, m_ssm_c_re, m_ssm_c_im, m_ssm_d, m_ca_rel_bias, m_w_fox_o, m_w_ssm_glu, m_w_ca_o, m_w_o, m_xa_wq, m_xa_wkv, m_xa_wo, m_ffn_w_up, m_ffn_conv_w, m_ffn_conv_b, m_ffn_w_down, m_ln_g, m_ln_b, v_w_in, v_b_in, v_ssm_lambda_re, v_ssm_lambda_im, v_ssm_log_dt, v_ssm_b_re, v_ssm_b_im, v_ssm_c_re, v_ssm_c_im, v_ssm_d, v_ca_rel_bias, v_w_fox_o, v_w_ssm_glu, v_w_ca_o, v_w_o, v_xa_wq, v_xa_wkv, v_xa_wo, v_ffn_w_up, v_ffn_conv_w, v_ffn_conv_b, v_ffn_w_down, v_ln_g, v_ln_b):
    given = dict(locals())
    state = {pre + n: given[pre + n] for n in WEIGHTS for pre in ("", "m_", "v_")}
    mem0 = mem[0]
    b_in_p = _pad_b_in(b_in)
    layer_params = [{**{n: state[n][l] for n in REPL}, "b_in_p": b_in_p[l:l + 1]} for l in range(DEPTH)]

    _RIDERS.clear()
    h, residuals = x[0], []
    gathered = _all_gather(_layer_shards(state, 0), name="all_gather_weights")
    for l in range(DEPTH):
        riders = []
        if l + 1 < DEPTH:
            shards = _layer_shards(state, l + 1)
            riders = [(idx, _all_gather([shards[i] for i in idx], name="all_gather_weights", host=host))
                      for host, idx in GATHER_HOSTS]
        h, res = _layer_fwd(h, mem0, layer_params[l], _gathered_weights(gathered))
        residuals.append(res)
        gathered = [None] * (len(BIG) + 1)
        for idx, rider in riders:
            for i, r in zip(idx, rider.results):
                gathered[i] = r
    dh, loss_cols = _rowwise(_loss_rows, [h, loss_target[0]], n_red=1, name="loss")
    loss = lax.psum(jnp.sum(loss_cols), ("x", "y", "c"))

    outs = [None] * DEPTH
    wire = [BF16] * len(BIG) + [F32]
    pending = None
    for l in reversed(range(-1, DEPTH)):
        if l >= 0:
            dh, big, small = _layer_bwd(dh, mem0, layer_params[l], residuals[l])
            reduce = _GradReduce([big[n] for n in BIG] + [_pack_small_grads(small)], wire,
                                 hosts=REDUCE_HOSTS if l > 0 else None)
        done, pending = pending, (l, reduce) if l >= 0 else None
        if done is None:
            continue
        l_done, reduce_done = done
        *recv_big, recv_small = reduce_done.result()
        layer_out = {}
        for n, recv in zip(BIG, recv_big):
            if recv.shape[1:] == BIG[n][0]:
                res4 = _adamw(recv, state[n], state["m_" + n], state["v_" + n], l_done, name="adamw_" + n)
            else:
                padded = [_pad_shard(n, state[pre + n][l_done]) for pre in ("", "m_", "v_")]
                res4 = [_unpad_shard(n, t) for t in _adamw(recv, *padded, name="adamw_" + n)]
            layer_out[n] = res4
        layer_out["recv_small"] = recv_small
        outs[l_done] = layer_out

    assert not _RIDERS, [r.host for r in _RIDERS]
    recv_small = jnp.concatenate([outs[l]["recv_small"] for l in range(DEPTH)], axis=1)
    packed = _adamw(recv_small, *[_pack_state(state, pre) for pre in ("", "m_", "v_")], name="adamw_small")
    small_out = [_unpack_state(t) for t in packed]
    stacked = lambda n, j: small_out[j][n] if n not in BIG else jnp.stack([outs[l][n][j] for l in range(DEPTH)])
    return (loss, dh[None], *[stacked(n, j) for j in range(4) for n in WEIGHTS])
```

```python
import functools
import math

import jax
import jax.numpy as jnp
from jax import lax
from jax.experimental import pallas as pl
from jax.experimental.pallas import tpu as pltpu

F32, BF16 = jnp.float32, jnp.bfloat16

D_MODEL = 1024
DEPTH = 4
CHUNK = 64
FOX_HEADS, FOX_HEAD_DIM, FOX_WIDTH = 8, 64, 512
SSM_GROUP, SSM_WIDTH, SSM_GROUPS, SSM_STATE = 16, 512, 32, 64
CA_HEADS, CA_HEAD_DIM, CA_WIDTH, CA_LEFT_CHUNKS = 8, 64, 512, 8
CA_BAND = (CA_LEFT_CHUNKS + 1) * CHUNK
CA_PAD = CA_LEFT_CHUNKS * CHUNK
REL_MIN, REL_MAX = -(CHUNK - 1), 4 * CHUNK
N_REL = REL_MAX - REL_MIN + 1
XA_HEADS, XA_HEAD_DIM = 4, 256
D_FF = 2816
DN_ALPHA = (2 * DEPTH) ** 0.25
LN_EPS = 1e-5
NEG_INF = -1e30
ADAM_LR, ADAM_B1, ADAM_B2, ADAM_EPS, ADAM_WD, ADAM_STEP = 0.001, 0.9, 0.999, 1e-08, 0.01, 10

N_DEV = 8
LANE = 128
N_SSM_CH = SSM_GROUPS * SSM_STATE
SCAN_CB = 256
N_IN = 6664
W_IN_SHARD, W_IN_SLOT = N_IN // N_DEV, 896
Z_W = 7168
Z_FQ, Z_FK, Z_FV, Z_FF, Z_SU, Z_CQ, Z_CK, Z_CV, Z_GF, Z_GS, Z_GC = (
    0, 512, 1024, 1536, 2048, 2560, 3072, 3584, 4096, 5120, 6144)
Z_PIECES = ((0, 512, Z_FQ), (512, 512, Z_FK), (1024, 512, Z_FV), (1536, 8, Z_FF), (1544, 512, Z_SU),
            (2056, 512, Z_CQ), (2568, 512, Z_CK), (3080, 512, Z_CV), (3592, 1024, Z_GF), (4616, 1024, Z_GS),
            (5640, 1024, Z_GC))
FF_HALF, FF_HALF_P = D_FF // N_DEV, 384
D_FF_P = N_DEV * FF_HALF_P

VMEM_LIMIT_BYTES = 56 * 1024 * 1024

BIG = {"w_in": ((1024, W_IN_SHARD), 1), "w_fox_o": ((512, 128), 1), "w_ssm_glu": ((512, 256), 1),
       "w_ca_o": ((512, 128), 1), "w_o": ((128, 1024), 0), "xa_wq": ((128, 1024), 0), "xa_wkv": ((1024, 256), 1),
       "xa_wo": ((128, 1024), 0), "ffn_w_up": ((1024, 2 * FF_HALF), 1), "ffn_w_down": ((FF_HALF, 1024), 0)}
SMALL = {"ca_rel_bias": (8, 320), "ffn_conv_w": (3, D_FF), "ln_g": (3, 1024), "ln_b": (3, 1024)}
REPL = {"b_in": (N_IN,), "ssm_lambda_re": (32, 64), "ssm_lambda_im": (32, 64), "ssm_log_dt": (32,),
        "ssm_b_re": (32, 64, 16), "ssm_b_im": (32, 64, 16), "ssm_c_re": (32, 16, 64), "ssm_c_im": (32, 16, 64),
        "ssm_d": (512,), "ffn_conv_b": (D_FF,)}
WEIGHTS = ("w_in", "b_in", "ssm_lambda_re", "ssm_lambda_im", "ssm_log_dt", "ssm_b_re", "ssm_b_im", "ssm_c_re",
           "ssm_c_im", "ssm_d", "ca_rel_bias", "w_fox_o", "w_ssm_glu", "w_ca_o", "w_o", "xa_wq", "xa_wkv", "xa_wo",
           "ffn_w_up", "ffn_conv_w", "ffn_conv_b", "ffn_w_down", "ln_g", "ln_b")
LANES = 1024
PACK_ROWS = 256
SMALL_ROWS = 8


def _w_in_segments():
    segs = []
    for src, width, dst in Z_PIECES:
        n = src
        while n < src + width:
            d = n // W_IN_SHARD
            end = min(src + width, (d + 1) * W_IN_SHARD)
            segs.append((W_IN_SLOT * d + n - W_IN_SHARD * d, dst + n - src, end - n))
            n = end
    return tuple(segs)


W_IN_SEGS = _w_in_segments()


def _pallas(body, **kw):
    return pl.pallas_call(body, **kw)


def _params(sem):
    return pltpu.CompilerParams(dimension_semantics=sem, vmem_limit_bytes=VMEM_LIMIT_BYTES)


class _Rider:
    def __init__(self, host, ins, out_shapes, sem_shapes, start, finish, then=None):
        self.host, self.ins, self.out_shapes, self.sem_shapes = host, list(ins), list(out_shapes), list(sem_shapes)
        self.start, self.finish, self.then, self.results = start, finish, then, None


_RIDERS = []


def _pcall(body, *, name, **kw):
    rider = next((r for r in _RIDERS if r.host == name), None)
    if rider is None:
        return _pallas(body, name=name, **kw)
    _RIDERS.remove(rider)
    grid, in_specs, scratch = kw["grid"], list(kw["in_specs"]), list(kw.get("scratch_shapes", ()))
    single = not isinstance(kw["out_shape"], (list, tuple))
    out_specs = [kw["out_specs"]] if single else list(kw["out_specs"])
    out_shape = [kw["out_shape"]] if single else list(kw["out_shape"])
    n_in, n_out, n_scr = len(in_specs), len(out_specs), len(scratch)
    r_in, r_out = len(rider.ins), len(rider.out_shapes)

    def fused(*refs):
        a, ra = refs[:n_in], refs[n_in:n_in + r_in]
        o, ro = refs[n_in + r_in:n_in + r_in + n_out], refs[n_in + r_in + n_out:n_in + r_in + n_out + r_out]
        scr, sems = refs[n_in + r_in + n_out + r_out:][:n_scr], refs[n_in + r_in + n_out + r_out + n_scr:]
        ids = [pl.program_id(d) for d in range(len(grid))]
        first = functools.reduce(jnp.logical_and, [i == 0 for i in ids])
        last = functools.reduce(jnp.logical_and, [i == g - 1 for i, g in zip(ids, grid)])

        @pl.when(first)
        def _():
            rider.start(ra, ro, sems)

        body(*a, *o, *scr)

        @pl.when(last)
        def _():
            rider.finish(ra, ro, sems)

    hbm = pl.BlockSpec(memory_space=pl.ANY)
    call = _pallas(fused, name=name, grid=grid, in_specs=in_specs + [hbm] * r_in, out_specs=out_specs + [hbm] * r_out,
                   out_shape=out_shape + rider.out_shapes, scratch_shapes=scratch + rider.sem_shapes,
                   compiler_params=_params(("arbitrary",) * len(grid)))

    def run(*operands):
        outs = call(*operands, *rider.ins)
        rider.results = list(outs[n_out:])
        if rider.then is not None:
            rider.then(rider.results)
        return outs[0] if single else list(outs[:n_out])

    return run


def _pick(dim, prefs):
    for p in prefs:
        if dim % p == 0:
            return p
    return dim


def _mm(a, b, *, ta=False, tb=False, bias=None, add=None, a_off=0, a_cols=None, b_slots=False, out_slots=None,
        name, out_dtype=F32):
    a_cols = a_cols if a_cols is not None else a.shape[1]
    M, K = (a_cols, a.shape[0]) if ta else (a.shape[0], a_cols)
    tm = _pick(M, (1024, 512, 256, 128))
    if ta:
        tk = _pick(K, (2048, 1024, 512, 256))
    elif b_slots and tb:
        tk = _pick(b.shape[2], (1024, 768, 512, 256, 128))
    else:
        tk = K if K <= 3072 else _pick(K, (1024, 512, 256, 128))
    nk = K // tk
    if b_slots:
        ns = b.shape[2]
        if tb:
            N = b.shape[1]
            tn = _pick(N, (512, 256, 128))
            per = ns // tk
            b_spec = pl.BlockSpec((None, tn, tk), lambda i, j, k: (k // per, j, k % per))
            b_dim = 1
            assert N_DEV * ns == K
        else:
            N = N_DEV * ns
            tn = _pick(ns, (512, 256, 128))
            per = ns // tn
            b_spec = pl.BlockSpec((None, tk, tn), lambda i, j, k: (j // per, k, j % per))
            b_dim = 0
            assert b.shape[1] == K
    else:
        N = b.shape[0] if tb else b.shape[1]
        assert (b.shape[1] if tb else b.shape[0]) == K, (a.shape, b.shape, ta, tb)
        tn = _pick(N if out_slots is None else out_slots, (512, 256, 128))
        if tb:
            b_spec = pl.BlockSpec((tn, tk), lambda i, j, k: (j, k))
            b_dim = 1
        else:
            b_spec = pl.BlockSpec((tk, tn), lambda i, j, k: (k, j))
            b_dim = 0
    if ta:
        assert a_off % tm == 0
        a_spec = pl.BlockSpec((tk, tm), lambda i, j, k: (k, i + a_off // tm))
        a_dim = 0
    else:
        assert a_off % tk == 0
        a_spec = pl.BlockSpec((tm, tk), lambda i, j, k: (i, k + a_off // tk))
        a_dim = 1
    cache_at = ta and nk == 1
    dims = (((1 if cache_at else a_dim,), (b_dim,)), ((), ()))
    ins, specs = [a, b], [a_spec, b_spec]
    if bias is not None:
        ins.append(bias)
        specs.append(pl.BlockSpec((1, tn), lambda i, j, k: (0, j)))
    add_scale = None
    if add is not None:
        ins.append(add[0])
        add_scale = add[1]
        specs.append(pl.BlockSpec((tm, tn), lambda i, j, k: (i, j)))
    if out_slots is None:
        out_spec = pl.BlockSpec((tm, tn), lambda i, j, k: (i, j))
        out_shape = jax.ShapeDtypeStruct((M, N), out_dtype)
    else:
        assert N == N_DEV * out_slots
        per_o = out_slots // tn
        out_spec = pl.BlockSpec((None, tm, tn), lambda i, j, k: (j // per_o, i, j % per_o))
        out_shape = jax.ShapeDtypeStruct((N_DEV, M, out_slots), out_dtype)

    def body(*refs):
        a_ref, b_ref = refs[0], refs[1]
        pos = 2
        bias_ref = add_ref = None
        if bias is not None:
            bias_ref = refs[pos]
            pos += 1
        if add is not None:
            add_ref = refs[pos]
            pos += 1
        o_ref = refs[pos]
        acc_ref = refs[pos + 1] if nk > 1 else None
        if cache_at:
            at_ref = refs[pos + 1]

            @pl.when(pl.program_id(1) == 0)
            def _():
                step = min(tk, 256)
                for c in range(0, tk, step):
                    at_ref[:, c:c + step] = a_ref[c:c + step, :].T.astype(BF16)

            lhs = at_ref[...]
        else:
            lhs = a_ref[...].astype(BF16)
        part = lax.dot_general(lhs, b_ref[...].astype(BF16), dims, preferred_element_type=F32)

        def finish(acc):
            if bias_ref is not None:
                acc = acc + bias_ref[...]
            if add_ref is not None:
                acc = acc + add_scale * add_ref[...]
            o_ref[...] = acc.astype(out_dtype)

        if nk == 1:
            finish(part)
        else:
            k = pl.program_id(2)

            @pl.when(k == 0)
            def _():
                acc_ref[...] = part

            @pl.when(k > 0)
            def _():
                acc_ref[...] += part

            @pl.when(k == nk - 1)
            def _():
                finish(acc_ref[...])

    return _pcall(
        body, name=name, grid=(M // tm, N // tn, nk), in_specs=specs, out_specs=out_spec, out_shape=out_shape,
        scratch_shapes=[pltpu.VMEM((tm, tn), F32)] if nk > 1 else [pltpu.VMEM((tm, tk), BF16)] if cache_at else [],
        compiler_params=_params(("parallel", "arbitrary", "arbitrary")),
    )(*ins)


SSM_BD = 4
SSM_BD_IN, SSM_BD_ST = SSM_WIDTH // SSM_BD, 2 * N_SSM_CH // SSM_BD


def _mm_bd(form, a, b, *, a_off=0, add=None, name):
    S = a.shape[0]
    off = a_off // SSM_BD_IN
    if form == "dw":
        def body(a_ref, b_ref, o_ref):
            at = a_ref[...].T.astype(BF16)
            o_ref[...] = lax.dot_general(at, b_ref[...].astype(BF16), _NN, preferred_element_type=F32)

        return _pcall(body, name=name, grid=(SSM_BD,),
                      in_specs=[pl.BlockSpec((S, SSM_BD_IN), lambda j: (0, off + j)),
                                pl.BlockSpec((S, SSM_BD_ST), lambda j: (0, j))],
                      out_specs=pl.BlockSpec((None, SSM_BD_IN, SSM_BD_ST), lambda j: (j, 0, 0)),
                      out_shape=jax.ShapeDtypeStruct((SSM_BD, SSM_BD_IN, SSM_BD_ST), F32),
                      compiler_params=_params(("parallel",)))(a, b)
    tm = _pick(S, (1024, 512, 256))
    wide, narrow = (SSM_BD_ST, SSM_BD_IN) if form == "in" else (SSM_BD_IN, SSM_BD_ST)
    dims = _NN if form == "in" else _NT

    def body(a_ref, w_ref, *rest):
        acc = lax.dot_general(a_ref[...].astype(BF16), w_ref[...].astype(BF16), dims, preferred_element_type=F32)
        if add is not None:
            acc = acc + add[1] * rest[0][...]
        rest[-1][...] = acc

    specs = [pl.BlockSpec((tm, narrow), lambda i, j: (i, off + j)),
             pl.BlockSpec((None, SSM_BD_IN, SSM_BD_ST), lambda i, j: (j, 0, 0))]
    ins = [a, b]
    if add is not None:
        specs.append(pl.BlockSpec((tm, wide), lambda i, j: (i, j)))
        ins.append(add[0])
    return _pcall(body, name=name, grid=(S // tm, SSM_BD), in_specs=specs,
                  out_specs=pl.BlockSpec((tm, wide), lambda i, j: (i, j)),
                  out_shape=jax.ShapeDtypeStruct((S, SSM_BD * wide), F32),
                  compiler_params=_params(("parallel", "parallel")))(*ins)


class Win:
    def __init__(self, arr, width, blk):
        self.arr, self.width, self.blk = arr, width, blk


def _rowwise(fn, rows, vecs=(), *, n_red=0, tr=256, name):
    wins = [r if isinstance(r, Win) else Win(r, r.shape[1], 0) for r in rows]
    S = wins[0].arr.shape[0]
    tr = min(tr, S)
    tile_args = [jax.ShapeDtypeStruct((tr, w.width), w.arr.dtype) for w in wins]
    tile_args += [jax.ShapeDtypeStruct(v.shape, v.dtype) for v in vecs]
    outs = jax.eval_shape(fn, *tile_args)
    n_row = len(outs) - n_red
    specs = [pl.BlockSpec((tr, w.width), functools.partial(lambda i, b: (i, b), b=w.blk)) for w in wins]
    specs += [pl.BlockSpec(v.shape, functools.partial(lambda i, nd: (0,) * nd, nd=v.ndim)) for v in vecs]
    out_specs = [pl.BlockSpec((tr, o.shape[1]), lambda i: (i, 0)) for o in outs[:n_row]]
    out_specs += [pl.BlockSpec(o.shape, functools.partial(lambda i, nd: (0,) * nd, nd=len(o.shape))) for o in outs[n_row:]]
    out_shape = [jax.ShapeDtypeStruct((S, o.shape[1]), o.dtype) for o in outs[:n_row]]
    out_shape += [jax.ShapeDtypeStruct(o.shape, o.dtype) for o in outs[n_row:]]
    n_in = len(wins) + len(vecs)

    def body(*refs):
        res = fn(*[r[...] for r in refs[:n_in]])
        o_refs = refs[n_in:]
        for o_ref, r in zip(o_refs[:n_row], res[:n_row]):
            o_ref[...] = r.astype(o_ref.dtype)
        i = pl.program_id(0)
        for o_ref, r in zip(o_refs[n_row:], res[n_row:]):
            @pl.when(i == 0)
            def _(o_ref=o_ref, r=r):
                o_ref[...] = r

            @pl.when(i > 0)
            def _(o_ref=o_ref, r=r):
                o_ref[...] += r

    return _pcall(
        body, name=name, grid=(S // tr,), in_specs=specs, out_specs=out_specs, out_shape=out_shape,
        compiler_params=_params(("arbitrary",)),
    )(*[w.arr for w in wins], *vecs)


def _whole(fn, *arrays, name):
    outs = jax.eval_shape(fn, *arrays)
    n_in = len(arrays)

    def body(*refs):
        res = fn(*[r[...] for r in refs[:n_in]])
        for o_ref, r in zip(refs[n_in:], res):
            o_ref[...] = r

    vm = pl.BlockSpec(memory_space=pltpu.VMEM)
    return _pcall(body, name=name, in_specs=[vm] * n_in, out_specs=[vm] * len(outs),
                  out_shape=[jax.ShapeDtypeStruct(o.shape, o.dtype) for o in outs])(*arrays)


def _split3(x):
    hi = x.astype(BF16)
    r = x - hi.astype(F32)
    mid = r.astype(BF16)
    lo = (r - mid.astype(F32)).astype(BF16)
    return hi, mid, lo


def _dot3(x, onehot, dims):
    return sum(lax.dot_general(t, onehot, dims, preferred_element_type=F32) for t in _split3(x))


_NT = (((1,), (1,)), ((), ()))
_NN = (((1,), (0,)), ((), ()))
_TN = (((0,), (0,)), ((), ()))


def _colmap(x, *, inverse, name):
    R = x.shape[0] if inverse else x.shape[1]
    tr = 256
    per = W_IN_SLOT // LANE
    n_out = N_DEV * per if inverse else Z_W // LANE
    segs = [(p, q, n) for q, p, n in W_IN_SEGS] if inverse else list(W_IN_SEGS)

    def body(x_ref, o_ref):
        ia = lax.broadcasted_iota(jnp.int32, (LANE, LANE), 0)
        ib = lax.broadcasted_iota(jnp.int32, (LANE, LANE), 1)

        def src_block(i):
            if inverse:
                return x_ref[:, i * LANE:(i + 1) * LANE]
            return x_ref[i // per, :, (i % per) * LANE:(i % per + 1) * LANE]

        for jb in range(n_out):
            acc = None
            for s0, d0, n in segs:
                lo, hi = max(d0, jb * LANE), min(d0 + n, (jb + 1) * LANE)
                if lo >= hi:
                    continue
                delta = d0 - s0
                for i in range((lo - delta) // LANE, (hi - delta - 1) // LANE + 1):
                    shift = jb * LANE - i * LANE - delta
                    sel = ((ia - ib == shift) & (ib >= lo - jb * LANE) & (ib < hi - jb * LANE)).astype(BF16)
                    blk = src_block(i)
                    part = _dot3(blk, sel, _NN) if inverse else lax.dot_general(blk, sel, _NN, preferred_element_type=F32)
                    acc = part if acc is None else acc + part
            if acc is None:
                acc = jnp.zeros((tr, LANE), F32)
            if inverse:
                o_ref[jb // per, :, (jb % per) * LANE:(jb % per + 1) * LANE] = acc
            else:
                o_ref[:, jb * LANE:(jb + 1) * LANE] = acc.astype(BF16)

    slot_spec = pl.BlockSpec((N_DEV, tr, W_IN_SLOT), lambda i: (0, i, 0))
    flat_spec = pl.BlockSpec((tr, Z_W), lambda i: (i, 0))
    if inverse:
        return _pcall(body, name=name, grid=(R // tr,), in_specs=[flat_spec], out_specs=slot_spec,
                      out_shape=jax.ShapeDtypeStruct((N_DEV, R, W_IN_SLOT), F32), compiler_params=_params(("parallel",)))(x)
    return _pcall(body, name=name, grid=(R // tr,), in_specs=[slot_spec], out_specs=flat_spec,
                  out_shape=jax.ShapeDtypeStruct((R, Z_W), BF16), compiler_params=_params(("parallel",)))(x)


def _log_sigmoid(x):
    return jnp.minimum(x, 0.0) - jnp.log(1.0 + jnp.exp(-jnp.abs(x)))


def _cum_heads(f, dcum=None, *, name):
    H, S = f.shape
    tn = min(512, S)
    rev = dcum is not None

    def body(*refs):
        j = pl.program_id(0)
        s_idx = lax.broadcasted_iota(jnp.int32, (S, tn), 0)
        t_idx = lax.broadcasted_iota(jnp.int32, (S, tn), 1) + j * tn
        if not rev:
            f_ref, o_ref = refs
            tri = (s_idx <= t_idx).astype(BF16)
            o_ref[...] = _dot3(_log_sigmoid(f_ref[...]), tri, _NN)
        else:
            fj_ref, d_ref, o_ref = refs
            tri = (s_idx >= t_idx).astype(BF16)
            o_ref[...] = _dot3(d_ref[...], tri, _NN) * jax.nn.sigmoid(-fj_ref[...])

    full = pl.BlockSpec((H, S), lambda j: (0, 0))
    blk = pl.BlockSpec((H, tn), lambda j: (0, j))
    ins, specs = ([f], [full]) if not rev else ([f, dcum], [blk, full])
    return _pcall(body, name=name, grid=(S // tn,), in_specs=specs, out_specs=blk,
                  out_shape=jax.ShapeDtypeStruct((H, S), F32), compiler_params=_params(("arbitrary",)))(*ins)


def _rel_onehot(qi, transposed):
    shape = (N_REL, CA_BAND) if transposed else (CA_BAND, N_REL)
    kk = lax.broadcasted_iota(jnp.int32, shape, 1 if transposed else 0)
    rr = lax.broadcasted_iota(jnp.int32, shape, 0 if transposed else 1)
    idx = jnp.clip(CA_PAD + qi - kk, REL_MIN, REL_MAX) - REL_MIN
    return (idx == rr).astype(BF16)


def _relbias_expand(rb):
    def body(rb_ref, o_ref):
        o_ref[0] = _dot3(rb_ref[...], _rel_onehot(pl.program_id(0), True), _NN)

    return _pcall(body, name="relbias_expand", grid=(CHUNK,),
                  in_specs=[pl.BlockSpec((CA_HEADS, N_REL), lambda q: (0, 0))],
                  out_specs=pl.BlockSpec((1, CA_HEADS, CA_BAND), lambda q: (q, 0, 0)),
                  out_shape=jax.ShapeDtypeStruct((CHUNK, CA_HEADS, CA_BAND), F32),
                  compiler_params=_params(("arbitrary",)))(rb)


def _relbias_reduce(db):
    def body(db_ref, o_ref):
        q = pl.program_id(0)
        part = _dot3(db_ref[0], _rel_onehot(q, False), _NN)

        @pl.when(q == 0)
        def _():
            o_ref[...] = part

        @pl.when(q > 0)
        def _():
            o_ref[...] += part

    return _pcall(body, name="relbias_reduce", grid=(CHUNK,),
                  in_specs=[pl.BlockSpec((1, CA_HEADS, CA_BAND), lambda q: (q, 0, 0))],
                  out_specs=pl.BlockSpec((CA_HEADS, N_REL), lambda q: (0, 0)),
                  out_shape=jax.ShapeDtypeStruct((CA_HEADS, N_REL), F32),
                  compiler_params=_params(("arbitrary",)))(db)


def _attn_cfg(mode, S):
    if mode == "fox":
        return min(256, S), FOX_HEAD_DIM ** -0.5
    if mode == "chunk":
        return CHUNK, CA_HEAD_DIM ** -0.5
    return min(512, S), XA_HEAD_DIM ** -0.5


def _scores(mode, i, tq, scale, qb, kb, extra):
    s = lax.dot_general(qb, kb, _NT, preferred_element_type=F32) * scale
    nk = kb.shape[0]
    if mode == "fox":
        cq, ck = extra
        s = s + cq - ck
        row = lax.broadcasted_iota(jnp.int32, (tq, nk), 0) + i * tq
        col = lax.broadcasted_iota(jnp.int32, (tq, nk), 1)
        s = jnp.where(row >= col, s, NEG_INF)
    elif mode == "chunk":
        (bias,) = extra
        col = lax.broadcasted_iota(jnp.int32, (tq, nk), 1) + i * CHUNK
        s = jnp.where(col >= CA_PAD, s + bias, NEG_INF)
    return s


class _AttnPlan:
    def __init__(self, mode, q, k, pp):
        self.mode, self.pp = mode, pp
        self.D, self.hpb, self.bw = (XA_HEAD_DIM, 1, XA_HEAD_DIM) if mode == "xa" else (64, 2, LANE)
        self.S, self.Sk = q[0].shape[0], k[0].shape[0]
        self.H = (XA_HEADS if mode == "xa" else FOX_HEADS)
        self.W = self.H * self.D
        self.gw = self.bw * pp
        self.hpg = self.hpb * pp
        self.tq, self.scale = _attn_cfg(mode, self.S)
        self.grid = (self.W // self.gw, self.S // self.tq)

    def rows(self, win):
        off = win[1] // self.gw
        return pl.BlockSpec((self.tq, self.gw), lambda g, i: (i, off + g))

    def cols(self, win):
        off = win[1] // self.gw
        return pl.BlockSpec((win[0].shape[0], self.gw), lambda g, i: (0, off + g))

    def extras(self):
        if self.mode == "fox":
            return [pl.BlockSpec((self.hpg, self.tq, 1), lambda g, i: (g, i, 0)),
                    pl.BlockSpec((self.hpg, 1, self.Sk), lambda g, i: (g, 0, 0))]
        if self.mode == "chunk":
            return [pl.BlockSpec((self.hpg, CHUNK, CA_BAND), lambda g, i: (g, 0, 0))]
        return []

    def per_row(self):
        return pl.BlockSpec((self.hpg, self.tq, 1), lambda g, i: (g, i, 0))

    def lanes(self, p):
        return slice(p * self.bw, (p + 1) * self.bw)

    def keys(self, i, ref, p):
        if self.mode == "chunk":
            return ref[pl.ds(pl.multiple_of(i * CHUNK, CHUNK), CA_BAND), self.lanes(p)]
        return ref[:, self.lanes(p)]

    def head(self, x, hh):
        if self.hpb == 1:
            return x
        lane = lax.broadcasted_iota(jnp.int32, x.shape, 1)
        return jnp.where(lane // self.D == hh, x, jnp.zeros_like(x))


def _attn_fwd(mode, q, k, v, extra=(), *, name):
    pl_ = _AttnPlan(mode, q, k, pp=4)
    n_ex = len(extra)
    out = (None, 0)

    def body(*refs):
        q_ref, k_ref, v_ref = refs[:3]
        o_ref, lse_ref = refs[3 + n_ex:]
        i = pl.program_id(1)
        for p in range(pl_.pp):
            qp = q_ref[:, pl_.lanes(p)].astype(BF16)
            kp = pl_.keys(i, k_ref, p).astype(BF16)
            vp = pl_.keys(i, v_ref, p).astype(BF16)
            acc = None
            for hh in range(pl_.hpb):
                h = p * pl_.hpb + hh
                ex = [r[h] for r in refs[3:3 + n_ex]]
                s = _scores(mode, i, pl_.tq, pl_.scale, pl_.head(qp, hh), kp, ex)
                m = jnp.max(s, axis=1, keepdims=True)
                e = jnp.exp(s - m)
                l = jnp.sum(e, axis=1, keepdims=True)
                part = lax.dot_general((e / l).astype(BF16), pl_.head(vp, hh), _NN, preferred_element_type=F32)
                acc = part if acc is None else acc + part
                lse_ref[h] = m + jnp.log(l)
            o_ref[:, pl_.lanes(p)] = acc

    return _pcall(
        body, name=name, grid=pl_.grid, in_specs=[pl_.rows(q), pl_.cols(k), pl_.cols(v)] + pl_.extras(),
        out_specs=[pl_.rows(out), pl_.per_row()],
        out_shape=[jax.ShapeDtypeStruct((pl_.S, pl_.W), F32), jax.ShapeDtypeStruct((pl_.H, pl_.S, 1), F32)],
        compiler_params=_params(("parallel", "arbitrary")),
    )(q[0], k[0], v[0], *extra)


def _attn_bwd(mode, q, k, v, lse, do, extra=(), *, name):
    pl_ = _AttnPlan(mode, q, k, pp=2 if mode == "fox" else 4)
    H, S, Sk, W = pl_.H, pl_.S, pl_.Sk, pl_.W
    n_ex = len(extra)
    out = (None, 0)
    kv_out = pl.BlockSpec((Sk, pl_.gw), lambda g, i: (0, g))
    ex_specs = pl_.extras()
    out_specs = [pl_.rows(out), kv_out, kv_out]
    out_shape = [jax.ShapeDtypeStruct((S, W), F32), jax.ShapeDtypeStruct((Sk, W), F32),
                 jax.ShapeDtypeStruct((Sk, W), F32)]
    if mode == "fox":
        out_specs += [pl_.per_row(), ex_specs[1]]
        out_shape += [jax.ShapeDtypeStruct((H, S, 1), F32), jax.ShapeDtypeStruct((H, 1, Sk), F32)]
    elif mode == "chunk":
        out_specs += [ex_specs[0]]
        out_shape += [jax.ShapeDtypeStruct((H, CHUNK, CA_BAND), F32)]

    def body(*refs):
        q_ref, k_ref, v_ref, lse_ref, do_ref = refs[:5]
        dq_ref, dk_ref, dv_ref = refs[5 + n_ex:8 + n_ex]
        rest = refs[8 + n_ex:]
        i = pl.program_id(1)

        @pl.when(i == 0)
        def _():
            dk_ref[...] = jnp.zeros_like(dk_ref)
            dv_ref[...] = jnp.zeros_like(dv_ref)
            if mode == "fox":
                rest[1][...] = jnp.zeros_like(rest[1])
            elif mode == "chunk":
                rest[0][...] = jnp.zeros_like(rest[0])

        for p in range(pl_.pp):
            lanes = pl_.lanes(p)
            qp = q_ref[:, lanes].astype(BF16)
            kp = pl_.keys(i, k_ref, p).astype(BF16)
            vp = pl_.keys(i, v_ref, p).astype(BF16)
            dop = do_ref[:, lanes].astype(BF16)
            dq = dk_part = dv_part = None
            for hh in range(pl_.hpb):
                h = p * pl_.hpb + hh
                ex = [r[h] for r in refs[5:5 + n_ex]]
                qh, doh = pl_.head(qp, hh), pl_.head(dop, hh)
                s = _scores(mode, i, pl_.tq, pl_.scale, qh, kp, ex)
                pr = jnp.exp(s - lse_ref[h])
                dp = lax.dot_general(doh, vp, _NT, preferred_element_type=F32)
                ds = pr * (dp - jnp.sum(dp * pr, axis=1, keepdims=True))
                dsb = (ds * pl_.scale).astype(BF16)
                parts = (lax.dot_general(dsb, pl_.head(kp, hh), _NN, preferred_element_type=F32),
                         lax.dot_general(dsb, qh, _TN, preferred_element_type=F32),
                         lax.dot_general(pr.astype(BF16), doh, _TN, preferred_element_type=F32))
                dq, dk_part, dv_part = parts if dq is None else (dq + parts[0], dk_part + parts[1], dv_part + parts[2])
                if mode == "chunk":
                    rest[0][h] += ds
                if mode == "fox":
                    rest[0][h] = jnp.sum(ds, axis=1, keepdims=True)
                    rest[1][h] += -jnp.sum(ds, axis=0, keepdims=True)
            dq_ref[:, lanes] = dq
            if mode == "chunk":
                win = pl.ds(pl.multiple_of(i * CHUNK, CHUNK), CA_BAND)
                dk_ref[win, lanes] += dk_part
                dv_ref[win, lanes] += dv_part
            else:
                dk_ref[:, lanes] += dk_part
                dv_ref[:, lanes] += dv_part

    return _pcall(
        body, name=name, grid=pl_.grid,
        in_specs=[pl_.rows(q), pl_.cols(k), pl_.cols(v), pl_.per_row(), pl_.rows(do)] + ex_specs,
        out_specs=out_specs, out_shape=out_shape,
        compiler_params=_params(("parallel", "arbitrary")),
    )(q[0], k[0], v[0], lse, do[0], *extra)


def _scan(x, a, h=None, *, name):
    S = x.shape[0]
    CB = SCAN_CB
    rev = h is not None
    n_grp = S // 8

    def body(*refs):
        if rev:
            x_ref, a_ref, h_ref, o_ref, da_ref = refs
        else:
            x_ref, a_ref, o_ref = refs
        ar = a_ref[:, :CB]
        ai = -a_ref[:, CB:] if rev else a_ref[:, CB:]
        zero = jnp.zeros((1, CB), F32)

        def group(g, carry):
            base = pl.multiple_of((n_grp - 1 - g) * 8 if rev else g * 8, 8)
            for j in (range(7, -1, -1) if rev else range(8)):
                t = base + j
                if rev:
                    hr, hi, dar, dai = carry
                else:
                    hr, hi = carry
                xr = x_ref[pl.ds(t, 1), :CB]
                xi = x_ref[pl.ds(t, 1), CB:]
                hr, hi = ar * hr - ai * hi + xr, ar * hi + ai * hr + xi
                o_ref[pl.ds(t, 1), :CB] = hr
                o_ref[pl.ds(t, 1), CB:] = hi
                if rev:
                    tp = jnp.maximum(t - 1, 0)
                    live = (t > 0).astype(F32)
                    pr = h_ref[pl.ds(tp, 1), :CB] * live
                    pi = h_ref[pl.ds(tp, 1), CB:] * live
                    carry = (hr, hi, dar + hr * pr + hi * pi, dai + hi * pr - hr * pi)
                else:
                    carry = (hr, hi)
            return carry

        if rev:
            _, _, dar, dai = lax.fori_loop(0, n_grp, group, (zero, zero, zero, zero))
            da_ref[:, :CB] = dar
            da_ref[:, CB:] = dai
        else:
            lax.fori_loop(0, n_grp, group, (zero, zero))

    big = pl.BlockSpec((S, 2 * CB), lambda c: (0, c))
    vec = pl.BlockSpec((1, 2 * CB), lambda c: (0, c))
    n_blk = x.shape[1] // (2 * CB)
    if rev:
        return _pcall(body, name=name, grid=(n_blk,), in_specs=[big, vec, big], out_specs=[big, vec],
                      out_shape=[jax.ShapeDtypeStruct(x.shape, F32), jax.ShapeDtypeStruct(a.shape, F32)],
                      compiler_params=_params(("parallel",)))(x, a, h)
    return _pcall(body, name=name, grid=(n_blk,), in_specs=[big, vec], out_specs=big,
                  out_shape=jax.ShapeDtypeStruct(x.shape, F32), compiler_params=_params(("parallel",)))(x, a)


def _ssm_prep1(lr_, li, ldt):
    lr = jnp.minimum(lr_, -1e-4)
    dt = jnp.exp(ldt)
    mag = jnp.exp(lr * dt)
    ar = mag * jnp.cos(li * dt)
    ai = mag * jnp.sin(li * dt)
    den = lr * lr + li * li
    gr = ((ar - 1.0) * lr + ai * li) / den
    gi = (ai * lr - (ar - 1.0) * li) / den
    return ar, ai, gr, gi


def _ssm_prep2(gr, gi, br, bi):
    return gr * br - gi * bi, gr * bi + gi * br


def _vjp_of(fn, n_in):
    def bwd(*args):
        cts = args[n_in:]
        return jax.vjp(fn, *args[:n_in])[1](cts[0] if len(cts) == 1 else tuple(cts))
    return bwd


def _to_blocked(r, i):
    lead = r.shape[:-1]
    t = jnp.stack([r.reshape(lead + (N_SSM_CH // SCAN_CB, SCAN_CB)), i.reshape(lead + (N_SSM_CH // SCAN_CB, SCAN_CB))],
                  axis=-2)
    return t.reshape(lead + (2 * N_SSM_CH,))


def _from_blocked(m):
    lead = m.shape[:-1]
    t = m.reshape(lead + (N_SSM_CH // SCAN_CB, 2, SCAN_CB))
    return t[..., 0, :].reshape(lead + (N_SSM_CH,)), t[..., 1, :].reshape(lead + (N_SSM_CH,))


_GPB = SSM_GROUPS // SSM_BD
_GPS = SCAN_CB // SSM_STATE


def _bd_eye():
    return jnp.eye(_GPB, dtype=F32).reshape(_GPB, _GPB // _GPS, _GPS)


def _blockdiag(r, i):
    v = jnp.stack([r, i]).reshape(2, SSM_BD, _GPB, SSM_GROUP, SSM_STATE)
    return jnp.einsum("qjgcp,gsh->jgcsqhp", v, _bd_eye()).reshape(SSM_BD, SSM_BD_IN, SSM_BD_ST)


def _blockdiag_inv(m):
    d = m.reshape(SSM_BD, _GPB, SSM_GROUP, _GPB // _GPS, 2, _GPS, SSM_STATE)
    v = jnp.einsum("jgcsqhp,gsh->qjgcp", d, _bd_eye()).reshape(2, SSM_GROUPS, SSM_GROUP, SSM_STATE)
    return v[0], v[1]


def _shift_rows(x, n):
    S = x.shape[0]
    row = lax.broadcasted_iota(jnp.int32, x.shape, 0)
    if n > 0:
        return jnp.where(row >= n, pltpu.roll(x, n, 0), 0.0)
    return jnp.where(row < S + n, pltpu.roll(x, S + n, 0), 0.0)


def _bf(x):
    return x.astype(BF16).astype(F32)


def _conv_pre(a, w, b):
    ab, wb = _bf(a), _bf(w)
    return wb[2:3] * ab + wb[1:2] * _shift_rows(ab, 1) + wb[0:1] * _shift_rows(ab, 2) + b


def _ffn_mid(up, conv_w, conv_b, dh=None, *, name):
    S = up.shape[0]
    tn = LANE
    nb = D_FF_P // tn
    rev = dh is not None

    def body(*refs):
        if not rev:
            a_ref, g_ref, w_ref, b_ref, o_ref = refs
            o_ref[...] = jax.nn.gelu(_conv_pre(a_ref[...], w_ref[...], b_ref[...])) * g_ref[...]
            return
        a_ref, g_ref, w_ref, b_ref, dh_ref, dup_a_ref, dup_g_ref, dw_ref, db_ref = refs
        a, w, dh_ = a_ref[...], w_ref[...], dh_ref[...]
        pre = _conv_pre(a, w, b_ref[...])
        gl, gelu_vjp = jax.vjp(jax.nn.gelu, pre)
        dup_g_ref[...] = dh_ * gl
        (dpre,) = gelu_vjp(dh_ * g_ref[...])
        db_ref[...] = jnp.sum(dpre, axis=0, keepdims=True)
        dpb, ab, wb = _bf(dpre), _bf(a), _bf(w)
        dup_a_ref[...] = wb[2:3] * dpb + wb[1:2] * _shift_rows(dpb, -1) + wb[0:1] * _shift_rows(dpb, -2)
        dw_ref[2:3, :] = jnp.sum(dpb * ab, axis=0, keepdims=True)
        dw_ref[1:2, :] = jnp.sum(dpb * _shift_rows(ab, 1), axis=0, keepdims=True)
        dw_ref[0:1, :] = jnp.sum(dpb * _shift_rows(ab, 2), axis=0, keepdims=True)

    a_spec = pl.BlockSpec((S, tn), lambda j: (0, j))
    g_spec = pl.BlockSpec((S, tn), lambda j: (0, j + nb))
    w_spec = pl.BlockSpec((3, tn), lambda j: (0, j))
    b_spec = pl.BlockSpec((1, tn), lambda j: (0, j))
    if not rev:
        return _pcall(body, name=name, grid=(nb,), in_specs=[a_spec, g_spec, w_spec, b_spec], out_specs=a_spec,
                      out_shape=jax.ShapeDtypeStruct((S, D_FF_P), F32), compiler_params=_params(("parallel",)))(
                          up, up, conv_w, conv_b)
    return _pcall(body, name=name, grid=(nb,), in_specs=[a_spec, g_spec, w_spec, b_spec, a_spec],
                  out_specs=[a_spec, a_spec, w_spec, b_spec],
                  out_shape=[jax.ShapeDtypeStruct((S, D_FF_P), F32), jax.ShapeDtypeStruct((S, D_FF_P), F32),
                             jax.ShapeDtypeStruct((3, D_FF_P), F32), jax.ShapeDtypeStruct((1, D_FF_P), F32)],
                  compiler_params=_params(("parallel",)))(up, up, conv_w, conv_b, dh)


def _ff_pad(t):
    lead = t.shape[:-1]
    t = t.reshape(lead + (N_DEV, FF_HALF))
    return jnp.pad(t, [(0, 0)] * len(lead) + [(0, 0), (0, FF_HALF_P - FF_HALF)]).reshape(lead + (D_FF_P,))


def _ff_unpad(t):
    lead = t.shape[:-1]
    return t.reshape(lead + (N_DEV, FF_HALF_P))[..., :FF_HALF].reshape(lead + (D_FF,))


def _ln_fwd(x, h, g, b):
    r = DN_ALPHA * x + h
    mu = jnp.mean(r, axis=-1, keepdims=True)
    var = jnp.mean(jnp.square(r - mu), axis=-1, keepdims=True)
    return r, (r - mu) * lax.rsqrt(var + LN_EPS) * g + b


def _ln_bwd(r, dy, g):
    mu = jnp.mean(r, axis=-1, keepdims=True)
    var = jnp.mean(jnp.square(r - mu), axis=-1, keepdims=True)
    xhat = (r - mu) * lax.rsqrt(var + LN_EPS)
    dxh = dy * g
    dr = lax.rsqrt(var + LN_EPS) * (dxh - jnp.mean(dxh, axis=-1, keepdims=True)
                                    - xhat * jnp.mean(dxh * xhat, axis=-1, keepdims=True))
    return dr, jnp.sum(dy * xhat, axis=0, keepdims=True), jnp.sum(dy, axis=0, keepdims=True)


def _merge(gf, gs, gc, ya, yb2, yc):
    yb = yb2[:, :D_MODEL] * jax.nn.sigmoid(yb2[:, D_MODEL:])
    return jax.nn.sigmoid(gf) * ya + jax.nn.sigmoid(gs) * yb + jax.nn.sigmoid(gc) * yc


def _s5_tail(hc, su, d):
    return jax.nn.gelu(hc + d * su)


def _s5_tail_bwd(hc, su, dgel, d):
    _, vjp = jax.vjp(jax.nn.gelu, hc + d * su)
    (dy,) = vjp(dgel)
    return dy, d * dy, jnp.sum(dy * su, axis=0, keepdims=True)


def _loss_rows(y, tgt):
    err = y - tgt
    return err * (1.0 / D_MODEL), jnp.sum(0.5 * jnp.square(err), axis=0, keepdims=True) * (1.0 / D_MODEL)


def _peer(k):
    x, y, c = lax.axis_index("x"), lax.axis_index("y"), lax.axis_index("c")
    return (x ^ ((k >> 2) & 1), y ^ ((k >> 1) & 1), c ^ (k & 1))


def _my_slot():
    return 4 * lax.axis_index("x") + 2 * lax.axis_index("y") + lax.axis_index("c")


def _peer_slot(k):
    px, py, pc = _peer(k)
    return 4 * px + 2 * py + pc


N_CHIP = N_DEV // 2
OTHER_CHIPS = (2, 4, 6)


def _chip_of(dev):
    return 2 * dev[0] + dev[1]


def _remote(src, dst, send, recv, dev):
    return pltpu.make_async_remote_copy(src_ref=src, dst_ref=dst, send_sem=send, recv_sem=recv, device_id=dev,
                                        device_id_type=pl.DeviceIdType.MESH)


def _all_gather(shards, *, name, host=None, then=None):
    return _exchange(shards, *_all_gather_parts(shards), name=name, host=host, then=then)


def _exchange(ins, out_shapes, sem_shapes, start, finish, *, name, host=None, then=None):
    if host is not None:
        rider = _Rider(host, ins, out_shapes, sem_shapes, start, finish, then)
        _RIDERS.append(rider)
        return rider
    n = len(ins)

    def body(*refs):
        start(refs[:n], refs[n:2 * n], refs[2 * n:])
        finish(refs[:n], refs[n:2 * n], refs[2 * n:])

    hbm = pl.BlockSpec(memory_space=pl.ANY)
    return _pcall(body, name=name, in_specs=[hbm] * n, out_specs=[hbm] * n, out_shape=list(out_shapes),
                  scratch_shapes=list(sem_shapes))(*ins)


def _all_gather_parts(shards):
    n = len(shards)

    def first_copies(ins, outs, sems):
        send, recv, _ = sems
        return [_remote(ins[t], outs[t].at[_my_slot()], send.at[t, k - 1], recv.at[t, k - 1], _peer(k))
                for k in (1,) + OTHER_CHIPS for t in range(n)]

    def local_copies(ins, outs, sems):
        return [pltpu.make_async_copy(ins[t], outs[t].at[_my_slot()], sems[2].at[t]) for t in range(n)]

    def start(ins, outs, sems):
        for cp in local_copies(ins, outs, sems) + first_copies(ins, outs, sems):
            cp.start()

    def finish(ins, outs, sems):
        send, recv, _ = sems
        sibling = _peer(1)
        passed = []
        for k in OTHER_CHIPS:
            for t in range(n):
                slot = outs[t].at[_peer_slot(k)]
                _remote(ins[t], slot, send.at[t, k - 1], recv.at[t, k - 1], _peer(k)).wait_recv()
                cp = _remote(slot, slot, send.at[t, k], recv.at[t, k], sibling)
                cp.start()
                passed.append(cp)
        for t in range(n):
            _remote(ins[t], outs[t].at[_peer_slot(1)], send.at[t, 0], recv.at[t, 0], sibling).wait_recv()
            for k in OTHER_CHIPS:
                _remote(ins[t], outs[t].at[_peer_slot(k + 1)], send.at[t, k], recv.at[t, k], sibling).wait_recv()
        for cp in first_copies(ins, outs, sems) + passed:
            cp.wait_send()
        for lc in local_copies(ins, outs, sems):
            lc.wait()

    out_shapes = [jax.ShapeDtypeStruct((N_DEV,) + s.shape, s.dtype) for s in shards]
    sem_shapes = [pltpu.SemaphoreType.DMA((n, N_DEV - 1)), pltpu.SemaphoreType.DMA((n, N_DEV - 1)),
                  pltpu.SemaphoreType.DMA((n,))]
    return out_shapes, sem_shapes, start, finish


def _sibling_swap(grads, *, name, host=None, then=None):
    n = len(grads)

    def copies(ins, outs, sems):
        c = lax.axis_index("c")
        return [_remote(ins[t].at[:, 1 - c], outs[t], sems[0].at[t], sems[1].at[t], _peer(1)) for t in range(n)]

    def start(ins, outs, sems):
        for cp in copies(ins, outs, sems):
            cp.start()

    def finish(ins, outs, sems):
        for cp in copies(ins, outs, sems):
            cp.wait()

    out_shapes = [jax.ShapeDtypeStruct((N_CHIP,) + g.shape[2:], g.dtype) for g in grads]
    sem_shapes = [pltpu.SemaphoreType.DMA((n,)), pltpu.SemaphoreType.DMA((n,))]
    return _exchange(grads, out_shapes, sem_shapes, start, finish, name=name, host=host, then=then)


def _pair_add(g, p, out_dtype, *, name):
    _, _, R, C = g.shape
    tr = _pick(R, (512, 256, 128, 64, 32, 16, 8))

    def body(c_ref, g_ref, p_ref, o_ref):
        o_ref[...] = (g_ref[...] + p_ref[...]).astype(out_dtype)

    grid_spec = pltpu.PrefetchScalarGridSpec(
        num_scalar_prefetch=1, grid=(N_CHIP, R // tr),
        in_specs=[pl.BlockSpec((None, None, tr, C), lambda j, i, c_ref: (j, c_ref[0], i, 0)),
                  pl.BlockSpec((None, tr, C), lambda j, i, c_ref: (j, i, 0))],
        out_specs=pl.BlockSpec((None, tr, C), lambda j, i, c_ref: (j, i, 0)))
    core = lax.axis_index("c").astype(jnp.int32).reshape(1)
    return _pcall(body, name=name, grid_spec=grid_spec, out_shape=jax.ShapeDtypeStruct(p.shape, out_dtype),
                  compiler_params=_params(("parallel", "parallel")))(core, g, p)


def _chip_exchange(sums, *, name, host=None):
    n = len(sums)

    def copies(ins, outs, sems, dst_is_mine):
        send, recv, _ = sems
        mine = 2 * lax.axis_index("x") + lax.axis_index("y")
        out = []
        for k in OTHER_CHIPS:
            theirs = _chip_of(_peer(k))
            for t in range(n):
                out.append(_remote(ins[t].at[theirs], outs[t].at[mine if dst_is_mine else theirs],
                                   send.at[t, k // 2 - 1], recv.at[t, k // 2 - 1], _peer(k)))
        return out

    def local_copies(ins, outs, sems):
        mine = 2 * lax.axis_index("x") + lax.axis_index("y")
        return [pltpu.make_async_copy(ins[t].at[mine], outs[t].at[mine], sems[2].at[t]) for t in range(n)]

    def start(ins, outs, sems):
        for cp in local_copies(ins, outs, sems) + copies(ins, outs, sems, True):
            cp.start()

    def finish(ins, outs, sems):
        for cp in copies(ins, outs, sems, False) + local_copies(ins, outs, sems):
            cp.wait()

    out_shapes = [jax.ShapeDtypeStruct(s.shape, s.dtype) for s in sums]
    sem_shapes = [pltpu.SemaphoreType.DMA((n, N_CHIP - 1)), pltpu.SemaphoreType.DMA((n, N_CHIP - 1)),
                  pltpu.SemaphoreType.DMA((n,))]
    return _exchange(sums, out_shapes, sem_shapes, start, finish, name=name, host=host)


class _GradReduce:
    def __init__(self, grads, wire_dtypes, hosts=None):
        pairs = [g.reshape((N_CHIP, 2) + g.shape[1:]) for g in grads]
        self.n, self.riders, self.recv = len(grads), [], None

        def after_swap(partner):
            sums = [_pair_add(g, p, dt, name="grad_pair_add") for g, p, dt in zip(pairs, partner, wire_dtypes)]
            if hosts is None:
                self.recv = _chip_exchange(sums, name="grad_chip_exchange")
            else:
                self.riders = [(idx, _chip_exchange([sums[i] for i in idx], name="grad_chip_exchange", host=h))
                               for h, idx in hosts[1]]

        if hosts is None:
            after_swap(_sibling_swap(pairs, name="grad_sibling_swap"))
        else:
            _sibling_swap(pairs, name="grad_sibling_swap", host=hosts[0], then=after_swap)

    def result(self):
        if self.recv is None:
            self.recv = [None] * self.n
            for idx, rider in self.riders:
                assert rider.results is not None, rider.host
                for i, r in zip(idx, rider.results):
                    self.recv[i] = r
        return self.recv


def _adamw(recv, w, m, v, layer=None, into=None, *, name):
    n_slots, R, C = recv.shape
    tr = _pick(R, (256, 128, 64, 32, 16, 8))

    def body(r_ref, w_ref, m_ref, v_ref, *rest):
        g_ref, d_ref, nm_ref, nv_ref = rest[-4:]
        g = r_ref[0].astype(F32)
        for s in range(1, n_slots):
            g = g + r_ref[s].astype(F32)
        m_new = ADAM_B1 * m_ref[...] + (1.0 - ADAM_B1) * g
        v_new = ADAM_B2 * v_ref[...] + (1.0 - ADAM_B2) * jnp.square(g)
        m_hat = m_new / (1.0 - ADAM_B1 ** ADAM_STEP)
        v_hat = v_new / (1.0 - ADAM_B2 ** ADAM_STEP)
        g_ref[...] = g
        d_ref[...] = -ADAM_LR * (m_hat / (jnp.sqrt(v_hat) + ADAM_EPS) + ADAM_WD * w_ref[...])
        nm_ref[...] = m_new
        nv_ref[...] = v_new

    row = pl.BlockSpec((tr, C), lambda i: (i, 0))
    state = row if layer is None else pl.BlockSpec((None, tr, C), lambda i: (layer, i, 0))
    in_specs = [pl.BlockSpec((n_slots, tr, C), lambda i: (0, i, 0)), state, state, state]
    if into is None:
        return _pcall(body, name=name, grid=(R // tr,), in_specs=in_specs, out_specs=[row] * 4,
                      out_shape=[jax.ShapeDtypeStruct((R, C), F32)] * 4, compiler_params=_params(("parallel",)),
                      )(recv, w, m, v)
    return _pcall(body, name=name, grid=(R // tr,), in_specs=in_specs + [pl.BlockSpec(memory_space=pl.ANY)] * 4,
                  out_specs=[state] * 4, out_shape=[jax.ShapeDtypeStruct((DEPTH, R, C), F32)] * 4,
                  input_output_aliases={4 + j: j for j in range(4)}, compiler_params=_params(("parallel",)),
                  )(recv, w, m, v, *into)


def _flat_pad(parts, rows):
    flat = jnp.concatenate([p.reshape(-1) for p in parts])
    return jnp.pad(flat, (0, rows * LANES - flat.shape[0])).reshape(rows, LANES)


def _small_shard_shape(n):
    return SMALL[n][:-1] + (SMALL[n][-1] // N_DEV,)


def _unpack_small(gathered):
    out, off = {}, 0
    flat = gathered.reshape(N_DEV, -1)
    for n in SMALL:
        r, c = _small_shard_shape(n)
        out[n] = flat[:, off:off + r * c].reshape(N_DEV, r, c).transpose(1, 0, 2).reshape(r, N_DEV * c)
        off += r * c
    return out


def _pack_state(state, prefix):
    flat = jnp.concatenate([state[prefix + n].reshape(DEPTH, -1) for n in (*SMALL, *REPL)], axis=1)
    return jnp.pad(flat, ((0, 0), (0, PACK_ROWS * LANES - flat.shape[1]))).reshape(DEPTH * PACK_ROWS, LANES)


def _pack_small_grads(grads):
    cols = []
    for n in SMALL:
        r, c = _small_shard_shape(n)
        cols.append(grads[n].reshape(r, N_DEV, c).transpose(1, 0, 2).reshape(N_DEV, r * c))
    cols += [jnp.broadcast_to(grads[n].reshape(1, -1), (N_DEV, grads[n].size)) for n in REPL]
    flat = jnp.concatenate(cols, axis=1)
    return jnp.pad(flat, ((0, 0), (0, PACK_ROWS * LANES - flat.shape[1]))).reshape(N_DEV, PACK_ROWS, LANES)


def _unpack_state(packed):
    out, off = {}, 0
    flat = packed.reshape(DEPTH, -1)
    for n, shape in [(n, _small_shard_shape(n)) for n in SMALL] + list(REPL.items()):
        sz = math.prod(shape)
        out[n] = flat[:, off:off + sz].reshape((DEPTH,) + shape)
        off += sz
    return out


def _pad_b_in(b):
    parts, pos = [], 0
    for src, width, dst in Z_PIECES:
        parts += [jnp.zeros((b.shape[0], dst - pos), b.dtype), b[:, src:src + width]]
        pos = dst + width
    return jnp.concatenate(parts + [jnp.zeros((b.shape[0], Z_W - pos), b.dtype)], axis=1)


def _unpad_b_in(bp):
    return jnp.concatenate([bp[:, dst:dst + width] for _, width, dst in Z_PIECES], axis=1)


def _up_shard_pad(t):
    lead = t.shape[:-1]
    t = jnp.pad(t.reshape(lead + (2, FF_HALF)), [(0, 0)] * len(lead) + [(0, 0), (0, FF_HALF_P - FF_HALF)])
    return t.reshape(lead + (2 * FF_HALF_P,))


def _up_shard_unpad(t):
    lead = t.shape[:-1]
    return t.reshape(lead + (2, FF_HALF_P))[..., :FF_HALF].reshape(lead + (2 * FF_HALF,))


def _pad_shard(n, t):
    lead = [(0, 0)] * (t.ndim - 2)
    if n == "w_in":
        return jnp.pad(t, lead + [(0, 0), (0, W_IN_SLOT - W_IN_SHARD)])
    if n == "ffn_w_up":
        return _up_shard_pad(t)
    if n == "ffn_w_down":
        return jnp.pad(t, lead + [(0, FF_HALF_P - FF_HALF), (0, 0)])
    return t


def _unpad_shard(n, t):
    if n == "w_in":
        return t[..., :W_IN_SHARD]
    if n == "ffn_w_up":
        return _up_shard_unpad(t)
    if n == "ffn_w_down":
        return t[..., :FF_HALF, :]
    return t


GATHER_HOSTS = (("fox_fwd", (0,)), ("chunk_fwd", (8, 9)), ("mm_up", (1, 2, 3, 4, 5, 6, 7, 10)))
REDUCE_HOSTS = ("mm_up_dw", (("fox_bwd", (0,)), ("chunk_bwd", (8, 9)), ("s5_scan_bwd", (1, 2, 3, 4, 5, 6, 7, 10))))


def _layer_shards(state, layer):
    shards = [_pad_shard(n, state[n][layer].astype(BF16)) for n in BIG]
    return shards + [_flat_pad([state[n][layer] for n in SMALL], SMALL_ROWS)]


def _gathered_weights(gathered):
    *big, small = gathered
    W = dict(zip(BIG, big))
    for n in ("w_o", "xa_wq", "xa_wo", "ffn_w_down"):
        W[n] = W[n].reshape(-1, D_MODEL)
    W["w_in_p"] = _colmap(W.pop("w_in"), inverse=False, name="w_in_colmap")
    W.update(_unpack_small(small))
    return W


def _ssm_params(p):
    prep1_in = (p["ssm_lambda_re"], p["ssm_lambda_im"], p["ssm_log_dt"][:, None])
    ar, ai, gr, gi = _whole(_ssm_prep1, *prep1_in, name="ssm_prep1")
    to_cn = lambda b: b.transpose(2, 0, 1).reshape(SSM_GROUP, N_SSM_CH)
    prep2_in = (gr.reshape(1, N_SSM_CH), gi.reshape(1, N_SSM_CH), to_cn(p["ssm_b_re"]), to_cn(p["ssm_b_im"]))
    bbr, bbi = _whole(_ssm_prep2, *prep2_in, name="ssm_prep2")
    to_gcp = lambda t: t.reshape(SSM_GROUP, SSM_GROUPS, SSM_STATE).transpose(1, 0, 2)
    bb = _blockdiag(to_gcp(bbr), to_gcp(bbi))
    cct = _blockdiag(p["ssm_c_re"], -p["ssm_c_im"])
    a_vec = _to_blocked(ar.reshape(1, N_SSM_CH), ai.reshape(1, N_SSM_CH))
    return dict(bb=bb, cct=cct, a_vec=a_vec, prep1_in=prep1_in, prep2_in=prep2_in)


def _layer_fwd(x, mem, p, W):
    sp = _ssm_params(p)
    z = _mm(x, W["w_in_p"], bias=p["b_in_p"], name="mm_in")
    f_t = z[:, Z_FF:Z_FF + FOX_HEADS].T
    cum = _cum_heads(f_t, name="fox_cum")
    ya_pre, lse_a = _attn_fwd("fox", (z, Z_FQ), (z, Z_FK), (z, Z_FV), (cum[:, :, None], cum[:, None, :]),
                              name="fox_fwd")
    ya = _mm(ya_pre, W["w_fox_o"], b_slots=True, name="mm_fox_o")
    x_ri = _mm_bd("in", z, sp["bb"], a_off=Z_SU, name="mm_s5_in")
    h_ri = _scan(x_ri, sp["a_vec"], name="s5_scan")
    hc = _mm_bd("out", h_ri, sp["cct"], name="mm_s5_out")
    d_row = p["ssm_d"][None, :]
    (gel,) = _rowwise(lambda a, b, c: (_s5_tail(a, b, c),), [hc, Win(z, 512, Z_SU // 512)], [d_row], name="s5_tail")
    yb2 = _mm(gel, W["w_ssm_glu"], b_slots=True, name="mm_glu")
    bias = _relbias_expand(W["ca_rel_bias"]).transpose(1, 0, 2)
    kv_band = jnp.pad(z[:, Z_CK:Z_CK + 2 * CA_WIDTH].astype(BF16), ((CA_PAD, 0), (0, 0)))
    yc_pre, lse_c = _attn_fwd("chunk", (z, Z_CQ), (kv_band, 0), (kv_band, CA_WIDTH), (bias,), name="chunk_fwd")
    yc = _mm(yc_pre, W["w_ca_o"], b_slots=True, name="mm_ca_o")
    gates = [Win(z, 1024, Z_GF // 1024), Win(z, 1024, Z_GS // 1024), Win(z, 1024, Z_GC // 1024)]
    (merged,) = _rowwise(lambda *a: (_merge(*a),), gates + [ya, yb2, yc], name="merge")
    h1 = _mm(merged, W["w_o"], name="mm_o")
    ln_g, ln_b = W["ln_g"], W["ln_b"]
    r1, x1 = _rowwise(_ln_fwd, [x, h1], [ln_g[0:1], ln_b[0:1]], name="ln_fwd")
    q = _mm(x1, W["xa_wq"], name="mm_xq")
    kv = _mm(mem, W["xa_wkv"], b_slots=True, name="mm_xkv")
    o, lse_x = _attn_fwd("xa", (q, 0), (kv, 0), (kv, D_MODEL), name="xa_fwd")
    h2 = _mm(o, W["xa_wo"], name="mm_xo")
    r2, x2 = _rowwise(_ln_fwd, [x1, h2], [ln_g[1:2], ln_b[1:2]], name="ln_fwd")
    up = _mm(x2, W["ffn_w_up"], b_slots=True, name="mm_up")
    hmid = _ffn_mid(up, _ff_pad(W["ffn_conv_w"]), _ff_pad(p["ffn_conv_b"][None, :]), name="ffn_mid")
    h3 = _mm(hmid, W["ffn_w_down"], name="mm_down")
    r3, x3 = _rowwise(_ln_fwd, [x2, h3], [ln_g[2:3], ln_b[2:3]], name="ln_fwd")
    res = dict(x=x, z=z, cum=cum, lse_a=lse_a, ya_pre=ya_pre, ya=ya, h_ri=h_ri, hc=hc, gel=gel, yb2=yb2, lse_c=lse_c,
               yc_pre=yc_pre, yc=yc, merged=merged, r1=r1, x1=x1, q=q, kv=kv, o=o, lse_x=lse_x, r2=r2, x2=x2, up=up,
               hmid=hmid, r3=r3, bias=bias, sp=sp, kv_band=kv_band, W=W)
    return x3, res


def _layer_bwd(dx3, mem, p, res):
    W = res["W"]
    x, z = res["x"], res["z"]
    sp = res["sp"]
    ln_g = W["ln_g"]
    big, small = {}, {}
    slots = lambda t: t.reshape(N_DEV, -1, D_MODEL)
    dr3, dg2, db2 = _rowwise(_ln_bwd, [res["r3"], dx3], [ln_g[2:3]], n_red=2, name="ln_bwd")
    dhmid = _mm(dr3, W["ffn_w_down"], tb=True, name="mm_down_dx")
    big["ffn_w_down"] = slots(_mm(res["hmid"], dr3, ta=True, name="mm_down_dw"))
    conv_w_p, conv_b_p = _ff_pad(W["ffn_conv_w"]), _ff_pad(p["ffn_conv_b"][None, :])
    dup_a, dup_g, dcw, dcb = _ffn_mid(res["up"], conv_w_p, conv_b_p, dhmid, name="ffn_mid_bwd")
    dup = jnp.concatenate([dup_a, dup_g], axis=1)
    small["ffn_conv_w"], small["ffn_conv_b"] = _ff_unpad(dcw), _ff_unpad(dcb)[0]
    dx2 = _mm(dup, W["ffn_w_up"], tb=True, b_slots=True, add=(dr3, DN_ALPHA), name="mm_up_dx")
    big["ffn_w_up"] = _mm(res["x2"], dup, ta=True, out_slots=2 * FF_HALF_P, name="mm_up_dw")
    dr2, dg1, db1 = _rowwise(_ln_bwd, [res["r2"], dx2], [ln_g[1:2]], n_red=2, name="ln_bwd")
    do = _mm(dr2, W["xa_wo"], tb=True, name="mm_xo_dx")
    big["xa_wo"] = slots(_mm(res["o"], dr2, ta=True, name="mm_xo_dw"))
    kv = res["kv"]
    dq, dk, dv = _attn_bwd("xa", (res["q"], 0), (kv, 0), (kv, D_MODEL), res["lse_x"], (do, 0), name="xa_bwd")
    dkv = jnp.concatenate([dk, dv], axis=1)
    dx1 = _mm(dq, W["xa_wq"], tb=True, add=(dr2, DN_ALPHA), name="mm_xq_dx")
    big["xa_wq"] = slots(_mm(res["x1"], dq, ta=True, name="mm_xq_dw"))
    big["xa_wkv"] = _mm(mem, dkv, ta=True, out_slots=256, name="mm_xkv_dw")
    dr1, dg0, db0 = _rowwise(_ln_bwd, [res["r1"], dx1], [ln_g[0:1]], n_red=2, name="ln_bwd")
    small["ln_g"] = jnp.concatenate([dg0, dg1, dg2], axis=0)
    small["ln_b"] = jnp.concatenate([db0, db1, db2], axis=0)
    dmerged = _mm(dr1, W["w_o"], tb=True, name="mm_o_dx")
    big["w_o"] = slots(_mm(res["merged"], dr1, ta=True, name="mm_o_dw"))
    gates = [Win(z, 1024, Z_GF // 1024), Win(z, 1024, Z_GS // 1024), Win(z, 1024, Z_GC // 1024)]
    dgf, dgs, dgc, dya, dyb2, dyc = _rowwise(_vjp_of(_merge, 6), gates + [res["ya"], res["yb2"], res["yc"], dmerged],
                                             name="merge_bwd")
    dya_pre = _mm(dya, W["w_fox_o"], tb=True, b_slots=True, name="mm_fox_o_dx")
    big["w_fox_o"] = _mm(res["ya_pre"], dya, ta=True, out_slots=128, name="mm_fox_o_dw")
    cum = res["cum"]
    dfq, dfk, dfv, dcq, dck = _attn_bwd("fox", (z, Z_FQ), (z, Z_FK), (z, Z_FV), res["lse_a"], (dya_pre, 0),
                                        (cum[:, :, None], cum[:, None, :]), name="fox_bwd")
    f_t = z[:, Z_FF:Z_FF + FOX_HEADS].T
    dff = _cum_heads(f_t, dcq[:, :, 0] + dck[:, 0, :], name="fox_cum_bwd")
    dgel = _mm(dyb2, W["w_ssm_glu"], tb=True, b_slots=True, name="mm_glu_dx")
    big["w_ssm_glu"] = _mm(res["gel"], dyb2, ta=True, out_slots=256, name="mm_glu_dw")
    d_row = p["ssm_d"][None, :]
    su_win = Win(z, 512, Z_SU // 512)
    dy, dsu1, dd = _rowwise(_s5_tail_bwd, [res["hc"], su_win, dgel], [d_row], n_red=1, name="s5_tail_bwd")
    small["ssm_d"] = dd[0]
    dh_ri = _mm_bd("in", dy, sp["cct"], name="mm_s5_out_dx")
    dcct = _mm_bd("dw", dy, res["h_ri"], name="mm_s5_out_dw")
    dx_ri, da_vec = _scan(dh_ri, sp["a_vec"], res["h_ri"], name="s5_scan_bwd")
    dsu = _mm_bd("out", dx_ri, sp["bb"], add=(dsu1, 1.0), name="mm_s5_in_dx")
    dbb = _mm_bd("dw", z, dx_ri, a_off=Z_SU, name="mm_s5_in_dw")
    dcr, dci = _blockdiag_inv(dcct)
    small["ssm_c_re"], small["ssm_c_im"] = dcr, -dci
    dbbr, dbbi = _blockdiag_inv(dbb)
    to_cn = lambda t: t.transpose(1, 0, 2).reshape(SSM_GROUP, N_SSM_CH)
    dgr, dgi, dbr, dbi = _whole(_vjp_of(_ssm_prep2, 4), *sp["prep2_in"], to_cn(dbbr), to_cn(dbbi), name="ssm_prep2_bwd")
    from_cn = lambda t: t.reshape(SSM_GROUP, SSM_GROUPS, SSM_STATE).transpose(1, 2, 0)
    small["ssm_b_re"], small["ssm_b_im"] = from_cn(dbr), from_cn(dbi)
    dar, dai = _from_blocked(da_vec)
    sq = lambda t: t.reshape(SSM_GROUPS, SSM_STATE)
    dlr, dli, dldt = _whole(_vjp_of(_ssm_prep1, 3), *sp["prep1_in"], sq(dar), sq(dai), sq(dgr), sq(dgi),
                            name="ssm_prep1_bwd")
    small["ssm_lambda_re"], small["ssm_lambda_im"], small["ssm_log_dt"] = dlr, dli, dldt[:, 0]
    dyc_pre = _mm(dyc, W["w_ca_o"], tb=True, b_slots=True, name="mm_ca_o_dx")
    big["w_ca_o"] = _mm(res["yc_pre"], dyc, ta=True, out_slots=128, name="mm_ca_o_dw")
    bias = res["bias"]
    kv_band = res["kv_band"]
    dcq_, dck_band, dcv_band, dbias = _attn_bwd("chunk", (z, Z_CQ), (kv_band, 0), (kv_band, CA_WIDTH), res["lse_c"],
                                                (dyc_pre, 0), (bias,), name="chunk_bwd")
    small["ca_rel_bias"] = _relbias_reduce(dbias.transpose(1, 0, 2))
    dff_p = jnp.pad(dff.T, ((0, 0), (0, 512 - FOX_HEADS)))
    dz = jnp.concatenate([dfq, dfk, dfv, dff_p, dsu, dcq_, dck_band[CA_PAD:], dcv_band[CA_PAD:], dgf, dgs, dgc],
                         axis=1)
    dx = _mm(dz, W["w_in_p"], tb=True, add=(dr1, DN_ALPHA), name="mm_in_dx")
    big["w_in"] = _colmap(_mm(x, dz, ta=True, name="mm_in_dw"), inverse=True, name="w_in_colmap_inv")
    (db_in_p,) = _rowwise(lambda t: (jnp.sum(t, axis=0, keepdims=True),), [dz], n_red=1, name="colsum")
    small["b_in"] = _unpad_b_in(db_in_p)[0]
    return dx, big, small


def kernel(x, mem, w_in, b_in, ssm_lambda_re, ssm_lambda_im, ssm_log_dt, ssm_b_re, ssm_b_im, ssm_c_re, ssm_c_im, ssm_d, ca_rel_bias, w_fox_o, w_ssm_glu, w_ca_o, w_o, xa_wq, xa_wkv, xa_wo, ffn_w_up, ffn_conv_w, ffn_conv_b, ffn_w_down, ln_g, ln_b, loss_target, m_w_in, m_b_in, m_ssm_lambda_re, m_ssm_lambda_im, m_ssm_log_dt, m_ssm_b_re, m_ssm_b_im, m_ssm_c_re, m_ssm_c_im, m_ssm_d, m_ca_rel_bias, m_w_fox_o, m_w_ssm_glu, m_w_ca_o, m_w_o, m_xa_wq, m_xa_wkv, m_xa_wo, m_ffn_w_up, m_ffn_conv_w, m_ffn_conv_b, m_ffn_w_down, m_ln_g, m_ln_b, v_w_in, v_b_in, v_ssm_lambda_re, v_ssm_lambda_im, v_ssm_log_dt, v_ssm_b_re, v_ssm_b_im, v_ssm_c_re, v_ssm_c_im, v_ssm_d, v_ca_rel_bias, v_w_fox_o, v_w_ssm_glu, v_w_ca_o, v_w_o, v_xa_wq, v_xa_wkv, v_xa_wo, v_ffn_w_up, v_ffn_conv_w, v_ffn_conv_b, v_ffn_w_down, v_ln_g, v_ln_b):
    given = dict(locals())
    state = {pre + n: given[pre + n] for n in WEIGHTS for pre in ("", "m_", "v_")}
    mem0 = mem[0]
    b_in_p = _pad_b_in(b_in)
    layer_params = [{**{n: state[n][l] for n in REPL}, "b_in_p": b_in_p[l:l + 1]} for l in range(DEPTH)]

    _RIDERS.clear()
    h, residuals = x[0], []
    gathered = _all_gather(_layer_shards(state, 0), name="all_gather_weights")
    for l in range(DEPTH):
        riders = []
        if l + 1 < DEPTH:
            shards = _layer_shards(state, l + 1)
            riders = [(idx, _all_gather([shards[i] for i in idx], name="all_gather_weights", host=host))
                      for host, idx in GATHER_HOSTS]
        h, res = _layer_fwd(h, mem0, layer_params[l], _gathered_weights(gathered))
        residuals.append(res)
        gathered = [None] * (len(BIG) + 1)
        for idx, rider in riders:
            for i, r in zip(idx, rider.results):
                gathered[i] = r
    dh, loss_cols = _rowwise(_loss_rows, [h, loss_target[0]], n_red=1, name="loss")
    loss = lax.psum(jnp.sum(loss_cols), ("x", "y", "c"))

    outs = [None] * DEPTH
    big_out = {n: None for n in BIG}
    padded = {pre + n: _pad_shard(n, state[pre + n]) for n in BIG for pre in ("", "m_", "v_")}
    wire = [BF16] * len(BIG) + [F32]
    pending = None
    for l in reversed(range(-1, DEPTH)):
        if l >= 0:
            dh, big, small = _layer_bwd(dh, mem0, layer_params[l], residuals[l])
            reduce = _GradReduce([big[n] for n in BIG] + [_pack_small_grads(small)], wire,
                                 hosts=REDUCE_HOSTS if l > 0 else None)
        done, pending = pending, (l, reduce) if l >= 0 else None
        if done is None:
            continue
        l_done, reduce_done = done
        *recv_big, recv_small = reduce_done.result()
        for n, recv in zip(BIG, recv_big):
            if big_out[n] is None:
                big_out[n] = [lax.empty((DEPTH,) + recv.shape[1:], F32) for _ in range(4)]
            big_out[n] = _adamw(recv, *[padded[pre + n] for pre in ("", "m_", "v_")], l_done, big_out[n],
                                name="adamw_" + n)
        outs[l_done] = recv_small

    assert not _RIDERS, [r.host for r in _RIDERS]
    packed = _adamw(jnp.concatenate(outs, axis=1), *[_pack_state(state, pre) for pre in ("", "m_", "v_")],
                    name="adamw_small")
    small_out = [_unpack_state(t) for t in packed]
    result = lambda n, j: _unpad_shard(n, big_out[n][j]) if n in BIG else small_out[j][n]
    return (loss, dh[None], *[result(n, j) for j in range(4) for n in WEIGHTS])
```

```python
import functools
import math

import jax
import jax.numpy as jnp
from jax import lax
from jax.experimental import pallas as pl
from jax.experimental.pallas import tpu as pltpu

F32, BF16 = jnp.float32, jnp.bfloat16

D_MODEL = 1024
DEPTH = 4
CHUNK = 64
FOX_HEADS, FOX_HEAD_DIM, FOX_WIDTH = 8, 64, 512
SSM_GROUP, SSM_WIDTH, SSM_GROUPS, SSM_STATE = 16, 512, 32, 64
CA_HEADS, CA_HEAD_DIM, CA_WIDTH, CA_LEFT_CHUNKS = 8, 64, 512, 8
CA_BAND = (CA_LEFT_CHUNKS + 1) * CHUNK
CA_PAD = CA_LEFT_CHUNKS * CHUNK
REL_MIN, REL_MAX = -(CHUNK - 1), 4 * CHUNK
N_REL = REL_MAX - REL_MIN + 1
XA_HEADS, XA_HEAD_DIM = 4, 256
D_FF = 2816
DN_ALPHA = (2 * DEPTH) ** 0.25
LN_EPS = 1e-5
NEG_INF = -1e30
ADAM_LR, ADAM_B1, ADAM_B2, ADAM_EPS, ADAM_WD, ADAM_STEP = 0.001, 0.9, 0.999, 1e-08, 0.01, 10

N_DEV = 8
LANE = 128
N_SSM_CH = SSM_GROUPS * SSM_STATE
SCAN_CB = 256
N_IN = 6664
W_IN_SHARD, W_IN_SLOT = N_IN // N_DEV, 896
Z_W = 7168
Z_FQ, Z_FK, Z_FV, Z_FF, Z_SU, Z_CQ, Z_CK, Z_CV, Z_GF, Z_GS, Z_GC = (
    0, 512, 1024, 1536, 2048, 2560, 3072, 3584, 4096, 5120, 6144)
Z_PIECES = ((0, 512, Z_FQ), (512, 512, Z_FK), (1024, 512, Z_FV), (1536, 8, Z_FF), (1544, 512, Z_SU),
            (2056, 512, Z_CQ), (2568, 512, Z_CK), (3080, 512, Z_CV), (3592, 1024, Z_GF), (4616, 1024, Z_GS),
            (5640, 1024, Z_GC))
FF_HALF, FF_HALF_P = D_FF // N_DEV, 384
D_FF_P = N_DEV * FF_HALF_P

VMEM_LIMIT_BYTES = 56 * 1024 * 1024

BIG = {"w_in": ((1024, W_IN_SHARD), 1), "w_fox_o": ((512, 128), 1), "w_ssm_glu": ((512, 256), 1),
       "w_ca_o": ((512, 128), 1), "w_o": ((128, 1024), 0), "xa_wq": ((128, 1024), 0), "xa_wkv": ((1024, 256), 1),
       "xa_wo": ((128, 1024), 0), "ffn_w_up": ((1024, 2 * FF_HALF), 1), "ffn_w_down": ((FF_HALF, 1024), 0)}
SMALL = {"ca_rel_bias": (8, 320), "ffn_conv_w": (3, D_FF), "ln_g": (3, 1024), "ln_b": (3, 1024)}
REPL = {"b_in": (N_IN,), "ssm_lambda_re": (32, 64), "ssm_lambda_im": (32, 64), "ssm_log_dt": (32,),
        "ssm_b_re": (32, 64, 16), "ssm_b_im": (32, 64, 16), "ssm_c_re": (32, 16, 64), "ssm_c_im": (32, 16, 64),
        "ssm_d": (512,), "ffn_conv_b": (D_FF,)}
WEIGHTS = ("w_in", "b_in", "ssm_lambda_re", "ssm_lambda_im", "ssm_log_dt", "ssm_b_re", "ssm_b_im", "ssm_c_re",
           "ssm_c_im", "ssm_d", "ca_rel_bias", "w_fox_o", "w_ssm_glu", "w_ca_o", "w_o", "xa_wq", "xa_wkv", "xa_wo",
           "ffn_w_up", "ffn_conv_w", "ffn_conv_b", "ffn_w_down", "ln_g", "ln_b")
LANES = 1024
PACK_ROWS = 256
SMALL_ROWS = 8


def _w_in_segments():
    segs = []
    for src, width, dst in Z_PIECES:
        n = src
        while n < src + width:
            d = n // W_IN_SHARD
            end = min(src + width, (d + 1) * W_IN_SHARD)
            segs.append((W_IN_SLOT * d + n - W_IN_SHARD * d, dst + n - src, end - n))
            n = end
    return tuple(segs)


W_IN_SEGS = _w_in_segments()


def _pallas(body, **kw):
    return pl.pallas_call(body, **kw)


def _params(sem):
    return pltpu.CompilerParams(dimension_semantics=sem, vmem_limit_bytes=VMEM_LIMIT_BYTES)


class _Rider:
    def __init__(self, host, ins, out_shapes, sem_shapes, start, finish, then=None):
        self.host, self.ins, self.out_shapes, self.sem_shapes = host, list(ins), list(out_shapes), list(sem_shapes)
        self.start, self.finish, self.then, self.results = start, finish, then, None


_RIDERS = []


def _pcall(body, *, name, **kw):
    rider = next((r for r in _RIDERS if r.host == name), None)
    if rider is None:
        return _pallas(body, name=name, **kw)
    _RIDERS.remove(rider)
    grid, in_specs, scratch = kw["grid"], list(kw["in_specs"]), list(kw.get("scratch_shapes", ()))
    single = not isinstance(kw["out_shape"], (list, tuple))
    out_specs = [kw["out_specs"]] if single else list(kw["out_specs"])
    out_shape = [kw["out_shape"]] if single else list(kw["out_shape"])
    n_in, n_out, n_scr = len(in_specs), len(out_specs), len(scratch)
    r_in, r_out = len(rider.ins), len(rider.out_shapes)

    def fused(*refs):
        a, ra = refs[:n_in], refs[n_in:n_in + r_in]
        o, ro = refs[n_in + r_in:n_in + r_in + n_out], refs[n_in + r_in + n_out:n_in + r_in + n_out + r_out]
        scr, sems = refs[n_in + r_in + n_out + r_out:][:n_scr], refs[n_in + r_in + n_out + r_out + n_scr:]
        ids = [pl.program_id(d) for d in range(len(grid))]
        first = functools.reduce(jnp.logical_and, [i == 0 for i in ids])
        last = functools.reduce(jnp.logical_and, [i == g - 1 for i, g in zip(ids, grid)])

        @pl.when(first)
        def _():
            rider.start(ra, ro, sems)

        body(*a, *o, *scr)

        @pl.when(last)
        def _():
            rider.finish(ra, ro, sems)

    hbm = pl.BlockSpec(memory_space=pl.ANY)
    call = _pallas(fused, name=name, grid=grid, in_specs=in_specs + [hbm] * r_in, out_specs=out_specs + [hbm] * r_out,
                   out_shape=out_shape + rider.out_shapes, scratch_shapes=scratch + rider.sem_shapes,
                   compiler_params=_params(("arbitrary",) * len(grid)))

    def run(*operands):
        outs = call(*operands, *rider.ins)
        rider.results = list(outs[n_out:])
        if rider.then is not None:
            rider.then(rider.results)
        return outs[0] if single else list(outs[:n_out])

    return run


def _pick(dim, prefs):
    for p in prefs:
        if dim % p == 0:
            return p
    return dim


def _mm(a, b, *, ta=False, tb=False, bias=None, add=None, a_off=0, a_cols=None, b_slots=False, out_slots=None,
        name, out_dtype=F32):
    a_cols = a_cols if a_cols is not None else a.shape[1]
    M, K = (a_cols, a.shape[0]) if ta else (a.shape[0], a_cols)
    tm = _pick(M, (1024, 512, 256, 128))
    if ta:
        tk = _pick(K, (2048, 1024, 512, 256))
    elif b_slots and tb:
        tk = _pick(b.shape[2], (1024, 768, 512, 256, 128))
    else:
        tk = K if K <= 3072 else _pick(K, (1024, 512, 256, 128))
    nk = K // tk
    if b_slots:
        ns = b.shape[2]
        if tb:
            N = b.shape[1]
            tn = _pick(N, (512, 256, 128))
            per = ns // tk
            b_spec = pl.BlockSpec((None, tn, tk), lambda i, j, k: (k // per, j, k % per))
            b_dim = 1
            assert N_DEV * ns == K
        else:
            N = N_DEV * ns
            tn = _pick(ns, (512, 256, 128))
            per = ns // tn
            b_spec = pl.BlockSpec((None, tk, tn), lambda i, j, k: (j // per, k, j % per))
            b_dim = 0
            assert b.shape[1] == K
    else:
        N = b.shape[0] if tb else b.shape[1]
        assert (b.shape[1] if tb else b.shape[0]) == K, (a.shape, b.shape, ta, tb)
        tn = _pick(N if out_slots is None else out_slots, (512, 256, 128))
        if tb:
            b_spec = pl.BlockSpec((tn, tk), lambda i, j, k: (j, k))
            b_dim = 1
        else:
            b_spec = pl.BlockSpec((tk, tn), lambda i, j, k: (k, j))
            b_dim = 0
    if ta:
        assert a_off % tm == 0
        a_spec = pl.BlockSpec((tk, tm), lambda i, j, k: (k, i + a_off // tm))
        a_dim = 0
    else:
        assert a_off % tk == 0
        a_spec = pl.BlockSpec((tm, tk), lambda i, j, k: (i, k + a_off // tk))
        a_dim = 1
    cache_at = ta and nk == 1
    dims = (((1 if cache_at else a_dim,), (b_dim,)), ((), ()))
    ins, specs = [a, b], [a_spec, b_spec]
    if bias is not None:
        ins.append(bias)
        specs.append(pl.BlockSpec((1, tn), lambda i, j, k: (0, j)))
    add_scale = None
    if add is not None:
        ins.append(add[0])
        add_scale = add[1]
        specs.append(pl.BlockSpec((tm, tn), lambda i, j, k: (i, j)))
    if out_slots is None:
        out_spec = pl.BlockSpec((tm, tn), lambda i, j, k: (i, j))
        out_shape = jax.ShapeDtypeStruct((M, N), out_dtype)
    else:
        assert N == N_DEV * out_slots
        per_o = out_slots // tn
        out_spec = pl.BlockSpec((None, tm, tn), lambda i, j, k: (j // per_o, i, j % per_o))
        out_shape = jax.ShapeDtypeStruct((N_DEV, M, out_slots), out_dtype)

    def body(*refs):
        a_ref, b_ref = refs[0], refs[1]
        pos = 2
        bias_ref = add_ref = None
        if bias is not None:
            bias_ref = refs[pos]
            pos += 1
        if add is not None:
            add_ref = refs[pos]
            pos += 1
        o_ref = refs[pos]
        acc_ref = refs[pos + 1] if nk > 1 else None
        if cache_at:
            at_ref = refs[pos + 1]

            @pl.when(pl.program_id(1) == 0)
            def _():
                step = min(tk, 256)
                for c in range(0, tk, step):
                    at_ref[:, c:c + step] = a_ref[c:c + step, :].T.astype(BF16)

            lhs = at_ref[...]
        else:
            lhs = a_ref[...].astype(BF16)
        part = lax.dot_general(lhs, b_ref[...].astype(BF16), dims, preferred_element_type=F32)

        def finish(acc):
            if bias_ref is not None:
                acc = acc + bias_ref[...]
            if add_ref is not None:
                acc = acc + add_scale * add_ref[...]
            o_ref[...] = acc.astype(out_dtype)

        if nk == 1:
            finish(part)
        else:
            k = pl.program_id(2)

            @pl.when(k == 0)
            def _():
                acc_ref[...] = part

            @pl.when(k > 0)
            def _():
                acc_ref[...] += part

            @pl.when(k == nk - 1)
            def _():
                finish(acc_ref[...])

    return _pcall(
        body, name=name, grid=(M // tm, N // tn, nk), in_specs=specs, out_specs=out_spec, out_shape=out_shape,
        scratch_shapes=[pltpu.VMEM((tm, tn), F32)] if nk > 1 else [pltpu.VMEM((tm, tk), BF16)] if cache_at else [],
        compiler_params=_params(("parallel", "arbitrary", "arbitrary")),
    )(*ins)


SSM_BD = 4
SSM_BD_IN, SSM_BD_ST = SSM_WIDTH // SSM_BD, 2 * N_SSM_CH // SSM_BD


def _mm_bd(form, a, b, *, a_off=0, add=None, name):
    S = a.shape[0]
    off = a_off // SSM_BD_IN
    if form == "dw":
        def body(a_ref, b_ref, o_ref):
            at = a_ref[...].T.astype(BF16)
            o_ref[...] = lax.dot_general(at, b_ref[...].astype(BF16), _NN, preferred_element_type=F32)

        return _pcall(body, name=name, grid=(SSM_BD,),
                      in_specs=[pl.BlockSpec((S, SSM_BD_IN), lambda j: (0, off + j)),
                                pl.BlockSpec((S, SSM_BD_ST), lambda j: (0, j))],
                      out_specs=pl.BlockSpec((None, SSM_BD_IN, SSM_BD_ST), lambda j: (j, 0, 0)),
                      out_shape=jax.ShapeDtypeStruct((SSM_BD, SSM_BD_IN, SSM_BD_ST), F32),
                      compiler_params=_params(("parallel",)))(a, b)
    tm = _pick(S, (1024, 512, 256))
    wide, narrow = (SSM_BD_ST, SSM_BD_IN) if form == "in" else (SSM_BD_IN, SSM_BD_ST)
    dims = _NN if form == "in" else _NT

    def body(a_ref, w_ref, *rest):
        acc = lax.dot_general(a_ref[...].astype(BF16), w_ref[...].astype(BF16), dims, preferred_element_type=F32)
        if add is not None:
            acc = acc + add[1] * rest[0][...]
        rest[-1][...] = acc

    specs = [pl.BlockSpec((tm, narrow), lambda i, j: (i, off + j)),
             pl.BlockSpec((None, SSM_BD_IN, SSM_BD_ST), lambda i, j: (j, 0, 0))]
    ins = [a, b]
    if add is not None:
        specs.append(pl.BlockSpec((tm, wide), lambda i, j: (i, j)))
        ins.append(add[0])
    return _pcall(body, name=name, grid=(S // tm, SSM_BD), in_specs=specs,
                  out_specs=pl.BlockSpec((tm, wide), lambda i, j: (i, j)),
                  out_shape=jax.ShapeDtypeStruct((S, SSM_BD * wide), F32),
                  compiler_params=_params(("parallel", "parallel")))(*ins)


class Win:
    def __init__(self, arr, width, blk):
        self.arr, self.width, self.blk = arr, width, blk


def _rowwise(fn, rows, vecs=(), *, n_red=0, tr=256, name):
    wins = [r if isinstance(r, Win) else Win(r, r.shape[1], 0) for r in rows]
    S = wins[0].arr.shape[0]
    tr = min(tr, S)
    tile_args = [jax.ShapeDtypeStruct((tr, w.width), w.arr.dtype) for w in wins]
    tile_args += [jax.ShapeDtypeStruct(v.shape, v.dtype) for v in vecs]
    outs = jax.eval_shape(fn, *tile_args)
    n_row = len(outs) - n_red
    specs = [pl.BlockSpec((tr, w.width), functools.partial(lambda i, b: (i, b), b=w.blk)) for w in wins]
    specs += [pl.BlockSpec(v.shape, functools.partial(lambda i, nd: (0,) * nd, nd=v.ndim)) for v in vecs]
    out_specs = [pl.BlockSpec((tr, o.shape[1]), lambda i: (i, 0)) for o in outs[:n_row]]
    out_specs += [pl.BlockSpec(o.shape, functools.partial(lambda i, nd: (0,) * nd, nd=len(o.shape))) for o in outs[n_row:]]
    out_shape = [jax.ShapeDtypeStruct((S, o.shape[1]), o.dtype) for o in outs[:n_row]]
    out_shape += [jax.ShapeDtypeStruct(o.shape, o.dtype) for o in outs[n_row:]]
    n_in = len(wins) + len(vecs)

    def body(*refs):
        res = fn(*[r[...] for r in refs[:n_in]])
        o_refs = refs[n_in:]
        for o_ref, r in zip(o_refs[:n_row], res[:n_row]):
            o_ref[...] = r.astype(o_ref.dtype)
        i = pl.program_id(0)
        for o_ref, r in zip(o_refs[n_row:], res[n_row:]):
            @pl.when(i == 0)
            def _(o_ref=o_ref, r=r):
                o_ref[...] = r

            @pl.when(i > 0)
            def _(o_ref=o_ref, r=r):
                o_ref[...] += r

    return _pcall(
        body, name=name, grid=(S // tr,), in_specs=specs, out_specs=out_specs, out_shape=out_shape,
        compiler_params=_params(("arbitrary",)),
    )(*[w.arr for w in wins], *vecs)


def _whole(fn, *arrays, name):
    outs = jax.eval_shape(fn, *arrays)
    n_in = len(arrays)

    def body(*refs):
        res = fn(*[r[...] for r in refs[:n_in]])
        for o_ref, r in zip(refs[n_in:], res):
            o_ref[...] = r

    vm = pl.BlockSpec(memory_space=pltpu.VMEM)
    return _pcall(body, name=name, in_specs=[vm] * n_in, out_specs=[vm] * len(outs),
                  out_shape=[jax.ShapeDtypeStruct(o.shape, o.dtype) for o in outs])(*arrays)


def _split3(x):
    hi = x.astype(BF16)
    r = x - hi.astype(F32)
    mid = r.astype(BF16)
    lo = (r - mid.astype(F32)).astype(BF16)
    return hi, mid, lo


def _dot3(x, onehot, dims):
    return sum(lax.dot_general(t, onehot, dims, preferred_element_type=F32) for t in _split3(x))


_NT = (((1,), (1,)), ((), ()))
_NN = (((1,), (0,)), ((), ()))
_TN = (((0,), (0,)), ((), ()))


def _colmap(x, *, inverse, name):
    R = x.shape[0] if inverse else x.shape[1]
    tr = 256
    per = W_IN_SLOT // LANE
    n_out = N_DEV * per if inverse else Z_W // LANE
    segs = [(p, q, n) for q, p, n in W_IN_SEGS] if inverse else list(W_IN_SEGS)

    def body(x_ref, o_ref):
        ia = lax.broadcasted_iota(jnp.int32, (LANE, LANE), 0)
        ib = lax.broadcasted_iota(jnp.int32, (LANE, LANE), 1)

        def src_block(i):
            if inverse:
                return x_ref[:, i * LANE:(i + 1) * LANE]
            return x_ref[i // per, :, (i % per) * LANE:(i % per + 1) * LANE]

        for jb in range(n_out):
            acc = None
            for s0, d0, n in segs:
                lo, hi = max(d0, jb * LANE), min(d0 + n, (jb + 1) * LANE)
                if lo >= hi:
                    continue
                delta = d0 - s0
                for i in range((lo - delta) // LANE, (hi - delta - 1) // LANE + 1):
                    shift = jb * LANE - i * LANE - delta
                    sel = ((ia - ib == shift) & (ib >= lo - jb * LANE) & (ib < hi - jb * LANE)).astype(BF16)
                    blk = src_block(i)
                    part = _dot3(blk, sel, _NN) if inverse else lax.dot_general(blk, sel, _NN, preferred_element_type=F32)
                    acc = part if acc is None else acc + part
            if acc is None:
                acc = jnp.zeros((tr, LANE), F32)
            if inverse:
                o_ref[jb // per, :, (jb % per) * LANE:(jb % per + 1) * LANE] = acc
            else:
                o_ref[:, jb * LANE:(jb + 1) * LANE] = acc.astype(BF16)

    slot_spec = pl.BlockSpec((N_DEV, tr, W_IN_SLOT), lambda i: (0, i, 0))
    flat_spec = pl.BlockSpec((tr, Z_W), lambda i: (i, 0))
    if inverse:
        return _pcall(body, name=name, grid=(R // tr,), in_specs=[flat_spec], out_specs=slot_spec,
                      out_shape=jax.ShapeDtypeStruct((N_DEV, R, W_IN_SLOT), F32), compiler_params=_params(("parallel",)))(x)
    return _pcall(body, name=name, grid=(R // tr,), in_specs=[slot_spec], out_specs=flat_spec,
                  out_shape=jax.ShapeDtypeStruct((R, Z_W), BF16), compiler_params=_params(("parallel",)))(x)


def _log_sigmoid(x):
    return jnp.minimum(x, 0.0) - jnp.log(1.0 + jnp.exp(-jnp.abs(x)))


def _cum_heads(f, dcum=None, *, name):
    H, S = f.shape
    tn = min(512, S)
    rev = dcum is not None

    def body(*refs):
        j = pl.program_id(0)
        s_idx = lax.broadcasted_iota(jnp.int32, (S, tn), 0)
        t_idx = lax.broadcasted_iota(jnp.int32, (S, tn), 1) + j * tn
        if not rev:
            f_ref, o_ref = refs
            tri = (s_idx <= t_idx).astype(BF16)
            o_ref[...] = _dot3(_log_sigmoid(f_ref[...]), tri, _NN)
        else:
            fj_ref, d_ref, o_ref = refs
            tri = (s_idx >= t_idx).astype(BF16)
            o_ref[...] = _dot3(d_ref[...], tri, _NN) * jax.nn.sigmoid(-fj_ref[...])

    full = pl.BlockSpec((H, S), lambda j: (0, 0))
    blk = pl.BlockSpec((H, tn), lambda j: (0, j))
    ins, specs = ([f], [full]) if not rev else ([f, dcum], [blk, full])
    return _pcall(body, name=name, grid=(S // tn,), in_specs=specs, out_specs=blk,
                  out_shape=jax.ShapeDtypeStruct((H, S), F32), compiler_params=_params(("arbitrary",)))(*ins)


def _rel_onehot(qi, transposed):
    shape = (N_REL, CA_BAND) if transposed else (CA_BAND, N_REL)
    kk = lax.broadcasted_iota(jnp.int32, shape, 1 if transposed else 0)
    rr = lax.broadcasted_iota(jnp.int32, shape, 0 if transposed else 1)
    idx = jnp.clip(CA_PAD + qi - kk, REL_MIN, REL_MAX) - REL_MIN
    return (idx == rr).astype(BF16)


def _relbias_expand(rb):
    rows = 8

    def body(rb_ref, o_ref):
        for r in range(rows):
            o_ref[r] = _dot3(rb_ref[...], _rel_onehot(pl.program_id(0) * rows + r, True), _NN)

    return _pcall(body, name="relbias_expand", grid=(CHUNK // rows,),
                  in_specs=[pl.BlockSpec((CA_HEADS, N_REL), lambda q: (0, 0))],
                  out_specs=pl.BlockSpec((rows, CA_HEADS, CA_BAND), lambda q: (q, 0, 0)),
                  out_shape=jax.ShapeDtypeStruct((CHUNK, CA_HEADS, CA_BAND), F32),
                  compiler_params=_params(("arbitrary",)))(rb)


def _relbias_reduce(db):
    rows = 8

    def body(db_ref, o_ref):
        q = pl.program_id(0)
        part = sum(_dot3(db_ref[r], _rel_onehot(q * rows + r, False), _NN) for r in range(rows))

        @pl.when(q == 0)
        def _():
            o_ref[...] = part

        @pl.when(q > 0)
        def _():
            o_ref[...] += part

    return _pcall(body, name="relbias_reduce", grid=(CHUNK // rows,),
                  in_specs=[pl.BlockSpec((rows, CA_HEADS, CA_BAND), lambda q: (q, 0, 0))],
                  out_specs=pl.BlockSpec((CA_HEADS, N_REL), lambda q: (0, 0)),
                  out_shape=jax.ShapeDtypeStruct((CA_HEADS, N_REL), F32),
                  compiler_params=_params(("arbitrary",)))(db)


def _attn_cfg(mode, S):
    if mode == "fox":
        return min(256, S), FOX_HEAD_DIM ** -0.5
    if mode == "chunk":
        return CHUNK, CA_HEAD_DIM ** -0.5
    return min(512, S), XA_HEAD_DIM ** -0.5


def _visible(mode, i, tq, nk):
    if mode == "fox":
        row = lax.broadcasted_iota(jnp.int32, (tq, nk), 0) + i * tq
        return row >= lax.broadcasted_iota(jnp.int32, (tq, nk), 1)
    if mode == "chunk":
        return lax.broadcasted_iota(jnp.int32, (tq, nk), 1) + i * CHUNK >= CA_PAD
    return None


def _scores(mode, qs, kb, extra, visible):
    s = lax.dot_general(qs, kb, _NT, preferred_element_type=F32)
    if mode == "fox":
        cq, ck = extra
        s = jnp.where(visible, s + cq - ck, NEG_INF)
    elif mode == "chunk":
        (bias,) = extra
        s = jnp.where(visible, s + bias, NEG_INF)
    return s


class _AttnPlan:
    def __init__(self, mode, q, k, pp):
        self.mode, self.pp = mode, pp
        self.D, self.hpb, self.bw = (XA_HEAD_DIM, 1, XA_HEAD_DIM) if mode == "xa" else (64, 2, LANE)
        self.S, self.Sk = q[0].shape[0], k[0].shape[0]
        self.H = (XA_HEADS if mode == "xa" else FOX_HEADS)
        self.W = self.H * self.D
        self.gw = self.bw * pp
        self.hpg = self.hpb * pp
        self.tq, self.scale = _attn_cfg(mode, self.S)
        self.grid = (self.W // self.gw, self.S // self.tq)

    def rows(self, win):
        off = win[1] // self.gw
        return pl.BlockSpec((self.tq, self.gw), lambda g, i: (i, off + g))

    def cols(self, win):
        off = win[1] // self.gw
        return pl.BlockSpec((win[0].shape[0], self.gw), lambda g, i: (0, off + g))

    def extras(self):
        if self.mode == "fox":
            return [pl.BlockSpec((self.hpg, self.tq, 1), lambda g, i: (g, i, 0)),
                    pl.BlockSpec((self.hpg, 1, self.Sk), lambda g, i: (g, 0, 0))]
        if self.mode == "chunk":
            return [pl.BlockSpec((self.hpg, CHUNK, CA_BAND), lambda g, i: (g, 0, 0))]
        return []

    def per_row(self):
        return pl.BlockSpec((self.hpg, self.tq, 1), lambda g, i: (g, i, 0))

    def lanes(self, p):
        return slice(p * self.bw, (p + 1) * self.bw)

    def keys(self, i, ref, p):
        if self.mode == "chunk":
            return ref[pl.ds(pl.multiple_of(i * CHUNK, CHUNK), CA_BAND), self.lanes(p)]
        return ref[:, self.lanes(p)]

    def head(self, x, hh):
        if self.hpb == 1:
            return x
        lane = lax.broadcasted_iota(jnp.int32, x.shape, 1)
        return jnp.where(lane // self.D == hh, x, jnp.zeros_like(x))


def _attn_fwd(mode, q, k, v, extra=(), *, name):
    pl_ = _AttnPlan(mode, q, k, pp=4)
    n_ex = len(extra)
    out = (None, 0)

    def body(*refs):
        q_ref, k_ref, v_ref = refs[:3]
        o_ref, lse_ref = refs[3 + n_ex:]
        i = pl.program_id(1)
        visible = _visible(mode, i, pl_.tq, CA_BAND if mode == "chunk" else pl_.Sk)
        for p in range(pl_.pp):
            qs = q_ref[:, pl_.lanes(p)].astype(BF16) * pl_.scale
            kp = pl_.keys(i, k_ref, p).astype(BF16)
            vp = pl_.keys(i, v_ref, p).astype(BF16)
            acc = None
            for hh in range(pl_.hpb):
                h = p * pl_.hpb + hh
                ex = [r[h] for r in refs[3:3 + n_ex]]
                s = _scores(mode, pl_.head(qs, hh), kp, ex, visible)
                m = jnp.max(s, axis=1, keepdims=True)
                e = jnp.exp(s - m)
                l = jnp.sum(e, axis=1, keepdims=True)
                part = lax.dot_general((e * (1.0 / l)).astype(BF16), pl_.head(vp, hh), _NN,
                                       preferred_element_type=F32)
                acc = part if acc is None else acc + part
                lse_ref[h] = m + jnp.log(l)
            o_ref[:, pl_.lanes(p)] = acc

    return _pcall(
        body, name=name, grid=pl_.grid, in_specs=[pl_.rows(q), pl_.cols(k), pl_.cols(v)] + pl_.extras(),
        out_specs=[pl_.rows(out), pl_.per_row()],
        out_shape=[jax.ShapeDtypeStruct((pl_.S, pl_.W), F32), jax.ShapeDtypeStruct((pl_.H, pl_.S, 1), F32)],
        compiler_params=_params(("parallel", "arbitrary")),
    )(q[0], k[0], v[0], *extra)


def _attn_bwd(mode, q, k, v, lse, do, extra=(), *, name):
    pl_ = _AttnPlan(mode, q, k, pp=2 if mode == "fox" else 4)
    H, S, Sk, W = pl_.H, pl_.S, pl_.Sk, pl_.W
    n_ex = len(extra)
    out = (None, 0)
    kv_out = pl.BlockSpec((Sk, pl_.gw), lambda g, i: (0, g))
    ex_specs = pl_.extras()
    out_specs = [pl_.rows(out), kv_out, kv_out]
    out_shape = [jax.ShapeDtypeStruct((S, W), F32), jax.ShapeDtypeStruct((Sk, W), F32),
                 jax.ShapeDtypeStruct((Sk, W), F32)]
    if mode == "fox":
        out_specs += [pl_.per_row(), ex_specs[1]]
        out_shape += [jax.ShapeDtypeStruct((H, S, 1), F32), jax.ShapeDtypeStruct((H, 1, Sk), F32)]
    elif mode == "chunk":
        out_specs += [ex_specs[0]]
        out_shape += [jax.ShapeDtypeStruct((H, CHUNK, CA_BAND), F32)]

    def body(*refs):
        q_ref, k_ref, v_ref, lse_ref, do_ref = refs[:5]
        dq_ref, dk_ref, dv_ref = refs[5 + n_ex:8 + n_ex]
        rest = refs[8 + n_ex:]
        i = pl.program_id(1)

        @pl.when(i == 0)
        def _():
            dk_ref[...] = jnp.zeros_like(dk_ref)
            dv_ref[...] = jnp.zeros_like(dv_ref)
            if mode == "fox":
                rest[1][...] = jnp.zeros_like(rest[1])
            elif mode == "chunk":
                rest[0][...] = jnp.zeros_like(rest[0])

        visible = _visible(mode, i, pl_.tq, CA_BAND if mode == "chunk" else pl_.Sk)
        for p in range(pl_.pp):
            lanes = pl_.lanes(p)
            qs = q_ref[:, lanes].astype(BF16) * pl_.scale
            kp = pl_.keys(i, k_ref, p).astype(BF16)
            vp = pl_.keys(i, v_ref, p).astype(BF16)
            dop = do_ref[:, lanes].astype(BF16)
            dq = dk_part = dv_part = None
            for hh in range(pl_.hpb):
                h = p * pl_.hpb + hh
                ex = [r[h] for r in refs[5:5 + n_ex]]
                qh, doh = pl_.head(qs, hh), pl_.head(dop, hh)
                s = _scores(mode, qh, kp, ex, visible)
                pr = jnp.exp(s - lse_ref[h])
                dp = lax.dot_general(doh, vp, _NT, preferred_element_type=F32)
                ds = pr * (dp - jnp.sum(dp * pr, axis=1, keepdims=True))
                dsb = ds.astype(BF16)
                parts = (lax.dot_general(dsb, pl_.head(kp, hh), _NN, preferred_element_type=F32) * pl_.scale,
                         lax.dot_general(dsb, qh, _TN, preferred_element_type=F32),
                         lax.dot_general(pr.astype(BF16), doh, _TN, preferred_element_type=F32))
                dq, dk_part, dv_part = parts if dq is None else (dq + parts[0], dk_part + parts[1], dv_part + parts[2])
                if mode == "chunk":
                    rest[0][h] += ds
                if mode == "fox":
                    rest[0][h] = jnp.sum(ds, axis=1, keepdims=True)
                    rest[1][h] += -jnp.sum(ds, axis=0, keepdims=True)
            dq_ref[:, lanes] = dq
            if mode == "chunk":
                win = pl.ds(pl.multiple_of(i * CHUNK, CHUNK), CA_BAND)
                dk_ref[win, lanes] += dk_part
                dv_ref[win, lanes] += dv_part
            else:
                dk_ref[:, lanes] += dk_part
                dv_ref[:, lanes] += dv_part

    return _pcall(
        body, name=name, grid=pl_.grid,
        in_specs=[pl_.rows(q), pl_.cols(k), pl_.cols(v), pl_.per_row(), pl_.rows(do)] + ex_specs,
        out_specs=out_specs, out_shape=out_shape,
        compiler_params=_params(("parallel", "arbitrary")),
    )(q[0], k[0], v[0], lse, do[0], *extra)


def _scan(x, a, h=None, *, name):
    S = x.shape[0]
    CB = SCAN_CB
    rev = h is not None
    n_grp = S // 8

    def body(*refs):
        if rev:
            x_ref, a_ref, h_ref, o_ref, da_ref = refs
        else:
            x_ref, a_ref, o_ref = refs
        ar = a_ref[:, :CB]
        ai = -a_ref[:, CB:] if rev else a_ref[:, CB:]
        zero = jnp.zeros((1, CB), F32)

        def group(g, carry):
            base = pl.multiple_of((n_grp - 1 - g) * 8 if rev else g * 8, 8)
            for j in (range(7, -1, -1) if rev else range(8)):
                t = base + j
                if rev:
                    hr, hi, dar, dai = carry
                else:
                    hr, hi = carry
                xr = x_ref[pl.ds(t, 1), :CB]
                xi = x_ref[pl.ds(t, 1), CB:]
                hr, hi = ar * hr - ai * hi + xr, ar * hi + ai * hr + xi
                o_ref[pl.ds(t, 1), :CB] = hr
                o_ref[pl.ds(t, 1), CB:] = hi
                if rev:
                    tp = jnp.maximum(t - 1, 0)
                    live = (t > 0).astype(F32)
                    pr = h_ref[pl.ds(tp, 1), :CB] * live
                    pi = h_ref[pl.ds(tp, 1), CB:] * live
                    carry = (hr, hi, dar + hr * pr + hi * pi, dai + hi * pr - hr * pi)
                else:
                    carry = (hr, hi)
            return carry

        if rev:
            _, _, dar, dai = lax.fori_loop(0, n_grp, group, (zero, zero, zero, zero))
            da_ref[:, :CB] = dar
            da_ref[:, CB:] = dai
        else:
            lax.fori_loop(0, n_grp, group, (zero, zero))

    big = pl.BlockSpec((S, 2 * CB), lambda c: (0, c))
    vec = pl.BlockSpec((1, 2 * CB), lambda c: (0, c))
    n_blk = x.shape[1] // (2 * CB)
    if rev:
        return _pcall(body, name=name, grid=(n_blk,), in_specs=[big, vec, big], out_specs=[big, vec],
                      out_shape=[jax.ShapeDtypeStruct(x.shape, F32), jax.ShapeDtypeStruct(a.shape, F32)],
                      compiler_params=_params(("parallel",)))(x, a, h)
    return _pcall(body, name=name, grid=(n_blk,), in_specs=[big, vec], out_specs=big,
                  out_shape=jax.ShapeDtypeStruct(x.shape, F32), compiler_params=_params(("parallel",)))(x, a)


def _ssm_prep1(lr_, li, ldt):
    lr = jnp.minimum(lr_, -1e-4)
    dt = jnp.exp(ldt)
    mag = jnp.exp(lr * dt)
    ar = mag * jnp.cos(li * dt)
    ai = mag * jnp.sin(li * dt)
    den = lr * lr + li * li
    gr = ((ar - 1.0) * lr + ai * li) / den
    gi = (ai * lr - (ar - 1.0) * li) / den
    return ar, ai, gr, gi


def _ssm_prep2(gr, gi, br, bi):
    return gr * br - gi * bi, gr * bi + gi * br


def _vjp_of(fn, n_in):
    def bwd(*args):
        cts = args[n_in:]
        return jax.vjp(fn, *args[:n_in])[1](cts[0] if len(cts) == 1 else tuple(cts))
    return bwd


def _to_blocked(r, i):
    lead = r.shape[:-1]
    t = jnp.stack([r.reshape(lead + (N_SSM_CH // SCAN_CB, SCAN_CB)), i.reshape(lead + (N_SSM_CH // SCAN_CB, SCAN_CB))],
                  axis=-2)
    return t.reshape(lead + (2 * N_SSM_CH,))


def _from_blocked(m):
    lead = m.shape[:-1]
    t = m.reshape(lead + (N_SSM_CH // SCAN_CB, 2, SCAN_CB))
    return t[..., 0, :].reshape(lead + (N_SSM_CH,)), t[..., 1, :].reshape(lead + (N_SSM_CH,))


_GPB = SSM_GROUPS // SSM_BD
_GPS = SCAN_CB // SSM_STATE


def _bd_eye():
    return jnp.eye(_GPB, dtype=F32).reshape(_GPB, _GPB // _GPS, _GPS)


def _blockdiag(r, i):
    v = jnp.stack([r, i]).reshape(2, SSM_BD, _GPB, SSM_GROUP, SSM_STATE)
    return jnp.einsum("qjgcp,gsh->jgcsqhp", v, _bd_eye()).reshape(SSM_BD, SSM_BD_IN, SSM_BD_ST)


def _blockdiag_inv(m):
    d = m.reshape(SSM_BD, _GPB, SSM_GROUP, _GPB // _GPS, 2, _GPS, SSM_STATE)
    v = jnp.einsum("jgcsqhp,gsh->qjgcp", d, _bd_eye()).reshape(2, SSM_GROUPS, SSM_GROUP, SSM_STATE)
    return v[0], v[1]


def _shift_rows(x, n):
    S = x.shape[0]
    row = lax.broadcasted_iota(jnp.int32, x.shape, 0)
    if n > 0:
        return jnp.where(row >= n, pltpu.roll(x, n, 0), 0.0)
    return jnp.where(row < S + n, pltpu.roll(x, S + n, 0), 0.0)


def _bf(x):
    return x.astype(BF16).astype(F32)


def _conv_pre(a, w, b):
    ab, wb = _bf(a), _bf(w)
    return wb[2:3] * ab + wb[1:2] * _shift_rows(ab, 1) + wb[0:1] * _shift_rows(ab, 2) + b


def _ffn_mid(up, conv_w, conv_b, dh=None, *, name):
    S = up.shape[0]
    tn = LANE
    nb = D_FF_P // tn
    rev = dh is not None

    def body(*refs):
        if not rev:
            a_ref, g_ref, w_ref, b_ref, o_ref = refs
            o_ref[...] = jax.nn.gelu(_conv_pre(a_ref[...], w_ref[...], b_ref[...])) * g_ref[...]
            return
        a_ref, g_ref, w_ref, b_ref, dh_ref, dup_a_ref, dup_g_ref, dw_ref, db_ref = refs
        a, w, dh_ = a_ref[...], w_ref[...], dh_ref[...]
        pre = _conv_pre(a, w, b_ref[...])
        gl, gelu_vjp = jax.vjp(jax.nn.gelu, pre)
        dup_g_ref[...] = dh_ * gl
        (dpre,) = gelu_vjp(dh_ * g_ref[...])
        db_ref[...] = jnp.sum(dpre, axis=0, keepdims=True)
        dpb, ab, wb = _bf(dpre), _bf(a), _bf(w)
        dup_a_ref[...] = wb[2:3] * dpb + wb[1:2] * _shift_rows(dpb, -1) + wb[0:1] * _shift_rows(dpb, -2)
        dw_ref[2:3, :] = jnp.sum(dpb * ab, axis=0, keepdims=True)
        dw_ref[1:2, :] = jnp.sum(dpb * _shift_rows(ab, 1), axis=0, keepdims=True)
        dw_ref[0:1, :] = jnp.sum(dpb * _shift_rows(ab, 2), axis=0, keepdims=True)

    a_spec = pl.BlockSpec((S, tn), lambda j: (0, j))
    g_spec = pl.BlockSpec((S, tn), lambda j: (0, j + nb))
    w_spec = pl.BlockSpec((3, tn), lambda j: (0, j))
    b_spec = pl.BlockSpec((1, tn), lambda j: (0, j))
    if not rev:
        return _pcall(body, name=name, grid=(nb,), in_specs=[a_spec, g_spec, w_spec, b_spec], out_specs=a_spec,
                      out_shape=jax.ShapeDtypeStruct((S, D_FF_P), F32), compiler_params=_params(("parallel",)))(
                          up, up, conv_w, conv_b)
    return _pcall(body, name=name, grid=(nb,), in_specs=[a_spec, g_spec, w_spec, b_spec, a_spec],
                  out_specs=[a_spec, a_spec, w_spec, b_spec],
                  out_shape=[jax.ShapeDtypeStruct((S, D_FF_P), F32), jax.ShapeDtypeStruct((S, D_FF_P), F32),
                             jax.ShapeDtypeStruct((3, D_FF_P), F32), jax.ShapeDtypeStruct((1, D_FF_P), F32)],
                  compiler_params=_params(("parallel",)))(up, up, conv_w, conv_b, dh)


def _ff_pad(t):
    lead = t.shape[:-1]
    t = t.reshape(lead + (N_DEV, FF_HALF))
    return jnp.pad(t, [(0, 0)] * len(lead) + [(0, 0), (0, FF_HALF_P - FF_HALF)]).reshape(lead + (D_FF_P,))


def _ff_unpad(t):
    lead = t.shape[:-1]
    return t.reshape(lead + (N_DEV, FF_HALF_P))[..., :FF_HALF].reshape(lead + (D_FF,))


def _ln_fwd(x, h, g, b):
    r = DN_ALPHA * x + h
    mu = jnp.mean(r, axis=-1, keepdims=True)
    var = jnp.mean(jnp.square(r - mu), axis=-1, keepdims=True)
    return r, (r - mu) * lax.rsqrt(var + LN_EPS) * g + b


def _ln_bwd(r, dy, g):
    mu = jnp.mean(r, axis=-1, keepdims=True)
    var = jnp.mean(jnp.square(r - mu), axis=-1, keepdims=True)
    xhat = (r - mu) * lax.rsqrt(var + LN_EPS)
    dxh = dy * g
    dr = lax.rsqrt(var + LN_EPS) * (dxh - jnp.mean(dxh, axis=-1, keepdims=True)
                                    - xhat * jnp.mean(dxh * xhat, axis=-1, keepdims=True))
    return dr, jnp.sum(dy * xhat, axis=0, keepdims=True), jnp.sum(dy, axis=0, keepdims=True)


def _merge(gf, gs, gc, ya, yb2, yc):
    yb = yb2[:, :D_MODEL] * jax.nn.sigmoid(yb2[:, D_MODEL:])
    return jax.nn.sigmoid(gf) * ya + jax.nn.sigmoid(gs) * yb + jax.nn.sigmoid(gc) * yc


def _s5_tail(hc, su, d):
    return jax.nn.gelu(hc + d * su)


def _s5_tail_bwd(hc, su, dgel, d):
    _, vjp = jax.vjp(jax.nn.gelu, hc + d * su)
    (dy,) = vjp(dgel)
    return dy, d * dy, jnp.sum(dy * su, axis=0, keepdims=True)


def _loss_rows(y, tgt):
    err = y - tgt
    return err * (1.0 / D_MODEL), jnp.sum(0.5 * jnp.square(err), axis=0, keepdims=True) * (1.0 / D_MODEL)


def _peer(k):
    x, y, c = lax.axis_index("x"), lax.axis_index("y"), lax.axis_index("c")
    return (x ^ ((k >> 2) & 1), y ^ ((k >> 1) & 1), c ^ (k & 1))


def _my_slot():
    return 4 * lax.axis_index("x") + 2 * lax.axis_index("y") + lax.axis_index("c")


def _peer_slot(k):
    px, py, pc = _peer(k)
    return 4 * px + 2 * py + pc


N_CHIP = N_DEV // 2
OTHER_CHIPS = (2, 4, 6)


def _chip_of(dev):
    return 2 * dev[0] + dev[1]


def _remote(src, dst, send, recv, dev):
    return pltpu.make_async_remote_copy(src_ref=src, dst_ref=dst, send_sem=send, recv_sem=recv, device_id=dev,
                                        device_id_type=pl.DeviceIdType.MESH)


def _all_gather(shards, *, name, host=None, then=None):
    return _exchange(shards, *_all_gather_parts(shards), name=name, host=host, then=then)


def _exchange(ins, out_shapes, sem_shapes, start, finish, *, name, host=None, then=None):
    if host is not None:
        rider = _Rider(host, ins, out_shapes, sem_shapes, start, finish, then)
        _RIDERS.append(rider)
        return rider
    n = len(ins)

    def body(*refs):
        start(refs[:n], refs[n:2 * n], refs[2 * n:])
        finish(refs[:n], refs[n:2 * n], refs[2 * n:])

    hbm = pl.BlockSpec(memory_space=pl.ANY)
    return _pcall(body, name=name, in_specs=[hbm] * n, out_specs=[hbm] * n, out_shape=list(out_shapes),
                  scratch_shapes=list(sem_shapes))(*ins)


def _all_gather_parts(shards):
    n = len(shards)

    def first_copies(ins, outs, sems):
        send, recv, _ = sems
        return [_remote(ins[t], outs[t].at[_my_slot()], send.at[t, k - 1], recv.at[t, k - 1], _peer(k))
                for k in (1,) + OTHER_CHIPS for t in range(n)]

    def local_copies(ins, outs, sems):
        return [pltpu.make_async_copy(ins[t], outs[t].at[_my_slot()], sems[2].at[t]) for t in range(n)]

    def start(ins, outs, sems):
        for cp in local_copies(ins, outs, sems) + first_copies(ins, outs, sems):
            cp.start()

    def finish(ins, outs, sems):
        send, recv, _ = sems
        sibling = _peer(1)
        passed = []
        for k in OTHER_CHIPS:
            for t in range(n):
                slot = outs[t].at[_peer_slot(k)]
                _remote(ins[t], slot, send.at[t, k - 1], recv.at[t, k - 1], _peer(k)).wait_recv()
                cp = _remote(slot, slot, send.at[t, k], recv.at[t, k], sibling)
                cp.start()
                passed.append(cp)
        for t in range(n):
            _remote(ins[t], outs[t].at[_peer_slot(1)], send.at[t, 0], recv.at[t, 0], sibling).wait_recv()
            for k in OTHER_CHIPS:
                _remote(ins[t], outs[t].at[_peer_slot(k + 1)], send.at[t, k], recv.at[t, k], sibling).wait_recv()
        for cp in first_copies(ins, outs, sems) + passed:
            cp.wait_send()
        for lc in local_copies(ins, outs, sems):
            lc.wait()

    out_shapes = [jax.ShapeDtypeStruct((N_DEV,) + s.shape, s.dtype) for s in shards]
    sem_shapes = [pltpu.SemaphoreType.DMA((n, N_DEV - 1)), pltpu.SemaphoreType.DMA((n, N_DEV - 1)),
                  pltpu.SemaphoreType.DMA((n,))]
    return out_shapes, sem_shapes, start, finish


def _sibling_swap(grads, *, name, host=None, then=None):
    n = len(grads)

    def copies(ins, outs, sems):
        c = lax.axis_index("c")
        return [_remote(ins[t].at[:, 1 - c], outs[t], sems[0].at[t], sems[1].at[t], _peer(1)) for t in range(n)]

    def start(ins, outs, sems):
        for cp in copies(ins, outs, sems):
            cp.start()

    def finish(ins, outs, sems):
        for cp in copies(ins, outs, sems):
            cp.wait()

    out_shapes = [jax.ShapeDtypeStruct((N_CHIP,) + g.shape[2:], g.dtype) for g in grads]
    sem_shapes = [pltpu.SemaphoreType.DMA((n,)), pltpu.SemaphoreType.DMA((n,))]
    return _exchange(grads, out_shapes, sem_shapes, start, finish, name=name, host=host, then=then)


def _pair_add(g, p, out_dtype, *, name):
    _, _, R, C = g.shape
    tr = _pick(R, (512, 256, 128, 64, 32, 16, 8))

    def body(c_ref, g_ref, p_ref, o_ref):
        o_ref[...] = (g_ref[...] + p_ref[...]).astype(out_dtype)

    grid_spec = pltpu.PrefetchScalarGridSpec(
        num_scalar_prefetch=1, grid=(N_CHIP, R // tr),
        in_specs=[pl.BlockSpec((None, None, tr, C), lambda j, i, c_ref: (j, c_ref[0], i, 0)),
                  pl.BlockSpec((None, tr, C), lambda j, i, c_ref: (j, i, 0))],
        out_specs=pl.BlockSpec((None, tr, C), lambda j, i, c_ref: (j, i, 0)))
    core = lax.axis_index("c").astype(jnp.int32).reshape(1)
    return _pcall(body, name=name, grid_spec=grid_spec, out_shape=jax.ShapeDtypeStruct(p.shape, out_dtype),
                  compiler_params=_params(("parallel", "parallel")))(core, g, p)


def _chip_exchange(sums, *, name, host=None):
    n = len(sums)

    def copies(ins, outs, sems, dst_is_mine):
        send, recv, _ = sems
        mine = 2 * lax.axis_index("x") + lax.axis_index("y")
        out = []
        for k in OTHER_CHIPS:
            theirs = _chip_of(_peer(k))
            for t in range(n):
                out.append(_remote(ins[t].at[theirs], outs[t].at[mine if dst_is_mine else theirs],
                                   send.at[t, k // 2 - 1], recv.at[t, k // 2 - 1], _peer(k)))
        return out

    def local_copies(ins, outs, sems):
        mine = 2 * lax.axis_index("x") + lax.axis_index("y")
        return [pltpu.make_async_copy(ins[t].at[mine], outs[t].at[mine], sems[2].at[t]) for t in range(n)]

    def start(ins, outs, sems):
        for cp in local_copies(ins, outs, sems) + copies(ins, outs, sems, True):
            cp.start()

    def finish(ins, outs, sems):
        for cp in copies(ins, outs, sems, False) + local_copies(ins, outs, sems):
            cp.wait()

    out_shapes = [jax.ShapeDtypeStruct(s.shape, s.dtype) for s in sums]
    sem_shapes = [pltpu.SemaphoreType.DMA((n, N_CHIP - 1)), pltpu.SemaphoreType.DMA((n, N_CHIP - 1)),
                  pltpu.SemaphoreType.DMA((n,))]
    return _exchange(sums, out_shapes, sem_shapes, start, finish, name=name, host=host)


class _GradReduce:
    def __init__(self, grads, wire_dtypes, hosts=None):
        pairs = [g.reshape((N_CHIP, 2) + g.shape[1:]) for g in grads]
        self.n, self.riders, self.recv = len(grads), [], None

        def after_swap(partner):
            sums = [_pair_add(g, p, dt, name="grad_pair_add") for g, p, dt in zip(pairs, partner, wire_dtypes)]
            if hosts is None:
                self.recv = _chip_exchange(sums, name="grad_chip_exchange")
            else:
                self.riders = [(idx, _chip_exchange([sums[i] for i in idx], name="grad_chip_exchange", host=h))
                               for h, idx in hosts[1]]

        if hosts is None:
            after_swap(_sibling_swap(pairs, name="grad_sibling_swap"))
        else:
            _sibling_swap(pairs, name="grad_sibling_swap", host=hosts[0], then=after_swap)

    def result(self):
        if self.recv is None:
            self.recv = [None] * self.n
            for idx, rider in self.riders:
                assert rider.results is not None, rider.host
                for i, r in zip(idx, rider.results):
                    self.recv[i] = r
        return self.recv


def _adamw(recv, w, m, v, layer=None, into=None, *, name):
    n_slots, R, C = recv.shape
    tr = _pick(R, (256, 128, 64, 32, 16, 8))

    def body(r_ref, w_ref, m_ref, v_ref, *rest):
        g_ref, d_ref, nm_ref, nv_ref = rest[-4:]
        g = r_ref[0].astype(F32)
        for s in range(1, n_slots):
            g = g + r_ref[s].astype(F32)
        m_new = ADAM_B1 * m_ref[...] + (1.0 - ADAM_B1) * g
        v_new = ADAM_B2 * v_ref[...] + (1.0 - ADAM_B2) * jnp.square(g)
        m_hat = m_new / (1.0 - ADAM_B1 ** ADAM_STEP)
        v_hat = v_new / (1.0 - ADAM_B2 ** ADAM_STEP)
        g_ref[...] = g
        d_ref[...] = -ADAM_LR * (m_hat / (jnp.sqrt(v_hat) + ADAM_EPS) + ADAM_WD * w_ref[...])
        nm_ref[...] = m_new
        nv_ref[...] = v_new

    row = pl.BlockSpec((tr, C), lambda i: (i, 0))
    state = row if layer is None else pl.BlockSpec((None, tr, C), lambda i: (layer, i, 0))
    in_specs = [pl.BlockSpec((n_slots, tr, C), lambda i: (0, i, 0)), state, state, state]
    if into is None:
        return _pcall(body, name=name, grid=(R // tr,), in_specs=in_specs, out_specs=[row] * 4,
                      out_shape=[jax.ShapeDtypeStruct((R, C), F32)] * 4, compiler_params=_params(("parallel",)),
                      )(recv, w, m, v)
    return _pcall(body, name=name, grid=(R // tr,), in_specs=in_specs + [pl.BlockSpec(memory_space=pl.ANY)] * 4,
                  out_specs=[state] * 4, out_shape=[jax.ShapeDtypeStruct((DEPTH, R, C), F32)] * 4,
                  input_output_aliases={4 + j: j for j in range(4)}, compiler_params=_params(("parallel",)),
                  )(recv, w, m, v, *into)


def _flat_pad(parts, rows):
    flat = jnp.concatenate([p.reshape(-1) for p in parts])
    return jnp.pad(flat, (0, rows * LANES - flat.shape[0])).reshape(rows, LANES)


def _small_shard_shape(n):
    return SMALL[n][:-1] + (SMALL[n][-1] // N_DEV,)


def _unpack_small(gathered):
    out, off = {}, 0
    flat = gathered.reshape(N_DEV, -1)
    for n in SMALL:
        r, c = _small_shard_shape(n)
        out[n] = flat[:, off:off + r * c].reshape(N_DEV, r, c).transpose(1, 0, 2).reshape(r, N_DEV * c)
        off += r * c
    return out


def _pack_state(state, prefix):
    flat = jnp.concatenate([state[prefix + n].reshape(DEPTH, -1) for n in (*SMALL, *REPL)], axis=1)
    return jnp.pad(flat, ((0, 0), (0, PACK_ROWS * LANES - flat.shape[1]))).reshape(DEPTH * PACK_ROWS, LANES)


def _pack_small_grads(grads):
    cols = []
    for n in SMALL:
        r, c = _small_shard_shape(n)
        cols.append(grads[n].reshape(r, N_DEV, c).transpose(1, 0, 2).reshape(N_DEV, r * c))
    cols += [jnp.broadcast_to(grads[n].reshape(1, -1), (N_DEV, grads[n].size)) for n in REPL]
    flat = jnp.concatenate(cols, axis=1)
    return jnp.pad(flat, ((0, 0), (0, PACK_ROWS * LANES - flat.shape[1]))).reshape(N_DEV, PACK_ROWS, LANES)


def _unpack_state(packed):
    out, off = {}, 0
    flat = packed.reshape(DEPTH, -1)
    for n, shape in [(n, _small_shard_shape(n)) for n in SMALL] + list(REPL.items()):
        sz = math.prod(shape)
        out[n] = flat[:, off:off + sz].reshape((DEPTH,) + shape)
        off += sz
    return out


def _pad_b_in(b):
    parts, pos = [], 0
    for src, width, dst in Z_PIECES:
        parts += [jnp.zeros((b.shape[0], dst - pos), b.dtype), b[:, src:src + width]]
        pos = dst + width
    return jnp.concatenate(parts + [jnp.zeros((b.shape[0], Z_W - pos), b.dtype)], axis=1)


def _unpad_b_in(bp):
    return jnp.concatenate([bp[:, dst:dst + width] for _, width, dst in Z_PIECES], axis=1)


def _up_shard_pad(t):
    gap = jnp.zeros(t.shape[:-1] + (FF_HALF_P - FF_HALF,), t.dtype)
    return jnp.concatenate([t[..., :FF_HALF], gap, t[..., FF_HALF:], gap], axis=-1)


def _up_shard_unpad(t):
    return jnp.concatenate([t[..., :FF_HALF], t[..., FF_HALF_P:FF_HALF_P + FF_HALF]], axis=-1)


def _pad_shard(n, t):
    lead = [(0, 0)] * (t.ndim - 2)
    if n == "w_in":
        return jnp.pad(t, lead + [(0, 0), (0, W_IN_SLOT - W_IN_SHARD)])
    if n == "ffn_w_up":
        return _up_shard_pad(t)
    if n == "ffn_w_down":
        return jnp.pad(t, lead + [(0, FF_HALF_P - FF_HALF), (0, 0)])
    return t


def _unpad_shard(n, t):
    if n == "w_in":
        return t[..., :W_IN_SHARD]
    if n == "ffn_w_up":
        return _up_shard_unpad(t)
    if n == "ffn_w_down":
        return t[..., :FF_HALF, :]
    return t


GATHER_HOSTS = (("fox_fwd", (0,)), ("chunk_fwd", (8, 9)), ("mm_up", (1, 2, 3, 4, 5, 6, 7, 10)))
REDUCE_HOSTS = ("mm_up_dw", (("fox_bwd", (0,)), ("chunk_bwd", (8, 9)), ("s5_scan_bwd", (1, 2, 3, 4, 5, 6, 7, 10))))


def _layer_shards(state, layer):
    shards = [_pad_shard(n, state[n][layer].astype(BF16)) for n in BIG]
    return shards + [_flat_pad([state[n][layer] for n in SMALL], SMALL_ROWS)]


def _gathered_weights(gathered):
    *big, small = gathered
    W = dict(zip(BIG, big))
    for n in ("w_o", "xa_wq", "xa_wo", "ffn_w_down"):
        W[n] = W[n].reshape(-1, D_MODEL)
    W["w_in_p"] = _colmap(W.pop("w_in"), inverse=False, name="w_in_colmap")
    W.update(_unpack_small(small))
    return W


def _ssm_params(p):
    prep1_in = (p["ssm_lambda_re"], p["ssm_lambda_im"], p["ssm_log_dt"][:, None])
    ar, ai, gr, gi = _whole(_ssm_prep1, *prep1_in, name="ssm_prep1")
    to_cn = lambda b: b.transpose(2, 0, 1).reshape(SSM_GROUP, N_SSM_CH)
    prep2_in = (gr.reshape(1, N_SSM_CH), gi.reshape(1, N_SSM_CH), to_cn(p["ssm_b_re"]), to_cn(p["ssm_b_im"]))
    bbr, bbi = _whole(_ssm_prep2, *prep2_in, name="ssm_prep2")
    to_gcp = lambda t: t.reshape(SSM_GROUP, SSM_GROUPS, SSM_STATE).transpose(1, 0, 2)
    bb = _blockdiag(to_gcp(bbr), to_gcp(bbi))
    cct = _blockdiag(p["ssm_c_re"], -p["ssm_c_im"])
    a_vec = _to_blocked(ar.reshape(1, N_SSM_CH), ai.reshape(1, N_SSM_CH))
    return dict(bb=bb, cct=cct, a_vec=a_vec, prep1_in=prep1_in, prep2_in=prep2_in)


def _layer_fwd(x, mem, p, W):
    sp = _ssm_params(p)
    z = _mm(x, W["w_in_p"], bias=p["b_in_p"], name="mm_in")
    f_t = z[:, Z_FF:Z_FF + FOX_HEADS].T
    cum = _cum_heads(f_t, name="fox_cum")
    ya_pre, lse_a = _attn_fwd("fox", (z, Z_FQ), (z, Z_FK), (z, Z_FV), (cum[:, :, None], cum[:, None, :]),
                              name="fox_fwd")
    ya = _mm(ya_pre, W["w_fox_o"], b_slots=True, name="mm_fox_o")
    x_ri = _mm_bd("in", z, sp["bb"], a_off=Z_SU, name="mm_s5_in")
    h_ri = _scan(x_ri, sp["a_vec"], name="s5_scan")
    hc = _mm_bd("out", h_ri, sp["cct"], name="mm_s5_out")
    d_row = p["ssm_d"][None, :]
    (gel,) = _rowwise(lambda a, b, c: (_s5_tail(a, b, c),), [hc, Win(z, 512, Z_SU // 512)], [d_row], name="s5_tail")
    yb2 = _mm(gel, W["w_ssm_glu"], b_slots=True, name="mm_glu")
    bias = _relbias_expand(W["ca_rel_bias"]).transpose(1, 0, 2)
    kv_band = jnp.pad(z[:, Z_CK:Z_CK + 2 * CA_WIDTH].astype(BF16), ((CA_PAD, 0), (0, 0)))
    yc_pre, lse_c = _attn_fwd("chunk", (z, Z_CQ), (kv_band, 0), (kv_band, CA_WIDTH), (bias,), name="chunk_fwd")
    yc = _mm(yc_pre, W["w_ca_o"], b_slots=True, name="mm_ca_o")
    gates = [Win(z, 1024, Z_GF // 1024), Win(z, 1024, Z_GS // 1024), Win(z, 1024, Z_GC // 1024)]
    (merged,) = _rowwise(lambda *a: (_merge(*a),), gates + [ya, yb2, yc], name="merge")
    h1 = _mm(merged, W["w_o"], name="mm_o")
    ln_g, ln_b = W["ln_g"], W["ln_b"]
    r1, x1 = _rowwise(_ln_fwd, [x, h1], [ln_g[0:1], ln_b[0:1]], name="ln_fwd")
    q = _mm(x1, W["xa_wq"], name="mm_xq")
    kv = _mm(mem, W["xa_wkv"], b_slots=True, name="mm_xkv")
    o, lse_x = _attn_fwd("xa", (q, 0), (kv, 0), (kv, D_MODEL), name="xa_fwd")
    h2 = _mm(o, W["xa_wo"], name="mm_xo")
    r2, x2 = _rowwise(_ln_fwd, [x1, h2], [ln_g[1:2], ln_b[1:2]], name="ln_fwd")
    up = _mm(x2, W["ffn_w_up"], b_slots=True, name="mm_up")
    hmid = _ffn_mid(up, _ff_pad(W["ffn_conv_w"]), _ff_pad(p["ffn_conv_b"][None, :]), name="ffn_mid")
    h3 = _mm(hmid, W["ffn_w_down"], name="mm_down")
    r3, x3 = _rowwise(_ln_fwd, [x2, h3], [ln_g[2:3], ln_b[2:3]], name="ln_fwd")
    res = dict(x=x, z=z, cum=cum, lse_a=lse_a, ya_pre=ya_pre, ya=ya, h_ri=h_ri, hc=hc, gel=gel, yb2=yb2, lse_c=lse_c,
               yc_pre=yc_pre, yc=yc, merged=merged, r1=r1, x1=x1, q=q, kv=kv, o=o, lse_x=lse_x, r2=r2, x2=x2, up=up,
               hmid=hmid, r3=r3, bias=bias, sp=sp, kv_band=kv_band, W=W)
    return x3, res


def _layer_bwd(dx3, mem, p, res):
    W = res["W"]
    x, z = res["x"], res["z"]
    sp = res["sp"]
    ln_g = W["ln_g"]
    big, small = {}, {}
    slots = lambda t: t.reshape(N_DEV, -1, D_MODEL)
    dr3, dg2, db2 = _rowwise(_ln_bwd, [res["r3"], dx3], [ln_g[2:3]], n_red=2, name="ln_bwd")
    dhmid = _mm(dr3, W["ffn_w_down"], tb=True, name="mm_down_dx")
    big["ffn_w_down"] = slots(_mm(res["hmid"], dr3, ta=True, name="mm_down_dw"))
    conv_w_p, conv_b_p = _ff_pad(W["ffn_conv_w"]), _ff_pad(p["ffn_conv_b"][None, :])
    dup_a, dup_g, dcw, dcb = _ffn_mid(res["up"], conv_w_p, conv_b_p, dhmid, name="ffn_mid_bwd")
    dup = jnp.concatenate([dup_a, dup_g], axis=1)
    small["ffn_conv_w"], small["ffn_conv_b"] = _ff_unpad(dcw), _ff_unpad(dcb)[0]
    dx2 = _mm(dup, W["ffn_w_up"], tb=True, b_slots=True, add=(dr3, DN_ALPHA), name="mm_up_dx")
    big["ffn_w_up"] = _mm(res["x2"], dup, ta=True, out_slots=2 * FF_HALF_P, name="mm_up_dw")
    dr2, dg1, db1 = _rowwise(_ln_bwd, [res["r2"], dx2], [ln_g[1:2]], n_red=2, name="ln_bwd")
    do = _mm(dr2, W["xa_wo"], tb=True, name="mm_xo_dx")
    big["xa_wo"] = slots(_mm(res["o"], dr2, ta=True, name="mm_xo_dw"))
    kv = res["kv"]
    dq, dk, dv = _attn_bwd("xa", (res["q"], 0), (kv, 0), (kv, D_MODEL), res["lse_x"], (do, 0), name="xa_bwd")
    dkv = jnp.concatenate([dk, dv], axis=1)
    dx1 = _mm(dq, W["xa_wq"], tb=True, add=(dr2, DN_ALPHA), name="mm_xq_dx")
    big["xa_wq"] = slots(_mm(res["x1"], dq, ta=True, name="mm_xq_dw"))
    big["xa_wkv"] = _mm(mem, dkv, ta=True, out_slots=256, name="mm_xkv_dw")
    dr1, dg0, db0 = _rowwise(_ln_bwd, [res["r1"], dx1], [ln_g[0:1]], n_red=2, name="ln_bwd")
    small["ln_g"] = jnp.concatenate([dg0, dg1, dg2], axis=0)
    small["ln_b"] = jnp.concatenate([db0, db1, db2], axis=0)
    dmerged = _mm(dr1, W["w_o"], tb=True, name="mm_o_dx")
    big["w_o"] = slots(_mm(res["merged"], dr1, ta=True, name="mm_o_dw"))
    gates = [Win(z, 1024, Z_GF // 1024), Win(z, 1024, Z_GS // 1024), Win(z, 1024, Z_GC // 1024)]
    dgf, dgs, dgc, dya, dyb2, dyc = _rowwise(_vjp_of(_merge, 6), gates + [res["ya"], res["yb2"], res["yc"], dmerged],
                                             name="merge_bwd")
    dya_pre = _mm(dya, W["w_fox_o"], tb=True, b_slots=True, name="mm_fox_o_dx")
    big["w_fox_o"] = _mm(res["ya_pre"], dya, ta=True, out_slots=128, name="mm_fox_o_dw")
    cum = res["cum"]
    dfq, dfk, dfv, dcq, dck = _attn_bwd("fox", (z, Z_FQ), (z, Z_FK), (z, Z_FV), res["lse_a"], (dya_pre, 0),
                                        (cum[:, :, None], cum[:, None, :]), name="fox_bwd")
    f_t = z[:, Z_FF:Z_FF + FOX_HEADS].T
    dff = _cum_heads(f_t, dcq[:, :, 0] + dck[:, 0, :], name="fox_cum_bwd")
    dgel = _mm(dyb2, W["w_ssm_glu"], tb=True, b_slots=True, name="mm_glu_dx")
    big["w_ssm_glu"] = _mm(res["gel"], dyb2, ta=True, out_slots=256, name="mm_glu_dw")
    d_row = p["ssm_d"][None, :]
    su_win = Win(z, 512, Z_SU // 512)
    dy, dsu1, dd = _rowwise(_s5_tail_bwd, [res["hc"], su_win, dgel], [d_row], n_red=1, name="s5_tail_bwd")
    small["ssm_d"] = dd[0]
    dh_ri = _mm_bd("in", dy, sp["cct"], name="mm_s5_out_dx")
    dcct = _mm_bd("dw", dy, res["h_ri"], name="mm_s5_out_dw")
    dx_ri, da_vec = _scan(dh_ri, sp["a_vec"], res["h_ri"], name="s5_scan_bwd")
    dsu = _mm_bd("out", dx_ri, sp["bb"], add=(dsu1, 1.0), name="mm_s5_in_dx")
    dbb = _mm_bd("dw", z, dx_ri, a_off=Z_SU, name="mm_s5_in_dw")
    dcr, dci = _blockdiag_inv(dcct)
    small["ssm_c_re"], small["ssm_c_im"] = dcr, -dci
    dbbr, dbbi = _blockdiag_inv(dbb)
    to_cn = lambda t: t.transpose(1, 0, 2).reshape(SSM_GROUP, N_SSM_CH)
    dgr, dgi, dbr, dbi = _whole(_vjp_of(_ssm_prep2, 4), *sp["prep2_in"], to_cn(dbbr), to_cn(dbbi), name="ssm_prep2_bwd")
    from_cn = lambda t: t.reshape(SSM_GROUP, SSM_GROUPS, SSM_STATE).transpose(1, 2, 0)
    small["ssm_b_re"], small["ssm_b_im"] = from_cn(dbr), from_cn(dbi)
    dar, dai = _from_blocked(da_vec)
    sq = lambda t: t.reshape(SSM_GROUPS, SSM_STATE)
    dlr, dli, dldt = _whole(_vjp_of(_ssm_prep1, 3), *sp["prep1_in"], sq(dar), sq(dai), sq(dgr), sq(dgi),
                            name="ssm_prep1_bwd")
    small["ssm_lambda_re"], small["ssm_lambda_im"], small["ssm_log_dt"] = dlr, dli, dldt[:, 0]
    dyc_pre = _mm(dyc, W["w_ca_o"], tb=True, b_slots=True, name="mm_ca_o_dx")
    big["w_ca_o"] = _mm(res["yc_pre"], dyc, ta=True, out_slots=128, name="mm_ca_o_dw")
    bias = res["bias"]
    kv_band = res["kv_band"]
    dcq_, dck_band, dcv_band, dbias = _attn_bwd("chunk", (z, Z_CQ), (kv_band, 0), (kv_band, CA_WIDTH), res["lse_c"],
                                                (dyc_pre, 0), (bias,), name="chunk_bwd")
    small["ca_rel_bias"] = _relbias_reduce(dbias.transpose(1, 0, 2))
    dff_p = jnp.pad(dff.T, ((0, 0), (0, 512 - FOX_HEADS)))
    dz = jnp.concatenate([dfq, dfk, dfv, dff_p, dsu, dcq_, dck_band[CA_PAD:], dcv_band[CA_PAD:], dgf, dgs, dgc],
                         axis=1)
    dx = _mm(dz, W["w_in_p"], tb=True, add=(dr1, DN_ALPHA), name="mm_in_dx")
    big["w_in"] = _colmap(_mm(x, dz, ta=True, name="mm_in_dw"), inverse=True, name="w_in_colmap_inv")
    (db_in_p,) = _rowwise(lambda t: (jnp.sum(t, axis=0, keepdims=True),), [dz], n_red=1, name="colsum")
    small["b_in"] = _unpad_b_in(db_in_p)[0]
    return dx, big, small


def kernel(x, mem, w_in, b_in, ssm_lambda_re, ssm_lambda_im, ssm_log_dt, ssm_b_re, ssm_b_im, ssm_c_re, ssm_c_im, ssm_d, ca_rel_bias, w_fox_o, w_ssm_glu, w_ca_o, w_o, xa_wq, xa_wkv, xa_wo, ffn_w_up, ffn_conv_w, ffn_conv_b, ffn_w_down, ln_g, ln_b, loss_target, m_w_in, m_b_in, m_ssm_lambda_re, m_ssm_lambda_im, m_ssm_log_dt, m_ssm_b_re, m_ssm_b_im, m_ssm_c_re, m_ssm_c_im, m_ssm_d, m_ca_rel_bias, m_w_fox_o, m_w_ssm_glu, m_w_ca_o, m_w_o, m_xa_wq, m_xa_wkv, m_xa_wo, m_ffn_w_up, m_ffn_conv_w, m_ffn_conv_b, m_ffn_w_down, m_ln_g, m_ln_b, v_w_in, v_b_in, v_ssm_lambda_re, v_ssm_lambda_im, v_ssm_log_dt, v_ssm_b_re, v_ssm_b_im, v_ssm_c_re, v_ssm_c_im, v_ssm_d, v_ca_rel_bias, v_w_fox_o, v_w_ssm_glu, v_w_ca_o, v_w_o, v_xa_wq, v_xa_wkv, v_xa_wo, v_ffn_w_up, v_ffn_conv_w, v_ffn_conv_b, v_ffn_w_down, v_ln_g, v_ln_b):
    given = dict(locals())
    state = {pre + n: given[pre + n] for n in WEIGHTS for pre in ("", "m_", "v_")}
    mem0 = mem[0]
    b_in_p = _pad_b_in(b_in)
    layer_params = [{**{n: state[n][l] for n in REPL}, "b_in_p": b_in_p[l:l + 1]} for l in range(DEPTH)]

    _RIDERS.clear()
    h, residuals = x[0], []
    gathered = _all_gather(_layer_shards(state, 0), name="all_gather_weights")
    for l in range(DEPTH):
        riders = []
        if l + 1 < DEPTH:
            shards = _layer_shards(state, l + 1)
            riders = [(idx, _all_gather([shards[i] for i in idx], name="all_gather_weights", host=host))
                      for host, idx in GATHER_HOSTS]
        h, res = _layer_fwd(h, mem0, layer_params[l], _gathered_weights(gathered))
        residuals.append(res)
        gathered = [None] * (len(BIG) + 1)
        for idx, rider in riders:
            for i, r in zip(idx, rider.results):
                gathered[i] = r
    dh, loss_cols = _rowwise(_loss_rows, [h, loss_target[0]], n_red=1, name="loss")
    loss = lax.psum(jnp.sum(loss_cols), ("x", "y", "c"))

    outs = [None] * DEPTH
    big_out = {n: None for n in BIG}
    padded = {pre + n: _pad_shard(n, state[pre + n]) for n in BIG for pre in ("", "m_", "v_")}
    wire = [BF16] * len(BIG) + [F32]
    pending = None
    for l in reversed(range(-1, DEPTH)):
        if l >= 0:
            dh, big, small = _layer_bwd(dh, mem0, layer_params[l], residuals[l])
            reduce = _GradReduce([big[n] for n in BIG] + [_pack_small_grads(small)], wire,
                                 hosts=REDUCE_HOSTS if l > 0 else None)
        done, pending = pending, (l, reduce) if l >= 0 else None
        if done is None:
            continue
        l_done, reduce_done = done
        *recv_big, recv_small = reduce_done.result()
        for n, recv in zip(BIG, recv_big):
            if big_out[n] is None:
                big_out[n] = [lax.empty((DEPTH,) + recv.shape[1:], F32) for _ in range(4)]
            big_out[n] = _adamw(recv, *[padded[pre + n] for pre in ("", "m_", "v_")], l_done, big_out[n],
                                name="adamw_" + n)
        outs[l_done] = recv_small

    assert not _RIDERS, [r.host for r in _RIDERS]
    packed = _adamw(jnp.concatenate(outs, axis=1), *[_pack_state(state, pre) for pre in ("", "m_", "v_")],
                    name="adamw_small")
    small_out = [_unpack_state(t) for t in packed]
    result = lambda n, j: _unpad_shard(n, big_out[n][j]) if n in BIG else small_out[j][n]
    return (loss, dh[None], *[result(n, j) for j in range(4) for n in WEIGHTS])
```

```python
import functools
import math

import jax
import jax.numpy as jnp
from jax import lax
from jax.experimental import pallas as pl
from jax.experimental.pallas import tpu as pltpu

F32, BF16 = jnp.float32, jnp.bfloat16

D_MODEL = 1024
DEPTH = 4
CHUNK = 64
FOX_HEADS, FOX_HEAD_DIM, FOX_WIDTH = 8, 64, 512
SSM_GROUP, SSM_WIDTH, SSM_GROUPS, SSM_STATE = 16, 512, 32, 64
CA_HEADS, CA_HEAD_DIM, CA_WIDTH, CA_LEFT_CHUNKS = 8, 64, 512, 8
CA_BAND = (CA_LEFT_CHUNKS + 1) * CHUNK
CA_PAD = CA_LEFT_CHUNKS * CHUNK
REL_MIN, REL_MAX = -(CHUNK - 1), 4 * CHUNK
N_REL = REL_MAX - REL_MIN + 1
XA_HEADS, XA_HEAD_DIM = 4, 256
D_FF = 2816
DN_ALPHA = (2 * DEPTH) ** 0.25
LN_EPS = 1e-5
NEG_INF = -1e30
ADAM_LR, ADAM_B1, ADAM_B2, ADAM_EPS, ADAM_WD, ADAM_STEP = 0.001, 0.9, 0.999, 1e-08, 0.01, 10

N_DEV = 8
LANE = 128
N_SSM_CH = SSM_GROUPS * SSM_STATE
SCAN_CB = 256
N_IN = 6664
W_IN_SHARD, W_IN_SLOT = N_IN // N_DEV, 896
Z_W = 7168
Z_FQ, Z_FK, Z_FV, Z_FF, Z_SU, Z_CQ, Z_CK, Z_CV, Z_GF, Z_GS, Z_GC = (
    0, 512, 1024, 1536, 2048, 2560, 3072, 3584, 4096, 5120, 6144)
Z_PIECES = ((0, 512, Z_FQ), (512, 512, Z_FK), (1024, 512, Z_FV), (1536, 8, Z_FF), (1544, 512, Z_SU),
            (2056, 512, Z_CQ), (2568, 512, Z_CK), (3080, 512, Z_CV), (3592, 1024, Z_GF), (4616, 1024, Z_GS),
            (5640, 1024, Z_GC))
FF_HALF, FF_HALF_P = D_FF // N_DEV, 384
D_FF_P = N_DEV * FF_HALF_P

VMEM_LIMIT_BYTES = 56 * 1024 * 1024

BIG = {"w_in": ((1024, W_IN_SHARD), 1), "w_fox_o": ((512, 128), 1), "w_ssm_glu": ((512, 256), 1),
       "w_ca_o": ((512, 128), 1), "w_o": ((128, 1024), 0), "xa_wq": ((128, 1024), 0), "xa_wkv": ((1024, 256), 1),
       "xa_wo": ((128, 1024), 0), "ffn_w_up": ((1024, 2 * FF_HALF), 1), "ffn_w_down": ((FF_HALF, 1024), 0)}
SMALL = {"ca_rel_bias": (8, 320), "ffn_conv_w": (3, D_FF), "ln_g": (3, 1024), "ln_b": (3, 1024)}
REPL = {"b_in": (N_IN,), "ssm_lambda_re": (32, 64), "ssm_lambda_im": (32, 64), "ssm_log_dt": (32,),
        "ssm_b_re": (32, 64, 16), "ssm_b_im": (32, 64, 16), "ssm_c_re": (32, 16, 64), "ssm_c_im": (32, 16, 64),
        "ssm_d": (512,), "ffn_conv_b": (D_FF,)}
WEIGHTS = ("w_in", "b_in", "ssm_lambda_re", "ssm_lambda_im", "ssm_log_dt", "ssm_b_re", "ssm_b_im", "ssm_c_re",
           "ssm_c_im", "ssm_d", "ca_rel_bias", "w_fox_o", "w_ssm_glu", "w_ca_o", "w_o", "xa_wq", "xa_wkv", "xa_wo",
           "ffn_w_up", "ffn_conv_w", "ffn_conv_b", "ffn_w_down", "ln_g", "ln_b")
LANES = 1024
PACK_ROWS = 256
SMALL_ROWS = 8


def _w_in_segments():
    segs = []
    for src, width, dst in Z_PIECES:
        n = src
        while n < src + width:
            d = n // W_IN_SHARD
            end = min(src + width, (d + 1) * W_IN_SHARD)
            segs.append((W_IN_SLOT * d + n - W_IN_SHARD * d, dst + n - src, end - n))
            n = end
    return tuple(segs)


W_IN_SEGS = _w_in_segments()


def _pallas(body, **kw):
    return pl.pallas_call(body, **kw)


def _params(sem):
    return pltpu.CompilerParams(dimension_semantics=sem, vmem_limit_bytes=VMEM_LIMIT_BYTES)


class _Rider:
    def __init__(self, host, ins, out_shapes, sem_shapes, start, finish, then=None):
        self.host, self.ins, self.out_shapes, self.sem_shapes = host, list(ins), list(out_shapes), list(sem_shapes)
        self.start, self.finish, self.then, self.results = start, finish, then, None


_RIDERS = []


def _pcall(body, *, name, **kw):
    rider = next((r for r in _RIDERS if r.host == name), None)
    if rider is None:
        return _pallas(body, name=name, **kw)
    _RIDERS.remove(rider)
    grid, in_specs, scratch = kw["grid"], list(kw["in_specs"]), list(kw.get("scratch_shapes", ()))
    single = not isinstance(kw["out_shape"], (list, tuple))
    out_specs = [kw["out_specs"]] if single else list(kw["out_specs"])
    out_shape = [kw["out_shape"]] if single else list(kw["out_shape"])
    n_in, n_out, n_scr = len(in_specs), len(out_specs), len(scratch)
    r_in, r_out = len(rider.ins), len(rider.out_shapes)

    def fused(*refs):
        a, ra = refs[:n_in], refs[n_in:n_in + r_in]
        o, ro = refs[n_in + r_in:n_in + r_in + n_out], refs[n_in + r_in + n_out:n_in + r_in + n_out + r_out]
        scr, sems = refs[n_in + r_in + n_out + r_out:][:n_scr], refs[n_in + r_in + n_out + r_out + n_scr:]
        ids = [pl.program_id(d) for d in range(len(grid))]
        first = functools.reduce(jnp.logical_and, [i == 0 for i in ids])
        last = functools.reduce(jnp.logical_and, [i == g - 1 for i, g in zip(ids, grid)])

        @pl.when(first)
        def _():
            rider.start(ra, ro, sems)

        body(*a, *o, *scr)

        @pl.when(last)
        def _():
            rider.finish(ra, ro, sems)

    hbm = pl.BlockSpec(memory_space=pl.ANY)
    call = _pallas(fused, name=name, grid=grid, in_specs=in_specs + [hbm] * r_in, out_specs=out_specs + [hbm] * r_out,
                   out_shape=out_shape + rider.out_shapes, scratch_shapes=scratch + rider.sem_shapes,
                   compiler_params=_params(("arbitrary",) * len(grid)))

    def run(*operands):
        outs = call(*operands, *rider.ins)
        rider.results = list(outs[n_out:])
        if rider.then is not None:
            rider.then(rider.results)
        return outs[0] if single else list(outs[:n_out])

    return run


def _pick(dim, prefs):
    for p in prefs:
        if dim % p == 0:
            return p
    return dim


def _mm(a, b, *, ta=False, tb=False, bias=None, add=None, a_off=0, a_cols=None, b_slots=False, out_slots=None,
        name, out_dtype=F32):
    a_cols = a_cols if a_cols is not None else a.shape[1]
    M, K = (a_cols, a.shape[0]) if ta else (a.shape[0], a_cols)
    tm = _pick(M, (1024, 512, 256, 128))
    if ta:
        tk = _pick(K, (2048, 1024, 512, 256))
    elif b_slots and tb:
        tk = _pick(b.shape[2], (1024, 768, 512, 256, 128))
    else:
        tk = K if K <= 3072 else _pick(K, (1024, 512, 256, 128))
    nk = K // tk
    if b_slots:
        ns = b.shape[2]
        if tb:
            N = b.shape[1]
            tn = _pick(N, (512, 256, 128))
            per = ns // tk
            b_spec = pl.BlockSpec((None, tn, tk), lambda i, j, k: (k // per, j, k % per))
            b_dim = 1
            assert N_DEV * ns == K
        else:
            N = N_DEV * ns
            tn = _pick(ns, (512, 256, 128))
            per = ns // tn
            b_spec = pl.BlockSpec((None, tk, tn), lambda i, j, k: (j // per, k, j % per))
            b_dim = 0
            assert b.shape[1] == K
    else:
        N = b.shape[0] if tb else b.shape[1]
        assert (b.shape[1] if tb else b.shape[0]) == K, (a.shape, b.shape, ta, tb)
        tn = _pick(N if out_slots is None else out_slots, (512, 256, 128))
        if tb:
            b_spec = pl.BlockSpec((tn, tk), lambda i, j, k: (j, k))
            b_dim = 1
        else:
            b_spec = pl.BlockSpec((tk, tn), lambda i, j, k: (k, j))
            b_dim = 0
    if ta:
        assert a_off % tm == 0
        a_spec = pl.BlockSpec((tk, tm), lambda i, j, k: (k, i + a_off // tm))
        a_dim = 0
    else:
        assert a_off % tk == 0
        a_spec = pl.BlockSpec((tm, tk), lambda i, j, k: (i, k + a_off // tk))
        a_dim = 1
    cache_at = ta and nk == 1
    dims = (((1 if cache_at else a_dim,), (b_dim,)), ((), ()))
    ins, specs = [a, b], [a_spec, b_spec]
    if bias is not None:
        ins.append(bias)
        specs.append(pl.BlockSpec((1, tn), lambda i, j, k: (0, j)))
    add_scale = None
    if add is not None:
        ins.append(add[0])
        add_scale = add[1]
        specs.append(pl.BlockSpec((tm, tn), lambda i, j, k: (i, j)))
    if out_slots is None:
        out_spec = pl.BlockSpec((tm, tn), lambda i, j, k: (i, j))
        out_shape = jax.ShapeDtypeStruct((M, N), out_dtype)
    else:
        assert N == N_DEV * out_slots
        per_o = out_slots // tn
        out_spec = pl.BlockSpec((None, tm, tn), lambda i, j, k: (j // per_o, i, j % per_o))
        out_shape = jax.ShapeDtypeStruct((N_DEV, M, out_slots), out_dtype)

    def body(*refs):
        a_ref, b_ref = refs[0], refs[1]
        pos = 2
        bias_ref = add_ref = None
        if bias is not None:
            bias_ref = refs[pos]
            pos += 1
        if add is not None:
            add_ref = refs[pos]
            pos += 1
        o_ref = refs[pos]
        acc_ref = refs[pos + 1] if nk > 1 else None
        if cache_at:
            at_ref = refs[pos + 1]

            @pl.when(pl.program_id(1) == 0)
            def _():
                step = min(tk, 256)
                for c in range(0, tk, step):
                    at_ref[:, c:c + step] = a_ref[c:c + step, :].T.astype(BF16)

            lhs = at_ref[...]
        else:
            lhs = a_ref[...].astype(BF16)
        part = lax.dot_general(lhs, b_ref[...].astype(BF16), dims, preferred_element_type=F32)

        def finish(acc):
            if bias_ref is not None:
                acc = acc + bias_ref[...]
            if add_ref is not None:
                acc = acc + add_scale * add_ref[...]
            o_ref[...] = acc.astype(out_dtype)

        if nk == 1:
            finish(part)
        else:
            k = pl.program_id(2)

            @pl.when(k == 0)
            def _():
                acc_ref[...] = part

            @pl.when(k > 0)
            def _():
                acc_ref[...] += part

            @pl.when(k == nk - 1)
            def _():
                finish(acc_ref[...])

    return _pcall(
        body, name=name, grid=(M // tm, N // tn, nk), in_specs=specs, out_specs=out_spec, out_shape=out_shape,
        scratch_shapes=[pltpu.VMEM((tm, tn), F32)] if nk > 1 else [pltpu.VMEM((tm, tk), BF16)] if cache_at else [],
        compiler_params=_params(("parallel", "arbitrary", "arbitrary")),
    )(*ins)


SSM_BD = 4
SSM_BD_IN, SSM_BD_ST = SSM_WIDTH // SSM_BD, 2 * N_SSM_CH // SSM_BD


def _mm_bd(form, a, b, *, a_off=0, add=None, name):
    S = a.shape[0]
    off = a_off // SSM_BD_IN
    if form == "dw":
        def body(a_ref, b_ref, o_ref):
            at = a_ref[...].T.astype(BF16)
            o_ref[...] = lax.dot_general(at, b_ref[...].astype(BF16), _NN, preferred_element_type=F32)

        return _pcall(body, name=name, grid=(SSM_BD,),
                      in_specs=[pl.BlockSpec((S, SSM_BD_IN), lambda j: (0, off + j)),
                                pl.BlockSpec((S, SSM_BD_ST), lambda j: (0, j))],
                      out_specs=pl.BlockSpec((None, SSM_BD_IN, SSM_BD_ST), lambda j: (j, 0, 0)),
                      out_shape=jax.ShapeDtypeStruct((SSM_BD, SSM_BD_IN, SSM_BD_ST), F32),
                      compiler_params=_params(("parallel",)))(a, b)
    tm = _pick(S, (1024, 512, 256))
    wide, narrow = (SSM_BD_ST, SSM_BD_IN) if form == "in" else (SSM_BD_IN, SSM_BD_ST)
    dims = _NN if form == "in" else _NT

    def body(a_ref, w_ref, *rest):
        acc = lax.dot_general(a_ref[...].astype(BF16), w_ref[...].astype(BF16), dims, preferred_element_type=F32)
        if add is not None:
            acc = acc + add[1] * rest[0][...]
        rest[-1][...] = acc

    specs = [pl.BlockSpec((tm, narrow), lambda i, j: (i, off + j)),
             pl.BlockSpec((None, SSM_BD_IN, SSM_BD_ST), lambda i, j: (j, 0, 0))]
    ins = [a, b]
    if add is not None:
        specs.append(pl.BlockSpec((tm, wide), lambda i, j: (i, j)))
        ins.append(add[0])
    return _pcall(body, name=name, grid=(S // tm, SSM_BD), in_specs=specs,
                  out_specs=pl.BlockSpec((tm, wide), lambda i, j: (i, j)),
                  out_shape=jax.ShapeDtypeStruct((S, SSM_BD * wide), F32),
                  compiler_params=_params(("parallel", "parallel")))(*ins)


class Win:
    def __init__(self, arr, width, blk):
        self.arr, self.width, self.blk = arr, width, blk


def _rowwise(fn, rows, vecs=(), *, n_red=0, tr=256, name):
    wins = [r if isinstance(r, Win) else Win(r, r.shape[1], 0) for r in rows]
    S = wins[0].arr.shape[0]
    tr = min(tr, S)
    tile_args = [jax.ShapeDtypeStruct((tr, w.width), w.arr.dtype) for w in wins]
    tile_args += [jax.ShapeDtypeStruct(v.shape, v.dtype) for v in vecs]
    outs = jax.eval_shape(fn, *tile_args)
    n_row = len(outs) - n_red
    specs = [pl.BlockSpec((tr, w.width), functools.partial(lambda i, b: (i, b), b=w.blk)) for w in wins]
    specs += [pl.BlockSpec(v.shape, functools.partial(lambda i, nd: (0,) * nd, nd=v.ndim)) for v in vecs]
    out_specs = [pl.BlockSpec((tr, o.shape[1]), lambda i: (i, 0)) for o in outs[:n_row]]
    out_specs += [pl.BlockSpec(o.shape, functools.partial(lambda i, nd: (0,) * nd, nd=len(o.shape))) for o in outs[n_row:]]
    out_shape = [jax.ShapeDtypeStruct((S, o.shape[1]), o.dtype) for o in outs[:n_row]]
    out_shape += [jax.ShapeDtypeStruct(o.shape, o.dtype) for o in outs[n_row:]]
    n_in = len(wins) + len(vecs)

    def body(*refs):
        res = fn(*[r[...] for r in refs[:n_in]])
        o_refs = refs[n_in:]
        for o_ref, r in zip(o_refs[:n_row], res[:n_row]):
            o_ref[...] = r.astype(o_ref.dtype)
        i = pl.program_id(0)
        for o_ref, r in zip(o_refs[n_row:], res[n_row:]):
            @pl.when(i == 0)
            def _(o_ref=o_ref, r=r):
                o_ref[...] = r

            @pl.when(i > 0)
            def _(o_ref=o_ref, r=r):
                o_ref[...] += r

    return _pcall(
        body, name=name, grid=(S // tr,), in_specs=specs, out_specs=out_specs, out_shape=out_shape,
        compiler_params=_params(("arbitrary",)),
    )(*[w.arr for w in wins], *vecs)


def _whole(fn, *arrays, name):
    outs = jax.eval_shape(fn, *arrays)
    n_in = len(arrays)

    def body(*refs):
        res = fn(*[r[...] for r in refs[:n_in]])
        for o_ref, r in zip(refs[n_in:], res):
            o_ref[...] = r

    vm = pl.BlockSpec(memory_space=pltpu.VMEM)
    return _pcall(body, name=name, in_specs=[vm] * n_in, out_specs=[vm] * len(outs),
                  out_shape=[jax.ShapeDtypeStruct(o.shape, o.dtype) for o in outs])(*arrays)


def _split3(x):
    hi = x.astype(BF16)
    r = x - hi.astype(F32)
    mid = r.astype(BF16)
    lo = (r - mid.astype(F32)).astype(BF16)
    return hi, mid, lo


def _dot3(x, onehot, dims):
    return sum(lax.dot_general(t, onehot, dims, preferred_element_type=F32) for t in _split3(x))


_NT = (((1,), (1,)), ((), ()))
_NN = (((1,), (0,)), ((), ()))
_TN = (((0,), (0,)), ((), ()))


def _colmap(x, *, inverse, name):
    R = x.shape[0] if inverse else x.shape[1]
    tr = 256
    per = W_IN_SLOT // LANE
    n_out = N_DEV * per if inverse else Z_W // LANE
    segs = [(p, q, n) for q, p, n in W_IN_SEGS] if inverse else list(W_IN_SEGS)

    def body(x_ref, o_ref):
        ia = lax.broadcasted_iota(jnp.int32, (LANE, LANE), 0)
        ib = lax.broadcasted_iota(jnp.int32, (LANE, LANE), 1)

        def src_block(i):
            if inverse:
                return x_ref[:, i * LANE:(i + 1) * LANE]
            return x_ref[i // per, :, (i % per) * LANE:(i % per + 1) * LANE]

        for jb in range(n_out):
            acc = None
            for s0, d0, n in segs:
                lo, hi = max(d0, jb * LANE), min(d0 + n, (jb + 1) * LANE)
                if lo >= hi:
                    continue
                delta = d0 - s0
                for i in range((lo - delta) // LANE, (hi - delta - 1) // LANE + 1):
                    shift = jb * LANE - i * LANE - delta
                    sel = ((ia - ib == shift) & (ib >= lo - jb * LANE) & (ib < hi - jb * LANE)).astype(BF16)
                    blk = src_block(i)
                    part = _dot3(blk, sel, _NN) if inverse else lax.dot_general(blk, sel, _NN, preferred_element_type=F32)
                    acc = part if acc is None else acc + part
            if acc is None:
                acc = jnp.zeros((tr, LANE), F32)
            if inverse:
                o_ref[jb // per, :, (jb % per) * LANE:(jb % per + 1) * LANE] = acc
            else:
                o_ref[:, jb * LANE:(jb + 1) * LANE] = acc.astype(BF16)

    slot_spec = pl.BlockSpec((N_DEV, tr, W_IN_SLOT), lambda i: (0, i, 0))
    flat_spec = pl.BlockSpec((tr, Z_W), lambda i: (i, 0))
    if inverse:
        return _pcall(body, name=name, grid=(R // tr,), in_specs=[flat_spec], out_specs=slot_spec,
                      out_shape=jax.ShapeDtypeStruct((N_DEV, R, W_IN_SLOT), F32), compiler_params=_params(("parallel",)))(x)
    return _pcall(body, name=name, grid=(R // tr,), in_specs=[slot_spec], out_specs=flat_spec,
                  out_shape=jax.ShapeDtypeStruct((R, Z_W), BF16), compiler_params=_params(("parallel",)))(x)


def _log_sigmoid(x):
    return jnp.minimum(x, 0.0) - jnp.log(1.0 + jnp.exp(-jnp.abs(x)))


def _cum_heads(f, dcum=None, *, name):
    H, S = f.shape
    tn = min(512, S)
    rev = dcum is not None

    def body(*refs):
        j = pl.program_id(0)
        s_idx = lax.broadcasted_iota(jnp.int32, (S, tn), 0)
        t_idx = lax.broadcasted_iota(jnp.int32, (S, tn), 1) + j * tn
        if not rev:
            f_ref, o_ref = refs
            tri = (s_idx <= t_idx).astype(BF16)
            o_ref[...] = _dot3(_log_sigmoid(f_ref[...]), tri, _NN)
        else:
            fj_ref, d_ref, o_ref = refs
            tri = (s_idx >= t_idx).astype(BF16)
            o_ref[...] = _dot3(d_ref[...], tri, _NN) * jax.nn.sigmoid(-fj_ref[...])

    full = pl.BlockSpec((H, S), lambda j: (0, 0))
    blk = pl.BlockSpec((H, tn), lambda j: (0, j))
    ins, specs = ([f], [full]) if not rev else ([f, dcum], [blk, full])
    return _pcall(body, name=name, grid=(S // tn,), in_specs=specs, out_specs=blk,
                  out_shape=jax.ShapeDtypeStruct((H, S), F32), compiler_params=_params(("arbitrary",)))(*ins)


def _rel_onehot(qi, transposed):
    shape = (N_REL, CA_BAND) if transposed else (CA_BAND, N_REL)
    kk = lax.broadcasted_iota(jnp.int32, shape, 1 if transposed else 0)
    rr = lax.broadcasted_iota(jnp.int32, shape, 0 if transposed else 1)
    idx = jnp.clip(CA_PAD + qi - kk, REL_MIN, REL_MAX) - REL_MIN
    return (idx == rr).astype(BF16)


def _relbias_expand(rb):
    rows = 8

    def body(rb_ref, o_ref):
        for r in range(rows):
            o_ref[r] = _dot3(rb_ref[...], _rel_onehot(pl.program_id(0) * rows + r, True), _NN)

    return _pcall(body, name="relbias_expand", grid=(CHUNK // rows,),
                  in_specs=[pl.BlockSpec((CA_HEADS, N_REL), lambda q: (0, 0))],
                  out_specs=pl.BlockSpec((rows, CA_HEADS, CA_BAND), lambda q: (q, 0, 0)),
                  out_shape=jax.ShapeDtypeStruct((CHUNK, CA_HEADS, CA_BAND), F32),
                  compiler_params=_params(("arbitrary",)))(rb)


def _relbias_reduce(db):
    rows = 8

    def body(db_ref, o_ref):
        q = pl.program_id(0)
        part = sum(_dot3(db_ref[r], _rel_onehot(q * rows + r, False), _NN) for r in range(rows))

        @pl.when(q == 0)
        def _():
            o_ref[...] = part

        @pl.when(q > 0)
        def _():
            o_ref[...] += part

    return _pcall(body, name="relbias_reduce", grid=(CHUNK // rows,),
                  in_specs=[pl.BlockSpec((rows, CA_HEADS, CA_BAND), lambda q: (q, 0, 0))],
                  out_specs=pl.BlockSpec((CA_HEADS, N_REL), lambda q: (0, 0)),
                  out_shape=jax.ShapeDtypeStruct((CA_HEADS, N_REL), F32),
                  compiler_params=_params(("arbitrary",)))(db)


def _attn_cfg(mode, S):
    if mode == "fox":
        return min(256, S), FOX_HEAD_DIM ** -0.5
    if mode == "chunk":
        return CHUNK, CA_HEAD_DIM ** -0.5
    return min(512, S), XA_HEAD_DIM ** -0.5


def _visible(mode, i, tq, nk):
    if mode == "fox":
        row = lax.broadcasted_iota(jnp.int32, (tq, nk), 0) + i * tq
        return row >= lax.broadcasted_iota(jnp.int32, (tq, nk), 1)
    if mode == "chunk":
        return lax.broadcasted_iota(jnp.int32, (tq, nk), 1) + i * CHUNK >= CA_PAD
    return None


def _scores(mode, qs, kb, extra, visible):
    s = lax.dot_general(qs, kb, _NT, preferred_element_type=F32)
    if mode == "fox":
        cq, ck = extra
        s = jnp.where(visible, s + cq - ck, NEG_INF)
    elif mode == "chunk":
        (bias,) = extra
        s = jnp.where(visible, s + bias, NEG_INF)
    return s


class _AttnPlan:
    def __init__(self, mode, q, k, pp):
        self.mode, self.pp = mode, pp
        self.D, self.hpb, self.bw = (XA_HEAD_DIM, 1, XA_HEAD_DIM) if mode == "xa" else (64, 2, LANE)
        self.S, self.Sk = q[0].shape[0], k[0].shape[0]
        self.H = (XA_HEADS if mode == "xa" else FOX_HEADS)
        self.W = self.H * self.D
        self.gw = self.bw * pp
        self.hpg = self.hpb * pp
        self.tq, self.scale = _attn_cfg(mode, self.S)
        self.grid = (self.W // self.gw, self.S // self.tq)

    def rows(self, win):
        off = win[1] // self.gw
        return pl.BlockSpec((self.tq, self.gw), lambda g, i: (i, off + g))

    def cols(self, win):
        off = win[1] // self.gw
        return pl.BlockSpec((win[0].shape[0], self.gw), lambda g, i: (0, off + g))

    def extras(self):
        if self.mode == "fox":
            return [pl.BlockSpec((self.hpg, self.tq, 1), lambda g, i: (g, i, 0)),
                    pl.BlockSpec((self.hpg, 1, self.Sk), lambda g, i: (g, 0, 0))]
        if self.mode == "chunk":
            return [pl.BlockSpec((self.hpg, CHUNK, CA_BAND), lambda g, i: (g, 0, 0))]
        return []

    def per_row(self):
        return pl.BlockSpec((self.hpg, self.tq, 1), lambda g, i: (g, i, 0))

    def lanes(self, p):
        return slice(p * self.bw, (p + 1) * self.bw)

    def keys(self, i, ref, p):
        if self.mode == "chunk":
            return ref[pl.ds(pl.multiple_of(i * CHUNK, CHUNK), CA_BAND), self.lanes(p)]
        return ref[:, self.lanes(p)]

    def head(self, x, hh):
        if self.hpb == 1:
            return x
        lane = lax.broadcasted_iota(jnp.int32, x.shape, 1)
        return jnp.where(lane // self.D == hh, x, jnp.zeros_like(x))


def _attn_fwd(mode, q, k, v, extra=(), *, name):
    pl_ = _AttnPlan(mode, q, k, pp=4)
    n_ex = len(extra)
    out = (None, 0)

    def body(*refs):
        q_ref, k_ref, v_ref = refs[:3]
        o_ref, lse_ref = refs[3 + n_ex:]
        i = pl.program_id(1)
        visible = _visible(mode, i, pl_.tq, CA_BAND if mode == "chunk" else pl_.Sk)
        for p in range(pl_.pp):
            qs = q_ref[:, pl_.lanes(p)].astype(BF16) * pl_.scale
            kp = pl_.keys(i, k_ref, p).astype(BF16)
            vp = pl_.keys(i, v_ref, p).astype(BF16)
            acc = None
            for hh in range(pl_.hpb):
                h = p * pl_.hpb + hh
                ex = [r[h] for r in refs[3:3 + n_ex]]
                s = _scores(mode, pl_.head(qs, hh), kp, ex, visible)
                m = jnp.max(s, axis=1, keepdims=True)
                e = jnp.exp(s - m)
                l = jnp.sum(e, axis=1, keepdims=True)
                part = lax.dot_general((e * (1.0 / l)).astype(BF16), pl_.head(vp, hh), _NN,
                                       preferred_element_type=F32)
                acc = part if acc is None else acc + part
                lse_ref[h] = m + jnp.log(l)
            o_ref[:, pl_.lanes(p)] = acc

    return _pcall(
        body, name=name, grid=pl_.grid, in_specs=[pl_.rows(q), pl_.cols(k), pl_.cols(v)] + pl_.extras(),
        out_specs=[pl_.rows(out), pl_.per_row()],
        out_shape=[jax.ShapeDtypeStruct((pl_.S, pl_.W), F32), jax.ShapeDtypeStruct((pl_.H, pl_.S, 1), F32)],
        compiler_params=_params(("parallel", "arbitrary")),
    )(q[0], k[0], v[0], *extra)


def _attn_bwd(mode, q, k, v, lse, do, extra=(), *, name):
    pl_ = _AttnPlan(mode, q, k, pp=2 if mode == "fox" else 4)
    H, S, Sk, W = pl_.H, pl_.S, pl_.Sk, pl_.W
    n_ex = len(extra)
    out = (None, 0)
    kv_out = pl.BlockSpec((Sk, pl_.gw), lambda g, i: (0, g))
    ex_specs = pl_.extras()
    out_specs = [pl_.rows(out), kv_out, kv_out]
    out_shape = [jax.ShapeDtypeStruct((S, W), F32), jax.ShapeDtypeStruct((Sk, W), F32),
                 jax.ShapeDtypeStruct((Sk, W), F32)]
    if mode == "fox":
        out_specs += [pl_.per_row(), ex_specs[1]]
        out_shape += [jax.ShapeDtypeStruct((H, S, 1), F32), jax.ShapeDtypeStruct((H, 1, Sk), F32)]
    elif mode == "chunk":
        out_specs += [ex_specs[0]]
        out_shape += [jax.ShapeDtypeStruct((H, CHUNK, CA_BAND), F32)]

    def body(*refs):
        q_ref, k_ref, v_ref, lse_ref, do_ref = refs[:5]
        dq_ref, dk_ref, dv_ref = refs[5 + n_ex:8 + n_ex]
        rest = refs[8 + n_ex:]
        i = pl.program_id(1)

        @pl.when(i == 0)
        def _():
            dk_ref[...] = jnp.zeros_like(dk_ref)
            dv_ref[...] = jnp.zeros_like(dv_ref)
            if mode == "fox":
                rest[1][...] = jnp.zeros_like(rest[1])
            elif mode == "chunk":
                rest[0][...] = jnp.zeros_like(rest[0])

        visible = _visible(mode, i, pl_.tq, CA_BAND if mode == "chunk" else pl_.Sk)
        for p in range(pl_.pp):
            lanes = pl_.lanes(p)
            qs = q_ref[:, lanes].astype(BF16) * pl_.scale
            kp = pl_.keys(i, k_ref, p).astype(BF16)
            vp = pl_.keys(i, v_ref, p).astype(BF16)
            dop = do_ref[:, lanes].astype(BF16)
            dq = dk_part = dv_part = None
            for hh in range(pl_.hpb):
                h = p * pl_.hpb + hh
                ex = [r[h] for r in refs[5:5 + n_ex]]
                qh, doh = pl_.head(qs, hh), pl_.head(dop, hh)
                s = _scores(mode, qh, kp, ex, visible)
                pr = jnp.exp(s - lse_ref[h])
                dp = lax.dot_general(doh, vp, _NT, preferred_element_type=F32)
                ds = pr * (dp - jnp.sum(dp * pr, axis=1, keepdims=True))
                dsb = ds.astype(BF16)
                parts = (lax.dot_general(dsb, pl_.head(kp, hh), _NN, preferred_element_type=F32) * pl_.scale,
                         lax.dot_general(dsb, qh, _TN, preferred_element_type=F32),
                         lax.dot_general(pr.astype(BF16), doh, _TN, preferred_element_type=F32))
                dq, dk_part, dv_part = parts if dq is None else (dq + parts[0], dk_part + parts[1], dv_part + parts[2])
                if mode == "chunk":
                    rest[0][h] += ds
                if mode == "fox":
                    rest[0][h] = jnp.sum(ds, axis=1, keepdims=True)
                    rest[1][h] += -jnp.sum(ds, axis=0, keepdims=True)
            dq_ref[:, lanes] = dq
            if mode == "chunk":
                win = pl.ds(pl.multiple_of(i * CHUNK, CHUNK), CA_BAND)
                dk_ref[win, lanes] += dk_part
                dv_ref[win, lanes] += dv_part
            else:
                dk_ref[:, lanes] += dk_part
                dv_ref[:, lanes] += dv_part

    return _pcall(
        body, name=name, grid=pl_.grid,
        in_specs=[pl_.rows(q), pl_.cols(k), pl_.cols(v), pl_.per_row(), pl_.rows(do)] + ex_specs,
        out_specs=out_specs, out_shape=out_shape,
        compiler_params=_params(("parallel", "arbitrary")),
    )(q[0], k[0], v[0], lse, do[0], *extra)


def _scan(x, a, h=None, *, name):
    S = x.shape[0]
    CB = SCAN_CB
    rev = h is not None
    n_grp = S // 8

    def body(*refs):
        if rev:
            x_ref, a_ref, h_ref, o_ref, da_ref = refs
        else:
            x_ref, a_ref, o_ref = refs
        ar = a_ref[:, :CB]
        ai = -a_ref[:, CB:] if rev else a_ref[:, CB:]
        zero = jnp.zeros((1, CB), F32)

        def group(g, carry):
            base = pl.multiple_of((n_grp - 1 - g) * 8 if rev else g * 8, 8)
            for j in (range(7, -1, -1) if rev else range(8)):
                t = base + j
                if rev:
                    hr, hi, dar, dai = carry
                else:
                    hr, hi = carry
                xr = x_ref[pl.ds(t, 1), :CB]
                xi = x_ref[pl.ds(t, 1), CB:]
                hr, hi = ar * hr - ai * hi + xr, ar * hi + ai * hr + xi
                o_ref[pl.ds(t, 1), :CB] = hr
                o_ref[pl.ds(t, 1), CB:] = hi
                if rev:
                    tp = jnp.maximum(t - 1, 0)
                    live = (t > 0).astype(F32)
                    pr = h_ref[pl.ds(tp, 1), :CB] * live
                    pi = h_ref[pl.ds(tp, 1), CB:] * live
                    carry = (hr, hi, dar + hr * pr + hi * pi, dai + hi * pr - hr * pi)
                else:
                    carry = (hr, hi)
            return carry

        if rev:
            _, _, dar, dai = lax.fori_loop(0, n_grp, group, (zero, zero, zero, zero))
            da_ref[:, :CB] = dar
            da_ref[:, CB:] = dai
        else:
            lax.fori_loop(0, n_grp, group, (zero, zero))

    big = pl.BlockSpec((S, 2 * CB), lambda c: (0, c))
    vec = pl.BlockSpec((1, 2 * CB), lambda c: (0, c))
    n_blk = x.shape[1] // (2 * CB)
    if rev:
        return _pcall(body, name=name, grid=(n_blk,), in_specs=[big, vec, big], out_specs=[big, vec],
                      out_shape=[jax.ShapeDtypeStruct(x.shape, F32), jax.ShapeDtypeStruct(a.shape, F32)],
                      compiler_params=_params(("parallel",)))(x, a, h)
    return _pcall(body, name=name, grid=(n_blk,), in_specs=[big, vec], out_specs=big,
                  out_shape=jax.ShapeDtypeStruct(x.shape, F32), compiler_params=_params(("parallel",)))(x, a)


def _ssm_prep1(lr_, li, ldt):
    lr = jnp.minimum(lr_, -1e-4)
    dt = jnp.exp(ldt)
    mag = jnp.exp(lr * dt)
    ar = mag * jnp.cos(li * dt)
    ai = mag * jnp.sin(li * dt)
    den = lr * lr + li * li
    gr = ((ar - 1.0) * lr + ai * li) / den
    gi = (ai * lr - (ar - 1.0) * li) / den
    return ar, ai, gr, gi


def _ssm_prep2(gr, gi, br, bi):
    return gr * br - gi * bi, gr * bi + gi * br


def _vjp_of(fn, n_in):
    def bwd(*args):
        cts = args[n_in:]
        return jax.vjp(fn, *args[:n_in])[1](cts[0] if len(cts) == 1 else tuple(cts))
    return bwd


def _to_blocked(r, i):
    lead = r.shape[:-1]
    t = jnp.stack([r.reshape(lead + (N_SSM_CH // SCAN_CB, SCAN_CB)), i.reshape(lead + (N_SSM_CH // SCAN_CB, SCAN_CB))],
                  axis=-2)
    return t.reshape(lead + (2 * N_SSM_CH,))


def _from_blocked(m):
    lead = m.shape[:-1]
    t = m.reshape(lead + (N_SSM_CH // SCAN_CB, 2, SCAN_CB))
    return t[..., 0, :].reshape(lead + (N_SSM_CH,)), t[..., 1, :].reshape(lead + (N_SSM_CH,))


_GPB = SSM_GROUPS // SSM_BD
_GPS = SCAN_CB // SSM_STATE


def _bd_eye():
    return jnp.eye(_GPB, dtype=F32).reshape(_GPB, _GPB // _GPS, _GPS)


def _blockdiag(r, i):
    v = jnp.stack([r, i]).reshape(2, SSM_BD, _GPB, SSM_GROUP, SSM_STATE)
    return jnp.einsum("qjgcp,gsh->jgcsqhp", v, _bd_eye()).reshape(SSM_BD, SSM_BD_IN, SSM_BD_ST)


def _blockdiag_inv(m):
    d = m.reshape(SSM_BD, _GPB, SSM_GROUP, _GPB // _GPS, 2, _GPS, SSM_STATE)
    v = jnp.einsum("jgcsqhp,gsh->qjgcp", d, _bd_eye()).reshape(2, SSM_GROUPS, SSM_GROUP, SSM_STATE)
    return v[0], v[1]


def _shift_rows(x, n):
    S = x.shape[0]
    row = lax.broadcasted_iota(jnp.int32, x.shape, 0)
    if n > 0:
        return jnp.where(row >= n, pltpu.roll(x, n, 0), 0.0)
    return jnp.where(row < S + n, pltpu.roll(x, S + n, 0), 0.0)


def _bf(x):
    return x.astype(BF16).astype(F32)


def _conv_pre(a, w, b):
    ab, wb = _bf(a), _bf(w)
    return wb[2:3] * ab + wb[1:2] * _shift_rows(ab, 1) + wb[0:1] * _shift_rows(ab, 2) + b


def _ffn_mid(up, conv_w, conv_b, dh=None, *, name):
    S = up.shape[0]
    tn = LANE
    nb = D_FF_P // tn
    rev = dh is not None

    def body(*refs):
        if not rev:
            a_ref, g_ref, w_ref, b_ref, o_ref = refs
            o_ref[...] = jax.nn.gelu(_conv_pre(a_ref[...], w_ref[...], b_ref[...])) * g_ref[...]
            return
        a_ref, g_ref, w_ref, b_ref, dh_ref, dup_a_ref, dup_g_ref, dw_ref, db_ref = refs
        a, w, dh_ = a_ref[...], w_ref[...], dh_ref[...]
        pre = _conv_pre(a, w, b_ref[...])
        gl, gelu_vjp = jax.vjp(jax.nn.gelu, pre)
        dup_g_ref[...] = dh_ * gl
        (dpre,) = gelu_vjp(dh_ * g_ref[...])
        db_ref[...] = jnp.sum(dpre, axis=0, keepdims=True)
        dpb, ab, wb = _bf(dpre), _bf(a), _bf(w)
        dup_a_ref[...] = wb[2:3] * dpb + wb[1:2] * _shift_rows(dpb, -1) + wb[0:1] * _shift_rows(dpb, -2)
        dw_ref[2:3, :] = jnp.sum(dpb * ab, axis=0, keepdims=True)
        dw_ref[1:2, :] = jnp.sum(dpb * _shift_rows(ab, 1), axis=0, keepdims=True)
        dw_ref[0:1, :] = jnp.sum(dpb * _shift_rows(ab, 2), axis=0, keepdims=True)

    a_spec = pl.BlockSpec((S, tn), lambda j: (0, j))
    g_spec = pl.BlockSpec((S, tn), lambda j: (0, j + nb))
    w_spec = pl.BlockSpec((3, tn), lambda j: (0, j))
    b_spec = pl.BlockSpec((1, tn), lambda j: (0, j))
    if not rev:
        return _pcall(body, name=name, grid=(nb,), in_specs=[a_spec, g_spec, w_spec, b_spec], out_specs=a_spec,
                      out_shape=jax.ShapeDtypeStruct((S, D_FF_P), F32), compiler_params=_params(("parallel",)))(
                          up, up, conv_w, conv_b)
    return _pcall(body, name=name, grid=(nb,), in_specs=[a_spec, g_spec, w_spec, b_spec, a_spec],
                  out_specs=[a_spec, a_spec, w_spec, b_spec],
                  out_shape=[jax.ShapeDtypeStruct((S, D_FF_P), F32), jax.ShapeDtypeStruct((S, D_FF_P), F32),
                             jax.ShapeDtypeStruct((3, D_FF_P), F32), jax.ShapeDtypeStruct((1, D_FF_P), F32)],
                  compiler_params=_params(("parallel",)))(up, up, conv_w, conv_b, dh)


def _ff_pad(t):
    lead = t.shape[:-1]
    t = t.reshape(lead + (N_DEV, FF_HALF))
    return jnp.pad(t, [(0, 0)] * len(lead) + [(0, 0), (0, FF_HALF_P - FF_HALF)]).reshape(lead + (D_FF_P,))


def _ff_unpad(t):
    lead = t.shape[:-1]
    return t.reshape(lead + (N_DEV, FF_HALF_P))[..., :FF_HALF].reshape(lead + (D_FF,))


def _ln_fwd(x, h, g, b):
    r = DN_ALPHA * x + h
    mu = jnp.mean(r, axis=-1, keepdims=True)
    var = jnp.mean(jnp.square(r - mu), axis=-1, keepdims=True)
    return r, (r - mu) * lax.rsqrt(var + LN_EPS) * g + b


def _ln_bwd(r, dy, g):
    mu = jnp.mean(r, axis=-1, keepdims=True)
    var = jnp.mean(jnp.square(r - mu), axis=-1, keepdims=True)
    xhat = (r - mu) * lax.rsqrt(var + LN_EPS)
    dxh = dy * g
    dr = lax.rsqrt(var + LN_EPS) * (dxh - jnp.mean(dxh, axis=-1, keepdims=True)
                                    - xhat * jnp.mean(dxh * xhat, axis=-1, keepdims=True))
    return dr, jnp.sum(dy * xhat, axis=0, keepdims=True), jnp.sum(dy, axis=0, keepdims=True)


def _merge(gf, gs, gc, ya, yb2, yc):
    yb = yb2[:, :D_MODEL] * jax.nn.sigmoid(yb2[:, D_MODEL:])
    return jax.nn.sigmoid(gf) * ya + jax.nn.sigmoid(gs) * yb + jax.nn.sigmoid(gc) * yc


def _s5_tail(hc, su, d):
    return jax.nn.gelu(hc + d * su)


def _s5_tail_bwd(hc, su, dgel, d):
    _, vjp = jax.vjp(jax.nn.gelu, hc + d * su)
    (dy,) = vjp(dgel)
    return dy, d * dy, jnp.sum(dy * su, axis=0, keepdims=True)


def _loss_rows(y, tgt):
    err = y - tgt
    return err * (1.0 / D_MODEL), jnp.sum(0.5 * jnp.square(err), axis=0, keepdims=True) * (1.0 / D_MODEL)


def _peer(k):
    x, y, c = lax.axis_index("x"), lax.axis_index("y"), lax.axis_index("c")
    return (x ^ ((k >> 2) & 1), y ^ ((k >> 1) & 1), c ^ (k & 1))


def _my_slot():
    return 4 * lax.axis_index("x") + 2 * lax.axis_index("y") + lax.axis_index("c")


def _peer_slot(k):
    px, py, pc = _peer(k)
    return 4 * px + 2 * py + pc


N_CHIP = N_DEV // 2
OTHER_CHIPS = (2, 4, 6)


def _chip_of(dev):
    return 2 * dev[0] + dev[1]


def _remote(src, dst, send, recv, dev):
    return pltpu.make_async_remote_copy(src_ref=src, dst_ref=dst, send_sem=send, recv_sem=recv, device_id=dev,
                                        device_id_type=pl.DeviceIdType.MESH)


def _all_gather(shards, *, name, host=None, then=None):
    return _exchange(shards, *_all_gather_parts(shards), name=name, host=host, then=then)


def _exchange(ins, out_shapes, sem_shapes, start, finish, *, name, host=None, then=None):
    if host is not None:
        rider = _Rider(host, ins, out_shapes, sem_shapes, start, finish, then)
        _RIDERS.append(rider)
        return rider
    n = len(ins)

    def body(*refs):
        start(refs[:n], refs[n:2 * n], refs[2 * n:])
        finish(refs[:n], refs[n:2 * n], refs[2 * n:])

    hbm = pl.BlockSpec(memory_space=pl.ANY)
    return _pcall(body, name=name, in_specs=[hbm] * n, out_specs=[hbm] * n, out_shape=list(out_shapes),
                  scratch_shapes=list(sem_shapes))(*ins)


def _all_gather_parts(shards):
    n = len(shards)

    def first_copies(ins, outs, sems):
        send, recv, _ = sems
        return [_remote(ins[t], outs[t].at[_my_slot()], send.at[t, k - 1], recv.at[t, k - 1], _peer(k))
                for k in (1,) + OTHER_CHIPS for t in range(n)]

    def local_copies(ins, outs, sems):
        return [pltpu.make_async_copy(ins[t], outs[t].at[_my_slot()], sems[2].at[t]) for t in range(n)]

    def start(ins, outs, sems):
        for cp in local_copies(ins, outs, sems) + first_copies(ins, outs, sems):
            cp.start()

    def finish(ins, outs, sems):
        send, recv, _ = sems
        sibling = _peer(1)
        passed = []
        for k in OTHER_CHIPS:
            for t in range(n):
                slot = outs[t].at[_peer_slot(k)]
                _remote(ins[t], slot, send.at[t, k - 1], recv.at[t, k - 1], _peer(k)).wait_recv()
                cp = _remote(slot, slot, send.at[t, k], recv.at[t, k], sibling)
                cp.start()
                passed.append(cp)
        for t in range(n):
            _remote(ins[t], outs[t].at[_peer_slot(1)], send.at[t, 0], recv.at[t, 0], sibling).wait_recv()
            for k in OTHER_CHIPS:
                _remote(ins[t], outs[t].at[_peer_slot(k + 1)], send.at[t, k], recv.at[t, k], sibling).wait_recv()
        for cp in first_copies(ins, outs, sems) + passed:
            cp.wait_send()
        for lc in local_copies(ins, outs, sems):
            lc.wait()

    out_shapes = [jax.ShapeDtypeStruct((N_DEV,) + s.shape, s.dtype) for s in shards]
    sem_shapes = [pltpu.SemaphoreType.DMA((n, N_DEV - 1)), pltpu.SemaphoreType.DMA((n, N_DEV - 1)),
                  pltpu.SemaphoreType.DMA((n,))]
    return out_shapes, sem_shapes, start, finish


def _sibling_swap(grads, *, name, host=None, then=None):
    n = len(grads)

    def copies(ins, outs, sems):
        c = lax.axis_index("c")
        return [_remote(ins[t].at[:, 1 - c], outs[t], sems[0].at[t], sems[1].at[t], _peer(1)) for t in range(n)]

    def start(ins, outs, sems):
        for cp in copies(ins, outs, sems):
            cp.start()

    def finish(ins, outs, sems):
        for cp in copies(ins, outs, sems):
            cp.wait()

    out_shapes = [jax.ShapeDtypeStruct((N_CHIP,) + g.shape[2:], g.dtype) for g in grads]
    sem_shapes = [pltpu.SemaphoreType.DMA((n,)), pltpu.SemaphoreType.DMA((n,))]
    return _exchange(grads, out_shapes, sem_shapes, start, finish, name=name, host=host, then=then)


def _pair_add(g, p, out_dtype, *, name):
    _, _, R, C = g.shape
    tr = _pick(R, (512, 256, 128, 64, 32, 16, 8))

    def body(c_ref, g_ref, p_ref, o_ref):
        o_ref[...] = (g_ref[...] + p_ref[...]).astype(out_dtype)

    grid_spec = pltpu.PrefetchScalarGridSpec(
        num_scalar_prefetch=1, grid=(N_CHIP, R // tr),
        in_specs=[pl.BlockSpec((None, None, tr, C), lambda j, i, c_ref: (j, c_ref[0], i, 0)),
                  pl.BlockSpec((None, tr, C), lambda j, i, c_ref: (j, i, 0))],
        out_specs=pl.BlockSpec((None, tr, C), lambda j, i, c_ref: (j, i, 0)))
    core = lax.axis_index("c").astype(jnp.int32).reshape(1)
    return _pcall(body, name=name, grid_spec=grid_spec, out_shape=jax.ShapeDtypeStruct(p.shape, out_dtype),
                  compiler_params=_params(("parallel", "parallel")))(core, g, p)


def _chip_exchange(sums, *, name, host=None):
    n = len(sums)

    def copies(ins, outs, sems, dst_is_mine):
        send, recv, _ = sems
        mine = 2 * lax.axis_index("x") + lax.axis_index("y")
        out = []
        for k in OTHER_CHIPS:
            theirs = _chip_of(_peer(k))
            for t in range(n):
                out.append(_remote(ins[t].at[theirs], outs[t].at[mine if dst_is_mine else theirs],
                                   send.at[t, k // 2 - 1], recv.at[t, k // 2 - 1], _peer(k)))
        return out

    def local_copies(ins, outs, sems):
        mine = 2 * lax.axis_index("x") + lax.axis_index("y")
        return [pltpu.make_async_copy(ins[t].at[mine], outs[t].at[mine], sems[2].at[t]) for t in range(n)]

    def start(ins, outs, sems):
        for cp in local_copies(ins, outs, sems) + copies(ins, outs, sems, True):
            cp.start()

    def finish(ins, outs, sems):
        for cp in copies(ins, outs, sems, False) + local_copies(ins, outs, sems):
            cp.wait()

    out_shapes = [jax.ShapeDtypeStruct(s.shape, s.dtype) for s in sums]
    sem_shapes = [pltpu.SemaphoreType.DMA((n, N_CHIP - 1)), pltpu.SemaphoreType.DMA((n, N_CHIP - 1)),
                  pltpu.SemaphoreType.DMA((n,))]
    return _exchange(sums, out_shapes, sem_shapes, start, finish, name=name, host=host)


class _GradReduce:
    def __init__(self, grads, wire_dtypes, hosts=None):
        pairs = [g.reshape((N_CHIP, 2) + g.shape[1:]) for g in grads]
        self.n, self.riders, self.recv = len(grads), [], None

        def after_swap(partner):
            sums = [_pair_add(g, p, dt, name="grad_pair_add") for g, p, dt in zip(pairs, partner, wire_dtypes)]
            if hosts is None:
                self.recv = _chip_exchange(sums, name="grad_chip_exchange")
            else:
                self.riders = [(idx, _chip_exchange([sums[i] for i in idx], name="grad_chip_exchange", host=h))
                               for h, idx in hosts[1]]

        if hosts is None:
            after_swap(_sibling_swap(pairs, name="grad_sibling_swap"))
        else:
            _sibling_swap(pairs, name="grad_sibling_swap", host=hosts[0], then=after_swap)

    def result(self):
        if self.recv is None:
            self.recv = [None] * self.n
            for idx, rider in self.riders:
                assert rider.results is not None, rider.host
                for i, r in zip(idx, rider.results):
                    self.recv[i] = r
        return self.recv


def _adamw(recv, w, m, v, layer=None, into=None, *, name):
    n_slots, R, C = recv.shape
    tr = _pick(R, (256, 128, 64, 32, 16, 8))

    def body(r_ref, w_ref, m_ref, v_ref, *rest):
        g_ref, d_ref, nm_ref, nv_ref = rest[-4:]
        g = r_ref[0].astype(F32)
        for s in range(1, n_slots):
            g = g + r_ref[s].astype(F32)
        m_new = ADAM_B1 * m_ref[...] + (1.0 - ADAM_B1) * g
        v_new = ADAM_B2 * v_ref[...] + (1.0 - ADAM_B2) * jnp.square(g)
        m_hat = m_new / (1.0 - ADAM_B1 ** ADAM_STEP)
        v_hat = v_new / (1.0 - ADAM_B2 ** ADAM_STEP)
        g_ref[...] = g
        d_ref[...] = -ADAM_LR * (m_hat / (jnp.sqrt(v_hat) + ADAM_EPS) + ADAM_WD * w_ref[...])
        nm_ref[...] = m_new
        nv_ref[...] = v_new

    row = pl.BlockSpec((tr, C), lambda i: (i, 0))
    state = row if layer is None else pl.BlockSpec((None, tr, C), lambda i: (layer, i, 0))
    in_specs = [pl.BlockSpec((n_slots, tr, C), lambda i: (0, i, 0)), state, state, state]
    if into is None:
        return _pcall(body, name=name, grid=(R // tr,), in_specs=in_specs, out_specs=[row] * 4,
                      out_shape=[jax.ShapeDtypeStruct((R, C), F32)] * 4, compiler_params=_params(("parallel",)),
                      )(recv, w, m, v)
    return _pcall(body, name=name, grid=(R // tr,), in_specs=in_specs + [pl.BlockSpec(memory_space=pl.ANY)] * 4,
                  out_specs=[state] * 4, out_shape=[jax.ShapeDtypeStruct((DEPTH, R, C), F32)] * 4,
                  input_output_aliases={4 + j: j for j in range(4)}, compiler_params=_params(("parallel",)),
                  )(recv, w, m, v, *into)


def _flat_pad(parts, rows):
    flat = jnp.concatenate([p.reshape(-1) for p in parts])
    return jnp.pad(flat, (0, rows * LANES - flat.shape[0])).reshape(rows, LANES)


def _small_shard_shape(n):
    return SMALL[n][:-1] + (SMALL[n][-1] // N_DEV,)


def _unpack_small(gathered):
    out, off = {}, 0
    flat = gathered.reshape(N_DEV, -1)
    for n in SMALL:
        r, c = _small_shard_shape(n)
        out[n] = flat[:, off:off + r * c].reshape(N_DEV, r, c).transpose(1, 0, 2).reshape(r, N_DEV * c)
        off += r * c
    return out


def _pack_state(state, prefix):
    flat = jnp.concatenate([state[prefix + n].reshape(DEPTH, -1) for n in (*SMALL, *REPL)], axis=1)
    return jnp.pad(flat, ((0, 0), (0, PACK_ROWS * LANES - flat.shape[1]))).reshape(DEPTH * PACK_ROWS, LANES)


def _pack_small_grads(grads):
    cols = []
    for n in SMALL:
        r, c = _small_shard_shape(n)
        cols.append(grads[n].reshape(r, N_DEV, c).transpose(1, 0, 2).reshape(N_DEV, r * c))
    cols += [jnp.broadcast_to(grads[n].reshape(1, -1), (N_DEV, grads[n].size)) for n in REPL]
    flat = jnp.concatenate(cols, axis=1)
    return jnp.pad(flat, ((0, 0), (0, PACK_ROWS * LANES - flat.shape[1]))).reshape(N_DEV, PACK_ROWS, LANES)


def _unpack_state(packed):
    out, off = {}, 0
    flat = packed.reshape(DEPTH, -1)
    for n, shape in [(n, _small_shard_shape(n)) for n in SMALL] + list(REPL.items()):
        sz = math.prod(shape)
        out[n] = flat[:, off:off + sz].reshape((DEPTH,) + shape)
        off += sz
    return out


def _pad_b_in(b):
    parts, pos = [], 0
    for src, width, dst in Z_PIECES:
        parts += [jnp.zeros((b.shape[0], dst - pos), b.dtype), b[:, src:src + width]]
        pos = dst + width
    return jnp.concatenate(parts + [jnp.zeros((b.shape[0], Z_W - pos), b.dtype)], axis=1)


def _unpad_b_in(bp):
    return jnp.concatenate([bp[:, dst:dst + width] for _, width, dst in Z_PIECES], axis=1)


def _up_shard_pad(t):
    gap = jnp.zeros(t.shape[:-1] + (FF_HALF_P - FF_HALF,), t.dtype)
    return jnp.concatenate([t[..., :FF_HALF], gap, t[..., FF_HALF:], gap], axis=-1)


def _up_shard_unpad(t):
    return jnp.concatenate([t[..., :FF_HALF], t[..., FF_HALF_P:FF_HALF_P + FF_HALF]], axis=-1)


def _pad_shard(n, t):
    lead = [(0, 0)] * (t.ndim - 2)
    if n == "w_in":
        return jnp.pad(t, lead + [(0, 0), (0, W_IN_SLOT - W_IN_SHARD)])
    if n == "ffn_w_up":
        return _up_shard_pad(t)
    if n == "ffn_w_down":
        return jnp.pad(t, lead + [(0, FF_HALF_P - FF_HALF), (0, 0)])
    return t


def _unpad_shard(n, t):
    if n == "w_in":
        return t[..., :W_IN_SHARD]
    if n == "ffn_w_up":
        return _up_shard_unpad(t)
    if n == "ffn_w_down":
        return t[..., :FF_HALF, :]
    return t


GATHER_HOSTS = (("fox_fwd", (0,)), ("chunk_fwd", (8, 9)), ("mm_up", (1, 2, 3, 4, 5, 6, 7, 10)))
FIRST_GATHER_HOSTS = ((None, (0,)), ("mm_in", (1, 2, 3, 4, 5, 6, 7, 10)), ("s5_scan", (8, 9)))
REDUCE_HOSTS = ("mm_up_dw", (("fox_bwd", (0,)), ("s5_scan_bwd", (1, 2, 3, 4, 5, 6, 7, 10)), ("chunk_bwd", (8,)),
                             ("mm_in_dx", (9,))))


def _layer_shards(state, layer):
    shards = [_pad_shard(n, state[n][layer].astype(BF16)) for n in BIG]
    return shards + [_flat_pad([state[n][layer] for n in SMALL], SMALL_ROWS)]


class _Weights:
    def __init__(self, sources):
        self._sources, self._cache = sources, {}

    def _operand(self, i):
        src = self._sources[i]
        if isinstance(src, tuple):
            assert src[0].results is not None, src[0].host
            return src[0].results[src[1]]
        return src

    def __getitem__(self, n):
        if n not in self._cache:
            if n in SMALL:
                self._cache.update(_unpack_small(self._operand(len(BIG))))
            elif n == "w_in_p":
                self._cache[n] = _colmap(self._operand(0), inverse=False, name="w_in_colmap")
            else:
                w = self._operand(list(BIG).index(n))
                self._cache[n] = w.reshape(-1, D_MODEL) if BIG[n][1] == 0 else w
        return self._cache[n]


def _gather_riding(shards, hosts):
    sources = [None] * len(shards)
    for host, idx in hosts:
        group = [shards[i] for i in idx]
        if host is None:
            for i, g in zip(idx, _all_gather(group, name="all_gather_weights")):
                sources[i] = g
        else:
            rider = _all_gather(group, name="all_gather_weights", host=host)
            for pos, i in enumerate(idx):
                sources[i] = (rider, pos)
    return sources


def _ssm_params(p):
    prep1_in = (p["ssm_lambda_re"], p["ssm_lambda_im"], p["ssm_log_dt"][:, None])
    ar, ai, gr, gi = _whole(_ssm_prep1, *prep1_in, name="ssm_prep1")
    to_cn = lambda b: b.transpose(2, 0, 1).reshape(SSM_GROUP, N_SSM_CH)
    prep2_in = (gr.reshape(1, N_SSM_CH), gi.reshape(1, N_SSM_CH), to_cn(p["ssm_b_re"]), to_cn(p["ssm_b_im"]))
    bbr, bbi = _whole(_ssm_prep2, *prep2_in, name="ssm_prep2")
    to_gcp = lambda t: t.reshape(SSM_GROUP, SSM_GROUPS, SSM_STATE).transpose(1, 0, 2)
    bb = _blockdiag(to_gcp(bbr), to_gcp(bbi))
    cct = _blockdiag(p["ssm_c_re"], -p["ssm_c_im"])
    a_vec = _to_blocked(ar.reshape(1, N_SSM_CH), ai.reshape(1, N_SSM_CH))
    return dict(bb=bb, cct=cct, a_vec=a_vec, prep1_in=prep1_in, prep2_in=prep2_in)


def _layer_fwd(x, mem, p, W):
    sp = _ssm_params(p)
    z = _mm(x, W["w_in_p"], bias=p["b_in_p"], name="mm_in")
    f_t = z[:, Z_FF:Z_FF + FOX_HEADS].T
    cum = _cum_heads(f_t, name="fox_cum")
    ya_pre, lse_a = _attn_fwd("fox", (z, Z_FQ), (z, Z_FK), (z, Z_FV), (cum[:, :, None], cum[:, None, :]),
                              name="fox_fwd")
    ya = _mm(ya_pre, W["w_fox_o"], b_slots=True, name="mm_fox_o")
    x_ri = _mm_bd("in", z, sp["bb"], a_off=Z_SU, name="mm_s5_in")
    h_ri = _scan(x_ri, sp["a_vec"], name="s5_scan")
    hc = _mm_bd("out", h_ri, sp["cct"], name="mm_s5_out")
    d_row = p["ssm_d"][None, :]
    (gel,) = _rowwise(lambda a, b, c: (_s5_tail(a, b, c),), [hc, Win(z, 512, Z_SU // 512)], [d_row], name="s5_tail")
    yb2 = _mm(gel, W["w_ssm_glu"], b_slots=True, name="mm_glu")
    bias = _relbias_expand(W["ca_rel_bias"]).transpose(1, 0, 2)
    kv_band = jnp.pad(z[:, Z_CK:Z_CK + 2 * CA_WIDTH].astype(BF16), ((CA_PAD, 0), (0, 0)))
    yc_pre, lse_c = _attn_fwd("chunk", (z, Z_CQ), (kv_band, 0), (kv_band, CA_WIDTH), (bias,), name="chunk_fwd")
    yc = _mm(yc_pre, W["w_ca_o"], b_slots=True, name="mm_ca_o")
    gates = [Win(z, 1024, Z_GF // 1024), Win(z, 1024, Z_GS // 1024), Win(z, 1024, Z_GC // 1024)]
    (merged,) = _rowwise(lambda *a: (_merge(*a),), gates + [ya, yb2, yc], name="merge")
    h1 = _mm(merged, W["w_o"], name="mm_o")
    ln_g, ln_b = W["ln_g"], W["ln_b"]
    r1, x1 = _rowwise(_ln_fwd, [x, h1], [ln_g[0:1], ln_b[0:1]], name="ln_fwd")
    q = _mm(x1, W["xa_wq"], name="mm_xq")
    kv = _mm(mem, W["xa_wkv"], b_slots=True, name="mm_xkv")
    o, lse_x = _attn_fwd("xa", (q, 0), (kv, 0), (kv, D_MODEL), name="xa_fwd")
    h2 = _mm(o, W["xa_wo"], name="mm_xo")
    r2, x2 = _rowwise(_ln_fwd, [x1, h2], [ln_g[1:2], ln_b[1:2]], name="ln_fwd")
    up = _mm(x2, W["ffn_w_up"], b_slots=True, name="mm_up")
    hmid = _ffn_mid(up, _ff_pad(W["ffn_conv_w"]), _ff_pad(p["ffn_conv_b"][None, :]), name="ffn_mid")
    h3 = _mm(hmid, W["ffn_w_down"], name="mm_down")
    r3, x3 = _rowwise(_ln_fwd, [x2, h3], [ln_g[2:3], ln_b[2:3]], name="ln_fwd")
    res = dict(x=x, z=z, cum=cum, lse_a=lse_a, ya_pre=ya_pre, ya=ya, h_ri=h_ri, hc=hc, gel=gel, yb2=yb2, lse_c=lse_c,
               yc_pre=yc_pre, yc=yc, merged=merged, r1=r1, x1=x1, q=q, kv=kv, o=o, lse_x=lse_x, r2=r2, x2=x2, up=up,
               hmid=hmid, r3=r3, bias=bias, sp=sp, kv_band=kv_band, W=W)
    return x3, res


def _layer_bwd(dx3, mem, p, res):
    W = res["W"]
    x, z = res["x"], res["z"]
    sp = res["sp"]
    ln_g = W["ln_g"]
    big, small = {}, {}
    slots = lambda t: t.reshape(N_DEV, -1, D_MODEL)
    dr3, dg2, db2 = _rowwise(_ln_bwd, [res["r3"], dx3], [ln_g[2:3]], n_red=2, name="ln_bwd")
    dhmid = _mm(dr3, W["ffn_w_down"], tb=True, name="mm_down_dx")
    big["ffn_w_down"] = slots(_mm(res["hmid"], dr3, ta=True, name="mm_down_dw"))
    conv_w_p, conv_b_p = _ff_pad(W["ffn_conv_w"]), _ff_pad(p["ffn_conv_b"][None, :])
    dup_a, dup_g, dcw, dcb = _ffn_mid(res["up"], conv_w_p, conv_b_p, dhmid, name="ffn_mid_bwd")
    dup = jnp.concatenate([dup_a, dup_g], axis=1)
    small["ffn_conv_w"], small["ffn_conv_b"] = _ff_unpad(dcw), _ff_unpad(dcb)[0]
    dx2 = _mm(dup, W["ffn_w_up"], tb=True, b_slots=True, add=(dr3, DN_ALPHA), name="mm_up_dx")
    big["ffn_w_up"] = _mm(res["x2"], dup, ta=True, out_slots=2 * FF_HALF_P, name="mm_up_dw")
    dr2, dg1, db1 = _rowwise(_ln_bwd, [res["r2"], dx2], [ln_g[1:2]], n_red=2, name="ln_bwd")
    do = _mm(dr2, W["xa_wo"], tb=True, name="mm_xo_dx")
    big["xa_wo"] = slots(_mm(res["o"], dr2, ta=True, name="mm_xo_dw"))
    kv = res["kv"]
    dq, dk, dv = _attn_bwd("xa", (res["q"], 0), (kv, 0), (kv, D_MODEL), res["lse_x"], (do, 0), name="xa_bwd")
    dkv = jnp.concatenate([dk, dv], axis=1)
    dx1 = _mm(dq, W["xa_wq"], tb=True, add=(dr2, DN_ALPHA), name="mm_xq_dx")
    big["xa_wq"] = slots(_mm(res["x1"], dq, ta=True, name="mm_xq_dw"))
    big["xa_wkv"] = _mm(mem, dkv, ta=True, out_slots=256, name="mm_xkv_dw")
    dr1, dg0, db0 = _rowwise(_ln_bwd, [res["r1"], dx1], [ln_g[0:1]], n_red=2, name="ln_bwd")
    small["ln_g"] = jnp.concatenate([dg0, dg1, dg2], axis=0)
    small["ln_b"] = jnp.concatenate([db0, db1, db2], axis=0)
    dmerged = _mm(dr1, W["w_o"], tb=True, name="mm_o_dx")
    big["w_o"] = slots(_mm(res["merged"], dr1, ta=True, name="mm_o_dw"))
    gates = [Win(z, 1024, Z_GF // 1024), Win(z, 1024, Z_GS // 1024), Win(z, 1024, Z_GC // 1024)]
    dgf, dgs, dgc, dya, dyb2, dyc = _rowwise(_vjp_of(_merge, 6), gates + [res["ya"], res["yb2"], res["yc"], dmerged],
                                             name="merge_bwd")
    dya_pre = _mm(dya, W["w_fox_o"], tb=True, b_slots=True, name="mm_fox_o_dx")
    big["w_fox_o"] = _mm(res["ya_pre"], dya, ta=True, out_slots=128, name="mm_fox_o_dw")
    cum = res["cum"]
    dfq, dfk, dfv, dcq, dck = _attn_bwd("fox", (z, Z_FQ), (z, Z_FK), (z, Z_FV), res["lse_a"], (dya_pre, 0),
                                        (cum[:, :, None], cum[:, None, :]), name="fox_bwd")
    f_t = z[:, Z_FF:Z_FF + FOX_HEADS].T
    dff = _cum_heads(f_t, dcq[:, :, 0] + dck[:, 0, :], name="fox_cum_bwd")
    dgel = _mm(dyb2, W["w_ssm_glu"], tb=True, b_slots=True, name="mm_glu_dx")
    big["w_ssm_glu"] = _mm(res["gel"], dyb2, ta=True, out_slots=256, name="mm_glu_dw")
    d_row = p["ssm_d"][None, :]
    su_win = Win(z, 512, Z_SU // 512)
    dy, dsu1, dd = _rowwise(_s5_tail_bwd, [res["hc"], su_win, dgel], [d_row], n_red=1, name="s5_tail_bwd")
    small["ssm_d"] = dd[0]
    dh_ri = _mm_bd("in", dy, sp["cct"], name="mm_s5_out_dx")
    dcct = _mm_bd("dw", dy, res["h_ri"], name="mm_s5_out_dw")
    dx_ri, da_vec = _scan(dh_ri, sp["a_vec"], res["h_ri"], name="s5_scan_bwd")
    dsu = _mm_bd("out", dx_ri, sp["bb"], add=(dsu1, 1.0), name="mm_s5_in_dx")
    dbb = _mm_bd("dw", z, dx_ri, a_off=Z_SU, name="mm_s5_in_dw")
    dcr, dci = _blockdiag_inv(dcct)
    small["ssm_c_re"], small["ssm_c_im"] = dcr, -dci
    dbbr, dbbi = _blockdiag_inv(dbb)
    to_cn = lambda t: t.transpose(1, 0, 2).reshape(SSM_GROUP, N_SSM_CH)
    dgr, dgi, dbr, dbi = _whole(_vjp_of(_ssm_prep2, 4), *sp["prep2_in"], to_cn(dbbr), to_cn(dbbi), name="ssm_prep2_bwd")
    from_cn = lambda t: t.reshape(SSM_GROUP, SSM_GROUPS, SSM_STATE).transpose(1, 2, 0)
    small["ssm_b_re"], small["ssm_b_im"] = from_cn(dbr), from_cn(dbi)
    dar, dai = _from_blocked(da_vec)
    sq = lambda t: t.reshape(SSM_GROUPS, SSM_STATE)
    dlr, dli, dldt = _whole(_vjp_of(_ssm_prep1, 3), *sp["prep1_in"], sq(dar), sq(dai), sq(dgr), sq(dgi),
                            name="ssm_prep1_bwd")
    small["ssm_lambda_re"], small["ssm_lambda_im"], small["ssm_log_dt"] = dlr, dli, dldt[:, 0]
    dyc_pre = _mm(dyc, W["w_ca_o"], tb=True, b_slots=True, name="mm_ca_o_dx")
    big["w_ca_o"] = _mm(res["yc_pre"], dyc, ta=True, out_slots=128, name="mm_ca_o_dw")
    bias = res["bias"]
    kv_band = res["kv_band"]
    dcq_, dck_band, dcv_band, dbias = _attn_bwd("chunk", (z, Z_CQ), (kv_band, 0), (kv_band, CA_WIDTH), res["lse_c"],
                                                (dyc_pre, 0), (bias,), name="chunk_bwd")
    small["ca_rel_bias"] = _relbias_reduce(dbias.transpose(1, 0, 2))
    dff_p = jnp.pad(dff.T, ((0, 0), (0, 512 - FOX_HEADS)))
    dz = jnp.concatenate([dfq, dfk, dfv, dff_p, dsu, dcq_, dck_band[CA_PAD:], dcv_band[CA_PAD:], dgf, dgs, dgc],
                         axis=1)
    dx = _mm(dz, W["w_in_p"], tb=True, add=(dr1, DN_ALPHA), name="mm_in_dx")
    big["w_in"] = _colmap(_mm(x, dz, ta=True, name="mm_in_dw"), inverse=True, name="w_in_colmap_inv")
    (db_in_p,) = _rowwise(lambda t: (jnp.sum(t, axis=0, keepdims=True),), [dz], n_red=1, name="colsum")
    small["b_in"] = _unpad_b_in(db_in_p)[0]
    return dx, big, small


def kernel(x, mem, w_in, b_in, ssm_lambda_re, ssm_lambda_im, ssm_log_dt, ssm_b_re, ssm_b_im, ssm_c_re, ssm_c_im, ssm_d, ca_rel_bias, w_fox_o, w_ssm_glu, w_ca_o, w_o, xa_wq, xa_wkv, xa_wo, ffn_w_up, ffn_conv_w, ffn_conv_b, ffn_w_down, ln_g, ln_b, loss_target, m_w_in, m_b_in, m_ssm_lambda_re, m_ssm_lambda_im, m_ssm_log_dt, m_ssm_b_re, m_ssm_b_im, m_ssm_c_re, m_ssm_c_im, m_ssm_d, m_ca_rel_bias, m_w_fox_o, m_w_ssm_glu, m_w_ca_o, m_w_o, m_xa_wq, m_xa_wkv, m_xa_wo, m_ffn_w_up, m_ffn_conv_w, m_ffn_conv_b, m_ffn_w_down, m_ln_g, m_ln_b, v_w_in, v_b_in, v_ssm_lambda_re, v_ssm_lambda_im, v_ssm_log_dt, v_ssm_b_re, v_ssm_b_im, v_ssm_c_re, v_ssm_c_im, v_ssm_d, v_ca_rel_bias, v_w_fox_o, v_w_ssm_glu, v_w_ca_o, v_w_o, v_xa_wq, v_xa_wkv, v_xa_wo, v_ffn_w_up, v_ffn_conv_w, v_ffn_conv_b, v_ffn_w_down, v_ln_g, v_ln_b):
    given = dict(locals())
    state = {pre + n: given[pre + n] for n in WEIGHTS for pre in ("", "m_", "v_")}
    mem0 = mem[0]
    b_in_p = _pad_b_in(b_in)
    layer_params = [{**{n: state[n][l] for n in REPL}, "b_in_p": b_in_p[l:l + 1]} for l in range(DEPTH)]

    _RIDERS.clear()
    h, residuals = x[0], []
    sources = _gather_riding(_layer_shards(state, 0), FIRST_GATHER_HOSTS)
    for l in range(DEPTH):
        following = _gather_riding(_layer_shards(state, l + 1), GATHER_HOSTS) if l + 1 < DEPTH else None
        h, res = _layer_fwd(h, mem0, layer_params[l], _Weights(sources))
        residuals.append(res)
        sources = following
    dh, loss_cols = _rowwise(_loss_rows, [h, loss_target[0]], n_red=1, name="loss")
    loss = lax.psum(jnp.sum(loss_cols), ("x", "y", "c"))

    outs = [None] * DEPTH
    big_out = {n: None for n in BIG}
    padded = {pre + n: _pad_shard(n, state[pre + n]) for n in BIG for pre in ("", "m_", "v_")}
    wire = [BF16] * len(BIG) + [F32]
    pending = None
    for l in reversed(range(-1, DEPTH)):
        if l >= 0:
            dh, big, small = _layer_bwd(dh, mem0, layer_params[l], residuals[l])
            reduce = _GradReduce([big[n] for n in BIG] + [_pack_small_grads(small)], wire,
                                 hosts=REDUCE_HOSTS if l > 0 else None)
        done, pending = pending, (l, reduce) if l >= 0 else None
        if done is None:
            continue
        l_done, reduce_done = done
        *recv_big, recv_small = reduce_done.result()
        for n, recv in zip(BIG, recv_big):
            if big_out[n] is None:
                big_out[n] = [lax.empty((DEPTH,) + recv.shape[1:], F32) for _ in range(4)]
            big_out[n] = _adamw(recv, *[padded[pre + n] for pre in ("", "m_", "v_")], l_done, big_out[n],
                                name="adamw_" + n)
        outs[l_done] = recv_small

    assert not _RIDERS, [r.host for r in _RIDERS]
    packed = _adamw(jnp.concatenate(outs, axis=1), *[_pack_state(state, pre) for pre in ("", "m_", "v_")],
                    name="adamw_small")
    small_out = [_unpack_state(t) for t in packed]
    result = lambda n, j: _unpad_shard(n, big_out[n][j]) if n in BIG else small_out[j][n]
    return (loss, dh[None], *[result(n, j) for j in range(4) for n in WEIGHTS])
```

```python
import functools
import math

import jax
import jax.numpy as jnp
from jax import lax
from jax.experimental import pallas as pl
from jax.experimental.pallas import tpu as pltpu

F32, BF16 = jnp.float32, jnp.bfloat16

D_MODEL = 1024
DEPTH = 4
CHUNK = 64
FOX_HEADS, FOX_HEAD_DIM, FOX_WIDTH = 8, 64, 512
SSM_GROUP, SSM_WIDTH, SSM_GROUPS, SSM_STATE = 16, 512, 32, 64
CA_HEADS, CA_HEAD_DIM, CA_WIDTH, CA_LEFT_CHUNKS = 8, 64, 512, 8
CA_BAND = (CA_LEFT_CHUNKS + 1) * CHUNK
CA_PAD = CA_LEFT_CHUNKS * CHUNK
REL_MIN, REL_MAX = -(CHUNK - 1), 4 * CHUNK
N_REL = REL_MAX - REL_MIN + 1
XA_HEADS, XA_HEAD_DIM = 4, 256
D_FF = 2816
DN_ALPHA = (2 * DEPTH) ** 0.25
LN_EPS = 1e-5
NEG_INF = -1e30
ADAM_LR, ADAM_B1, ADAM_B2, ADAM_EPS, ADAM_WD, ADAM_STEP = 0.001, 0.9, 0.999, 1e-08, 0.01, 10

N_DEV = 8
LANE = 128
N_SSM_CH = SSM_GROUPS * SSM_STATE
SCAN_CB = 256
N_IN = 6664
W_IN_SHARD, W_IN_SLOT = N_IN // N_DEV, 896
Z_W = 7168
Z_FQ, Z_FK, Z_FV, Z_FF, Z_SU, Z_CQ, Z_CK, Z_CV, Z_GF, Z_GS, Z_GC = (
    0, 512, 1024, 1536, 2048, 2560, 3072, 3584, 4096, 5120, 6144)
Z_PIECES = ((0, 512, Z_FQ), (512, 512, Z_FK), (1024, 512, Z_FV), (1536, 8, Z_FF), (1544, 512, Z_SU),
            (2056, 512, Z_CQ), (2568, 512, Z_CK), (3080, 512, Z_CV), (3592, 1024, Z_GF), (4616, 1024, Z_GS),
            (5640, 1024, Z_GC))
FF_HALF, FF_HALF_P = D_FF // N_DEV, 384
D_FF_P = N_DEV * FF_HALF_P

VMEM_LIMIT_BYTES = 56 * 1024 * 1024

BIG = {"w_in": ((1024, W_IN_SHARD), 1), "w_fox_o": ((512, 128), 1), "w_ssm_glu": ((512, 256), 1),
       "w_ca_o": ((512, 128), 1), "w_o": ((128, 1024), 0), "xa_wq": ((128, 1024), 0), "xa_wkv": ((1024, 256), 1),
       "xa_wo": ((128, 1024), 0), "ffn_w_up": ((1024, 2 * FF_HALF), 1), "ffn_w_down": ((FF_HALF, 1024), 0)}
SMALL = {"ca_rel_bias": (8, 320), "ffn_conv_w": (3, D_FF), "ln_g": (3, 1024), "ln_b": (3, 1024)}
REPL = {"b_in": (N_IN,), "ssm_lambda_re": (32, 64), "ssm_lambda_im": (32, 64), "ssm_log_dt": (32,),
        "ssm_b_re": (32, 64, 16), "ssm_b_im": (32, 64, 16), "ssm_c_re": (32, 16, 64), "ssm_c_im": (32, 16, 64),
        "ssm_d": (512,), "ffn_conv_b": (D_FF,)}
WEIGHTS = ("w_in", "b_in", "ssm_lambda_re", "ssm_lambda_im", "ssm_log_dt", "ssm_b_re", "ssm_b_im", "ssm_c_re",
           "ssm_c_im", "ssm_d", "ca_rel_bias", "w_fox_o", "w_ssm_glu", "w_ca_o", "w_o", "xa_wq", "xa_wkv", "xa_wo",
           "ffn_w_up", "ffn_conv_w", "ffn_conv_b", "ffn_w_down", "ln_g", "ln_b")
LANES = 1024
PACK_ROWS = 256
SMALL_ROWS = 8


def _w_in_segments():
    segs = []
    for src, width, dst in Z_PIECES:
        n = src
        while n < src + width:
            d = n // W_IN_SHARD
            end = min(src + width, (d + 1) * W_IN_SHARD)
            segs.append((W_IN_SLOT * d + n - W_IN_SHARD * d, dst + n - src, end - n))
            n = end
    return tuple(segs)


W_IN_SEGS = _w_in_segments()


def _pallas(body, **kw):
    return pl.pallas_call(body, **kw)


def _params(sem):
    return pltpu.CompilerParams(dimension_semantics=sem, vmem_limit_bytes=VMEM_LIMIT_BYTES)


class _Rider:
    def __init__(self, host, ins, out_shapes, sem_shapes, start, finish, then=None):
        self.host, self.ins, self.out_shapes, self.sem_shapes = host, list(ins), list(out_shapes), list(sem_shapes)
        self.start, self.finish, self.then, self.results = start, finish, then, None


_RIDERS = []


def _pcall(body, *, name, **kw):
    rider = next((r for r in _RIDERS if r.host == name), None)
    if rider is None:
        return _pallas(body, name=name, **kw)
    _RIDERS.remove(rider)
    grid, in_specs, scratch = kw["grid"], list(kw["in_specs"]), list(kw.get("scratch_shapes", ()))
    single = not isinstance(kw["out_shape"], (list, tuple))
    out_specs = [kw["out_specs"]] if single else list(kw["out_specs"])
    out_shape = [kw["out_shape"]] if single else list(kw["out_shape"])
    n_in, n_out, n_scr = len(in_specs), len(out_specs), len(scratch)
    r_in, r_out = len(rider.ins), len(rider.out_shapes)

    def fused(*refs):
        a, ra = refs[:n_in], refs[n_in:n_in + r_in]
        o, ro = refs[n_in + r_in:n_in + r_in + n_out], refs[n_in + r_in + n_out:n_in + r_in + n_out + r_out]
        scr, sems = refs[n_in + r_in + n_out + r_out:][:n_scr], refs[n_in + r_in + n_out + r_out + n_scr:]
        ids = [pl.program_id(d) for d in range(len(grid))]
        first = functools.reduce(jnp.logical_and, [i == 0 for i in ids])
        last = functools.reduce(jnp.logical_and, [i == g - 1 for i, g in zip(ids, grid)])

        @pl.when(first)
        def _():
            rider.start(ra, ro, sems)

        body(*a, *o, *scr)

        @pl.when(last)
        def _():
            rider.finish(ra, ro, sems)

    hbm = pl.BlockSpec(memory_space=pl.ANY)
    call = _pallas(fused, name=name, grid=grid, in_specs=in_specs + [hbm] * r_in, out_specs=out_specs + [hbm] * r_out,
                   out_shape=out_shape + rider.out_shapes, scratch_shapes=scratch + rider.sem_shapes,
                   compiler_params=_params(("arbitrary",) * len(grid)))

    def run(*operands):
        outs = call(*operands, *rider.ins)
        rider.results = list(outs[n_out:])
        if rider.then is not None:
            rider.then(rider.results)
        return outs[0] if single else list(outs[:n_out])

    return run


def _pick(dim, prefs):
    for p in prefs:
        if dim % p == 0:
            return p
    return dim


def _mm(a, b, *, ta=False, tb=False, bias=None, add=None, a_off=0, a_cols=None, b_slots=False, out_slots=None,
        name, out_dtype=F32):
    a_cols = a_cols if a_cols is not None else a.shape[1]
    M, K = (a_cols, a.shape[0]) if ta else (a.shape[0], a_cols)
    tm = _pick(M, (1024, 512, 256, 128))
    if ta:
        tk = _pick(K, (2048, 1024, 512, 256))
    elif b_slots and tb:
        tk = _pick(b.shape[2], (1024, 768, 512, 256, 128))
    else:
        tk = K if K <= 3072 else _pick(K, (1024, 512, 256, 128))
    nk = K // tk
    if b_slots:
        ns = b.shape[2]
        if tb:
            N = b.shape[1]
            tn = _pick(N, (512, 256, 128))
            per = ns // tk
            b_spec = pl.BlockSpec((None, tn, tk), lambda i, j, k: (k // per, j, k % per))
            b_dim = 1
            assert N_DEV * ns == K
        else:
            N = N_DEV * ns
            tn = _pick(ns, (512, 256, 128))
            per = ns // tn
            b_spec = pl.BlockSpec((None, tk, tn), lambda i, j, k: (j // per, k, j % per))
            b_dim = 0
            assert b.shape[1] == K
    else:
        N = b.shape[0] if tb else b.shape[1]
        assert (b.shape[1] if tb else b.shape[0]) == K, (a.shape, b.shape, ta, tb)
        tn = _pick(N if out_slots is None else out_slots, (512, 256, 128))
        if tb:
            b_spec = pl.BlockSpec((tn, tk), lambda i, j, k: (j, k))
            b_dim = 1
        else:
            b_spec = pl.BlockSpec((tk, tn), lambda i, j, k: (k, j))
            b_dim = 0
    if ta:
        assert a_off % tm == 0
        a_spec = pl.BlockSpec((tk, tm), lambda i, j, k: (k, i + a_off // tm))
        a_dim = 0
    else:
        assert a_off % tk == 0
        a_spec = pl.BlockSpec((tm, tk), lambda i, j, k: (i, k + a_off // tk))
        a_dim = 1
    cache_at = ta and nk == 1
    dims = (((1 if cache_at else a_dim,), (b_dim,)), ((), ()))
    ins, specs = [a, b], [a_spec, b_spec]
    if bias is not None:
        ins.append(bias)
        specs.append(pl.BlockSpec((1, tn), lambda i, j, k: (0, j)))
    add_scale = None
    if add is not None:
        ins.append(add[0])
        add_scale = add[1]
        specs.append(pl.BlockSpec((tm, tn), lambda i, j, k: (i, j)))
    if out_slots is None:
        out_spec = pl.BlockSpec((tm, tn), lambda i, j, k: (i, j))
        out_shape = jax.ShapeDtypeStruct((M, N), out_dtype)
    else:
        assert N == N_DEV * out_slots
        per_o = out_slots // tn
        out_spec = pl.BlockSpec((None, tm, tn), lambda i, j, k: (j // per_o, i, j % per_o))
        out_shape = jax.ShapeDtypeStruct((N_DEV, M, out_slots), out_dtype)

    def body(*refs):
        a_ref, b_ref = refs[0], refs[1]
        pos = 2
        bias_ref = add_ref = None
        if bias is not None:
            bias_ref = refs[pos]
            pos += 1
        if add is not None:
            add_ref = refs[pos]
            pos += 1
        o_ref = refs[pos]
        acc_ref = refs[pos + 1] if nk > 1 else None
        if cache_at:
            at_ref = refs[pos + 1]

            @pl.when(pl.program_id(1) == 0)
            def _():
                step = min(tk, 256)
                for c in range(0, tk, step):
                    at_ref[:, c:c + step] = a_ref[c:c + step, :].T.astype(BF16)

            lhs = at_ref[...]
        else:
            lhs = a_ref[...].astype(BF16)
        part = lax.dot_general(lhs, b_ref[...].astype(BF16), dims, preferred_element_type=F32)

        def finish(acc):
            if bias_ref is not None:
                acc = acc + bias_ref[...]
            if add_ref is not None:
                acc = acc + add_scale * add_ref[...]
            o_ref[...] = acc.astype(out_dtype)

        if nk == 1:
            finish(part)
        else:
            k = pl.program_id(2)

            @pl.when(k == 0)
            def _():
                acc_ref[...] = part

            @pl.when(k > 0)
            def _():
                acc_ref[...] += part

            @pl.when(k == nk - 1)
            def _():
                finish(acc_ref[...])

    return _pcall(
        body, name=name, grid=(M // tm, N // tn, nk), in_specs=specs, out_specs=out_spec, out_shape=out_shape,
        scratch_shapes=[pltpu.VMEM((tm, tn), F32)] if nk > 1 else [pltpu.VMEM((tm, tk), BF16)] if cache_at else [],
        compiler_params=_params(("parallel", "arbitrary", "arbitrary")),
    )(*ins)


SSM_BD = 4
SSM_BD_IN, SSM_BD_ST = SSM_WIDTH // SSM_BD, 2 * N_SSM_CH // SSM_BD


def _mm_bd(form, a, b, *, a_off=0, add=None, name):
    S = a.shape[0]
    off = a_off // SSM_BD_IN
    if form == "dw":
        def body(a_ref, b_ref, o_ref):
            at = a_ref[...].T.astype(BF16)
            o_ref[...] = lax.dot_general(at, b_ref[...].astype(BF16), _NN, preferred_element_type=F32)

        return _pcall(body, name=name, grid=(SSM_BD,),
                      in_specs=[pl.BlockSpec((S, SSM_BD_IN), lambda j: (0, off + j)),
                                pl.BlockSpec((S, SSM_BD_ST), lambda j: (0, j))],
                      out_specs=pl.BlockSpec((None, SSM_BD_IN, SSM_BD_ST), lambda j: (j, 0, 0)),
                      out_shape=jax.ShapeDtypeStruct((SSM_BD, SSM_BD_IN, SSM_BD_ST), F32),
                      compiler_params=_params(("parallel",)))(a, b)
    tm = _pick(S, (1024, 512, 256))
    wide, narrow = (SSM_BD_ST, SSM_BD_IN) if form == "in" else (SSM_BD_IN, SSM_BD_ST)
    dims = _NN if form == "in" else _NT

    def body(a_ref, w_ref, *rest):
        acc = lax.dot_general(a_ref[...].astype(BF16), w_ref[...].astype(BF16), dims, preferred_element_type=F32)
        if add is not None:
            acc = acc + add[1] * rest[0][...]
        rest[-1][...] = acc

    specs = [pl.BlockSpec((tm, narrow), lambda i, j: (i, off + j)),
             pl.BlockSpec((None, SSM_BD_IN, SSM_BD_ST), lambda i, j: (j, 0, 0))]
    ins = [a, b]
    if add is not None:
        specs.append(pl.BlockSpec((tm, wide), lambda i, j: (i, j)))
        ins.append(add[0])
    return _pcall(body, name=name, grid=(S // tm, SSM_BD), in_specs=specs,
                  out_specs=pl.BlockSpec((tm, wide), lambda i, j: (i, j)),
                  out_shape=jax.ShapeDtypeStruct((S, SSM_BD * wide), F32),
                  compiler_params=_params(("parallel", "parallel")))(*ins)


class Win:
    def __init__(self, arr, width, blk):
        self.arr, self.width, self.blk = arr, width, blk


def _rowwise(fn, rows, vecs=(), *, n_red=0, tr=256, name):
    wins = [r if isinstance(r, Win) else Win(r, r.shape[1], 0) for r in rows]
    S = wins[0].arr.shape[0]
    tr = min(tr, S)
    tile_args = [jax.ShapeDtypeStruct((tr, w.width), w.arr.dtype) for w in wins]
    tile_args += [jax.ShapeDtypeStruct(v.shape, v.dtype) for v in vecs]
    outs = jax.eval_shape(fn, *tile_args)
    n_row = len(outs) - n_red
    specs = [pl.BlockSpec((tr, w.width), functools.partial(lambda i, b: (i, b), b=w.blk)) for w in wins]
    specs += [pl.BlockSpec(v.shape, functools.partial(lambda i, nd: (0,) * nd, nd=v.ndim)) for v in vecs]
    out_specs = [pl.BlockSpec((tr, o.shape[1]), lambda i: (i, 0)) for o in outs[:n_row]]
    out_specs += [pl.BlockSpec(o.shape, functools.partial(lambda i, nd: (0,) * nd, nd=len(o.shape))) for o in outs[n_row:]]
    out_shape = [jax.ShapeDtypeStruct((S, o.shape[1]), o.dtype) for o in outs[:n_row]]
    out_shape += [jax.ShapeDtypeStruct(o.shape, o.dtype) for o in outs[n_row:]]
    n_in = len(wins) + len(vecs)

    def body(*refs):
        res = fn(*[r[...] for r in refs[:n_in]])
        o_refs = refs[n_in:]
        for o_ref, r in zip(o_refs[:n_row], res[:n_row]):
            o_ref[...] = r.astype(o_ref.dtype)
        i = pl.program_id(0)
        for o_ref, r in zip(o_refs[n_row:], res[n_row:]):
            @pl.when(i == 0)
            def _(o_ref=o_ref, r=r):
                o_ref[...] = r

            @pl.when(i > 0)
            def _(o_ref=o_ref, r=r):
                o_ref[...] += r

    return _pcall(
        body, name=name, grid=(S // tr,), in_specs=specs, out_specs=out_specs, out_shape=out_shape,
        compiler_params=_params(("arbitrary",)),
    )(*[w.arr for w in wins], *vecs)


def _whole(fn, *arrays, name):
    outs = jax.eval_shape(fn, *arrays)
    n_in = len(arrays)

    def body(*refs):
        res = fn(*[r[...] for r in refs[:n_in]])
        for o_ref, r in zip(refs[n_in:], res):
            o_ref[...] = r

    vm = pl.BlockSpec(memory_space=pltpu.VMEM)
    return _pcall(body, name=name, in_specs=[vm] * n_in, out_specs=[vm] * len(outs),
                  out_shape=[jax.ShapeDtypeStruct(o.shape, o.dtype) for o in outs])(*arrays)


def _split3(x):
    hi = x.astype(BF16)
    r = x - hi.astype(F32)
    mid = r.astype(BF16)
    lo = (r - mid.astype(F32)).astype(BF16)
    return hi, mid, lo


def _dot3(x, onehot, dims):
    return sum(lax.dot_general(t, onehot, dims, preferred_element_type=F32) for t in _split3(x))


_NT = (((1,), (1,)), ((), ()))
_NN = (((1,), (0,)), ((), ()))
_TN = (((0,), (0,)), ((), ()))


def _colmap(x, *, inverse, name):
    R = x.shape[0] if inverse else x.shape[1]
    tr = 256
    per = W_IN_SLOT // LANE
    n_out = N_DEV * per if inverse else Z_W // LANE
    segs = [(p, q, n) for q, p, n in W_IN_SEGS] if inverse else list(W_IN_SEGS)

    def body(x_ref, o_ref):
        ia = lax.broadcasted_iota(jnp.int32, (LANE, LANE), 0)
        ib = lax.broadcasted_iota(jnp.int32, (LANE, LANE), 1)

        def src_block(i):
            if inverse:
                return x_ref[:, i * LANE:(i + 1) * LANE]
            return x_ref[i // per, :, (i % per) * LANE:(i % per + 1) * LANE]

        for jb in range(n_out):
            acc = None
            for s0, d0, n in segs:
                lo, hi = max(d0, jb * LANE), min(d0 + n, (jb + 1) * LANE)
                if lo >= hi:
                    continue
                delta = d0 - s0
                for i in range((lo - delta) // LANE, (hi - delta - 1) // LANE + 1):
                    shift = jb * LANE - i * LANE - delta
                    sel = ((ia - ib == shift) & (ib >= lo - jb * LANE) & (ib < hi - jb * LANE)).astype(BF16)
                    blk = src_block(i)
                    part = _dot3(blk, sel, _NN) if inverse else lax.dot_general(blk, sel, _NN, preferred_element_type=F32)
                    acc = part if acc is None else acc + part
            if acc is None:
                acc = jnp.zeros((tr, LANE), F32)
            if inverse:
                o_ref[jb // per, :, (jb % per) * LANE:(jb % per + 1) * LANE] = acc
            else:
                o_ref[:, jb * LANE:(jb + 1) * LANE] = acc.astype(BF16)

    slot_spec = pl.BlockSpec((N_DEV, tr, W_IN_SLOT), lambda i: (0, i, 0))
    flat_spec = pl.BlockSpec((tr, Z_W), lambda i: (i, 0))
    if inverse:
        return _pcall(body, name=name, grid=(R // tr,), in_specs=[flat_spec], out_specs=slot_spec,
                      out_shape=jax.ShapeDtypeStruct((N_DEV, R, W_IN_SLOT), F32), compiler_params=_params(("parallel",)))(x)
    return _pcall(body, name=name, grid=(R // tr,), in_specs=[slot_spec], out_specs=flat_spec,
                  out_shape=jax.ShapeDtypeStruct((R, Z_W), BF16), compiler_params=_params(("parallel",)))(x)


def _log_sigmoid(x):
    return jnp.minimum(x, 0.0) - jnp.log(1.0 + jnp.exp(-jnp.abs(x)))


def _cum_heads(f, dcum=None, *, name):
    H, S = f.shape
    tn = min(512, S)
    rev = dcum is not None

    def body(*refs):
        j = pl.program_id(0)
        s_idx = lax.broadcasted_iota(jnp.int32, (S, tn), 0)
        t_idx = lax.broadcasted_iota(jnp.int32, (S, tn), 1) + j * tn
        if not rev:
            f_ref, o_ref = refs
            tri = (s_idx <= t_idx).astype(BF16)
            o_ref[...] = _dot3(_log_sigmoid(f_ref[...]), tri, _NN)
        else:
            fj_ref, d_ref, o_ref = refs
            tri = (s_idx >= t_idx).astype(BF16)
            o_ref[...] = _dot3(d_ref[...], tri, _NN) * jax.nn.sigmoid(-fj_ref[...])

    full = pl.BlockSpec((H, S), lambda j: (0, 0))
    blk = pl.BlockSpec((H, tn), lambda j: (0, j))
    ins, specs = ([f], [full]) if not rev else ([f, dcum], [blk, full])
    return _pcall(body, name=name, grid=(S // tn,), in_specs=specs, out_specs=blk,
                  out_shape=jax.ShapeDtypeStruct((H, S), F32), compiler_params=_params(("arbitrary",)))(*ins)


def _rel_onehot(qi, transposed):
    shape = (N_REL, CA_BAND) if transposed else (CA_BAND, N_REL)
    kk = lax.broadcasted_iota(jnp.int32, shape, 1 if transposed else 0)
    rr = lax.broadcasted_iota(jnp.int32, shape, 0 if transposed else 1)
    idx = jnp.clip(CA_PAD + qi - kk, REL_MIN, REL_MAX) - REL_MIN
    return (idx == rr).astype(BF16)


def _relbias_expand(rb):
    rows = 8

    def body(rb_ref, o_ref):
        for r in range(rows):
            o_ref[r] = _dot3(rb_ref[...], _rel_onehot(pl.program_id(0) * rows + r, True), _NN)

    return _pcall(body, name="relbias_expand", grid=(CHUNK // rows,),
                  in_specs=[pl.BlockSpec((CA_HEADS, N_REL), lambda q: (0, 0))],
                  out_specs=pl.BlockSpec((rows, CA_HEADS, CA_BAND), lambda q: (q, 0, 0)),
                  out_shape=jax.ShapeDtypeStruct((CHUNK, CA_HEADS, CA_BAND), F32),
                  compiler_params=_params(("arbitrary",)))(rb)


def _relbias_reduce(db):
    rows = 8

    def body(db_ref, o_ref):
        q = pl.program_id(0)
        part = sum(_dot3(db_ref[r], _rel_onehot(q * rows + r, False), _NN) for r in range(rows))

        @pl.when(q == 0)
        def _():
            o_ref[...] = part

        @pl.when(q > 0)
        def _():
            o_ref[...] += part

    return _pcall(body, name="relbias_reduce", grid=(CHUNK // rows,),
                  in_specs=[pl.BlockSpec((rows, CA_HEADS, CA_BAND), lambda q: (q, 0, 0))],
                  out_specs=pl.BlockSpec((CA_HEADS, N_REL), lambda q: (0, 0)),
                  out_shape=jax.ShapeDtypeStruct((CA_HEADS, N_REL), F32),
                  compiler_params=_params(("arbitrary",)))(db)


def _attn_cfg(mode, S):
    if mode == "fox":
        return min(256, S), FOX_HEAD_DIM ** -0.5
    if mode == "chunk":
        return CHUNK, CA_HEAD_DIM ** -0.5
    return min(512, S), XA_HEAD_DIM ** -0.5


def _visible(mode, i, tq, nk):
    if mode == "fox":
        row = lax.broadcasted_iota(jnp.int32, (tq, nk), 0) + i * tq
        return row >= lax.broadcasted_iota(jnp.int32, (tq, nk), 1)
    if mode == "chunk":
        return lax.broadcasted_iota(jnp.int32, (tq, nk), 1) + i * CHUNK >= CA_PAD
    return None


def _scores(mode, qs, kb, extra, visible):
    s = lax.dot_general(qs, kb, _NT, preferred_element_type=F32)
    if mode == "fox":
        cq, ck = extra
        s = jnp.where(visible, s + cq - ck, NEG_INF)
    elif mode == "chunk":
        (bias,) = extra
        s = jnp.where(visible, s + bias, NEG_INF)
    return s


class _AttnPlan:
    def __init__(self, mode, q, k, pp):
        self.mode, self.pp = mode, pp
        self.D, self.hpb, self.bw = (XA_HEAD_DIM, 1, XA_HEAD_DIM) if mode == "xa" else (64, 2, LANE)
        self.S, self.Sk = q[0].shape[0], k[0].shape[0]
        self.H = (XA_HEADS if mode == "xa" else FOX_HEADS)
        self.W = self.H * self.D
        self.gw = self.bw * pp
        self.hpg = self.hpb * pp
        self.tq, self.scale = _attn_cfg(mode, self.S)
        self.grid = (self.W // self.gw, self.S // self.tq)

    def rows(self, win):
        off = win[1] // self.gw
        return pl.BlockSpec((self.tq, self.gw), lambda g, i: (i, off + g))

    def cols(self, win):
        off = win[1] // self.gw
        return pl.BlockSpec((win[0].shape[0], self.gw), lambda g, i: (0, off + g))

    def extras(self):
        if self.mode == "fox":
            return [pl.BlockSpec((self.hpg, self.tq, 1), lambda g, i: (g, i, 0)),
                    pl.BlockSpec((self.hpg, 1, self.Sk), lambda g, i: (g, 0, 0))]
        if self.mode == "chunk":
            return [pl.BlockSpec((self.hpg, CHUNK, CA_BAND), lambda g, i: (g, 0, 0))]
        return []

    def per_row(self):
        return pl.BlockSpec((self.hpg, self.tq, 1), lambda g, i: (g, i, 0))

    def lanes(self, p):
        return slice(p * self.bw, (p + 1) * self.bw)

    def keys(self, i, ref, p):
        if self.mode == "chunk":
            return ref[pl.ds(pl.multiple_of(i * CHUNK, CHUNK), CA_BAND), self.lanes(p)]
        return ref[:, self.lanes(p)]

    def head(self, x, hh):
        if self.hpb == 1:
            return x
        lane = lax.broadcasted_iota(jnp.int32, x.shape, 1)
        return jnp.where(lane // self.D == hh, x, jnp.zeros_like(x))


def _attn_fwd(mode, q, k, v, extra=(), *, name):
    pl_ = _AttnPlan(mode, q, k, pp=4)
    n_ex = len(extra)
    out = (None, 0)

    def body(*refs):
        q_ref, k_ref, v_ref = refs[:3]
        o_ref, lse_ref = refs[3 + n_ex:]
        i = pl.program_id(1)
        visible = _visible(mode, i, pl_.tq, CA_BAND if mode == "chunk" else pl_.Sk)
        for p in range(pl_.pp):
            qs = q_ref[:, pl_.lanes(p)].astype(BF16) * pl_.scale
            kp = pl_.keys(i, k_ref, p).astype(BF16)
            vp = pl_.keys(i, v_ref, p).astype(BF16)
            acc = None
            for hh in range(pl_.hpb):
                h = p * pl_.hpb + hh
                ex = [r[h] for r in refs[3:3 + n_ex]]
                s = _scores(mode, pl_.head(qs, hh), kp, ex, visible)
                m = jnp.max(s, axis=1, keepdims=True)
                e = jnp.exp(s - m)
                l = jnp.sum(e, axis=1, keepdims=True)
                part = lax.dot_general((e * (1.0 / l)).astype(BF16), pl_.head(vp, hh), _NN,
                                       preferred_element_type=F32)
                acc = part if acc is None else acc + part
                lse_ref[h] = m + jnp.log(l)
            o_ref[:, pl_.lanes(p)] = acc

    return _pcall(
        body, name=name, grid=pl_.grid, in_specs=[pl_.rows(q), pl_.cols(k), pl_.cols(v)] + pl_.extras(),
        out_specs=[pl_.rows(out), pl_.per_row()],
        out_shape=[jax.ShapeDtypeStruct((pl_.S, pl_.W), F32), jax.ShapeDtypeStruct((pl_.H, pl_.S, 1), F32)],
        compiler_params=_params(("parallel", "arbitrary")),
    )(q[0], k[0], v[0], *extra)


def _attn_bwd(mode, q, k, v, lse, do, extra=(), *, name):
    pl_ = _AttnPlan(mode, q, k, pp=2 if mode == "fox" else 4)
    H, S, Sk, W = pl_.H, pl_.S, pl_.Sk, pl_.W
    n_ex = len(extra)
    out = (None, 0)
    kv_out = pl.BlockSpec((Sk, pl_.gw), lambda g, i: (0, g))
    ex_specs = pl_.extras()
    out_specs = [pl_.rows(out), kv_out, kv_out]
    out_shape = [jax.ShapeDtypeStruct((S, W), F32), jax.ShapeDtypeStruct((Sk, W), F32),
                 jax.ShapeDtypeStruct((Sk, W), F32)]
    if mode == "fox":
        out_specs += [pl_.per_row(), ex_specs[1]]
        out_shape += [jax.ShapeDtypeStruct((H, S, 1), F32), jax.ShapeDtypeStruct((H, 1, Sk), F32)]
    elif mode == "chunk":
        out_specs += [ex_specs[0]]
        out_shape += [jax.ShapeDtypeStruct((H, CHUNK, CA_BAND), F32)]

    def body(*refs):
        q_ref, k_ref, v_ref, lse_ref, do_ref = refs[:5]
        dq_ref, dk_ref, dv_ref = refs[5 + n_ex:8 + n_ex]
        rest = refs[8 + n_ex:]
        i = pl.program_id(1)

        @pl.when(i == 0)
        def _():
            dk_ref[...] = jnp.zeros_like(dk_ref)
            dv_ref[...] = jnp.zeros_like(dv_ref)
            if mode == "fox":
                rest[1][...] = jnp.zeros_like(rest[1])
            elif mode == "chunk":
                rest[0][...] = jnp.zeros_like(rest[0])

        visible = _visible(mode, i, pl_.tq, CA_BAND if mode == "chunk" else pl_.Sk)
        for p in range(pl_.pp):
            lanes = pl_.lanes(p)
            qs = q_ref[:, lanes].astype(BF16) * pl_.scale
            kp = pl_.keys(i, k_ref, p).astype(BF16)
            vp = pl_.keys(i, v_ref, p).astype(BF16)
            dop = do_ref[:, lanes].astype(BF16)
            dq = dk_part = dv_part = None
            for hh in range(pl_.hpb):
                h = p * pl_.hpb + hh
                ex = [r[h] for r in refs[5:5 + n_ex]]
                qh, doh = pl_.head(qs, hh), pl_.head(dop, hh)
                s = _scores(mode, qh, kp, ex, visible)
                pr = jnp.exp(s - lse_ref[h])
                dp = lax.dot_general(doh, vp, _NT, preferred_element_type=F32)
                ds = pr * (dp - jnp.sum(dp * pr, axis=1, keepdims=True))
                dsb = ds.astype(BF16)
                parts = (lax.dot_general(dsb, pl_.head(kp, hh), _NN, preferred_element_type=F32) * pl_.scale,
                         lax.dot_general(dsb, qh, _TN, preferred_element_type=F32),
                         lax.dot_general(pr.astype(BF16), doh, _TN, preferred_element_type=F32))
                dq, dk_part, dv_part = parts if dq is None else (dq + parts[0], dk_part + parts[1], dv_part + parts[2])
                if mode == "chunk":
                    rest[0][h] += ds
                if mode == "fox":
                    rest[0][h] = jnp.sum(ds, axis=1, keepdims=True)
                    rest[1][h] += -jnp.sum(ds, axis=0, keepdims=True)
            dq_ref[:, lanes] = dq
            if mode == "chunk":
                win = pl.ds(pl.multiple_of(i * CHUNK, CHUNK), CA_BAND)
                dk_ref[win, lanes] += dk_part
                dv_ref[win, lanes] += dv_part
            else:
                dk_ref[:, lanes] += dk_part
                dv_ref[:, lanes] += dv_part

    return _pcall(
        body, name=name, grid=pl_.grid,
        in_specs=[pl_.rows(q), pl_.cols(k), pl_.cols(v), pl_.per_row(), pl_.rows(do)] + ex_specs,
        out_specs=out_specs, out_shape=out_shape,
        compiler_params=_params(("parallel", "arbitrary")),
    )(q[0], k[0], v[0], lse, do[0], *extra)


def _fox_blocks(cum, tq):
    H, S = cum.shape
    return cum[:, :, None], cum.reshape(H, S // tq, 1, tq)


def _fox_visible(i, kb, tq):
    row = lax.broadcasted_iota(jnp.int32, (tq, tq), 0)
    col = lax.broadcasted_iota(jnp.int32, (tq, tq), 1)
    return col + (kb - i) * tq <= row


def _fox_fwd(q, k, v, cum, *, name):
    pl_ = _AttnPlan("fox", q, k, pp=4)
    tq, nkb = pl_.tq, pl_.S // pl_.tq
    cq3, ck4 = _fox_blocks(cum, tq)

    def body(q_ref, k_ref, v_ref, cq_ref, ck_ref, o_ref, lse_ref):
        i = pl.program_id(1)
        for p in range(pl_.pp):
            lanes = pl_.lanes(p)
            qs = q_ref[:, lanes].astype(BF16) * pl_.scale
            pair = None
            for hh in range(pl_.hpb):
                h = p * pl_.hpb + hh
                qh, cq = pl_.head(qs, hh), cq_ref[h]

                def step(kb, carry, h=h, hh=hh, qh=qh, cq=cq, lanes=lanes):
                    m, l, acc = carry
                    rows = pl.ds(pl.multiple_of(kb * tq, tq), tq)
                    kb_ = k_ref[rows, lanes].astype(BF16)
                    vb_ = pl_.head(v_ref[rows, lanes].astype(BF16), hh)
                    s = lax.dot_general(qh, kb_, _NT, preferred_element_type=F32) + cq - ck_ref[h, kb]
                    s = jnp.where(_fox_visible(i, kb, tq), s, NEG_INF)
                    m_new = jnp.maximum(m, jnp.max(s, axis=1, keepdims=True))
                    alpha, e = jnp.exp(m - m_new), jnp.exp(s - m_new)
                    l = alpha * l + jnp.sum(e, axis=1, keepdims=True)
                    acc = alpha * acc + lax.dot_general(e.astype(BF16), vb_, _NN, preferred_element_type=F32)
                    return m_new, l, acc

                init = (jnp.full((tq, 1), NEG_INF, F32), jnp.zeros((tq, 1), F32), jnp.zeros((tq, pl_.bw), F32))
                m, l, acc = lax.fori_loop(0, i + 1, step, init)
                part = acc * (1.0 / l)
                pair = part if pair is None else pair + part
                lse_ref[h] = m + jnp.log(l)
            o_ref[:, lanes] = pair

    ck_spec = pl.BlockSpec((pl_.hpg, nkb, 1, tq), lambda g, i: (g, 0, 0, 0))
    return _pcall(
        body, name=name, grid=pl_.grid, in_specs=[pl_.rows(q), pl_.cols(k), pl_.cols(v), pl_.per_row(), ck_spec],
        out_specs=[pl_.rows((None, 0)), pl_.per_row()],
        out_shape=[jax.ShapeDtypeStruct((pl_.S, pl_.W), F32), jax.ShapeDtypeStruct((pl_.H, pl_.S, 1), F32)],
        compiler_params=_params(("parallel", "arbitrary")),
    )(q[0], k[0], v[0], cq3, ck4)


def _fox_bwd(q, k, v, lse, do, o, cum, *, name):
    pl_ = _AttnPlan("fox", q, k, pp=2)
    tq, nkb = pl_.tq, pl_.S // pl_.tq
    H, S, W = pl_.H, pl_.S, pl_.W
    cq3, ck4 = _fox_blocks(cum, tq)

    def body(q_ref, k_ref, v_ref, lse_ref, do_ref, o_ref, cq_ref, ck_ref, dq_ref, dk_ref, dv_ref, dcq_ref, dck_ref):
        i = pl.program_id(1)

        @pl.when(i == 0)
        def _():
            dk_ref[...] = jnp.zeros_like(dk_ref)
            dv_ref[...] = jnp.zeros_like(dv_ref)
            dck_ref[...] = jnp.zeros_like(dck_ref)

        for p in range(pl_.pp):
            lanes = pl_.lanes(p)
            qs = q_ref[:, lanes].astype(BF16) * pl_.scale
            do_f = do_ref[:, lanes]
            dop, doo = do_f.astype(BF16), do_f * o_ref[:, lanes]
            pair = None
            for hh in range(pl_.hpb):
                h = p * pl_.hpb + hh
                qh, doh, cq, lse_h = pl_.head(qs, hh), pl_.head(dop, hh), cq_ref[h], lse_ref[h]
                delta = jnp.sum(pl_.head(doo, hh), axis=1, keepdims=True)

                def step(kb, carry, h=h, hh=hh, qh=qh, doh=doh, cq=cq, lse_h=lse_h, delta=delta, lanes=lanes):
                    dq, dcq = carry
                    rows = pl.ds(pl.multiple_of(kb * tq, tq), tq)
                    kb_ = k_ref[rows, lanes].astype(BF16)
                    vb_ = v_ref[rows, lanes].astype(BF16)
                    s = lax.dot_general(qh, kb_, _NT, preferred_element_type=F32) + cq - ck_ref[h, kb]
                    pr = jnp.exp(jnp.where(_fox_visible(i, kb, tq), s, NEG_INF) - lse_h)
                    dp = lax.dot_general(doh, vb_, _NT, preferred_element_type=F32)
                    ds = pr * (dp - delta)
                    dsb = ds.astype(BF16)
                    dk_ref[rows, lanes] += lax.dot_general(dsb, qh, _TN, preferred_element_type=F32)
                    dv_ref[rows, lanes] += lax.dot_general(pr.astype(BF16), doh, _TN, preferred_element_type=F32)
                    dck_ref[h, kb] += -jnp.sum(ds, axis=0, keepdims=True)
                    dq = dq + lax.dot_general(dsb, pl_.head(kb_, hh), _NN, preferred_element_type=F32)
                    return dq, dcq + jnp.sum(ds, axis=1, keepdims=True)

                init = (jnp.zeros((tq, pl_.bw), F32), jnp.zeros((tq, 1), F32))
                dq, dcq = lax.fori_loop(0, i + 1, step, init)
                pair = dq if pair is None else pair + dq
                dcq_ref[h] = dcq
            dq_ref[:, lanes] = pair * pl_.scale

    out = (None, 0)
    kv_out = pl.BlockSpec((S, pl_.gw), lambda g, i: (0, g))
    ck_spec = pl.BlockSpec((pl_.hpg, nkb, 1, tq), lambda g, i: (g, 0, 0, 0))
    dq, dk, dv, dcq, dck = _pcall(
        body, name=name, grid=pl_.grid,
        in_specs=[pl_.rows(q), pl_.cols(k), pl_.cols(v), pl_.per_row(), pl_.rows(do), pl_.rows(o), pl_.per_row(),
                  ck_spec],
        out_specs=[pl_.rows(out), kv_out, kv_out, pl_.per_row(), ck_spec],
        out_shape=[jax.ShapeDtypeStruct((S, W), F32), jax.ShapeDtypeStruct((S, W), F32),
                   jax.ShapeDtypeStruct((S, W), F32), jax.ShapeDtypeStruct((H, S, 1), F32),
                   jax.ShapeDtypeStruct((H, nkb, 1, tq), F32)],
        compiler_params=_params(("parallel", "arbitrary")),
    )(q[0], k[0], v[0], lse, do[0], o[0], cq3, ck4)
    return dq, dk, dv, dcq[:, :, 0] + dck.reshape(H, S)


def _scan(x, a, h=None, *, name):
    S = x.shape[0]
    CB = SCAN_CB
    rev = h is not None
    n_grp = S // 8

    def body(*refs):
        if rev:
            x_ref, a_ref, h_ref, o_ref, da_ref = refs
        else:
            x_ref, a_ref, o_ref = refs
        ar = a_ref[:, :CB]
        ai = -a_ref[:, CB:] if rev else a_ref[:, CB:]
        zero = jnp.zeros((1, CB), F32)

        def group(g, carry):
            base = pl.multiple_of((n_grp - 1 - g) * 8 if rev else g * 8, 8)
            for j in (range(7, -1, -1) if rev else range(8)):
                t = base + j
                if rev:
                    hr, hi, dar, dai = carry
                else:
                    hr, hi = carry
                xr = x_ref[pl.ds(t, 1), :CB]
                xi = x_ref[pl.ds(t, 1), CB:]
                hr, hi = ar * hr - ai * hi + xr, ar * hi + ai * hr + xi
                o_ref[pl.ds(t, 1), :CB] = hr
                o_ref[pl.ds(t, 1), CB:] = hi
                if rev:
                    tp = jnp.maximum(t - 1, 0)
                    live = (t > 0).astype(F32)
                    pr = h_ref[pl.ds(tp, 1), :CB] * live
                    pi = h_ref[pl.ds(tp, 1), CB:] * live
                    carry = (hr, hi, dar + hr * pr + hi * pi, dai + hi * pr - hr * pi)
                else:
                    carry = (hr, hi)
            return carry

        if rev:
            _, _, dar, dai = lax.fori_loop(0, n_grp, group, (zero, zero, zero, zero))
            da_ref[:, :CB] = dar
            da_ref[:, CB:] = dai
        else:
            lax.fori_loop(0, n_grp, group, (zero, zero))

    big = pl.BlockSpec((S, 2 * CB), lambda c: (0, c))
    vec = pl.BlockSpec((1, 2 * CB), lambda c: (0, c))
    n_blk = x.shape[1] // (2 * CB)
    if rev:
        return _pcall(body, name=name, grid=(n_blk,), in_specs=[big, vec, big], out_specs=[big, vec],
                      out_shape=[jax.ShapeDtypeStruct(x.shape, F32), jax.ShapeDtypeStruct(a.shape, F32)],
                      compiler_params=_params(("parallel",)))(x, a, h)
    return _pcall(body, name=name, grid=(n_blk,), in_specs=[big, vec], out_specs=big,
                  out_shape=jax.ShapeDtypeStruct(x.shape, F32), compiler_params=_params(("parallel",)))(x, a)


def _ssm_prep1(lr_, li, ldt):
    lr = jnp.minimum(lr_, -1e-4)
    dt = jnp.exp(ldt)
    mag = jnp.exp(lr * dt)
    ar = mag * jnp.cos(li * dt)
    ai = mag * jnp.sin(li * dt)
    den = lr * lr + li * li
    gr = ((ar - 1.0) * lr + ai * li) / den
    gi = (ai * lr - (ar - 1.0) * li) / den
    return ar, ai, gr, gi


def _ssm_prep2(gr, gi, br, bi):
    return gr * br - gi * bi, gr * bi + gi * br


def _vjp_of(fn, n_in):
    def bwd(*args):
        cts = args[n_in:]
        return jax.vjp(fn, *args[:n_in])[1](cts[0] if len(cts) == 1 else tuple(cts))
    return bwd


def _to_blocked(r, i):
    lead = r.shape[:-1]
    t = jnp.stack([r.reshape(lead + (N_SSM_CH // SCAN_CB, SCAN_CB)), i.reshape(lead + (N_SSM_CH // SCAN_CB, SCAN_CB))],
                  axis=-2)
    return t.reshape(lead + (2 * N_SSM_CH,))


def _from_blocked(m):
    lead = m.shape[:-1]
    t = m.reshape(lead + (N_SSM_CH // SCAN_CB, 2, SCAN_CB))
    return t[..., 0, :].reshape(lead + (N_SSM_CH,)), t[..., 1, :].reshape(lead + (N_SSM_CH,))


_GPB = SSM_GROUPS // SSM_BD
_GPS = SCAN_CB // SSM_STATE


def _bd_eye():
    return jnp.eye(_GPB, dtype=F32).reshape(_GPB, _GPB // _GPS, _GPS)


def _blockdiag(r, i):
    v = jnp.stack([r, i]).reshape(2, SSM_BD, _GPB, SSM_GROUP, SSM_STATE)
    return jnp.einsum("qjgcp,gsh->jgcsqhp", v, _bd_eye()).reshape(SSM_BD, SSM_BD_IN, SSM_BD_ST)


def _blockdiag_inv(m):
    d = m.reshape(SSM_BD, _GPB, SSM_GROUP, _GPB // _GPS, 2, _GPS, SSM_STATE)
    v = jnp.einsum("jgcsqhp,gsh->qjgcp", d, _bd_eye()).reshape(2, SSM_GROUPS, SSM_GROUP, SSM_STATE)
    return v[0], v[1]


def _shift_rows(x, n):
    S = x.shape[0]
    row = lax.broadcasted_iota(jnp.int32, x.shape, 0)
    if n > 0:
        return jnp.where(row >= n, pltpu.roll(x, n, 0), 0.0)
    return jnp.where(row < S + n, pltpu.roll(x, S + n, 0), 0.0)


def _bf(x):
    return x.astype(BF16).astype(F32)


def _conv_pre(a, w, b):
    ab, wb = _bf(a), _bf(w)
    return wb[2:3] * ab + wb[1:2] * _shift_rows(ab, 1) + wb[0:1] * _shift_rows(ab, 2) + b


def _ffn_mid(up, conv_w, conv_b, dh=None, *, name):
    S = up.shape[0]
    tn = LANE
    nb = D_FF_P // tn
    rev = dh is not None

    def body(*refs):
        if not rev:
            a_ref, g_ref, w_ref, b_ref, o_ref = refs
            o_ref[...] = jax.nn.gelu(_conv_pre(a_ref[...], w_ref[...], b_ref[...])) * g_ref[...]
            return
        a_ref, g_ref, w_ref, b_ref, dh_ref, dup_a_ref, dup_g_ref, dw_ref, db_ref = refs
        a, w, dh_ = a_ref[...], w_ref[...], dh_ref[...]
        pre = _conv_pre(a, w, b_ref[...])
        gl, gelu_vjp = jax.vjp(jax.nn.gelu, pre)
        dup_g_ref[...] = dh_ * gl
        (dpre,) = gelu_vjp(dh_ * g_ref[...])
        db_ref[...] = jnp.sum(dpre, axis=0, keepdims=True)
        dpb, ab, wb = _bf(dpre), _bf(a), _bf(w)
        dup_a_ref[...] = wb[2:3] * dpb + wb[1:2] * _shift_rows(dpb, -1) + wb[0:1] * _shift_rows(dpb, -2)
        dw_ref[2:3, :] = jnp.sum(dpb * ab, axis=0, keepdims=True)
        dw_ref[1:2, :] = jnp.sum(dpb * _shift_rows(ab, 1), axis=0, keepdims=True)
        dw_ref[0:1, :] = jnp.sum(dpb * _shift_rows(ab, 2), axis=0, keepdims=True)

    a_spec = pl.BlockSpec((S, tn), lambda j: (0, j))
    g_spec = pl.BlockSpec((S, tn), lambda j: (0, j + nb))
    w_spec = pl.BlockSpec((3, tn), lambda j: (0, j))
    b_spec = pl.BlockSpec((1, tn), lambda j: (0, j))
    if not rev:
        return _pcall(body, name=name, grid=(nb,), in_specs=[a_spec, g_spec, w_spec, b_spec], out_specs=a_spec,
                      out_shape=jax.ShapeDtypeStruct((S, D_FF_P), F32), compiler_params=_params(("parallel",)))(
                          up, up, conv_w, conv_b)
    return _pcall(body, name=name, grid=(nb,), in_specs=[a_spec, g_spec, w_spec, b_spec, a_spec],
                  out_specs=[a_spec, a_spec, w_spec, b_spec],
                  out_shape=[jax.ShapeDtypeStruct((S, D_FF_P), F32), jax.ShapeDtypeStruct((S, D_FF_P), F32),
                             jax.ShapeDtypeStruct((3, D_FF_P), F32), jax.ShapeDtypeStruct((1, D_FF_P), F32)],
                  compiler_params=_params(("parallel",)))(up, up, conv_w, conv_b, dh)


def _ff_pad(t):
    lead = t.shape[:-1]
    t = t.reshape(lead + (N_DEV, FF_HALF))
    return jnp.pad(t, [(0, 0)] * len(lead) + [(0, 0), (0, FF_HALF_P - FF_HALF)]).reshape(lead + (D_FF_P,))


def _ff_unpad(t):
    lead = t.shape[:-1]
    return t.reshape(lead + (N_DEV, FF_HALF_P))[..., :FF_HALF].reshape(lead + (D_FF,))


def _ln_fwd(x, h, g, b):
    r = DN_ALPHA * x + h
    mu = jnp.mean(r, axis=-1, keepdims=True)
    var = jnp.mean(jnp.square(r - mu), axis=-1, keepdims=True)
    return r, (r - mu) * lax.rsqrt(var + LN_EPS) * g + b


def _ln_bwd(r, dy, g):
    mu = jnp.mean(r, axis=-1, keepdims=True)
    var = jnp.mean(jnp.square(r - mu), axis=-1, keepdims=True)
    xhat = (r - mu) * lax.rsqrt(var + LN_EPS)
    dxh = dy * g
    dr = lax.rsqrt(var + LN_EPS) * (dxh - jnp.mean(dxh, axis=-1, keepdims=True)
                                    - xhat * jnp.mean(dxh * xhat, axis=-1, keepdims=True))
    return dr, jnp.sum(dy * xhat, axis=0, keepdims=True), jnp.sum(dy, axis=0, keepdims=True)


def _merge(gf, gs, gc, ya, yb2, yc):
    yb = yb2[:, :D_MODEL] * jax.nn.sigmoid(yb2[:, D_MODEL:])
    return jax.nn.sigmoid(gf) * ya + jax.nn.sigmoid(gs) * yb + jax.nn.sigmoid(gc) * yc


def _s5_tail(hc, su, d):
    return jax.nn.gelu(hc + d * su)


def _s5_tail_bwd(hc, su, dgel, d):
    _, vjp = jax.vjp(jax.nn.gelu, hc + d * su)
    (dy,) = vjp(dgel)
    return dy, d * dy, jnp.sum(dy * su, axis=0, keepdims=True)


def _loss_rows(y, tgt):
    err = y - tgt
    return err * (1.0 / D_MODEL), jnp.sum(0.5 * jnp.square(err), axis=0, keepdims=True) * (1.0 / D_MODEL)


def _peer(k):
    x, y, c = lax.axis_index("x"), lax.axis_index("y"), lax.axis_index("c")
    return (x ^ ((k >> 2) & 1), y ^ ((k >> 1) & 1), c ^ (k & 1))


def _my_slot():
    return 4 * lax.axis_index("x") + 2 * lax.axis_index("y") + lax.axis_index("c")


def _peer_slot(k):
    px, py, pc = _peer(k)
    return 4 * px + 2 * py + pc


N_CHIP = N_DEV // 2
OTHER_CHIPS = (2, 4, 6)


def _chip_of(dev):
    return 2 * dev[0] + dev[1]


def _remote(src, dst, send, recv, dev):
    return pltpu.make_async_remote_copy(src_ref=src, dst_ref=dst, send_sem=send, recv_sem=recv, device_id=dev,
                                        device_id_type=pl.DeviceIdType.MESH)


def _all_gather(shards, *, name, host=None, then=None):
    return _exchange(shards, *_all_gather_parts(shards), name=name, host=host, then=then)


def _exchange(ins, out_shapes, sem_shapes, start, finish, *, name, host=None, then=None):
    if host is not None:
        rider = _Rider(host, ins, out_shapes, sem_shapes, start, finish, then)
        _RIDERS.append(rider)
        return rider
    n = len(ins)

    def body(*refs):
        start(refs[:n], refs[n:2 * n], refs[2 * n:])
        finish(refs[:n], refs[n:2 * n], refs[2 * n:])

    hbm = pl.BlockSpec(memory_space=pl.ANY)
    return _pcall(body, name=name, in_specs=[hbm] * n, out_specs=[hbm] * n, out_shape=list(out_shapes),
                  scratch_shapes=list(sem_shapes))(*ins)


def _all_gather_parts(shards):
    n = len(shards)

    def first_copies(ins, outs, sems):
        send, recv, _ = sems
        return [_remote(ins[t], outs[t].at[_my_slot()], send.at[t, k - 1], recv.at[t, k - 1], _peer(k))
                for k in (1,) + OTHER_CHIPS for t in range(n)]

    def local_copies(ins, outs, sems):
        return [pltpu.make_async_copy(ins[t], outs[t].at[_my_slot()], sems[2].at[t]) for t in range(n)]

    def start(ins, outs, sems):
        for cp in local_copies(ins, outs, sems) + first_copies(ins, outs, sems):
            cp.start()

    def finish(ins, outs, sems):
        send, recv, _ = sems
        sibling = _peer(1)
        passed = []
        for k in OTHER_CHIPS:
            for t in range(n):
                slot = outs[t].at[_peer_slot(k)]
                _remote(ins[t], slot, send.at[t, k - 1], recv.at[t, k - 1], _peer(k)).wait_recv()
                cp = _remote(slot, slot, send.at[t, k], recv.at[t, k], sibling)
                cp.start()
                passed.append(cp)
        for t in range(n):
            _remote(ins[t], outs[t].at[_peer_slot(1)], send.at[t, 0], recv.at[t, 0], sibling).wait_recv()
            for k in OTHER_CHIPS:
                _remote(ins[t], outs[t].at[_peer_slot(k + 1)], send.at[t, k], recv.at[t, k], sibling).wait_recv()
        for cp in first_copies(ins, outs, sems) + passed:
            cp.wait_send()
        for lc in local_copies(ins, outs, sems):
            lc.wait()

    out_shapes = [jax.ShapeDtypeStruct((N_DEV,) + s.shape, s.dtype) for s in shards]
    sem_shapes = [pltpu.SemaphoreType.DMA((n, N_DEV - 1)), pltpu.SemaphoreType.DMA((n, N_DEV - 1)),
                  pltpu.SemaphoreType.DMA((n,))]
    return out_shapes, sem_shapes, start, finish


def _sibling_swap(grads, *, name, host=None, then=None):
    n = len(grads)

    def copies(ins, outs, sems):
        c = lax.axis_index("c")
        return [_remote(ins[t].at[:, 1 - c], outs[t], sems[0].at[t], sems[1].at[t], _peer(1)) for t in range(n)]

    def start(ins, outs, sems):
        for cp in copies(ins, outs, sems):
            cp.start()

    def finish(ins, outs, sems):
        for cp in copies(ins, outs, sems):
            cp.wait()

    out_shapes = [jax.ShapeDtypeStruct((N_CHIP,) + g.shape[2:], g.dtype) for g in grads]
    sem_shapes = [pltpu.SemaphoreType.DMA((n,)), pltpu.SemaphoreType.DMA((n,))]
    return _exchange(grads, out_shapes, sem_shapes, start, finish, name=name, host=host, then=then)


def _pair_add(g, p, out_dtype, *, name):
    _, _, R, C = g.shape
    tr = _pick(R, (512, 256, 128, 64, 32, 16, 8))

    def body(c_ref, g_ref, p_ref, o_ref):
        o_ref[...] = (g_ref[...] + p_ref[...]).astype(out_dtype)

    grid_spec = pltpu.PrefetchScalarGridSpec(
        num_scalar_prefetch=1, grid=(N_CHIP, R // tr),
        in_specs=[pl.BlockSpec((None, None, tr, C), lambda j, i, c_ref: (j, c_ref[0], i, 0)),
                  pl.BlockSpec((None, tr, C), lambda j, i, c_ref: (j, i, 0))],
        out_specs=pl.BlockSpec((None, tr, C), lambda j, i, c_ref: (j, i, 0)))
    core = lax.axis_index("c").astype(jnp.int32).reshape(1)
    return _pcall(body, name=name, grid_spec=grid_spec, out_shape=jax.ShapeDtypeStruct(p.shape, out_dtype),
                  compiler_params=_params(("parallel", "parallel")))(core, g, p)


def _chip_exchange(sums, *, name, host=None):
    n = len(sums)

    def copies(ins, outs, sems, dst_is_mine):
        send, recv, _ = sems
        mine = 2 * lax.axis_index("x") + lax.axis_index("y")
        out = []
        for k in OTHER_CHIPS:
            theirs = _chip_of(_peer(k))
            for t in range(n):
                out.append(_remote(ins[t].at[theirs], outs[t].at[mine if dst_is_mine else theirs],
                                   send.at[t, k // 2 - 1], recv.at[t, k // 2 - 1], _peer(k)))
        return out

    def local_copies(ins, outs, sems):
        mine = 2 * lax.axis_index("x") + lax.axis_index("y")
        return [pltpu.make_async_copy(ins[t].at[mine], outs[t].at[mine], sems[2].at[t]) for t in range(n)]

    def start(ins, outs, sems):
        for cp in local_copies(ins, outs, sems) + copies(ins, outs, sems, True):
            cp.start()

    def finish(ins, outs, sems):
        for cp in copies(ins, outs, sems, False) + local_copies(ins, outs, sems):
            cp.wait()

    out_shapes = [jax.ShapeDtypeStruct(s.shape, s.dtype) for s in sums]
    sem_shapes = [pltpu.SemaphoreType.DMA((n, N_CHIP - 1)), pltpu.SemaphoreType.DMA((n, N_CHIP - 1)),
                  pltpu.SemaphoreType.DMA((n,))]
    return _exchange(sums, out_shapes, sem_shapes, start, finish, name=name, host=host)


class _GradReduce:
    def __init__(self, grads, wire_dtypes, hosts=None):
        pairs = [g.reshape((N_CHIP, 2) + g.shape[1:]) for g in grads]
        self.n, self.riders, self.recv = len(grads), [], None

        def after_swap(partner):
            sums = [_pair_add(g, p, dt, name="grad_pair_add") for g, p, dt in zip(pairs, partner, wire_dtypes)]
            if hosts is None:
                self.recv = _chip_exchange(sums, name="grad_chip_exchange")
            else:
                self.riders = [(idx, _chip_exchange([sums[i] for i in idx], name="grad_chip_exchange", host=h))
                               for h, idx in hosts[1]]

        if hosts is None:
            after_swap(_sibling_swap(pairs, name="grad_sibling_swap"))
        else:
            _sibling_swap(pairs, name="grad_sibling_swap", host=hosts[0], then=after_swap)

    def result(self):
        if self.recv is None:
            self.recv = [None] * self.n
            for idx, rider in self.riders:
                assert rider.results is not None, rider.host
                for i, r in zip(idx, rider.results):
                    self.recv[i] = r
        return self.recv


def _adamw(recv, w, m, v, layer=None, into=None, *, name):
    n_slots, R, C = recv.shape
    tr = _pick(R, (256, 128, 64, 32, 16, 8))

    def body(r_ref, w_ref, m_ref, v_ref, *rest):
        g_ref, d_ref, nm_ref, nv_ref = rest[-4:]
        g = r_ref[0].astype(F32)
        for s in range(1, n_slots):
            g = g + r_ref[s].astype(F32)
        m_new = ADAM_B1 * m_ref[...] + (1.0 - ADAM_B1) * g
        v_new = ADAM_B2 * v_ref[...] + (1.0 - ADAM_B2) * jnp.square(g)
        m_hat = m_new / (1.0 - ADAM_B1 ** ADAM_STEP)
        v_hat = v_new / (1.0 - ADAM_B2 ** ADAM_STEP)
        g_ref[...] = g
        d_ref[...] = -ADAM_LR * (m_hat / (jnp.sqrt(v_hat) + ADAM_EPS) + ADAM_WD * w_ref[...])
        nm_ref[...] = m_new
        nv_ref[...] = v_new

    row = pl.BlockSpec((tr, C), lambda i: (i, 0))
    state = row if layer is None else pl.BlockSpec((None, tr, C), lambda i: (layer, i, 0))
    in_specs = [pl.BlockSpec((n_slots, tr, C), lambda i: (0, i, 0)), state, state, state]
    if into is None:
        return _pcall(body, name=name, grid=(R // tr,), in_specs=in_specs, out_specs=[row] * 4,
                      out_shape=[jax.ShapeDtypeStruct((R, C), F32)] * 4, compiler_params=_params(("parallel",)),
                      )(recv, w, m, v)
    return _pcall(body, name=name, grid=(R // tr,), in_specs=in_specs + [pl.BlockSpec(memory_space=pl.ANY)] * 4,
                  out_specs=[state] * 4, out_shape=[jax.ShapeDtypeStruct((DEPTH, R, C), F32)] * 4,
                  input_output_aliases={4 + j: j for j in range(4)}, compiler_params=_params(("parallel",)),
                  )(recv, w, m, v, *into)


def _flat_pad(parts, rows):
    flat = jnp.concatenate([p.reshape(-1) for p in parts])
    return jnp.pad(flat, (0, rows * LANES - flat.shape[0])).reshape(rows, LANES)


def _small_shard_shape(n):
    return SMALL[n][:-1] + (SMALL[n][-1] // N_DEV,)


def _unpack_small(gathered):
    out, off = {}, 0
    flat = gathered.reshape(N_DEV, -1)
    for n in SMALL:
        r, c = _small_shard_shape(n)
        out[n] = flat[:, off:off + r * c].reshape(N_DEV, r, c).transpose(1, 0, 2).reshape(r, N_DEV * c)
        off += r * c
    return out


def _pack_state(state, prefix):
    flat = jnp.concatenate([state[prefix + n].reshape(DEPTH, -1) for n in (*SMALL, *REPL)], axis=1)
    return jnp.pad(flat, ((0, 0), (0, PACK_ROWS * LANES - flat.shape[1]))).reshape(DEPTH * PACK_ROWS, LANES)


def _pack_small_grads(grads):
    cols = []
    for n in SMALL:
        r, c = _small_shard_shape(n)
        cols.append(grads[n].reshape(r, N_DEV, c).transpose(1, 0, 2).reshape(N_DEV, r * c))
    cols += [jnp.broadcast_to(grads[n].reshape(1, -1), (N_DEV, grads[n].size)) for n in REPL]
    flat = jnp.concatenate(cols, axis=1)
    return jnp.pad(flat, ((0, 0), (0, PACK_ROWS * LANES - flat.shape[1]))).reshape(N_DEV, PACK_ROWS, LANES)


def _unpack_state(packed):
    out, off = {}, 0
    flat = packed.reshape(DEPTH, -1)
    for n, shape in [(n, _small_shard_shape(n)) for n in SMALL] + list(REPL.items()):
        sz = math.prod(shape)
        out[n] = flat[:, off:off + sz].reshape((DEPTH,) + shape)
        off += sz
    return out


def _pad_b_in(b):
    parts, pos = [], 0
    for src, width, dst in Z_PIECES:
        parts += [jnp.zeros((b.shape[0], dst - pos), b.dtype), b[:, src:src + width]]
        pos = dst + width
    return jnp.concatenate(parts + [jnp.zeros((b.shape[0], Z_W - pos), b.dtype)], axis=1)


def _unpad_b_in(bp):
    return jnp.concatenate([bp[:, dst:dst + width] for _, width, dst in Z_PIECES], axis=1)


def _up_shard_pad(t):
    gap = jnp.zeros(t.shape[:-1] + (FF_HALF_P - FF_HALF,), t.dtype)
    return jnp.concatenate([t[..., :FF_HALF], gap, t[..., FF_HALF:], gap], axis=-1)


def _up_shard_unpad(t):
    return jnp.concatenate([t[..., :FF_HALF], t[..., FF_HALF_P:FF_HALF_P + FF_HALF]], axis=-1)


def _pad_shard(n, t):
    lead = [(0, 0)] * (t.ndim - 2)
    if n == "w_in":
        return jnp.pad(t, lead + [(0, 0), (0, W_IN_SLOT - W_IN_SHARD)])
    if n == "ffn_w_up":
        return _up_shard_pad(t)
    if n == "ffn_w_down":
        return jnp.pad(t, lead + [(0, FF_HALF_P - FF_HALF), (0, 0)])
    return t


def _unpad_shard(n, t):
    if n == "w_in":
        return t[..., :W_IN_SHARD]
    if n == "ffn_w_up":
        return _up_shard_unpad(t)
    if n == "ffn_w_down":
        return t[..., :FF_HALF, :]
    return t


GATHER_HOSTS = (("fox_fwd", (0,)), ("chunk_fwd", (8, 9)), ("mm_up", (1, 2, 3, 4, 5, 6, 7, 10)))
FIRST_GATHER_HOSTS = ((None, (0,)), ("mm_in", (1, 2, 3, 4, 5, 6, 7, 10)), ("s5_scan", (8, 9)))
REDUCE_HOSTS = ("mm_up_dw", (("fox_bwd", (0,)), ("s5_scan_bwd", (1, 2, 3, 4, 5, 6, 7, 10)), ("chunk_bwd", (8,)),
                             ("mm_in_dx", (9,))))


def _layer_shards(state, layer):
    shards = [_pad_shard(n, state[n][layer].astype(BF16)) for n in BIG]
    return shards + [_flat_pad([state[n][layer] for n in SMALL], SMALL_ROWS)]


class _Weights:
    def __init__(self, sources):
        self._sources, self._cache = sources, {}

    def _operand(self, i):
        src = self._sources[i]
        if isinstance(src, tuple):
            assert src[0].results is not None, src[0].host
            return src[0].results[src[1]]
        return src

    def __getitem__(self, n):
        if n not in self._cache:
            if n in SMALL:
                self._cache.update(_unpack_small(self._operand(len(BIG))))
            elif n == "w_in_p":
                self._cache[n] = _colmap(self._operand(0), inverse=False, name="w_in_colmap")
            else:
                w = self._operand(list(BIG).index(n))
                self._cache[n] = w.reshape(-1, D_MODEL) if BIG[n][1] == 0 else w
        return self._cache[n]


def _gather_riding(shards, hosts):
    sources = [None] * len(shards)
    for host, idx in hosts:
        group = [shards[i] for i in idx]
        if host is None:
            for i, g in zip(idx, _all_gather(group, name="all_gather_weights")):
                sources[i] = g
        else:
            rider = _all_gather(group, name="all_gather_weights", host=host)
            for pos, i in enumerate(idx):
                sources[i] = (rider, pos)
    return sources


def _ssm_params(p):
    prep1_in = (p["ssm_lambda_re"], p["ssm_lambda_im"], p["ssm_log_dt"][:, None])
    ar, ai, gr, gi = _whole(_ssm_prep1, *prep1_in, name="ssm_prep1")
    to_cn = lambda b: b.transpose(2, 0, 1).reshape(SSM_GROUP, N_SSM_CH)
    prep2_in = (gr.reshape(1, N_SSM_CH), gi.reshape(1, N_SSM_CH), to_cn(p["ssm_b_re"]), to_cn(p["ssm_b_im"]))
    bbr, bbi = _whole(_ssm_prep2, *prep2_in, name="ssm_prep2")
    to_gcp = lambda t: t.reshape(SSM_GROUP, SSM_GROUPS, SSM_STATE).transpose(1, 0, 2)
    bb = _blockdiag(to_gcp(bbr), to_gcp(bbi))
    cct = _blockdiag(p["ssm_c_re"], -p["ssm_c_im"])
    a_vec = _to_blocked(ar.reshape(1, N_SSM_CH), ai.reshape(1, N_SSM_CH))
    return dict(bb=bb, cct=cct, a_vec=a_vec, prep1_in=prep1_in, prep2_in=prep2_in)


def _layer_fwd(x, mem, p, W):
    sp = _ssm_params(p)
    z = _mm(x, W["w_in_p"], bias=p["b_in_p"], name="mm_in")
    f_t = z[:, Z_FF:Z_FF + FOX_HEADS].T
    cum = _cum_heads(f_t, name="fox_cum")
    ya_pre, lse_a = _fox_fwd((z, Z_FQ), (z, Z_FK), (z, Z_FV), cum, name="fox_fwd")
    ya = _mm(ya_pre, W["w_fox_o"], b_slots=True, name="mm_fox_o")
    x_ri = _mm_bd("in", z, sp["bb"], a_off=Z_SU, name="mm_s5_in")
    h_ri = _scan(x_ri, sp["a_vec"], name="s5_scan")
    hc = _mm_bd("out", h_ri, sp["cct"], name="mm_s5_out")
    d_row = p["ssm_d"][None, :]
    (gel,) = _rowwise(lambda a, b, c: (_s5_tail(a, b, c),), [hc, Win(z, 512, Z_SU // 512)], [d_row], name="s5_tail")
    yb2 = _mm(gel, W["w_ssm_glu"], b_slots=True, name="mm_glu")
    bias = _relbias_expand(W["ca_rel_bias"]).transpose(1, 0, 2)
    kv_band = jnp.pad(z[:, Z_CK:Z_CK + 2 * CA_WIDTH].astype(BF16), ((CA_PAD, 0), (0, 0)))
    yc_pre, lse_c = _attn_fwd("chunk", (z, Z_CQ), (kv_band, 0), (kv_band, CA_WIDTH), (bias,), name="chunk_fwd")
    yc = _mm(yc_pre, W["w_ca_o"], b_slots=True, name="mm_ca_o")
    gates = [Win(z, 1024, Z_GF // 1024), Win(z, 1024, Z_GS // 1024), Win(z, 1024, Z_GC // 1024)]
    (merged,) = _rowwise(lambda *a: (_merge(*a),), gates + [ya, yb2, yc], name="merge")
    h1 = _mm(merged, W["w_o"], name="mm_o")
    ln_g, ln_b = W["ln_g"], W["ln_b"]
    r1, x1 = _rowwise(_ln_fwd, [x, h1], [ln_g[0:1], ln_b[0:1]], name="ln_fwd")
    q = _mm(x1, W["xa_wq"], name="mm_xq")
    kv = _mm(mem, W["xa_wkv"], b_slots=True, name="mm_xkv")
    o, lse_x = _attn_fwd("xa", (q, 0), (kv, 0), (kv, D_MODEL), name="xa_fwd")
    h2 = _mm(o, W["xa_wo"], name="mm_xo")
    r2, x2 = _rowwise(_ln_fwd, [x1, h2], [ln_g[1:2], ln_b[1:2]], name="ln_fwd")
    up = _mm(x2, W["ffn_w_up"], b_slots=True, name="mm_up")
    hmid = _ffn_mid(up, _ff_pad(W["ffn_conv_w"]), _ff_pad(p["ffn_conv_b"][None, :]), name="ffn_mid")
    h3 = _mm(hmid, W["ffn_w_down"], name="mm_down")
    r3, x3 = _rowwise(_ln_fwd, [x2, h3], [ln_g[2:3], ln_b[2:3]], name="ln_fwd")
    res = dict(x=x, z=z, cum=cum, lse_a=lse_a, ya_pre=ya_pre, ya=ya, h_ri=h_ri, hc=hc, gel=gel, yb2=yb2, lse_c=lse_c,
               yc_pre=yc_pre, yc=yc, merged=merged, r1=r1, x1=x1, q=q, kv=kv, o=o, lse_x=lse_x, r2=r2, x2=x2, up=up,
               hmid=hmid, r3=r3, bias=bias, sp=sp, kv_band=kv_band, W=W)
    return x3, res


def _layer_bwd(dx3, mem, p, res):
    W = res["W"]
    x, z = res["x"], res["z"]
    sp = res["sp"]
    ln_g = W["ln_g"]
    big, small = {}, {}
    slots = lambda t: t.reshape(N_DEV, -1, D_MODEL)
    dr3, dg2, db2 = _rowwise(_ln_bwd, [res["r3"], dx3], [ln_g[2:3]], n_red=2, name="ln_bwd")
    dhmid = _mm(dr3, W["ffn_w_down"], tb=True, name="mm_down_dx")
    big["ffn_w_down"] = slots(_mm(res["hmid"], dr3, ta=True, name="mm_down_dw"))
    conv_w_p, conv_b_p = _ff_pad(W["ffn_conv_w"]), _ff_pad(p["ffn_conv_b"][None, :])
    dup_a, dup_g, dcw, dcb = _ffn_mid(res["up"], conv_w_p, conv_b_p, dhmid, name="ffn_mid_bwd")
    dup = jnp.concatenate([dup_a, dup_g], axis=1)
    small["ffn_conv_w"], small["ffn_conv_b"] = _ff_unpad(dcw), _ff_unpad(dcb)[0]
    dx2 = _mm(dup, W["ffn_w_up"], tb=True, b_slots=True, add=(dr3, DN_ALPHA), name="mm_up_dx")
    big["ffn_w_up"] = _mm(res["x2"], dup, ta=True, out_slots=2 * FF_HALF_P, name="mm_up_dw")
    dr2, dg1, db1 = _rowwise(_ln_bwd, [res["r2"], dx2], [ln_g[1:2]], n_red=2, name="ln_bwd")
    do = _mm(dr2, W["xa_wo"], tb=True, name="mm_xo_dx")
    big["xa_wo"] = slots(_mm(res["o"], dr2, ta=True, name="mm_xo_dw"))
    kv = res["kv"]
    dq, dk, dv = _attn_bwd("xa", (res["q"], 0), (kv, 0), (kv, D_MODEL), res["lse_x"], (do, 0), name="xa_bwd")
    dkv = jnp.concatenate([dk, dv], axis=1)
    dx1 = _mm(dq, W["xa_wq"], tb=True, add=(dr2, DN_ALPHA), name="mm_xq_dx")
    big["xa_wq"] = slots(_mm(res["x1"], dq, ta=True, name="mm_xq_dw"))
    big["xa_wkv"] = _mm(mem, dkv, ta=True, out_slots=256, name="mm_xkv_dw")
    dr1, dg0, db0 = _rowwise(_ln_bwd, [res["r1"], dx1], [ln_g[0:1]], n_red=2, name="ln_bwd")
    small["ln_g"] = jnp.concatenate([dg0, dg1, dg2], axis=0)
    small["ln_b"] = jnp.concatenate([db0, db1, db2], axis=0)
    dmerged = _mm(dr1, W["w_o"], tb=True, name="mm_o_dx")
    big["w_o"] = slots(_mm(res["merged"], dr1, ta=True, name="mm_o_dw"))
    gates = [Win(z, 1024, Z_GF // 1024), Win(z, 1024, Z_GS // 1024), Win(z, 1024, Z_GC // 1024)]
    dgf, dgs, dgc, dya, dyb2, dyc = _rowwise(_vjp_of(_merge, 6), gates + [res["ya"], res["yb2"], res["yc"], dmerged],
                                             name="merge_bwd")
    dya_pre = _mm(dya, W["w_fox_o"], tb=True, b_slots=True, name="mm_fox_o_dx")
    big["w_fox_o"] = _mm(res["ya_pre"], dya, ta=True, out_slots=128, name="mm_fox_o_dw")
    dfq, dfk, dfv, dcum = _fox_bwd((z, Z_FQ), (z, Z_FK), (z, Z_FV), res["lse_a"], (dya_pre, 0), (res["ya_pre"], 0),
                                   res["cum"], name="fox_bwd")
    f_t = z[:, Z_FF:Z_FF + FOX_HEADS].T
    dff = _cum_heads(f_t, dcum, name="fox_cum_bwd")
    dgel = _mm(dyb2, W["w_ssm_glu"], tb=True, b_slots=True, name="mm_glu_dx")
    big["w_ssm_glu"] = _mm(res["gel"], dyb2, ta=True, out_slots=256, name="mm_glu_dw")
    d_row = p["ssm_d"][None, :]
    su_win = Win(z, 512, Z_SU // 512)
    dy, dsu1, dd = _rowwise(_s5_tail_bwd, [res["hc"], su_win, dgel], [d_row], n_red=1, name="s5_tail_bwd")
    small["ssm_d"] = dd[0]
    dh_ri = _mm_bd("in", dy, sp["cct"], name="mm_s5_out_dx")
    dcct = _mm_bd("dw", dy, res["h_ri"], name="mm_s5_out_dw")
    dx_ri, da_vec = _scan(dh_ri, sp["a_vec"], res["h_ri"], name="s5_scan_bwd")
    dsu = _mm_bd("out", dx_ri, sp["bb"], add=(dsu1, 1.0), name="mm_s5_in_dx")
    dbb = _mm_bd("dw", z, dx_ri, a_off=Z_SU, name="mm_s5_in_dw")
    dcr, dci = _blockdiag_inv(dcct)
    small["ssm_c_re"], small["ssm_c_im"] = dcr, -dci
    dbbr, dbbi = _blockdiag_inv(dbb)
    to_cn = lambda t: t.transpose(1, 0, 2).reshape(SSM_GROUP, N_SSM_CH)
    dgr, dgi, dbr, dbi = _whole(_vjp_of(_ssm_prep2, 4), *sp["prep2_in"], to_cn(dbbr), to_cn(dbbi), name="ssm_prep2_bwd")
    from_cn = lambda t: t.reshape(SSM_GROUP, SSM_GROUPS, SSM_STATE).transpose(1, 2, 0)
    small["ssm_b_re"], small["ssm_b_im"] = from_cn(dbr), from_cn(dbi)
    dar, dai = _from_blocked(da_vec)
    sq = lambda t: t.reshape(SSM_GROUPS, SSM_STATE)
    dlr, dli, dldt = _whole(_vjp_of(_ssm_prep1, 3), *sp["prep1_in"], sq(dar), sq(dai), sq(dgr), sq(dgi),
                            name="ssm_prep1_bwd")
    small["ssm_lambda_re"], small["ssm_lambda_im"], small["ssm_log_dt"] = dlr, dli, dldt[:, 0]
    dyc_pre = _mm(dyc, W["w_ca_o"], tb=True, b_slots=True, name="mm_ca_o_dx")
    big["w_ca_o"] = _mm(res["yc_pre"], dyc, ta=True, out_slots=128, name="mm_ca_o_dw")
    bias = res["bias"]
    kv_band = res["kv_band"]
    dcq_, dck_band, dcv_band, dbias = _attn_bwd("chunk", (z, Z_CQ), (kv_band, 0), (kv_band, CA_WIDTH), res["lse_c"],
                                                (dyc_pre, 0), (bias,), name="chunk_bwd")
    small["ca_rel_bias"] = _relbias_reduce(dbias.transpose(1, 0, 2))
    dff_p = jnp.pad(dff.T, ((0, 0), (0, 512 - FOX_HEADS)))
    dz = jnp.concatenate([dfq, dfk, dfv, dff_p, dsu, dcq_, dck_band[CA_PAD:], dcv_band[CA_PAD:], dgf, dgs, dgc],
                         axis=1)
    dx = _mm(dz, W["w_in_p"], tb=True, add=(dr1, DN_ALPHA), name="mm_in_dx")
    big["w_in"] = _colmap(_mm(x, dz, ta=True, name="mm_in_dw"), inverse=True, name="w_in_colmap_inv")
    (db_in_p,) = _rowwise(lambda t: (jnp.sum(t, axis=0, keepdims=True),), [dz], n_red=1, name="colsum")
    small["b_in"] = _unpad_b_in(db_in_p)[0]
    return dx, big, small


def kernel(x, mem, w_in, b_in, ssm_lambda_re, ssm_lambda_im, ssm_log_dt, ssm_b_re, ssm_b_im, ssm_c_re, ssm_c_im, ssm_d, ca_rel_bias, w_fox_o, w_ssm_glu, w_ca_o, w_o, xa_wq, xa_wkv, xa_wo, ffn_w_up, ffn_conv_w, ffn_conv_b, ffn_w_down, ln_g, ln_b, loss_target, m_w_in, m_b_in, m_ssm_lambda_re, m_ssm_lambda_im, m_ssm_log_dt, m_ssm_b_re, m_ssm_b_im, m_ssm_c_re, m_ssm_c_im, m_ssm_d, m_ca_rel_bias, m_w_fox_o, m_w_ssm_glu, m_w_ca_o, m_w_o, m_xa_wq, m_xa_wkv, m_xa_wo, m_ffn_w_up, m_ffn_conv_w, m_ffn_conv_b, m_ffn_w_down, m_ln_g, m_ln_b, v_w_in, v_b_in, v_ssm_lambda_re, v_ssm_lambda_im, v_ssm_log_dt, v_ssm_b_re, v_ssm_b_im, v_ssm_c_re, v_ssm_c_im, v_ssm_d, v_ca_rel_bias, v_w_fox_o, v_w_ssm_glu, v_w_ca_o, v_w_o, v_xa_wq, v_xa_wkv, v_xa_wo, v_ffn_w_up, v_ffn_conv_w, v_ffn_conv_b, v_ffn_w_down, v_ln_g, v_ln_b):
    given = dict(locals())
    state = {pre + n: given[pre + n] for n in WEIGHTS for pre in ("", "m_", "v_")}
    mem0 = mem[0]
    b_in_p = _pad_b_in(b_in)
    layer_params = [{**{n: state[n][l] for n in REPL}, "b_in_p": b_in_p[l:l + 1]} for l in range(DEPTH)]

    _RIDERS.clear()
    h, residuals = x[0], []
    sources = _gather_riding(_layer_shards(state, 0), FIRST_GATHER_HOSTS)
    for l in range(DEPTH):
        following = _gather_riding(_layer_shards(state, l + 1), GATHER_HOSTS) if l + 1 < DEPTH else None
        h, res = _layer_fwd(h, mem0, layer_params[l], _Weights(sources))
        residuals.append(res)
        sources = following
    dh, loss_cols = _rowwise(_loss_rows, [h, loss_target[0]], n_red=1, name="loss")
    loss = lax.psum(jnp.sum(loss_cols), ("x", "y", "c"))

    outs = [None] * DEPTH
    big_out = {n: None for n in BIG}
    padded = {pre + n: _pad_shard(n, state[pre + n]) for n in BIG for pre in ("", "m_", "v_")}
    wire = [BF16] * len(BIG) + [F32]
    pending = None
    for l in reversed(range(-1, DEPTH)):
        if l >= 0:
            dh, big, small = _layer_bwd(dh, mem0, layer_params[l], residuals[l])
            reduce = _GradReduce([big[n] for n in BIG] + [_pack_small_grads(small)], wire,
                                 hosts=REDUCE_HOSTS if l > 0 else None)
        done, pending = pending, (l, reduce) if l >= 0 else None
        if done is None:
            continue
        l_done, reduce_done = done
        *recv_big, recv_small = reduce_done.result()
        for n, recv in zip(BIG, recv_big):
            if big_out[n] is None:
                big_out[n] = [lax.empty((DEPTH,) + recv.shape[1:], F32) for _ in range(4)]
            big_out[n] = _adamw(recv, *[padded[pre + n] for pre in ("", "m_", "v_")], l_done, big_out[n],
                                name="adamw_" + n)
        outs[l_done] = recv_small

    assert not _RIDERS, [r.host for r in _RIDERS]
    packed = _adamw(jnp.concatenate(outs, axis=1), *[_pack_state(state, pre) for pre in ("", "m_", "v_")],
                    name="adamw_small")
    small_out = [_unpack_state(t) for t in packed]
    result = lambda n, j: _unpad_shard(n, big_out[n][j]) if n in BIG else small_out[j][n]
    return (loss, dh[None], *[result(n, j) for j in range(4) for n in WEIGHTS])
```

```python
import functools
import math

import jax
import jax.numpy as jnp
from jax import lax
from jax.experimental import pallas as pl
from jax.experimental.pallas import tpu as pltpu

F32, BF16 = jnp.float32, jnp.bfloat16

D_MODEL = 1024
DEPTH = 4
CHUNK = 64
FOX_HEADS, FOX_HEAD_DIM, FOX_WIDTH = 8, 64, 512
SSM_GROUP, SSM_WIDTH, SSM_GROUPS, SSM_STATE = 16, 512, 32, 64
CA_HEADS, CA_HEAD_DIM, CA_WIDTH, CA_LEFT_CHUNKS = 8, 64, 512, 8
CA_BAND = (CA_LEFT_CHUNKS + 1) * CHUNK
CA_PAD = CA_LEFT_CHUNKS * CHUNK
REL_MIN, REL_MAX = -(CHUNK - 1), 4 * CHUNK
N_REL = REL_MAX - REL_MIN + 1
XA_HEADS, XA_HEAD_DIM = 4, 256
D_FF = 2816
DN_ALPHA = (2 * DEPTH) ** 0.25
LN_EPS = 1e-5
NEG_INF = -1e30
ADAM_LR, ADAM_B1, ADAM_B2, ADAM_EPS, ADAM_WD, ADAM_STEP = 0.001, 0.9, 0.999, 1e-08, 0.01, 10

N_DEV = 8
LANE = 128
N_SSM_CH = SSM_GROUPS * SSM_STATE
SCAN_CB = 256
N_IN = 6664
W_IN_SHARD, W_IN_SLOT = N_IN // N_DEV, 896
Z_W = 7168
Z_FQ, Z_FK, Z_FV, Z_FF, Z_SU, Z_CQ, Z_CK, Z_CV, Z_GF, Z_GS, Z_GC = (
    0, 512, 1024, 1536, 2048, 2560, 3072, 3584, 4096, 5120, 6144)
Z_PIECES = ((0, 512, Z_FQ), (512, 512, Z_FK), (1024, 512, Z_FV), (1536, 8, Z_FF), (1544, 512, Z_SU),
            (2056, 512, Z_CQ), (2568, 512, Z_CK), (3080, 512, Z_CV), (3592, 1024, Z_GF), (4616, 1024, Z_GS),
            (5640, 1024, Z_GC))
FF_HALF, FF_HALF_P = D_FF // N_DEV, 384
D_FF_P = N_DEV * FF_HALF_P

VMEM_LIMIT_BYTES = 56 * 1024 * 1024

BIG = {"w_in": ((1024, W_IN_SHARD), 1), "w_fox_o": ((512, 128), 1), "w_ssm_glu": ((512, 256), 1),
       "w_ca_o": ((512, 128), 1), "w_o": ((128, 1024), 0), "xa_wq": ((128, 1024), 0), "xa_wkv": ((1024, 256), 1),
       "xa_wo": ((128, 1024), 0), "ffn_w_up": ((1024, 2 * FF_HALF), 1), "ffn_w_down": ((FF_HALF, 1024), 0)}
SMALL = {"ca_rel_bias": (8, 320), "ffn_conv_w": (3, D_FF), "ln_g": (3, 1024), "ln_b": (3, 1024)}
REPL = {"b_in": (N_IN,), "ssm_lambda_re": (32, 64), "ssm_lambda_im": (32, 64), "ssm_log_dt": (32,),
        "ssm_b_re": (32, 64, 16), "ssm_b_im": (32, 64, 16), "ssm_c_re": (32, 16, 64), "ssm_c_im": (32, 16, 64),
        "ssm_d": (512,), "ffn_conv_b": (D_FF,)}
WEIGHTS = ("w_in", "b_in", "ssm_lambda_re", "ssm_lambda_im", "ssm_log_dt", "ssm_b_re", "ssm_b_im", "ssm_c_re",
           "ssm_c_im", "ssm_d", "ca_rel_bias", "w_fox_o", "w_ssm_glu", "w_ca_o", "w_o", "xa_wq", "xa_wkv", "xa_wo",
           "ffn_w_up", "ffn_conv_w", "ffn_conv_b", "ffn_w_down", "ln_g", "ln_b")
LANES = 1024
PACK_ROWS = 256
SMALL_ROWS = 8


def _w_in_segments():
    segs = []
    for src, width, dst in Z_PIECES:
        n = src
        while n < src + width:
            d = n // W_IN_SHARD
            end = min(src + width, (d + 1) * W_IN_SHARD)
            segs.append((W_IN_SLOT * d + n - W_IN_SHARD * d, dst + n - src, end - n))
            n = end
    return tuple(segs)


W_IN_SEGS = _w_in_segments()


def _pallas(body, **kw):
    return pl.pallas_call(body, **kw)


def _params(sem):
    return pltpu.CompilerParams(dimension_semantics=sem, vmem_limit_bytes=VMEM_LIMIT_BYTES)


class _Rider:
    def __init__(self, host, ins, out_shapes, sem_shapes, start, finish, then=None):
        self.host, self.ins, self.out_shapes, self.sem_shapes = host, list(ins), list(out_shapes), list(sem_shapes)
        self.start, self.finish, self.then, self.results = start, finish, then, None


_RIDERS = []


def _pcall(body, *, name, **kw):
    rider = next((r for r in _RIDERS if r.host == name), None)
    if rider is None:
        return _pallas(body, name=name, **kw)
    _RIDERS.remove(rider)
    grid, in_specs, scratch = kw["grid"], list(kw["in_specs"]), list(kw.get("scratch_shapes", ()))
    single = not isinstance(kw["out_shape"], (list, tuple))
    out_specs = [kw["out_specs"]] if single else list(kw["out_specs"])
    out_shape = [kw["out_shape"]] if single else list(kw["out_shape"])
    n_in, n_out, n_scr = len(in_specs), len(out_specs), len(scratch)
    r_in, r_out = len(rider.ins), len(rider.out_shapes)

    def fused(*refs):
        a, ra = refs[:n_in], refs[n_in:n_in + r_in]
        o, ro = refs[n_in + r_in:n_in + r_in + n_out], refs[n_in + r_in + n_out:n_in + r_in + n_out + r_out]
        scr, sems = refs[n_in + r_in + n_out + r_out:][:n_scr], refs[n_in + r_in + n_out + r_out + n_scr:]
        ids = [pl.program_id(d) for d in range(len(grid))]
        first = functools.reduce(jnp.logical_and, [i == 0 for i in ids])
        last = functools.reduce(jnp.logical_and, [i == g - 1 for i, g in zip(ids, grid)])

        @pl.when(first)
        def _():
            rider.start(ra, ro, sems)

        body(*a, *o, *scr)

        @pl.when(last)
        def _():
            rider.finish(ra, ro, sems)

    hbm = pl.BlockSpec(memory_space=pl.ANY)
    call = _pallas(fused, name=name, grid=grid, in_specs=in_specs + [hbm] * r_in, out_specs=out_specs + [hbm] * r_out,
                   out_shape=out_shape + rider.out_shapes, scratch_shapes=scratch + rider.sem_shapes,
                   compiler_params=_params(("arbitrary",) * len(grid)))

    def run(*operands):
        outs = call(*operands, *rider.ins)
        rider.results = list(outs[n_out:])
        if rider.then is not None:
            rider.then(rider.results)
        return outs[0] if single else list(outs[:n_out])

    return run


def _pick(dim, prefs):
    for p in prefs:
        if dim % p == 0:
            return p
    return dim


def _mm(a, b, *, ta=False, tb=False, bias=None, add=None, a_off=0, a_cols=None, b_slots=False, out_slots=None,
        name, out_dtype=F32):
    a_cols = a_cols if a_cols is not None else a.shape[1]
    M, K = (a_cols, a.shape[0]) if ta else (a.shape[0], a_cols)
    tm = _pick(M, (1024, 512, 256, 128))
    if ta:
        tk = _pick(K, (2048, 1024, 512, 256))
    elif b_slots and tb:
        tk = _pick(b.shape[2], (1024, 768, 512, 256, 128))
    else:
        tk = K if K <= 3072 else _pick(K, (1024, 512, 256, 128))
    nk = K // tk
    if b_slots:
        ns = b.shape[2]
        if tb:
            N = b.shape[1]
            tn = _pick(N, (512, 256, 128))
            per = ns // tk
            b_spec = pl.BlockSpec((None, tn, tk), lambda i, j, k: (k // per, j, k % per))
            b_dim = 1
            assert N_DEV * ns == K
        else:
            N = N_DEV * ns
            tn = _pick(ns, (512, 256, 128))
            per = ns // tn
            b_spec = pl.BlockSpec((None, tk, tn), lambda i, j, k: (j // per, k, j % per))
            b_dim = 0
            assert b.shape[1] == K
    else:
        N = b.shape[0] if tb else b.shape[1]
        assert (b.shape[1] if tb else b.shape[0]) == K, (a.shape, b.shape, ta, tb)
        tn = _pick(N if out_slots is None else out_slots, (512, 256, 128))
        if tb:
            b_spec = pl.BlockSpec((tn, tk), lambda i, j, k: (j, k))
            b_dim = 1
        else:
            b_spec = pl.BlockSpec((tk, tn), lambda i, j, k: (k, j))
            b_dim = 0
    if ta:
        assert a_off % tm == 0
        a_spec = pl.BlockSpec((tk, tm), lambda i, j, k: (k, i + a_off // tm))
        a_dim = 0
    else:
        assert a_off % tk == 0
        a_spec = pl.BlockSpec((tm, tk), lambda i, j, k: (i, k + a_off // tk))
        a_dim = 1
    cache_at = ta and nk == 1
    dims = (((1 if cache_at else a_dim,), (b_dim,)), ((), ()))
    ins, specs = [a, b], [a_spec, b_spec]
    if bias is not None:
        ins.append(bias)
        specs.append(pl.BlockSpec((1, tn), lambda i, j, k: (0, j)))
    add_scale = None
    if add is not None:
        ins.append(add[0])
        add_scale = add[1]
        specs.append(pl.BlockSpec((tm, tn), lambda i, j, k: (i, j)))
    if out_slots is None:
        out_spec = pl.BlockSpec((tm, tn), lambda i, j, k: (i, j))
        out_shape = jax.ShapeDtypeStruct((M, N), out_dtype)
    else:
        assert N == N_DEV * out_slots
        per_o = out_slots // tn
        out_spec = pl.BlockSpec((None, tm, tn), lambda i, j, k: (j // per_o, i, j % per_o))
        out_shape = jax.ShapeDtypeStruct((N_DEV, M, out_slots), out_dtype)

    def body(*refs):
        a_ref, b_ref = refs[0], refs[1]
        pos = 2
        bias_ref = add_ref = None
        if bias is not None:
            bias_ref = refs[pos]
            pos += 1
        if add is not None:
            add_ref = refs[pos]
            pos += 1
        o_ref = refs[pos]
        acc_ref = refs[pos + 1] if nk > 1 else None
        if cache_at:
            at_ref = refs[pos + 1]

            @pl.when(pl.program_id(1) == 0)
            def _():
                step = min(tk, 256)
                for c in range(0, tk, step):
                    at_ref[:, c:c + step] = a_ref[c:c + step, :].T.astype(BF16)

            lhs = at_ref[...]
        else:
            lhs = a_ref[...].astype(BF16)
        part = lax.dot_general(lhs, b_ref[...].astype(BF16), dims, preferred_element_type=F32)

        def finish(acc):
            if bias_ref is not None:
                acc = acc + bias_ref[...]
            if add_ref is not None:
                acc = acc + add_scale * add_ref[...]
            o_ref[...] = acc.astype(out_dtype)

        if nk == 1:
            finish(part)
        else:
            k = pl.program_id(2)

            @pl.when(k == 0)
            def _():
                acc_ref[...] = part

            @pl.when(k > 0)
            def _():
                acc_ref[...] += part

            @pl.when(k == nk - 1)
            def _():
                finish(acc_ref[...])

    return _pcall(
        body, name=name, grid=(M // tm, N // tn, nk), in_specs=specs, out_specs=out_spec, out_shape=out_shape,
        scratch_shapes=[pltpu.VMEM((tm, tn), F32)] if nk > 1 else [pltpu.VMEM((tm, tk), BF16)] if cache_at else [],
        compiler_params=_params(("parallel", "arbitrary", "arbitrary")),
    )(*ins)


SSM_BD = 4
SSM_BD_IN, SSM_BD_ST = SSM_WIDTH // SSM_BD, 2 * N_SSM_CH // SSM_BD


def _mm_bd(form, a, b, *, a_off=0, add=None, name):
    S = a.shape[0]
    off = a_off // SSM_BD_IN
    if form == "dw":
        def body(a_ref, b_ref, o_ref):
            at = a_ref[...].T.astype(BF16)
            o_ref[...] = lax.dot_general(at, b_ref[...].astype(BF16), _NN, preferred_element_type=F32)

        return _pcall(body, name=name, grid=(SSM_BD,),
                      in_specs=[pl.BlockSpec((S, SSM_BD_IN), lambda j: (0, off + j)),
                                pl.BlockSpec((S, SSM_BD_ST), lambda j: (0, j))],
                      out_specs=pl.BlockSpec((None, SSM_BD_IN, SSM_BD_ST), lambda j: (j, 0, 0)),
                      out_shape=jax.ShapeDtypeStruct((SSM_BD, SSM_BD_IN, SSM_BD_ST), F32),
                      compiler_params=_params(("parallel",)))(a, b)
    tm = _pick(S, (1024, 512, 256))
    wide, narrow = (SSM_BD_ST, SSM_BD_IN) if form == "in" else (SSM_BD_IN, SSM_BD_ST)
    dims = _NN if form == "in" else _NT

    def body(a_ref, w_ref, *rest):
        acc = lax.dot_general(a_ref[...].astype(BF16), w_ref[...].astype(BF16), dims, preferred_element_type=F32)
        if add is not None:
            acc = acc + add[1] * rest[0][...]
        rest[-1][...] = acc

    specs = [pl.BlockSpec((tm, narrow), lambda i, j: (i, off + j)),
             pl.BlockSpec((None, SSM_BD_IN, SSM_BD_ST), lambda i, j: (j, 0, 0))]
    ins = [a, b]
    if add is not None:
        specs.append(pl.BlockSpec((tm, wide), lambda i, j: (i, j)))
        ins.append(add[0])
    return _pcall(body, name=name, grid=(S // tm, SSM_BD), in_specs=specs,
                  out_specs=pl.BlockSpec((tm, wide), lambda i, j: (i, j)),
                  out_shape=jax.ShapeDtypeStruct((S, SSM_BD * wide), F32),
                  compiler_params=_params(("parallel", "parallel")))(*ins)


class Win:
    def __init__(self, arr, width, blk):
        self.arr, self.width, self.blk = arr, width, blk


def _rowwise(fn, rows, vecs=(), *, n_red=0, tr=256, name):
    wins = [r if isinstance(r, Win) else Win(r, r.shape[1], 0) for r in rows]
    S = wins[0].arr.shape[0]
    tr = min(tr, S)
    tile_args = [jax.ShapeDtypeStruct((tr, w.width), w.arr.dtype) for w in wins]
    tile_args += [jax.ShapeDtypeStruct(v.shape, v.dtype) for v in vecs]
    outs = jax.eval_shape(fn, *tile_args)
    n_row = len(outs) - n_red
    specs = [pl.BlockSpec((tr, w.width), functools.partial(lambda i, b: (i, b), b=w.blk)) for w in wins]
    specs += [pl.BlockSpec(v.shape, functools.partial(lambda i, nd: (0,) * nd, nd=v.ndim)) for v in vecs]
    out_specs = [pl.BlockSpec((tr, o.shape[1]), lambda i: (i, 0)) for o in outs[:n_row]]
    out_specs += [pl.BlockSpec(o.shape, functools.partial(lambda i, nd: (0,) * nd, nd=len(o.shape))) for o in outs[n_row:]]
    out_shape = [jax.ShapeDtypeStruct((S, o.shape[1]), o.dtype) for o in outs[:n_row]]
    out_shape += [jax.ShapeDtypeStruct(o.shape, o.dtype) for o in outs[n_row:]]
    n_in = len(wins) + len(vecs)

    def body(*refs):
        res = fn(*[r[...] for r in refs[:n_in]])
        o_refs = refs[n_in:]
        for o_ref, r in zip(o_refs[:n_row], res[:n_row]):
            o_ref[...] = r.astype(o_ref.dtype)
        i = pl.program_id(0)
        for o_ref, r in zip(o_refs[n_row:], res[n_row:]):
            @pl.when(i == 0)
            def _(o_ref=o_ref, r=r):
                o_ref[...] = r

            @pl.when(i > 0)
            def _(o_ref=o_ref, r=r):
                o_ref[...] += r

    return _pcall(
        body, name=name, grid=(S // tr,), in_specs=specs, out_specs=out_specs, out_shape=out_shape,
        compiler_params=_params(("arbitrary",)),
    )(*[w.arr for w in wins], *vecs)


def _whole(fn, *arrays, name):
    outs = jax.eval_shape(fn, *arrays)
    n_in = len(arrays)

    def body(*refs):
        res = fn(*[r[...] for r in refs[:n_in]])
        for o_ref, r in zip(refs[n_in:], res):
            o_ref[...] = r

    vm = pl.BlockSpec(memory_space=pltpu.VMEM)
    return _pcall(body, name=name, in_specs=[vm] * n_in, out_specs=[vm] * len(outs),
                  out_shape=[jax.ShapeDtypeStruct(o.shape, o.dtype) for o in outs])(*arrays)


def _split3(x):
    hi = x.astype(BF16)
    r = x - hi.astype(F32)
    mid = r.astype(BF16)
    lo = (r - mid.astype(F32)).astype(BF16)
    return hi, mid, lo


def _dot3(x, onehot, dims):
    return sum(lax.dot_general(t, onehot, dims, preferred_element_type=F32) for t in _split3(x))


_NT = (((1,), (1,)), ((), ()))
_NN = (((1,), (0,)), ((), ()))
_TN = (((0,), (0,)), ((), ()))


def _colmap(x, *, inverse, name):
    R = x.shape[0] if inverse else x.shape[1]
    tr = 256
    per = W_IN_SLOT // LANE
    n_out = N_DEV * per if inverse else Z_W // LANE
    segs = [(p, q, n) for q, p, n in W_IN_SEGS] if inverse else list(W_IN_SEGS)

    def body(x_ref, o_ref):
        ia = lax.broadcasted_iota(jnp.int32, (LANE, LANE), 0)
        ib = lax.broadcasted_iota(jnp.int32, (LANE, LANE), 1)

        def src_block(i):
            if inverse:
                return x_ref[:, i * LANE:(i + 1) * LANE]
            return x_ref[i // per, :, (i % per) * LANE:(i % per + 1) * LANE]

        for jb in range(n_out):
            acc = None
            for s0, d0, n in segs:
                lo, hi = max(d0, jb * LANE), min(d0 + n, (jb + 1) * LANE)
                if lo >= hi:
                    continue
                delta = d0 - s0
                for i in range((lo - delta) // LANE, (hi - delta - 1) // LANE + 1):
                    shift = jb * LANE - i * LANE - delta
                    sel = ((ia - ib == shift) & (ib >= lo - jb * LANE) & (ib < hi - jb * LANE)).astype(BF16)
                    blk = src_block(i)
                    part = _dot3(blk, sel, _NN) if inverse else lax.dot_general(blk, sel, _NN, preferred_element_type=F32)
                    acc = part if acc is None else acc + part
            if acc is None:
                acc = jnp.zeros((tr, LANE), F32)
            if inverse:
                o_ref[jb // per, :, (jb % per) * LANE:(jb % per + 1) * LANE] = acc
            else:
                o_ref[:, jb * LANE:(jb + 1) * LANE] = acc.astype(BF16)

    slot_spec = pl.BlockSpec((N_DEV, tr, W_IN_SLOT), lambda i: (0, i, 0))
    flat_spec = pl.BlockSpec((tr, Z_W), lambda i: (i, 0))
    if inverse:
        return _pcall(body, name=name, grid=(R // tr,), in_specs=[flat_spec], out_specs=slot_spec,
                      out_shape=jax.ShapeDtypeStruct((N_DEV, R, W_IN_SLOT), F32), compiler_params=_params(("parallel",)))(x)
    return _pcall(body, name=name, grid=(R // tr,), in_specs=[slot_spec], out_specs=flat_spec,
                  out_shape=jax.ShapeDtypeStruct((R, Z_W), BF16), compiler_params=_params(("parallel",)))(x)


def _log_sigmoid(x):
    return jnp.minimum(x, 0.0) - jnp.log(1.0 + jnp.exp(-jnp.abs(x)))


def _cum_heads(f, dcum=None, *, name):
    H, S = f.shape
    tn = min(512, S)
    rev = dcum is not None

    def body(*refs):
        j = pl.program_id(0)
        s_idx = lax.broadcasted_iota(jnp.int32, (S, tn), 0)
        t_idx = lax.broadcasted_iota(jnp.int32, (S, tn), 1) + j * tn
        if not rev:
            f_ref, o_ref = refs
            tri = (s_idx <= t_idx).astype(BF16)
            o_ref[...] = _dot3(_log_sigmoid(f_ref[...]), tri, _NN)
        else:
            fj_ref, d_ref, o_ref = refs
            tri = (s_idx >= t_idx).astype(BF16)
            o_ref[...] = _dot3(d_ref[...], tri, _NN) * jax.nn.sigmoid(-fj_ref[...])

    full = pl.BlockSpec((H, S), lambda j: (0, 0))
    blk = pl.BlockSpec((H, tn), lambda j: (0, j))
    ins, specs = ([f], [full]) if not rev else ([f, dcum], [blk, full])
    return _pcall(body, name=name, grid=(S // tn,), in_specs=specs, out_specs=blk,
                  out_shape=jax.ShapeDtypeStruct((H, S), F32), compiler_params=_params(("arbitrary",)))(*ins)


def _rel_onehot(qi, transposed):
    shape = (N_REL, CA_BAND) if transposed else (CA_BAND, N_REL)
    kk = lax.broadcasted_iota(jnp.int32, shape, 1 if transposed else 0)
    rr = lax.broadcasted_iota(jnp.int32, shape, 0 if transposed else 1)
    idx = jnp.clip(CA_PAD + qi - kk, REL_MIN, REL_MAX) - REL_MIN
    return (idx == rr).astype(BF16)


def _relbias_expand(rb):
    rows = 8

    def body(rb_ref, o_ref):
        for r in range(rows):
            o_ref[r] = _dot3(rb_ref[...], _rel_onehot(pl.program_id(0) * rows + r, True), _NN)

    return _pcall(body, name="relbias_expand", grid=(CHUNK // rows,),
                  in_specs=[pl.BlockSpec((CA_HEADS, N_REL), lambda q: (0, 0))],
                  out_specs=pl.BlockSpec((rows, CA_HEADS, CA_BAND), lambda q: (q, 0, 0)),
                  out_shape=jax.ShapeDtypeStruct((CHUNK, CA_HEADS, CA_BAND), F32),
                  compiler_params=_params(("arbitrary",)))(rb)


def _relbias_reduce(db):
    rows = 8

    def body(db_ref, o_ref):
        q = pl.program_id(0)
        part = sum(_dot3(db_ref[r], _rel_onehot(q * rows + r, False), _NN) for r in range(rows))

        @pl.when(q == 0)
        def _():
            o_ref[...] = part

        @pl.when(q > 0)
        def _():
            o_ref[...] += part

    return _pcall(body, name="relbias_reduce", grid=(CHUNK // rows,),
                  in_specs=[pl.BlockSpec((rows, CA_HEADS, CA_BAND), lambda q: (q, 0, 0))],
                  out_specs=pl.BlockSpec((CA_HEADS, N_REL), lambda q: (0, 0)),
                  out_shape=jax.ShapeDtypeStruct((CA_HEADS, N_REL), F32),
                  compiler_params=_params(("arbitrary",)))(db)


def _attn_cfg(mode, S):
    if mode == "fox":
        return min(256, S), FOX_HEAD_DIM ** -0.5
    if mode == "chunk":
        return CHUNK, CA_HEAD_DIM ** -0.5
    return min(512, S), XA_HEAD_DIM ** -0.5


def _visible(mode, i, tq, nk):
    if mode == "fox":
        row = lax.broadcasted_iota(jnp.int32, (tq, nk), 0) + i * tq
        return row >= lax.broadcasted_iota(jnp.int32, (tq, nk), 1)
    if mode == "chunk":
        return lax.broadcasted_iota(jnp.int32, (tq, nk), 1) + i * CHUNK >= CA_PAD
    return None


def _scores(mode, qs, kb, extra, visible):
    s = lax.dot_general(qs, kb, _NT, preferred_element_type=F32)
    if mode == "fox":
        cq, ck = extra
        s = jnp.where(visible, s + cq - ck, NEG_INF)
    elif mode == "chunk":
        (bias,) = extra
        s = jnp.where(visible, s + bias, NEG_INF)
    return s


class _AttnPlan:
    def __init__(self, mode, q, k, pp):
        self.mode, self.pp = mode, pp
        self.D, self.hpb, self.bw = (XA_HEAD_DIM, 1, XA_HEAD_DIM) if mode == "xa" else (64, 2, LANE)
        self.S, self.Sk = q[0].shape[0], k[0].shape[0]
        self.H = (XA_HEADS if mode == "xa" else FOX_HEADS)
        self.W = self.H * self.D
        self.gw = self.bw * pp
        self.hpg = self.hpb * pp
        self.tq, self.scale = _attn_cfg(mode, self.S)
        self.grid = (self.W // self.gw, self.S // self.tq)

    def rows(self, win):
        off = win[1] // self.gw
        return pl.BlockSpec((self.tq, self.gw), lambda g, i: (i, off + g))

    def cols(self, win):
        off = win[1] // self.gw
        return pl.BlockSpec((win[0].shape[0], self.gw), lambda g, i: (0, off + g))

    def extras(self):
        if self.mode == "fox":
            return [pl.BlockSpec((self.hpg, self.tq, 1), lambda g, i: (g, i, 0)),
                    pl.BlockSpec((self.hpg, 1, self.Sk), lambda g, i: (g, 0, 0))]
        if self.mode == "chunk":
            return [pl.BlockSpec((self.hpg, CHUNK, CA_BAND), lambda g, i: (g, 0, 0))]
        return []

    def per_row(self):
        return pl.BlockSpec((self.hpg, self.tq, 1), lambda g, i: (g, i, 0))

    def lanes(self, p):
        return slice(p * self.bw, (p + 1) * self.bw)

    def keys(self, i, ref, p, nk):
        if self.mode == "chunk":
            return ref[pl.ds(pl.multiple_of(i * CHUNK, CHUNK), CA_BAND), self.lanes(p)]
        return ref[0:nk, self.lanes(p)]

    def key_ranges(self, i, run):
        nq = self.grid[1]
        if self.mode != "fox" or nq % 4:
            return run(CA_BAND if self.mode == "chunk" else self.Sk)
        for part in range(4):
            lo, hi = part * nq // 4, (part + 1) * nq // 4
            pl.when((i >= lo) & (i < hi))(functools.partial(run, hi * self.tq))

    def head(self, x, hh):
        if self.hpb == 1:
            return x
        lane = lax.broadcasted_iota(jnp.int32, x.shape, 1)
        return jnp.where(lane // self.D == hh, x, jnp.zeros_like(x))


def _attn_fwd(mode, q, k, v, extra=(), *, name):
    pl_ = _AttnPlan(mode, q, k, pp=4)
    n_ex = len(extra)
    out = (None, 0)

    def body(*refs):
        q_ref, k_ref, v_ref = refs[:3]
        o_ref, lse_ref = refs[3 + n_ex:]
        i = pl.program_id(1)

        def run(nk):
            visible = _visible(mode, i, pl_.tq, nk)
            for p in range(pl_.pp):
                qs = q_ref[:, pl_.lanes(p)].astype(BF16) * pl_.scale
                kp = pl_.keys(i, k_ref, p, nk).astype(BF16)
                vp = pl_.keys(i, v_ref, p, nk).astype(BF16)
                acc = None
                for hh in range(pl_.hpb):
                    h = p * pl_.hpb + hh
                    ex = [r[h] for r in refs[3:3 + n_ex]]
                    if mode == "fox":
                        ex[1] = ex[1][:, 0:nk]
                    s = _scores(mode, pl_.head(qs, hh), kp, ex, visible)
                    m = jnp.max(s, axis=1, keepdims=True)
                    e = jnp.exp(s - m)
                    l = jnp.sum(e, axis=1, keepdims=True)
                    part = lax.dot_general((e * (1.0 / l)).astype(BF16), pl_.head(vp, hh), _NN,
                                           preferred_element_type=F32)
                    acc = part if acc is None else acc + part
                    lse_ref[h] = m + jnp.log(l)
                o_ref[:, pl_.lanes(p)] = acc

        pl_.key_ranges(i, run)

    return _pcall(
        body, name=name, grid=pl_.grid, in_specs=[pl_.rows(q), pl_.cols(k), pl_.cols(v)] + pl_.extras(),
        out_specs=[pl_.rows(out), pl_.per_row()],
        out_shape=[jax.ShapeDtypeStruct((pl_.S, pl_.W), F32), jax.ShapeDtypeStruct((pl_.H, pl_.S, 1), F32)],
        compiler_params=_params(("parallel", "arbitrary")),
    )(q[0], k[0], v[0], *extra)


def _attn_bwd(mode, q, k, v, lse, do, extra=(), *, name):
    pl_ = _AttnPlan(mode, q, k, pp=2 if mode == "fox" else 4)
    H, S, Sk, W = pl_.H, pl_.S, pl_.Sk, pl_.W
    n_ex = len(extra)
    out = (None, 0)
    kv_out = pl.BlockSpec((Sk, pl_.gw), lambda g, i: (0, g))
    ex_specs = pl_.extras()
    out_specs = [pl_.rows(out), kv_out, kv_out]
    out_shape = [jax.ShapeDtypeStruct((S, W), F32), jax.ShapeDtypeStruct((Sk, W), F32),
                 jax.ShapeDtypeStruct((Sk, W), F32)]
    if mode == "fox":
        out_specs += [pl_.per_row(), ex_specs[1]]
        out_shape += [jax.ShapeDtypeStruct((H, S, 1), F32), jax.ShapeDtypeStruct((H, 1, Sk), F32)]
    elif mode == "chunk":
        out_specs += [ex_specs[0]]
        out_shape += [jax.ShapeDtypeStruct((H, CHUNK, CA_BAND), F32)]

    def body(*refs):
        q_ref, k_ref, v_ref, lse_ref, do_ref = refs[:5]
        dq_ref, dk_ref, dv_ref = refs[5 + n_ex:8 + n_ex]
        rest = refs[8 + n_ex:]
        i = pl.program_id(1)

        @pl.when(i == 0)
        def _():
            dk_ref[...] = jnp.zeros_like(dk_ref)
            dv_ref[...] = jnp.zeros_like(dv_ref)
            if mode == "fox":
                rest[1][...] = jnp.zeros_like(rest[1])
            elif mode == "chunk":
                rest[0][...] = jnp.zeros_like(rest[0])

        def run(nk):
            visible = _visible(mode, i, pl_.tq, nk)
            for p in range(pl_.pp):
                lanes = pl_.lanes(p)
                qs = q_ref[:, lanes].astype(BF16) * pl_.scale
                kp = pl_.keys(i, k_ref, p, nk).astype(BF16)
                vp = pl_.keys(i, v_ref, p, nk).astype(BF16)
                dop = do_ref[:, lanes].astype(BF16)
                dq = dk_part = dv_part = None
                for hh in range(pl_.hpb):
                    h = p * pl_.hpb + hh
                    ex = [r[h] for r in refs[5:5 + n_ex]]
                    if mode == "fox":
                        ex[1] = ex[1][:, 0:nk]
                    qh, doh = pl_.head(qs, hh), pl_.head(dop, hh)
                    s = _scores(mode, qh, kp, ex, visible)
                    pr = jnp.exp(s - lse_ref[h])
                    dp = lax.dot_general(doh, vp, _NT, preferred_element_type=F32)
                    ds = pr * (dp - jnp.sum(dp * pr, axis=1, keepdims=True))
                    dsb = ds.astype(BF16)
                    parts = (lax.dot_general(dsb, pl_.head(kp, hh), _NN, preferred_element_type=F32) * pl_.scale,
                             lax.dot_general(dsb, qh, _TN, preferred_element_type=F32),
                             lax.dot_general(pr.astype(BF16), doh, _TN, preferred_element_type=F32))
                    dq, dk_part, dv_part = parts if dq is None else (dq + parts[0], dk_part + parts[1],
                                                                     dv_part + parts[2])
                    if mode == "chunk":
                        rest[0][h] += ds
                    if mode == "fox":
                        rest[0][h] = jnp.sum(ds, axis=1, keepdims=True)
                        rest[1][h, :, 0:nk] += -jnp.sum(ds, axis=0, keepdims=True)
                dq_ref[:, lanes] = dq
                if mode == "chunk":
                    win = pl.ds(pl.multiple_of(i * CHUNK, CHUNK), CA_BAND)
                    dk_ref[win, lanes] += dk_part
                    dv_ref[win, lanes] += dv_part
                else:
                    dk_ref[0:nk, lanes] += dk_part
                    dv_ref[0:nk, lanes] += dv_part

        pl_.key_ranges(i, run)

    return _pcall(
        body, name=name, grid=pl_.grid,
        in_specs=[pl_.rows(q), pl_.cols(k), pl_.cols(v), pl_.per_row(), pl_.rows(do)] + ex_specs,
        out_specs=out_specs, out_shape=out_shape,
        compiler_params=_params(("parallel", "arbitrary")),
    )(q[0], k[0], v[0], lse, do[0], *extra)


def _scan(x, a, h=None, *, name):
    S = x.shape[0]
    CB = SCAN_CB
    rev = h is not None
    n_grp = S // 8

    def body(*refs):
        if rev:
            x_ref, a_ref, h_ref, o_ref, da_ref = refs
        else:
            x_ref, a_ref, o_ref = refs
        ar = a_ref[:, :CB]
        ai = -a_ref[:, CB:] if rev else a_ref[:, CB:]
        zero = jnp.zeros((1, CB), F32)

        def group(g, carry):
            base = pl.multiple_of((n_grp - 1 - g) * 8 if rev else g * 8, 8)
            for j in (range(7, -1, -1) if rev else range(8)):
                t = base + j
                if rev:
                    hr, hi, dar, dai = carry
                else:
                    hr, hi = carry
                xr = x_ref[pl.ds(t, 1), :CB]
                xi = x_ref[pl.ds(t, 1), CB:]
                hr, hi = ar * hr - ai * hi + xr, ar * hi + ai * hr + xi
                o_ref[pl.ds(t, 1), :CB] = hr
                o_ref[pl.ds(t, 1), CB:] = hi
                if rev:
                    tp = jnp.maximum(t - 1, 0)
                    live = (t > 0).astype(F32)
                    pr = h_ref[pl.ds(tp, 1), :CB] * live
                    pi = h_ref[pl.ds(tp, 1), CB:] * live
                    carry = (hr, hi, dar + hr * pr + hi * pi, dai + hi * pr - hr * pi)
                else:
                    carry = (hr, hi)
            return carry

        if rev:
            _, _, dar, dai = lax.fori_loop(0, n_grp, group, (zero, zero, zero, zero))
            da_ref[:, :CB] = dar
            da_ref[:, CB:] = dai
        else:
            lax.fori_loop(0, n_grp, group, (zero, zero))

    big = pl.BlockSpec((S, 2 * CB), lambda c: (0, c))
    vec = pl.BlockSpec((1, 2 * CB), lambda c: (0, c))
    n_blk = x.shape[1] // (2 * CB)
    if rev:
        return _pcall(body, name=name, grid=(n_blk,), in_specs=[big, vec, big], out_specs=[big, vec],
                      out_shape=[jax.ShapeDtypeStruct(x.shape, F32), jax.ShapeDtypeStruct(a.shape, F32)],
                      compiler_params=_params(("parallel",)))(x, a, h)
    return _pcall(body, name=name, grid=(n_blk,), in_specs=[big, vec], out_specs=big,
                  out_shape=jax.ShapeDtypeStruct(x.shape, F32), compiler_params=_params(("parallel",)))(x, a)


def _ssm_prep1(lr_, li, ldt):
    lr = jnp.minimum(lr_, -1e-4)
    dt = jnp.exp(ldt)
    mag = jnp.exp(lr * dt)
    ar = mag * jnp.cos(li * dt)
    ai = mag * jnp.sin(li * dt)
    den = lr * lr + li * li
    gr = ((ar - 1.0) * lr + ai * li) / den
    gi = (ai * lr - (ar - 1.0) * li) / den
    return ar, ai, gr, gi


def _ssm_prep2(gr, gi, br, bi):
    return gr * br - gi * bi, gr * bi + gi * br


def _vjp_of(fn, n_in):
    def bwd(*args):
        cts = args[n_in:]
        return jax.vjp(fn, *args[:n_in])[1](cts[0] if len(cts) == 1 else tuple(cts))
    return bwd


def _to_blocked(r, i):
    lead = r.shape[:-1]
    t = jnp.stack([r.reshape(lead + (N_SSM_CH // SCAN_CB, SCAN_CB)), i.reshape(lead + (N_SSM_CH // SCAN_CB, SCAN_CB))],
                  axis=-2)
    return t.reshape(lead + (2 * N_SSM_CH,))


def _from_blocked(m):
    lead = m.shape[:-1]
    t = m.reshape(lead + (N_SSM_CH // SCAN_CB, 2, SCAN_CB))
    return t[..., 0, :].reshape(lead + (N_SSM_CH,)), t[..., 1, :].reshape(lead + (N_SSM_CH,))


_GPB = SSM_GROUPS // SSM_BD
_GPS = SCAN_CB // SSM_STATE


def _bd_eye():
    return jnp.eye(_GPB, dtype=F32).reshape(_GPB, _GPB // _GPS, _GPS)


def _blockdiag(r, i):
    v = jnp.stack([r, i]).reshape(2, SSM_BD, _GPB, SSM_GROUP, SSM_STATE)
    return jnp.einsum("qjgcp,gsh->jgcsqhp", v, _bd_eye()).reshape(SSM_BD, SSM_BD_IN, SSM_BD_ST)


def _blockdiag_inv(m):
    d = m.reshape(SSM_BD, _GPB, SSM_GROUP, _GPB // _GPS, 2, _GPS, SSM_STATE)
    v = jnp.einsum("jgcsqhp,gsh->qjgcp", d, _bd_eye()).reshape(2, SSM_GROUPS, SSM_GROUP, SSM_STATE)
    return v[0], v[1]


def _shift_rows(x, n):
    S = x.shape[0]
    row = lax.broadcasted_iota(jnp.int32, x.shape, 0)
    if n > 0:
        return jnp.where(row >= n, pltpu.roll(x, n, 0), 0.0)
    return jnp.where(row < S + n, pltpu.roll(x, S + n, 0), 0.0)


def _bf(x):
    return x.astype(BF16).astype(F32)


def _conv_pre(a, w, b):
    ab, wb = _bf(a), _bf(w)
    return wb[2:3] * ab + wb[1:2] * _shift_rows(ab, 1) + wb[0:1] * _shift_rows(ab, 2) + b


def _ffn_mid(up, conv_w, conv_b, dh=None, *, name):
    S = up.shape[0]
    tn = LANE
    nb = D_FF_P // tn
    rev = dh is not None

    def body(*refs):
        if not rev:
            a_ref, g_ref, w_ref, b_ref, o_ref = refs
            o_ref[...] = jax.nn.gelu(_conv_pre(a_ref[...], w_ref[...], b_ref[...])) * g_ref[...]
            return
        a_ref, g_ref, w_ref, b_ref, dh_ref, dup_a_ref, dup_g_ref, dw_ref, db_ref = refs
        a, w, dh_ = a_ref[...], w_ref[...], dh_ref[...]
        pre = _conv_pre(a, w, b_ref[...])
        gl, gelu_vjp = jax.vjp(jax.nn.gelu, pre)
        dup_g_ref[...] = dh_ * gl
        (dpre,) = gelu_vjp(dh_ * g_ref[...])
        db_ref[...] = jnp.sum(dpre, axis=0, keepdims=True)
        dpb, ab, wb = _bf(dpre), _bf(a), _bf(w)
        dup_a_ref[...] = wb[2:3] * dpb + wb[1:2] * _shift_rows(dpb, -1) + wb[0:1] * _shift_rows(dpb, -2)
        dw_ref[2:3, :] = jnp.sum(dpb * ab, axis=0, keepdims=True)
        dw_ref[1:2, :] = jnp.sum(dpb * _shift_rows(ab, 1), axis=0, keepdims=True)
        dw_ref[0:1, :] = jnp.sum(dpb * _shift_rows(ab, 2), axis=0, keepdims=True)

    a_spec = pl.BlockSpec((S, tn), lambda j: (0, j))
    g_spec = pl.BlockSpec((S, tn), lambda j: (0, j + nb))
    w_spec = pl.BlockSpec((3, tn), lambda j: (0, j))
    b_spec = pl.BlockSpec((1, tn), lambda j: (0, j))
    if not rev:
        return _pcall(body, name=name, grid=(nb,), in_specs=[a_spec, g_spec, w_spec, b_spec], out_specs=a_spec,
                      out_shape=jax.ShapeDtypeStruct((S, D_FF_P), F32), compiler_params=_params(("parallel",)))(
                          up, up, conv_w, conv_b)
    return _pcall(body, name=name, grid=(nb,), in_specs=[a_spec, g_spec, w_spec, b_spec, a_spec],
                  out_specs=[a_spec, a_spec, w_spec, b_spec],
                  out_shape=[jax.ShapeDtypeStruct((S, D_FF_P), F32), jax.ShapeDtypeStruct((S, D_FF_P), F32),
                             jax.ShapeDtypeStruct((3, D_FF_P), F32), jax.ShapeDtypeStruct((1, D_FF_P), F32)],
                  compiler_params=_params(("parallel",)))(up, up, conv_w, conv_b, dh)


def _ff_pad(t):
    lead = t.shape[:-1]
    t = t.reshape(lead + (N_DEV, FF_HALF))
    return jnp.pad(t, [(0, 0)] * len(lead) + [(0, 0), (0, FF_HALF_P - FF_HALF)]).reshape(lead + (D_FF_P,))


def _ff_unpad(t):
    lead = t.shape[:-1]
    return t.reshape(lead + (N_DEV, FF_HALF_P))[..., :FF_HALF].reshape(lead + (D_FF,))


def _ln_fwd(x, h, g, b):
    r = DN_ALPHA * x + h
    mu = jnp.mean(r, axis=-1, keepdims=True)
    var = jnp.mean(jnp.square(r - mu), axis=-1, keepdims=True)
    return r, (r - mu) * lax.rsqrt(var + LN_EPS) * g + b


def _ln_bwd(r, dy, g):
    mu = jnp.mean(r, axis=-1, keepdims=True)
    var = jnp.mean(jnp.square(r - mu), axis=-1, keepdims=True)
    xhat = (r - mu) * lax.rsqrt(var + LN_EPS)
    dxh = dy * g
    dr = lax.rsqrt(var + LN_EPS) * (dxh - jnp.mean(dxh, axis=-1, keepdims=True)
                                    - xhat * jnp.mean(dxh * xhat, axis=-1, keepdims=True))
    return dr, jnp.sum(dy * xhat, axis=0, keepdims=True), jnp.sum(dy, axis=0, keepdims=True)


def _merge(gf, gs, gc, ya, yb2, yc):
    yb = yb2[:, :D_MODEL] * jax.nn.sigmoid(yb2[:, D_MODEL:])
    return jax.nn.sigmoid(gf) * ya + jax.nn.sigmoid(gs) * yb + jax.nn.sigmoid(gc) * yc


def _s5_tail(hc, su, d):
    return jax.nn.gelu(hc + d * su)


def _s5_tail_bwd(hc, su, dgel, d):
    _, vjp = jax.vjp(jax.nn.gelu, hc + d * su)
    (dy,) = vjp(dgel)
    return dy, d * dy, jnp.sum(dy * su, axis=0, keepdims=True)


def _loss_rows(y, tgt):
    err = y - tgt
    return err * (1.0 / D_MODEL), jnp.sum(0.5 * jnp.square(err), axis=0, keepdims=True) * (1.0 / D_MODEL)


def _peer(k):
    x, y, c = lax.axis_index("x"), lax.axis_index("y"), lax.axis_index("c")
    return (x ^ ((k >> 2) & 1), y ^ ((k >> 1) & 1), c ^ (k & 1))


def _my_slot():
    return 4 * lax.axis_index("x") + 2 * lax.axis_index("y") + lax.axis_index("c")


def _peer_slot(k):
    px, py, pc = _peer(k)
    return 4 * px + 2 * py + pc


N_CHIP = N_DEV // 2
OTHER_CHIPS = (2, 4, 6)


def _chip_of(dev):
    return 2 * dev[0] + dev[1]


def _remote(src, dst, send, recv, dev):
    return pltpu.make_async_remote_copy(src_ref=src, dst_ref=dst, send_sem=send, recv_sem=recv, device_id=dev,
                                        device_id_type=pl.DeviceIdType.MESH)


def _all_gather(shards, *, name, host=None, then=None):
    return _exchange(shards, *_all_gather_parts(shards), name=name, host=host, then=then)


def _exchange(ins, out_shapes, sem_shapes, start, finish, *, name, host=None, then=None):
    if host is not None:
        rider = _Rider(host, ins, out_shapes, sem_shapes, start, finish, then)
        _RIDERS.append(rider)
        return rider
    n = len(ins)

    def body(*refs):
        start(refs[:n], refs[n:2 * n], refs[2 * n:])
        finish(refs[:n], refs[n:2 * n], refs[2 * n:])

    hbm = pl.BlockSpec(memory_space=pl.ANY)
    return _pcall(body, name=name, in_specs=[hbm] * n, out_specs=[hbm] * n, out_shape=list(out_shapes),
                  scratch_shapes=list(sem_shapes))(*ins)


def _all_gather_parts(shards):
    n = len(shards)

    def first_copies(ins, outs, sems):
        send, recv, _ = sems
        return [_remote(ins[t], outs[t].at[_my_slot()], send.at[t, k - 1], recv.at[t, k - 1], _peer(k))
                for k in (1,) + OTHER_CHIPS for t in range(n)]

    def local_copies(ins, outs, sems):
        return [pltpu.make_async_copy(ins[t], outs[t].at[_my_slot()], sems[2].at[t]) for t in range(n)]

    def start(ins, outs, sems):
        for cp in local_copies(ins, outs, sems) + first_copies(ins, outs, sems):
            cp.start()

    def finish(ins, outs, sems):
        send, recv, _ = sems
        sibling = _peer(1)
        passed = []
        for k in OTHER_CHIPS:
            for t in range(n):
                slot = outs[t].at[_peer_slot(k)]
                _remote(ins[t], slot, send.at[t, k - 1], recv.at[t, k - 1], _peer(k)).wait_recv()
                cp = _remote(slot, slot, send.at[t, k], recv.at[t, k], sibling)
                cp.start()
                passed.append(cp)
        for t in range(n):
            _remote(ins[t], outs[t].at[_peer_slot(1)], send.at[t, 0], recv.at[t, 0], sibling).wait_recv()
            for k in OTHER_CHIPS:
                _remote(ins[t], outs[t].at[_peer_slot(k + 1)], send.at[t, k], recv.at[t, k], sibling).wait_recv()
        for cp in first_copies(ins, outs, sems) + passed:
            cp.wait_send()
        for lc in local_copies(ins, outs, sems):
            lc.wait()

    out_shapes = [jax.ShapeDtypeStruct((N_DEV,) + s.shape, s.dtype) for s in shards]
    sem_shapes = [pltpu.SemaphoreType.DMA((n, N_DEV - 1)), pltpu.SemaphoreType.DMA((n, N_DEV - 1)),
                  pltpu.SemaphoreType.DMA((n,))]
    return out_shapes, sem_shapes, start, finish


def _sibling_swap(grads, *, name, host=None, then=None):
    n = len(grads)

    def copies(ins, outs, sems):
        c = lax.axis_index("c")
        return [_remote(ins[t].at[:, 1 - c], outs[t], sems[0].at[t], sems[1].at[t], _peer(1)) for t in range(n)]

    def start(ins, outs, sems):
        for cp in copies(ins, outs, sems):
            cp.start()

    def finish(ins, outs, sems):
        for cp in copies(ins, outs, sems):
            cp.wait()

    out_shapes = [jax.ShapeDtypeStruct((N_CHIP,) + g.shape[2:], g.dtype) for g in grads]
    sem_shapes = [pltpu.SemaphoreType.DMA((n,)), pltpu.SemaphoreType.DMA((n,))]
    return _exchange(grads, out_shapes, sem_shapes, start, finish, name=name, host=host, then=then)


def _pair_add(g, p, out_dtype, *, name):
    _, _, R, C = g.shape
    tr = _pick(R, (512, 256, 128, 64, 32, 16, 8))

    def body(c_ref, g_ref, p_ref, o_ref):
        o_ref[...] = (g_ref[...] + p_ref[...]).astype(out_dtype)

    grid_spec = pltpu.PrefetchScalarGridSpec(
        num_scalar_prefetch=1, grid=(N_CHIP, R // tr),
        in_specs=[pl.BlockSpec((None, None, tr, C), lambda j, i, c_ref: (j, c_ref[0], i, 0)),
                  pl.BlockSpec((None, tr, C), lambda j, i, c_ref: (j, i, 0))],
        out_specs=pl.BlockSpec((None, tr, C), lambda j, i, c_ref: (j, i, 0)))
    core = lax.axis_index("c").astype(jnp.int32).reshape(1)
    return _pcall(body, name=name, grid_spec=grid_spec, out_shape=jax.ShapeDtypeStruct(p.shape, out_dtype),
                  compiler_params=_params(("parallel", "parallel")))(core, g, p)


def _chip_exchange(sums, *, name, host=None):
    n = len(sums)

    def copies(ins, outs, sems, dst_is_mine):
        send, recv, _ = sems
        mine = 2 * lax.axis_index("x") + lax.axis_index("y")
        out = []
        for k in OTHER_CHIPS:
            theirs = _chip_of(_peer(k))
            for t in range(n):
                out.append(_remote(ins[t].at[theirs], outs[t].at[mine if dst_is_mine else theirs],
                                   send.at[t, k // 2 - 1], recv.at[t, k // 2 - 1], _peer(k)))
        return out

    def local_copies(ins, outs, sems):
        mine = 2 * lax.axis_index("x") + lax.axis_index("y")
        return [pltpu.make_async_copy(ins[t].at[mine], outs[t].at[mine], sems[2].at[t]) for t in range(n)]

    def start(ins, outs, sems):
        for cp in local_copies(ins, outs, sems) + copies(ins, outs, sems, True):
            cp.start()

    def finish(ins, outs, sems):
        for cp in copies(ins, outs, sems, False) + local_copies(ins, outs, sems):
            cp.wait()

    out_shapes = [jax.ShapeDtypeStruct(s.shape, s.dtype) for s in sums]
    sem_shapes = [pltpu.SemaphoreType.DMA((n, N_CHIP - 1)), pltpu.SemaphoreType.DMA((n, N_CHIP - 1)),
                  pltpu.SemaphoreType.DMA((n,))]
    return _exchange(sums, out_shapes, sem_shapes, start, finish, name=name, host=host)


class _GradReduce:
    def __init__(self, grads, wire_dtypes, hosts=None):
        pairs = [g.reshape((N_CHIP, 2) + g.shape[1:]) for g in grads]
        self.n, self.riders, self.recv = len(grads), [], None

        def after_swap(partner):
            sums = [_pair_add(g, p, dt, name="grad_pair_add") for g, p, dt in zip(pairs, partner, wire_dtypes)]
            if hosts is None:
                self.recv = _chip_exchange(sums, name="grad_chip_exchange")
            else:
                self.riders = [(idx, _chip_exchange([sums[i] for i in idx], name="grad_chip_exchange", host=h))
                               for h, idx in hosts[1]]

        if hosts is None:
            after_swap(_sibling_swap(pairs, name="grad_sibling_swap"))
        else:
            _sibling_swap(pairs, name="grad_sibling_swap", host=hosts[0], then=after_swap)

    def result(self):
        if self.recv is None:
            self.recv = [None] * self.n
            for idx, rider in self.riders:
                assert rider.results is not None, rider.host
                for i, r in zip(idx, rider.results):
                    self.recv[i] = r
        return self.recv


def _adamw(recv, w, m, v, layer=None, into=None, *, name):
    n_slots, R, C = recv.shape
    tr = _pick(R, (256, 128, 64, 32, 16, 8))

    def body(r_ref, w_ref, m_ref, v_ref, *rest):
        g_ref, d_ref, nm_ref, nv_ref = rest[-4:]
        g = r_ref[0].astype(F32)
        for s in range(1, n_slots):
            g = g + r_ref[s].astype(F32)
        m_new = ADAM_B1 * m_ref[...] + (1.0 - ADAM_B1) * g
        v_new = ADAM_B2 * v_ref[...] + (1.0 - ADAM_B2) * jnp.square(g)
        m_hat = m_new / (1.0 - ADAM_B1 ** ADAM_STEP)
        v_hat = v_new / (1.0 - ADAM_B2 ** ADAM_STEP)
        g_ref[...] = g
        d_ref[...] = -ADAM_LR * (m_hat / (jnp.sqrt(v_hat) + ADAM_EPS) + ADAM_WD * w_ref[...])
        nm_ref[...] = m_new
        nv_ref[...] = v_new

    row = pl.BlockSpec((tr, C), lambda i: (i, 0))
    state = row if layer is None else pl.BlockSpec((None, tr, C), lambda i: (layer, i, 0))
    in_specs = [pl.BlockSpec((n_slots, tr, C), lambda i: (0, i, 0)), state, state, state]
    if into is None:
        return _pcall(body, name=name, grid=(R // tr,), in_specs=in_specs, out_specs=[row] * 4,
                      out_shape=[jax.ShapeDtypeStruct((R, C), F32)] * 4, compiler_params=_params(("parallel",)),
                      )(recv, w, m, v)
    return _pcall(body, name=name, grid=(R // tr,), in_specs=in_specs + [pl.BlockSpec(memory_space=pl.ANY)] * 4,
                  out_specs=[state] * 4, out_shape=[jax.ShapeDtypeStruct((DEPTH, R, C), F32)] * 4,
                  input_output_aliases={4 + j: j for j in range(4)}, compiler_params=_params(("parallel",)),
                  )(recv, w, m, v, *into)


def _flat_pad(parts, rows):
    flat = jnp.concatenate([p.reshape(-1) for p in parts])
    return jnp.pad(flat, (0, rows * LANES - flat.shape[0])).reshape(rows, LANES)


def _small_shard_shape(n):
    return SMALL[n][:-1] + (SMALL[n][-1] // N_DEV,)


def _unpack_small(gathered):
    out, off = {}, 0
    flat = gathered.reshape(N_DEV, -1)
    for n in SMALL:
        r, c = _small_shard_shape(n)
        out[n] = flat[:, off:off + r * c].reshape(N_DEV, r, c).transpose(1, 0, 2).reshape(r, N_DEV * c)
        off += r * c
    return out


def _pack_state(state, prefix):
    flat = jnp.concatenate([state[prefix + n].reshape(DEPTH, -1) for n in (*SMALL, *REPL)], axis=1)
    return jnp.pad(flat, ((0, 0), (0, PACK_ROWS * LANES - flat.shape[1]))).reshape(DEPTH * PACK_ROWS, LANES)


def _pack_small_grads(grads):
    cols = []
    for n in SMALL:
        r, c = _small_shard_shape(n)
        cols.append(grads[n].reshape(r, N_DEV, c).transpose(1, 0, 2).reshape(N_DEV, r * c))
    cols += [jnp.broadcast_to(grads[n].reshape(1, -1), (N_DEV, grads[n].size)) for n in REPL]
    flat = jnp.concatenate(cols, axis=1)
    return jnp.pad(flat, ((0, 0), (0, PACK_ROWS * LANES - flat.shape[1]))).reshape(N_DEV, PACK_ROWS, LANES)


def _unpack_state(packed):
    out, off = {}, 0
    flat = packed.reshape(DEPTH, -1)
    for n, shape in [(n, _small_shard_shape(n)) for n in SMALL] + list(REPL.items()):
        sz = math.prod(shape)
        out[n] = flat[:, off:off + sz].reshape((DEPTH,) + shape)
        off += sz
    return out


def _pad_b_in(b):
    parts, pos = [], 0
    for src, width, dst in Z_PIECES:
        parts += [jnp.zeros((b.shape[0], dst - pos), b.dtype), b[:, src:src + width]]
        pos = dst + width
    return jnp.concatenate(parts + [jnp.zeros((b.shape[0], Z_W - pos), b.dtype)], axis=1)


def _unpad_b_in(bp):
    return jnp.concatenate([bp[:, dst:dst + width] for _, width, dst in Z_PIECES], axis=1)


def _up_shard_pad(t):
    gap = jnp.zeros(t.shape[:-1] + (FF_HALF_P - FF_HALF,), t.dtype)
    return jnp.concatenate([t[..., :FF_HALF], gap, t[..., FF_HALF:], gap], axis=-1)


def _up_shard_unpad(t):
    return jnp.concatenate([t[..., :FF_HALF], t[..., FF_HALF_P:FF_HALF_P + FF_HALF]], axis=-1)


def _pad_shard(n, t):
    lead = [(0, 0)] * (t.ndim - 2)
    if n == "w_in":
        return jnp.pad(t, lead + [(0, 0), (0, W_IN_SLOT - W_IN_SHARD)])
    if n == "ffn_w_up":
        return _up_shard_pad(t)
    if n == "ffn_w_down":
        return jnp.pad(t, lead + [(0, FF_HALF_P - FF_HALF), (0, 0)])
    return t


def _unpad_shard(n, t):
    if n == "w_in":
        return t[..., :W_IN_SHARD]
    if n == "ffn_w_up":
        return _up_shard_unpad(t)
    if n == "ffn_w_down":
        return t[..., :FF_HALF, :]
    return t


GATHER_HOSTS = (("fox_fwd", (0,)), ("chunk_fwd", (8, 9)), ("mm_up", (1, 2, 3, 4, 5, 6, 7, 10)))
FIRST_GATHER_HOSTS = ((None, (0,)), ("mm_in", (1, 2, 3, 4, 5, 6, 7, 10)), ("s5_scan", (8, 9)))
REDUCE_HOSTS = ("mm_up_dw", (("fox_bwd", (0,)), ("s5_scan_bwd", (1, 2, 3, 4, 5, 6, 7, 10)), ("chunk_bwd", (8,)),
                             ("mm_in_dx", (9,))))


def _layer_shards(state, layer):
    shards = [_pad_shard(n, state[n][layer].astype(BF16)) for n in BIG]
    return shards + [_flat_pad([state[n][layer] for n in SMALL], SMALL_ROWS)]


class _Weights:
    def __init__(self, sources):
        self._sources, self._cache = sources, {}

    def _operand(self, i):
        src = self._sources[i]
        if isinstance(src, tuple):
            assert src[0].results is not None, src[0].host
            return src[0].results[src[1]]
        return src

    def __getitem__(self, n):
        if n not in self._cache:
            if n in SMALL:
                self._cache.update(_unpack_small(self._operand(len(BIG))))
            elif n == "w_in_p":
                self._cache[n] = _colmap(self._operand(0), inverse=False, name="w_in_colmap")
            else:
                w = self._operand(list(BIG).index(n))
                self._cache[n] = w.reshape(-1, D_MODEL) if BIG[n][1] == 0 else w
        return self._cache[n]


def _gather_riding(shards, hosts):
    sources = [None] * len(shards)
    for host, idx in hosts:
        group = [shards[i] for i in idx]
        if host is None:
            for i, g in zip(idx, _all_gather(group, name="all_gather_weights")):
                sources[i] = g
        else:
            rider = _all_gather(group, name="all_gather_weights", host=host)
            for pos, i in enumerate(idx):
                sources[i] = (rider, pos)
    return sources


def _ssm_params(p):
    prep1_in = (p["ssm_lambda_re"], p["ssm_lambda_im"], p["ssm_log_dt"][:, None])
    ar, ai, gr, gi = _whole(_ssm_prep1, *prep1_in, name="ssm_prep1")
    to_cn = lambda b: b.transpose(2, 0, 1).reshape(SSM_GROUP, N_SSM_CH)
    prep2_in = (gr.reshape(1, N_SSM_CH), gi.reshape(1, N_SSM_CH), to_cn(p["ssm_b_re"]), to_cn(p["ssm_b_im"]))
    bbr, bbi = _whole(_ssm_prep2, *prep2_in, name="ssm_prep2")
    to_gcp = lambda t: t.reshape(SSM_GROUP, SSM_GROUPS, SSM_STATE).transpose(1, 0, 2)
    bb = _blockdiag(to_gcp(bbr), to_gcp(bbi))
    cct = _blockdiag(p["ssm_c_re"], -p["ssm_c_im"])
    a_vec = _to_blocked(ar.reshape(1, N_SSM_CH), ai.reshape(1, N_SSM_CH))
    return dict(bb=bb, cct=cct, a_vec=a_vec, prep1_in=prep1_in, prep2_in=prep2_in)


def _layer_fwd(x, mem, p, W):
    sp = _ssm_params(p)
    z = _mm(x, W["w_in_p"], bias=p["b_in_p"], name="mm_in")
    f_t = z[:, Z_FF:Z_FF + FOX_HEADS].T
    cum = _cum_heads(f_t, name="fox_cum")
    ya_pre, lse_a = _attn_fwd("fox", (z, Z_FQ), (z, Z_FK), (z, Z_FV), (cum[:, :, None], cum[:, None, :]),
                              name="fox_fwd")
    ya = _mm(ya_pre, W["w_fox_o"], b_slots=True, name="mm_fox_o")
    x_ri = _mm_bd("in", z, sp["bb"], a_off=Z_SU, name="mm_s5_in")
    h_ri = _scan(x_ri, sp["a_vec"], name="s5_scan")
    hc = _mm_bd("out", h_ri, sp["cct"], name="mm_s5_out")
    d_row = p["ssm_d"][None, :]
    (gel,) = _rowwise(lambda a, b, c: (_s5_tail(a, b, c),), [hc, Win(z, 512, Z_SU // 512)], [d_row], name="s5_tail")
    yb2 = _mm(gel, W["w_ssm_glu"], b_slots=True, name="mm_glu")
    bias = _relbias_expand(W["ca_rel_bias"]).transpose(1, 0, 2)
    kv_band = jnp.pad(z[:, Z_CK:Z_CK + 2 * CA_WIDTH].astype(BF16), ((CA_PAD, 0), (0, 0)))
    yc_pre, lse_c = _attn_fwd("chunk", (z, Z_CQ), (kv_band, 0), (kv_band, CA_WIDTH), (bias,), name="chunk_fwd")
    yc = _mm(yc_pre, W["w_ca_o"], b_slots=True, name="mm_ca_o")
    gates = [Win(z, 1024, Z_GF // 1024), Win(z, 1024, Z_GS // 1024), Win(z, 1024, Z_GC // 1024)]
    (merged,) = _rowwise(lambda *a: (_merge(*a),), gates + [ya, yb2, yc], name="merge")
    h1 = _mm(merged, W["w_o"], name="mm_o")
    ln_g, ln_b = W["ln_g"], W["ln_b"]
    r1, x1 = _rowwise(_ln_fwd, [x, h1], [ln_g[0:1], ln_b[0:1]], name="ln_fwd")
    q = _mm(x1, W["xa_wq"], name="mm_xq")
    kv = _mm(mem, W["xa_wkv"], b_slots=True, name="mm_xkv")
    o, lse_x = _attn_fwd("xa", (q, 0), (kv, 0), (kv, D_MODEL), name="xa_fwd")
    h2 = _mm(o, W["xa_wo"], name="mm_xo")
    r2, x2 = _rowwise(_ln_fwd, [x1, h2], [ln_g[1:2], ln_b[1:2]], name="ln_fwd")
    up = _mm(x2, W["ffn_w_up"], b_slots=True, name="mm_up")
    hmid = _ffn_mid(up, _ff_pad(W["ffn_conv_w"]), _ff_pad(p["ffn_conv_b"][None, :]), name="ffn_mid")
    h3 = _mm(hmid, W["ffn_w_down"], name="mm_down")
    r3, x3 = _rowwise(_ln_fwd, [x2, h3], [ln_g[2:3], ln_b[2:3]], name="ln_fwd")
    res = dict(x=x, z=z, cum=cum, lse_a=lse_a, ya_pre=ya_pre, ya=ya, h_ri=h_ri, hc=hc, gel=gel, yb2=yb2, lse_c=lse_c,
               yc_pre=yc_pre, yc=yc, merged=merged, r1=r1, x1=x1, q=q, kv=kv, o=o, lse_x=lse_x, r2=r2, x2=x2, up=up,
               hmid=hmid, r3=r3, bias=bias, sp=sp, kv_band=kv_band, W=W)
    return x3, res


def _layer_bwd(dx3, mem, p, res):
    W = res["W"]
    x, z = res["x"], res["z"]
    sp = res["sp"]
    ln_g = W["ln_g"]
    big, small = {}, {}
    slots = lambda t: t.reshape(N_DEV, -1, D_MODEL)
    dr3, dg2, db2 = _rowwise(_ln_bwd, [res["r3"], dx3], [ln_g[2:3]], n_red=2, name="ln_bwd")
    dhmid = _mm(dr3, W["ffn_w_down"], tb=True, name="mm_down_dx")
    big["ffn_w_down"] = slots(_mm(res["hmid"], dr3, ta=True, name="mm_down_dw"))
    conv_w_p, conv_b_p = _ff_pad(W["ffn_conv_w"]), _ff_pad(p["ffn_conv_b"][None, :])
    dup_a, dup_g, dcw, dcb = _ffn_mid(res["up"], conv_w_p, conv_b_p, dhmid, name="ffn_mid_bwd")
    dup = jnp.concatenate([dup_a, dup_g], axis=1)
    small["ffn_conv_w"], small["ffn_conv_b"] = _ff_unpad(dcw), _ff_unpad(dcb)[0]
    dx2 = _mm(dup, W["ffn_w_up"], tb=True, b_slots=True, add=(dr3, DN_ALPHA), name="mm_up_dx")
    big["ffn_w_up"] = _mm(res["x2"], dup, ta=True, out_slots=2 * FF_HALF_P, name="mm_up_dw")
    dr2, dg1, db1 = _rowwise(_ln_bwd, [res["r2"], dx2], [ln_g[1:2]], n_red=2, name="ln_bwd")
    do = _mm(dr2, W["xa_wo"], tb=True, name="mm_xo_dx")
    big["xa_wo"] = slots(_mm(res["o"], dr2, ta=True, name="mm_xo_dw"))
    kv = res["kv"]
    dq, dk, dv = _attn_bwd("xa", (res["q"], 0), (kv, 0), (kv, D_MODEL), res["lse_x"], (do, 0), name="xa_bwd")
    dkv = jnp.concatenate([dk, dv], axis=1)
    dx1 = _mm(dq, W["xa_wq"], tb=True, add=(dr2, DN_ALPHA), name="mm_xq_dx")
    big["xa_wq"] = slots(_mm(res["x1"], dq, ta=True, name="mm_xq_dw"))
    big["xa_wkv"] = _mm(mem, dkv, ta=True, out_slots=256, name="mm_xkv_dw")
    dr1, dg0, db0 = _rowwise(_ln_bwd, [res["r1"], dx1], [ln_g[0:1]], n_red=2, name="ln_bwd")
    small["ln_g"] = jnp.concatenate([dg0, dg1, dg2], axis=0)
    small["ln_b"] = jnp.concatenate([db0, db1, db2], axis=0)
    dmerged = _mm(dr1, W["w_o"], tb=True, name="mm_o_dx")
    big["w_o"] = slots(_mm(res["merged"], dr1, ta=True, name="mm_o_dw"))
    gates = [Win(z, 1024, Z_GF // 1024), Win(z, 1024, Z_GS // 1024), Win(z, 1024, Z_GC // 1024)]
    dgf, dgs, dgc, dya, dyb2, dyc = _rowwise(_vjp_of(_merge, 6), gates + [res["ya"], res["yb2"], res["yc"], dmerged],
                                             name="merge_bwd")
    dya_pre = _mm(dya, W["w_fox_o"], tb=True, b_slots=True, name="mm_fox_o_dx")
    big["w_fox_o"] = _mm(res["ya_pre"], dya, ta=True, out_slots=128, name="mm_fox_o_dw")
    cum = res["cum"]
    dfq, dfk, dfv, dcq, dck = _attn_bwd("fox", (z, Z_FQ), (z, Z_FK), (z, Z_FV), res["lse_a"], (dya_pre, 0),
                                        (cum[:, :, None], cum[:, None, :]), name="fox_bwd")
    f_t = z[:, Z_FF:Z_FF + FOX_HEADS].T
    dff = _cum_heads(f_t, dcq[:, :, 0] + dck[:, 0, :], name="fox_cum_bwd")
    dgel = _mm(dyb2, W["w_ssm_glu"], tb=True, b_slots=True, name="mm_glu_dx")
    big["w_ssm_glu"] = _mm(res["gel"], dyb2, ta=True, out_slots=256, name="mm_glu_dw")
    d_row = p["ssm_d"][None, :]
    su_win = Win(z, 512, Z_SU // 512)
    dy, dsu1, dd = _rowwise(_s5_tail_bwd, [res["hc"], su_win, dgel], [d_row], n_red=1, name="s5_tail_bwd")
    small["ssm_d"] = dd[0]
    dh_ri = _mm_bd("in", dy, sp["cct"], name="mm_s5_out_dx")
    dcct = _mm_bd("dw", dy, res["h_ri"], name="mm_s5_out_dw")
    dx_ri, da_vec = _scan(dh_ri, sp["a_vec"], res["h_ri"], name="s5_scan_bwd")
    dsu = _mm_bd("out", dx_ri, sp["bb"], add=(dsu1, 1.0), name="mm_s5_in_dx")
    dbb = _mm_bd("dw", z, dx_ri, a_off=Z_SU, name="mm_s5_in_dw")
    dcr, dci = _blockdiag_inv(dcct)
    small["ssm_c_re"], small["ssm_c_im"] = dcr, -dci
    dbbr, dbbi = _blockdiag_inv(dbb)
    to_cn = lambda t: t.transpose(1, 0, 2).reshape(SSM_GROUP, N_SSM_CH)
    dgr, dgi, dbr, dbi = _whole(_vjp_of(_ssm_prep2, 4), *sp["prep2_in"], to_cn(dbbr), to_cn(dbbi), name="ssm_prep2_bwd")
    from_cn = lambda t: t.reshape(SSM_GROUP, SSM_GROUPS, SSM_STATE).transpose(1, 2, 0)
    small["ssm_b_re"], small["ssm_b_im"] = from_cn(dbr), from_cn(dbi)
    dar, dai = _from_blocked(da_vec)
    sq = lambda t: t.reshape(SSM_GROUPS, SSM_STATE)
    dlr, dli, dldt = _whole(_vjp_of(_ssm_prep1, 3), *sp["prep1_in"], sq(dar), sq(dai), sq(dgr), sq(dgi),
                            name="ssm_prep1_bwd")
    small["ssm_lambda_re"], small["ssm_lambda_im"], small["ssm_log_dt"] = dlr, dli, dldt[:, 0]
    dyc_pre = _mm(dyc, W["w_ca_o"], tb=True, b_slots=True, name="mm_ca_o_dx")
    big["w_ca_o"] = _mm(res["yc_pre"], dyc, ta=True, out_slots=128, name="mm_ca_o_dw")
    bias = res["bias"]
    kv_band = res["kv_band"]
    dcq_, dck_band, dcv_band, dbias = _attn_bwd("chunk", (z, Z_CQ), (kv_band, 0), (kv_band, CA_WIDTH), res["lse_c"],
                                                (dyc_pre, 0), (bias,), name="chunk_bwd")
    small["ca_rel_bias"] = _relbias_reduce(dbias.transpose(1, 0, 2))
    dff_p = jnp.pad(dff.T, ((0, 0), (0, 512 - FOX_HEADS)))
    dz = jnp.concatenate([dfq, dfk, dfv, dff_p, dsu, dcq_, dck_band[CA_PAD:], dcv_band[CA_PAD:], dgf, dgs, dgc],
                         axis=1)
    dx = _mm(dz, W["w_in_p"], tb=True, add=(dr1, DN_ALPHA), name="mm_in_dx")
    big["w_in"] = _colmap(_mm(x, dz, ta=True, name="mm_in_dw"), inverse=True, name="w_in_colmap_inv")
    (db_in_p,) = _rowwise(lambda t: (jnp.sum(t, axis=0, keepdims=True),), [dz], n_red=1, name="colsum")
    small["b_in"] = _unpad_b_in(db_in_p)[0]
    return dx, big, small


def kernel(x, mem, w_in, b_in, ssm_lambda_re, ssm_lambda_im, ssm_log_dt, ssm_b_re, ssm_b_im, ssm_c_re, ssm_c_im, ssm_d, ca_rel_bias, w_fox_o, w_ssm_glu, w_ca_o, w_o, xa_wq, xa_wkv, xa_wo, ffn_w_up, ffn_conv_w, ffn_conv_b, ffn_w_down, ln_g, ln_b, loss_target, m_w_in, m_b_in, m_ssm_lambda_re, m_ssm_lambda_im, m_ssm_log_dt, m_ssm_b_re, m_ssm_b_im, m_ssm_c_re, m_ssm_c_im, m_ssm_d, m_ca_rel_bias, m_w_fox_o, m_w_ssm_glu, m_w_ca_o, m_w_o, m_xa_wq, m_xa_wkv, m_xa_wo, m_ffn_w_up, m_ffn_conv_w, m_ffn_conv_b, m_ffn_w_down, m_ln_g, m_ln_b, v_w_in, v_b_in, v_ssm_lambda_re, v_ssm_lambda_im, v_ssm_log_dt, v_ssm_b_re, v_ssm_b_im, v_ssm_c_re, v_ssm_c_im, v_ssm_d, v_ca_rel_bias, v_w_fox_o, v_w_ssm_glu, v_w_ca_o, v_w_o, v_xa_wq, v_xa_wkv, v_xa_wo, v_ffn_w_up, v_ffn_conv_w, v_ffn_conv_b, v_ffn_w_down, v_ln_g, v_ln_b):
    given = dict(locals())
    state = {pre + n: given[pre + n] for n in WEIGHTS for pre in ("", "m_", "v_")}
    mem0 = mem[0]
    b_in_p = _pad_b_in(b_in)
    layer_params = [{**{n: state[n][l] for n in REPL}, "b_in_p": b_in_p[l:l + 1]} for l in range(DEPTH)]

    _RIDERS.clear()
    h, residuals = x[0], []
    sources = _gather_riding(_layer_shards(state, 0), FIRST_GATHER_HOSTS)
    for l in range(DEPTH):
        following = _gather_riding(_layer_shards(state, l + 1), GATHER_HOSTS) if l + 1 < DEPTH else None
        h, res = _layer_fwd(h, mem0, layer_params[l], _Weights(sources))
        residuals.append(res)
        sources = following
    dh, loss_cols = _rowwise(_loss_rows, [h, loss_target[0]], n_red=1, name="loss")
    loss = lax.psum(jnp.sum(loss_cols), ("x", "y", "c"))

    outs = [None] * DEPTH
    big_out = {n: None for n in BIG}
    padded = {pre + n: _pad_shard(n, state[pre + n]) for n in BIG for pre in ("", "m_", "v_")}
    wire = [BF16] * len(BIG) + [F32]
    pending = None
    for l in reversed(range(-1, DEPTH)):
        if l >= 0:
            dh, big, small = _layer_bwd(dh, mem0, layer_params[l], residuals[l])
            reduce = _GradReduce([big[n] for n in BIG] + [_pack_small_grads(small)], wire,
                                 hosts=REDUCE_HOSTS if l > 0 else None)
        done, pending = pending, (l, reduce) if l >= 0 else None
        if done is None:
            continue
        l_done, reduce_done = done
        *recv_big, recv_small = reduce_done.result()
        for n, recv in zip(BIG, recv_big):
            if big_out[n] is None:
                big_out[n] = [lax.empty((DEPTH,) + recv.shape[1:], F32) for _ in range(4)]
            big_out[n] = _adamw(recv, *[padded[pre + n] for pre in ("", "m_", "v_")], l_done, big_out[n],
                                name="adamw_" + n)
        outs[l_done] = recv_small

    assert not _RIDERS, [r.host for r in _RIDERS]
    packed = _adamw(jnp.concatenate(outs, axis=1), *[_pack_state(state, pre) for pre in ("", "m_", "v_")],
                    name="adamw_small")
    small_out = [_unpack_state(t) for t in packed]
    result = lambda n, j: _unpad_shard(n, big_out[n][j]) if n in BIG else small_out[j][n]
    return (loss, dh[None], *[result(n, j) for j in range(4) for n in WEIGHTS])
```

```python
import functools
import math

import jax
import jax.numpy as jnp
from jax import lax
from jax.experimental import pallas as pl
from jax.experimental.pallas import tpu as pltpu

F32, BF16 = jnp.float32, jnp.bfloat16

D_MODEL = 1024
DEPTH = 4
CHUNK = 64
FOX_HEADS, FOX_HEAD_DIM, FOX_WIDTH = 8, 64, 512
SSM_GROUP, SSM_WIDTH, SSM_GROUPS, SSM_STATE = 16, 512, 32, 64
CA_HEADS, CA_HEAD_DIM, CA_WIDTH, CA_LEFT_CHUNKS = 8, 64, 512, 8
CA_BAND = (CA_LEFT_CHUNKS + 1) * CHUNK
CA_PAD = CA_LEFT_CHUNKS * CHUNK
REL_MIN, REL_MAX = -(CHUNK - 1), 4 * CHUNK
N_REL = REL_MAX - REL_MIN + 1
XA_HEADS, XA_HEAD_DIM = 4, 256
D_FF = 2816
DN_ALPHA = (2 * DEPTH) ** 0.25
LN_EPS = 1e-5
NEG_INF = -1e30
ADAM_LR, ADAM_B1, ADAM_B2, ADAM_EPS, ADAM_WD, ADAM_STEP = 0.001, 0.9, 0.999, 1e-08, 0.01, 10

N_DEV = 8
LANE = 128
N_SSM_CH = SSM_GROUPS * SSM_STATE
SCAN_CB = 256
N_IN = 6664
W_IN_SHARD, W_IN_SLOT = N_IN // N_DEV, 896
Z_W = 7168
Z_FQ, Z_FK, Z_FV, Z_FF, Z_SU, Z_CQ, Z_CK, Z_CV, Z_GF, Z_GS, Z_GC = (
    0, 512, 1024, 1536, 2048, 2560, 3072, 3584, 4096, 5120, 6144)
Z_PIECES = ((0, 512, Z_FQ), (512, 512, Z_FK), (1024, 512, Z_FV), (1536, 8, Z_FF), (1544, 512, Z_SU),
            (2056, 512, Z_CQ), (2568, 512, Z_CK), (3080, 512, Z_CV), (3592, 1024, Z_GF), (4616, 1024, Z_GS),
            (5640, 1024, Z_GC))
FF_HALF, FF_HALF_P = D_FF // N_DEV, 384
D_FF_P = N_DEV * FF_HALF_P

VMEM_LIMIT_BYTES = 56 * 1024 * 1024

BIG = {"w_in": ((1024, W_IN_SHARD), 1), "w_fox_o": ((512, 128), 1), "w_ssm_glu": ((512, 256), 1),
       "w_ca_o": ((512, 128), 1), "w_o": ((128, 1024), 0), "xa_wq": ((128, 1024), 0), "xa_wkv": ((1024, 256), 1),
       "xa_wo": ((128, 1024), 0), "ffn_w_up": ((1024, 2 * FF_HALF), 1), "ffn_w_down": ((FF_HALF, 1024), 0)}
SMALL = {"ca_rel_bias": (8, 320), "ffn_conv_w": (3, D_FF), "ln_g": (3, 1024), "ln_b": (3, 1024)}
REPL = {"b_in": (N_IN,), "ssm_lambda_re": (32, 64), "ssm_lambda_im": (32, 64), "ssm_log_dt": (32,),
        "ssm_b_re": (32, 64, 16), "ssm_b_im": (32, 64, 16), "ssm_c_re": (32, 16, 64), "ssm_c_im": (32, 16, 64),
        "ssm_d": (512,), "ffn_conv_b": (D_FF,)}
WEIGHTS = ("w_in", "b_in", "ssm_lambda_re", "ssm_lambda_im", "ssm_log_dt", "ssm_b_re", "ssm_b_im", "ssm_c_re",
           "ssm_c_im", "ssm_d", "ca_rel_bias", "w_fox_o", "w_ssm_glu", "w_ca_o", "w_o", "xa_wq", "xa_wkv", "xa_wo",
           "ffn_w_up", "ffn_conv_w", "ffn_conv_b", "ffn_w_down", "ln_g", "ln_b")
LANES = 1024
PACK_ROWS = 256
SMALL_ROWS = 8


def _w_in_segments():
    segs = []
    for src, width, dst in Z_PIECES:
        n = src
        while n < src + width:
            d = n // W_IN_SHARD
            end = min(src + width, (d + 1) * W_IN_SHARD)
            segs.append((W_IN_SLOT * d + n - W_IN_SHARD * d, dst + n - src, end - n))
            n = end
    return tuple(segs)


W_IN_SEGS = _w_in_segments()


def _pallas(body, **kw):
    return pl.pallas_call(body, **kw)


def _params(sem):
    return pltpu.CompilerParams(dimension_semantics=sem, vmem_limit_bytes=VMEM_LIMIT_BYTES)


class _Rider:
    def __init__(self, host, ins, out_shapes, sem_shapes, start, finish, then=None):
        self.host, self.ins, self.out_shapes, self.sem_shapes = host, list(ins), list(out_shapes), list(sem_shapes)
        self.start, self.finish, self.then, self.results = start, finish, then, None


_RIDERS = []


def _pcall(body, *, name, **kw):
    rider = next((r for r in _RIDERS if r.host == name), None)
    if rider is None:
        return _pallas(body, name=name, **kw)
    _RIDERS.remove(rider)
    grid, in_specs, scratch = kw["grid"], list(kw["in_specs"]), list(kw.get("scratch_shapes", ()))
    single = not isinstance(kw["out_shape"], (list, tuple))
    out_specs = [kw["out_specs"]] if single else list(kw["out_specs"])
    out_shape = [kw["out_shape"]] if single else list(kw["out_shape"])
    n_in, n_out, n_scr = len(in_specs), len(out_specs), len(scratch)
    r_in, r_out = len(rider.ins), len(rider.out_shapes)

    def fused(*refs):
        a, ra = refs[:n_in], refs[n_in:n_in + r_in]
        o, ro = refs[n_in + r_in:n_in + r_in + n_out], refs[n_in + r_in + n_out:n_in + r_in + n_out + r_out]
        scr, sems = refs[n_in + r_in + n_out + r_out:][:n_scr], refs[n_in + r_in + n_out + r_out + n_scr:]
        ids = [pl.program_id(d) for d in range(len(grid))]
        first = functools.reduce(jnp.logical_and, [i == 0 for i in ids])
        last = functools.reduce(jnp.logical_and, [i == g - 1 for i, g in zip(ids, grid)])

        @pl.when(first)
        def _():
            rider.start(ra, ro, sems)

        body(*a, *o, *scr)

        @pl.when(last)
        def _():
            rider.finish(ra, ro, sems)

    hbm = pl.BlockSpec(memory_space=pl.ANY)
    call = _pallas(fused, name=name, grid=grid, in_specs=in_specs + [hbm] * r_in, out_specs=out_specs + [hbm] * r_out,
                   out_shape=out_shape + rider.out_shapes, scratch_shapes=scratch + rider.sem_shapes,
                   compiler_params=_params(("arbitrary",) * len(grid)))

    def run(*operands):
        outs = call(*operands, *rider.ins)
        rider.results = list(outs[n_out:])
        if rider.then is not None:
            rider.then(rider.results)
        return outs[0] if single else list(outs[:n_out])

    return run


def _pick(dim, prefs):
    for p in prefs:
        if dim % p == 0:
            return p
    return dim


def _mm(a, b, *, ta=False, tb=False, bias=None, add=None, a_off=0, a_cols=None, b_slots=False, out_slots=None,
        name, out_dtype=F32):
    a_cols = a_cols if a_cols is not None else a.shape[1]
    M, K = (a_cols, a.shape[0]) if ta else (a.shape[0], a_cols)
    tm = _pick(M, (1024, 512, 256, 128))
    if ta:
        tk = _pick(K, (2048, 1024, 512, 256))
    elif b_slots and tb:
        tk = _pick(b.shape[2], (1024, 768, 512, 256, 128))
    else:
        tk = K if K <= 3072 else _pick(K, (1024, 512, 256, 128))
    nk = K // tk
    if b_slots:
        ns = b.shape[2]
        if tb:
            N = b.shape[1]
            tn = _pick(N, (512, 256, 128))
            per = ns // tk
            b_spec = pl.BlockSpec((None, tn, tk), lambda i, j, k: (k // per, j, k % per))
            b_dim = 1
            assert N_DEV * ns == K
        else:
            N = N_DEV * ns
            tn = _pick(ns, (512, 256, 128))
            per = ns // tn
            b_spec = pl.BlockSpec((None, tk, tn), lambda i, j, k: (j // per, k, j % per))
            b_dim = 0
            assert b.shape[1] == K
    else:
        N = b.shape[0] if tb else b.shape[1]
        assert (b.shape[1] if tb else b.shape[0]) == K, (a.shape, b.shape, ta, tb)
        tn = _pick(N if out_slots is None else out_slots, (512, 256, 128))
        if tb:
            b_spec = pl.BlockSpec((tn, tk), lambda i, j, k: (j, k))
            b_dim = 1
        else:
            b_spec = pl.BlockSpec((tk, tn), lambda i, j, k: (k, j))
            b_dim = 0
    if ta:
        assert a_off % tm == 0
        a_spec = pl.BlockSpec((tk, tm), lambda i, j, k: (k, i + a_off // tm))
        a_dim = 0
    else:
        assert a_off % tk == 0
        a_spec = pl.BlockSpec((tm, tk), lambda i, j, k: (i, k + a_off // tk))
        a_dim = 1
    cache_at = ta and nk == 1
    dims = (((1 if cache_at else a_dim,), (b_dim,)), ((), ()))
    ins, specs = [a, b], [a_spec, b_spec]
    if bias is not None:
        ins.append(bias)
        specs.append(pl.BlockSpec((1, tn), lambda i, j, k: (0, j)))
    add_scale = None
    if add is not None:
        ins.append(add[0])
        add_scale = add[1]
        specs.append(pl.BlockSpec((tm, tn), lambda i, j, k: (i, j)))
    if out_slots is None:
        out_spec = pl.BlockSpec((tm, tn), lambda i, j, k: (i, j))
        out_shape = jax.ShapeDtypeStruct((M, N), out_dtype)
    else:
        assert N == N_DEV * out_slots
        per_o = out_slots // tn
        out_spec = pl.BlockSpec((None, tm, tn), lambda i, j, k: (j // per_o, i, j % per_o))
        out_shape = jax.ShapeDtypeStruct((N_DEV, M, out_slots), out_dtype)

    def body(*refs):
        a_ref, b_ref = refs[0], refs[1]
        pos = 2
        bias_ref = add_ref = None
        if bias is not None:
            bias_ref = refs[pos]
            pos += 1
        if add is not None:
            add_ref = refs[pos]
            pos += 1
        o_ref = refs[pos]
        acc_ref = refs[pos + 1] if nk > 1 else None
        if cache_at:
            at_ref = refs[pos + 1]

            @pl.when(pl.program_id(1) == 0)
            def _():
                step = min(tk, 256)
                for c in range(0, tk, step):
                    at_ref[:, c:c + step] = a_ref[c:c + step, :].T.astype(BF16)

            lhs = at_ref[...]
        else:
            lhs = a_ref[...].astype(BF16)
        part = lax.dot_general(lhs, b_ref[...].astype(BF16), dims, preferred_element_type=F32)

        def finish(acc):
            if bias_ref is not None:
                acc = acc + bias_ref[...]
            if add_ref is not None:
                acc = acc + add_scale * add_ref[...]
            o_ref[...] = acc.astype(out_dtype)

        if nk == 1:
            finish(part)
        else:
            k = pl.program_id(2)

            @pl.when(k == 0)
            def _():
                acc_ref[...] = part

            @pl.when(k > 0)
            def _():
                acc_ref[...] += part

            @pl.when(k == nk - 1)
            def _():
                finish(acc_ref[...])

    return _pcall(
        body, name=name, grid=(M // tm, N // tn, nk), in_specs=specs, out_specs=out_spec, out_shape=out_shape,
        scratch_shapes=[pltpu.VMEM((tm, tn), F32)] if nk > 1 else [pltpu.VMEM((tm, tk), BF16)] if cache_at else [],
        compiler_params=_params(("parallel", "arbitrary", "arbitrary")),
    )(*ins)


SSM_BD = 4
SSM_BD_IN, SSM_BD_ST = SSM_WIDTH // SSM_BD, 2 * N_SSM_CH // SSM_BD


def _mm_bd(form, a, b, *, a_off=0, add=None, name):
    S = a.shape[0]
    off = a_off // SSM_BD_IN
    if form == "dw":
        def body(a_ref, b_ref, o_ref):
            at = a_ref[...].T.astype(BF16)
            o_ref[...] = lax.dot_general(at, b_ref[...].astype(BF16), _NN, preferred_element_type=F32)

        return _pcall(body, name=name, grid=(SSM_BD,),
                      in_specs=[pl.BlockSpec((S, SSM_BD_IN), lambda j: (0, off + j)),
                                pl.BlockSpec((S, SSM_BD_ST), lambda j: (0, j))],
                      out_specs=pl.BlockSpec((None, SSM_BD_IN, SSM_BD_ST), lambda j: (j, 0, 0)),
                      out_shape=jax.ShapeDtypeStruct((SSM_BD, SSM_BD_IN, SSM_BD_ST), F32),
                      compiler_params=_params(("parallel",)))(a, b)
    tm = _pick(S, (1024, 512, 256))
    wide, narrow = (SSM_BD_ST, SSM_BD_IN) if form == "in" else (SSM_BD_IN, SSM_BD_ST)
    dims = _NN if form == "in" else _NT

    def body(a_ref, w_ref, *rest):
        acc = lax.dot_general(a_ref[...].astype(BF16), w_ref[...].astype(BF16), dims, preferred_element_type=F32)
        if add is not None:
            acc = acc + add[1] * rest[0][...]
        rest[-1][...] = acc

    specs = [pl.BlockSpec((tm, narrow), lambda i, j: (i, off + j)),
             pl.BlockSpec((None, SSM_BD_IN, SSM_BD_ST), lambda i, j: (j, 0, 0))]
    ins = [a, b]
    if add is not None:
        specs.append(pl.BlockSpec((tm, wide), lambda i, j: (i, j)))
        ins.append(add[0])
    return _pcall(body, name=name, grid=(S // tm, SSM_BD), in_specs=specs,
                  out_specs=pl.BlockSpec((tm, wide), lambda i, j: (i, j)),
                  out_shape=jax.ShapeDtypeStruct((S, SSM_BD * wide), F32),
                  compiler_params=_params(("parallel", "parallel")))(*ins)


class Win:
    def __init__(self, arr, width, blk):
        self.arr, self.width, self.blk = arr, width, blk


def _rowwise(fn, rows, vecs=(), *, n_red=0, tr=256, name):
    wins = [r if isinstance(r, Win) else Win(r, r.shape[1], 0) for r in rows]
    S = wins[0].arr.shape[0]
    tr = min(tr, S)
    tile_args = [jax.ShapeDtypeStruct((tr, w.width), w.arr.dtype) for w in wins]
    tile_args += [jax.ShapeDtypeStruct(v.shape, v.dtype) for v in vecs]
    outs = jax.eval_shape(fn, *tile_args)
    n_row = len(outs) - n_red
    specs = [pl.BlockSpec((tr, w.width), functools.partial(lambda i, b: (i, b), b=w.blk)) for w in wins]
    specs += [pl.BlockSpec(v.shape, functools.partial(lambda i, nd: (0,) * nd, nd=v.ndim)) for v in vecs]
    out_specs = [pl.BlockSpec((tr, o.shape[1]), lambda i: (i, 0)) for o in outs[:n_row]]
    out_specs += [pl.BlockSpec(o.shape, functools.partial(lambda i, nd: (0,) * nd, nd=len(o.shape))) for o in outs[n_row:]]
    out_shape = [jax.ShapeDtypeStruct((S, o.shape[1]), o.dtype) for o in outs[:n_row]]
    out_shape += [jax.ShapeDtypeStruct(o.shape, o.dtype) for o in outs[n_row:]]
    n_in = len(wins) + len(vecs)

    def body(*refs):
        res = fn(*[r[...] for r in refs[:n_in]])
        o_refs = refs[n_in:]
        for o_ref, r in zip(o_refs[:n_row], res[:n_row]):
            o_ref[...] = r.astype(o_ref.dtype)
        i = pl.program_id(0)
        for o_ref, r in zip(o_refs[n_row:], res[n_row:]):
            @pl.when(i == 0)
            def _(o_ref=o_ref, r=r):
                o_ref[...] = r

            @pl.when(i > 0)
            def _(o_ref=o_ref, r=r):
                o_ref[...] += r

    return _pcall(
        body, name=name, grid=(S // tr,), in_specs=specs, out_specs=out_specs, out_shape=out_shape,
        compiler_params=_params(("arbitrary",)),
    )(*[w.arr for w in wins], *vecs)


def _whole(fn, *arrays, name):
    outs = jax.eval_shape(fn, *arrays)
    n_in = len(arrays)

    def body(*refs):
        res = fn(*[r[...] for r in refs[:n_in]])
        for o_ref, r in zip(refs[n_in:], res):
            o_ref[...] = r

    vm = pl.BlockSpec(memory_space=pltpu.VMEM)
    return _pcall(body, name=name, in_specs=[vm] * n_in, out_specs=[vm] * len(outs),
                  out_shape=[jax.ShapeDtypeStruct(o.shape, o.dtype) for o in outs])(*arrays)


def _split3(x):
    hi = x.astype(BF16)
    r = x - hi.astype(F32)
    mid = r.astype(BF16)
    lo = (r - mid.astype(F32)).astype(BF16)
    return hi, mid, lo


def _dot3(x, onehot, dims):
    return sum(lax.dot_general(t, onehot, dims, preferred_element_type=F32) for t in _split3(x))


_NT = (((1,), (1,)), ((), ()))
_NN = (((1,), (0,)), ((), ()))
_TN = (((0,), (0,)), ((), ()))


def _colmap(x, *, inverse, name):
    R = x.shape[0] if inverse else x.shape[1]
    tr = 256
    per = W_IN_SLOT // LANE
    n_out = N_DEV * per if inverse else Z_W // LANE
    segs = [(p, q, n) for q, p, n in W_IN_SEGS] if inverse else list(W_IN_SEGS)

    def body(x_ref, o_ref):
        ia = lax.broadcasted_iota(jnp.int32, (LANE, LANE), 0)
        ib = lax.broadcasted_iota(jnp.int32, (LANE, LANE), 1)

        def src_block(i):
            if inverse:
                return x_ref[:, i * LANE:(i + 1) * LANE]
            return x_ref[i // per, :, (i % per) * LANE:(i % per + 1) * LANE]

        for jb in range(n_out):
            acc = None
            for s0, d0, n in segs:
                lo, hi = max(d0, jb * LANE), min(d0 + n, (jb + 1) * LANE)
                if lo >= hi:
                    continue
                delta = d0 - s0
                for i in range((lo - delta) // LANE, (hi - delta - 1) // LANE + 1):
                    shift = jb * LANE - i * LANE - delta
                    sel = ((ia - ib == shift) & (ib >= lo - jb * LANE) & (ib < hi - jb * LANE)).astype(BF16)
                    blk = src_block(i)
                    part = _dot3(blk, sel, _NN) if inverse else lax.dot_general(blk, sel, _NN, preferred_element_type=F32)
                    acc = part if acc is None else acc + part
            if acc is None:
                acc = jnp.zeros((tr, LANE), F32)
            if inverse:
                o_ref[jb // per, :, (jb % per) * LANE:(jb % per + 1) * LANE] = acc
            else:
                o_ref[:, jb * LANE:(jb + 1) * LANE] = acc.astype(BF16)

    slot_spec = pl.BlockSpec((N_DEV, tr, W_IN_SLOT), lambda i: (0, i, 0))
    flat_spec = pl.BlockSpec((tr, Z_W), lambda i: (i, 0))
    if inverse:
        return _pcall(body, name=name, grid=(R // tr,), in_specs=[flat_spec], out_specs=slot_spec,
                      out_shape=jax.ShapeDtypeStruct((N_DEV, R, W_IN_SLOT), F32), compiler_params=_params(("parallel",)))(x)
    return _pcall(body, name=name, grid=(R // tr,), in_specs=[slot_spec], out_specs=flat_spec,
                  out_shape=jax.ShapeDtypeStruct((R, Z_W), BF16), compiler_params=_params(("parallel",)))(x)


def _log_sigmoid(x):
    return jnp.minimum(x, 0.0) - jnp.log(1.0 + jnp.exp(-jnp.abs(x)))


def _cum_heads(f, dcum=None, *, name):
    H, S = f.shape
    tn = min(512, S)
    rev = dcum is not None

    def body(*refs):
        j = pl.program_id(0)
        s_idx = lax.broadcasted_iota(jnp.int32, (S, tn), 0)
        t_idx = lax.broadcasted_iota(jnp.int32, (S, tn), 1) + j * tn
        if not rev:
            f_ref, o_ref = refs
            tri = (s_idx <= t_idx).astype(BF16)
            o_ref[...] = _dot3(_log_sigmoid(f_ref[...]), tri, _NN)
        else:
            fj_ref, d_ref, o_ref = refs
            tri = (s_idx >= t_idx).astype(BF16)
            o_ref[...] = _dot3(d_ref[...], tri, _NN) * jax.nn.sigmoid(-fj_ref[...])

    full = pl.BlockSpec((H, S), lambda j: (0, 0))
    blk = pl.BlockSpec((H, tn), lambda j: (0, j))
    ins, specs = ([f], [full]) if not rev else ([f, dcum], [blk, full])
    return _pcall(body, name=name, grid=(S // tn,), in_specs=specs, out_specs=blk,
                  out_shape=jax.ShapeDtypeStruct((H, S), F32), compiler_params=_params(("arbitrary",)))(*ins)


def _rel_onehot(qi, transposed):
    shape = (N_REL, CA_BAND) if transposed else (CA_BAND, N_REL)
    kk = lax.broadcasted_iota(jnp.int32, shape, 1 if transposed else 0)
    rr = lax.broadcasted_iota(jnp.int32, shape, 0 if transposed else 1)
    idx = jnp.clip(CA_PAD + qi - kk, REL_MIN, REL_MAX) - REL_MIN
    return (idx == rr).astype(BF16)


def _relbias_expand(rb):
    rows = 8

    def body(rb_ref, o_ref):
        for r in range(rows):
            o_ref[r] = _dot3(rb_ref[...], _rel_onehot(pl.program_id(0) * rows + r, True), _NN)

    return _pcall(body, name="relbias_expand", grid=(CHUNK // rows,),
                  in_specs=[pl.BlockSpec((CA_HEADS, N_REL), lambda q: (0, 0))],
                  out_specs=pl.BlockSpec((rows, CA_HEADS, CA_BAND), lambda q: (q, 0, 0)),
                  out_shape=jax.ShapeDtypeStruct((CHUNK, CA_HEADS, CA_BAND), F32),
                  compiler_params=_params(("arbitrary",)))(rb)


def _relbias_reduce(db):
    rows = 8

    def body(db_ref, o_ref):
        q = pl.program_id(0)
        part = sum(_dot3(db_ref[r], _rel_onehot(q * rows + r, False), _NN) for r in range(rows))

        @pl.when(q == 0)
        def _():
            o_ref[...] = part

        @pl.when(q > 0)
        def _():
            o_ref[...] += part

    return _pcall(body, name="relbias_reduce", grid=(CHUNK // rows,),
                  in_specs=[pl.BlockSpec((rows, CA_HEADS, CA_BAND), lambda q: (q, 0, 0))],
                  out_specs=pl.BlockSpec((CA_HEADS, N_REL), lambda q: (0, 0)),
                  out_shape=jax.ShapeDtypeStruct((CA_HEADS, N_REL), F32),
                  compiler_params=_params(("arbitrary",)))(db)


def _attn_cfg(mode, S):
    if mode == "fox":
        return min(256, S), FOX_HEAD_DIM ** -0.5
    if mode == "chunk":
        return CHUNK, CA_HEAD_DIM ** -0.5
    return min(512, S), XA_HEAD_DIM ** -0.5


def _visible(mode, i, tq, nk):
    if mode == "fox":
        row = lax.broadcasted_iota(jnp.int32, (tq, nk), 0) + i * tq
        return row >= lax.broadcasted_iota(jnp.int32, (tq, nk), 1)
    if mode == "chunk":
        return lax.broadcasted_iota(jnp.int32, (tq, nk), 1) + i * CHUNK >= CA_PAD
    return None


def _scores(mode, qs, kb, extra, visible):
    s = lax.dot_general(qs, kb, _NT, preferred_element_type=F32)
    if mode == "fox":
        cq, ck = extra
        s = jnp.where(visible, s + cq - ck, NEG_INF)
    elif mode == "chunk":
        (bias,) = extra
        s = jnp.where(visible, s + bias, NEG_INF)
    return s


class _AttnPlan:
    def __init__(self, mode, q, k, pp):
        self.mode, self.pp = mode, pp
        self.D, self.hpb, self.bw = (XA_HEAD_DIM, 1, XA_HEAD_DIM) if mode == "xa" else (64, 2, LANE)
        self.S, self.Sk = q[0].shape[0], k[0].shape[0]
        self.H = (XA_HEADS if mode == "xa" else FOX_HEADS)
        self.W = self.H * self.D
        self.gw = self.bw * pp
        self.hpg = self.hpb * pp
        self.tq, self.scale = _attn_cfg(mode, self.S)
        self.grid = (self.W // self.gw, self.S // self.tq)

    def rows(self, win):
        off = win[1] // self.gw
        return pl.BlockSpec((self.tq, self.gw), lambda g, i: (i, off + g))

    def cols(self, win):
        off = win[1] // self.gw
        return pl.BlockSpec((win[0].shape[0], self.gw), lambda g, i: (0, off + g))

    def extras(self):
        if self.mode == "fox":
            return [pl.BlockSpec((self.hpg, self.tq, 1), lambda g, i: (g, i, 0)),
                    pl.BlockSpec((self.hpg, 1, self.Sk), lambda g, i: (g, 0, 0))]
        if self.mode == "chunk":
            return [pl.BlockSpec((self.hpg, CHUNK, CA_BAND), lambda g, i: (g, 0, 0))]
        return []

    def per_row(self):
        return pl.BlockSpec((self.hpg, self.tq, 1), lambda g, i: (g, i, 0))

    def lanes(self, p):
        return slice(p * self.bw, (p + 1) * self.bw)

    def keys(self, i, ref, p, nk):
        if self.mode == "chunk":
            return ref[pl.ds(pl.multiple_of(i * CHUNK, CHUNK), CA_BAND), self.lanes(p)]
        return ref[0:nk, self.lanes(p)]

    def key_ranges(self, i, run):
        nq = self.grid[1]
        if self.mode != "fox" or nq % 4:
            return run(CA_BAND if self.mode == "chunk" else self.Sk)
        for part in range(4):
            lo, hi = part * nq // 4, (part + 1) * nq // 4
            pl.when((i >= lo) & (i < hi))(functools.partial(run, hi * self.tq))

    def head(self, x, hh):
        if self.hpb == 1:
            return x
        lane = lax.broadcasted_iota(jnp.int32, x.shape, 1)
        return jnp.where(lane // self.D == hh, x, jnp.zeros_like(x))


def _attn_fwd(mode, q, k, v, extra=(), *, name):
    pl_ = _AttnPlan(mode, q, k, pp=4)
    n_ex = len(extra)
    out = (None, 0)

    def body(*refs):
        q_ref, k_ref, v_ref = refs[:3]
        o_ref, lse_ref = refs[3 + n_ex:]
        i = pl.program_id(1)

        def run(nk):
            visible = _visible(mode, i, pl_.tq, nk)
            for p in range(pl_.pp):
                qs = q_ref[:, pl_.lanes(p)].astype(BF16) * pl_.scale
                kp = pl_.keys(i, k_ref, p, nk).astype(BF16)
                vp = pl_.keys(i, v_ref, p, nk).astype(BF16)
                acc = None
                for hh in range(pl_.hpb):
                    h = p * pl_.hpb + hh
                    ex = [r[h] for r in refs[3:3 + n_ex]]
                    if mode == "fox":
                        ex[1] = ex[1][:, 0:nk]
                    s = _scores(mode, pl_.head(qs, hh), kp, ex, visible)
                    m = jnp.max(s, axis=1, keepdims=True)
                    e = jnp.exp(s - m)
                    l = jnp.sum(e, axis=1, keepdims=True)
                    part = lax.dot_general((e * (1.0 / l)).astype(BF16), pl_.head(vp, hh), _NN,
                                           preferred_element_type=F32)
                    acc = part if acc is None else acc + part
                    lse_ref[h] = m + jnp.log(l)
                o_ref[:, pl_.lanes(p)] = acc

        pl_.key_ranges(i, run)

    return _pcall(
        body, name=name, grid=pl_.grid, in_specs=[pl_.rows(q), pl_.cols(k), pl_.cols(v)] + pl_.extras(),
        out_specs=[pl_.rows(out), pl_.per_row()],
        out_shape=[jax.ShapeDtypeStruct((pl_.S, pl_.W), F32), jax.ShapeDtypeStruct((pl_.H, pl_.S, 1), F32)],
        compiler_params=_params(("parallel", "arbitrary")),
    )(q[0], k[0], v[0], *extra)


def _attn_bwd(mode, q, k, v, lse, do, extra=(), *, name):
    pl_ = _AttnPlan(mode, q, k, pp=2 if mode == "fox" else 4)
    H, S, Sk, W = pl_.H, pl_.S, pl_.Sk, pl_.W
    n_ex = len(extra)
    out = (None, 0)
    kv_out = pl.BlockSpec((Sk, pl_.gw), lambda g, i: (0, g))
    ex_specs = pl_.extras()
    out_specs = [pl_.rows(out), kv_out, kv_out]
    out_shape = [jax.ShapeDtypeStruct((S, W), F32), jax.ShapeDtypeStruct((Sk, W), F32),
                 jax.ShapeDtypeStruct((Sk, W), F32)]
    if mode == "fox":
        out_specs += [pl_.per_row(), ex_specs[1]]
        out_shape += [jax.ShapeDtypeStruct((H, S, 1), F32), jax.ShapeDtypeStruct((H, 1, Sk), F32)]
    elif mode == "chunk":
        out_specs += [ex_specs[0]]
        out_shape += [jax.ShapeDtypeStruct((H, CHUNK, CA_BAND), F32)]

    def body(*refs):
        q_ref, k_ref, v_ref, lse_ref, do_ref = refs[:5]
        dq_ref, dk_ref, dv_ref = refs[5 + n_ex:8 + n_ex]
        rest = refs[8 + n_ex:]
        i = pl.program_id(1)

        @pl.when(i == 0)
        def _():
            dk_ref[...] = jnp.zeros_like(dk_ref)
            dv_ref[...] = jnp.zeros_like(dv_ref)
            if mode == "fox":
                rest[1][...] = jnp.zeros_like(rest[1])
            elif mode == "chunk":
                rest[0][...] = jnp.zeros_like(rest[0])

        def run(nk):
            visible = _visible(mode, i, pl_.tq, nk)
            for p in range(pl_.pp):
                lanes = pl_.lanes(p)
                qs = q_ref[:, lanes].astype(BF16) * pl_.scale
                kp = pl_.keys(i, k_ref, p, nk).astype(BF16)
                vp = pl_.keys(i, v_ref, p, nk).astype(BF16)
                dop = do_ref[:, lanes].astype(BF16)
                dq = dk_part = dv_part = None
                for hh in range(pl_.hpb):
                    h = p * pl_.hpb + hh
                    ex = [r[h] for r in refs[5:5 + n_ex]]
                    if mode == "fox":
                        ex[1] = ex[1][:, 0:nk]
                    qh, doh = pl_.head(qs, hh), pl_.head(dop, hh)
                    s = _scores(mode, qh, kp, ex, visible)
                    pr = jnp.exp(s - lse_ref[h])
                    dp = lax.dot_general(doh, vp, _NT, preferred_element_type=F32)
                    ds = pr * (dp - jnp.sum(dp * pr, axis=1, keepdims=True))
                    dsb = ds.astype(BF16)
                    parts = (lax.dot_general(dsb, pl_.head(kp, hh), _NN, preferred_element_type=F32) * pl_.scale,
                             lax.dot_general(dsb, qh, _TN, preferred_element_type=F32),
                             lax.dot_general(pr.astype(BF16), doh, _TN, preferred_element_type=F32))
                    dq, dk_part, dv_part = parts if dq is None else (dq + parts[0], dk_part + parts[1],
                                                                     dv_part + parts[2])
                    if mode == "chunk":
                        rest[0][h] += ds
                    if mode == "fox":
                        rest[0][h] = jnp.sum(ds, axis=1, keepdims=True)
                        rest[1][h, :, 0:nk] += -jnp.sum(ds, axis=0, keepdims=True)
                dq_ref[:, lanes] = dq
                if mode == "chunk":
                    win = pl.ds(pl.multiple_of(i * CHUNK, CHUNK), CA_BAND)
                    dk_ref[win, lanes] += dk_part
                    dv_ref[win, lanes] += dv_part
                else:
                    dk_ref[0:nk, lanes] += dk_part
                    dv_ref[0:nk, lanes] += dv_part

        pl_.key_ranges(i, run)

    return _pcall(
        body, name=name, grid=pl_.grid,
        in_specs=[pl_.rows(q), pl_.cols(k), pl_.cols(v), pl_.per_row(), pl_.rows(do)] + ex_specs,
        out_specs=out_specs, out_shape=out_shape,
        compiler_params=_params(("parallel", "arbitrary")),
    )(q[0], k[0], v[0], lse, do[0], *extra)


def _scan(x, a, h=None, *, name):
    S = x.shape[0]
    CB = SCAN_CB
    rev = h is not None
    n_grp = S // 8

    def body(*refs):
        if rev:
            x_ref, a_ref, h_ref, o_ref, da_ref = refs
        else:
            x_ref, a_ref, o_ref = refs
        ar = a_ref[:, :CB]
        ai = -a_ref[:, CB:] if rev else a_ref[:, CB:]
        zero = jnp.zeros((1, CB), F32)

        def group(g, carry):
            base = pl.multiple_of((n_grp - 1 - g) * 8 if rev else g * 8, 8)
            for j in (range(7, -1, -1) if rev else range(8)):
                t = base + j
                if rev:
                    hr, hi, dar, dai = carry
                else:
                    hr, hi = carry
                xr = x_ref[pl.ds(t, 1), :CB]
                xi = x_ref[pl.ds(t, 1), CB:]
                hr, hi = ar * hr - ai * hi + xr, ar * hi + ai * hr + xi
                o_ref[pl.ds(t, 1), :CB] = hr
                o_ref[pl.ds(t, 1), CB:] = hi
                if rev:
                    tp = jnp.maximum(t - 1, 0)
                    live = (t > 0).astype(F32)
                    pr = h_ref[pl.ds(tp, 1), :CB] * live
                    pi = h_ref[pl.ds(tp, 1), CB:] * live
                    carry = (hr, hi, dar + hr * pr + hi * pi, dai + hi * pr - hr * pi)
                else:
                    carry = (hr, hi)
            return carry

        if rev:
            _, _, dar, dai = lax.fori_loop(0, n_grp, group, (zero, zero, zero, zero))
            da_ref[:, :CB] = dar
            da_ref[:, CB:] = dai
        else:
            lax.fori_loop(0, n_grp, group, (zero, zero))

    big = pl.BlockSpec((S, 2 * CB), lambda c: (0, c))
    vec = pl.BlockSpec((1, 2 * CB), lambda c: (0, c))
    n_blk = x.shape[1] // (2 * CB)
    if rev:
        return _pcall(body, name=name, grid=(n_blk,), in_specs=[big, vec, big], out_specs=[big, vec],
                      out_shape=[jax.ShapeDtypeStruct(x.shape, F32), jax.ShapeDtypeStruct(a.shape, F32)],
                      compiler_params=_params(("parallel",)))(x, a, h)
    return _pcall(body, name=name, grid=(n_blk,), in_specs=[big, vec], out_specs=big,
                  out_shape=jax.ShapeDtypeStruct(x.shape, F32), compiler_params=_params(("parallel",)))(x, a)


def _ssm_prep1(lr_, li, ldt):
    lr = jnp.minimum(lr_, -1e-4)
    dt = jnp.exp(ldt)
    mag = jnp.exp(lr * dt)
    ar = mag * jnp.cos(li * dt)
    ai = mag * jnp.sin(li * dt)
    den = lr * lr + li * li
    gr = ((ar - 1.0) * lr + ai * li) / den
    gi = (ai * lr - (ar - 1.0) * li) / den
    return ar, ai, gr, gi


def _ssm_prep2(gr, gi, br, bi):
    return gr * br - gi * bi, gr * bi + gi * br


def _vjp_of(fn, n_in):
    def bwd(*args):
        cts = args[n_in:]
        return jax.vjp(fn, *args[:n_in])[1](cts[0] if len(cts) == 1 else tuple(cts))
    return bwd


def _to_blocked(r, i):
    lead = r.shape[:-1]
    t = jnp.stack([r.reshape(lead + (N_SSM_CH // SCAN_CB, SCAN_CB)), i.reshape(lead + (N_SSM_CH // SCAN_CB, SCAN_CB))],
                  axis=-2)
    return t.reshape(lead + (2 * N_SSM_CH,))


def _from_blocked(m):
    lead = m.shape[:-1]
    t = m.reshape(lead + (N_SSM_CH // SCAN_CB, 2, SCAN_CB))
    return t[..., 0, :].reshape(lead + (N_SSM_CH,)), t[..., 1, :].reshape(lead + (N_SSM_CH,))


_GPB = SSM_GROUPS // SSM_BD
_GPS = SCAN_CB // SSM_STATE


def _bd_eye():
    return jnp.eye(_GPB, dtype=F32).reshape(_GPB, _GPB // _GPS, _GPS)


def _blockdiag(r, i):
    v = jnp.stack([r, i]).reshape(2, SSM_BD, _GPB, SSM_GROUP, SSM_STATE)
    return jnp.einsum("qjgcp,gsh->jgcsqhp", v, _bd_eye()).reshape(SSM_BD, SSM_BD_IN, SSM_BD_ST)


def _blockdiag_inv(m):
    d = m.reshape(SSM_BD, _GPB, SSM_GROUP, _GPB // _GPS, 2, _GPS, SSM_STATE)
    v = jnp.einsum("jgcsqhp,gsh->qjgcp", d, _bd_eye()).reshape(2, SSM_GROUPS, SSM_GROUP, SSM_STATE)
    return v[0], v[1]


def _shift_rows(x, n):
    S = x.shape[0]
    row = lax.broadcasted_iota(jnp.int32, x.shape, 0)
    if n > 0:
        return jnp.where(row >= n, pltpu.roll(x, n, 0), 0.0)
    return jnp.where(row < S + n, pltpu.roll(x, S + n, 0), 0.0)


def _bf(x):
    return x.astype(BF16).astype(F32)


def _conv_pre(a, w, b):
    ab, wb = _bf(a), _bf(w)
    return wb[2:3] * ab + wb[1:2] * _shift_rows(ab, 1) + wb[0:1] * _shift_rows(ab, 2) + b


def _ffn_mid(up, conv_w, conv_b, dh=None, *, name):
    S = up.shape[0]
    tn = 2 * LANE
    nb = D_FF_P // tn
    rev = dh is not None

    def body(*refs):
        if not rev:
            a_ref, g_ref, w_ref, b_ref, o_ref = refs
            o_ref[...] = jax.nn.gelu(_conv_pre(a_ref[...], w_ref[...], b_ref[...])) * g_ref[...]
            return
        a_ref, g_ref, w_ref, b_ref, dh_ref, dup_a_ref, dup_g_ref, dw_ref, db_ref = refs
        a, w, dh_ = a_ref[...], w_ref[...], dh_ref[...]
        pre = _conv_pre(a, w, b_ref[...])
        gl, gelu_vjp = jax.vjp(jax.nn.gelu, pre)
        dup_g_ref[...] = dh_ * gl
        (dpre,) = gelu_vjp(dh_ * g_ref[...])
        db_ref[...] = jnp.sum(dpre, axis=0, keepdims=True)
        dpb, ab, wb = _bf(dpre), _bf(a), _bf(w)
        dup_a_ref[...] = wb[2:3] * dpb + wb[1:2] * _shift_rows(dpb, -1) + wb[0:1] * _shift_rows(dpb, -2)
        dw_ref[2:3, :] = jnp.sum(dpb * ab, axis=0, keepdims=True)
        dw_ref[1:2, :] = jnp.sum(dpb * _shift_rows(ab, 1), axis=0, keepdims=True)
        dw_ref[0:1, :] = jnp.sum(dpb * _shift_rows(ab, 2), axis=0, keepdims=True)

    a_spec = pl.BlockSpec((S, tn), lambda j: (0, j))
    g_spec = pl.BlockSpec((S, tn), lambda j: (0, j + nb))
    w_spec = pl.BlockSpec((3, tn), lambda j: (0, j))
    b_spec = pl.BlockSpec((1, tn), lambda j: (0, j))
    if not rev:
        return _pcall(body, name=name, grid=(nb,), in_specs=[a_spec, g_spec, w_spec, b_spec], out_specs=a_spec,
                      out_shape=jax.ShapeDtypeStruct((S, D_FF_P), F32), compiler_params=_params(("parallel",)))(
                          up, up, conv_w, conv_b)
    return _pcall(body, name=name, grid=(nb,), in_specs=[a_spec, g_spec, w_spec, b_spec, a_spec],
                  out_specs=[a_spec, a_spec, w_spec, b_spec],
                  out_shape=[jax.ShapeDtypeStruct((S, D_FF_P), F32), jax.ShapeDtypeStruct((S, D_FF_P), F32),
                             jax.ShapeDtypeStruct((3, D_FF_P), F32), jax.ShapeDtypeStruct((1, D_FF_P), F32)],
                  compiler_params=_params(("parallel",)))(up, up, conv_w, conv_b, dh)


def _ff_pad(t):
    lead = t.shape[:-1]
    t = t.reshape(lead + (N_DEV, FF_HALF))
    return jnp.pad(t, [(0, 0)] * len(lead) + [(0, 0), (0, FF_HALF_P - FF_HALF)]).reshape(lead + (D_FF_P,))


def _ff_unpad(t):
    lead = t.shape[:-1]
    return t.reshape(lead + (N_DEV, FF_HALF_P))[..., :FF_HALF].reshape(lead + (D_FF,))


def _ln_fwd(x, h, g, b):
    r = DN_ALPHA * x + h
    mu = jnp.mean(r, axis=-1, keepdims=True)
    var = jnp.mean(jnp.square(r - mu), axis=-1, keepdims=True)
    return r, (r - mu) * lax.rsqrt(var + LN_EPS) * g + b


def _ln_bwd(r, dy, g):
    mu = jnp.mean(r, axis=-1, keepdims=True)
    var = jnp.mean(jnp.square(r - mu), axis=-1, keepdims=True)
    xhat = (r - mu) * lax.rsqrt(var + LN_EPS)
    dxh = dy * g
    dr = lax.rsqrt(var + LN_EPS) * (dxh - jnp.mean(dxh, axis=-1, keepdims=True)
                                    - xhat * jnp.mean(dxh * xhat, axis=-1, keepdims=True))
    return dr, jnp.sum(dy * xhat, axis=0, keepdims=True), jnp.sum(dy, axis=0, keepdims=True)


def _merge(gf, gs, gc, ya, yb2, yc):
    yb = yb2[:, :D_MODEL] * jax.nn.sigmoid(yb2[:, D_MODEL:])
    return jax.nn.sigmoid(gf) * ya + jax.nn.sigmoid(gs) * yb + jax.nn.sigmoid(gc) * yc


def _s5_tail(hc, su, d):
    return jax.nn.gelu(hc + d * su)


def _s5_tail_bwd(hc, su, dgel, d):
    _, vjp = jax.vjp(jax.nn.gelu, hc + d * su)
    (dy,) = vjp(dgel)
    return dy, d * dy, jnp.sum(dy * su, axis=0, keepdims=True)


def _loss_rows(y, tgt):
    err = y - tgt
    return err * (1.0 / D_MODEL), jnp.sum(0.5 * jnp.square(err), axis=0, keepdims=True) * (1.0 / D_MODEL)


def _peer(k):
    x, y, c = lax.axis_index("x"), lax.axis_index("y"), lax.axis_index("c")
    return (x ^ ((k >> 2) & 1), y ^ ((k >> 1) & 1), c ^ (k & 1))


def _my_slot():
    return 4 * lax.axis_index("x") + 2 * lax.axis_index("y") + lax.axis_index("c")


def _peer_slot(k):
    px, py, pc = _peer(k)
    return 4 * px + 2 * py + pc


N_CHIP = N_DEV // 2
OTHER_CHIPS = (2, 4, 6)


def _chip_of(dev):
    return 2 * dev[0] + dev[1]


def _remote(src, dst, send, recv, dev):
    return pltpu.make_async_remote_copy(src_ref=src, dst_ref=dst, send_sem=send, recv_sem=recv, device_id=dev,
                                        device_id_type=pl.DeviceIdType.MESH)


def _all_gather(shards, *, name, host=None, then=None):
    return _exchange(shards, *_all_gather_parts(shards), name=name, host=host, then=then)


def _exchange(ins, out_shapes, sem_shapes, start, finish, *, name, host=None, then=None):
    if host is not None:
        rider = _Rider(host, ins, out_shapes, sem_shapes, start, finish, then)
        _RIDERS.append(rider)
        return rider
    n = len(ins)

    def body(*refs):
        start(refs[:n], refs[n:2 * n], refs[2 * n:])
        finish(refs[:n], refs[n:2 * n], refs[2 * n:])

    hbm = pl.BlockSpec(memory_space=pl.ANY)
    return _pcall(body, name=name, in_specs=[hbm] * n, out_specs=[hbm] * n, out_shape=list(out_shapes),
                  scratch_shapes=list(sem_shapes))(*ins)


def _all_gather_parts(shards):
    n = len(shards)

    def first_copies(ins, outs, sems):
        send, recv, _ = sems
        return [_remote(ins[t], outs[t].at[_my_slot()], send.at[t, k - 1], recv.at[t, k - 1], _peer(k))
                for k in (1,) + OTHER_CHIPS for t in range(n)]

    def local_copies(ins, outs, sems):
        return [pltpu.make_async_copy(ins[t], outs[t].at[_my_slot()], sems[2].at[t]) for t in range(n)]

    def start(ins, outs, sems):
        for cp in local_copies(ins, outs, sems) + first_copies(ins, outs, sems):
            cp.start()

    def finish(ins, outs, sems):
        send, recv, _ = sems
        sibling = _peer(1)
        passed = []
        for k in OTHER_CHIPS:
            for t in range(n):
                slot = outs[t].at[_peer_slot(k)]
                _remote(ins[t], slot, send.at[t, k - 1], recv.at[t, k - 1], _peer(k)).wait_recv()
                cp = _remote(slot, slot, send.at[t, k], recv.at[t, k], sibling)
                cp.start()
                passed.append(cp)
        for t in range(n):
            _remote(ins[t], outs[t].at[_peer_slot(1)], send.at[t, 0], recv.at[t, 0], sibling).wait_recv()
            for k in OTHER_CHIPS:
                _remote(ins[t], outs[t].at[_peer_slot(k + 1)], send.at[t, k], recv.at[t, k], sibling).wait_recv()
        for cp in first_copies(ins, outs, sems) + passed:
            cp.wait_send()
        for lc in local_copies(ins, outs, sems):
            lc.wait()

    out_shapes = [jax.ShapeDtypeStruct((N_DEV,) + s.shape, s.dtype) for s in shards]
    sem_shapes = [pltpu.SemaphoreType.DMA((n, N_DEV - 1)), pltpu.SemaphoreType.DMA((n, N_DEV - 1)),
                  pltpu.SemaphoreType.DMA((n,))]
    return out_shapes, sem_shapes, start, finish


def _sibling_swap(grads, *, name, host=None, then=None):
    n = len(grads)

    def copies(ins, outs, sems):
        c = lax.axis_index("c")
        return [_remote(ins[t].at[:, 1 - c], outs[t], sems[0].at[t], sems[1].at[t], _peer(1)) for t in range(n)]

    def start(ins, outs, sems):
        for cp in copies(ins, outs, sems):
            cp.start()

    def finish(ins, outs, sems):
        for cp in copies(ins, outs, sems):
            cp.wait()

    out_shapes = [jax.ShapeDtypeStruct((N_CHIP,) + g.shape[2:], g.dtype) for g in grads]
    sem_shapes = [pltpu.SemaphoreType.DMA((n,)), pltpu.SemaphoreType.DMA((n,))]
    return _exchange(grads, out_shapes, sem_shapes, start, finish, name=name, host=host, then=then)


def _pair_add(g, p, out_dtype, *, name):
    _, _, R, C = g.shape
    tr = _pick(R, (512, 256, 128, 64, 32, 16, 8))

    def body(c_ref, g_ref, p_ref, o_ref):
        o_ref[...] = (g_ref[...] + p_ref[...]).astype(out_dtype)

    grid_spec = pltpu.PrefetchScalarGridSpec(
        num_scalar_prefetch=1, grid=(N_CHIP, R // tr),
        in_specs=[pl.BlockSpec((None, None, tr, C), lambda j, i, c_ref: (j, c_ref[0], i, 0)),
                  pl.BlockSpec((None, tr, C), lambda j, i, c_ref: (j, i, 0))],
        out_specs=pl.BlockSpec((None, tr, C), lambda j, i, c_ref: (j, i, 0)))
    core = lax.axis_index("c").astype(jnp.int32).reshape(1)
    return _pcall(body, name=name, grid_spec=grid_spec, out_shape=jax.ShapeDtypeStruct(p.shape, out_dtype),
                  compiler_params=_params(("parallel", "parallel")))(core, g, p)


def _chip_exchange(sums, *, name, host=None):
    n = len(sums)

    def copies(ins, outs, sems, dst_is_mine):
        send, recv, _ = sems
        mine = 2 * lax.axis_index("x") + lax.axis_index("y")
        out = []
        for k in OTHER_CHIPS:
            theirs = _chip_of(_peer(k))
            for t in range(n):
                out.append(_remote(ins[t].at[theirs], outs[t].at[mine if dst_is_mine else theirs],
                                   send.at[t, k // 2 - 1], recv.at[t, k // 2 - 1], _peer(k)))
        return out

    def local_copies(ins, outs, sems):
        mine = 2 * lax.axis_index("x") + lax.axis_index("y")
        return [pltpu.make_async_copy(ins[t].at[mine], outs[t].at[mine], sems[2].at[t]) for t in range(n)]

    def start(ins, outs, sems):
        for cp in local_copies(ins, outs, sems) + copies(ins, outs, sems, True):
            cp.start()

    def finish(ins, outs, sems):
        for cp in copies(ins, outs, sems, False) + local_copies(ins, outs, sems):
            cp.wait()

    out_shapes = [jax.ShapeDtypeStruct(s.shape, s.dtype) for s in sums]
    sem_shapes = [pltpu.SemaphoreType.DMA((n, N_CHIP - 1)), pltpu.SemaphoreType.DMA((n, N_CHIP - 1)),
                  pltpu.SemaphoreType.DMA((n,))]
    return _exchange(sums, out_shapes, sem_shapes, start, finish, name=name, host=host)


class _GradReduce:
    def __init__(self, grads, wire_dtypes, hosts=None):
        pairs = [g.reshape((N_CHIP, 2) + g.shape[1:]) for g in grads]
        self.n, self.riders, self.recv = len(grads), [], None

        def after_swap(partner):
            sums = [_pair_add(g, p, dt, name="grad_pair_add") for g, p, dt in zip(pairs, partner, wire_dtypes)]
            if hosts is None:
                self.recv = _chip_exchange(sums, name="grad_chip_exchange")
            else:
                self.riders = [(idx, _chip_exchange([sums[i] for i in idx], name="grad_chip_exchange", host=h))
                               for h, idx in hosts[1]]

        if hosts is None:
            after_swap(_sibling_swap(pairs, name="grad_sibling_swap"))
        else:
            _sibling_swap(pairs, name="grad_sibling_swap", host=hosts[0], then=after_swap)

    def result(self):
        if self.recv is None:
            self.recv = [None] * self.n
            for idx, rider in self.riders:
                assert rider.results is not None, rider.host
                for i, r in zip(idx, rider.results):
                    self.recv[i] = r
        return self.recv


def _adamw(recv, w, m, v, layer=None, into=None, *, name):
    n_slots, R, C = recv.shape
    tr = _pick(R, (256, 128, 64, 32, 16, 8))

    def body(r_ref, w_ref, m_ref, v_ref, *rest):
        g_ref, d_ref, nm_ref, nv_ref = rest[-4:]
        g = r_ref[0].astype(F32)
        for s in range(1, n_slots):
            g = g + r_ref[s].astype(F32)
        m_new = ADAM_B1 * m_ref[...] + (1.0 - ADAM_B1) * g
        v_new = ADAM_B2 * v_ref[...] + (1.0 - ADAM_B2) * jnp.square(g)
        m_hat = m_new / (1.0 - ADAM_B1 ** ADAM_STEP)
        v_hat = v_new / (1.0 - ADAM_B2 ** ADAM_STEP)
        g_ref[...] = g
        d_ref[...] = -ADAM_LR * (m_hat / (jnp.sqrt(v_hat) + ADAM_EPS) + ADAM_WD * w_ref[...])
        nm_ref[...] = m_new
        nv_ref[...] = v_new

    row = pl.BlockSpec((tr, C), lambda i: (i, 0))
    state = row if layer is None else pl.BlockSpec((None, tr, C), lambda i: (layer, i, 0))
    in_specs = [pl.BlockSpec((n_slots, tr, C), lambda i: (0, i, 0)), state, state, state]
    if into is None:
        return _pcall(body, name=name, grid=(R // tr,), in_specs=in_specs, out_specs=[row] * 4,
                      out_shape=[jax.ShapeDtypeStruct((R, C), F32)] * 4, compiler_params=_params(("parallel",)),
                      )(recv, w, m, v)
    return _pcall(body, name=name, grid=(R // tr,), in_specs=in_specs + [pl.BlockSpec(memory_space=pl.ANY)] * 4,
                  out_specs=[state] * 4, out_shape=[jax.ShapeDtypeStruct((DEPTH, R, C), F32)] * 4,
                  input_output_aliases={4 + j: j for j in range(4)}, compiler_params=_params(("parallel",)),
                  )(recv, w, m, v, *into)


def _flat_pad(parts, rows):
    flat = jnp.concatenate([p.reshape(-1) for p in parts])
    return jnp.pad(flat, (0, rows * LANES - flat.shape[0])).reshape(rows, LANES)


def _small_shard_shape(n):
    return SMALL[n][:-1] + (SMALL[n][-1] // N_DEV,)


def _unpack_small(gathered):
    out, off = {}, 0
    flat = gathered.reshape(N_DEV, -1)
    for n in SMALL:
        r, c = _small_shard_shape(n)
        out[n] = flat[:, off:off + r * c].reshape(N_DEV, r, c).transpose(1, 0, 2).reshape(r, N_DEV * c)
        off += r * c
    return out


def _pack_state(state, prefix):
    flat = jnp.concatenate([state[prefix + n].reshape(DEPTH, -1) for n in (*SMALL, *REPL)], axis=1)
    return jnp.pad(flat, ((0, 0), (0, PACK_ROWS * LANES - flat.shape[1]))).reshape(DEPTH * PACK_ROWS, LANES)


def _pack_small_grads(grads):
    cols = []
    for n in SMALL:
        r, c = _small_shard_shape(n)
        cols.append(grads[n].reshape(r, N_DEV, c).transpose(1, 0, 2).reshape(N_DEV, r * c))
    cols += [jnp.broadcast_to(grads[n].reshape(1, -1), (N_DEV, grads[n].size)) for n in REPL]
    flat = jnp.concatenate(cols, axis=1)
    return jnp.pad(flat, ((0, 0), (0, PACK_ROWS * LANES - flat.shape[1]))).reshape(N_DEV, PACK_ROWS, LANES)


def _unpack_state(packed):
    out, off = {}, 0
    flat = packed.reshape(DEPTH, -1)
    for n, shape in [(n, _small_shard_shape(n)) for n in SMALL] + list(REPL.items()):
        sz = math.prod(shape)
        out[n] = flat[:, off:off + sz].reshape((DEPTH,) + shape)
        off += sz
    return out


def _pad_b_in(b):
    parts, pos = [], 0
    for src, width, dst in Z_PIECES:
        parts += [jnp.zeros((b.shape[0], dst - pos), b.dtype), b[:, src:src + width]]
        pos = dst + width
    return jnp.concatenate(parts + [jnp.zeros((b.shape[0], Z_W - pos), b.dtype)], axis=1)


def _unpad_b_in(bp):
    return jnp.concatenate([bp[:, dst:dst + width] for _, width, dst in Z_PIECES], axis=1)


def _up_shard_pad(t):
    gap = jnp.zeros(t.shape[:-1] + (FF_HALF_P - FF_HALF,), t.dtype)
    return jnp.concatenate([t[..., :FF_HALF], gap, t[..., FF_HALF:], gap], axis=-1)


def _up_shard_unpad(t):
    return jnp.concatenate([t[..., :FF_HALF], t[..., FF_HALF_P:FF_HALF_P + FF_HALF]], axis=-1)


def _pad_shard(n, t):
    lead = [(0, 0)] * (t.ndim - 2)
    if n == "w_in":
        return jnp.pad(t, lead + [(0, 0), (0, W_IN_SLOT - W_IN_SHARD)])
    if n == "ffn_w_up":
        return _up_shard_pad(t)
    if n == "ffn_w_down":
        return jnp.pad(t, lead + [(0, FF_HALF_P - FF_HALF), (0, 0)])
    return t


def _unpad_shard(n, t):
    if n == "w_in":
        return t[..., :W_IN_SHARD]
    if n == "ffn_w_up":
        return _up_shard_unpad(t)
    if n == "ffn_w_down":
        return t[..., :FF_HALF, :]
    return t


GATHER_HOSTS = (("chunk_fwd", (0,)), ("fox_fwd", (8,)), ("s5_scan", (9,)), ("mm_in", (1, 2, 3, 4)),
                ("mm_up", (5, 6, 7, 10)))
FIRST_GATHER_HOSTS = ((None, (0,)), ("mm_in", (1, 2, 3, 4, 5, 6, 7, 10)), ("s5_scan", (8, 9)))
SECOND_GATHER_HOSTS = (("fox_fwd", (0,)), ("chunk_fwd", (8, 9)), ("mm_up", (1, 2, 3, 4, 5, 6, 7, 10)))
REDUCE_HOSTS = ("mm_up_dw", (("fox_bwd", (0,)), ("s5_scan_bwd", (1, 2, 3, 4, 5, 6, 7, 10)), ("chunk_bwd", (8,)),
                             ("mm_in_dx", (9,))))


def _layer_shards(state, layer):
    shards = [_pad_shard(n, state[n][layer].astype(BF16)) for n in BIG]
    return shards + [_flat_pad([state[n][layer] for n in SMALL], SMALL_ROWS)]


class _Weights:
    def __init__(self, sources):
        self._sources, self._cache = sources, {}

    def _operand(self, i):
        src = self._sources[i]
        if isinstance(src, tuple):
            assert src[0].results is not None, src[0].host
            return src[0].results[src[1]]
        return src

    def __getitem__(self, n):
        if n not in self._cache:
            if n in SMALL:
                self._cache.update(_unpack_small(self._operand(len(BIG))))
            elif n == "w_in_p":
                self._cache[n] = _colmap(self._operand(0), inverse=False, name="w_in_colmap")
            else:
                w = self._operand(list(BIG).index(n))
                self._cache[n] = w.reshape(-1, D_MODEL) if BIG[n][1] == 0 else w
        return self._cache[n]


def _gather_riding(shards, hosts):
    sources = [None] * len(shards)
    for host, idx in hosts:
        group = [shards[i] for i in idx]
        if host is None:
            for i, g in zip(idx, _all_gather(group, name="all_gather_weights")):
                sources[i] = g
        else:
            rider = _all_gather(group, name="all_gather_weights", host=host)
            for pos, i in enumerate(idx):
                sources[i] = (rider, pos)
    return sources


def _ssm_params(p):
    prep1_in = (p["ssm_lambda_re"], p["ssm_lambda_im"], p["ssm_log_dt"][:, None])
    ar, ai, gr, gi = _whole(_ssm_prep1, *prep1_in, name="ssm_prep1")
    to_cn = lambda b: b.transpose(2, 0, 1).reshape(SSM_GROUP, N_SSM_CH)
    prep2_in = (gr.reshape(1, N_SSM_CH), gi.reshape(1, N_SSM_CH), to_cn(p["ssm_b_re"]), to_cn(p["ssm_b_im"]))
    bbr, bbi = _whole(_ssm_prep2, *prep2_in, name="ssm_prep2")
    to_gcp = lambda t: t.reshape(SSM_GROUP, SSM_GROUPS, SSM_STATE).transpose(1, 0, 2)
    bb = _blockdiag(to_gcp(bbr), to_gcp(bbi))
    cct = _blockdiag(p["ssm_c_re"], -p["ssm_c_im"])
    a_vec = _to_blocked(ar.reshape(1, N_SSM_CH), ai.reshape(1, N_SSM_CH))
    return dict(bb=bb, cct=cct, a_vec=a_vec, prep1_in=prep1_in, prep2_in=prep2_in)


def _layer_fwd(x, mem, p, W):
    sp = _ssm_params(p)
    z = _mm(x, W["w_in_p"], bias=p["b_in_p"], name="mm_in")
    f_t = z[:, Z_FF:Z_FF + FOX_HEADS].T
    cum = _cum_heads(f_t, name="fox_cum")
    ya_pre, lse_a = _attn_fwd("fox", (z, Z_FQ), (z, Z_FK), (z, Z_FV), (cum[:, :, None], cum[:, None, :]),
                              name="fox_fwd")
    ya = _mm(ya_pre, W["w_fox_o"], b_slots=True, name="mm_fox_o")
    x_ri = _mm_bd("in", z, sp["bb"], a_off=Z_SU, name="mm_s5_in")
    h_ri = _scan(x_ri, sp["a_vec"], name="s5_scan")
    hc = _mm_bd("out", h_ri, sp["cct"], name="mm_s5_out")
    d_row = p["ssm_d"][None, :]
    (gel,) = _rowwise(lambda a, b, c: (_s5_tail(a, b, c),), [hc, Win(z, 512, Z_SU // 512)], [d_row], name="s5_tail")
    yb2 = _mm(gel, W["w_ssm_glu"], b_slots=True, name="mm_glu")
    bias = _relbias_expand(W["ca_rel_bias"]).transpose(1, 0, 2)
    kv_band = jnp.pad(z[:, Z_CK:Z_CK + 2 * CA_WIDTH].astype(BF16), ((CA_PAD, 0), (0, 0)))
    yc_pre, lse_c = _attn_fwd("chunk", (z, Z_CQ), (kv_band, 0), (kv_band, CA_WIDTH), (bias,), name="chunk_fwd")
    yc = _mm(yc_pre, W["w_ca_o"], b_slots=True, name="mm_ca_o")
    gates = [Win(z, 1024, Z_GF // 1024), Win(z, 1024, Z_GS // 1024), Win(z, 1024, Z_GC // 1024)]
    (merged,) = _rowwise(lambda *a: (_merge(*a),), gates + [ya, yb2, yc], name="merge")
    h1 = _mm(merged, W["w_o"], name="mm_o")
    ln_g, ln_b = W["ln_g"], W["ln_b"]
    r1, x1 = _rowwise(_ln_fwd, [x, h1], [ln_g[0:1], ln_b[0:1]], name="ln_fwd")
    q = _mm(x1, W["xa_wq"], name="mm_xq")
    kv = _mm(mem, W["xa_wkv"], b_slots=True, name="mm_xkv")
    o, lse_x = _attn_fwd("xa", (q, 0), (kv, 0), (kv, D_MODEL), name="xa_fwd")
    h2 = _mm(o, W["xa_wo"], name="mm_xo")
    r2, x2 = _rowwise(_ln_fwd, [x1, h2], [ln_g[1:2], ln_b[1:2]], name="ln_fwd")
    up = _mm(x2, W["ffn_w_up"], b_slots=True, name="mm_up")
    hmid = _ffn_mid(up, _ff_pad(W["ffn_conv_w"]), _ff_pad(p["ffn_conv_b"][None, :]), name="ffn_mid")
    h3 = _mm(hmid, W["ffn_w_down"], name="mm_down")
    r3, x3 = _rowwise(_ln_fwd, [x2, h3], [ln_g[2:3], ln_b[2:3]], name="ln_fwd")
    res = dict(x=x, z=z, cum=cum, lse_a=lse_a, ya_pre=ya_pre, ya=ya, h_ri=h_ri, hc=hc, gel=gel, yb2=yb2, lse_c=lse_c,
               yc_pre=yc_pre, yc=yc, merged=merged, r1=r1, x1=x1, q=q, kv=kv, o=o, lse_x=lse_x, r2=r2, x2=x2, up=up,
               hmid=hmid, r3=r3, bias=bias, sp=sp, kv_band=kv_band, W=W)
    return x3, res


def _layer_bwd(dx3, mem, p, res):
    W = res["W"]
    x, z = res["x"], res["z"]
    sp = res["sp"]
    ln_g = W["ln_g"]
    big, small = {}, {}
    slots = lambda t: t.reshape(N_DEV, -1, D_MODEL)
    dr3, dg2, db2 = _rowwise(_ln_bwd, [res["r3"], dx3], [ln_g[2:3]], n_red=2, name="ln_bwd")
    dhmid = _mm(dr3, W["ffn_w_down"], tb=True, name="mm_down_dx")
    big["ffn_w_down"] = slots(_mm(res["hmid"], dr3, ta=True, name="mm_down_dw"))
    conv_w_p, conv_b_p = _ff_pad(W["ffn_conv_w"]), _ff_pad(p["ffn_conv_b"][None, :])
    dup_a, dup_g, dcw, dcb = _ffn_mid(res["up"], conv_w_p, conv_b_p, dhmid, name="ffn_mid_bwd")
    dup = jnp.concatenate([dup_a, dup_g], axis=1)
    small["ffn_conv_w"], small["ffn_conv_b"] = _ff_unpad(dcw), _ff_unpad(dcb)[0]
    dx2 = _mm(dup, W["ffn_w_up"], tb=True, b_slots=True, add=(dr3, DN_ALPHA), name="mm_up_dx")
    big["ffn_w_up"] = _mm(res["x2"], dup, ta=True, out_slots=2 * FF_HALF_P, name="mm_up_dw")
    dr2, dg1, db1 = _rowwise(_ln_bwd, [res["r2"], dx2], [ln_g[1:2]], n_red=2, name="ln_bwd")
    do = _mm(dr2, W["xa_wo"], tb=True, name="mm_xo_dx")
    big["xa_wo"] = slots(_mm(res["o"], dr2, ta=True, name="mm_xo_dw"))
    kv = res["kv"]
    dq, dk, dv = _attn_bwd("xa", (res["q"], 0), (kv, 0), (kv, D_MODEL), res["lse_x"], (do, 0), name="xa_bwd")
    dkv = jnp.concatenate([dk, dv], axis=1)
    dx1 = _mm(dq, W["xa_wq"], tb=True, add=(dr2, DN_ALPHA), name="mm_xq_dx")
    big["xa_wq"] = slots(_mm(res["x1"], dq, ta=True, name="mm_xq_dw"))
    big["xa_wkv"] = _mm(mem, dkv, ta=True, out_slots=256, name="mm_xkv_dw")
    dr1, dg0, db0 = _rowwise(_ln_bwd, [res["r1"], dx1], [ln_g[0:1]], n_red=2, name="ln_bwd")
    small["ln_g"] = jnp.concatenate([dg0, dg1, dg2], axis=0)
    small["ln_b"] = jnp.concatenate([db0, db1, db2], axis=0)
    dmerged = _mm(dr1, W["w_o"], tb=True, name="mm_o_dx")
    big["w_o"] = slots(_mm(res["merged"], dr1, ta=True, name="mm_o_dw"))
    gates = [Win(z, 1024, Z_GF // 1024), Win(z, 1024, Z_GS // 1024), Win(z, 1024, Z_GC // 1024)]
    dgf, dgs, dgc, dya, dyb2, dyc = _rowwise(_vjp_of(_merge, 6), gates + [res["ya"], res["yb2"], res["yc"], dmerged],
                                             name="merge_bwd")
    dya_pre = _mm(dya, W["w_fox_o"], tb=True, b_slots=True, name="mm_fox_o_dx")
    big["w_fox_o"] = _mm(res["ya_pre"], dya, ta=True, out_slots=128, name="mm_fox_o_dw")
    cum = res["cum"]
    dfq, dfk, dfv, dcq, dck = _attn_bwd("fox", (z, Z_FQ), (z, Z_FK), (z, Z_FV), res["lse_a"], (dya_pre, 0),
                                        (cum[:, :, None], cum[:, None, :]), name="fox_bwd")
    f_t = z[:, Z_FF:Z_FF + FOX_HEADS].T
    dff = _cum_heads(f_t, dcq[:, :, 0] + dck[:, 0, :], name="fox_cum_bwd")
    dgel = _mm(dyb2, W["w_ssm_glu"], tb=True, b_slots=True, name="mm_glu_dx")
    big["w_ssm_glu"] = _mm(res["gel"], dyb2, ta=True, out_slots=256, name="mm_glu_dw")
    d_row = p["ssm_d"][None, :]
    su_win = Win(z, 512, Z_SU // 512)
    dy, dsu1, dd = _rowwise(_s5_tail_bwd, [res["hc"], su_win, dgel], [d_row], n_red=1, name="s5_tail_bwd")
    small["ssm_d"] = dd[0]
    dh_ri = _mm_bd("in", dy, sp["cct"], name="mm_s5_out_dx")
    dcct = _mm_bd("dw", dy, res["h_ri"], name="mm_s5_out_dw")
    dx_ri, da_vec = _scan(dh_ri, sp["a_vec"], res["h_ri"], name="s5_scan_bwd")
    dsu = _mm_bd("out", dx_ri, sp["bb"], add=(dsu1, 1.0), name="mm_s5_in_dx")
    dbb = _mm_bd("dw", z, dx_ri, a_off=Z_SU, name="mm_s5_in_dw")
    dcr, dci = _blockdiag_inv(dcct)
    small["ssm_c_re"], small["ssm_c_im"] = dcr, -dci
    dbbr, dbbi = _blockdiag_inv(dbb)
    to_cn = lambda t: t.transpose(1, 0, 2).reshape(SSM_GROUP, N_SSM_CH)
    dgr, dgi, dbr, dbi = _whole(_vjp_of(_ssm_prep2, 4), *sp["prep2_in"], to_cn(dbbr), to_cn(dbbi), name="ssm_prep2_bwd")
    from_cn = lambda t: t.reshape(SSM_GROUP, SSM_GROUPS, SSM_STATE).transpose(1, 2, 0)
    small["ssm_b_re"], small["ssm_b_im"] = from_cn(dbr), from_cn(dbi)
    dar, dai = _from_blocked(da_vec)
    sq = lambda t: t.reshape(SSM_GROUPS, SSM_STATE)
    dlr, dli, dldt = _whole(_vjp_of(_ssm_prep1, 3), *sp["prep1_in"], sq(dar), sq(dai), sq(dgr), sq(dgi),
                            name="ssm_prep1_bwd")
    small["ssm_lambda_re"], small["ssm_lambda_im"], small["ssm_log_dt"] = dlr, dli, dldt[:, 0]
    dyc_pre = _mm(dyc, W["w_ca_o"], tb=True, b_slots=True, name="mm_ca_o_dx")
    big["w_ca_o"] = _mm(res["yc_pre"], dyc, ta=True, out_slots=128, name="mm_ca_o_dw")
    bias = res["bias"]
    kv_band = res["kv_band"]
    dcq_, dck_band, dcv_band, dbias = _attn_bwd("chunk", (z, Z_CQ), (kv_band, 0), (kv_band, CA_WIDTH), res["lse_c"],
                                                (dyc_pre, 0), (bias,), name="chunk_bwd")
    small["ca_rel_bias"] = _relbias_reduce(dbias.transpose(1, 0, 2))
    dff_p = jnp.pad(dff.T, ((0, 0), (0, 512 - FOX_HEADS)))
    dz = jnp.concatenate([dfq, dfk, dfv, dff_p, dsu, dcq_, dck_band[CA_PAD:], dcv_band[CA_PAD:], dgf, dgs, dgc],
                         axis=1)
    dx = _mm(dz, W["w_in_p"], tb=True, add=(dr1, DN_ALPHA), name="mm_in_dx")
    big["w_in"] = _colmap(_mm(x, dz, ta=True, name="mm_in_dw"), inverse=True, name="w_in_colmap_inv")
    (db_in_p,) = _rowwise(lambda t: (jnp.sum(t, axis=0, keepdims=True),), [dz], n_red=1, name="colsum")
    small["b_in"] = _unpad_b_in(db_in_p)[0]
    return dx, big, small


def kernel(x, mem, w_in, b_in, ssm_lambda_re, ssm_lambda_im, ssm_log_dt, ssm_b_re, ssm_b_im, ssm_c_re, ssm_c_im, ssm_d, ca_rel_bias, w_fox_o, w_ssm_glu, w_ca_o, w_o, xa_wq, xa_wkv, xa_wo, ffn_w_up, ffn_conv_w, ffn_conv_b, ffn_w_down, ln_g, ln_b, loss_target, m_w_in, m_b_in, m_ssm_lambda_re, m_ssm_lambda_im, m_ssm_log_dt, m_ssm_b_re, m_ssm_b_im, m_ssm_c_re, m_ssm_c_im, m_ssm_d, m_ca_rel_bias, m_w_fox_o, m_w_ssm_glu, m_w_ca_o, m_w_o, m_xa_wq, m_xa_wkv, m_xa_wo, m_ffn_w_up, m_ffn_conv_w, m_ffn_conv_b, m_ffn_w_down, m_ln_g, m_ln_b, v_w_in, v_b_in, v_ssm_lambda_re, v_ssm_lambda_im, v_ssm_log_dt, v_ssm_b_re, v_ssm_b_im, v_ssm_c_re, v_ssm_c_im, v_ssm_d, v_ca_rel_bias, v_w_fox_o, v_w_ssm_glu, v_w_ca_o, v_w_o, v_xa_wq, v_xa_wkv, v_xa_wo, v_ffn_w_up, v_ffn_conv_w, v_ffn_conv_b, v_ffn_w_down, v_ln_g, v_ln_b):
    given = dict(locals())
    state = {pre + n: given[pre + n] for n in WEIGHTS for pre in ("", "m_", "v_")}
    mem0 = mem[0]
    b_in_p = _pad_b_in(b_in)
    layer_params = [{**{n: state[n][l] for n in REPL}, "b_in_p": b_in_p[l:l + 1]} for l in range(DEPTH)]

    _RIDERS.clear()
    h, residuals = x[0], []
    sources = _gather_riding(_layer_shards(state, 0), FIRST_GATHER_HOSTS)
    for l in range(DEPTH):
        hosts = SECOND_GATHER_HOSTS if l == 0 else GATHER_HOSTS
        following = _gather_riding(_layer_shards(state, l + 1), hosts) if l + 1 < DEPTH else None
        h, res = _layer_fwd(h, mem0, layer_params[l], _Weights(sources))
        residuals.append(res)
        sources = following
    dh, loss_cols = _rowwise(_loss_rows, [h, loss_target[0]], n_red=1, name="loss")
    loss = lax.psum(jnp.sum(loss_cols), ("x", "y", "c"))

    outs = [None] * DEPTH
    big_out = {n: None for n in BIG}
    padded = {pre + n: _pad_shard(n, state[pre + n]) for n in BIG for pre in ("", "m_", "v_")}
    wire = [BF16] * len(BIG) + [F32]
    pending = None
    for l in reversed(range(-1, DEPTH)):
        if l >= 0:
            dh, big, small = _layer_bwd(dh, mem0, layer_params[l], residuals[l])
            reduce = _GradReduce([big[n] for n in BIG] + [_pack_small_grads(small)], wire,
                                 hosts=REDUCE_HOSTS if l > 0 else None)
        done, pending = pending, (l, reduce) if l >= 0 else None
        if done is None:
            continue
        l_done, reduce_done = done
        *recv_big, recv_small = reduce_done.result()
        for n, recv in zip(BIG, recv_big):
            if big_out[n] is None:
                big_out[n] = [lax.empty((DEPTH,) + recv.shape[1:], F32) for _ in range(4)]
            big_out[n] = _adamw(recv, *[padded[pre + n] for pre in ("", "m_", "v_")], l_done, big_out[n],
                                name="adamw_" + n)
        outs[l_done] = recv_small

    assert not _RIDERS, [r.host for r in _RIDERS]
    packed = _adamw(jnp.concatenate(outs, axis=1), *[_pack_state(state, pre) for pre in ("", "m_", "v_")],
                    name="adamw_small")
    small_out = [_unpack_state(t) for t in packed]
    result = lambda n, j: _unpad_shard(n, big_out[n][j]) if n in BIG else small_out[j][n]
    return (loss, dh[None], *[result(n, j) for j in range(4) for n in WEIGHTS])
```

```python
import functools
import math

import jax
import jax.numpy as jnp
from jax import lax
from jax.experimental import pallas as pl
from jax.experimental.pallas import tpu as pltpu

F32, BF16 = jnp.float32, jnp.bfloat16

D_MODEL = 1024
DEPTH = 4
CHUNK = 64
FOX_HEADS, FOX_HEAD_DIM, FOX_WIDTH = 8, 64, 512
SSM_GROUP, SSM_WIDTH, SSM_GROUPS, SSM_STATE = 16, 512, 32, 64
CA_HEADS, CA_HEAD_DIM, CA_WIDTH, CA_LEFT_CHUNKS = 8, 64, 512, 8
CA_BAND = (CA_LEFT_CHUNKS + 1) * CHUNK
CA_PAD = CA_LEFT_CHUNKS * CHUNK
REL_MIN, REL_MAX = -(CHUNK - 1), 4 * CHUNK
N_REL = REL_MAX - REL_MIN + 1
XA_HEADS, XA_HEAD_DIM = 4, 256
D_FF = 2816
DN_ALPHA = (2 * DEPTH) ** 0.25
LN_EPS = 1e-5
NEG_INF = -1e30
ADAM_LR, ADAM_B1, ADAM_B2, ADAM_EPS, ADAM_WD, ADAM_STEP = 0.001, 0.9, 0.999, 1e-08, 0.01, 10

N_DEV = 8
LANE = 128
N_SSM_CH = SSM_GROUPS * SSM_STATE
SCAN_CB = 256
N_IN = 6664
W_IN_SHARD, W_IN_SLOT = N_IN // N_DEV, 896
Z_W = 7168
Z_FQ, Z_FK, Z_FV, Z_FF, Z_SU, Z_CQ, Z_CK, Z_CV, Z_GF, Z_GS, Z_GC = (
    0, 512, 1024, 1536, 2048, 2560, 3072, 3584, 4096, 5120, 6144)
Z_PIECES = ((0, 512, Z_FQ), (512, 512, Z_FK), (1024, 512, Z_FV), (1536, 8, Z_FF), (1544, 512, Z_SU),
            (2056, 512, Z_CQ), (2568, 512, Z_CK), (3080, 512, Z_CV), (3592, 1024, Z_GF), (4616, 1024, Z_GS),
            (5640, 1024, Z_GC))
FF_HALF, FF_HALF_P = D_FF // N_DEV, 384
D_FF_P = N_DEV * FF_HALF_P

VMEM_LIMIT_BYTES = 56 * 1024 * 1024

BIG = {"w_in": ((1024, W_IN_SHARD), 1), "w_fox_o": ((512, 128), 1), "w_ssm_glu": ((512, 256), 1),
       "w_ca_o": ((512, 128), 1), "w_o": ((128, 1024), 0), "xa_wq": ((128, 1024), 0), "xa_wkv": ((1024, 256), 1),
       "xa_wo": ((128, 1024), 0), "ffn_w_up": ((1024, 2 * FF_HALF), 1), "ffn_w_down": ((FF_HALF, 1024), 0)}
SMALL = {"ca_rel_bias": (8, 320), "ffn_conv_w": (3, D_FF), "ln_g": (3, 1024), "ln_b": (3, 1024)}
REPL = {"b_in": (N_IN,), "ssm_lambda_re": (32, 64), "ssm_lambda_im": (32, 64), "ssm_log_dt": (32,),
        "ssm_b_re": (32, 64, 16), "ssm_b_im": (32, 64, 16), "ssm_c_re": (32, 16, 64), "ssm_c_im": (32, 16, 64),
        "ssm_d": (512,), "ffn_conv_b": (D_FF,)}
WEIGHTS = ("w_in", "b_in", "ssm_lambda_re", "ssm_lambda_im", "ssm_log_dt", "ssm_b_re", "ssm_b_im", "ssm_c_re",
           "ssm_c_im", "ssm_d", "ca_rel_bias", "w_fox_o", "w_ssm_glu", "w_ca_o", "w_o", "xa_wq", "xa_wkv", "xa_wo",
           "ffn_w_up", "ffn_conv_w", "ffn_conv_b", "ffn_w_down", "ln_g", "ln_b")
LANES = 1024
PACK_ROWS = 256
SMALL_ROWS = 8


def _w_in_segments():
    segs = []
    for src, width, dst in Z_PIECES:
        n = src
        while n < src + width:
            d = n // W_IN_SHARD
            end = min(src + width, (d + 1) * W_IN_SHARD)
            segs.append((W_IN_SLOT * d + n - W_IN_SHARD * d, dst + n - src, end - n))
            n = end
    return tuple(segs)


W_IN_SEGS = _w_in_segments()


def _pallas(body, **kw):
    return pl.pallas_call(body, **kw)


def _params(sem):
    return pltpu.CompilerParams(dimension_semantics=sem, vmem_limit_bytes=VMEM_LIMIT_BYTES)


class _Rider:
    def __init__(self, host, ins, out_shapes, sem_shapes, start, finish, then=None):
        self.host, self.ins, self.out_shapes, self.sem_shapes = host, list(ins), list(out_shapes), list(sem_shapes)
        self.start, self.finish, self.then, self.results = start, finish, then, None


_RIDERS = []


def _pcall(body, *, name, **kw):
    rider = next((r for r in _RIDERS if r.host == name), None)
    if rider is None:
        return _pallas(body, name=name, **kw)
    _RIDERS.remove(rider)
    grid, in_specs, scratch = kw["grid"], list(kw["in_specs"]), list(kw.get("scratch_shapes", ()))
    single = not isinstance(kw["out_shape"], (list, tuple))
    out_specs = [kw["out_specs"]] if single else list(kw["out_specs"])
    out_shape = [kw["out_shape"]] if single else list(kw["out_shape"])
    n_in, n_out, n_scr = len(in_specs), len(out_specs), len(scratch)
    r_in, r_out = len(rider.ins), len(rider.out_shapes)

    def fused(*refs):
        a, ra = refs[:n_in], refs[n_in:n_in + r_in]
        o, ro = refs[n_in + r_in:n_in + r_in + n_out], refs[n_in + r_in + n_out:n_in + r_in + n_out + r_out]
        scr, sems = refs[n_in + r_in + n_out + r_out:][:n_scr], refs[n_in + r_in + n_out + r_out + n_scr:]
        ids = [pl.program_id(d) for d in range(len(grid))]
        first = functools.reduce(jnp.logical_and, [i == 0 for i in ids])
        last = functools.reduce(jnp.logical_and, [i == g - 1 for i, g in zip(ids, grid)])

        @pl.when(first)
        def _():
            rider.start(ra, ro, sems)

        body(*a, *o, *scr)

        @pl.when(last)
        def _():
            rider.finish(ra, ro, sems)

    hbm = pl.BlockSpec(memory_space=pl.ANY)
    call = _pallas(fused, name=name, grid=grid, in_specs=in_specs + [hbm] * r_in, out_specs=out_specs + [hbm] * r_out,
                   out_shape=out_shape + rider.out_shapes, scratch_shapes=scratch + rider.sem_shapes,
                   compiler_params=_params(("arbitrary",) * len(grid)))

    def run(*operands):
        outs = call(*operands, *rider.ins)
        rider.results = list(outs[n_out:])
        if rider.then is not None:
            rider.then(rider.results)
        return outs[0] if single else list(outs[:n_out])

    return run


def _pick(dim, prefs):
    for p in prefs:
        if dim % p == 0:
            return p
    return dim


def _mm(a, b, *, ta=False, tb=False, bias=None, add=None, a_off=0, a_cols=None, b_slots=False, out_slots=None,
        name, out_dtype=F32):
    a_cols = a_cols if a_cols is not None else a.shape[1]
    M, K = (a_cols, a.shape[0]) if ta else (a.shape[0], a_cols)
    tm = _pick(M, (1024, 512, 256, 128))
    if ta:
        tk = _pick(K, (2048, 1024, 512, 256))
    elif b_slots and tb:
        tk = _pick(b.shape[2], (1024, 768, 512, 256, 128))
    else:
        tk = K if K <= 3072 else _pick(K, (1024, 512, 256, 128))
    nk = K // tk
    if b_slots:
        ns = b.shape[2]
        if tb:
            N = b.shape[1]
            tn = _pick(N, (512, 256, 128))
            per = ns // tk
            b_spec = pl.BlockSpec((None, tn, tk), lambda i, j, k: (k // per, j, k % per))
            b_dim = 1
            assert N_DEV * ns == K
        else:
            N = N_DEV * ns
            tn = _pick(ns, (768, 512, 256, 128))
            per = ns // tn
            b_spec = pl.BlockSpec((None, tk, tn), lambda i, j, k: (j // per, k, j % per))
            b_dim = 0
            assert b.shape[1] == K
    else:
        N = b.shape[0] if tb else b.shape[1]
        assert (b.shape[1] if tb else b.shape[0]) == K, (a.shape, b.shape, ta, tb)
        tn = _pick(N, (512, 256, 128)) if out_slots is None else _pick(out_slots, (768, 512, 256, 128))
        if tb:
            b_spec = pl.BlockSpec((tn, tk), lambda i, j, k: (j, k))
            b_dim = 1
        else:
            b_spec = pl.BlockSpec((tk, tn), lambda i, j, k: (k, j))
            b_dim = 0
    if ta:
        assert a_off % tm == 0
        a_spec = pl.BlockSpec((tk, tm), lambda i, j, k: (k, i + a_off // tm))
        a_dim = 0
    else:
        assert a_off % tk == 0
        a_spec = pl.BlockSpec((tm, tk), lambda i, j, k: (i, k + a_off // tk))
        a_dim = 1
    cache_at = ta and nk == 1
    dims = (((1 if cache_at else a_dim,), (b_dim,)), ((), ()))
    ins, specs = [a, b], [a_spec, b_spec]
    if bias is not None:
        ins.append(bias)
        specs.append(pl.BlockSpec((1, tn), lambda i, j, k: (0, j)))
    add_scale = None
    if add is not None:
        ins.append(add[0])
        add_scale = add[1]
        specs.append(pl.BlockSpec((tm, tn), lambda i, j, k: (i, j)))
    if out_slots is None:
        out_spec = pl.BlockSpec((tm, tn), lambda i, j, k: (i, j))
        out_shape = jax.ShapeDtypeStruct((M, N), out_dtype)
    else:
        assert N == N_DEV * out_slots
        per_o = out_slots // tn
        out_spec = pl.BlockSpec((None, tm, tn), lambda i, j, k: (j // per_o, i, j % per_o))
        out_shape = jax.ShapeDtypeStruct((N_DEV, M, out_slots), out_dtype)

    def body(*refs):
        a_ref, b_ref = refs[0], refs[1]
        pos = 2
        bias_ref = add_ref = None
        if bias is not None:
            bias_ref = refs[pos]
            pos += 1
        if add is not None:
            add_ref = refs[pos]
            pos += 1
        o_ref = refs[pos]
        acc_ref = refs[pos + 1] if nk > 1 else None
        if cache_at:
            at_ref = refs[pos + 1]

            @pl.when(pl.program_id(1) == 0)
            def _():
                step = min(tk, 256)
                for c in range(0, tk, step):
                    at_ref[:, c:c + step] = a_ref[c:c + step, :].T.astype(BF16)

            lhs = at_ref[...]
        else:
            lhs = a_ref[...].astype(BF16)
        part = lax.dot_general(lhs, b_ref[...].astype(BF16), dims, preferred_element_type=F32)

        def finish(acc):
            if bias_ref is not None:
                acc = acc + bias_ref[...]
            if add_ref is not None:
                acc = acc + add_scale * add_ref[...]
            o_ref[...] = acc.astype(out_dtype)

        if nk == 1:
            finish(part)
        else:
            k = pl.program_id(2)

            @pl.when(k == 0)
            def _():
                acc_ref[...] = part

            @pl.when(k > 0)
            def _():
                acc_ref[...] += part

            @pl.when(k == nk - 1)
            def _():
                finish(acc_ref[...])

    return _pcall(
        body, name=name, grid=(M // tm, N // tn, nk), in_specs=specs, out_specs=out_spec, out_shape=out_shape,
        scratch_shapes=[pltpu.VMEM((tm, tn), F32)] if nk > 1 else [pltpu.VMEM((tm, tk), BF16)] if cache_at else [],
        compiler_params=_params(("parallel", "arbitrary", "arbitrary")),
    )(*ins)


SSM_BD = 4
SSM_BD_IN, SSM_BD_ST = SSM_WIDTH // SSM_BD, 2 * N_SSM_CH // SSM_BD


def _mm_bd(form, a, b, *, a_off=0, add=None, name):
    S = a.shape[0]
    off = a_off // SSM_BD_IN
    if form == "dw":
        def body(a_ref, b_ref, o_ref):
            at = a_ref[...].T.astype(BF16)
            o_ref[...] = lax.dot_general(at, b_ref[...].astype(BF16), _NN, preferred_element_type=F32)

        return _pcall(body, name=name, grid=(SSM_BD,),
                      in_specs=[pl.BlockSpec((S, SSM_BD_IN), lambda j: (0, off + j)),
                                pl.BlockSpec((S, SSM_BD_ST), lambda j: (0, j))],
                      out_specs=pl.BlockSpec((None, SSM_BD_IN, SSM_BD_ST), lambda j: (j, 0, 0)),
                      out_shape=jax.ShapeDtypeStruct((SSM_BD, SSM_BD_IN, SSM_BD_ST), F32),
                      compiler_params=_params(("parallel",)))(a, b)
    tm = _pick(S, (1024, 512, 256))
    wide, narrow = (SSM_BD_ST, SSM_BD_IN) if form == "in" else (SSM_BD_IN, SSM_BD_ST)
    dims = _NN if form == "in" else _NT

    def body(a_ref, w_ref, *rest):
        acc = lax.dot_general(a_ref[...].astype(BF16), w_ref[...].astype(BF16), dims, preferred_element_type=F32)
        if add is not None:
            acc = acc + add[1] * rest[0][...]
        rest[-1][...] = acc

    specs = [pl.BlockSpec((tm, narrow), lambda i, j: (i, off + j)),
             pl.BlockSpec((None, SSM_BD_IN, SSM_BD_ST), lambda i, j: (j, 0, 0))]
    ins = [a, b]
    if add is not None:
        specs.append(pl.BlockSpec((tm, wide), lambda i, j: (i, j)))
        ins.append(add[0])
    return _pcall(body, name=name, grid=(S // tm, SSM_BD), in_specs=specs,
                  out_specs=pl.BlockSpec((tm, wide), lambda i, j: (i, j)),
                  out_shape=jax.ShapeDtypeStruct((S, SSM_BD * wide), F32),
                  compiler_params=_params(("parallel", "parallel")))(*ins)


class Win:
    def __init__(self, arr, width, blk):
        self.arr, self.width, self.blk = arr, width, blk


def _rowwise(fn, rows, vecs=(), *, n_red=0, tr=256, name):
    wins = [r if isinstance(r, Win) else Win(r, r.shape[1], 0) for r in rows]
    S = wins[0].arr.shape[0]
    tr = min(tr, S)
    tile_args = [jax.ShapeDtypeStruct((tr, w.width), w.arr.dtype) for w in wins]
    tile_args += [jax.ShapeDtypeStruct(v.shape, v.dtype) for v in vecs]
    outs = jax.eval_shape(fn, *tile_args)
    n_row = len(outs) - n_red
    specs = [pl.BlockSpec((tr, w.width), functools.partial(lambda i, b: (i, b), b=w.blk)) for w in wins]
    specs += [pl.BlockSpec(v.shape, functools.partial(lambda i, nd: (0,) * nd, nd=v.ndim)) for v in vecs]
    out_specs = [pl.BlockSpec((tr, o.shape[1]), lambda i: (i, 0)) for o in outs[:n_row]]
    out_specs += [pl.BlockSpec(o.shape, functools.partial(lambda i, nd: (0,) * nd, nd=len(o.shape))) for o in outs[n_row:]]
    out_shape = [jax.ShapeDtypeStruct((S, o.shape[1]), o.dtype) for o in outs[:n_row]]
    out_shape += [jax.ShapeDtypeStruct(o.shape, o.dtype) for o in outs[n_row:]]
    n_in = len(wins) + len(vecs)

    def body(*refs):
        res = fn(*[r[...] for r in refs[:n_in]])
        o_refs = refs[n_in:]
        for o_ref, r in zip(o_refs[:n_row], res[:n_row]):
            o_ref[...] = r.astype(o_ref.dtype)
        i = pl.program_id(0)
        for o_ref, r in zip(o_refs[n_row:], res[n_row:]):
            @pl.when(i == 0)
            def _(o_ref=o_ref, r=r):
                o_ref[...] = r

            @pl.when(i > 0)
            def _(o_ref=o_ref, r=r):
                o_ref[...] += r

    return _pcall(
        body, name=name, grid=(S // tr,), in_specs=specs, out_specs=out_specs, out_shape=out_shape,
        compiler_params=_params(("arbitrary",)),
    )(*[w.arr for w in wins], *vecs)


def _whole(fn, *arrays, name):
    outs = jax.eval_shape(fn, *arrays)
    n_in = len(arrays)

    def body(*refs):
        res = fn(*[r[...] for r in refs[:n_in]])
        for o_ref, r in zip(refs[n_in:], res):
            o_ref[...] = r

    vm = pl.BlockSpec(memory_space=pltpu.VMEM)
    return _pcall(body, name=name, in_specs=[vm] * n_in, out_specs=[vm] * len(outs),
                  out_shape=[jax.ShapeDtypeStruct(o.shape, o.dtype) for o in outs])(*arrays)


def _split3(x):
    hi = x.astype(BF16)
    r = x - hi.astype(F32)
    mid = r.astype(BF16)
    lo = (r - mid.astype(F32)).astype(BF16)
    return hi, mid, lo


def _dot3(x, onehot, dims):
    return sum(lax.dot_general(t, onehot, dims, preferred_element_type=F32) for t in _split3(x))


_NT = (((1,), (1,)), ((), ()))
_NN = (((1,), (0,)), ((), ()))
_TN = (((0,), (0,)), ((), ()))


def _colmap(x, *, inverse, name):
    R = x.shape[0] if inverse else x.shape[1]
    tr = 256
    per = W_IN_SLOT // LANE
    n_out = N_DEV * per if inverse else Z_W // LANE
    segs = [(p, q, n) for q, p, n in W_IN_SEGS] if inverse else list(W_IN_SEGS)

    def body(x_ref, o_ref):
        ia = lax.broadcasted_iota(jnp.int32, (LANE, LANE), 0)
        ib = lax.broadcasted_iota(jnp.int32, (LANE, LANE), 1)

        def src_block(i):
            if inverse:
                return x_ref[:, i * LANE:(i + 1) * LANE]
            return x_ref[i // per, :, (i % per) * LANE:(i % per + 1) * LANE]

        for jb in range(n_out):
            acc = None
            for s0, d0, n in segs:
                lo, hi = max(d0, jb * LANE), min(d0 + n, (jb + 1) * LANE)
                if lo >= hi:
                    continue
                delta = d0 - s0
                for i in range((lo - delta) // LANE, (hi - delta - 1) // LANE + 1):
                    shift = jb * LANE - i * LANE - delta
                    sel = ((ia - ib == shift) & (ib >= lo - jb * LANE) & (ib < hi - jb * LANE)).astype(BF16)
                    blk = src_block(i)
                    part = _dot3(blk, sel, _NN) if inverse else lax.dot_general(blk, sel, _NN, preferred_element_type=F32)
                    acc = part if acc is None else acc + part
            if acc is None:
                acc = jnp.zeros((tr, LANE), F32)
            if inverse:
                o_ref[jb // per, :, (jb % per) * LANE:(jb % per + 1) * LANE] = acc
            else:
                o_ref[:, jb * LANE:(jb + 1) * LANE] = acc.astype(BF16)

    slot_spec = pl.BlockSpec((N_DEV, tr, W_IN_SLOT), lambda i: (0, i, 0))
    flat_spec = pl.BlockSpec((tr, Z_W), lambda i: (i, 0))
    if inverse:
        return _pcall(body, name=name, grid=(R // tr,), in_specs=[flat_spec], out_specs=slot_spec,
                      out_shape=jax.ShapeDtypeStruct((N_DEV, R, W_IN_SLOT), F32), compiler_params=_params(("parallel",)))(x)
    return _pcall(body, name=name, grid=(R // tr,), in_specs=[slot_spec], out_specs=flat_spec,
                  out_shape=jax.ShapeDtypeStruct((R, Z_W), BF16), compiler_params=_params(("parallel",)))(x)


def _log_sigmoid(x):
    return jnp.minimum(x, 0.0) - jnp.log(1.0 + jnp.exp(-jnp.abs(x)))


def _cum_heads(f, dcum=None, *, name):
    H, S = f.shape
    tn = min(512, S)
    rev = dcum is not None

    def body(*refs):
        j = pl.program_id(0)
        s_idx = lax.broadcasted_iota(jnp.int32, (S, tn), 0)
        t_idx = lax.broadcasted_iota(jnp.int32, (S, tn), 1) + j * tn
        if not rev:
            f_ref, o_ref = refs
            tri = (s_idx <= t_idx).astype(BF16)
            o_ref[...] = _dot3(_log_sigmoid(f_ref[...]), tri, _NN)
        else:
            fj_ref, d_ref, o_ref = refs
            tri = (s_idx >= t_idx).astype(BF16)
            o_ref[...] = _dot3(d_ref[...], tri, _NN) * jax.nn.sigmoid(-fj_ref[...])

    full = pl.BlockSpec((H, S), lambda j: (0, 0))
    blk = pl.BlockSpec((H, tn), lambda j: (0, j))
    ins, specs = ([f], [full]) if not rev else ([f, dcum], [blk, full])
    return _pcall(body, name=name, grid=(S // tn,), in_specs=specs, out_specs=blk,
                  out_shape=jax.ShapeDtypeStruct((H, S), F32), compiler_params=_params(("arbitrary",)))(*ins)


def _rel_onehot(qi, transposed):
    shape = (N_REL, CA_BAND) if transposed else (CA_BAND, N_REL)
    kk = lax.broadcasted_iota(jnp.int32, shape, 1 if transposed else 0)
    rr = lax.broadcasted_iota(jnp.int32, shape, 0 if transposed else 1)
    idx = jnp.clip(CA_PAD + qi - kk, REL_MIN, REL_MAX) - REL_MIN
    return (idx == rr).astype(BF16)


def _relbias_expand(rb):
    rows = 8

    def body(rb_ref, o_ref):
        for r in range(rows):
            o_ref[r] = _dot3(rb_ref[...], _rel_onehot(pl.program_id(0) * rows + r, True), _NN)

    return _pcall(body, name="relbias_expand", grid=(CHUNK // rows,),
                  in_specs=[pl.BlockSpec((CA_HEADS, N_REL), lambda q: (0, 0))],
                  out_specs=pl.BlockSpec((rows, CA_HEADS, CA_BAND), lambda q: (q, 0, 0)),
                  out_shape=jax.ShapeDtypeStruct((CHUNK, CA_HEADS, CA_BAND), F32),
                  compiler_params=_params(("arbitrary",)))(rb)


def _relbias_reduce(db):
    rows = 8

    def body(db_ref, o_ref):
        q = pl.program_id(0)
        part = sum(_dot3(db_ref[r], _rel_onehot(q * rows + r, False), _NN) for r in range(rows))

        @pl.when(q == 0)
        def _():
            o_ref[...] = part

        @pl.when(q > 0)
        def _():
            o_ref[...] += part

    return _pcall(body, name="relbias_reduce", grid=(CHUNK // rows,),
                  in_specs=[pl.BlockSpec((rows, CA_HEADS, CA_BAND), lambda q: (q, 0, 0))],
                  out_specs=pl.BlockSpec((CA_HEADS, N_REL), lambda q: (0, 0)),
                  out_shape=jax.ShapeDtypeStruct((CA_HEADS, N_REL), F32),
                  compiler_params=_params(("arbitrary",)))(db)


def _attn_cfg(mode, S):
    if mode == "fox":
        return min(256, S), FOX_HEAD_DIM ** -0.5
    if mode == "chunk":
        return CHUNK, CA_HEAD_DIM ** -0.5
    return min(512, S), XA_HEAD_DIM ** -0.5


def _visible(mode, i, tq, nk):
    if mode == "fox":
        row = lax.broadcasted_iota(jnp.int32, (tq, nk), 0) + i * tq
        return row >= lax.broadcasted_iota(jnp.int32, (tq, nk), 1)
    if mode == "chunk":
        return lax.broadcasted_iota(jnp.int32, (tq, nk), 1) + i * CHUNK >= CA_PAD
    return None


def _scores(mode, qs, kb, extra, visible):
    s = lax.dot_general(qs, kb, _NT, preferred_element_type=F32)
    if mode == "fox":
        cq, ck = extra
        s = jnp.where(visible, s + cq - ck, NEG_INF)
    elif mode == "chunk":
        (bias,) = extra
        s = jnp.where(visible, s + bias, NEG_INF)
    return s


class _AttnPlan:
    def __init__(self, mode, q, k, pp):
        self.mode, self.pp = mode, pp
        self.D, self.hpb, self.bw = (XA_HEAD_DIM, 1, XA_HEAD_DIM) if mode == "xa" else (64, 2, LANE)
        self.S, self.Sk = q[0].shape[0], k[0].shape[0]
        self.H = (XA_HEADS if mode == "xa" else FOX_HEADS)
        self.W = self.H * self.D
        self.gw = self.bw * pp
        self.hpg = self.hpb * pp
        self.tq, self.scale = _attn_cfg(mode, self.S)
        self.grid = (self.W // self.gw, self.S // self.tq)

    def rows(self, win):
        off = win[1] // self.gw
        return pl.BlockSpec((self.tq, self.gw), lambda g, i: (i, off + g))

    def cols(self, win):
        off = win[1] // self.gw
        return pl.BlockSpec((win[0].shape[0], self.gw), lambda g, i: (0, off + g))

    def extras(self):
        if self.mode == "fox":
            return [pl.BlockSpec((self.hpg, self.tq, 1), lambda g, i: (g, i, 0)),
                    pl.BlockSpec((self.hpg, 1, self.Sk), lambda g, i: (g, 0, 0))]
        if self.mode == "chunk":
            return [pl.BlockSpec((self.hpg, CHUNK, CA_BAND), lambda g, i: (g, 0, 0))]
        return []

    def per_row(self):
        return pl.BlockSpec((self.hpg, self.tq, 1), lambda g, i: (g, i, 0))

    def lanes(self, p):
        return slice(p * self.bw, (p + 1) * self.bw)

    def keys(self, i, ref, p, nk):
        if self.mode == "chunk":
            return ref[pl.ds(pl.multiple_of(i * CHUNK, CHUNK), CA_BAND), self.lanes(p)]
        return ref[0:nk, self.lanes(p)]

    def key_ranges(self, i, run):
        nq = self.grid[1]
        if self.mode != "fox" or nq % 4:
            return run(CA_BAND if self.mode == "chunk" else self.Sk)
        for part in range(4):
            lo, hi = part * nq // 4, (part + 1) * nq // 4
            pl.when((i >= lo) & (i < hi))(functools.partial(run, hi * self.tq))

    def head(self, x, hh):
        if self.hpb == 1:
            return x
        lane = lax.broadcasted_iota(jnp.int32, x.shape, 1)
        return jnp.where(lane // self.D == hh, x, jnp.zeros_like(x))


def _attn_fwd(mode, q, k, v, extra=(), *, name):
    pl_ = _AttnPlan(mode, q, k, pp=4)
    n_ex = len(extra)
    out = (None, 0)

    def body(*refs):
        q_ref, k_ref, v_ref = refs[:3]
        o_ref, lse_ref = refs[3 + n_ex:]
        i = pl.program_id(1)

        def run(nk):
            visible = _visible(mode, i, pl_.tq, nk)
            for p in range(pl_.pp):
                qs = q_ref[:, pl_.lanes(p)].astype(BF16) * pl_.scale
                kp = pl_.keys(i, k_ref, p, nk).astype(BF16)
                vp = pl_.keys(i, v_ref, p, nk).astype(BF16)
                acc = None
                for hh in range(pl_.hpb):
                    h = p * pl_.hpb + hh
                    ex = [r[h] for r in refs[3:3 + n_ex]]
                    if mode == "fox":
                        ex[1] = ex[1][:, 0:nk]
                    s = _scores(mode, pl_.head(qs, hh), kp, ex, visible)
                    m = jnp.max(s, axis=1, keepdims=True)
                    e = jnp.exp(s - m)
                    l = jnp.sum(e, axis=1, keepdims=True)
                    part = lax.dot_general((e * (1.0 / l)).astype(BF16), pl_.head(vp, hh), _NN,
                                           preferred_element_type=F32)
                    acc = part if acc is None else acc + part
                    lse_ref[h] = m + jnp.log(l)
                o_ref[:, pl_.lanes(p)] = acc

        pl_.key_ranges(i, run)

    return _pcall(
        body, name=name, grid=pl_.grid, in_specs=[pl_.rows(q), pl_.cols(k), pl_.cols(v)] + pl_.extras(),
        out_specs=[pl_.rows(out), pl_.per_row()],
        out_shape=[jax.ShapeDtypeStruct((pl_.S, pl_.W), F32), jax.ShapeDtypeStruct((pl_.H, pl_.S, 1), F32)],
        compiler_params=_params(("parallel", "arbitrary")),
    )(q[0], k[0], v[0], *extra)


def _attn_bwd(mode, q, k, v, lse, do, extra=(), *, name):
    pl_ = _AttnPlan(mode, q, k, pp=2 if mode == "fox" else 4)
    H, S, Sk, W = pl_.H, pl_.S, pl_.Sk, pl_.W
    n_ex = len(extra)
    out = (None, 0)
    kv_out = pl.BlockSpec((Sk, pl_.gw), lambda g, i: (0, g))
    ex_specs = pl_.extras()
    out_specs = [pl_.rows(out), kv_out, kv_out]
    out_shape = [jax.ShapeDtypeStruct((S, W), F32), jax.ShapeDtypeStruct((Sk, W), F32),
                 jax.ShapeDtypeStruct((Sk, W), F32)]
    if mode == "fox":
        out_specs += [pl_.per_row(), ex_specs[1]]
        out_shape += [jax.ShapeDtypeStruct((H, S, 1), F32), jax.ShapeDtypeStruct((H, 1, Sk), F32)]
    elif mode == "chunk":
        out_specs += [ex_specs[0]]
        out_shape += [jax.ShapeDtypeStruct((H, CHUNK, CA_BAND), F32)]

    def body(*refs):
        q_ref, k_ref, v_ref, lse_ref, do_ref = refs[:5]
        dq_ref, dk_ref, dv_ref = refs[5 + n_ex:8 + n_ex]
        rest = refs[8 + n_ex:]
        i = pl.program_id(1)

        @pl.when(i == 0)
        def _():
            dk_ref[...] = jnp.zeros_like(dk_ref)
            dv_ref[...] = jnp.zeros_like(dv_ref)
            if mode == "fox":
                rest[1][...] = jnp.zeros_like(rest[1])
            elif mode == "chunk":
                rest[0][...] = jnp.zeros_like(rest[0])

        def run(nk):
            visible = _visible(mode, i, pl_.tq, nk)
            for p in range(pl_.pp):
                lanes = pl_.lanes(p)
                qs = q_ref[:, lanes].astype(BF16) * pl_.scale
                kp = pl_.keys(i, k_ref, p, nk).astype(BF16)
                vp = pl_.keys(i, v_ref, p, nk).astype(BF16)
                dop = do_ref[:, lanes].astype(BF16)
                dq = dk_part = dv_part = None
                for hh in range(pl_.hpb):
                    h = p * pl_.hpb + hh
                    ex = [r[h] for r in refs[5:5 + n_ex]]
                    if mode == "fox":
                        ex[1] = ex[1][:, 0:nk]
                    qh, doh = pl_.head(qs, hh), pl_.head(dop, hh)
                    s = _scores(mode, qh, kp, ex, visible)
                    pr = jnp.exp(s - lse_ref[h])
                    dp = lax.dot_general(doh, vp, _NT, preferred_element_type=F32)
                    ds = pr * (dp - jnp.sum(dp * pr, axis=1, keepdims=True))
                    dsb = ds.astype(BF16)
                    parts = (lax.dot_general(dsb, pl_.head(kp, hh), _NN, preferred_element_type=F32) * pl_.scale,
                             lax.dot_general(dsb, qh, _TN, preferred_element_type=F32),
                             lax.dot_general(pr.astype(BF16), doh, _TN, preferred_element_type=F32))
                    dq, dk_part, dv_part = parts if dq is None else (dq + parts[0], dk_part + parts[1],
                                                                     dv_part + parts[2])
                    if mode == "chunk":
                        rest[0][h] += ds
                    if mode == "fox":
                        rest[0][h] = jnp.sum(ds, axis=1, keepdims=True)
                        rest[1][h, :, 0:nk] += -jnp.sum(ds, axis=0, keepdims=True)
                dq_ref[:, lanes] = dq
                if mode == "chunk":
                    win = pl.ds(pl.multiple_of(i * CHUNK, CHUNK), CA_BAND)
                    dk_ref[win, lanes] += dk_part
                    dv_ref[win, lanes] += dv_part
                else:
                    dk_ref[0:nk, lanes] += dk_part
                    dv_ref[0:nk, lanes] += dv_part

        pl_.key_ranges(i, run)

    return _pcall(
        body, name=name, grid=pl_.grid,
        in_specs=[pl_.rows(q), pl_.cols(k), pl_.cols(v), pl_.per_row(), pl_.rows(do)] + ex_specs,
        out_specs=out_specs, out_shape=out_shape,
        compiler_params=_params(("parallel", "arbitrary")),
    )(q[0], k[0], v[0], lse, do[0], *extra)


def _scan(x, a, h=None, *, name):
    S = x.shape[0]
    CB = SCAN_CB
    rev = h is not None
    n_grp = S // 8

    def body(*refs):
        if rev:
            x_ref, a_ref, h_ref, o_ref, da_ref = refs
        else:
            x_ref, a_ref, o_ref = refs
        ar = a_ref[:, :CB]
        ai = -a_ref[:, CB:] if rev else a_ref[:, CB:]
        zero = jnp.zeros((1, CB), F32)

        def group(g, carry):
            base = pl.multiple_of((n_grp - 1 - g) * 8 if rev else g * 8, 8)
            for j in (range(7, -1, -1) if rev else range(8)):
                t = base + j
                if rev:
                    hr, hi, dar, dai = carry
                else:
                    hr, hi = carry
                xr = x_ref[pl.ds(t, 1), :CB]
                xi = x_ref[pl.ds(t, 1), CB:]
                hr, hi = ar * hr - ai * hi + xr, ar * hi + ai * hr + xi
                o_ref[pl.ds(t, 1), :CB] = hr
                o_ref[pl.ds(t, 1), CB:] = hi
                if rev:
                    tp = jnp.maximum(t - 1, 0)
                    live = (t > 0).astype(F32)
                    pr = h_ref[pl.ds(tp, 1), :CB] * live
                    pi = h_ref[pl.ds(tp, 1), CB:] * live
                    carry = (hr, hi, dar + hr * pr + hi * pi, dai + hi * pr - hr * pi)
                else:
                    carry = (hr, hi)
            return carry

        if rev:
            _, _, dar, dai = lax.fori_loop(0, n_grp, group, (zero, zero, zero, zero))
            da_ref[:, :CB] = dar
            da_ref[:, CB:] = dai
        else:
            lax.fori_loop(0, n_grp, group, (zero, zero))

    big = pl.BlockSpec((S, 2 * CB), lambda c: (0, c))
    vec = pl.BlockSpec((1, 2 * CB), lambda c: (0, c))
    n_blk = x.shape[1] // (2 * CB)
    if rev:
        return _pcall(body, name=name, grid=(n_blk,), in_specs=[big, vec, big], out_specs=[big, vec],
                      out_shape=[jax.ShapeDtypeStruct(x.shape, F32), jax.ShapeDtypeStruct(a.shape, F32)],
                      compiler_params=_params(("parallel",)))(x, a, h)
    return _pcall(body, name=name, grid=(n_blk,), in_specs=[big, vec], out_specs=big,
                  out_shape=jax.ShapeDtypeStruct(x.shape, F32), compiler_params=_params(("parallel",)))(x, a)


def _ssm_prep1(lr_, li, ldt):
    lr = jnp.minimum(lr_, -1e-4)
    dt = jnp.exp(ldt)
    mag = jnp.exp(lr * dt)
    ar = mag * jnp.cos(li * dt)
    ai = mag * jnp.sin(li * dt)
    den = lr * lr + li * li
    gr = ((ar - 1.0) * lr + ai * li) / den
    gi = (ai * lr - (ar - 1.0) * li) / den
    return ar, ai, gr, gi


def _ssm_prep2(gr, gi, br, bi):
    return gr * br - gi * bi, gr * bi + gi * br


def _vjp_of(fn, n_in):
    def bwd(*args):
        cts = args[n_in:]
        return jax.vjp(fn, *args[:n_in])[1](cts[0] if len(cts) == 1 else tuple(cts))
    return bwd


def _to_blocked(r, i):
    lead = r.shape[:-1]
    t = jnp.stack([r.reshape(lead + (N_SSM_CH // SCAN_CB, SCAN_CB)), i.reshape(lead + (N_SSM_CH // SCAN_CB, SCAN_CB))],
                  axis=-2)
    return t.reshape(lead + (2 * N_SSM_CH,))


def _from_blocked(m):
    lead = m.shape[:-1]
    t = m.reshape(lead + (N_SSM_CH // SCAN_CB, 2, SCAN_CB))
    return t[..., 0, :].reshape(lead + (N_SSM_CH,)), t[..., 1, :].reshape(lead + (N_SSM_CH,))


_GPB = SSM_GROUPS // SSM_BD
_GPS = SCAN_CB // SSM_STATE


def _bd_eye():
    return jnp.eye(_GPB, dtype=F32).reshape(_GPB, _GPB // _GPS, _GPS)


def _blockdiag(r, i):
    v = jnp.stack([r, i]).reshape(2, SSM_BD, _GPB, SSM_GROUP, SSM_STATE)
    return jnp.einsum("qjgcp,gsh->jgcsqhp", v, _bd_eye()).reshape(SSM_BD, SSM_BD_IN, SSM_BD_ST)


def _blockdiag_inv(m):
    d = m.reshape(SSM_BD, _GPB, SSM_GROUP, _GPB // _GPS, 2, _GPS, SSM_STATE)
    v = jnp.einsum("jgcsqhp,gsh->qjgcp", d, _bd_eye()).reshape(2, SSM_GROUPS, SSM_GROUP, SSM_STATE)
    return v[0], v[1]


def _shift_rows(x, n):
    S = x.shape[0]
    row = lax.broadcasted_iota(jnp.int32, x.shape, 0)
    if n > 0:
        return jnp.where(row >= n, pltpu.roll(x, n, 0), 0.0)
    return jnp.where(row < S + n, pltpu.roll(x, S + n, 0), 0.0)


def _bf(x):
    return x.astype(BF16).astype(F32)


def _conv_pre(a, w, b):
    ab, wb = _bf(a), _bf(w)
    return wb[2:3] * ab + wb[1:2] * _shift_rows(ab, 1) + wb[0:1] * _shift_rows(ab, 2) + b


def _ffn_mid(up, conv_w, conv_b, dh=None, *, name):
    S = up.shape[0]
    tn = 2 * LANE
    nb = D_FF_P // tn
    rev = dh is not None

    def body(*refs):
        if not rev:
            a_ref, g_ref, w_ref, b_ref, o_ref = refs
            o_ref[...] = jax.nn.gelu(_conv_pre(a_ref[...], w_ref[...], b_ref[...])) * g_ref[...]
            return
        a_ref, g_ref, w_ref, b_ref, dh_ref, dup_a_ref, dup_g_ref, dw_ref, db_ref = refs
        a, w, dh_ = a_ref[...], w_ref[...], dh_ref[...]
        pre = _conv_pre(a, w, b_ref[...])
        gl, gelu_vjp = jax.vjp(jax.nn.gelu, pre)
        dup_g_ref[...] = dh_ * gl
        (dpre,) = gelu_vjp(dh_ * g_ref[...])
        db_ref[...] = jnp.sum(dpre, axis=0, keepdims=True)
        dpb, ab, wb = _bf(dpre), _bf(a), _bf(w)
        dup_a_ref[...] = wb[2:3] * dpb + wb[1:2] * _shift_rows(dpb, -1) + wb[0:1] * _shift_rows(dpb, -2)
        dw_ref[2:3, :] = jnp.sum(dpb * ab, axis=0, keepdims=True)
        dw_ref[1:2, :] = jnp.sum(dpb * _shift_rows(ab, 1), axis=0, keepdims=True)
        dw_ref[0:1, :] = jnp.sum(dpb * _shift_rows(ab, 2), axis=0, keepdims=True)

    a_spec = pl.BlockSpec((S, tn), lambda j: (0, j))
    g_spec = pl.BlockSpec((S, tn), lambda j: (0, j + nb))
    w_spec = pl.BlockSpec((3, tn), lambda j: (0, j))
    b_spec = pl.BlockSpec((1, tn), lambda j: (0, j))
    if not rev:
        return _pcall(body, name=name, grid=(nb,), in_specs=[a_spec, g_spec, w_spec, b_spec], out_specs=a_spec,
                      out_shape=jax.ShapeDtypeStruct((S, D_FF_P), F32), compiler_params=_params(("parallel",)))(
                          up, up, conv_w, conv_b)
    return _pcall(body, name=name, grid=(nb,), in_specs=[a_spec, g_spec, w_spec, b_spec, a_spec],
                  out_specs=[a_spec, a_spec, w_spec, b_spec],
                  out_shape=[jax.ShapeDtypeStruct((S, D_FF_P), F32), jax.ShapeDtypeStruct((S, D_FF_P), F32),
                             jax.ShapeDtypeStruct((3, D_FF_P), F32), jax.ShapeDtypeStruct((1, D_FF_P), F32)],
                  compiler_params=_params(("parallel",)))(up, up, conv_w, conv_b, dh)


def _ff_pad(t):
    lead = t.shape[:-1]
    t = t.reshape(lead + (N_DEV, FF_HALF))
    return jnp.pad(t, [(0, 0)] * len(lead) + [(0, 0), (0, FF_HALF_P - FF_HALF)]).reshape(lead + (D_FF_P,))


def _ff_unpad(t):
    lead = t.shape[:-1]
    return t.reshape(lead + (N_DEV, FF_HALF_P))[..., :FF_HALF].reshape(lead + (D_FF,))


def _ln_fwd(x, h, g, b):
    r = DN_ALPHA * x + h
    mu = jnp.mean(r, axis=-1, keepdims=True)
    var = jnp.mean(jnp.square(r - mu), axis=-1, keepdims=True)
    return r, (r - mu) * lax.rsqrt(var + LN_EPS) * g + b


def _ln_bwd(r, dy, g):
    mu = jnp.mean(r, axis=-1, keepdims=True)
    var = jnp.mean(jnp.square(r - mu), axis=-1, keepdims=True)
    xhat = (r - mu) * lax.rsqrt(var + LN_EPS)
    dxh = dy * g
    dr = lax.rsqrt(var + LN_EPS) * (dxh - jnp.mean(dxh, axis=-1, keepdims=True)
                                    - xhat * jnp.mean(dxh * xhat, axis=-1, keepdims=True))
    return dr, jnp.sum(dy * xhat, axis=0, keepdims=True), jnp.sum(dy, axis=0, keepdims=True)


def _merge(gf, gs, gc, ya, yb2, yc):
    yb = yb2[:, :D_MODEL] * jax.nn.sigmoid(yb2[:, D_MODEL:])
    return jax.nn.sigmoid(gf) * ya + jax.nn.sigmoid(gs) * yb + jax.nn.sigmoid(gc) * yc


def _s5_tail(hc, su, d):
    return jax.nn.gelu(hc + d * su)


def _s5_tail_bwd(hc, su, dgel, d):
    _, vjp = jax.vjp(jax.nn.gelu, hc + d * su)
    (dy,) = vjp(dgel)
    return dy, d * dy, jnp.sum(dy * su, axis=0, keepdims=True)


def _loss_rows(y, tgt):
    err = y - tgt
    return err * (1.0 / D_MODEL), jnp.sum(0.5 * jnp.square(err), axis=0, keepdims=True) * (1.0 / D_MODEL)


def _peer(k):
    x, y, c = lax.axis_index("x"), lax.axis_index("y"), lax.axis_index("c")
    return (x ^ ((k >> 2) & 1), y ^ ((k >> 1) & 1), c ^ (k & 1))


def _my_slot():
    return 4 * lax.axis_index("x") + 2 * lax.axis_index("y") + lax.axis_index("c")


def _peer_slot(k):
    px, py, pc = _peer(k)
    return 4 * px + 2 * py + pc


N_CHIP = N_DEV // 2
OTHER_CHIPS = (2, 4, 6)


def _chip_of(dev):
    return 2 * dev[0] + dev[1]


def _remote(src, dst, send, recv, dev):
    return pltpu.make_async_remote_copy(src_ref=src, dst_ref=dst, send_sem=send, recv_sem=recv, device_id=dev,
                                        device_id_type=pl.DeviceIdType.MESH)


def _all_gather(shards, *, name, host=None, then=None):
    return _exchange(shards, *_all_gather_parts(shards), name=name, host=host, then=then)


def _exchange(ins, out_shapes, sem_shapes, start, finish, *, name, host=None, then=None):
    if host is not None:
        rider = _Rider(host, ins, out_shapes, sem_shapes, start, finish, then)
        _RIDERS.append(rider)
        return rider
    n = len(ins)

    def body(*refs):
        start(refs[:n], refs[n:2 * n], refs[2 * n:])
        finish(refs[:n], refs[n:2 * n], refs[2 * n:])

    hbm = pl.BlockSpec(memory_space=pl.ANY)
    return _pcall(body, name=name, in_specs=[hbm] * n, out_specs=[hbm] * n, out_shape=list(out_shapes),
                  scratch_shapes=list(sem_shapes))(*ins)


def _all_gather_parts(shards):
    n = len(shards)

    def first_copies(ins, outs, sems):
        send, recv, _ = sems
        return [_remote(ins[t], outs[t].at[_my_slot()], send.at[t, k - 1], recv.at[t, k - 1], _peer(k))
                for k in (1,) + OTHER_CHIPS for t in range(n)]

    def local_copies(ins, outs, sems):
        return [pltpu.make_async_copy(ins[t], outs[t].at[_my_slot()], sems[2].at[t]) for t in range(n)]

    def start(ins, outs, sems):
        for cp in local_copies(ins, outs, sems) + first_copies(ins, outs, sems):
            cp.start()

    def finish(ins, outs, sems):
        send, recv, _ = sems
        sibling = _peer(1)
        passed = []
        for k in OTHER_CHIPS:
            for t in range(n):
                slot = outs[t].at[_peer_slot(k)]
                _remote(ins[t], slot, send.at[t, k - 1], recv.at[t, k - 1], _peer(k)).wait_recv()
                cp = _remote(slot, slot, send.at[t, k], recv.at[t, k], sibling)
                cp.start()
                passed.append(cp)
        for t in range(n):
            _remote(ins[t], outs[t].at[_peer_slot(1)], send.at[t, 0], recv.at[t, 0], sibling).wait_recv()
            for k in OTHER_CHIPS:
                _remote(ins[t], outs[t].at[_peer_slot(k + 1)], send.at[t, k], recv.at[t, k], sibling).wait_recv()
        for cp in first_copies(ins, outs, sems) + passed:
            cp.wait_send()
        for lc in local_copies(ins, outs, sems):
            lc.wait()

    out_shapes = [jax.ShapeDtypeStruct((N_DEV,) + s.shape, s.dtype) for s in shards]
    sem_shapes = [pltpu.SemaphoreType.DMA((n, N_DEV - 1)), pltpu.SemaphoreType.DMA((n, N_DEV - 1)),
                  pltpu.SemaphoreType.DMA((n,))]
    return out_shapes, sem_shapes, start, finish


def _sibling_swap(grads, *, name, host=None, then=None):
    n = len(grads)

    def copies(ins, outs, sems):
        c = lax.axis_index("c")
        return [_remote(ins[t].at[:, 1 - c], outs[t], sems[0].at[t], sems[1].at[t], _peer(1)) for t in range(n)]

    def start(ins, outs, sems):
        for cp in copies(ins, outs, sems):
            cp.start()

    def finish(ins, outs, sems):
        for cp in copies(ins, outs, sems):
            cp.wait()

    out_shapes = [jax.ShapeDtypeStruct((N_CHIP,) + g.shape[2:], g.dtype) for g in grads]
    sem_shapes = [pltpu.SemaphoreType.DMA((n,)), pltpu.SemaphoreType.DMA((n,))]
    return _exchange(grads, out_shapes, sem_shapes, start, finish, name=name, host=host, then=then)


def _pair_add(g, p, out_dtype, *, name):
    _, _, R, C = g.shape
    tr = _pick(R, (512, 256, 128, 64, 32, 16, 8))

    def body(c_ref, g_ref, p_ref, o_ref):
        o_ref[...] = (g_ref[...] + p_ref[...]).astype(out_dtype)

    grid_spec = pltpu.PrefetchScalarGridSpec(
        num_scalar_prefetch=1, grid=(N_CHIP, R // tr),
        in_specs=[pl.BlockSpec((None, None, tr, C), lambda j, i, c_ref: (j, c_ref[0], i, 0)),
                  pl.BlockSpec((None, tr, C), lambda j, i, c_ref: (j, i, 0))],
        out_specs=pl.BlockSpec((None, tr, C), lambda j, i, c_ref: (j, i, 0)))
    core = lax.axis_index("c").astype(jnp.int32).reshape(1)
    return _pcall(body, name=name, grid_spec=grid_spec, out_shape=jax.ShapeDtypeStruct(p.shape, out_dtype),
                  compiler_params=_params(("parallel", "parallel")))(core, g, p)


def _chip_exchange(sums, *, name, host=None):
    n = len(sums)

    def copies(ins, outs, sems, dst_is_mine):
        send, recv, _ = sems
        mine = 2 * lax.axis_index("x") + lax.axis_index("y")
        out = []
        for k in OTHER_CHIPS:
            theirs = _chip_of(_peer(k))
            for t in range(n):
                out.append(_remote(ins[t].at[theirs], outs[t].at[mine if dst_is_mine else theirs],
                                   send.at[t, k // 2 - 1], recv.at[t, k // 2 - 1], _peer(k)))
        return out

    def local_copies(ins, outs, sems):
        mine = 2 * lax.axis_index("x") + lax.axis_index("y")
        return [pltpu.make_async_copy(ins[t].at[mine], outs[t].at[mine], sems[2].at[t]) for t in range(n)]

    def start(ins, outs, sems):
        for cp in local_copies(ins, outs, sems) + copies(ins, outs, sems, True):
            cp.start()

    def finish(ins, outs, sems):
        for cp in copies(ins, outs, sems, False) + local_copies(ins, outs, sems):
            cp.wait()

    out_shapes = [jax.ShapeDtypeStruct(s.shape, s.dtype) for s in sums]
    sem_shapes = [pltpu.SemaphoreType.DMA((n, N_CHIP - 1)), pltpu.SemaphoreType.DMA((n, N_CHIP - 1)),
                  pltpu.SemaphoreType.DMA((n,))]
    return _exchange(sums, out_shapes, sem_shapes, start, finish, name=name, host=host)


class _GradReduce:
    def __init__(self, grads, wire_dtypes, hosts=None):
        pairs = [g.reshape((N_CHIP, 2) + g.shape[1:]) for g in grads]
        self.n, self.riders, self.recv = len(grads), [], None

        def after_swap(partner):
            sums = [_pair_add(g, p, dt, name="grad_pair_add") for g, p, dt in zip(pairs, partner, wire_dtypes)]
            if hosts is None:
                self.recv = _chip_exchange(sums, name="grad_chip_exchange")
            else:
                self.riders = [(idx, _chip_exchange([sums[i] for i in idx], name="grad_chip_exchange", host=h))
                               for h, idx in hosts[1]]

        if hosts is None:
            after_swap(_sibling_swap(pairs, name="grad_sibling_swap"))
        else:
            _sibling_swap(pairs, name="grad_sibling_swap", host=hosts[0], then=after_swap)

    def result(self):
        if self.recv is None:
            self.recv = [None] * self.n
            for idx, rider in self.riders:
                assert rider.results is not None, rider.host
                for i, r in zip(idx, rider.results):
                    self.recv[i] = r
        return self.recv


def _adamw(recv, w, m, v, layer=None, into=None, *, name):
    n_slots, R, C = recv.shape
    tr = _pick(R, (256, 128, 64, 32, 16, 8))

    def body(r_ref, w_ref, m_ref, v_ref, *rest):
        g_ref, d_ref, nm_ref, nv_ref = rest[-4:]
        g = r_ref[0].astype(F32)
        for s in range(1, n_slots):
            g = g + r_ref[s].astype(F32)
        m_new = ADAM_B1 * m_ref[...] + (1.0 - ADAM_B1) * g
        v_new = ADAM_B2 * v_ref[...] + (1.0 - ADAM_B2) * jnp.square(g)
        m_hat = m_new / (1.0 - ADAM_B1 ** ADAM_STEP)
        v_hat = v_new / (1.0 - ADAM_B2 ** ADAM_STEP)
        g_ref[...] = g
        d_ref[...] = -ADAM_LR * (m_hat / (jnp.sqrt(v_hat) + ADAM_EPS) + ADAM_WD * w_ref[...])
        nm_ref[...] = m_new
        nv_ref[...] = v_new

    row = pl.BlockSpec((tr, C), lambda i: (i, 0))
    state = row if layer is None else pl.BlockSpec((None, tr, C), lambda i: (layer, i, 0))
    in_specs = [pl.BlockSpec((n_slots, tr, C), lambda i: (0, i, 0)), state, state, state]
    if into is None:
        return _pcall(body, name=name, grid=(R // tr,), in_specs=in_specs, out_specs=[row] * 4,
                      out_shape=[jax.ShapeDtypeStruct((R, C), F32)] * 4, compiler_params=_params(("parallel",)),
                      )(recv, w, m, v)
    return _pcall(body, name=name, grid=(R // tr,), in_specs=in_specs + [pl.BlockSpec(memory_space=pl.ANY)] * 4,
                  out_specs=[state] * 4, out_shape=[jax.ShapeDtypeStruct((DEPTH, R, C), F32)] * 4,
                  input_output_aliases={4 + j: j for j in range(4)}, compiler_params=_params(("parallel",)),
                  )(recv, w, m, v, *into)


def _flat_pad(parts, rows):
    flat = jnp.concatenate([p.reshape(-1) for p in parts])
    return jnp.pad(flat, (0, rows * LANES - flat.shape[0])).reshape(rows, LANES)


def _small_shard_shape(n):
    return SMALL[n][:-1] + (SMALL[n][-1] // N_DEV,)


def _unpack_small(gathered):
    out, off = {}, 0
    flat = gathered.reshape(N_DEV, -1)
    for n in SMALL:
        r, c = _small_shard_shape(n)
        out[n] = flat[:, off:off + r * c].reshape(N_DEV, r, c).transpose(1, 0, 2).reshape(r, N_DEV * c)
        off += r * c
    return out


def _pack_state(state, prefix):
    flat = jnp.concatenate([state[prefix + n].reshape(DEPTH, -1) for n in (*SMALL, *REPL)], axis=1)
    return jnp.pad(flat, ((0, 0), (0, PACK_ROWS * LANES - flat.shape[1]))).reshape(DEPTH * PACK_ROWS, LANES)


def _pack_small_grads(grads):
    cols = []
    for n in SMALL:
        r, c = _small_shard_shape(n)
        cols.append(grads[n].reshape(r, N_DEV, c).transpose(1, 0, 2).reshape(N_DEV, r * c))
    cols += [jnp.broadcast_to(grads[n].reshape(1, -1), (N_DEV, grads[n].size)) for n in REPL]
    flat = jnp.concatenate(cols, axis=1)
    return jnp.pad(flat, ((0, 0), (0, PACK_ROWS * LANES - flat.shape[1]))).reshape(N_DEV, PACK_ROWS, LANES)


def _unpack_state(packed):
    out, off = {}, 0
    flat = packed.reshape(DEPTH, -1)
    for n, shape in [(n, _small_shard_shape(n)) for n in SMALL] + list(REPL.items()):
        sz = math.prod(shape)
        out[n] = flat[:, off:off + sz].reshape((DEPTH,) + shape)
        off += sz
    return out


def _pad_b_in(b):
    parts, pos = [], 0
    for src, width, dst in Z_PIECES:
        parts += [jnp.zeros((b.shape[0], dst - pos), b.dtype), b[:, src:src + width]]
        pos = dst + width
    return jnp.concatenate(parts + [jnp.zeros((b.shape[0], Z_W - pos), b.dtype)], axis=1)


def _unpad_b_in(bp):
    return jnp.concatenate([bp[:, dst:dst + width] for _, width, dst in Z_PIECES], axis=1)


def _up_shard_pad(t):
    gap = jnp.zeros(t.shape[:-1] + (FF_HALF_P - FF_HALF,), t.dtype)
    return jnp.concatenate([t[..., :FF_HALF], gap, t[..., FF_HALF:], gap], axis=-1)


def _up_shard_unpad(t):
    return jnp.concatenate([t[..., :FF_HALF], t[..., FF_HALF_P:FF_HALF_P + FF_HALF]], axis=-1)


def _pad_shard(n, t):
    lead = [(0, 0)] * (t.ndim - 2)
    if n == "w_in":
        return jnp.pad(t, lead + [(0, 0), (0, W_IN_SLOT - W_IN_SHARD)])
    if n == "ffn_w_up":
        return _up_shard_pad(t)
    if n == "ffn_w_down":
        return jnp.pad(t, lead + [(0, FF_HALF_P - FF_HALF), (0, 0)])
    return t


def _unpad_shard(n, t):
    if n == "w_in":
        return t[..., :W_IN_SHARD]
    if n == "ffn_w_up":
        return _up_shard_unpad(t)
    if n == "ffn_w_down":
        return t[..., :FF_HALF, :]
    return t


GATHER_HOSTS = (("chunk_fwd", (0,)), ("fox_fwd", (8,)), ("s5_scan", (9,)), ("mm_in", (1, 2, 3, 4)),
                ("mm_up", (5, 6, 7, 10)))
FIRST_GATHER_HOSTS = ((None, (0,)), ("mm_in", (1, 2, 3, 4, 5, 6, 7, 10)), ("s5_scan", (8, 9)))
SECOND_GATHER_HOSTS = (("fox_fwd", (0,)), ("chunk_fwd", (8, 9)), ("mm_up", (1, 2, 3, 4, 5, 6, 7, 10)))
REDUCE_HOSTS = ("mm_up_dw", (("fox_bwd", (0,)), ("s5_scan_bwd", (1, 2, 3, 4, 5, 6, 7, 10)), ("chunk_bwd", (8,)),
                             ("mm_in_dx", (9,))))


def _layer_shards(state, layer):
    shards = [_pad_shard(n, state[n][layer].astype(BF16)) for n in BIG]
    return shards + [_flat_pad([state[n][layer] for n in SMALL], SMALL_ROWS)]


class _Weights:
    def __init__(self, sources):
        self._sources, self._cache = sources, {}

    def _operand(self, i):
        src = self._sources[i]
        if isinstance(src, tuple):
            assert src[0].results is not None, src[0].host
            return src[0].results[src[1]]
        return src

    def __getitem__(self, n):
        if n not in self._cache:
            if n in SMALL:
                self._cache.update(_unpack_small(self._operand(len(BIG))))
            elif n == "w_in_p":
                self._cache[n] = _colmap(self._operand(0), inverse=False, name="w_in_colmap")
            else:
                w = self._operand(list(BIG).index(n))
                self._cache[n] = w.reshape(-1, D_MODEL) if BIG[n][1] == 0 else w
        return self._cache[n]


def _gather_riding(shards, hosts):
    sources = [None] * len(shards)
    for host, idx in hosts:
        group = [shards[i] for i in idx]
        if host is None:
            for i, g in zip(idx, _all_gather(group, name="all_gather_weights")):
                sources[i] = g
        else:
            rider = _all_gather(group, name="all_gather_weights", host=host)
            for pos, i in enumerate(idx):
                sources[i] = (rider, pos)
    return sources


def _ssm_params(p):
    prep1_in = (p["ssm_lambda_re"], p["ssm_lambda_im"], p["ssm_log_dt"][:, None])
    ar, ai, gr, gi = _whole(_ssm_prep1, *prep1_in, name="ssm_prep1")
    to_cn = lambda b: b.transpose(2, 0, 1).reshape(SSM_GROUP, N_SSM_CH)
    prep2_in = (gr.reshape(1, N_SSM_CH), gi.reshape(1, N_SSM_CH), to_cn(p["ssm_b_re"]), to_cn(p["ssm_b_im"]))
    bbr, bbi = _whole(_ssm_prep2, *prep2_in, name="ssm_prep2")
    to_gcp = lambda t: t.reshape(SSM_GROUP, SSM_GROUPS, SSM_STATE).transpose(1, 0, 2)
    bb = _blockdiag(to_gcp(bbr), to_gcp(bbi))
    cct = _blockdiag(p["ssm_c_re"], -p["ssm_c_im"])
    a_vec = _to_blocked(ar.reshape(1, N_SSM_CH), ai.reshape(1, N_SSM_CH))
    return dict(bb=bb, cct=cct, a_vec=a_vec, prep1_in=prep1_in, prep2_in=prep2_in)


def _layer_fwd(x, mem, p, W):
    sp = _ssm_params(p)
    z = _mm(x, W["w_in_p"], bias=p["b_in_p"], name="mm_in")
    f_t = z[:, Z_FF:Z_FF + FOX_HEADS].T
    cum = _cum_heads(f_t, name="fox_cum")
    ya_pre, lse_a = _attn_fwd("fox", (z, Z_FQ), (z, Z_FK), (z, Z_FV), (cum[:, :, None], cum[:, None, :]),
                              name="fox_fwd")
    ya = _mm(ya_pre, W["w_fox_o"], b_slots=True, name="mm_fox_o")
    x_ri = _mm_bd("in", z, sp["bb"], a_off=Z_SU, name="mm_s5_in")
    h_ri = _scan(x_ri, sp["a_vec"], name="s5_scan")
    hc = _mm_bd("out", h_ri, sp["cct"], name="mm_s5_out")
    d_row = p["ssm_d"][None, :]
    (gel,) = _rowwise(lambda a, b, c: (_s5_tail(a, b, c),), [hc, Win(z, 512, Z_SU // 512)], [d_row], name="s5_tail")
    yb2 = _mm(gel, W["w_ssm_glu"], b_slots=True, name="mm_glu")
    bias = _relbias_expand(W["ca_rel_bias"]).transpose(1, 0, 2)
    kv_band = jnp.pad(z[:, Z_CK:Z_CK + 2 * CA_WIDTH].astype(BF16), ((CA_PAD, 0), (0, 0)))
    yc_pre, lse_c = _attn_fwd("chunk", (z, Z_CQ), (kv_band, 0), (kv_band, CA_WIDTH), (bias,), name="chunk_fwd")
    yc = _mm(yc_pre, W["w_ca_o"], b_slots=True, name="mm_ca_o")
    gates = [Win(z, 1024, Z_GF // 1024), Win(z, 1024, Z_GS // 1024), Win(z, 1024, Z_GC // 1024)]
    (merged,) = _rowwise(lambda *a: (_merge(*a),), gates + [ya, yb2, yc], name="merge")
    h1 = _mm(merged, W["w_o"], name="mm_o")
    ln_g, ln_b = W["ln_g"], W["ln_b"]
    r1, x1 = _rowwise(_ln_fwd, [x, h1], [ln_g[0:1], ln_b[0:1]], tr=512, name="ln_fwd")
    q = _mm(x1, W["xa_wq"], name="mm_xq")
    kv = _mm(mem, W["xa_wkv"], b_slots=True, name="mm_xkv")
    o, lse_x = _attn_fwd("xa", (q, 0), (kv, 0), (kv, D_MODEL), name="xa_fwd")
    h2 = _mm(o, W["xa_wo"], name="mm_xo")
    r2, x2 = _rowwise(_ln_fwd, [x1, h2], [ln_g[1:2], ln_b[1:2]], tr=512, name="ln_fwd")
    up = _mm(x2, W["ffn_w_up"], b_slots=True, name="mm_up")
    hmid = _ffn_mid(up, _ff_pad(W["ffn_conv_w"]), _ff_pad(p["ffn_conv_b"][None, :]), name="ffn_mid")
    h3 = _mm(hmid, W["ffn_w_down"], name="mm_down")
    r3, x3 = _rowwise(_ln_fwd, [x2, h3], [ln_g[2:3], ln_b[2:3]], tr=512, name="ln_fwd")
    res = dict(x=x, z=z, cum=cum, lse_a=lse_a, ya_pre=ya_pre, ya=ya, h_ri=h_ri, hc=hc, gel=gel, yb2=yb2, lse_c=lse_c,
               yc_pre=yc_pre, yc=yc, merged=merged, r1=r1, x1=x1, q=q, kv=kv, o=o, lse_x=lse_x, r2=r2, x2=x2, up=up,
               hmid=hmid, r3=r3, bias=bias, sp=sp, kv_band=kv_band, W=W)
    return x3, res


def _layer_bwd(dx3, mem, p, res):
    W = res["W"]
    x, z = res["x"], res["z"]
    sp = res["sp"]
    ln_g = W["ln_g"]
    big, small = {}, {}
    slots = lambda t: t.reshape(N_DEV, -1, D_MODEL)
    dr3, dg2, db2 = _rowwise(_ln_bwd, [res["r3"], dx3], [ln_g[2:3]], n_red=2, tr=512, name="ln_bwd")
    dhmid = _mm(dr3, W["ffn_w_down"], tb=True, name="mm_down_dx")
    big["ffn_w_down"] = slots(_mm(res["hmid"], dr3, ta=True, name="mm_down_dw"))
    conv_w_p, conv_b_p = _ff_pad(W["ffn_conv_w"]), _ff_pad(p["ffn_conv_b"][None, :])
    dup_a, dup_g, dcw, dcb = _ffn_mid(res["up"], conv_w_p, conv_b_p, dhmid, name="ffn_mid_bwd")
    dup = jnp.concatenate([dup_a, dup_g], axis=1)
    small["ffn_conv_w"], small["ffn_conv_b"] = _ff_unpad(dcw), _ff_unpad(dcb)[0]
    dx2 = _mm(dup, W["ffn_w_up"], tb=True, b_slots=True, add=(dr3, DN_ALPHA), name="mm_up_dx")
    big["ffn_w_up"] = _mm(res["x2"], dup, ta=True, out_slots=2 * FF_HALF_P, name="mm_up_dw")
    dr2, dg1, db1 = _rowwise(_ln_bwd, [res["r2"], dx2], [ln_g[1:2]], n_red=2, tr=512, name="ln_bwd")
    do = _mm(dr2, W["xa_wo"], tb=True, name="mm_xo_dx")
    big["xa_wo"] = slots(_mm(res["o"], dr2, ta=True, name="mm_xo_dw"))
    kv = res["kv"]
    dq, dk, dv = _attn_bwd("xa", (res["q"], 0), (kv, 0), (kv, D_MODEL), res["lse_x"], (do, 0), name="xa_bwd")
    dkv = jnp.concatenate([dk, dv], axis=1)
    dx1 = _mm(dq, W["xa_wq"], tb=True, add=(dr2, DN_ALPHA), name="mm_xq_dx")
    big["xa_wq"] = slots(_mm(res["x1"], dq, ta=True, name="mm_xq_dw"))
    big["xa_wkv"] = _mm(mem, dkv, ta=True, out_slots=256, name="mm_xkv_dw")
    dr1, dg0, db0 = _rowwise(_ln_bwd, [res["r1"], dx1], [ln_g[0:1]], n_red=2, tr=512, name="ln_bwd")
    small["ln_g"] = jnp.concatenate([dg0, dg1, dg2], axis=0)
    small["ln_b"] = jnp.concatenate([db0, db1, db2], axis=0)
    dmerged = _mm(dr1, W["w_o"], tb=True, name="mm_o_dx")
    big["w_o"] = slots(_mm(res["merged"], dr1, ta=True, name="mm_o_dw"))
    gates = [Win(z, 1024, Z_GF // 1024), Win(z, 1024, Z_GS // 1024), Win(z, 1024, Z_GC // 1024)]
    dgf, dgs, dgc, dya, dyb2, dyc = _rowwise(_vjp_of(_merge, 6), gates + [res["ya"], res["yb2"], res["yc"], dmerged],
                                             name="merge_bwd")
    dya_pre = _mm(dya, W["w_fox_o"], tb=True, b_slots=True, name="mm_fox_o_dx")
    big["w_fox_o"] = _mm(res["ya_pre"], dya, ta=True, out_slots=128, name="mm_fox_o_dw")
    cum = res["cum"]
    dfq, dfk, dfv, dcq, dck = _attn_bwd("fox", (z, Z_FQ), (z, Z_FK), (z, Z_FV), res["lse_a"], (dya_pre, 0),
                                        (cum[:, :, None], cum[:, None, :]), name="fox_bwd")
    f_t = z[:, Z_FF:Z_FF + FOX_HEADS].T
    dff = _cum_heads(f_t, dcq[:, :, 0] + dck[:, 0, :], name="fox_cum_bwd")
    dgel = _mm(dyb2, W["w_ssm_glu"], tb=True, b_slots=True, name="mm_glu_dx")
    big["w_ssm_glu"] = _mm(res["gel"], dyb2, ta=True, out_slots=256, name="mm_glu_dw")
    d_row = p["ssm_d"][None, :]
    su_win = Win(z, 512, Z_SU // 512)
    dy, dsu1, dd = _rowwise(_s5_tail_bwd, [res["hc"], su_win, dgel], [d_row], n_red=1, name="s5_tail_bwd")
    small["ssm_d"] = dd[0]
    dh_ri = _mm_bd("in", dy, sp["cct"], name="mm_s5_out_dx")
    dcct = _mm_bd("dw", dy, res["h_ri"], name="mm_s5_out_dw")
    dx_ri, da_vec = _scan(dh_ri, sp["a_vec"], res["h_ri"], name="s5_scan_bwd")
    dsu = _mm_bd("out", dx_ri, sp["bb"], add=(dsu1, 1.0), name="mm_s5_in_dx")
    dbb = _mm_bd("dw", z, dx_ri, a_off=Z_SU, name="mm_s5_in_dw")
    dcr, dci = _blockdiag_inv(dcct)
    small["ssm_c_re"], small["ssm_c_im"] = dcr, -dci
    dbbr, dbbi = _blockdiag_inv(dbb)
    to_cn = lambda t: t.transpose(1, 0, 2).reshape(SSM_GROUP, N_SSM_CH)
    dgr, dgi, dbr, dbi = _whole(_vjp_of(_ssm_prep2, 4), *sp["prep2_in"], to_cn(dbbr), to_cn(dbbi), name="ssm_prep2_bwd")
    from_cn = lambda t: t.reshape(SSM_GROUP, SSM_GROUPS, SSM_STATE).transpose(1, 2, 0)
    small["ssm_b_re"], small["ssm_b_im"] = from_cn(dbr), from_cn(dbi)
    dar, dai = _from_blocked(da_vec)
    sq = lambda t: t.reshape(SSM_GROUPS, SSM_STATE)
    dlr, dli, dldt = _whole(_vjp_of(_ssm_prep1, 3), *sp["prep1_in"], sq(dar), sq(dai), sq(dgr), sq(dgi),
                            name="ssm_prep1_bwd")
    small["ssm_lambda_re"], small["ssm_lambda_im"], small["ssm_log_dt"] = dlr, dli, dldt[:, 0]
    dyc_pre = _mm(dyc, W["w_ca_o"], tb=True, b_slots=True, name="mm_ca_o_dx")
    big["w_ca_o"] = _mm(res["yc_pre"], dyc, ta=True, out_slots=128, name="mm_ca_o_dw")
    bias = res["bias"]
    kv_band = res["kv_band"]
    dcq_, dck_band, dcv_band, dbias = _attn_bwd("chunk", (z, Z_CQ), (kv_band, 0), (kv_band, CA_WIDTH), res["lse_c"],
                                                (dyc_pre, 0), (bias,), name="chunk_bwd")
    small["ca_rel_bias"] = _relbias_reduce(dbias.transpose(1, 0, 2))
    dff_p = jnp.pad(dff.T, ((0, 0), (0, 512 - FOX_HEADS)))
    dz = jnp.concatenate([dfq, dfk, dfv, dff_p, dsu, dcq_, dck_band[CA_PAD:], dcv_band[CA_PAD:], dgf, dgs, dgc],
                         axis=1)
    dx = _mm(dz, W["w_in_p"], tb=True, add=(dr1, DN_ALPHA), name="mm_in_dx")
    big["w_in"] = _colmap(_mm(x, dz, ta=True, name="mm_in_dw"), inverse=True, name="w_in_colmap_inv")
    (db_in_p,) = _rowwise(lambda t: (jnp.sum(t, axis=0, keepdims=True),), [dz], n_red=1, name="colsum")
    small["b_in"] = _unpad_b_in(db_in_p)[0]
    return dx, big, small


def kernel(x, mem, w_in, b_in, ssm_lambda_re, ssm_lambda_im, ssm_log_dt, ssm_b_re, ssm_b_im, ssm_c_re, ssm_c_im, ssm_d, ca_rel_bias, w_fox_o, w_ssm_glu, w_ca_o, w_o, xa_wq, xa_wkv, xa_wo, ffn_w_up, ffn_conv_w, ffn_conv_b, ffn_w_down, ln_g, ln_b, loss_target, m_w_in, m_b_in, m_ssm_lambda_re, m_ssm_lambda_im, m_ssm_log_dt, m_ssm_b_re, m_ssm_b_im, m_ssm_c_re, m_ssm_c_im, m_ssm_d, m_ca_rel_bias, m_w_fox_o, m_w_ssm_glu, m_w_ca_o, m_w_o, m_xa_wq, m_xa_wkv, m_xa_wo, m_ffn_w_up, m_ffn_conv_w, m_ffn_conv_b, m_ffn_w_down, m_ln_g, m_ln_b, v_w_in, v_b_in, v_ssm_lambda_re, v_ssm_lambda_im, v_ssm_log_dt, v_ssm_b_re, v_ssm_b_im, v_ssm_c_re, v_ssm_c_im, v_ssm_d, v_ca_rel_bias, v_w_fox_o, v_w_ssm_glu, v_w_ca_o, v_w_o, v_xa_wq, v_xa_wkv, v_xa_wo, v_ffn_w_up, v_ffn_conv_w, v_ffn_conv_b, v_ffn_w_down, v_ln_g, v_ln_b):
    given = dict(locals())
    state = {pre + n: given[pre + n] for n in WEIGHTS for pre in ("", "m_", "v_")}
    mem0 = mem[0]
    b_in_p = _pad_b_in(b_in)
    layer_params = [{**{n: state[n][l] for n in REPL}, "b_in_p": b_in_p[l:l + 1]} for l in range(DEPTH)]

    _RIDERS.clear()
    h, residuals = x[0], []
    sources = _gather_riding(_layer_shards(state, 0), FIRST_GATHER_HOSTS)
    for l in range(DEPTH):
        hosts = SECOND_GATHER_HOSTS if l == 0 else GATHER_HOSTS
        following = _gather_riding(_layer_shards(state, l + 1), hosts) if l + 1 < DEPTH else None
        h, res = _layer_fwd(h, mem0, layer_params[l], _Weights(sources))
        residuals.append(res)
        sources = following
    dh, loss_cols = _rowwise(_loss_rows, [h, loss_target[0]], n_red=1, name="loss")
    loss = lax.psum(jnp.sum(loss_cols), ("x", "y", "c"))

    outs = [None] * DEPTH
    big_out = {n: None for n in BIG}
    padded = {pre + n: _pad_shard(n, state[pre + n]) for n in BIG for pre in ("", "m_", "v_")}
    wire = [BF16] * len(BIG) + [F32]
    pending = None
    for l in reversed(range(-1, DEPTH)):
        if l >= 0:
            dh, big, small = _layer_bwd(dh, mem0, layer_params[l], residuals[l])
            reduce = _GradReduce([big[n] for n in BIG] + [_pack_small_grads(small)], wire,
                                 hosts=REDUCE_HOSTS if l > 0 else None)
        done, pending = pending, (l, reduce) if l >= 0 else None
        if done is None:
            continue
        l_done, reduce_done = done
        *recv_big, recv_small = reduce_done.result()
        for n, recv in zip(BIG, recv_big):
            if big_out[n] is None:
                big_out[n] = [lax.empty((DEPTH,) + recv.shape[1:], F32) for _ in range(4)]
            big_out[n] = _adamw(recv, *[padded[pre + n] for pre in ("", "m_", "v_")], l_done, big_out[n],
                                name="adamw_" + n)
        outs[l_done] = recv_small

    assert not _RIDERS, [r.host for r in _RIDERS]
    packed = _adamw(jnp.concatenate(outs, axis=1), *[_pack_state(state, pre) for pre in ("", "m_", "v_")],
                    name="adamw_small")
    small_out = [_unpack_state(t) for t in packed]
    result = lambda n, j: _unpad_shard(n, big_out[n][j]) if n in BIG else small_out[j][n]
    return (loss, dh[None], *[result(n, j) for j in range(4) for n in WEIGHTS])
```

```python
import functools
import math

import jax
import jax.numpy as jnp
from jax import lax
from jax.experimental import pallas as pl
from jax.experimental.pallas import tpu as pltpu

F32, BF16 = jnp.float32, jnp.bfloat16

D_MODEL = 1024
DEPTH = 4
CHUNK = 64
FOX_HEADS, FOX_HEAD_DIM, FOX_WIDTH = 8, 64, 512
SSM_GROUP, SSM_WIDTH, SSM_GROUPS, SSM_STATE = 16, 512, 32, 64
CA_HEADS, CA_HEAD_DIM, CA_WIDTH, CA_LEFT_CHUNKS = 8, 64, 512, 8
CA_BAND = (CA_LEFT_CHUNKS + 1) * CHUNK
CA_PAD = CA_LEFT_CHUNKS * CHUNK
REL_MIN, REL_MAX = -(CHUNK - 1), 4 * CHUNK
N_REL = REL_MAX - REL_MIN + 1
XA_HEADS, XA_HEAD_DIM = 4, 256
D_FF = 2816
DN_ALPHA = (2 * DEPTH) ** 0.25
LN_EPS = 1e-5
NEG_INF = -1e30
ADAM_LR, ADAM_B1, ADAM_B2, ADAM_EPS, ADAM_WD, ADAM_STEP = 0.001, 0.9, 0.999, 1e-08, 0.01, 10

N_DEV = 8
LANE = 128
N_SSM_CH = SSM_GROUPS * SSM_STATE
SCAN_CB = 256
N_IN = 6664
W_IN_SHARD, W_IN_SLOT = N_IN // N_DEV, 896
Z_W = 7168
Z_FQ, Z_FK, Z_FV, Z_FF, Z_SU, Z_CQ, Z_CK, Z_CV, Z_GF, Z_GS, Z_GC = (
    0, 512, 1024, 1536, 2048, 2560, 3072, 3584, 4096, 5120, 6144)
Z_PIECES = ((0, 512, Z_FQ), (512, 512, Z_FK), (1024, 512, Z_FV), (1536, 8, Z_FF), (1544, 512, Z_SU),
            (2056, 512, Z_CQ), (2568, 512, Z_CK), (3080, 512, Z_CV), (3592, 1024, Z_GF), (4616, 1024, Z_GS),
            (5640, 1024, Z_GC))
FF_HALF, FF_HALF_P = D_FF // N_DEV, 384
D_FF_P = N_DEV * FF_HALF_P

VMEM_LIMIT_BYTES = 56 * 1024 * 1024

BIG = {"w_in": ((1024, W_IN_SHARD), 1), "w_fox_o": ((512, 128), 1), "w_ssm_glu": ((512, 256), 1),
       "w_ca_o": ((512, 128), 1), "w_o": ((128, 1024), 0), "xa_wq": ((128, 1024), 0), "xa_wkv": ((1024, 256), 1),
       "xa_wo": ((128, 1024), 0), "ffn_w_up": ((1024, 2 * FF_HALF), 1), "ffn_w_down": ((FF_HALF, 1024), 0)}
SMALL = {"ca_rel_bias": (8, 320), "ffn_conv_w": (3, D_FF), "ln_g": (3, 1024), "ln_b": (3, 1024)}
REPL = {"b_in": (N_IN,), "ssm_lambda_re": (32, 64), "ssm_lambda_im": (32, 64), "ssm_log_dt": (32,),
        "ssm_b_re": (32, 64, 16), "ssm_b_im": (32, 64, 16), "ssm_c_re": (32, 16, 64), "ssm_c_im": (32, 16, 64),
        "ssm_d": (512,), "ffn_conv_b": (D_FF,)}
WEIGHTS = ("w_in", "b_in", "ssm_lambda_re", "ssm_lambda_im", "ssm_log_dt", "ssm_b_re", "ssm_b_im", "ssm_c_re",
           "ssm_c_im", "ssm_d", "ca_rel_bias", "w_fox_o", "w_ssm_glu", "w_ca_o", "w_o", "xa_wq", "xa_wkv", "xa_wo",
           "ffn_w_up", "ffn_conv_w", "ffn_conv_b", "ffn_w_down", "ln_g", "ln_b")
LANES = 1024
PACK_ROWS = 256
SMALL_ROWS = 8


def _w_in_segments():
    segs = []
    for src, width, dst in Z_PIECES:
        n = src
        while n < src + width:
            d = n // W_IN_SHARD
            end = min(src + width, (d + 1) * W_IN_SHARD)
            segs.append((W_IN_SLOT * d + n - W_IN_SHARD * d, dst + n - src, end - n))
            n = end
    return tuple(segs)


W_IN_SEGS = _w_in_segments()


def _pallas(body, **kw):
    return pl.pallas_call(body, **kw)


def _params(sem):
    return pltpu.CompilerParams(dimension_semantics=sem, vmem_limit_bytes=VMEM_LIMIT_BYTES)


class _Rider:
    def __init__(self, host, ins, out_shapes, sem_shapes, start, finish, then=None):
        self.host, self.ins, self.out_shapes, self.sem_shapes = host, list(ins), list(out_shapes), list(sem_shapes)
        self.start, self.finish, self.then, self.results = start, finish, then, None


_RIDERS = []


def _pcall(body, *, name, **kw):
    rider = next((r for r in _RIDERS if r.host == name), None)
    if rider is None:
        return _pallas(body, name=name, **kw)
    _RIDERS.remove(rider)
    grid, in_specs, scratch = kw["grid"], list(kw["in_specs"]), list(kw.get("scratch_shapes", ()))
    single = not isinstance(kw["out_shape"], (list, tuple))
    out_specs = [kw["out_specs"]] if single else list(kw["out_specs"])
    out_shape = [kw["out_shape"]] if single else list(kw["out_shape"])
    n_in, n_out, n_scr = len(in_specs), len(out_specs), len(scratch)
    r_in, r_out = len(rider.ins), len(rider.out_shapes)

    def fused(*refs):
        a, ra = refs[:n_in], refs[n_in:n_in + r_in]
        o, ro = refs[n_in + r_in:n_in + r_in + n_out], refs[n_in + r_in + n_out:n_in + r_in + n_out + r_out]
        scr, sems = refs[n_in + r_in + n_out + r_out:][:n_scr], refs[n_in + r_in + n_out + r_out + n_scr:]
        ids = [pl.program_id(d) for d in range(len(grid))]
        first = functools.reduce(jnp.logical_and, [i == 0 for i in ids])
        last = functools.reduce(jnp.logical_and, [i == g - 1 for i, g in zip(ids, grid)])

        @pl.when(first)
        def _():
            rider.start(ra, ro, sems)

        body(*a, *o, *scr)

        @pl.when(last)
        def _():
            rider.finish(ra, ro, sems)

    hbm = pl.BlockSpec(memory_space=pl.ANY)
    call = _pallas(fused, name=name, grid=grid, in_specs=in_specs + [hbm] * r_in, out_specs=out_specs + [hbm] * r_out,
                   out_shape=out_shape + rider.out_shapes, scratch_shapes=scratch + rider.sem_shapes,
                   compiler_params=_params(("arbitrary",) * len(grid)))

    def run(*operands):
        outs = call(*operands, *rider.ins)
        rider.results = list(outs[n_out:])
        if rider.then is not None:
            rider.then(rider.results)
        return outs[0] if single else list(outs[:n_out])

    return run


def _pick(dim, prefs):
    for p in prefs:
        if dim % p == 0:
            return p
    return dim


def _mm(a, b, *, ta=False, tb=False, bias=None, add=None, a_off=0, a_cols=None, b_slots=False, out_slots=None,
        name, out_dtype=F32):
    a_cols = a_cols if a_cols is not None else a.shape[1]
    M, K = (a_cols, a.shape[0]) if ta else (a.shape[0], a_cols)
    tm = _pick(M, (1024, 512, 256, 128))
    if ta:
        tk = _pick(K, (2048, 1024, 512, 256))
    elif b_slots and tb:
        tk = _pick(b.shape[2], (1024, 768, 512, 256, 128))
    else:
        tk = K if K <= 3072 else _pick(K, (1024, 512, 256, 128))
    nk = K // tk
    if b_slots:
        ns = b.shape[2]
        if tb:
            N = b.shape[1]
            tn = _pick(N, (512, 256, 128))
            per = ns // tk
            b_spec = pl.BlockSpec((None, tn, tk), lambda i, j, k: (k // per, j, k % per))
            b_dim = 1
            assert N_DEV * ns == K
        else:
            N = N_DEV * ns
            tn = _pick(ns, (768, 512, 256, 128))
            per = ns // tn
            b_spec = pl.BlockSpec((None, tk, tn), lambda i, j, k: (j // per, k, j % per))
            b_dim = 0
            assert b.shape[1] == K
    else:
        N = b.shape[0] if tb else b.shape[1]
        assert (b.shape[1] if tb else b.shape[0]) == K, (a.shape, b.shape, ta, tb)
        tn = _pick(N, (512, 256, 128)) if out_slots is None else _pick(out_slots, (768, 512, 256, 128))
        if tb:
            b_spec = pl.BlockSpec((tn, tk), lambda i, j, k: (j, k))
            b_dim = 1
        else:
            b_spec = pl.BlockSpec((tk, tn), lambda i, j, k: (k, j))
            b_dim = 0
    if ta:
        assert a_off % tm == 0
        a_spec = pl.BlockSpec((tk, tm), lambda i, j, k: (k, i + a_off // tm))
        a_dim = 0
    else:
        assert a_off % tk == 0
        a_spec = pl.BlockSpec((tm, tk), lambda i, j, k: (i, k + a_off // tk))
        a_dim = 1
    cache_at = ta and nk == 1
    dims = (((1 if cache_at else a_dim,), (b_dim,)), ((), ()))
    ins, specs = [a, b], [a_spec, b_spec]
    if bias is not None:
        ins.append(bias)
        specs.append(pl.BlockSpec((1, tn), lambda i, j, k: (0, j)))
    add_scale = None
    if add is not None:
        ins.append(add[0])
        add_scale = add[1]
        specs.append(pl.BlockSpec((tm, tn), lambda i, j, k: (i, j)))
    if out_slots is None:
        out_spec = pl.BlockSpec((tm, tn), lambda i, j, k: (i, j))
        out_shape = jax.ShapeDtypeStruct((M, N), out_dtype)
    else:
        assert N == N_DEV * out_slots
        per_o = out_slots // tn
        out_spec = pl.BlockSpec((None, tm, tn), lambda i, j, k: (j // per_o, i, j % per_o))
        out_shape = jax.ShapeDtypeStruct((N_DEV, M, out_slots), out_dtype)

    def body(*refs):
        a_ref, b_ref = refs[0], refs[1]
        pos = 2
        bias_ref = add_ref = None
        if bias is not None:
            bias_ref = refs[pos]
            pos += 1
        if add is not None:
            add_ref = refs[pos]
            pos += 1
        o_ref = refs[pos]
        acc_ref = refs[pos + 1] if nk > 1 else None
        if cache_at:
            at_ref = refs[pos + 1]

            @pl.when(pl.program_id(1) == 0)
            def _():
                step = min(tk, 256)
                for c in range(0, tk, step):
                    at_ref[:, c:c + step] = a_ref[c:c + step, :].T.astype(BF16)

            lhs = at_ref[...]
        else:
            lhs = a_ref[...].astype(BF16)
        part = lax.dot_general(lhs, b_ref[...].astype(BF16), dims, preferred_element_type=F32)

        def finish(acc):
            if bias_ref is not None:
                acc = acc + bias_ref[...]
            if add_ref is not None:
                acc = acc + add_scale * add_ref[...]
            o_ref[...] = acc.astype(out_dtype)

        if nk == 1:
            finish(part)
        else:
            k = pl.program_id(2)

            @pl.when(k == 0)
            def _():
                acc_ref[...] = part

            @pl.when(k > 0)
            def _():
                acc_ref[...] += part

            @pl.when(k == nk - 1)
            def _():
                finish(acc_ref[...])

    return _pcall(
        body, name=name, grid=(M // tm, N // tn, nk), in_specs=specs, out_specs=out_spec, out_shape=out_shape,
        scratch_shapes=[pltpu.VMEM((tm, tn), F32)] if nk > 1 else [pltpu.VMEM((tm, tk), BF16)] if cache_at else [],
        compiler_params=_params(("parallel", "arbitrary", "arbitrary")),
    )(*ins)


SSM_BD = 4
SSM_BD_IN, SSM_BD_ST = SSM_WIDTH // SSM_BD, 2 * N_SSM_CH // SSM_BD


def _mm_bd(form, a, b, *, a_off=0, add=None, name):
    S = a.shape[0]
    off = a_off // SSM_BD_IN
    if form == "dw":
        def body(a_ref, b_ref, o_ref):
            at = a_ref[...].T.astype(BF16)
            o_ref[...] = lax.dot_general(at, b_ref[...].astype(BF16), _NN, preferred_element_type=F32)

        return _pcall(body, name=name, grid=(SSM_BD,),
                      in_specs=[pl.BlockSpec((S, SSM_BD_IN), lambda j: (0, off + j)),
                                pl.BlockSpec((S, SSM_BD_ST), lambda j: (0, j))],
                      out_specs=pl.BlockSpec((None, SSM_BD_IN, SSM_BD_ST), lambda j: (j, 0, 0)),
                      out_shape=jax.ShapeDtypeStruct((SSM_BD, SSM_BD_IN, SSM_BD_ST), F32),
                      compiler_params=_params(("parallel",)))(a, b)
    tm = _pick(S, (1024, 512, 256))
    wide, narrow = (SSM_BD_ST, SSM_BD_IN) if form == "in" else (SSM_BD_IN, SSM_BD_ST)
    dims = _NN if form == "in" else _NT

    def body(a_ref, w_ref, *rest):
        acc = lax.dot_general(a_ref[...].astype(BF16), w_ref[...].astype(BF16), dims, preferred_element_type=F32)
        if add is not None:
            acc = acc + add[1] * rest[0][...]
        rest[-1][...] = acc

    specs = [pl.BlockSpec((tm, narrow), lambda i, j: (i, off + j)),
             pl.BlockSpec((None, SSM_BD_IN, SSM_BD_ST), lambda i, j: (j, 0, 0))]
    ins = [a, b]
    if add is not None:
        specs.append(pl.BlockSpec((tm, wide), lambda i, j: (i, j)))
        ins.append(add[0])
    return _pcall(body, name=name, grid=(S // tm, SSM_BD), in_specs=specs,
                  out_specs=pl.BlockSpec((tm, wide), lambda i, j: (i, j)),
                  out_shape=jax.ShapeDtypeStruct((S, SSM_BD * wide), F32),
                  compiler_params=_params(("parallel", "parallel")))(*ins)


class Win:
    def __init__(self, arr, width, blk):
        self.arr, self.width, self.blk = arr, width, blk


def _rowwise(fn, rows, vecs=(), *, n_red=0, tr=256, name):
    wins = [r if isinstance(r, Win) else Win(r, r.shape[1], 0) for r in rows]
    S = wins[0].arr.shape[0]
    tr = min(tr, S)
    tile_args = [jax.ShapeDtypeStruct((tr, w.width), w.arr.dtype) for w in wins]
    tile_args += [jax.ShapeDtypeStruct(v.shape, v.dtype) for v in vecs]
    outs = jax.eval_shape(fn, *tile_args)
    n_row = len(outs) - n_red
    specs = [pl.BlockSpec((tr, w.width), functools.partial(lambda i, b: (i, b), b=w.blk)) for w in wins]
    specs += [pl.BlockSpec(v.shape, functools.partial(lambda i, nd: (0,) * nd, nd=v.ndim)) for v in vecs]
    out_specs = [pl.BlockSpec((tr, o.shape[1]), lambda i: (i, 0)) for o in outs[:n_row]]
    out_specs += [pl.BlockSpec(o.shape, functools.partial(lambda i, nd: (0,) * nd, nd=len(o.shape))) for o in outs[n_row:]]
    out_shape = [jax.ShapeDtypeStruct((S, o.shape[1]), o.dtype) for o in outs[:n_row]]
    out_shape += [jax.ShapeDtypeStruct(o.shape, o.dtype) for o in outs[n_row:]]
    n_in = len(wins) + len(vecs)

    def body(*refs):
        res = fn(*[r[...] for r in refs[:n_in]])
        o_refs = refs[n_in:]
        for o_ref, r in zip(o_refs[:n_row], res[:n_row]):
            o_ref[...] = r.astype(o_ref.dtype)
        i = pl.program_id(0)
        for o_ref, r in zip(o_refs[n_row:], res[n_row:]):
            @pl.when(i == 0)
            def _(o_ref=o_ref, r=r):
                o_ref[...] = r

            @pl.when(i > 0)
            def _(o_ref=o_ref, r=r):
                o_ref[...] += r

    return _pcall(
        body, name=name, grid=(S // tr,), in_specs=specs, out_specs=out_specs, out_shape=out_shape,
        compiler_params=_params(("arbitrary",)),
    )(*[w.arr for w in wins], *vecs)


def _whole(fn, *arrays, name):
    outs = jax.eval_shape(fn, *arrays)
    n_in = len(arrays)

    def body(*refs):
        res = fn(*[r[...] for r in refs[:n_in]])
        for o_ref, r in zip(refs[n_in:], res):
            o_ref[...] = r

    vm = pl.BlockSpec(memory_space=pltpu.VMEM)
    return _pcall(body, name=name, in_specs=[vm] * n_in, out_specs=[vm] * len(outs),
                  out_shape=[jax.ShapeDtypeStruct(o.shape, o.dtype) for o in outs])(*arrays)


def _split3(x):
    hi = x.astype(BF16)
    r = x - hi.astype(F32)
    mid = r.astype(BF16)
    lo = (r - mid.astype(F32)).astype(BF16)
    return hi, mid, lo


def _dot3(x, onehot, dims):
    return sum(lax.dot_general(t, onehot, dims, preferred_element_type=F32) for t in _split3(x))


_NT = (((1,), (1,)), ((), ()))
_NN = (((1,), (0,)), ((), ()))
_TN = (((0,), (0,)), ((), ()))


def _colmap(x, *, inverse, name):
    R = x.shape[0] if inverse else x.shape[1]
    tr = 256
    per = W_IN_SLOT // LANE
    n_out = N_DEV * per if inverse else Z_W // LANE
    segs = [(p, q, n) for q, p, n in W_IN_SEGS] if inverse else list(W_IN_SEGS)

    def body(x_ref, o_ref):
        ia = lax.broadcasted_iota(jnp.int32, (LANE, LANE), 0)
        ib = lax.broadcasted_iota(jnp.int32, (LANE, LANE), 1)

        def src_block(i):
            if inverse:
                return x_ref[:, i * LANE:(i + 1) * LANE]
            return x_ref[i // per, :, (i % per) * LANE:(i % per + 1) * LANE]

        for jb in range(n_out):
            acc = None
            for s0, d0, n in segs:
                lo, hi = max(d0, jb * LANE), min(d0 + n, (jb + 1) * LANE)
                if lo >= hi:
                    continue
                delta = d0 - s0
                for i in range((lo - delta) // LANE, (hi - delta - 1) // LANE + 1):
                    shift = jb * LANE - i * LANE - delta
                    sel = ((ia - ib == shift) & (ib >= lo - jb * LANE) & (ib < hi - jb * LANE)).astype(BF16)
                    blk = src_block(i)
                    part = _dot3(blk, sel, _NN) if inverse else lax.dot_general(blk, sel, _NN, preferred_element_type=F32)
                    acc = part if acc is None else acc + part
            if acc is None:
                acc = jnp.zeros((tr, LANE), F32)
            if inverse:
                o_ref[jb // per, :, (jb % per) * LANE:(jb % per + 1) * LANE] = acc
            else:
                o_ref[:, jb * LANE:(jb + 1) * LANE] = acc.astype(BF16)

    slot_spec = pl.BlockSpec((N_DEV, tr, W_IN_SLOT), lambda i: (0, i, 0))
    flat_spec = pl.BlockSpec((tr, Z_W), lambda i: (i, 0))
    if inverse:
        return _pcall(body, name=name, grid=(R // tr,), in_specs=[flat_spec], out_specs=slot_spec,
                      out_shape=jax.ShapeDtypeStruct((N_DEV, R, W_IN_SLOT), F32), compiler_params=_params(("parallel",)))(x)
    return _pcall(body, name=name, grid=(R // tr,), in_specs=[slot_spec], out_specs=flat_spec,
                  out_shape=jax.ShapeDtypeStruct((R, Z_W), BF16), compiler_params=_params(("parallel",)))(x)


def _log_sigmoid(x):
    return jnp.minimum(x, 0.0) - jnp.log(1.0 + jnp.exp(-jnp.abs(x)))


def _cum_heads(f, dcum=None, *, name):
    H, S = f.shape
    tn = min(512, S)
    rev = dcum is not None

    def body(*refs):
        j = pl.program_id(0)
        s_idx = lax.broadcasted_iota(jnp.int32, (S, tn), 0)
        t_idx = lax.broadcasted_iota(jnp.int32, (S, tn), 1) + j * tn
        if not rev:
            f_ref, o_ref = refs
            tri = (s_idx <= t_idx).astype(BF16)
            o_ref[...] = _dot3(_log_sigmoid(f_ref[...]), tri, _NN)
        else:
            fj_ref, d_ref, o_ref = refs
            tri = (s_idx >= t_idx).astype(BF16)
            o_ref[...] = _dot3(d_ref[...], tri, _NN) * jax.nn.sigmoid(-fj_ref[...])

    full = pl.BlockSpec((H, S), lambda j: (0, 0))
    blk = pl.BlockSpec((H, tn), lambda j: (0, j))
    ins, specs = ([f], [full]) if not rev else ([f, dcum], [blk, full])
    return _pcall(body, name=name, grid=(S // tn,), in_specs=specs, out_specs=blk,
                  out_shape=jax.ShapeDtypeStruct((H, S), F32), compiler_params=_params(("arbitrary",)))(*ins)


def _rel_onehot(qi, transposed):
    shape = (N_REL, CA_BAND) if transposed else (CA_BAND, N_REL)
    kk = lax.broadcasted_iota(jnp.int32, shape, 1 if transposed else 0)
    rr = lax.broadcasted_iota(jnp.int32, shape, 0 if transposed else 1)
    idx = jnp.clip(CA_PAD + qi - kk, REL_MIN, REL_MAX) - REL_MIN
    return (idx == rr).astype(BF16)


def _relbias_expand(rb):
    rows = 8

    def body(rb_ref, o_ref):
        for r in range(rows):
            o_ref[r] = _dot3(rb_ref[...], _rel_onehot(pl.program_id(0) * rows + r, True), _NN)

    return _pcall(body, name="relbias_expand", grid=(CHUNK // rows,),
                  in_specs=[pl.BlockSpec((CA_HEADS, N_REL), lambda q: (0, 0))],
                  out_specs=pl.BlockSpec((rows, CA_HEADS, CA_BAND), lambda q: (q, 0, 0)),
                  out_shape=jax.ShapeDtypeStruct((CHUNK, CA_HEADS, CA_BAND), F32),
                  compiler_params=_params(("arbitrary",)))(rb)


def _relbias_reduce(db):
    rows = 8

    def body(db_ref, o_ref):
        q = pl.program_id(0)
        part = sum(_dot3(db_ref[r], _rel_onehot(q * rows + r, False), _NN) for r in range(rows))

        @pl.when(q == 0)
        def _():
            o_ref[...] = part

        @pl.when(q > 0)
        def _():
            o_ref[...] += part

    return _pcall(body, name="relbias_reduce", grid=(CHUNK // rows,),
                  in_specs=[pl.BlockSpec((rows, CA_HEADS, CA_BAND), lambda q: (q, 0, 0))],
                  out_specs=pl.BlockSpec((CA_HEADS, N_REL), lambda q: (0, 0)),
                  out_shape=jax.ShapeDtypeStruct((CA_HEADS, N_REL), F32),
                  compiler_params=_params(("arbitrary",)))(db)


def _attn_cfg(mode, S):
    if mode == "fox":
        return min(256, S), FOX_HEAD_DIM ** -0.5
    if mode == "chunk":
        return CHUNK, CA_HEAD_DIM ** -0.5
    return min(512, S), XA_HEAD_DIM ** -0.5


def _visible(mode, i, tq, nk):
    if mode == "fox":
        row = lax.broadcasted_iota(jnp.int32, (tq, nk), 0) + i * tq
        return row >= lax.broadcasted_iota(jnp.int32, (tq, nk), 1)
    if mode == "chunk":
        return lax.broadcasted_iota(jnp.int32, (tq, nk), 1) + i * CHUNK >= CA_PAD
    return None


def _scores(mode, qs, kb, extra, visible):
    s = lax.dot_general(qs, kb, _NT, preferred_element_type=F32)
    if mode == "fox":
        cq, ck = extra
        s = jnp.where(visible, s + cq - ck, NEG_INF)
    elif mode == "chunk":
        (bias,) = extra
        s = jnp.where(visible, s + bias, NEG_INF)
    return s


class _AttnPlan:
    def __init__(self, mode, q, k, pp):
        self.mode, self.pp = mode, pp
        self.D, self.hpb, self.bw = (XA_HEAD_DIM, 1, XA_HEAD_DIM) if mode == "xa" else (64, 2, LANE)
        self.S, self.Sk = q[0].shape[0], k[0].shape[0]
        self.H = (XA_HEADS if mode == "xa" else FOX_HEADS)
        self.W = self.H * self.D
        self.gw = self.bw * pp
        self.hpg = self.hpb * pp
        self.tq, self.scale = _attn_cfg(mode, self.S)
        self.grid = (self.W // self.gw, self.S // self.tq)

    def rows(self, win):
        off = win[1] // self.gw
        return pl.BlockSpec((self.tq, self.gw), lambda g, i: (i, off + g))

    def cols(self, win):
        off = win[1] // self.gw
        return pl.BlockSpec((win[0].shape[0], self.gw), lambda g, i: (0, off + g))

    def extras(self):
        if self.mode == "fox":
            return [pl.BlockSpec((self.hpg, self.tq, 1), lambda g, i: (g, i, 0)),
                    pl.BlockSpec((self.hpg, 1, self.Sk), lambda g, i: (g, 0, 0))]
        if self.mode == "chunk":
            return [pl.BlockSpec((self.hpg, CHUNK, CA_BAND), lambda g, i: (g, 0, 0))]
        return []

    def per_row(self):
        return pl.BlockSpec((self.hpg, self.tq, 1), lambda g, i: (g, i, 0))

    def lanes(self, p):
        return slice(p * self.bw, (p + 1) * self.bw)

    def keys(self, i, ref, p, nk):
        if self.mode == "chunk":
            return ref[pl.ds(pl.multiple_of(i * CHUNK, CHUNK), CA_BAND), self.lanes(p)]
        return ref[0:nk, self.lanes(p)]

    def key_ranges(self, i, run):
        nq = self.grid[1]
        if self.mode != "fox" or nq % 4:
            return run(CA_BAND if self.mode == "chunk" else self.Sk)
        for part in range(4):
            lo, hi = part * nq // 4, (part + 1) * nq // 4
            pl.when((i >= lo) & (i < hi))(functools.partial(run, hi * self.tq))

    def head(self, x, hh):
        if self.hpb == 1:
            return x
        lane = lax.broadcasted_iota(jnp.int32, x.shape, 1)
        return jnp.where(lane // self.D == hh, x, jnp.zeros_like(x))


def _attn_fwd(mode, q, k, v, extra=(), *, name):
    pl_ = _AttnPlan(mode, q, k, pp=4)
    n_ex = len(extra)
    out = (None, 0)

    def body(*refs):
        q_ref, k_ref, v_ref = refs[:3]
        o_ref, lse_ref = refs[3 + n_ex:]
        i = pl.program_id(1)

        def run(nk):
            visible = _visible(mode, i, pl_.tq, nk)
            for p in range(pl_.pp):
                qs = q_ref[:, pl_.lanes(p)].astype(BF16) * pl_.scale
                kp = pl_.keys(i, k_ref, p, nk).astype(BF16)
                vp = pl_.keys(i, v_ref, p, nk).astype(BF16)
                acc = None
                for hh in range(pl_.hpb):
                    h = p * pl_.hpb + hh
                    ex = [r[h] for r in refs[3:3 + n_ex]]
                    if mode == "fox":
                        ex[1] = ex[1][:, 0:nk]
                    s = _scores(mode, pl_.head(qs, hh), kp, ex, visible)
                    m = jnp.max(s, axis=1, keepdims=True)
                    e = jnp.exp(s - m)
                    l = jnp.sum(e, axis=1, keepdims=True)
                    part = lax.dot_general((e * (1.0 / l)).astype(BF16), pl_.head(vp, hh), _NN,
                                           preferred_element_type=F32)
                    acc = part if acc is None else acc + part
                    lse_ref[h] = m + jnp.log(l)
                o_ref[:, pl_.lanes(p)] = acc

        pl_.key_ranges(i, run)

    return _pcall(
        body, name=name, grid=pl_.grid, in_specs=[pl_.rows(q), pl_.cols(k), pl_.cols(v)] + pl_.extras(),
        out_specs=[pl_.rows(out), pl_.per_row()],
        out_shape=[jax.ShapeDtypeStruct((pl_.S, pl_.W), F32), jax.ShapeDtypeStruct((pl_.H, pl_.S, 1), F32)],
        compiler_params=_params(("parallel", "arbitrary")),
    )(q[0], k[0], v[0], *extra)


def _attn_bwd(mode, q, k, v, lse, do, extra=(), *, name):
    pl_ = _AttnPlan(mode, q, k, pp=2 if mode == "fox" else 4)
    H, S, Sk, W = pl_.H, pl_.S, pl_.Sk, pl_.W
    n_ex = len(extra)
    out = (None, 0)
    kv_out = pl.BlockSpec((Sk, pl_.gw), lambda g, i: (0, g))
    ex_specs = pl_.extras()
    out_specs = [pl_.rows(out), kv_out, kv_out]
    out_shape = [jax.ShapeDtypeStruct((S, W), F32), jax.ShapeDtypeStruct((Sk, W), F32),
                 jax.ShapeDtypeStruct((Sk, W), F32)]
    if mode == "fox":
        out_specs += [pl_.per_row(), ex_specs[1]]
        out_shape += [jax.ShapeDtypeStruct((H, S, 1), F32), jax.ShapeDtypeStruct((H, 1, Sk), F32)]
    elif mode == "chunk":
        out_specs += [ex_specs[0]]
        out_shape += [jax.ShapeDtypeStruct((H, CHUNK, CA_BAND), F32)]

    def body(*refs):
        q_ref, k_ref, v_ref, lse_ref, do_ref = refs[:5]
        dq_ref, dk_ref, dv_ref = refs[5 + n_ex:8 + n_ex]
        rest = refs[8 + n_ex:]
        i = pl.program_id(1)

        @pl.when(i == 0)
        def _():
            dk_ref[...] = jnp.zeros_like(dk_ref)
            dv_ref[...] = jnp.zeros_like(dv_ref)
            if mode == "fox":
                rest[1][...] = jnp.zeros_like(rest[1])
            elif mode == "chunk":
                rest[0][...] = jnp.zeros_like(rest[0])

        def run(nk):
            visible = _visible(mode, i, pl_.tq, nk)
            for p in range(pl_.pp):
                lanes = pl_.lanes(p)
                qs = q_ref[:, lanes].astype(BF16) * pl_.scale
                kp = pl_.keys(i, k_ref, p, nk).astype(BF16)
                vp = pl_.keys(i, v_ref, p, nk).astype(BF16)
                dop = do_ref[:, lanes].astype(BF16)
                dq = dk_part = dv_part = None
                for hh in range(pl_.hpb):
                    h = p * pl_.hpb + hh
                    ex = [r[h] for r in refs[5:5 + n_ex]]
                    if mode == "fox":
                        ex[1] = ex[1][:, 0:nk]
                    qh, doh = pl_.head(qs, hh), pl_.head(dop, hh)
                    s = _scores(mode, qh, kp, ex, visible)
                    pr = jnp.exp(s - lse_ref[h])
                    dp = lax.dot_general(doh, vp, _NT, preferred_element_type=F32)
                    ds = pr * (dp - jnp.sum(dp * pr, axis=1, keepdims=True))
                    dsb = ds.astype(BF16)
                    parts = (lax.dot_general(dsb, pl_.head(kp, hh), _NN, preferred_element_type=F32) * pl_.scale,
                             lax.dot_general(dsb, qh, _TN, preferred_element_type=F32),
                             lax.dot_general(pr.astype(BF16), doh, _TN, preferred_element_type=F32))
                    dq, dk_part, dv_part = parts if dq is None else (dq + parts[0], dk_part + parts[1],
                                                                     dv_part + parts[2])
                    if mode == "chunk":
                        rest[0][h] += ds
                    if mode == "fox":
                        rest[0][h] = jnp.sum(ds, axis=1, keepdims=True)
                        rest[1][h, :, 0:nk] += -jnp.sum(ds, axis=0, keepdims=True)
                dq_ref[:, lanes] = dq
                if mode == "chunk":
                    win = pl.ds(pl.multiple_of(i * CHUNK, CHUNK), CA_BAND)
                    dk_ref[win, lanes] += dk_part
                    dv_ref[win, lanes] += dv_part
                else:
                    dk_ref[0:nk, lanes] += dk_part
                    dv_ref[0:nk, lanes] += dv_part

        pl_.key_ranges(i, run)

    return _pcall(
        body, name=name, grid=pl_.grid,
        in_specs=[pl_.rows(q), pl_.cols(k), pl_.cols(v), pl_.per_row(), pl_.rows(do)] + ex_specs,
        out_specs=out_specs, out_shape=out_shape,
        compiler_params=_params(("parallel", "arbitrary")),
    )(q[0], k[0], v[0], lse, do[0], *extra)


def _scan(x, a, h=None, *, name):
    S = x.shape[0]
    CB = SCAN_CB
    rev = h is not None
    n_grp = S // 8

    def body(*refs):
        if rev:
            x_ref, a_ref, h_ref, o_ref, da_ref = refs
        else:
            x_ref, a_ref, o_ref = refs
        ar = a_ref[:, :CB]
        ai = -a_ref[:, CB:] if rev else a_ref[:, CB:]
        zero = jnp.zeros((1, CB), F32)

        def group(g, carry):
            base = pl.multiple_of((n_grp - 1 - g) * 8 if rev else g * 8, 8)
            for j in (range(7, -1, -1) if rev else range(8)):
                t = base + j
                if rev:
                    hr, hi, dar, dai = carry
                else:
                    hr, hi = carry
                xr = x_ref[pl.ds(t, 1), :CB]
                xi = x_ref[pl.ds(t, 1), CB:]
                hr, hi = ar * hr - ai * hi + xr, ar * hi + ai * hr + xi
                o_ref[pl.ds(t, 1), :CB] = hr
                o_ref[pl.ds(t, 1), CB:] = hi
                if rev:
                    tp = jnp.maximum(t - 1, 0)
                    live = (t > 0).astype(F32)
                    pr = h_ref[pl.ds(tp, 1), :CB] * live
                    pi = h_ref[pl.ds(tp, 1), CB:] * live
                    carry = (hr, hi, dar + hr * pr + hi * pi, dai + hi * pr - hr * pi)
                else:
                    carry = (hr, hi)
            return carry

        if rev:
            _, _, dar, dai = lax.fori_loop(0, n_grp, group, (zero, zero, zero, zero))
            da_ref[:, :CB] = dar
            da_ref[:, CB:] = dai
        else:
            lax.fori_loop(0, n_grp, group, (zero, zero))

    big = pl.BlockSpec((S, 2 * CB), lambda c: (0, c))
    vec = pl.BlockSpec((1, 2 * CB), lambda c: (0, c))
    n_blk = x.shape[1] // (2 * CB)
    if rev:
        return _pcall(body, name=name, grid=(n_blk,), in_specs=[big, vec, big], out_specs=[big, vec],
                      out_shape=[jax.ShapeDtypeStruct(x.shape, F32), jax.ShapeDtypeStruct(a.shape, F32)],
                      compiler_params=_params(("parallel",)))(x, a, h)
    return _pcall(body, name=name, grid=(n_blk,), in_specs=[big, vec], out_specs=big,
                  out_shape=jax.ShapeDtypeStruct(x.shape, F32), compiler_params=_params(("parallel",)))(x, a)


def _ssm_prep1(lr_, li, ldt):
    lr = jnp.minimum(lr_, -1e-4)
    dt = jnp.exp(ldt)
    mag = jnp.exp(lr * dt)
    ar = mag * jnp.cos(li * dt)
    ai = mag * jnp.sin(li * dt)
    den = lr * lr + li * li
    gr = ((ar - 1.0) * lr + ai * li) / den
    gi = (ai * lr - (ar - 1.0) * li) / den
    return ar, ai, gr, gi


def _ssm_prep2(gr, gi, br, bi):
    return gr * br - gi * bi, gr * bi + gi * br


def _vjp_of(fn, n_in):
    def bwd(*args):
        cts = args[n_in:]
        return jax.vjp(fn, *args[:n_in])[1](cts[0] if len(cts) == 1 else tuple(cts))
    return bwd


def _to_blocked(r, i):
    lead = r.shape[:-1]
    t = jnp.stack([r.reshape(lead + (N_SSM_CH // SCAN_CB, SCAN_CB)), i.reshape(lead + (N_SSM_CH // SCAN_CB, SCAN_CB))],
                  axis=-2)
    return t.reshape(lead + (2 * N_SSM_CH,))


def _from_blocked(m):
    lead = m.shape[:-1]
    t = m.reshape(lead + (N_SSM_CH // SCAN_CB, 2, SCAN_CB))
    return t[..., 0, :].reshape(lead + (N_SSM_CH,)), t[..., 1, :].reshape(lead + (N_SSM_CH,))


_GPB = SSM_GROUPS // SSM_BD
_GPS = SCAN_CB // SSM_STATE


def _bd_eye():
    return jnp.eye(_GPB, dtype=F32).reshape(_GPB, _GPB // _GPS, _GPS)


def _blockdiag(r, i):
    v = jnp.stack([r, i]).reshape(2, SSM_BD, _GPB, SSM_GROUP, SSM_STATE)
    return jnp.einsum("qjgcp,gsh->jgcsqhp", v, _bd_eye()).reshape(SSM_BD, SSM_BD_IN, SSM_BD_ST)


def _blockdiag_inv(m):
    d = m.reshape(SSM_BD, _GPB, SSM_GROUP, _GPB // _GPS, 2, _GPS, SSM_STATE)
    v = jnp.einsum("jgcsqhp,gsh->qjgcp", d, _bd_eye()).reshape(2, SSM_GROUPS, SSM_GROUP, SSM_STATE)
    return v[0], v[1]


def _shift_rows(x, n):
    S = x.shape[0]
    row = lax.broadcasted_iota(jnp.int32, x.shape, 0)
    if n > 0:
        return jnp.where(row >= n, pltpu.roll(x, n, 0), 0.0)
    return jnp.where(row < S + n, pltpu.roll(x, S + n, 0), 0.0)


def _bf(x):
    return x.astype(BF16).astype(F32)


def _conv_pre(a, w, b):
    ab, wb = _bf(a), _bf(w)
    return wb[2:3] * ab + wb[1:2] * _shift_rows(ab, 1) + wb[0:1] * _shift_rows(ab, 2) + b


def _ffn_mid(up, conv_w, conv_b, dh=None, *, name):
    S = up.shape[0]
    tn = 2 * LANE
    nb = D_FF_P // tn
    rev = dh is not None

    def body(*refs):
        if not rev:
            a_ref, g_ref, w_ref, b_ref, o_ref = refs
            o_ref[...] = jax.nn.gelu(_conv_pre(a_ref[...], w_ref[...], b_ref[...])) * g_ref[...]
            return
        a_ref, g_ref, w_ref, b_ref, dh_ref, dup_a_ref, dup_g_ref, dw_ref, db_ref = refs
        a, w, dh_ = a_ref[...], w_ref[...], dh_ref[...]
        pre = _conv_pre(a, w, b_ref[...])
        gl, gelu_vjp = jax.vjp(jax.nn.gelu, pre)
        dup_g_ref[...] = dh_ * gl
        (dpre,) = gelu_vjp(dh_ * g_ref[...])
        db_ref[...] = jnp.sum(dpre, axis=0, keepdims=True)
        dpb, ab, wb = _bf(dpre), _bf(a), _bf(w)
        dup_a_ref[...] = wb[2:3] * dpb + wb[1:2] * _shift_rows(dpb, -1) + wb[0:1] * _shift_rows(dpb, -2)
        dw_ref[2:3, :] = jnp.sum(dpb * ab, axis=0, keepdims=True)
        dw_ref[1:2, :] = jnp.sum(dpb * _shift_rows(ab, 1), axis=0, keepdims=True)
        dw_ref[0:1, :] = jnp.sum(dpb * _shift_rows(ab, 2), axis=0, keepdims=True)

    a_spec = pl.BlockSpec((S, tn), lambda j: (0, j))
    g_spec = pl.BlockSpec((S, tn), lambda j: (0, j + nb))
    w_spec = pl.BlockSpec((3, tn), lambda j: (0, j))
    b_spec = pl.BlockSpec((1, tn), lambda j: (0, j))
    if not rev:
        return _pcall(body, name=name, grid=(nb,), in_specs=[a_spec, g_spec, w_spec, b_spec], out_specs=a_spec,
                      out_shape=jax.ShapeDtypeStruct((S, D_FF_P), F32), compiler_params=_params(("parallel",)))(
                          up, up, conv_w, conv_b)
    return _pcall(body, name=name, grid=(nb,), in_specs=[a_spec, g_spec, w_spec, b_spec, a_spec],
                  out_specs=[a_spec, a_spec, w_spec, b_spec],
                  out_shape=[jax.ShapeDtypeStruct((S, D_FF_P), F32), jax.ShapeDtypeStruct((S, D_FF_P), F32),
                             jax.ShapeDtypeStruct((3, D_FF_P), F32), jax.ShapeDtypeStruct((1, D_FF_P), F32)],
                  compiler_params=_params(("parallel",)))(up, up, conv_w, conv_b, dh)


def _ff_pad(t):
    lead = t.shape[:-1]
    t = t.reshape(lead + (N_DEV, FF_HALF))
    return jnp.pad(t, [(0, 0)] * len(lead) + [(0, 0), (0, FF_HALF_P - FF_HALF)]).reshape(lead + (D_FF_P,))


def _ff_unpad(t):
    lead = t.shape[:-1]
    return t.reshape(lead + (N_DEV, FF_HALF_P))[..., :FF_HALF].reshape(lead + (D_FF,))


def _ln_fwd(x, h, g, b):
    r = DN_ALPHA * x + h
    mu = jnp.mean(r, axis=-1, keepdims=True)
    var = jnp.mean(jnp.square(r - mu), axis=-1, keepdims=True)
    return r, (r - mu) * lax.rsqrt(var + LN_EPS) * g + b


def _ln_bwd(r, dy, g):
    mu = jnp.mean(r, axis=-1, keepdims=True)
    var = jnp.mean(jnp.square(r - mu), axis=-1, keepdims=True)
    xhat = (r - mu) * lax.rsqrt(var + LN_EPS)
    dxh = dy * g
    dr = lax.rsqrt(var + LN_EPS) * (dxh - jnp.mean(dxh, axis=-1, keepdims=True)
                                    - xhat * jnp.mean(dxh * xhat, axis=-1, keepdims=True))
    return dr, jnp.sum(dy * xhat, axis=0, keepdims=True), jnp.sum(dy, axis=0, keepdims=True)


def _merge(gf, gs, gc, ya, yb2, yc):
    yb = yb2[:, :D_MODEL] * jax.nn.sigmoid(yb2[:, D_MODEL:])
    return jax.nn.sigmoid(gf) * ya + jax.nn.sigmoid(gs) * yb + jax.nn.sigmoid(gc) * yc


def _s5_tail(hc, su, d):
    return jax.nn.gelu(hc + d * su)


def _s5_tail_bwd(hc, su, dgel, d):
    _, vjp = jax.vjp(jax.nn.gelu, hc + d * su)
    (dy,) = vjp(dgel)
    return dy, d * dy, jnp.sum(dy * su, axis=0, keepdims=True)


def _loss_rows(y, tgt):
    err = y - tgt
    return err * (1.0 / D_MODEL), jnp.sum(0.5 * jnp.square(err), axis=0, keepdims=True) * (1.0 / D_MODEL)


def _peer(k):
    x, y, c = lax.axis_index("x"), lax.axis_index("y"), lax.axis_index("c")
    return (x ^ ((k >> 2) & 1), y ^ ((k >> 1) & 1), c ^ (k & 1))


def _my_slot():
    return 4 * lax.axis_index("x") + 2 * lax.axis_index("y") + lax.axis_index("c")


def _peer_slot(k):
    px, py, pc = _peer(k)
    return 4 * px + 2 * py + pc


N_CHIP = N_DEV // 2
OTHER_CHIPS = (2, 4, 6)


def _chip_of(dev):
    return 2 * dev[0] + dev[1]


def _remote(src, dst, send, recv, dev):
    return pltpu.make_async_remote_copy(src_ref=src, dst_ref=dst, send_sem=send, recv_sem=recv, device_id=dev,
                                        device_id_type=pl.DeviceIdType.MESH)


def _all_gather(shards, *, name, host=None, then=None):
    return _exchange(shards, *_all_gather_parts(shards), name=name, host=host, then=then)


def _exchange(ins, out_shapes, sem_shapes, start, finish, *, name, host=None, then=None):
    if host is not None:
        rider = _Rider(host, ins, out_shapes, sem_shapes, start, finish, then)
        _RIDERS.append(rider)
        return rider
    n = len(ins)

    def body(*refs):
        start(refs[:n], refs[n:2 * n], refs[2 * n:])
        finish(refs[:n], refs[n:2 * n], refs[2 * n:])

    hbm = pl.BlockSpec(memory_space=pl.ANY)
    return _pcall(body, name=name, in_specs=[hbm] * n, out_specs=[hbm] * n, out_shape=list(out_shapes),
                  scratch_shapes=list(sem_shapes))(*ins)


def _all_gather_parts(shards):
    n = len(shards)

    def first_copies(ins, outs, sems):
        send, recv, _ = sems
        return [_remote(ins[t], outs[t].at[_my_slot()], send.at[t, k - 1], recv.at[t, k - 1], _peer(k))
                for k in (1,) + OTHER_CHIPS for t in range(n)]

    def local_copies(ins, outs, sems):
        return [pltpu.make_async_copy(ins[t], outs[t].at[_my_slot()], sems[2].at[t]) for t in range(n)]

    def start(ins, outs, sems):
        for cp in local_copies(ins, outs, sems) + first_copies(ins, outs, sems):
            cp.start()

    def finish(ins, outs, sems):
        send, recv, _ = sems
        sibling = _peer(1)
        passed = []
        for k in OTHER_CHIPS:
            for t in range(n):
                slot = outs[t].at[_peer_slot(k)]
                _remote(ins[t], slot, send.at[t, k - 1], recv.at[t, k - 1], _peer(k)).wait_recv()
                cp = _remote(slot, slot, send.at[t, k], recv.at[t, k], sibling)
                cp.start()
                passed.append(cp)
        for t in range(n):
            _remote(ins[t], outs[t].at[_peer_slot(1)], send.at[t, 0], recv.at[t, 0], sibling).wait_recv()
            for k in OTHER_CHIPS:
                _remote(ins[t], outs[t].at[_peer_slot(k + 1)], send.at[t, k], recv.at[t, k], sibling).wait_recv()
        for cp in first_copies(ins, outs, sems) + passed:
            cp.wait_send()
        for lc in local_copies(ins, outs, sems):
            lc.wait()

    out_shapes = [jax.ShapeDtypeStruct((N_DEV,) + s.shape, s.dtype) for s in shards]
    sem_shapes = [pltpu.SemaphoreType.DMA((n, N_DEV - 1)), pltpu.SemaphoreType.DMA((n, N_DEV - 1)),
                  pltpu.SemaphoreType.DMA((n,))]
    return out_shapes, sem_shapes, start, finish


def _sibling_swap(grads, *, name, host=None, then=None):
    n = len(grads)

    def copies(ins, outs, sems):
        c = lax.axis_index("c")
        return [_remote(ins[t].at[:, 1 - c], outs[t], sems[0].at[t], sems[1].at[t], _peer(1)) for t in range(n)]

    def start(ins, outs, sems):
        for cp in copies(ins, outs, sems):
            cp.start()

    def finish(ins, outs, sems):
        for cp in copies(ins, outs, sems):
            cp.wait()

    out_shapes = [jax.ShapeDtypeStruct((N_CHIP,) + g.shape[2:], g.dtype) for g in grads]
    sem_shapes = [pltpu.SemaphoreType.DMA((n,)), pltpu.SemaphoreType.DMA((n,))]
    return _exchange(grads, out_shapes, sem_shapes, start, finish, name=name, host=host, then=then)


def _pair_add(g, p, out_dtype, *, name):
    _, _, R, C = g.shape
    tr = _pick(R, (512, 256, 128, 64, 32, 16, 8))

    def body(c_ref, g_ref, p_ref, o_ref):
        o_ref[...] = (g_ref[...] + p_ref[...]).astype(out_dtype)

    grid_spec = pltpu.PrefetchScalarGridSpec(
        num_scalar_prefetch=1, grid=(N_CHIP, R // tr),
        in_specs=[pl.BlockSpec((None, None, tr, C), lambda j, i, c_ref: (j, c_ref[0], i, 0)),
                  pl.BlockSpec((None, tr, C), lambda j, i, c_ref: (j, i, 0))],
        out_specs=pl.BlockSpec((None, tr, C), lambda j, i, c_ref: (j, i, 0)))
    core = lax.axis_index("c").astype(jnp.int32).reshape(1)
    return _pcall(body, name=name, grid_spec=grid_spec, out_shape=jax.ShapeDtypeStruct(p.shape, out_dtype),
                  compiler_params=_params(("parallel", "parallel")))(core, g, p)


def _chip_exchange(sums, *, name, host=None):
    n = len(sums)

    def copies(ins, outs, sems, dst_is_mine):
        send, recv, _ = sems
        mine = 2 * lax.axis_index("x") + lax.axis_index("y")
        out = []
        for k in OTHER_CHIPS:
            theirs = _chip_of(_peer(k))
            for t in range(n):
                out.append(_remote(ins[t].at[theirs], outs[t].at[mine if dst_is_mine else theirs],
                                   send.at[t, k // 2 - 1], recv.at[t, k // 2 - 1], _peer(k)))
        return out

    def local_copies(ins, outs, sems):
        mine = 2 * lax.axis_index("x") + lax.axis_index("y")
        return [pltpu.make_async_copy(ins[t].at[mine], outs[t].at[mine], sems[2].at[t]) for t in range(n)]

    def start(ins, outs, sems):
        for cp in local_copies(ins, outs, sems) + copies(ins, outs, sems, True):
            cp.start()

    def finish(ins, outs, sems):
        for cp in copies(ins, outs, sems, False) + local_copies(ins, outs, sems):
            cp.wait()

    out_shapes = [jax.ShapeDtypeStruct(s.shape, s.dtype) for s in sums]
    sem_shapes = [pltpu.SemaphoreType.DMA((n, N_CHIP - 1)), pltpu.SemaphoreType.DMA((n, N_CHIP - 1)),
                  pltpu.SemaphoreType.DMA((n,))]
    return _exchange(sums, out_shapes, sem_shapes, start, finish, name=name, host=host)


class _GradReduce:
    def __init__(self, grads, wire_dtypes, hosts=None):
        pairs = [g.reshape((N_CHIP, 2) + g.shape[1:]) for g in grads]
        self.n, self.riders, self.recv = len(grads), [], None

        def after_swap(partner):
            sums = [_pair_add(g, p, dt, name="grad_pair_add") for g, p, dt in zip(pairs, partner, wire_dtypes)]
            if hosts is None:
                self.recv = _chip_exchange(sums, name="grad_chip_exchange")
            else:
                self.riders = [(idx, _chip_exchange([sums[i] for i in idx], name="grad_chip_exchange", host=h))
                               for h, idx in hosts[1]]

        if hosts is None:
            after_swap(_sibling_swap(pairs, name="grad_sibling_swap"))
        else:
            _sibling_swap(pairs, name="grad_sibling_swap", host=hosts[0], then=after_swap)

    def result(self):
        if self.recv is None:
            self.recv = [None] * self.n
            for idx, rider in self.riders:
                assert rider.results is not None, rider.host
                for i, r in zip(idx, rider.results):
                    self.recv[i] = r
        return self.recv


def _adamw(recv, w, m, v, layer=None, into=None, *, name):
    n_slots, R, C = recv.shape
    tr = _pick(R, (256, 128, 64, 32, 16, 8))

    def body(r_ref, w_ref, m_ref, v_ref, *rest):
        g_ref, d_ref, nm_ref, nv_ref = rest[-4:]
        g = r_ref[0].astype(F32)
        for s in range(1, n_slots):
            g = g + r_ref[s].astype(F32)
        m_new = ADAM_B1 * m_ref[...] + (1.0 - ADAM_B1) * g
        v_new = ADAM_B2 * v_ref[...] + (1.0 - ADAM_B2) * jnp.square(g)
        m_hat = m_new / (1.0 - ADAM_B1 ** ADAM_STEP)
        v_hat = v_new / (1.0 - ADAM_B2 ** ADAM_STEP)
        g_ref[...] = g
        d_ref[...] = -ADAM_LR * (m_hat / (jnp.sqrt(v_hat) + ADAM_EPS) + ADAM_WD * w_ref[...])
        nm_ref[...] = m_new
        nv_ref[...] = v_new

    row = pl.BlockSpec((tr, C), lambda i: (i, 0))
    state = row if layer is None else pl.BlockSpec((None, tr, C), lambda i: (layer, i, 0))
    in_specs = [pl.BlockSpec((n_slots, tr, C), lambda i: (0, i, 0)), state, state, state]
    if into is None:
        return _pcall(body, name=name, grid=(R // tr,), in_specs=in_specs, out_specs=[row] * 4,
                      out_shape=[jax.ShapeDtypeStruct((R, C), F32)] * 4, compiler_params=_params(("parallel",)),
                      )(recv, w, m, v)
    return _pcall(body, name=name, grid=(R // tr,), in_specs=in_specs + [pl.BlockSpec(memory_space=pl.ANY)] * 4,
                  out_specs=[state] * 4, out_shape=[jax.ShapeDtypeStruct((DEPTH, R, C), F32)] * 4,
                  input_output_aliases={4 + j: j for j in range(4)}, compiler_params=_params(("parallel",)),
                  )(recv, w, m, v, *into)


def _flat_pad(parts, rows):
    flat = jnp.concatenate([p.reshape(-1) for p in parts])
    return jnp.pad(flat, (0, rows * LANES - flat.shape[0])).reshape(rows, LANES)


def _small_shard_shape(n):
    return SMALL[n][:-1] + (SMALL[n][-1] // N_DEV,)


def _unpack_small(gathered):
    out, off = {}, 0
    flat = gathered.reshape(N_DEV, -1)
    for n in SMALL:
        r, c = _small_shard_shape(n)
        out[n] = flat[:, off:off + r * c].reshape(N_DEV, r, c).transpose(1, 0, 2).reshape(r, N_DEV * c)
        off += r * c
    return out


def _pack_state(state, prefix):
    flat = jnp.concatenate([state[prefix + n].reshape(DEPTH, -1) for n in (*SMALL, *REPL)], axis=1)
    return jnp.pad(flat, ((0, 0), (0, PACK_ROWS * LANES - flat.shape[1]))).reshape(DEPTH * PACK_ROWS, LANES)


def _pack_small_grads(grads):
    cols = []
    for n in SMALL:
        r, c = _small_shard_shape(n)
        cols.append(grads[n].reshape(r, N_DEV, c).transpose(1, 0, 2).reshape(N_DEV, r * c))
    cols += [jnp.broadcast_to(grads[n].reshape(1, -1), (N_DEV, grads[n].size)) for n in REPL]
    flat = jnp.concatenate(cols, axis=1)
    return jnp.pad(flat, ((0, 0), (0, PACK_ROWS * LANES - flat.shape[1]))).reshape(N_DEV, PACK_ROWS, LANES)


def _unpack_state(packed):
    out, off = {}, 0
    flat = packed.reshape(DEPTH, -1)
    for n, shape in [(n, _small_shard_shape(n)) for n in SMALL] + list(REPL.items()):
        sz = math.prod(shape)
        out[n] = flat[:, off:off + sz].reshape((DEPTH,) + shape)
        off += sz
    return out


def _pad_b_in(b):
    parts, pos = [], 0
    for src, width, dst in Z_PIECES:
        parts += [jnp.zeros((b.shape[0], dst - pos), b.dtype), b[:, src:src + width]]
        pos = dst + width
    return jnp.concatenate(parts + [jnp.zeros((b.shape[0], Z_W - pos), b.dtype)], axis=1)


def _unpad_b_in(bp):
    return jnp.concatenate([bp[:, dst:dst + width] for _, width, dst in Z_PIECES], axis=1)


def _up_shard_pad(t):
    gap = jnp.zeros(t.shape[:-1] + (FF_HALF_P - FF_HALF,), t.dtype)
    return jnp.concatenate([t[..., :FF_HALF], gap, t[..., FF_HALF:], gap], axis=-1)


def _up_shard_unpad(t):
    return jnp.concatenate([t[..., :FF_HALF], t[..., FF_HALF_P:FF_HALF_P + FF_HALF]], axis=-1)


def _pad_shard(n, t):
    lead = [(0, 0)] * (t.ndim - 2)
    if n == "w_in":
        return jnp.pad(t, lead + [(0, 0), (0, W_IN_SLOT - W_IN_SHARD)])
    if n == "ffn_w_up":
        return _up_shard_pad(t)
    if n == "ffn_w_down":
        return jnp.pad(t, lead + [(0, FF_HALF_P - FF_HALF), (0, 0)])
    return t


def _unpad_shard(n, t):
    if n == "w_in":
        return t[..., :W_IN_SHARD]
    if n == "ffn_w_up":
        return _up_shard_unpad(t)
    if n == "ffn_w_down":
        return t[..., :FF_HALF, :]
    return t


GATHER_HOSTS = (("chunk_fwd", (0,)), ("fox_fwd", (8,)), ("s5_scan", (9,)), ("mm_down", (1, 2, 3, 4)),
                ("mm_up", (5, 6, 7, 10)))
FIRST_GATHER_HOSTS = ((None, (0,)), ("mm_in", (1, 2, 3, 4, 5, 6, 7, 10)), ("s5_scan", (8, 9)))
SECOND_GATHER_HOSTS = (("fox_fwd", (0,)), ("chunk_fwd", (8, 9)), ("mm_up", (1, 2, 3, 4, 5, 6, 7, 10)))
REDUCE_HOSTS = ("mm_up_dw", (("fox_bwd", (0,)), ("s5_scan_bwd", (1, 2, 3, 4, 5, 6, 7, 10)), ("chunk_bwd", (8,)),
                             ("mm_in_dx", (9,))))


def _layer_shards(state, layer):
    shards = [_pad_shard(n, state[n][layer].astype(BF16)) for n in BIG]
    return shards + [_flat_pad([state[n][layer] for n in SMALL], SMALL_ROWS)]


class _Weights:
    def __init__(self, sources):
        self._sources, self._cache = sources, {}

    def _operand(self, i):
        src = self._sources[i]
        if isinstance(src, tuple):
            assert src[0].results is not None, src[0].host
            return src[0].results[src[1]]
        return src

    def __getitem__(self, n):
        if n not in self._cache:
            if n in SMALL:
                self._cache.update(_unpack_small(self._operand(len(BIG))))
            elif n == "w_in_p":
                self._cache[n] = _colmap(self._operand(0), inverse=False, name="w_in_colmap")
            else:
                w = self._operand(list(BIG).index(n))
                self._cache[n] = w.reshape(-1, D_MODEL) if BIG[n][1] == 0 else w
        return self._cache[n]


def _gather_riding(shards, hosts):
    sources = [None] * len(shards)
    for host, idx in hosts:
        group = [shards[i] for i in idx]
        if host is None:
            for i, g in zip(idx, _all_gather(group, name="all_gather_weights")):
                sources[i] = g
        else:
            rider = _all_gather(group, name="all_gather_weights", host=host)
            for pos, i in enumerate(idx):
                sources[i] = (rider, pos)
    return sources


def _ssm_params(p):
    prep1_in = (p["ssm_lambda_re"], p["ssm_lambda_im"], p["ssm_log_dt"][:, None])
    ar, ai, gr, gi = _whole(_ssm_prep1, *prep1_in, name="ssm_prep1")
    to_cn = lambda b: b.transpose(2, 0, 1).reshape(SSM_GROUP, N_SSM_CH)
    prep2_in = (gr.reshape(1, N_SSM_CH), gi.reshape(1, N_SSM_CH), to_cn(p["ssm_b_re"]), to_cn(p["ssm_b_im"]))
    bbr, bbi = _whole(_ssm_prep2, *prep2_in, name="ssm_prep2")
    to_gcp = lambda t: t.reshape(SSM_GROUP, SSM_GROUPS, SSM_STATE).transpose(1, 0, 2)
    bb = _blockdiag(to_gcp(bbr), to_gcp(bbi))
    cct = _blockdiag(p["ssm_c_re"], -p["ssm_c_im"])
    a_vec = _to_blocked(ar.reshape(1, N_SSM_CH), ai.reshape(1, N_SSM_CH))
    return dict(bb=bb, cct=cct, a_vec=a_vec, prep1_in=prep1_in, prep2_in=prep2_in)


def _layer_fwd(x, mem, p, W):
    sp = _ssm_params(p)
    z = _mm(x, W["w_in_p"], bias=p["b_in_p"], name="mm_in")
    f_t = z[:, Z_FF:Z_FF + FOX_HEADS].T
    cum = _cum_heads(f_t, name="fox_cum")
    ya_pre, lse_a = _attn_fwd("fox", (z, Z_FQ), (z, Z_FK), (z, Z_FV), (cum[:, :, None], cum[:, None, :]),
                              name="fox_fwd")
    ya = _mm(ya_pre, W["w_fox_o"], b_slots=True, name="mm_fox_o")
    x_ri = _mm_bd("in", z, sp["bb"], a_off=Z_SU, name="mm_s5_in")
    h_ri = _scan(x_ri, sp["a_vec"], name="s5_scan")
    hc = _mm_bd("out", h_ri, sp["cct"], name="mm_s5_out")
    d_row = p["ssm_d"][None, :]
    (gel,) = _rowwise(lambda a, b, c: (_s5_tail(a, b, c),), [hc, Win(z, 512, Z_SU // 512)], [d_row], name="s5_tail")
    yb2 = _mm(gel, W["w_ssm_glu"], b_slots=True, name="mm_glu")
    bias = _relbias_expand(W["ca_rel_bias"]).transpose(1, 0, 2)
    kv_band = jnp.pad(z[:, Z_CK:Z_CK + 2 * CA_WIDTH].astype(BF16), ((CA_PAD, 0), (0, 0)))
    yc_pre, lse_c = _attn_fwd("chunk", (z, Z_CQ), (kv_band, 0), (kv_band, CA_WIDTH), (bias,), name="chunk_fwd")
    yc = _mm(yc_pre, W["w_ca_o"], b_slots=True, name="mm_ca_o")
    gates = [Win(z, 1024, Z_GF // 1024), Win(z, 1024, Z_GS // 1024), Win(z, 1024, Z_GC // 1024)]
    (merged,) = _rowwise(lambda *a: (_merge(*a),), gates + [ya, yb2, yc], name="merge")
    h1 = _mm(merged, W["w_o"], name="mm_o")
    ln_g, ln_b = W["ln_g"], W["ln_b"]
    r1, x1 = _rowwise(_ln_fwd, [x, h1], [ln_g[0:1], ln_b[0:1]], tr=512, name="ln_fwd")
    q = _mm(x1, W["xa_wq"], name="mm_xq")
    kv = _mm(mem, W["xa_wkv"], b_slots=True, name="mm_xkv")
    o, lse_x = _attn_fwd("xa", (q, 0), (kv, 0), (kv, D_MODEL), name="xa_fwd")
    h2 = _mm(o, W["xa_wo"], name="mm_xo")
    r2, x2 = _rowwise(_ln_fwd, [x1, h2], [ln_g[1:2], ln_b[1:2]], tr=512, name="ln_fwd")
    up = _mm(x2, W["ffn_w_up"], b_slots=True, name="mm_up")
    hmid = _ffn_mid(up, _ff_pad(W["ffn_conv_w"]), _ff_pad(p["ffn_conv_b"][None, :]), name="ffn_mid")
    h3 = _mm(hmid, W["ffn_w_down"], name="mm_down")
    r3, x3 = _rowwise(_ln_fwd, [x2, h3], [ln_g[2:3], ln_b[2:3]], tr=512, name="ln_fwd")
    res = dict(x=x, z=z, cum=cum, lse_a=lse_a, ya_pre=ya_pre, ya=ya, h_ri=h_ri, hc=hc, gel=gel, yb2=yb2, lse_c=lse_c,
               yc_pre=yc_pre, yc=yc, merged=merged, r1=r1, x1=x1, q=q, kv=kv, o=o, lse_x=lse_x, r2=r2, x2=x2, up=up,
               hmid=hmid, r3=r3, bias=bias, sp=sp, kv_band=kv_band, W=W)
    return x3, res


def _layer_bwd(dx3, mem, p, res):
    W = res["W"]
    x, z = res["x"], res["z"]
    sp = res["sp"]
    ln_g = W["ln_g"]
    big, small = {}, {}
    slots = lambda t: t.reshape(N_DEV, -1, D_MODEL)
    dr3, dg2, db2 = _rowwise(_ln_bwd, [res["r3"], dx3], [ln_g[2:3]], n_red=2, tr=512, name="ln_bwd")
    dhmid = _mm(dr3, W["ffn_w_down"], tb=True, name="mm_down_dx")
    big["ffn_w_down"] = slots(_mm(res["hmid"], dr3, ta=True, name="mm_down_dw"))
    conv_w_p, conv_b_p = _ff_pad(W["ffn_conv_w"]), _ff_pad(p["ffn_conv_b"][None, :])
    dup_a, dup_g, dcw, dcb = _ffn_mid(res["up"], conv_w_p, conv_b_p, dhmid, name="ffn_mid_bwd")
    dup = jnp.concatenate([dup_a, dup_g], axis=1)
    small["ffn_conv_w"], small["ffn_conv_b"] = _ff_unpad(dcw), _ff_unpad(dcb)[0]
    dx2 = _mm(dup, W["ffn_w_up"], tb=True, b_slots=True, add=(dr3, DN_ALPHA), name="mm_up_dx")
    big["ffn_w_up"] = _mm(res["x2"], dup, ta=True, out_slots=2 * FF_HALF_P, name="mm_up_dw")
    dr2, dg1, db1 = _rowwise(_ln_bwd, [res["r2"], dx2], [ln_g[1:2]], n_red=2, tr=512, name="ln_bwd")
    do = _mm(dr2, W["xa_wo"], tb=True, name="mm_xo_dx")
    big["xa_wo"] = slots(_mm(res["o"], dr2, ta=True, name="mm_xo_dw"))
    kv = res["kv"]
    dq, dk, dv = _attn_bwd("xa", (res["q"], 0), (kv, 0), (kv, D_MODEL), res["lse_x"], (do, 0), name="xa_bwd")
    dkv = jnp.concatenate([dk, dv], axis=1)
    dx1 = _mm(dq, W["xa_wq"], tb=True, add=(dr2, DN_ALPHA), name="mm_xq_dx")
    big["xa_wq"] = slots(_mm(res["x1"], dq, ta=True, name="mm_xq_dw"))
    big["xa_wkv"] = _mm(mem, dkv, ta=True, out_slots=256, name="mm_xkv_dw")
    dr1, dg0, db0 = _rowwise(_ln_bwd, [res["r1"], dx1], [ln_g[0:1]], n_red=2, tr=512, name="ln_bwd")
    small["ln_g"] = jnp.concatenate([dg0, dg1, dg2], axis=0)
    small["ln_b"] = jnp.concatenate([db0, db1, db2], axis=0)
    dmerged = _mm(dr1, W["w_o"], tb=True, name="mm_o_dx")
    big["w_o"] = slots(_mm(res["merged"], dr1, ta=True, name="mm_o_dw"))
    gates = [Win(z, 1024, Z_GF // 1024), Win(z, 1024, Z_GS // 1024), Win(z, 1024, Z_GC // 1024)]
    dgf, dgs, dgc, dya, dyb2, dyc = _rowwise(_vjp_of(_merge, 6), gates + [res["ya"], res["yb2"], res["yc"], dmerged],
                                             name="merge_bwd")
    dya_pre = _mm(dya, W["w_fox_o"], tb=True, b_slots=True, name="mm_fox_o_dx")
    big["w_fox_o"] = _mm(res["ya_pre"], dya, ta=True, out_slots=128, name="mm_fox_o_dw")
    cum = res["cum"]
    dfq, dfk, dfv, dcq, dck = _attn_bwd("fox", (z, Z_FQ), (z, Z_FK), (z, Z_FV), res["lse_a"], (dya_pre, 0),
                                        (cum[:, :, None], cum[:, None, :]), name="fox_bwd")
    f_t = z[:, Z_FF:Z_FF + FOX_HEADS].T
    dff = _cum_heads(f_t, dcq[:, :, 0] + dck[:, 0, :], name="fox_cum_bwd")
    dgel = _mm(dyb2, W["w_ssm_glu"], tb=True, b_slots=True, name="mm_glu_dx")
    big["w_ssm_glu"] = _mm(res["gel"], dyb2, ta=True, out_slots=256, name="mm_glu_dw")
    d_row = p["ssm_d"][None, :]
    su_win = Win(z, 512, Z_SU // 512)
    dy, dsu1, dd = _rowwise(_s5_tail_bwd, [res["hc"], su_win, dgel], [d_row], n_red=1, name="s5_tail_bwd")
    small["ssm_d"] = dd[0]
    dh_ri = _mm_bd("in", dy, sp["cct"], name="mm_s5_out_dx")
    dcct = _mm_bd("dw", dy, res["h_ri"], name="mm_s5_out_dw")
    dx_ri, da_vec = _scan(dh_ri, sp["a_vec"], res["h_ri"], name="s5_scan_bwd")
    dsu = _mm_bd("out", dx_ri, sp["bb"], add=(dsu1, 1.0), name="mm_s5_in_dx")
    dbb = _mm_bd("dw", z, dx_ri, a_off=Z_SU, name="mm_s5_in_dw")
    dcr, dci = _blockdiag_inv(dcct)
    small["ssm_c_re"], small["ssm_c_im"] = dcr, -dci
    dbbr, dbbi = _blockdiag_inv(dbb)
    to_cn = lambda t: t.transpose(1, 0, 2).reshape(SSM_GROUP, N_SSM_CH)
    dgr, dgi, dbr, dbi = _whole(_vjp_of(_ssm_prep2, 4), *sp["prep2_in"], to_cn(dbbr), to_cn(dbbi), name="ssm_prep2_bwd")
    from_cn = lambda t: t.reshape(SSM_GROUP, SSM_GROUPS, SSM_STATE).transpose(1, 2, 0)
    small["ssm_b_re"], small["ssm_b_im"] = from_cn(dbr), from_cn(dbi)
    dar, dai = _from_blocked(da_vec)
    sq = lambda t: t.reshape(SSM_GROUPS, SSM_STATE)
    dlr, dli, dldt = _whole(_vjp_of(_ssm_prep1, 3), *sp["prep1_in"], sq(dar), sq(dai), sq(dgr), sq(dgi),
                            name="ssm_prep1_bwd")
    small["ssm_lambda_re"], small["ssm_lambda_im"], small["ssm_log_dt"] = dlr, dli, dldt[:, 0]
    dyc_pre = _mm(dyc, W["w_ca_o"], tb=True, b_slots=True, name="mm_ca_o_dx")
    big["w_ca_o"] = _mm(res["yc_pre"], dyc, ta=True, out_slots=128, name="mm_ca_o_dw")
    bias = res["bias"]
    kv_band = res["kv_band"]
    dcq_, dck_band, dcv_band, dbias = _attn_bwd("chunk", (z, Z_CQ), (kv_band, 0), (kv_band, CA_WIDTH), res["lse_c"],
                                                (dyc_pre, 0), (bias,), name="chunk_bwd")
    small["ca_rel_bias"] = _relbias_reduce(dbias.transpose(1, 0, 2))
    dff_p = jnp.pad(dff.T, ((0, 0), (0, 512 - FOX_HEADS)))
    dz = jnp.concatenate([dfq, dfk, dfv, dff_p, dsu, dcq_, dck_band[CA_PAD:], dcv_band[CA_PAD:], dgf, dgs, dgc],
                         axis=1)
    dx = _mm(dz, W["w_in_p"], tb=True, add=(dr1, DN_ALPHA), name="mm_in_dx")
    big["w_in"] = _colmap(_mm(x, dz, ta=True, name="mm_in_dw"), inverse=True, name="w_in_colmap_inv")
    (db_in_p,) = _rowwise(lambda t: (jnp.sum(t, axis=0, keepdims=True),), [dz], n_red=1, name="colsum")
    small["b_in"] = _unpad_b_in(db_in_p)[0]
    return dx, big, small


def kernel(x, mem, w_in, b_in, ssm_lambda_re, ssm_lambda_im, ssm_log_dt, ssm_b_re, ssm_b_im, ssm_c_re, ssm_c_im, ssm_d, ca_rel_bias, w_fox_o, w_ssm_glu, w_ca_o, w_o, xa_wq, xa_wkv, xa_wo, ffn_w_up, ffn_conv_w, ffn_conv_b, ffn_w_down, ln_g, ln_b, loss_target, m_w_in, m_b_in, m_ssm_lambda_re, m_ssm_lambda_im, m_ssm_log_dt, m_ssm_b_re, m_ssm_b_im, m_ssm_c_re, m_ssm_c_im, m_ssm_d, m_ca_rel_bias, m_w_fox_o, m_w_ssm_glu, m_w_ca_o, m_w_o, m_xa_wq, m_xa_wkv, m_xa_wo, m_ffn_w_up, m_ffn_conv_w, m_ffn_conv_b, m_ffn_w_down, m_ln_g, m_ln_b, v_w_in, v_b_in, v_ssm_lambda_re, v_ssm_lambda_im, v_ssm_log_dt, v_ssm_b_re, v_ssm_b_im, v_ssm_c_re, v_ssm_c_im, v_ssm_d, v_ca_rel_bias, v_w_fox_o, v_w_ssm_glu, v_w_ca_o, v_w_o, v_xa_wq, v_xa_wkv, v_xa_wo, v_ffn_w_up, v_ffn_conv_w, v_ffn_conv_b, v_ffn_w_down, v_ln_g, v_ln_b):
    given = dict(locals())
    state = {pre + n: given[pre + n] for n in WEIGHTS for pre in ("", "m_", "v_")}
    mem0 = mem[0]
    b_in_p = _pad_b_in(b_in)
    layer_params = [{**{n: state[n][l] for n in REPL}, "b_in_p": b_in_p[l:l + 1]} for l in range(DEPTH)]

    _RIDERS.clear()
    h, residuals = x[0], []
    sources = _gather_riding(_layer_shards(state, 0), FIRST_GATHER_HOSTS)
    for l in range(DEPTH):
        hosts = SECOND_GATHER_HOSTS if l == 0 else GATHER_HOSTS
        following = _gather_riding(_layer_shards(state, l + 1), hosts) if l + 1 < DEPTH else None
        h, res = _layer_fwd(h, mem0, layer_params[l], _Weights(sources))
        residuals.append(res)
        sources = following
    dh, loss_cols = _rowwise(_loss_rows, [h, loss_target[0]], n_red=1, name="loss")
    loss = lax.psum(jnp.sum(loss_cols), ("x", "y", "c"))

    outs = [None] * DEPTH
    big_out = {n: None for n in BIG}
    padded = {pre + n: _pad_shard(n, state[pre + n]) for n in BIG for pre in ("", "m_", "v_")}
    wire = [BF16] * len(BIG) + [F32]
    pending = None
    for l in reversed(range(-1, DEPTH)):
        if l >= 0:
            dh, big, small = _layer_bwd(dh, mem0, layer_params[l], residuals[l])
            reduce = _GradReduce([big[n] for n in BIG] + [_pack_small_grads(small)], wire,
                                 hosts=REDUCE_HOSTS if l > 0 else None)
        done, pending = pending, (l, reduce) if l >= 0 else None
        if done is None:
            continue
        l_done, reduce_done = done
        *recv_big, recv_small = reduce_done.result()
        for n, recv in zip(BIG, recv_big):
            if big_out[n] is None:
                big_out[n] = [lax.empty((DEPTH,) + recv.shape[1:], F32) for _ in range(4)]
            big_out[n] = _adamw(recv, *[padded[pre + n] for pre in ("", "m_", "v_")], l_done, big_out[n],
                                name="adamw_" + n)
        outs[l_done] = recv_small

    assert not _RIDERS, [r.host for r in _RIDERS]
    packed = _adamw(jnp.concatenate(outs, axis=1), *[_pack_state(state, pre) for pre in ("", "m_", "v_")],
                    name="adamw_small")
    small_out = [_unpack_state(t) for t in packed]
    result = lambda n, j: _unpad_shard(n, big_out[n][j]) if n in BIG else small_out[j][n]
    return (loss, dh[None], *[result(n, j) for j in range(4) for n in WEIGHTS])
```

```python
import functools
import math

import jax
import jax.numpy as jnp
from jax import lax
from jax.experimental import pallas as pl
from jax.experimental.pallas import tpu as pltpu

F32, BF16 = jnp.float32, jnp.bfloat16

D_MODEL = 1024
DEPTH = 4
CHUNK = 64
FOX_HEADS, FOX_HEAD_DIM, FOX_WIDTH = 8, 64, 512
SSM_GROUP, SSM_WIDTH, SSM_GROUPS, SSM_STATE = 16, 512, 32, 64
CA_HEADS, CA_HEAD_DIM, CA_WIDTH, CA_LEFT_CHUNKS = 8, 64, 512, 8
CA_BAND = (CA_LEFT_CHUNKS + 1) * CHUNK
CA_PAD = CA_LEFT_CHUNKS * CHUNK
REL_MIN, REL_MAX = -(CHUNK - 1), 4 * CHUNK
N_REL = REL_MAX - REL_MIN + 1
XA_HEADS, XA_HEAD_DIM = 4, 256
D_FF = 2816
DN_ALPHA = (2 * DEPTH) ** 0.25
LN_EPS = 1e-5
NEG_INF = -1e30
ADAM_LR, ADAM_B1, ADAM_B2, ADAM_EPS, ADAM_WD, ADAM_STEP = 0.001, 0.9, 0.999, 1e-08, 0.01, 10

N_DEV = 8
LANE = 128
N_SSM_CH = SSM_GROUPS * SSM_STATE
SCAN_CB = 256
N_IN = 6664
W_IN_SHARD, W_IN_SLOT = N_IN // N_DEV, 896
Z_W = 7168
Z_FQ, Z_FK, Z_FV, Z_FF, Z_SU, Z_CQ, Z_CK, Z_CV, Z_GF, Z_GS, Z_GC = (
    0, 512, 1024, 1536, 2048, 2560, 3072, 3584, 4096, 5120, 6144)
Z_PIECES = ((0, 512, Z_FQ), (512, 512, Z_FK), (1024, 512, Z_FV), (1536, 8, Z_FF), (1544, 512, Z_SU),
            (2056, 512, Z_CQ), (2568, 512, Z_CK), (3080, 512, Z_CV), (3592, 1024, Z_GF), (4616, 1024, Z_GS),
            (5640, 1024, Z_GC))
FF_HALF, FF_HALF_P = D_FF // N_DEV, 384
D_FF_P = N_DEV * FF_HALF_P

VMEM_LIMIT_BYTES = 56 * 1024 * 1024

BIG = {"w_in": ((1024, W_IN_SHARD), 1), "w_fox_o": ((512, 128), 1), "w_ssm_glu": ((512, 256), 1),
       "w_ca_o": ((512, 128), 1), "w_o": ((128, 1024), 0), "xa_wq": ((128, 1024), 0), "xa_wkv": ((1024, 256), 1),
       "xa_wo": ((128, 1024), 0), "ffn_w_up": ((1024, 2 * FF_HALF), 1), "ffn_w_down": ((FF_HALF, 1024), 0)}
SMALL = {"ca_rel_bias": (8, 320), "ffn_conv_w": (3, D_FF), "ln_g": (3, 1024), "ln_b": (3, 1024)}
REPL = {"b_in": (N_IN,), "ssm_lambda_re": (32, 64), "ssm_lambda_im": (32, 64), "ssm_log_dt": (32,),
        "ssm_b_re": (32, 64, 16), "ssm_b_im": (32, 64, 16), "ssm_c_re": (32, 16, 64), "ssm_c_im": (32, 16, 64),
        "ssm_d": (512,), "ffn_conv_b": (D_FF,)}
WEIGHTS = ("w_in", "b_in", "ssm_lambda_re", "ssm_lambda_im", "ssm_log_dt", "ssm_b_re", "ssm_b_im", "ssm_c_re",
           "ssm_c_im", "ssm_d", "ca_rel_bias", "w_fox_o", "w_ssm_glu", "w_ca_o", "w_o", "xa_wq", "xa_wkv", "xa_wo",
           "ffn_w_up", "ffn_conv_w", "ffn_conv_b", "ffn_w_down", "ln_g", "ln_b")
LANES = 1024
PACK_ROWS = 256
SMALL_ROWS = 8


def _w_in_segments():
    segs = []
    for src, width, dst in Z_PIECES:
        n = src
        while n < src + width:
            d = n // W_IN_SHARD
            end = min(src + width, (d + 1) * W_IN_SHARD)
            segs.append((W_IN_SLOT * d + n - W_IN_SHARD * d, dst + n - src, end - n))
            n = end
    return tuple(segs)


W_IN_SEGS = _w_in_segments()


def _pallas(body, **kw):
    return pl.pallas_call(body, **kw)


def _params(sem):
    return pltpu.CompilerParams(dimension_semantics=sem, vmem_limit_bytes=VMEM_LIMIT_BYTES)


class _Rider:
    def __init__(self, host, ins, out_shapes, sem_shapes, start, finish, then=None):
        self.host, self.ins, self.out_shapes, self.sem_shapes = host, list(ins), list(out_shapes), list(sem_shapes)
        self.start, self.finish, self.then, self.results = start, finish, then, None


_RIDERS = []


def _pcall(body, *, name, **kw):
    rider = next((r for r in _RIDERS if r.host == name), None)
    if rider is None:
        return _pallas(body, name=name, **kw)
    _RIDERS.remove(rider)
    grid, in_specs, scratch = kw["grid"], list(kw["in_specs"]), list(kw.get("scratch_shapes", ()))
    single = not isinstance(kw["out_shape"], (list, tuple))
    out_specs = [kw["out_specs"]] if single else list(kw["out_specs"])
    out_shape = [kw["out_shape"]] if single else list(kw["out_shape"])
    n_in, n_out, n_scr = len(in_specs), len(out_specs), len(scratch)
    r_in, r_out = len(rider.ins), len(rider.out_shapes)

    def fused(*refs):
        a, ra = refs[:n_in], refs[n_in:n_in + r_in]
        o, ro = refs[n_in + r_in:n_in + r_in + n_out], refs[n_in + r_in + n_out:n_in + r_in + n_out + r_out]
        scr, sems = refs[n_in + r_in + n_out + r_out:][:n_scr], refs[n_in + r_in + n_out + r_out + n_scr:]
        ids = [pl.program_id(d) for d in range(len(grid))]
        first = functools.reduce(jnp.logical_and, [i == 0 for i in ids])
        last = functools.reduce(jnp.logical_and, [i == g - 1 for i, g in zip(ids, grid)])

        @pl.when(first)
        def _():
            rider.start(ra, ro, sems)

        body(*a, *o, *scr)

        @pl.when(last)
        def _():
            rider.finish(ra, ro, sems)

    hbm = pl.BlockSpec(memory_space=pl.ANY)
    call = _pallas(fused, name=name, grid=grid, in_specs=in_specs + [hbm] * r_in, out_specs=out_specs + [hbm] * r_out,
                   out_shape=out_shape + rider.out_shapes, scratch_shapes=scratch + rider.sem_shapes,
                   compiler_params=_params(("arbitrary",) * len(grid)))

    def run(*operands):
        outs = call(*operands, *rider.ins)
        rider.results = list(outs[n_out:])
        if rider.then is not None:
            rider.then(rider.results)
        return outs[0] if single else list(outs[:n_out])

    return run


def _pick(dim, prefs):
    for p in prefs:
        if dim % p == 0:
            return p
    return dim


def _mm(a, b, *, ta=False, tb=False, bias=None, add=None, a_off=0, a_cols=None, b_slots=False, out_slots=None,
        name, out_dtype=F32):
    a_cols = a_cols if a_cols is not None else a.shape[1]
    M, K = (a_cols, a.shape[0]) if ta else (a.shape[0], a_cols)
    tm = _pick(M, (1024, 512, 256, 128))
    if ta:
        tk = _pick(K, (2048, 1024, 512, 256))
    elif b_slots and tb:
        tk = _pick(b.shape[2], (1024, 768, 512, 256, 128))
    else:
        tk = K if K <= 3072 else _pick(K, (1024, 512, 256, 128))
    nk = K // tk
    if b_slots:
        ns = b.shape[2]
        if tb:
            N = b.shape[1]
            tn = _pick(N, (512, 256, 128))
            per = ns // tk
            b_spec = pl.BlockSpec((None, tn, tk), lambda i, j, k: (k // per, j, k % per))
            b_dim = 1
            assert N_DEV * ns == K
        else:
            N = N_DEV * ns
            tn = _pick(ns, (768, 512, 256, 128))
            per = ns // tn
            b_spec = pl.BlockSpec((None, tk, tn), lambda i, j, k: (j // per, k, j % per))
            b_dim = 0
            assert b.shape[1] == K
    else:
        N = b.shape[0] if tb else b.shape[1]
        assert (b.shape[1] if tb else b.shape[0]) == K, (a.shape, b.shape, ta, tb)
        tn = _pick(N, (512, 256, 128)) if out_slots is None else _pick(out_slots, (768, 512, 256, 128))
        if tb:
            b_spec = pl.BlockSpec((tn, tk), lambda i, j, k: (j, k))
            b_dim = 1
        else:
            b_spec = pl.BlockSpec((tk, tn), lambda i, j, k: (k, j))
            b_dim = 0
    if ta:
        assert a_off % tm == 0
        a_spec = pl.BlockSpec((tk, tm), lambda i, j, k: (k, i + a_off // tm))
        a_dim = 0
    else:
        assert a_off % tk == 0
        a_spec = pl.BlockSpec((tm, tk), lambda i, j, k: (i, k + a_off // tk))
        a_dim = 1
    cache_at = ta and nk == 1
    dims = (((1 if cache_at else a_dim,), (b_dim,)), ((), ()))
    ins, specs = [a, b], [a_spec, b_spec]
    if bias is not None:
        ins.append(bias)
        specs.append(pl.BlockSpec((1, tn), lambda i, j, k: (0, j)))
    add_scale = None
    if add is not None:
        ins.append(add[0])
        add_scale = add[1]
        specs.append(pl.BlockSpec((tm, tn), lambda i, j, k: (i, j)))
    if out_slots is None:
        out_spec = pl.BlockSpec((tm, tn), lambda i, j, k: (i, j))
        out_shape = jax.ShapeDtypeStruct((M, N), out_dtype)
    else:
        assert N == N_DEV * out_slots
        per_o = out_slots // tn
        out_spec = pl.BlockSpec((None, tm, tn), lambda i, j, k: (j // per_o, i, j % per_o))
        out_shape = jax.ShapeDtypeStruct((N_DEV, M, out_slots), out_dtype)

    def body(*refs):
        a_ref, b_ref = refs[0], refs[1]
        pos = 2
        bias_ref = add_ref = None
        if bias is not None:
            bias_ref = refs[pos]
            pos += 1
        if add is not None:
            add_ref = refs[pos]
            pos += 1
        o_ref = refs[pos]
        acc_ref = refs[pos + 1] if nk > 1 else None
        if cache_at:
            at_ref = refs[pos + 1]

            @pl.when(pl.program_id(1) == 0)
            def _():
                step = min(tk, 256)
                for c in range(0, tk, step):
                    at_ref[:, c:c + step] = a_ref[c:c + step, :].astype(F32).T.astype(BF16)

            lhs = at_ref[...]
        else:
            lhs = a_ref[...].astype(BF16)
        part = lax.dot_general(lhs, b_ref[...].astype(BF16), dims, preferred_element_type=F32)

        def finish(acc):
            if bias_ref is not None:
                acc = acc + bias_ref[...]
            if add_ref is not None:
                acc = acc + add_scale * add_ref[...]
            o_ref[...] = acc.astype(out_dtype)

        if nk == 1:
            finish(part)
        else:
            k = pl.program_id(2)

            @pl.when(k == 0)
            def _():
                acc_ref[...] = part

            @pl.when(k > 0)
            def _():
                acc_ref[...] += part

            @pl.when(k == nk - 1)
            def _():
                finish(acc_ref[...])

    return _pcall(
        body, name=name, grid=(M // tm, N // tn, nk), in_specs=specs, out_specs=out_spec, out_shape=out_shape,
        scratch_shapes=[pltpu.VMEM((tm, tn), F32)] if nk > 1 else [pltpu.VMEM((tm, tk), BF16)] if cache_at else [],
        compiler_params=_params(("parallel", "arbitrary", "arbitrary")),
    )(*ins)


SSM_BD = 4
SSM_BD_IN, SSM_BD_ST = SSM_WIDTH // SSM_BD, 2 * N_SSM_CH // SSM_BD


def _mm_bd(form, a, b, *, a_off=0, add=None, name):
    S = a.shape[0]
    off = a_off // SSM_BD_IN
    if form == "dw":
        def body(a_ref, b_ref, o_ref):
            at = a_ref[...].T.astype(BF16)
            o_ref[...] = lax.dot_general(at, b_ref[...].astype(BF16), _NN, preferred_element_type=F32)

        return _pcall(body, name=name, grid=(SSM_BD,),
                      in_specs=[pl.BlockSpec((S, SSM_BD_IN), lambda j: (0, off + j)),
                                pl.BlockSpec((S, SSM_BD_ST), lambda j: (0, j))],
                      out_specs=pl.BlockSpec((None, SSM_BD_IN, SSM_BD_ST), lambda j: (j, 0, 0)),
                      out_shape=jax.ShapeDtypeStruct((SSM_BD, SSM_BD_IN, SSM_BD_ST), F32),
                      compiler_params=_params(("parallel",)))(a, b)
    tm = _pick(S, (1024, 512, 256))
    wide, narrow = (SSM_BD_ST, SSM_BD_IN) if form == "in" else (SSM_BD_IN, SSM_BD_ST)
    dims = _NN if form == "in" else _NT

    def body(a_ref, w_ref, *rest):
        acc = lax.dot_general(a_ref[...].astype(BF16), w_ref[...].astype(BF16), dims, preferred_element_type=F32)
        if add is not None:
            acc = acc + add[1] * rest[0][...]
        rest[-1][...] = acc

    specs = [pl.BlockSpec((tm, narrow), lambda i, j: (i, off + j)),
             pl.BlockSpec((None, SSM_BD_IN, SSM_BD_ST), lambda i, j: (j, 0, 0))]
    ins = [a, b]
    if add is not None:
        specs.append(pl.BlockSpec((tm, wide), lambda i, j: (i, j)))
        ins.append(add[0])
    return _pcall(body, name=name, grid=(S // tm, SSM_BD), in_specs=specs,
                  out_specs=pl.BlockSpec((tm, wide), lambda i, j: (i, j)),
                  out_shape=jax.ShapeDtypeStruct((S, SSM_BD * wide), F32),
                  compiler_params=_params(("parallel", "parallel")))(*ins)


class Win:
    def __init__(self, arr, width, blk):
        self.arr, self.width, self.blk = arr, width, blk


def _rowwise(fn, rows, vecs=(), *, n_red=0, tr=256, name):
    wins = [r if isinstance(r, Win) else Win(r, r.shape[1], 0) for r in rows]
    S = wins[0].arr.shape[0]
    tr = min(tr, S)
    tile_args = [jax.ShapeDtypeStruct((tr, w.width), w.arr.dtype) for w in wins]
    tile_args += [jax.ShapeDtypeStruct(v.shape, v.dtype) for v in vecs]
    outs = jax.eval_shape(fn, *tile_args)
    n_row = len(outs) - n_red
    specs = [pl.BlockSpec((tr, w.width), functools.partial(lambda i, b: (i, b), b=w.blk)) for w in wins]
    specs += [pl.BlockSpec(v.shape, functools.partial(lambda i, nd: (0,) * nd, nd=v.ndim)) for v in vecs]
    out_specs = [pl.BlockSpec((tr, o.shape[1]), lambda i: (i, 0)) for o in outs[:n_row]]
    out_specs += [pl.BlockSpec(o.shape, functools.partial(lambda i, nd: (0,) * nd, nd=len(o.shape))) for o in outs[n_row:]]
    out_shape = [jax.ShapeDtypeStruct((S, o.shape[1]), o.dtype) for o in outs[:n_row]]
    out_shape += [jax.ShapeDtypeStruct(o.shape, o.dtype) for o in outs[n_row:]]
    n_in = len(wins) + len(vecs)

    def body(*refs):
        res = fn(*[r[...] for r in refs[:n_in]])
        o_refs = refs[n_in:]
        for o_ref, r in zip(o_refs[:n_row], res[:n_row]):
            o_ref[...] = r.astype(o_ref.dtype)
        i = pl.program_id(0)
        for o_ref, r in zip(o_refs[n_row:], res[n_row:]):
            @pl.when(i == 0)
            def _(o_ref=o_ref, r=r):
                o_ref[...] = r

            @pl.when(i > 0)
            def _(o_ref=o_ref, r=r):
                o_ref[...] += r

    return _pcall(
        body, name=name, grid=(S // tr,), in_specs=specs, out_specs=out_specs, out_shape=out_shape,
        compiler_params=_params(("arbitrary",)),
    )(*[w.arr for w in wins], *vecs)


def _whole(fn, *arrays, name):
    outs = jax.eval_shape(fn, *arrays)
    n_in = len(arrays)

    def body(*refs):
        res = fn(*[r[...] for r in refs[:n_in]])
        for o_ref, r in zip(refs[n_in:], res):
            o_ref[...] = r

    vm = pl.BlockSpec(memory_space=pltpu.VMEM)
    return _pcall(body, name=name, in_specs=[vm] * n_in, out_specs=[vm] * len(outs),
                  out_shape=[jax.ShapeDtypeStruct(o.shape, o.dtype) for o in outs])(*arrays)


def _split3(x):
    hi = x.astype(BF16)
    r = x - hi.astype(F32)
    mid = r.astype(BF16)
    lo = (r - mid.astype(F32)).astype(BF16)
    return hi, mid, lo


def _dot3(x, onehot, dims):
    return sum(lax.dot_general(t, onehot, dims, preferred_element_type=F32) for t in _split3(x))


_NT = (((1,), (1,)), ((), ()))
_NN = (((1,), (0,)), ((), ()))
_TN = (((0,), (0,)), ((), ()))


def _colmap(x, *, inverse, name):
    R = x.shape[0] if inverse else x.shape[1]
    tr = 256
    per = W_IN_SLOT // LANE
    n_out = N_DEV * per if inverse else Z_W // LANE
    segs = [(p, q, n) for q, p, n in W_IN_SEGS] if inverse else list(W_IN_SEGS)

    def body(x_ref, o_ref):
        ia = lax.broadcasted_iota(jnp.int32, (LANE, LANE), 0)
        ib = lax.broadcasted_iota(jnp.int32, (LANE, LANE), 1)

        def src_block(i):
            if inverse:
                return x_ref[:, i * LANE:(i + 1) * LANE]
            return x_ref[i // per, :, (i % per) * LANE:(i % per + 1) * LANE]

        for jb in range(n_out):
            acc = None
            for s0, d0, n in segs:
                lo, hi = max(d0, jb * LANE), min(d0 + n, (jb + 1) * LANE)
                if lo >= hi:
                    continue
                delta = d0 - s0
                for i in range((lo - delta) // LANE, (hi - delta - 1) // LANE + 1):
                    shift = jb * LANE - i * LANE - delta
                    sel = ((ia - ib == shift) & (ib >= lo - jb * LANE) & (ib < hi - jb * LANE)).astype(BF16)
                    blk = src_block(i)
                    part = _dot3(blk, sel, _NN) if inverse else lax.dot_general(blk, sel, _NN, preferred_element_type=F32)
                    acc = part if acc is None else acc + part
            if acc is None:
                acc = jnp.zeros((tr, LANE), F32)
            if inverse:
                o_ref[jb // per, :, (jb % per) * LANE:(jb % per + 1) * LANE] = acc
            else:
                o_ref[:, jb * LANE:(jb + 1) * LANE] = acc.astype(BF16)

    slot_spec = pl.BlockSpec((N_DEV, tr, W_IN_SLOT), lambda i: (0, i, 0))
    flat_spec = pl.BlockSpec((tr, Z_W), lambda i: (i, 0))
    if inverse:
        return _pcall(body, name=name, grid=(R // tr,), in_specs=[flat_spec], out_specs=slot_spec,
                      out_shape=jax.ShapeDtypeStruct((N_DEV, R, W_IN_SLOT), F32), compiler_params=_params(("parallel",)))(x)
    return _pcall(body, name=name, grid=(R // tr,), in_specs=[slot_spec], out_specs=flat_spec,
                  out_shape=jax.ShapeDtypeStruct((R, Z_W), BF16), compiler_params=_params(("parallel",)))(x)


def _log_sigmoid(x):
    return jnp.minimum(x, 0.0) - jnp.log(1.0 + jnp.exp(-jnp.abs(x)))


def _cum_heads(f, dcum=None, *, name):
    H, S = f.shape
    tn = min(512, S)
    rev = dcum is not None

    def body(*refs):
        j = pl.program_id(0)
        s_idx = lax.broadcasted_iota(jnp.int32, (S, tn), 0)
        t_idx = lax.broadcasted_iota(jnp.int32, (S, tn), 1) + j * tn
        if not rev:
            f_ref, o_ref = refs
            tri = (s_idx <= t_idx).astype(BF16)
            o_ref[...] = _dot3(_log_sigmoid(f_ref[...]), tri, _NN)
        else:
            fj_ref, d_ref, o_ref = refs
            tri = (s_idx >= t_idx).astype(BF16)
            o_ref[...] = _dot3(d_ref[...], tri, _NN) * jax.nn.sigmoid(-fj_ref[...])

    full = pl.BlockSpec((H, S), lambda j: (0, 0))
    blk = pl.BlockSpec((H, tn), lambda j: (0, j))
    ins, specs = ([f], [full]) if not rev else ([f, dcum], [blk, full])
    return _pcall(body, name=name, grid=(S // tn,), in_specs=specs, out_specs=blk,
                  out_shape=jax.ShapeDtypeStruct((H, S), F32), compiler_params=_params(("arbitrary",)))(*ins)


def _rel_onehot(qi, transposed):
    shape = (N_REL, CA_BAND) if transposed else (CA_BAND, N_REL)
    kk = lax.broadcasted_iota(jnp.int32, shape, 1 if transposed else 0)
    rr = lax.broadcasted_iota(jnp.int32, shape, 0 if transposed else 1)
    idx = jnp.clip(CA_PAD + qi - kk, REL_MIN, REL_MAX) - REL_MIN
    return (idx == rr).astype(BF16)


def _relbias_expand(rb):
    rows = 8

    def body(rb_ref, o_ref):
        for r in range(rows):
            o_ref[r] = _dot3(rb_ref[...], _rel_onehot(pl.program_id(0) * rows + r, True), _NN)

    return _pcall(body, name="relbias_expand", grid=(CHUNK // rows,),
                  in_specs=[pl.BlockSpec((CA_HEADS, N_REL), lambda q: (0, 0))],
                  out_specs=pl.BlockSpec((rows, CA_HEADS, CA_BAND), lambda q: (q, 0, 0)),
                  out_shape=jax.ShapeDtypeStruct((CHUNK, CA_HEADS, CA_BAND), F32),
                  compiler_params=_params(("arbitrary",)))(rb)


def _relbias_reduce(db):
    rows = 8

    def body(db_ref, o_ref):
        q = pl.program_id(0)
        part = sum(_dot3(db_ref[r], _rel_onehot(q * rows + r, False), _NN) for r in range(rows))

        @pl.when(q == 0)
        def _():
            o_ref[...] = part

        @pl.when(q > 0)
        def _():
            o_ref[...] += part

    return _pcall(body, name="relbias_reduce", grid=(CHUNK // rows,),
                  in_specs=[pl.BlockSpec((rows, CA_HEADS, CA_BAND), lambda q: (q, 0, 0))],
                  out_specs=pl.BlockSpec((CA_HEADS, N_REL), lambda q: (0, 0)),
                  out_shape=jax.ShapeDtypeStruct((CA_HEADS, N_REL), F32),
                  compiler_params=_params(("arbitrary",)))(db)


def _attn_cfg(mode, S):
    if mode == "fox":
        return min(256, S), FOX_HEAD_DIM ** -0.5
    if mode == "chunk":
        return CHUNK, CA_HEAD_DIM ** -0.5
    return min(512, S), XA_HEAD_DIM ** -0.5


def _visible(mode, i, tq, nk):
    if mode == "fox":
        row = lax.broadcasted_iota(jnp.int32, (tq, nk), 0) + i * tq
        return row >= lax.broadcasted_iota(jnp.int32, (tq, nk), 1)
    if mode == "chunk":
        return lax.broadcasted_iota(jnp.int32, (tq, nk), 1) + i * CHUNK >= CA_PAD
    return None


def _scores(mode, qs, kb, extra, visible):
    s = lax.dot_general(qs, kb, _NT, preferred_element_type=F32)
    if mode == "fox":
        cq, ck = extra
        s = jnp.where(visible, s + cq - ck, NEG_INF)
    elif mode == "chunk":
        (bias,) = extra
        s = jnp.where(visible, s + bias, NEG_INF)
    return s


class _AttnPlan:
    def __init__(self, mode, q, k, pp):
        self.mode, self.pp = mode, pp
        self.D, self.hpb, self.bw = (XA_HEAD_DIM, 1, XA_HEAD_DIM) if mode == "xa" else (64, 2, LANE)
        self.S, self.Sk = q[0].shape[0], k[0].shape[0]
        self.H = (XA_HEADS if mode == "xa" else FOX_HEADS)
        self.W = self.H * self.D
        self.gw = self.bw * pp
        self.hpg = self.hpb * pp
        self.tq, self.scale = _attn_cfg(mode, self.S)
        self.grid = (self.W // self.gw, self.S // self.tq)

    def rows(self, win):
        off = win[1] // self.gw
        return pl.BlockSpec((self.tq, self.gw), lambda g, i: (i, off + g))

    def cols(self, win):
        off = win[1] // self.gw
        return pl.BlockSpec((win[0].shape[0], self.gw), lambda g, i: (0, off + g))

    def extras(self):
        if self.mode == "fox":
            return [pl.BlockSpec((self.hpg, self.tq, 1), lambda g, i: (g, i, 0)),
                    pl.BlockSpec((self.hpg, 1, self.Sk), lambda g, i: (g, 0, 0))]
        if self.mode == "chunk":
            return [pl.BlockSpec((self.hpg, CHUNK, CA_BAND), lambda g, i: (g, 0, 0))]
        return []

    def per_row(self):
        return pl.BlockSpec((self.hpg, self.tq, 1), lambda g, i: (g, i, 0))

    def lanes(self, p):
        return slice(p * self.bw, (p + 1) * self.bw)

    def keys(self, i, ref, p, nk):
        if self.mode == "chunk":
            return ref[pl.ds(pl.multiple_of(i * CHUNK, CHUNK), CA_BAND), self.lanes(p)]
        return ref[0:nk, self.lanes(p)]

    def key_ranges(self, i, run):
        nq = self.grid[1]
        if self.mode != "fox" or nq % 4:
            return run(CA_BAND if self.mode == "chunk" else self.Sk)
        for part in range(4):
            lo, hi = part * nq // 4, (part + 1) * nq // 4
            pl.when((i >= lo) & (i < hi))(functools.partial(run, hi * self.tq))

    def head(self, x, hh):
        if self.hpb == 1:
            return x
        lane = lax.broadcasted_iota(jnp.int32, x.shape, 1)
        return jnp.where(lane // self.D == hh, x, jnp.zeros_like(x))


def _attn_fwd(mode, q, k, v, extra=(), *, name):
    pl_ = _AttnPlan(mode, q, k, pp=4)
    n_ex = len(extra)
    out = (None, 0)

    def body(*refs):
        q_ref, k_ref, v_ref = refs[:3]
        o_ref, lse_ref = refs[3 + n_ex:]
        i = pl.program_id(1)

        def run(nk):
            visible = _visible(mode, i, pl_.tq, nk)
            for p in range(pl_.pp):
                qs = q_ref[:, pl_.lanes(p)].astype(BF16) * pl_.scale
                kp = pl_.keys(i, k_ref, p, nk).astype(BF16)
                vp = pl_.keys(i, v_ref, p, nk).astype(BF16)
                acc = None
                for hh in range(pl_.hpb):
                    h = p * pl_.hpb + hh
                    ex = [r[h] for r in refs[3:3 + n_ex]]
                    if mode == "fox":
                        ex[1] = ex[1][:, 0:nk]
                    s = _scores(mode, pl_.head(qs, hh), kp, ex, visible)
                    m = jnp.max(s, axis=1, keepdims=True)
                    e = jnp.exp(s - m)
                    l = jnp.sum(e, axis=1, keepdims=True)
                    part = lax.dot_general((e * (1.0 / l)).astype(BF16), pl_.head(vp, hh), _NN,
                                           preferred_element_type=F32)
                    acc = part if acc is None else acc + part
                    lse_ref[h] = m + jnp.log(l)
                o_ref[:, pl_.lanes(p)] = acc

        pl_.key_ranges(i, run)

    return _pcall(
        body, name=name, grid=pl_.grid, in_specs=[pl_.rows(q), pl_.cols(k), pl_.cols(v)] + pl_.extras(),
        out_specs=[pl_.rows(out), pl_.per_row()],
        out_shape=[jax.ShapeDtypeStruct((pl_.S, pl_.W), F32), jax.ShapeDtypeStruct((pl_.H, pl_.S, 1), F32)],
        compiler_params=_params(("parallel", "arbitrary")),
    )(q[0], k[0], v[0], *extra)


def _attn_bwd(mode, q, k, v, lse, do, extra=(), *, name):
    pl_ = _AttnPlan(mode, q, k, pp=2 if mode == "fox" else 4)
    H, S, Sk, W = pl_.H, pl_.S, pl_.Sk, pl_.W
    n_ex = len(extra)
    out = (None, 0)
    kv_out = pl.BlockSpec((Sk, pl_.gw), lambda g, i: (0, g))
    ex_specs = pl_.extras()
    out_specs = [pl_.rows(out), kv_out, kv_out]
    out_shape = [jax.ShapeDtypeStruct((S, W), F32), jax.ShapeDtypeStruct((Sk, W), F32),
                 jax.ShapeDtypeStruct((Sk, W), F32)]
    if mode == "fox":
        out_specs += [pl_.per_row(), ex_specs[1]]
        out_shape += [jax.ShapeDtypeStruct((H, S, 1), F32), jax.ShapeDtypeStruct((H, 1, Sk), F32)]
    elif mode == "chunk":
        out_specs += [ex_specs[0]]
        out_shape += [jax.ShapeDtypeStruct((H, CHUNK, CA_BAND), F32)]

    def body(*refs):
        q_ref, k_ref, v_ref, lse_ref, do_ref = refs[:5]
        dq_ref, dk_ref, dv_ref = refs[5 + n_ex:8 + n_ex]
        rest = refs[8 + n_ex:]
        i = pl.program_id(1)

        @pl.when(i == 0)
        def _():
            dk_ref[...] = jnp.zeros_like(dk_ref)
            dv_ref[...] = jnp.zeros_like(dv_ref)
            if mode == "fox":
                rest[1][...] = jnp.zeros_like(rest[1])
            elif mode == "chunk":
                rest[0][...] = jnp.zeros_like(rest[0])

        def run(nk):
            visible = _visible(mode, i, pl_.tq, nk)
            for p in range(pl_.pp):
                lanes = pl_.lanes(p)
                qs = q_ref[:, lanes].astype(BF16) * pl_.scale
                kp = pl_.keys(i, k_ref, p, nk).astype(BF16)
                vp = pl_.keys(i, v_ref, p, nk).astype(BF16)
                dop = do_ref[:, lanes].astype(BF16)
                dq = dk_part = dv_part = None
                for hh in range(pl_.hpb):
                    h = p * pl_.hpb + hh
                    ex = [r[h] for r in refs[5:5 + n_ex]]
                    if mode == "fox":
                        ex[1] = ex[1][:, 0:nk]
                    qh, doh = pl_.head(qs, hh), pl_.head(dop, hh)
                    s = _scores(mode, qh, kp, ex, visible)
                    pr = jnp.exp(s - lse_ref[h])
                    dp = lax.dot_general(doh, vp, _NT, preferred_element_type=F32)
                    ds = pr * (dp - jnp.sum(dp * pr, axis=1, keepdims=True))
                    dsb = ds.astype(BF16)
                    parts = (lax.dot_general(dsb, pl_.head(kp, hh), _NN, preferred_element_type=F32) * pl_.scale,
                             lax.dot_general(dsb, qh, _TN, preferred_element_type=F32),
                             lax.dot_general(pr.astype(BF16), doh, _TN, preferred_element_type=F32))
                    dq, dk_part, dv_part = parts if dq is None else (dq + parts[0], dk_part + parts[1],
                                                                     dv_part + parts[2])
                    if mode == "chunk":
                        rest[0][h] += ds
                    if mode == "fox":
                        rest[0][h] = jnp.sum(ds, axis=1, keepdims=True)
                        rest[1][h, :, 0:nk] += -jnp.sum(ds, axis=0, keepdims=True)
                dq_ref[:, lanes] = dq
                if mode == "chunk":
                    win = pl.ds(pl.multiple_of(i * CHUNK, CHUNK), CA_BAND)
                    dk_ref[win, lanes] += dk_part
                    dv_ref[win, lanes] += dv_part
                else:
                    dk_ref[0:nk, lanes] += dk_part
                    dv_ref[0:nk, lanes] += dv_part

        pl_.key_ranges(i, run)

    return _pcall(
        body, name=name, grid=pl_.grid,
        in_specs=[pl_.rows(q), pl_.cols(k), pl_.cols(v), pl_.per_row(), pl_.rows(do)] + ex_specs,
        out_specs=out_specs, out_shape=out_shape,
        compiler_params=_params(("parallel", "arbitrary")),
    )(q[0], k[0], v[0], lse, do[0], *extra)


def _scan(x, a, h=None, *, name):
    S = x.shape[0]
    CB = SCAN_CB
    rev = h is not None
    n_grp = S // 8

    def body(*refs):
        if rev:
            x_ref, a_ref, h_ref, o_ref, da_ref = refs
        else:
            x_ref, a_ref, o_ref = refs
        ar = a_ref[:, :CB]
        ai = -a_ref[:, CB:] if rev else a_ref[:, CB:]
        zero = jnp.zeros((1, CB), F32)

        def group(g, carry):
            base = pl.multiple_of((n_grp - 1 - g) * 8 if rev else g * 8, 8)
            for j in (range(7, -1, -1) if rev else range(8)):
                t = base + j
                if rev:
                    hr, hi, dar, dai = carry
                else:
                    hr, hi = carry
                xr = x_ref[pl.ds(t, 1), :CB]
                xi = x_ref[pl.ds(t, 1), CB:]
                hr, hi = ar * hr - ai * hi + xr, ar * hi + ai * hr + xi
                o_ref[pl.ds(t, 1), :CB] = hr
                o_ref[pl.ds(t, 1), CB:] = hi
                if rev:
                    tp = jnp.maximum(t - 1, 0)
                    live = (t > 0).astype(F32)
                    pr = h_ref[pl.ds(tp, 1), :CB] * live
                    pi = h_ref[pl.ds(tp, 1), CB:] * live
                    carry = (hr, hi, dar + hr * pr + hi * pi, dai + hi * pr - hr * pi)
                else:
                    carry = (hr, hi)
            return carry

        if rev:
            _, _, dar, dai = lax.fori_loop(0, n_grp, group, (zero, zero, zero, zero))
            da_ref[:, :CB] = dar
            da_ref[:, CB:] = dai
        else:
            lax.fori_loop(0, n_grp, group, (zero, zero))

    big = pl.BlockSpec((S, 2 * CB), lambda c: (0, c))
    vec = pl.BlockSpec((1, 2 * CB), lambda c: (0, c))
    n_blk = x.shape[1] // (2 * CB)
    if rev:
        return _pcall(body, name=name, grid=(n_blk,), in_specs=[big, vec, big], out_specs=[big, vec],
                      out_shape=[jax.ShapeDtypeStruct(x.shape, F32), jax.ShapeDtypeStruct(a.shape, F32)],
                      compiler_params=_params(("parallel",)))(x, a, h)
    return _pcall(body, name=name, grid=(n_blk,), in_specs=[big, vec], out_specs=big,
                  out_shape=jax.ShapeDtypeStruct(x.shape, F32), compiler_params=_params(("parallel",)))(x, a)


def _ssm_prep1(lr_, li, ldt):
    lr = jnp.minimum(lr_, -1e-4)
    dt = jnp.exp(ldt)
    mag = jnp.exp(lr * dt)
    ar = mag * jnp.cos(li * dt)
    ai = mag * jnp.sin(li * dt)
    den = lr * lr + li * li
    gr = ((ar - 1.0) * lr + ai * li) / den
    gi = (ai * lr - (ar - 1.0) * li) / den
    return ar, ai, gr, gi


def _ssm_prep2(gr, gi, br, bi):
    return gr * br - gi * bi, gr * bi + gi * br


def _vjp_of(fn, n_in):
    def bwd(*args):
        cts = args[n_in:]
        return jax.vjp(fn, *args[:n_in])[1](cts[0] if len(cts) == 1 else tuple(cts))
    return bwd


def _to_blocked(r, i):
    lead = r.shape[:-1]
    t = jnp.stack([r.reshape(lead + (N_SSM_CH // SCAN_CB, SCAN_CB)), i.reshape(lead + (N_SSM_CH // SCAN_CB, SCAN_CB))],
                  axis=-2)
    return t.reshape(lead + (2 * N_SSM_CH,))


def _from_blocked(m):
    lead = m.shape[:-1]
    t = m.reshape(lead + (N_SSM_CH // SCAN_CB, 2, SCAN_CB))
    return t[..., 0, :].reshape(lead + (N_SSM_CH,)), t[..., 1, :].reshape(lead + (N_SSM_CH,))


_GPB = SSM_GROUPS // SSM_BD
_GPS = SCAN_CB // SSM_STATE


def _bd_eye():
    return jnp.eye(_GPB, dtype=F32).reshape(_GPB, _GPB // _GPS, _GPS)


def _blockdiag(r, i):
    v = jnp.stack([r, i]).reshape(2, SSM_BD, _GPB, SSM_GROUP, SSM_STATE)
    return jnp.einsum("qjgcp,gsh->jgcsqhp", v, _bd_eye()).reshape(SSM_BD, SSM_BD_IN, SSM_BD_ST)


def _blockdiag_inv(m):
    d = m.reshape(SSM_BD, _GPB, SSM_GROUP, _GPB // _GPS, 2, _GPS, SSM_STATE)
    v = jnp.einsum("jgcsqhp,gsh->qjgcp", d, _bd_eye()).reshape(2, SSM_GROUPS, SSM_GROUP, SSM_STATE)
    return v[0], v[1]


def _shift_rows(x, n):
    S = x.shape[0]
    row = lax.broadcasted_iota(jnp.int32, x.shape, 0)
    if n > 0:
        return jnp.where(row >= n, pltpu.roll(x, n, 0), 0.0)
    return jnp.where(row < S + n, pltpu.roll(x, S + n, 0), 0.0)


def _bf(x):
    return x.astype(BF16).astype(F32)


def _conv_pre(a, w, b):
    ab, wb = _bf(a), _bf(w)
    return wb[2:3] * ab + wb[1:2] * _shift_rows(ab, 1) + wb[0:1] * _shift_rows(ab, 2) + b


def _ffn_mid(up, conv_w, conv_b, dh=None, *, name):
    S = up.shape[0]
    tn = 2 * LANE
    nb = D_FF_P // tn
    rev = dh is not None

    def body(*refs):
        if not rev:
            a_ref, g_ref, w_ref, b_ref, o_ref = refs
            o_ref[...] = (jax.nn.gelu(_conv_pre(a_ref[...], w_ref[...], b_ref[...])) * g_ref[...]).astype(BF16)
            return
        a_ref, g_ref, w_ref, b_ref, dh_ref, dup_a_ref, dup_g_ref, dw_ref, db_ref = refs
        a, w, dh_ = a_ref[...], w_ref[...], dh_ref[...]
        pre = _conv_pre(a, w, b_ref[...])
        gl, gelu_vjp = jax.vjp(jax.nn.gelu, pre)
        dup_g_ref[...] = (dh_ * gl).astype(BF16)
        (dpre,) = gelu_vjp(dh_ * g_ref[...])
        db_ref[...] = jnp.sum(dpre, axis=0, keepdims=True)
        dpb, ab, wb = _bf(dpre), _bf(a), _bf(w)
        dup_a_ref[...] = (wb[2:3] * dpb + wb[1:2] * _shift_rows(dpb, -1) + wb[0:1] * _shift_rows(dpb, -2)).astype(BF16)
        dw_ref[2:3, :] = jnp.sum(dpb * ab, axis=0, keepdims=True)
        dw_ref[1:2, :] = jnp.sum(dpb * _shift_rows(ab, 1), axis=0, keepdims=True)
        dw_ref[0:1, :] = jnp.sum(dpb * _shift_rows(ab, 2), axis=0, keepdims=True)

    a_spec = pl.BlockSpec((S, tn), lambda j: (0, j))
    g_spec = pl.BlockSpec((S, tn), lambda j: (0, j + nb))
    w_spec = pl.BlockSpec((3, tn), lambda j: (0, j))
    b_spec = pl.BlockSpec((1, tn), lambda j: (0, j))
    if not rev:
        return _pcall(body, name=name, grid=(nb,), in_specs=[a_spec, g_spec, w_spec, b_spec], out_specs=a_spec,
                      out_shape=jax.ShapeDtypeStruct((S, D_FF_P), BF16), compiler_params=_params(("parallel",)))(
                          up, up, conv_w, conv_b)
    return _pcall(body, name=name, grid=(nb,), in_specs=[a_spec, g_spec, w_spec, b_spec, a_spec],
                  out_specs=[a_spec, a_spec, w_spec, b_spec],
                  out_shape=[jax.ShapeDtypeStruct((S, D_FF_P), BF16), jax.ShapeDtypeStruct((S, D_FF_P), BF16),
                             jax.ShapeDtypeStruct((3, D_FF_P), F32), jax.ShapeDtypeStruct((1, D_FF_P), F32)],
                  compiler_params=_params(("parallel",)))(up, up, conv_w, conv_b, dh)


def _ff_pad(t):
    lead = t.shape[:-1]
    t = t.reshape(lead + (N_DEV, FF_HALF))
    return jnp.pad(t, [(0, 0)] * len(lead) + [(0, 0), (0, FF_HALF_P - FF_HALF)]).reshape(lead + (D_FF_P,))


def _ff_unpad(t):
    lead = t.shape[:-1]
    return t.reshape(lead + (N_DEV, FF_HALF_P))[..., :FF_HALF].reshape(lead + (D_FF,))


def _ln_fwd(x, h, g, b):
    r = DN_ALPHA * x + h
    mu = jnp.mean(r, axis=-1, keepdims=True)
    var = jnp.mean(jnp.square(r - mu), axis=-1, keepdims=True)
    return r, (r - mu) * lax.rsqrt(var + LN_EPS) * g + b


def _ln_bwd(r, dy, g):
    mu = jnp.mean(r, axis=-1, keepdims=True)
    var = jnp.mean(jnp.square(r - mu), axis=-1, keepdims=True)
    xhat = (r - mu) * lax.rsqrt(var + LN_EPS)
    dxh = dy * g
    dr = lax.rsqrt(var + LN_EPS) * (dxh - jnp.mean(dxh, axis=-1, keepdims=True)
                                    - xhat * jnp.mean(dxh * xhat, axis=-1, keepdims=True))
    return dr, jnp.sum(dy * xhat, axis=0, keepdims=True), jnp.sum(dy, axis=0, keepdims=True)


def _merge(gf, gs, gc, ya, yb2, yc):
    yb = yb2[:, :D_MODEL] * jax.nn.sigmoid(yb2[:, D_MODEL:])
    return jax.nn.sigmoid(gf) * ya + jax.nn.sigmoid(gs) * yb + jax.nn.sigmoid(gc) * yc


def _s5_tail(hc, su, d):
    return jax.nn.gelu(hc + d * su)


def _s5_tail_bwd(hc, su, dgel, d):
    _, vjp = jax.vjp(jax.nn.gelu, hc + d * su)
    (dy,) = vjp(dgel)
    return dy, d * dy, jnp.sum(dy * su, axis=0, keepdims=True)


def _loss_rows(y, tgt):
    err = y - tgt
    return err * (1.0 / D_MODEL), jnp.sum(0.5 * jnp.square(err), axis=0, keepdims=True) * (1.0 / D_MODEL)


def _peer(k):
    x, y, c = lax.axis_index("x"), lax.axis_index("y"), lax.axis_index("c")
    return (x ^ ((k >> 2) & 1), y ^ ((k >> 1) & 1), c ^ (k & 1))


def _my_slot():
    return 4 * lax.axis_index("x") + 2 * lax.axis_index("y") + lax.axis_index("c")


def _peer_slot(k):
    px, py, pc = _peer(k)
    return 4 * px + 2 * py + pc


N_CHIP = N_DEV // 2
OTHER_CHIPS = (2, 4, 6)


def _chip_of(dev):
    return 2 * dev[0] + dev[1]


def _remote(src, dst, send, recv, dev):
    return pltpu.make_async_remote_copy(src_ref=src, dst_ref=dst, send_sem=send, recv_sem=recv, device_id=dev,
                                        device_id_type=pl.DeviceIdType.MESH)


def _all_gather(shards, *, name, host=None, then=None):
    return _exchange(shards, *_all_gather_parts(shards), name=name, host=host, then=then)


def _exchange(ins, out_shapes, sem_shapes, start, finish, *, name, host=None, then=None):
    if host is not None:
        rider = _Rider(host, ins, out_shapes, sem_shapes, start, finish, then)
        _RIDERS.append(rider)
        return rider
    n = len(ins)

    def body(*refs):
        start(refs[:n], refs[n:2 * n], refs[2 * n:])
        finish(refs[:n], refs[n:2 * n], refs[2 * n:])

    hbm = pl.BlockSpec(memory_space=pl.ANY)
    return _pcall(body, name=name, in_specs=[hbm] * n, out_specs=[hbm] * n, out_shape=list(out_shapes),
                  scratch_shapes=list(sem_shapes))(*ins)


def _all_gather_parts(shards):
    n = len(shards)

    def first_copies(ins, outs, sems):
        send, recv, _ = sems
        return [_remote(ins[t], outs[t].at[_my_slot()], send.at[t, k - 1], recv.at[t, k - 1], _peer(k))
                for k in (1,) + OTHER_CHIPS for t in range(n)]

    def local_copies(ins, outs, sems):
        return [pltpu.make_async_copy(ins[t], outs[t].at[_my_slot()], sems[2].at[t]) for t in range(n)]

    def start(ins, outs, sems):
        for cp in local_copies(ins, outs, sems) + first_copies(ins, outs, sems):
            cp.start()

    def finish(ins, outs, sems):
        send, recv, _ = sems
        sibling = _peer(1)
        passed = []
        for k in OTHER_CHIPS:
            for t in range(n):
                slot = outs[t].at[_peer_slot(k)]
                _remote(ins[t], slot, send.at[t, k - 1], recv.at[t, k - 1], _peer(k)).wait_recv()
                cp = _remote(slot, slot, send.at[t, k], recv.at[t, k], sibling)
                cp.start()
                passed.append(cp)
        for t in range(n):
            _remote(ins[t], outs[t].at[_peer_slot(1)], send.at[t, 0], recv.at[t, 0], sibling).wait_recv()
            for k in OTHER_CHIPS:
                _remote(ins[t], outs[t].at[_peer_slot(k + 1)], send.at[t, k], recv.at[t, k], sibling).wait_recv()
        for cp in first_copies(ins, outs, sems) + passed:
            cp.wait_send()
        for lc in local_copies(ins, outs, sems):
            lc.wait()

    out_shapes = [jax.ShapeDtypeStruct((N_DEV,) + s.shape, s.dtype) for s in shards]
    sem_shapes = [pltpu.SemaphoreType.DMA((n, N_DEV - 1)), pltpu.SemaphoreType.DMA((n, N_DEV - 1)),
                  pltpu.SemaphoreType.DMA((n,))]
    return out_shapes, sem_shapes, start, finish


def _sibling_swap(grads, *, name, host=None, then=None):
    n = len(grads)

    def copies(ins, outs, sems):
        c = lax.axis_index("c")
        return [_remote(ins[t].at[:, 1 - c], outs[t], sems[0].at[t], sems[1].at[t], _peer(1)) for t in range(n)]

    def start(ins, outs, sems):
        for cp in copies(ins, outs, sems):
            cp.start()

    def finish(ins, outs, sems):
        for cp in copies(ins, outs, sems):
            cp.wait()

    out_shapes = [jax.ShapeDtypeStruct((N_CHIP,) + g.shape[2:], g.dtype) for g in grads]
    sem_shapes = [pltpu.SemaphoreType.DMA((n,)), pltpu.SemaphoreType.DMA((n,))]
    return _exchange(grads, out_shapes, sem_shapes, start, finish, name=name, host=host, then=then)


def _pair_add(g, p, out_dtype, *, name):
    _, _, R, C = g.shape
    tr = _pick(R, (512, 256, 128, 64, 32, 16, 8))

    def body(c_ref, g_ref, p_ref, o_ref):
        o_ref[...] = (g_ref[...] + p_ref[...]).astype(out_dtype)

    grid_spec = pltpu.PrefetchScalarGridSpec(
        num_scalar_prefetch=1, grid=(N_CHIP, R // tr),
        in_specs=[pl.BlockSpec((None, None, tr, C), lambda j, i, c_ref: (j, c_ref[0], i, 0)),
                  pl.BlockSpec((None, tr, C), lambda j, i, c_ref: (j, i, 0))],
        out_specs=pl.BlockSpec((None, tr, C), lambda j, i, c_ref: (j, i, 0)))
    core = lax.axis_index("c").astype(jnp.int32).reshape(1)
    return _pcall(body, name=name, grid_spec=grid_spec, out_shape=jax.ShapeDtypeStruct(p.shape, out_dtype),
                  compiler_params=_params(("parallel", "parallel")))(core, g, p)


def _chip_exchange(sums, *, name, host=None):
    n = len(sums)

    def copies(ins, outs, sems, dst_is_mine):
        send, recv, _ = sems
        mine = 2 * lax.axis_index("x") + lax.axis_index("y")
        out = []
        for k in OTHER_CHIPS:
            theirs = _chip_of(_peer(k))
            for t in range(n):
                out.append(_remote(ins[t].at[theirs], outs[t].at[mine if dst_is_mine else theirs],
                                   send.at[t, k // 2 - 1], recv.at[t, k // 2 - 1], _peer(k)))
        return out

    def local_copies(ins, outs, sems):
        mine = 2 * lax.axis_index("x") + lax.axis_index("y")
        return [pltpu.make_async_copy(ins[t].at[mine], outs[t].at[mine], sems[2].at[t]) for t in range(n)]

    def start(ins, outs, sems):
        for cp in local_copies(ins, outs, sems) + copies(ins, outs, sems, True):
            cp.start()

    def finish(ins, outs, sems):
        for cp in copies(ins, outs, sems, False) + local_copies(ins, outs, sems):
            cp.wait()

    out_shapes = [jax.ShapeDtypeStruct(s.shape, s.dtype) for s in sums]
    sem_shapes = [pltpu.SemaphoreType.DMA((n, N_CHIP - 1)), pltpu.SemaphoreType.DMA((n, N_CHIP - 1)),
                  pltpu.SemaphoreType.DMA((n,))]
    return _exchange(sums, out_shapes, sem_shapes, start, finish, name=name, host=host)


class _GradReduce:
    def __init__(self, grads, wire_dtypes, hosts=None):
        pairs = [g.reshape((N_CHIP, 2) + g.shape[1:]) for g in grads]
        self.n, self.riders, self.recv = len(grads), [], None

        def after_swap(partner):
            sums = [_pair_add(g, p, dt, name="grad_pair_add") for g, p, dt in zip(pairs, partner, wire_dtypes)]
            if hosts is None:
                self.recv = _chip_exchange(sums, name="grad_chip_exchange")
            else:
                self.riders = [(idx, _chip_exchange([sums[i] for i in idx], name="grad_chip_exchange", host=h))
                               for h, idx in hosts[1]]

        if hosts is None:
            after_swap(_sibling_swap(pairs, name="grad_sibling_swap"))
        else:
            _sibling_swap(pairs, name="grad_sibling_swap", host=hosts[0], then=after_swap)

    def result(self):
        if self.recv is None:
            self.recv = [None] * self.n
            for idx, rider in self.riders:
                assert rider.results is not None, rider.host
                for i, r in zip(idx, rider.results):
                    self.recv[i] = r
        return self.recv


def _adamw(recv, w, m, v, layer=None, into=None, *, name):
    n_slots, R, C = recv.shape
    tr = _pick(R, (256, 128, 64, 32, 16, 8))

    def body(r_ref, w_ref, m_ref, v_ref, *rest):
        g_ref, d_ref, nm_ref, nv_ref = rest[-4:]
        g = r_ref[0].astype(F32)
        for s in range(1, n_slots):
            g = g + r_ref[s].astype(F32)
        m_new = ADAM_B1 * m_ref[...] + (1.0 - ADAM_B1) * g
        v_new = ADAM_B2 * v_ref[...] + (1.0 - ADAM_B2) * jnp.square(g)
        m_hat = m_new / (1.0 - ADAM_B1 ** ADAM_STEP)
        v_hat = v_new / (1.0 - ADAM_B2 ** ADAM_STEP)
        g_ref[...] = g
        d_ref[...] = -ADAM_LR * (m_hat / (jnp.sqrt(v_hat) + ADAM_EPS) + ADAM_WD * w_ref[...])
        nm_ref[...] = m_new
        nv_ref[...] = v_new

    row = pl.BlockSpec((tr, C), lambda i: (i, 0))
    state = row if layer is None else pl.BlockSpec((None, tr, C), lambda i: (layer, i, 0))
    in_specs = [pl.BlockSpec((n_slots, tr, C), lambda i: (0, i, 0)), state, state, state]
    if into is None:
        return _pcall(body, name=name, grid=(R // tr,), in_specs=in_specs, out_specs=[row] * 4,
                      out_shape=[jax.ShapeDtypeStruct((R, C), F32)] * 4, compiler_params=_params(("parallel",)),
                      )(recv, w, m, v)
    return _pcall(body, name=name, grid=(R // tr,), in_specs=in_specs + [pl.BlockSpec(memory_space=pl.ANY)] * 4,
                  out_specs=[state] * 4, out_shape=[jax.ShapeDtypeStruct((DEPTH, R, C), F32)] * 4,
                  input_output_aliases={4 + j: j for j in range(4)}, compiler_params=_params(("parallel",)),
                  )(recv, w, m, v, *into)


def _flat_pad(parts, rows):
    flat = jnp.concatenate([p.reshape(-1) for p in parts])
    return jnp.pad(flat, (0, rows * LANES - flat.shape[0])).reshape(rows, LANES)


def _small_shard_shape(n):
    return SMALL[n][:-1] + (SMALL[n][-1] // N_DEV,)


def _unpack_small(gathered):
    out, off = {}, 0
    flat = gathered.reshape(N_DEV, -1)
    for n in SMALL:
        r, c = _small_shard_shape(n)
        out[n] = flat[:, off:off + r * c].reshape(N_DEV, r, c).transpose(1, 0, 2).reshape(r, N_DEV * c)
        off += r * c
    return out


def _pack_state(state, prefix):
    flat = jnp.concatenate([state[prefix + n].reshape(DEPTH, -1) for n in (*SMALL, *REPL)], axis=1)
    return jnp.pad(flat, ((0, 0), (0, PACK_ROWS * LANES - flat.shape[1]))).reshape(DEPTH * PACK_ROWS, LANES)


def _pack_small_grads(grads):
    cols = []
    for n in SMALL:
        r, c = _small_shard_shape(n)
        cols.append(grads[n].reshape(r, N_DEV, c).transpose(1, 0, 2).reshape(N_DEV, r * c))
    cols += [jnp.broadcast_to(grads[n].reshape(1, -1), (N_DEV, grads[n].size)) for n in REPL]
    flat = jnp.concatenate(cols, axis=1)
    return jnp.pad(flat, ((0, 0), (0, PACK_ROWS * LANES - flat.shape[1]))).reshape(N_DEV, PACK_ROWS, LANES)


def _unpack_state(packed):
    out, off = {}, 0
    flat = packed.reshape(DEPTH, -1)
    for n, shape in [(n, _small_shard_shape(n)) for n in SMALL] + list(REPL.items()):
        sz = math.prod(shape)
        out[n] = flat[:, off:off + sz].reshape((DEPTH,) + shape)
        off += sz
    return out


def _pad_b_in(b):
    parts, pos = [], 0
    for src, width, dst in Z_PIECES:
        parts += [jnp.zeros((b.shape[0], dst - pos), b.dtype), b[:, src:src + width]]
        pos = dst + width
    return jnp.concatenate(parts + [jnp.zeros((b.shape[0], Z_W - pos), b.dtype)], axis=1)


def _unpad_b_in(bp):
    return jnp.concatenate([bp[:, dst:dst + width] for _, width, dst in Z_PIECES], axis=1)


def _up_shard_pad(t):
    gap = jnp.zeros(t.shape[:-1] + (FF_HALF_P - FF_HALF,), t.dtype)
    return jnp.concatenate([t[..., :FF_HALF], gap, t[..., FF_HALF:], gap], axis=-1)


def _up_shard_unpad(t):
    return jnp.concatenate([t[..., :FF_HALF], t[..., FF_HALF_P:FF_HALF_P + FF_HALF]], axis=-1)


def _pad_shard(n, t):
    lead = [(0, 0)] * (t.ndim - 2)
    if n == "w_in":
        return jnp.pad(t, lead + [(0, 0), (0, W_IN_SLOT - W_IN_SHARD)])
    if n == "ffn_w_up":
        return _up_shard_pad(t)
    if n == "ffn_w_down":
        return jnp.pad(t, lead + [(0, FF_HALF_P - FF_HALF), (0, 0)])
    return t


def _unpad_shard(n, t):
    if n == "w_in":
        return t[..., :W_IN_SHARD]
    if n == "ffn_w_up":
        return _up_shard_unpad(t)
    if n == "ffn_w_down":
        return t[..., :FF_HALF, :]
    return t


GATHER_HOSTS = (("chunk_fwd", (0,)), ("fox_fwd", (8,)), ("s5_scan", (9,)), ("mm_down", (1, 2, 3, 4)),
                ("mm_up", (5, 6, 7, 10)))
FIRST_GATHER_HOSTS = ((None, (0,)), ("mm_in", (1, 2, 3, 4, 5, 6, 7, 10)), ("s5_scan", (8, 9)))
SECOND_GATHER_HOSTS = (("fox_fwd", (0,)), ("chunk_fwd", (8, 9)), ("mm_up", (1, 2, 3, 4, 5, 6, 7, 10)))
REDUCE_HOSTS = ("mm_up_dw", (("fox_bwd", (0,)), ("s5_scan_bwd", (1, 2, 3, 4, 5, 6, 7, 10)), ("chunk_bwd", (8,)),
                             ("mm_in_dx", (9,))))


def _layer_shards(state, layer):
    shards = [_pad_shard(n, state[n][layer].astype(BF16)) for n in BIG]
    return shards + [_flat_pad([state[n][layer] for n in SMALL], SMALL_ROWS)]


class _Weights:
    def __init__(self, sources):
        self._sources, self._cache = sources, {}

    def _operand(self, i):
        src = self._sources[i]
        if isinstance(src, tuple):
            assert src[0].results is not None, src[0].host
            return src[0].results[src[1]]
        return src

    def __getitem__(self, n):
        if n not in self._cache:
            if n in SMALL:
                self._cache.update(_unpack_small(self._operand(len(BIG))))
            elif n == "w_in_p":
                self._cache[n] = _colmap(self._operand(0), inverse=False, name="w_in_colmap")
            else:
                w = self._operand(list(BIG).index(n))
                self._cache[n] = w.reshape(-1, D_MODEL) if BIG[n][1] == 0 else w
        return self._cache[n]


def _gather_riding(shards, hosts):
    sources = [None] * len(shards)
    for host, idx in hosts:
        group = [shards[i] for i in idx]
        if host is None:
            for i, g in zip(idx, _all_gather(group, name="all_gather_weights")):
                sources[i] = g
        else:
            rider = _all_gather(group, name="all_gather_weights", host=host)
            for pos, i in enumerate(idx):
                sources[i] = (rider, pos)
    return sources


def _ssm_params(p):
    prep1_in = (p["ssm_lambda_re"], p["ssm_lambda_im"], p["ssm_log_dt"][:, None])
    ar, ai, gr, gi = _whole(_ssm_prep1, *prep1_in, name="ssm_prep1")
    to_cn = lambda b: b.transpose(2, 0, 1).reshape(SSM_GROUP, N_SSM_CH)
    prep2_in = (gr.reshape(1, N_SSM_CH), gi.reshape(1, N_SSM_CH), to_cn(p["ssm_b_re"]), to_cn(p["ssm_b_im"]))
    bbr, bbi = _whole(_ssm_prep2, *prep2_in, name="ssm_prep2")
    to_gcp = lambda t: t.reshape(SSM_GROUP, SSM_GROUPS, SSM_STATE).transpose(1, 0, 2)
    bb = _blockdiag(to_gcp(bbr), to_gcp(bbi))
    cct = _blockdiag(p["ssm_c_re"], -p["ssm_c_im"])
    a_vec = _to_blocked(ar.reshape(1, N_SSM_CH), ai.reshape(1, N_SSM_CH))
    return dict(bb=bb, cct=cct, a_vec=a_vec, prep1_in=prep1_in, prep2_in=prep2_in)


def _layer_fwd(x, mem, p, W):
    sp = _ssm_params(p)
    z = _mm(x, W["w_in_p"], bias=p["b_in_p"], name="mm_in")
    f_t = z[:, Z_FF:Z_FF + FOX_HEADS].T
    cum = _cum_heads(f_t, name="fox_cum")
    ya_pre, lse_a = _attn_fwd("fox", (z, Z_FQ), (z, Z_FK), (z, Z_FV), (cum[:, :, None], cum[:, None, :]),
                              name="fox_fwd")
    ya = _mm(ya_pre, W["w_fox_o"], b_slots=True, name="mm_fox_o")
    x_ri = _mm_bd("in", z, sp["bb"], a_off=Z_SU, name="mm_s5_in")
    h_ri = _scan(x_ri, sp["a_vec"], name="s5_scan")
    hc = _mm_bd("out", h_ri, sp["cct"], name="mm_s5_out")
    d_row = p["ssm_d"][None, :]
    (gel,) = _rowwise(lambda a, b, c: (_s5_tail(a, b, c),), [hc, Win(z, 512, Z_SU // 512)], [d_row], name="s5_tail")
    yb2 = _mm(gel, W["w_ssm_glu"], b_slots=True, name="mm_glu")
    bias = _relbias_expand(W["ca_rel_bias"]).transpose(1, 0, 2)
    kv_band = jnp.pad(z[:, Z_CK:Z_CK + 2 * CA_WIDTH].astype(BF16), ((CA_PAD, 0), (0, 0)))
    yc_pre, lse_c = _attn_fwd("chunk", (z, Z_CQ), (kv_band, 0), (kv_band, CA_WIDTH), (bias,), name="chunk_fwd")
    yc = _mm(yc_pre, W["w_ca_o"], b_slots=True, name="mm_ca_o")
    gates = [Win(z, 1024, Z_GF // 1024), Win(z, 1024, Z_GS // 1024), Win(z, 1024, Z_GC // 1024)]
    (merged,) = _rowwise(lambda *a: (_merge(*a),), gates + [ya, yb2, yc], name="merge")
    h1 = _mm(merged, W["w_o"], name="mm_o")
    ln_g, ln_b = W["ln_g"], W["ln_b"]
    r1, x1 = _rowwise(_ln_fwd, [x, h1], [ln_g[0:1], ln_b[0:1]], tr=512, name="ln_fwd")
    q = _mm(x1, W["xa_wq"], name="mm_xq")
    kv = _mm(mem, W["xa_wkv"], b_slots=True, name="mm_xkv")
    o, lse_x = _attn_fwd("xa", (q, 0), (kv, 0), (kv, D_MODEL), name="xa_fwd")
    h2 = _mm(o, W["xa_wo"], name="mm_xo")
    r2, x2 = _rowwise(_ln_fwd, [x1, h2], [ln_g[1:2], ln_b[1:2]], tr=512, name="ln_fwd")
    up = _mm(x2, W["ffn_w_up"], b_slots=True, name="mm_up")
    hmid = _ffn_mid(up, _ff_pad(W["ffn_conv_w"]), _ff_pad(p["ffn_conv_b"][None, :]), name="ffn_mid")
    h3 = _mm(hmid, W["ffn_w_down"], name="mm_down")
    r3, x3 = _rowwise(_ln_fwd, [x2, h3], [ln_g[2:3], ln_b[2:3]], tr=512, name="ln_fwd")
    res = dict(x=x, z=z, cum=cum, lse_a=lse_a, ya_pre=ya_pre, ya=ya, h_ri=h_ri, hc=hc, gel=gel, yb2=yb2, lse_c=lse_c,
               yc_pre=yc_pre, yc=yc, merged=merged, r1=r1, x1=x1, q=q, kv=kv, o=o, lse_x=lse_x, r2=r2, x2=x2, up=up,
               hmid=hmid, r3=r3, bias=bias, sp=sp, kv_band=kv_band, W=W)
    return x3, res


def _layer_bwd(dx3, mem, p, res):
    W = res["W"]
    x, z = res["x"], res["z"]
    sp = res["sp"]
    ln_g = W["ln_g"]
    big, small = {}, {}
    slots = lambda t: t.reshape(N_DEV, -1, D_MODEL)
    dr3, dg2, db2 = _rowwise(_ln_bwd, [res["r3"], dx3], [ln_g[2:3]], n_red=2, tr=512, name="ln_bwd")
    dhmid = _mm(dr3, W["ffn_w_down"], tb=True, name="mm_down_dx")
    big["ffn_w_down"] = slots(_mm(res["hmid"], dr3, ta=True, name="mm_down_dw"))
    conv_w_p, conv_b_p = _ff_pad(W["ffn_conv_w"]), _ff_pad(p["ffn_conv_b"][None, :])
    dup_a, dup_g, dcw, dcb = _ffn_mid(res["up"], conv_w_p, conv_b_p, dhmid, name="ffn_mid_bwd")
    dup = jnp.concatenate([dup_a, dup_g], axis=1)
    small["ffn_conv_w"], small["ffn_conv_b"] = _ff_unpad(dcw), _ff_unpad(dcb)[0]
    dx2 = _mm(dup, W["ffn_w_up"], tb=True, b_slots=True, add=(dr3, DN_ALPHA), name="mm_up_dx")
    big["ffn_w_up"] = _mm(res["x2"], dup, ta=True, out_slots=2 * FF_HALF_P, name="mm_up_dw")
    dr2, dg1, db1 = _rowwise(_ln_bwd, [res["r2"], dx2], [ln_g[1:2]], n_red=2, tr=512, name="ln_bwd")
    do = _mm(dr2, W["xa_wo"], tb=True, name="mm_xo_dx")
    big["xa_wo"] = slots(_mm(res["o"], dr2, ta=True, name="mm_xo_dw"))
    kv = res["kv"]
    dq, dk, dv = _attn_bwd("xa", (res["q"], 0), (kv, 0), (kv, D_MODEL), res["lse_x"], (do, 0), name="xa_bwd")
    dkv = jnp.concatenate([dk, dv], axis=1)
    dx1 = _mm(dq, W["xa_wq"], tb=True, add=(dr2, DN_ALPHA), name="mm_xq_dx")
    big["xa_wq"] = slots(_mm(res["x1"], dq, ta=True, name="mm_xq_dw"))
    big["xa_wkv"] = _mm(mem, dkv, ta=True, out_slots=256, name="mm_xkv_dw")
    dr1, dg0, db0 = _rowwise(_ln_bwd, [res["r1"], dx1], [ln_g[0:1]], n_red=2, tr=512, name="ln_bwd")
    small["ln_g"] = jnp.concatenate([dg0, dg1, dg2], axis=0)
    small["ln_b"] = jnp.concatenate([db0, db1, db2], axis=0)
    dmerged = _mm(dr1, W["w_o"], tb=True, name="mm_o_dx")
    big["w_o"] = slots(_mm(res["merged"], dr1, ta=True, name="mm_o_dw"))
    gates = [Win(z, 1024, Z_GF // 1024), Win(z, 1024, Z_GS // 1024), Win(z, 1024, Z_GC // 1024)]
    dgf, dgs, dgc, dya, dyb2, dyc = _rowwise(_vjp_of(_merge, 6), gates + [res["ya"], res["yb2"], res["yc"], dmerged],
                                             name="merge_bwd")
    dya_pre = _mm(dya, W["w_fox_o"], tb=True, b_slots=True, name="mm_fox_o_dx")
    big["w_fox_o"] = _mm(res["ya_pre"], dya, ta=True, out_slots=128, name="mm_fox_o_dw")
    cum = res["cum"]
    dfq, dfk, dfv, dcq, dck = _attn_bwd("fox", (z, Z_FQ), (z, Z_FK), (z, Z_FV), res["lse_a"], (dya_pre, 0),
                                        (cum[:, :, None], cum[:, None, :]), name="fox_bwd")
    f_t = z[:, Z_FF:Z_FF + FOX_HEADS].T
    dff = _cum_heads(f_t, dcq[:, :, 0] + dck[:, 0, :], name="fox_cum_bwd")
    dgel = _mm(dyb2, W["w_ssm_glu"], tb=True, b_slots=True, name="mm_glu_dx")
    big["w_ssm_glu"] = _mm(res["gel"], dyb2, ta=True, out_slots=256, name="mm_glu_dw")
    d_row = p["ssm_d"][None, :]
    su_win = Win(z, 512, Z_SU // 512)
    dy, dsu1, dd = _rowwise(_s5_tail_bwd, [res["hc"], su_win, dgel], [d_row], n_red=1, name="s5_tail_bwd")
    small["ssm_d"] = dd[0]
    dh_ri = _mm_bd("in", dy, sp["cct"], name="mm_s5_out_dx")
    dcct = _mm_bd("dw", dy, res["h_ri"], name="mm_s5_out_dw")
    dx_ri, da_vec = _scan(dh_ri, sp["a_vec"], res["h_ri"], name="s5_scan_bwd")
    dsu = _mm_bd("out", dx_ri, sp["bb"], add=(dsu1, 1.0), name="mm_s5_in_dx")
    dbb = _mm_bd("dw", z, dx_ri, a_off=Z_SU, name="mm_s5_in_dw")
    dcr, dci = _blockdiag_inv(dcct)
    small["ssm_c_re"], small["ssm_c_im"] = dcr, -dci
    dbbr, dbbi = _blockdiag_inv(dbb)
    to_cn = lambda t: t.transpose(1, 0, 2).reshape(SSM_GROUP, N_SSM_CH)
    dgr, dgi, dbr, dbi = _whole(_vjp_of(_ssm_prep2, 4), *sp["prep2_in"], to_cn(dbbr), to_cn(dbbi), name="ssm_prep2_bwd")
    from_cn = lambda t: t.reshape(SSM_GROUP, SSM_GROUPS, SSM_STATE).transpose(1, 2, 0)
    small["ssm_b_re"], small["ssm_b_im"] = from_cn(dbr), from_cn(dbi)
    dar, dai = _from_blocked(da_vec)
    sq = lambda t: t.reshape(SSM_GROUPS, SSM_STATE)
    dlr, dli, dldt = _whole(_vjp_of(_ssm_prep1, 3), *sp["prep1_in"], sq(dar), sq(dai), sq(dgr), sq(dgi),
                            name="ssm_prep1_bwd")
    small["ssm_lambda_re"], small["ssm_lambda_im"], small["ssm_log_dt"] = dlr, dli, dldt[:, 0]
    dyc_pre = _mm(dyc, W["w_ca_o"], tb=True, b_slots=True, name="mm_ca_o_dx")
    big["w_ca_o"] = _mm(res["yc_pre"], dyc, ta=True, out_slots=128, name="mm_ca_o_dw")
    bias = res["bias"]
    kv_band = res["kv_band"]
    dcq_, dck_band, dcv_band, dbias = _attn_bwd("chunk", (z, Z_CQ), (kv_band, 0), (kv_band, CA_WIDTH), res["lse_c"],
                                                (dyc_pre, 0), (bias,), name="chunk_bwd")
    small["ca_rel_bias"] = _relbias_reduce(dbias.transpose(1, 0, 2))
    dff_p = jnp.pad(dff.T, ((0, 0), (0, 512 - FOX_HEADS)))
    dz = jnp.concatenate([dfq, dfk, dfv, dff_p, dsu, dcq_, dck_band[CA_PAD:], dcv_band[CA_PAD:], dgf, dgs, dgc],
                         axis=1)
    dx = _mm(dz, W["w_in_p"], tb=True, add=(dr1, DN_ALPHA), name="mm_in_dx")
    big["w_in"] = _colmap(_mm(x, dz, ta=True, name="mm_in_dw"), inverse=True, name="w_in_colmap_inv")
    (db_in_p,) = _rowwise(lambda t: (jnp.sum(t, axis=0, keepdims=True),), [dz], n_red=1, name="colsum")
    small["b_in"] = _unpad_b_in(db_in_p)[0]
    return dx, big, small


def kernel(x, mem, w_in, b_in, ssm_lambda_re, ssm_lambda_im, ssm_log_dt, ssm_b_re, ssm_b_im, ssm_c_re, ssm_c_im, ssm_d, ca_rel_bias, w_fox_o, w_ssm_glu, w_ca_o, w_o, xa_wq, xa_wkv, xa_wo, ffn_w_up, ffn_conv_w, ffn_conv_b, ffn_w_down, ln_g, ln_b, loss_target, m_w_in, m_b_in, m_ssm_lambda_re, m_ssm_lambda_im, m_ssm_log_dt, m_ssm_b_re, m_ssm_b_im, m_ssm_c_re, m_ssm_c_im, m_ssm_d, m_ca_rel_bias, m_w_fox_o, m_w_ssm_glu, m_w_ca_o, m_w_o, m_xa_wq, m_xa_wkv, m_xa_wo, m_ffn_w_up, m_ffn_conv_w, m_ffn_conv_b, m_ffn_w_down, m_ln_g, m_ln_b, v_w_in, v_b_in, v_ssm_lambda_re, v_ssm_lambda_im, v_ssm_log_dt, v_ssm_b_re, v_ssm_b_im, v_ssm_c_re, v_ssm_c_im, v_ssm_d, v_ca_rel_bias, v_w_fox_o, v_w_ssm_glu, v_w_ca_o, v_w_o, v_xa_wq, v_xa_wkv, v_xa_wo, v_ffn_w_up, v_ffn_conv_w, v_ffn_conv_b, v_ffn_w_down, v_ln_g, v_ln_b):
    given = dict(locals())
    state = {pre + n: given[pre + n] for n in WEIGHTS for pre in ("", "m_", "v_")}
    mem0 = mem[0]
    b_in_p = _pad_b_in(b_in)
    layer_params = [{**{n: state[n][l] for n in REPL}, "b_in_p": b_in_p[l:l + 1]} for l in range(DEPTH)]

    _RIDERS.clear()
    h, residuals = x[0], []
    sources = _gather_riding(_layer_shards(state, 0), FIRST_GATHER_HOSTS)
    for l in range(DEPTH):
        hosts = SECOND_GATHER_HOSTS if l == 0 else GATHER_HOSTS
        following = _gather_riding(_layer_shards(state, l + 1), hosts) if l + 1 < DEPTH else None
        h, res = _layer_fwd(h, mem0, layer_params[l], _Weights(sources))
        residuals.append(res)
        sources = following
    dh, loss_cols = _rowwise(_loss_rows, [h, loss_target[0]], n_red=1, name="loss")
    loss = lax.psum(jnp.sum(loss_cols), ("x", "y", "c"))

    outs = [None] * DEPTH
    big_out = {n: None for n in BIG}
    padded = {pre + n: _pad_shard(n, state[pre + n]) for n in BIG for pre in ("", "m_", "v_")}
    wire = [BF16] * len(BIG) + [F32]
    pending = None
    for l in reversed(range(-1, DEPTH)):
        if l >= 0:
            dh, big, small = _layer_bwd(dh, mem0, layer_params[l], residuals[l])
            reduce = _GradReduce([big[n] for n in BIG] + [_pack_small_grads(small)], wire,
                                 hosts=REDUCE_HOSTS if l > 0 else None)
        done, pending = pending, (l, reduce) if l >= 0 else None
        if done is None:
            continue
        l_done, reduce_done = done
        *recv_big, recv_small = reduce_done.result()
        for n, recv in zip(BIG, recv_big):
            if big_out[n] is None:
                big_out[n] = [lax.empty((DEPTH,) + recv.shape[1:], F32) for _ in range(4)]
            big_out[n] = _adamw(recv, *[padded[pre + n] for pre in ("", "m_", "v_")], l_done, big_out[n],
                                name="adamw_" + n)
        outs[l_done] = recv_small

    assert not _RIDERS, [r.host for r in _RIDERS]
    packed = _adamw(jnp.concatenate(outs, axis=1), *[_pack_state(state, pre) for pre in ("", "m_", "v_")],
                    name="adamw_small")
    small_out = [_unpack_state(t) for t in packed]
    result = lambda n, j: _unpad_shard(n, big_out[n][j]) if n in BIG else small_out[j][n]
    return (loss, dh[None], *[result(n, j) for j in range(4) for n in WEIGHTS])
```
